```python
import math
import jax, jax.numpy as jnp
from jax import lax
import numpy as np

D_MODEL = 1024
BATCH = 8
SEQ = 8192
DEPTH = 2

N_A_LAYERS = DEPTH // 2
N_B_LAYERS = DEPTH - N_A_LAYERS

GDN_HEADS = 8
GDN_HEAD_DIM = 128
GDN_WIDTH = GDN_HEADS * GDN_HEAD_DIM
CONV_WIDTH = 4
CHUNK = 64
GDN_IN_COLS = 4 * GDN_WIDTH + 2 * GDN_HEADS

SB_HEADS = 8
SB_HEAD_DIM = 128
SB_WIDTH = SB_HEADS * SB_HEAD_DIM
Q_BLOCK = 128

D_FF = 4 * D_MODEL

EPS = 1e-6

kernel_name = "yoco_gdn_stickbreaking_hybrid"


def rms_norm(x, gain):
    xf = x.astype(jnp.float32)
    y = xf * lax.rsqrt(jnp.mean(xf * xf, axis=-1, keepdims=True) + EPS)
    return (y * gain.astype(jnp.float32)).astype(x.dtype)


def l2_norm(x):
    xf = x.astype(jnp.float32)
    return xf * lax.rsqrt(jnp.sum(xf * xf, axis=-1, keepdims=True) + EPS)


def causal_dwconv(x, w):
    k_w = w.shape[0]
    t_len = x.shape[1]
    xp = jnp.pad(x, ((0, 0), (k_w - 1, 0), (0, 0)))
    return sum(xp[:, i:i + t_len, :] * w[i] for i in range(k_w))


def gated_delta_rule_chunked(q, k, v, beta, g):
    b, t_len, h, dk = q.shape
    dv = v.shape[-1]
    n = t_len // CHUNK

    def chunks(t):
        t = jnp.moveaxis(t, 2, 1)
        return t.reshape((b, h, n, CHUNK) + t.shape[3:])

    q, k, v, beta, g = (chunks(t) for t in (q, k, v, beta, g))
    gc = jnp.cumsum(g, axis=-1)
    idx = jnp.arange(CHUNK)
    incl = idx[:, None] >= idx[None, :]
    strict = idx[:, None] > idx[None, :]
    diff = gc[..., :, None] - gc[..., None, :]
    decay = jnp.where(incl, jnp.exp(jnp.where(incl, diff, 0.0)), 0.0)

    kb = k * beta[..., None]
    lower = jnp.where(strict, jnp.einsum('bhnid,bhnjd->bhnij', kb, k) * decay, 0.0)
    eye = jnp.eye(CHUNK, dtype=jnp.float32)
    t_mat = lax.linalg.triangular_solve(eye + lower, jnp.broadcast_to(eye, lower.shape),
                                        left_side=True, lower=True)
    w = t_mat @ (kb * jnp.exp(gc)[..., None])
    u = t_mat @ (v * beta[..., None])
    attn = jnp.einsum('bhnid,bhnjd->bhnij', q, k) * decay
    qg = q * jnp.exp(gc)[..., None]
    kg = k * jnp.exp(gc[..., -1:] - gc)[..., None]
    g_last = jnp.exp(gc[..., -1])

    xs = tuple(jnp.moveaxis(t, 2, 0) for t in (qg, kg, w, u, attn, g_last))

    def step(state, inp):
        qg_c, kg_c, w_c, u_c, attn_c, gl_c = inp
        v_new = u_c - w_c @ state
        o_c = qg_c @ state + attn_c @ v_new
        state = state * gl_c[..., None, None] + jnp.einsum('bhck,bhcv->bhkv', kg_c, v_new)
        return state, o_c

    s0 = jnp.zeros((b, h, dk, dv), jnp.float32)
    _, o = lax.scan(step, s0, xs)
    o = jnp.moveaxis(o, 0, 2).reshape(b, h, t_len, dv)
    return jnp.moveaxis(o, 1, 2)


def gated_deltanet(h, w_in, conv_w, a_log, dt_bias, out_gain, w_out):
    b, t_len, _ = h.shape
    proj = h @ w_in
    qkv, gate, b_raw, a_raw = jnp.split(
        proj, [3 * GDN_WIDTH, 4 * GDN_WIDTH, 4 * GDN_WIDTH + GDN_HEADS], axis=-1)
    qkv = jax.nn.silu(causal_dwconv(qkv, conv_w))
    q, k, v = jnp.split(qkv, 3, axis=-1)

    def heads(t):
        return t.reshape(b, t_len, GDN_HEADS, GDN_HEAD_DIM).astype(jnp.float32)

    q = l2_norm(heads(q)) * (GDN_HEAD_DIM ** -0.5)
    k = l2_norm(heads(k))
    v = heads(v)
    beta = jax.nn.sigmoid(b_raw.astype(jnp.float32))
    g = -jnp.exp(a_log.astype(jnp.float32)) * jax.nn.softplus(
        a_raw.astype(jnp.float32) + dt_bias.astype(jnp.float32))
    o = gated_delta_rule_chunked(q, k, v, beta, g)
    o = rms_norm(o, out_gain) * jax.nn.silu(heads(gate))
    return o.reshape(b, t_len, GDN_WIDTH).astype(h.dtype) @ w_out


def stick_breaking_attention(q, k, v):
    b, h, t_len, d = q.shape
    nb = t_len // Q_BLOCK
    qb = jnp.moveaxis(q.reshape(b, h, nb, Q_BLOCK, d), 2, 0)
    key_pos = jnp.arange(t_len)
    scale = d ** -0.5

    def block(args):
        q_blk, i = args
        z = jnp.einsum('bhqd,bhkd->bhqk', q_blk, k).astype(jnp.float32) * scale
        q_pos = i * Q_BLOCK + jnp.arange(Q_BLOCK)
        before = key_pos[None, :] < q_pos[:, None]
        log_beta = jax.nn.log_sigmoid(z)
        log_1m = jnp.where(before, jax.nn.log_sigmoid(-z), 0.0)
        tail = lax.cumsum(log_1m, axis=3, reverse=True) - log_1m
        a = jnp.where(before, jnp.exp(log_beta + tail), 0.0)
        return jnp.einsum('bhqk,bhkd->bhqd', a.astype(v.dtype), v)

    o = lax.map(block, (qb, jnp.arange(nb)))
    return jnp.moveaxis(o, 0, 2).reshape(b, h, t_len, d)


def shared_kv(x, kv_gain, w_kv):
    b, t_len, _ = x.shape
    kv = rms_norm(x, kv_gain) @ w_kv
    k, v = jnp.split(kv, 2, axis=-1)
    k = k.reshape(b, t_len, SB_HEADS, SB_HEAD_DIM).transpose(0, 2, 1, 3)
    v = v.reshape(b, t_len, SB_HEADS, SB_HEAD_DIM).transpose(0, 2, 1, 3)
    return k, v


def stick_breaking_mixer(h, w_q, w_o, k_sh, v_sh):
    b, t_len, _ = h.shape
    q = (h @ w_q).reshape(b, t_len, SB_HEADS, SB_HEAD_DIM).transpose(0, 2, 1, 3)
    o = stick_breaking_attention(q, k_sh, v_sh)
    return o.transpose(0, 2, 1, 3).reshape(b, t_len, SB_WIDTH) @ w_o


def squared_relu_mlp(h, w_up, w_down):
    return jnp.square(jax.nn.relu(h @ w_up)) @ w_down


def _fwd_setup_inputs(seed: int = 0) -> dict:
    key = jax.random.key(seed)
    ks = jax.random.split(key, 20)
    f32 = jnp.float32

    def nrm(k, shape, fan_in):
        return jax.random.normal(k, shape, f32) * (fan_in ** -0.5)

    def gain(k, shape):
        return 1.0 + 0.05 * jax.random.normal(k, shape, f32)

    x = jax.random.normal(ks[0], (BATCH, SEQ, D_MODEL), f32)
    dt = jnp.exp(jax.random.uniform(ks[10], (N_A_LAYERS, GDN_HEADS), f32,
                                    minval=math.log(1e-3), maxval=math.log(1e-1)))
    dt_bias = dt + jnp.log(-jnp.expm1(-dt))
    a_log = jnp.log(jax.random.uniform(ks[11], (N_A_LAYERS, GDN_HEADS), f32,
                                       minval=1.0, maxval=16.0))
    return {
        "x": x,
        "mix_pre_gain": gain(ks[1], (DEPTH, D_MODEL)),
        "mix_post_gain": gain(ks[2], (DEPTH, D_MODEL)),
        "mlp_pre_gain": gain(ks[3], (DEPTH, D_MODEL)),
        "mlp_post_gain": gain(ks[4], (DEPTH, D_MODEL)),
        "mlp_w_up": nrm(ks[5], (DEPTH, D_MODEL, D_FF), D_MODEL),
        "mlp_w_down": nrm(ks[6], (DEPTH, D_FF, D_MODEL), D_FF),
        "gdn_w_in": nrm(ks[7], (N_A_LAYERS, D_MODEL, GDN_IN_COLS), D_MODEL),
        "gdn_conv_w": nrm(ks[8], (N_A_LAYERS, CONV_WIDTH, 3 * GDN_WIDTH), CONV_WIDTH),
        "gdn_a_log": a_log,
        "gdn_dt_bias": dt_bias,
        "gdn_out_gain": gain(ks[12], (N_A_LAYERS, GDN_HEAD_DIM)),
        "gdn_w_out": nrm(ks[13], (N_A_LAYERS, GDN_WIDTH, D_MODEL), GDN_WIDTH),
        "kv_gain": gain(ks[14], (D_MODEL,)),
        "w_kv": nrm(ks[15], (D_MODEL, 2 * SB_WIDTH), D_MODEL),
        "sb_w_q": nrm(ks[16], (N_B_LAYERS, D_MODEL, SB_WIDTH), D_MODEL),
        "sb_w_o": nrm(ks[17], (N_B_LAYERS, SB_WIDTH, D_MODEL), SB_WIDTH),
    }


def _fwd_reference(x, mix_pre_gain, mix_post_gain, mlp_pre_gain, mlp_post_gain, mlp_w_up, mlp_w_down,
              gdn_w_in, gdn_conv_w, gdn_a_log, gdn_dt_bias, gdn_out_gain, gdn_w_out,
              kv_gain, w_kv, sb_w_q, sb_w_o):
    k_sh = None
    v_sh = None
    for layer in range(DEPTH):
        h = rms_norm(x, mix_pre_gain[layer])
        if layer < N_A_LAYERS:
            a = layer
            mix = gated_deltanet(h, gdn_w_in[a], gdn_conv_w[a], gdn_a_log[a], gdn_dt_bias[a],
                                 gdn_out_gain[a], gdn_w_out[a])
        else:
            if layer == N_A_LAYERS:
                k_sh, v_sh = shared_kv(x, kv_gain, w_kv)
            bl = layer - N_A_LAYERS
            mix = stick_breaking_mixer(h, sb_w_q[bl], sb_w_o[bl], k_sh, v_sh)
        x = x + rms_norm(mix, mix_post_gain[layer])
        h = rms_norm(x, mlp_pre_gain[layer])
        x = x + rms_norm(squared_relu_mlp(h, mlp_w_up[layer], mlp_w_down[layer]), mlp_post_gain[layer])
    return x


import jax as _jax
import jax.numpy as _jnp

TWIN_FORMAT = 'train_step'
FWD_PARAMS = ['x', 'mix_pre_gain', 'mix_post_gain', 'mlp_pre_gain', 'mlp_post_gain', 'mlp_w_up', 'mlp_w_down', 'gdn_w_in', 'gdn_conv_w', 'gdn_a_log', 'gdn_dt_bias', 'gdn_out_gain', 'gdn_w_out', 'kv_gain', 'w_kv', 'sb_w_q', 'sb_w_o']
TWIN_WEIGHTS = ['mix_pre_gain', 'mix_post_gain', 'mlp_pre_gain', 'mlp_post_gain', 'mlp_w_up', 'mlp_w_down', 'gdn_w_in', 'gdn_conv_w', 'gdn_a_log', 'gdn_dt_bias', 'gdn_out_gain', 'gdn_w_out', 'kv_gain', 'w_kv', 'sb_w_q', 'sb_w_o']
TWIN_DIFF_INPUT = 'x'
TWIN_INPUTS = ['x', 'mix_pre_gain', 'mix_post_gain', 'mlp_pre_gain', 'mlp_post_gain', 'mlp_w_up', 'mlp_w_down', 'gdn_w_in', 'gdn_conv_w', 'gdn_a_log', 'gdn_dt_bias', 'gdn_out_gain', 'gdn_w_out', 'kv_gain', 'w_kv', 'sb_w_q', 'sb_w_o', 'loss_target', 'm_mix_pre_gain', 'm_mix_post_gain', 'm_mlp_pre_gain', 'm_mlp_post_gain', 'm_mlp_w_up', 'm_mlp_w_down', 'm_gdn_w_in', 'm_gdn_conv_w', 'm_gdn_a_log', 'm_gdn_dt_bias', 'm_gdn_out_gain', 'm_gdn_w_out', 'm_kv_gain', 'm_w_kv', 'm_sb_w_q', 'm_sb_w_o', 'v_mix_pre_gain', 'v_mix_post_gain', 'v_mlp_pre_gain', 'v_mlp_post_gain', 'v_mlp_w_up', 'v_mlp_w_down', 'v_gdn_w_in', 'v_gdn_conv_w', 'v_gdn_a_log', 'v_gdn_dt_bias', 'v_gdn_out_gain', 'v_gdn_w_out', 'v_kv_gain', 'v_w_kv', 'v_sb_w_q', 'v_sb_w_o']
TWIN_OUTPUTS = ['loss', 'grad_x', 'grad_mix_pre_gain', 'grad_mix_post_gain', 'grad_mlp_pre_gain', 'grad_mlp_post_gain', 'grad_mlp_w_up', 'grad_mlp_w_down', 'grad_gdn_w_in', 'grad_gdn_conv_w', 'grad_gdn_a_log', 'grad_gdn_dt_bias', 'grad_gdn_out_gain', 'grad_gdn_w_out', 'grad_kv_gain', 'grad_w_kv', 'grad_sb_w_q', 'grad_sb_w_o', 'delta_mix_pre_gain', 'delta_mix_post_gain', 'delta_mlp_pre_gain', 'delta_mlp_post_gain', 'delta_mlp_w_up', 'delta_mlp_w_down', 'delta_gdn_w_in', 'delta_gdn_conv_w', 'delta_gdn_a_log', 'delta_gdn_dt_bias', 'delta_gdn_out_gain', 'delta_gdn_w_out', 'delta_kv_gain', 'delta_w_kv', 'delta_sb_w_q', 'delta_sb_w_o', 'new_m_mix_pre_gain', 'new_m_mix_post_gain', 'new_m_mlp_pre_gain', 'new_m_mlp_post_gain', 'new_m_mlp_w_up', 'new_m_mlp_w_down', 'new_m_gdn_w_in', 'new_m_gdn_conv_w', 'new_m_gdn_a_log', 'new_m_gdn_dt_bias', 'new_m_gdn_out_gain', 'new_m_gdn_w_out', 'new_m_kv_gain', 'new_m_w_kv', 'new_m_sb_w_q', 'new_m_sb_w_o', 'new_v_mix_pre_gain', 'new_v_mix_post_gain', 'new_v_mlp_pre_gain', 'new_v_mlp_post_gain', 'new_v_mlp_w_up', 'new_v_mlp_w_down', 'new_v_gdn_w_in', 'new_v_gdn_conv_w', 'new_v_gdn_a_log', 'new_v_gdn_dt_bias', 'new_v_gdn_out_gain', 'new_v_gdn_w_out', 'new_v_kv_gain', 'new_v_w_kv', 'new_v_sb_w_q', 'new_v_sb_w_o']
TWIN_LEAF_KINDS = {'loss': 'loss', 'grad_x': 'grad_x', 'grad_mix_pre_gain': 'grad_w', 'grad_mix_post_gain': 'grad_w', 'grad_mlp_pre_gain': 'grad_w', 'grad_mlp_post_gain': 'grad_w', 'grad_mlp_w_up': 'grad_w', 'grad_mlp_w_down': 'grad_w', 'grad_gdn_w_in': 'grad_w', 'grad_gdn_conv_w': 'grad_w', 'grad_gdn_a_log': 'grad_w', 'grad_gdn_dt_bias': 'grad_w', 'grad_gdn_out_gain': 'grad_w', 'grad_gdn_w_out': 'grad_w', 'grad_kv_gain': 'grad_w', 'grad_w_kv': 'grad_w', 'grad_sb_w_q': 'grad_w', 'grad_sb_w_o': 'grad_w', 'delta_mix_pre_gain': 'delta_w', 'delta_mix_post_gain': 'delta_w', 'delta_mlp_pre_gain': 'delta_w', 'delta_mlp_post_gain': 'delta_w', 'delta_mlp_w_up': 'delta_w', 'delta_mlp_w_down': 'delta_w', 'delta_gdn_w_in': 'delta_w', 'delta_gdn_conv_w': 'delta_w', 'delta_gdn_a_log': 'delta_w', 'delta_gdn_dt_bias': 'delta_w', 'delta_gdn_out_gain': 'delta_w', 'delta_gdn_w_out': 'delta_w', 'delta_kv_gain': 'delta_w', 'delta_w_kv': 'delta_w', 'delta_sb_w_q': 'delta_w', 'delta_sb_w_o': 'delta_w', 'new_m_mix_pre_gain': 'new_m', 'new_m_mix_post_gain': 'new_m', 'new_m_mlp_pre_gain': 'new_m', 'new_m_mlp_post_gain': 'new_m', 'new_m_mlp_w_up': 'new_m', 'new_m_mlp_w_down': 'new_m', 'new_m_gdn_w_in': 'new_m', 'new_m_gdn_conv_w': 'new_m', 'new_m_gdn_a_log': 'new_m', 'new_m_gdn_dt_bias': 'new_m', 'new_m_gdn_out_gain': 'new_m', 'new_m_gdn_w_out': 'new_m', 'new_m_kv_gain': 'new_m', 'new_m_w_kv': 'new_m', 'new_m_sb_w_q': 'new_m', 'new_m_sb_w_o': 'new_m', 'new_v_mix_pre_gain': 'new_v', 'new_v_mix_post_gain': 'new_v', 'new_v_mlp_pre_gain': 'new_v', 'new_v_mlp_post_gain': 'new_v', 'new_v_mlp_w_up': 'new_v', 'new_v_mlp_w_down': 'new_v', 'new_v_gdn_w_in': 'new_v', 'new_v_gdn_conv_w': 'new_v', 'new_v_gdn_a_log': 'new_v', 'new_v_gdn_dt_bias': 'new_v', 'new_v_gdn_out_gain': 'new_v', 'new_v_gdn_w_out': 'new_v', 'new_v_kv_gain': 'new_v', 'new_v_w_kv': 'new_v', 'new_v_sb_w_q': 'new_v', 'new_v_sb_w_o': 'new_v'}


def _forward(args):
    return _fwd_reference(*[args[k] for k in FWD_PARAMS])


def _output_shape():
    def fwd():
        inp = _fwd_setup_inputs(0)
        return _fwd_reference(*[inp[k] for k in FWD_PARAMS])
    out = _jax.eval_shape(fwd)
    return out.shape, out.dtype

N_MICROBATCH = 1
ADAM_LR = 0.001
ADAM_B1 = 0.9
ADAM_B2 = 0.999
ADAM_EPS = 1e-08
ADAM_WD = 0.01
ADAM_STEP = 10
PER_EXAMPLE_BATCH_AXIS = {'x': 0, 'loss_target': 0}
SHARED_INPUTS = []
_WEIGHT_DTYPES = {'mix_pre_gain': _jnp.float32, 'mix_post_gain': _jnp.float32, 'mlp_pre_gain': _jnp.float32, 'mlp_post_gain': _jnp.float32, 'mlp_w_up': _jnp.float32, 'mlp_w_down': _jnp.float32, 'gdn_w_in': _jnp.float32, 'gdn_conv_w': _jnp.float32, 'gdn_a_log': _jnp.float32, 'gdn_dt_bias': _jnp.float32, 'gdn_out_gain': _jnp.float32, 'gdn_w_out': _jnp.float32, 'kv_gain': _jnp.float32, 'w_kv': _jnp.float32, 'sb_w_q': _jnp.float32, 'sb_w_o': _jnp.float32}
MOMENT_SCALE = {'mix_pre_gain': 1.482019e+00, 'mix_post_gain': 6.687450e+01, 'mlp_pre_gain': 7.623502e+00, 'mlp_post_gain': 7.036331e+01, 'mlp_w_up': 3.753865e+00, 'mlp_w_down': 2.162160e+01, 'gdn_w_in': 1.035416e+00, 'gdn_conv_w': 5.512853e+00, 'gdn_a_log': 1.534856e+01, 'gdn_dt_bias': 1.546932e+01, 'gdn_out_gain': 4.415183e+01, 'gdn_w_out': 1.482906e+01, 'kv_gain': 1.843851e+01, 'w_kv': 1.315250e+01, 'sb_w_q': 3.383173e-01, 'sb_w_o': 1.779021e+01}


def _to_microbatches(a, axis):
    t = _jnp.moveaxis(a, axis, 0)
    t = t.reshape((N_MICROBATCH, t.shape[0] // N_MICROBATCH) + t.shape[1:])
    return _jnp.moveaxis(t, 1, axis + 1)


def setup_inputs(seed: int = 0) -> dict:
    inp = _fwd_setup_inputs(seed)
    key = _jax.random.fold_in(_jax.random.key(seed), 7919)
    shape, _ = _output_shape()
    out = dict(inp)
    out["loss_target"] = _jax.random.normal(_jax.random.fold_in(key, 0), shape, _jnp.float32)
    for i, name in enumerate(TWIN_WEIGHTS):
        w = inp[name].astype(_jnp.float32)
        if MOMENT_SCALE is None:
            s = _jnp.sqrt(_jnp.mean(_jnp.square(w)) + 1e-30)
        else:
            s = MOMENT_SCALE[name]
        km, kv = _jax.random.split(_jax.random.fold_in(key, i + 1))
        out[name] = w
        out["m_" + name] = s * _jax.random.normal(km, w.shape, _jnp.float32)
        out["v_" + name] = (s * s) * _jax.random.uniform(kv, w.shape, _jnp.float32, 0.5, 1.5)
    if N_MICROBATCH > 1:
        for name, axis in PER_EXAMPLE_BATCH_AXIS.items():
            out[name] = _to_microbatches(out[name], axis)
    return {'x': out['x'], 'mix_pre_gain': out['mix_pre_gain'], 'mix_post_gain': out['mix_post_gain'], 'mlp_pre_gain': out['mlp_pre_gain'], 'mlp_post_gain': out['mlp_post_gain'], 'mlp_w_up': out['mlp_w_up'], 'mlp_w_down': out['mlp_w_down'], 'gdn_w_in': out['gdn_w_in'], 'gdn_conv_w': out['gdn_conv_w'], 'gdn_a_log': out['gdn_a_log'], 'gdn_dt_bias': out['gdn_dt_bias'], 'gdn_out_gain': out['gdn_out_gain'], 'gdn_w_out': out['gdn_w_out'], 'kv_gain': out['kv_gain'], 'w_kv': out['w_kv'], 'sb_w_q': out['sb_w_q'], 'sb_w_o': out['sb_w_o'], 'loss_target': out['loss_target'], 'm_mix_pre_gain': out['m_mix_pre_gain'], 'm_mix_post_gain': out['m_mix_post_gain'], 'm_mlp_pre_gain': out['m_mlp_pre_gain'], 'm_mlp_post_gain': out['m_mlp_post_gain'], 'm_mlp_w_up': out['m_mlp_w_up'], 'm_mlp_w_down': out['m_mlp_w_down'], 'm_gdn_w_in': out['m_gdn_w_in'], 'm_gdn_conv_w': out['m_gdn_conv_w'], 'm_gdn_a_log': out['m_gdn_a_log'], 'm_gdn_dt_bias': out['m_gdn_dt_bias'], 'm_gdn_out_gain': out['m_gdn_out_gain'], 'm_gdn_w_out': out['m_gdn_w_out'], 'm_kv_gain': out['m_kv_gain'], 'm_w_kv': out['m_w_kv'], 'm_sb_w_q': out['m_sb_w_q'], 'm_sb_w_o': out['m_sb_w_o'], 'v_mix_pre_gain': out['v_mix_pre_gain'], 'v_mix_post_gain': out['v_mix_post_gain'], 'v_mlp_pre_gain': out['v_mlp_pre_gain'], 'v_mlp_post_gain': out['v_mlp_post_gain'], 'v_mlp_w_up': out['v_mlp_w_up'], 'v_mlp_w_down': out['v_mlp_w_down'], 'v_gdn_w_in': out['v_gdn_w_in'], 'v_gdn_conv_w': out['v_gdn_conv_w'], 'v_gdn_a_log': out['v_gdn_a_log'], 'v_gdn_dt_bias': out['v_gdn_dt_bias'], 'v_gdn_out_gain': out['v_gdn_out_gain'], 'v_gdn_w_out': out['v_gdn_w_out'], 'v_kv_gain': out['v_kv_gain'], 'v_w_kv': out['v_w_kv'], 'v_sb_w_q': out['v_sb_w_q'], 'v_sb_w_o': out['v_sb_w_o']}


def _loss(weights, diff, rest, loss_target):
    with _jax.named_scope("forward"):
        args = {**rest, TWIN_DIFF_INPUT: diff, **{k: w.astype(_WEIGHT_DTYPES[k]) for k, w in weights.items()}}
        y = _forward(args)
    with _jax.named_scope("loss_head"):
        err = _jnp.square(y.astype(_jnp.float32) - loss_target)
        return 0.5 * _jnp.sum(_jnp.mean(err, axis=-1)) if err.ndim else 0.5 * err


def _adamw(w, g, m, v):
    m = ADAM_B1 * m + (1.0 - ADAM_B1) * g
    v = ADAM_B2 * v + (1.0 - ADAM_B2) * _jnp.square(g)
    m_hat = m / (1.0 - ADAM_B1 ** ADAM_STEP)
    v_hat = v / (1.0 - ADAM_B2 ** ADAM_STEP)
    delta = -ADAM_LR * (m_hat / (_jnp.sqrt(v_hat) + ADAM_EPS) + ADAM_WD * w)
    return delta, m, v


def reference(x, mix_pre_gain, mix_post_gain, mlp_pre_gain, mlp_post_gain, mlp_w_up, mlp_w_down, gdn_w_in, gdn_conv_w, gdn_a_log, gdn_dt_bias, gdn_out_gain, gdn_w_out, kv_gain, w_kv, sb_w_q, sb_w_o, loss_target, m_mix_pre_gain, m_mix_post_gain, m_mlp_pre_gain, m_mlp_post_gain, m_mlp_w_up, m_mlp_w_down, m_gdn_w_in, m_gdn_conv_w, m_gdn_a_log, m_gdn_dt_bias, m_gdn_out_gain, m_gdn_w_out, m_kv_gain, m_w_kv, m_sb_w_q, m_sb_w_o, v_mix_pre_gain, v_mix_post_gain, v_mlp_pre_gain, v_mlp_post_gain, v_mlp_w_up, v_mlp_w_down, v_gdn_w_in, v_gdn_conv_w, v_gdn_a_log, v_gdn_dt_bias, v_gdn_out_gain, v_gdn_w_out, v_kv_gain, v_w_kv, v_sb_w_q, v_sb_w_o):
    given = dict(x=x, mix_pre_gain=mix_pre_gain, mix_post_gain=mix_post_gain, mlp_pre_gain=mlp_pre_gain, mlp_post_gain=mlp_post_gain, mlp_w_up=mlp_w_up, mlp_w_down=mlp_w_down, gdn_w_in=gdn_w_in, gdn_conv_w=gdn_conv_w, gdn_a_log=gdn_a_log, gdn_dt_bias=gdn_dt_bias, gdn_out_gain=gdn_out_gain, gdn_w_out=gdn_w_out, kv_gain=kv_gain, w_kv=w_kv, sb_w_q=sb_w_q, sb_w_o=sb_w_o, loss_target=loss_target, m_mix_pre_gain=m_mix_pre_gain, m_mix_post_gain=m_mix_post_gain, m_mlp_pre_gain=m_mlp_pre_gain, m_mlp_post_gain=m_mlp_post_gain, m_mlp_w_up=m_mlp_w_up, m_mlp_w_down=m_mlp_w_down, m_gdn_w_in=m_gdn_w_in, m_gdn_conv_w=m_gdn_conv_w, m_gdn_a_log=m_gdn_a_log, m_gdn_dt_bias=m_gdn_dt_bias, m_gdn_out_gain=m_gdn_out_gain, m_gdn_w_out=m_gdn_w_out, m_kv_gain=m_kv_gain, m_w_kv=m_w_kv, m_sb_w_q=m_sb_w_q, m_sb_w_o=m_sb_w_o, v_mix_pre_gain=v_mix_pre_gain, v_mix_post_gain=v_mix_post_gain, v_mlp_pre_gain=v_mlp_pre_gain, v_mlp_post_gain=v_mlp_post_gain, v_mlp_w_up=v_mlp_w_up, v_mlp_w_down=v_mlp_w_down, v_gdn_w_in=v_gdn_w_in, v_gdn_conv_w=v_gdn_conv_w, v_gdn_a_log=v_gdn_a_log, v_gdn_dt_bias=v_gdn_dt_bias, v_gdn_out_gain=v_gdn_out_gain, v_gdn_w_out=v_gdn_w_out, v_kv_gain=v_kv_gain, v_w_kv=v_w_kv, v_sb_w_q=v_sb_w_q, v_sb_w_o=v_sb_w_o)
    weights = {n: given[n] for n in TWIN_WEIGHTS}
    shared = {n: given[n] for n in SHARED_INPUTS}
    per_example = {n: given[n] for n in ['x']}
    grad_fn = _jax.value_and_grad(_loss, argnums=(0, 1))

    def one_microbatch(ex, loss_target):
        ex = dict(ex)
        diff = ex.pop(TWIN_DIFF_INPUT)
        return grad_fn(weights, diff, {**shared, **ex}, loss_target)

    if N_MICROBATCH == 1:
        loss, (grad_w, grad_x) = one_microbatch(per_example, given["loss_target"])
    else:
        def body(carry, xs):
            loss_sum, grad_sum = carry
            l_k, (gw_k, gx_k) = one_microbatch(xs[0], xs[1])
            with _jax.named_scope("update"):
                return (loss_sum + l_k, _jax.tree.map(_jnp.add, grad_sum, gw_k)), gx_k

        init = (_jnp.zeros((), _jnp.float32), _jax.tree.map(_jnp.zeros_like, weights))
        (loss, grad_w), grad_x = _jax.lax.scan(body, init, (per_example, given["loss_target"]))
    with _jax.named_scope("update"):
        delta_w, new_m, new_v = {}, {}, {}
        for n in TWIN_WEIGHTS:
            delta_w[n], new_m[n], new_v[n] = _adamw(weights[n], grad_w[n], given["m_" + n], given["v_" + n])
    return (loss, grad_x, *[grad_w[n] for n in TWIN_WEIGHTS], *[delta_w[n] for n in TWIN_WEIGHTS],
            *[new_m[n] for n in TWIN_WEIGHTS], *[new_v[n] for n in TWIN_WEIGHTS])
```

```python
import functools
import math

import jax
import jax.numpy as jnp
from jax import lax
from jax.experimental import pallas as pl
from jax.experimental.pallas import tpu as pltpu

F32 = jnp.float32
BF16 = jnp.bfloat16

N_DEV = 8
D_MODEL = 1024
D_FF = 4096
N_HEADS = 8
HEAD_DIM = 128
CHUNK = 64
CONV_WIDTH = 4
GDN_IN_COLS = 4 * D_MODEL + 2 * N_HEADS
GDN_IN_PAD = 4 * D_MODEL + 128
EPS = 1e-6

ADAM_LR = 0.001
ADAM_B1 = 0.9
ADAM_B2 = 0.999
ADAM_EPS = 1e-08
ADAM_WD = 0.01
ADAM_STEP = 10

VMEM_LIMIT_BYTES = 56 * 1024 * 1024
MESH = pl.DeviceIdType.MESH


def _params(*semantics):
    return pltpu.CompilerParams(dimension_semantics=semantics, vmem_limit_bytes=VMEM_LIMIT_BYTES)


def _dims(ta, tb):
    return (((0,) if ta else (1,), (1,) if tb else (0,)), ((), ()))


def _dot(a, b, ta=False, tb=False):
    return lax.dot_general(a.astype(BF16), b.astype(BF16), _dims(ta, tb), preferred_element_type=F32)


def _dot_f32(a, b, ta=False, tb=False):
    return lax.dot_general(a, b, _dims(ta, tb), precision=lax.Precision.HIGHEST, preferred_element_type=F32)


def _make_mm(dot):
    @functools.partial(jax.custom_vjp, nondiff_argnums=(2, 3))
    def mm(a, b, ta, tb):
        return dot(a, b, ta, tb)

    def fwd(a, b, ta, tb):
        return dot(a, b, ta, tb), (a, b)

    def bwd(ta, tb, res, g):
        a, b = res
        if not ta and not tb:
            return mm(g, b, False, True), mm(a, g, True, False)
        if not ta and tb:
            return mm(g, b, False, False), mm(g, a, True, False)
        if ta and not tb:
            return mm(b, g, False, True), mm(a, g, False, False)
        raise NotImplementedError

    mm.defvjp(fwd, bwd)
    return mm


_mm = _make_mm(_dot)
_mm_f32 = _make_mm(_dot_f32)


def _split_dot(x, ones_bf16):
    hi = x.astype(BF16)
    lo = (x - hi.astype(F32)).astype(BF16)
    return (jnp.dot(hi, ones_bf16, preferred_element_type=F32)
            + jnp.dot(lo, ones_bf16, preferred_element_type=F32))


def _rms(x, gain):
    r = lax.rsqrt(jnp.mean(x * x, axis=-1, keepdims=True) + EPS)
    return x * r * gain


def _rms_bwd(x, gain, dy):
    r = lax.rsqrt(jnp.mean(x * x, axis=-1, keepdims=True) + EPS)
    xh = x * r
    dgain = jnp.sum(dy * xh, axis=0, keepdims=True)
    dxh = dy * gain
    dx = r * (dxh - xh * jnp.mean(dxh * xh, axis=-1, keepdims=True))
    return dx, dgain


def _silu(x):
    return x / (1.0 + jnp.exp(-x))


def _acc_rows(ref, first, rows):
    @pl.when(first)
    def _():
        ref[...] = jnp.zeros_like(ref)

    for r, val in enumerate(rows):
        ref[r:r + 1, :] += val


def _norm_matmul(x, gain, w, out_dtype, name, tm=512, tn=512):
    n, d = x.shape
    g, _, wc = w.shape
    tn = min(tn, wc)
    per = wc // tn
    assert n % tm == 0 and wc % tn == 0

    def body(x_ref, gain_ref, w_ref, out_ref, h_ref):
        @pl.when(pl.program_id(1) == 0)
        def _():
            h_ref[...] = _rms(x_ref[...], gain_ref[...]).astype(BF16)

        out_ref[...] = jnp.dot(h_ref[...], w_ref[...], preferred_element_type=F32).astype(out_dtype)

    return pl.pallas_call(
        body, name=name, grid=(n // tm, g * per),
        in_specs=[pl.BlockSpec((tm, d), lambda i, j: (i, 0)),
                  pl.BlockSpec((1, d), lambda i, j: (0, 0)),
                  pl.BlockSpec((None, d, tn), lambda i, j: (j // per, 0, j % per))],
        out_specs=[pl.BlockSpec((tm, tn), lambda i, j: (i, j)),
                   pl.BlockSpec((tm, d), lambda i, j: (i, 0))],
        out_shape=[jax.ShapeDtypeStruct((n, g * wc), out_dtype), jax.ShapeDtypeStruct((n, d), BF16)],
        compiler_params=_params("parallel", "arbitrary"),
    )(x, gain, w)


def _norm_matmul_bwd(dout, w, x, gain, add, name, tm=512, tn=512):
    n, d = x.shape
    g, _, wc = w.shape
    tn = min(tn, wc)
    per = wc // tn
    ncol = g * per

    def body(dout_ref, w_ref, x_ref, gain_ref, add_ref, dx_ref, dgain_ref, acc_ref):
        i, j = pl.program_id(0), pl.program_id(1)

        @pl.when(j == 0)
        def _():
            acc_ref[...] = jnp.zeros_like(acc_ref)

        acc_ref[...] += _dot(dout_ref[...], w_ref[...], tb=True)

        @pl.when(j == ncol - 1)
        def _():
            dx, dgain = _rms_bwd(x_ref[...], gain_ref[...], acc_ref[...])
            dx_ref[...] = add_ref[...] + dx
            _acc_rows(dgain_ref, i == 0, [dgain])

    return pl.pallas_call(
        body, name=name, grid=(n // tm, ncol),
        in_specs=[pl.BlockSpec((tm, tn), lambda i, j: (i, j)),
                  pl.BlockSpec((None, d, tn), lambda i, j: (j // per, 0, j % per)),
                  pl.BlockSpec((tm, d), lambda i, j: (i, 0)),
                  pl.BlockSpec((1, d), lambda i, j: (0, 0)),
                  pl.BlockSpec((tm, d), lambda i, j: (i, 0))],
        out_specs=[pl.BlockSpec((tm, d), lambda i, j: (i, 0)),
                   pl.BlockSpec((8, d), lambda i, j: (0, 0))],
        out_shape=[jax.ShapeDtypeStruct((n, d), F32), jax.ShapeDtypeStruct((8, d), F32)],
        scratch_shapes=[pltpu.VMEM((tm, d), F32)],
        compiler_params=_params("arbitrary", "arbitrary"),
    )(dout, w, x, gain, add)


def _matmul_tn(a, b, groups, name, ta=512, tb=512, tk=1024):
    n, ka = a.shape
    _, kb = b.shape
    wc = kb // groups
    ta, tb, tk = min(ta, ka), min(tb, wc), min(tk, n)
    per = wc // tb
    nk = n // tk
    assert ka % ta == 0 and wc % tb == 0 and n % tk == 0

    def body(a_ref, b_ref, out_ref):
        @pl.when(pl.program_id(2) == 0)
        def _():
            out_ref[...] = jnp.zeros_like(out_ref)

        out_ref[...] += _dot(a_ref[...], b_ref[...], ta=True)

    return pl.pallas_call(
        body, name=name, grid=(ka // ta, groups * per, nk),
        in_specs=[pl.BlockSpec((tk, ta), lambda i, j, k: (k, i)),
                  pl.BlockSpec((tk, tb), lambda i, j, k: (k, j))],
        out_specs=pl.BlockSpec((None, ta, tb), lambda i, j, k: (j // per, i, j % per)),
        out_shape=jax.ShapeDtypeStruct((groups, ka, wc), F32),
        compiler_params=_params("parallel", "parallel", "arbitrary"),
    )(a, b)


def _gated_head_norm(o, gate, out_gain):
    parts = []
    for h in range(N_HEADS):
        sl = slice(h * HEAD_DIM, (h + 1) * HEAD_DIM)
        parts.append(_rms(o[:, sl], out_gain) * _silu(gate[:, sl]))
    return parts


def _out_proj(a_inputs, gated, x_in, w, gain, out_gain, name, tm=512):
    n, d = x_in.shape
    k = w.shape[0]

    def body(*refs):
        if gated:
            o_ref, gate_ref, og_ref, x_ref, w_ref, gain_ref, xo_ref, y_ref, a_ref = refs
            parts = _gated_head_norm(o_ref[...], gate_ref[...], og_ref[...])
            for h, part in enumerate(parts):
                a_ref[:, h * HEAD_DIM:(h + 1) * HEAD_DIM] = part.astype(BF16)
            a = a_ref[...]
        else:
            a_in_ref, x_ref, w_ref, gain_ref, xo_ref, y_ref = refs
            a = a_in_ref[...]
        y = jnp.dot(a, w_ref[...], preferred_element_type=F32)
        y_ref[...] = y
        xo_ref[...] = x_ref[...] + _rms(y, gain_ref[...])

    row = lambda i: (i, 0)
    const = lambda i: (0, 0)
    if gated:
        a_specs = [pl.BlockSpec((tm, k), row), pl.BlockSpec((tm, D_MODEL), lambda i: (i, 3)),
                   pl.BlockSpec((1, HEAD_DIM), const)]
        a_args = list(a_inputs) + [out_gain]
    else:
        a_specs = [pl.BlockSpec((tm, k), row)]
        a_args = list(a_inputs)
    out_specs = [pl.BlockSpec((tm, d), row), pl.BlockSpec((tm, d), row)]
    out_shape = [jax.ShapeDtypeStruct((n, d), F32), jax.ShapeDtypeStruct((n, d), F32)]
    if gated:
        out_specs.append(pl.BlockSpec((tm, k), row))
        out_shape.append(jax.ShapeDtypeStruct((n, k), BF16))
    return pl.pallas_call(
        body, name=name, grid=(n // tm,),
        in_specs=a_specs + [pl.BlockSpec((tm, d), row), pl.BlockSpec((k, d), const), pl.BlockSpec((1, d), const)],
        out_specs=out_specs, out_shape=out_shape,
        compiler_params=_params("parallel"),
    )(*a_args, x_in, w, gain)


def _out_proj_bwd(dxo, y, w, gain, gated_inputs, name, tm=512):
    n, d = dxo.shape
    k = w.shape[0]
    gated = gated_inputs is not None

    def body(*refs):
        if gated:
            (dxo_ref, y_ref, w_ref, gain_ref, o_ref, gate_ref, og_ref,
             dy_ref, dgain_ref, do_ref, dgate_ref, dog_ref) = refs
        else:
            dxo_ref, y_ref, w_ref, gain_ref, dy_ref, dgain_ref, da_ref = refs
        first = pl.program_id(0) == 0
        dy, dgain = _rms_bwd(y_ref[...], gain_ref[...], dxo_ref[...])
        dy_ref[...] = dy.astype(BF16)
        _acc_rows(dgain_ref, first, [dgain])
        da = _dot(dy_ref[...], w_ref[...], tb=True)
        if not gated:
            da_ref[...] = da.astype(BF16)
            return
        og = og_ref[...]
        dog = jnp.zeros_like(og)
        for h in range(N_HEADS):
            sl = slice(h * HEAD_DIM, (h + 1) * HEAD_DIM)
            fn = lambda o_h, g_h, gn: _rms(o_h, gn) * _silu(g_h)
            _, vjp = jax.vjp(fn, o_ref[:, sl], gate_ref[:, sl], og)
            do_h, dgate_h, dog_h = vjp(da[:, sl])
            do_ref[:, sl] = do_h
            dgate_ref[:, sl] = dgate_h
            dog = dog + dog_h
        _acc_rows(dog_ref, first, [dog])

    row = lambda i: (i, 0)
    const = lambda i: (0, 0)
    in_specs = [pl.BlockSpec((tm, d), row), pl.BlockSpec((tm, d), row), pl.BlockSpec((k, d), const),
                pl.BlockSpec((1, d), const)]
    args = [dxo, y, w, gain]
    out_specs = [pl.BlockSpec((tm, d), row), pl.BlockSpec((8, d), const)]
    out_shape = [jax.ShapeDtypeStruct((n, d), BF16), jax.ShapeDtypeStruct((8, d), F32)]
    if gated:
        in_specs += [pl.BlockSpec((tm, k), row), pl.BlockSpec((tm, D_MODEL), lambda i: (i, 3)),
                     pl.BlockSpec((1, HEAD_DIM), const)]
        args += list(gated_inputs)
        out_specs += [pl.BlockSpec((tm, k), row), pl.BlockSpec((tm, k), row), pl.BlockSpec((8, HEAD_DIM), const)]
        out_shape += [jax.ShapeDtypeStruct((n, k), F32), jax.ShapeDtypeStruct((n, k), F32),
                      jax.ShapeDtypeStruct((8, HEAD_DIM), F32)]
    else:
        out_specs.append(pl.BlockSpec((tm, k), row))
        out_shape.append(jax.ShapeDtypeStruct((n, k), BF16))
    return pl.pallas_call(
        body, name=name, grid=(n // tm,), in_specs=in_specs, out_specs=out_specs, out_shape=out_shape,
        compiler_params=_params("arbitrary"),
    )(*args)


def _mlp_fwd(x_in, g_pre, w_up, w_down, g_post, name, tm=512):
    n, d = x_in.shape
    g, _, wc = w_up.shape

    def body(x_ref, gpre_ref, wup_ref, wdown_ref, gpost_ref, xo_ref, y_ref, h_ref, u_ref, a_ref, acc_ref):
        j = pl.program_id(1)

        @pl.when(j == 0)
        def _():
            h_ref[...] = _rms(x_ref[...], gpre_ref[...]).astype(BF16)
            acc_ref[...] = jnp.zeros_like(acc_ref)

        u = jnp.dot(h_ref[...], wup_ref[...], preferred_element_type=F32).astype(BF16)
        u_ref[...] = u
        a = jnp.square(jnp.maximum(u, 0))
        a_ref[...] = a
        acc_ref[...] += jnp.dot(a, wdown_ref[...], preferred_element_type=F32)

        @pl.when(j == g - 1)
        def _():
            y = acc_ref[...]
            y_ref[...] = y
            xo_ref[...] = x_ref[...] + _rms(y, gpost_ref[...])

    row = lambda i, j: (i, 0)
    const = lambda i, j: (0, 0)
    return pl.pallas_call(
        body, name=name, grid=(n // tm, g),
        in_specs=[pl.BlockSpec((tm, d), row), pl.BlockSpec((1, d), const),
                  pl.BlockSpec((None, d, wc), lambda i, j: (j, 0, 0)),
                  pl.BlockSpec((wc, d), lambda i, j: (j, 0)), pl.BlockSpec((1, d), const)],
        out_specs=[pl.BlockSpec((tm, d), row), pl.BlockSpec((tm, d), row), pl.BlockSpec((tm, d), row),
                   pl.BlockSpec((tm, wc), lambda i, j: (i, j)), pl.BlockSpec((tm, wc), lambda i, j: (i, j))],
        out_shape=[jax.ShapeDtypeStruct((n, d), F32), jax.ShapeDtypeStruct((n, d), F32),
                   jax.ShapeDtypeStruct((n, d), BF16), jax.ShapeDtypeStruct((n, g * wc), BF16),
                   jax.ShapeDtypeStruct((n, g * wc), BF16)],
        scratch_shapes=[pltpu.VMEM((tm, d), F32)],
        compiler_params=_params("parallel", "arbitrary"),
    )(x_in, g_pre, w_up, w_down, g_post)


def _mlp_bwd(dxo, y, g_post, w_down, u, w_up, x_in, g_pre, name, tm=512):
    n, d = dxo.shape
    g, _, wc = w_up.shape

    def body(dxo_ref, y_ref, gpost_ref, wdown_ref, u_ref, wup_ref, x_ref, gpre_ref,
             dx_ref, dy_ref, du_ref, dgain_ref, acc_ref, dgpost_ref):
        i, j = pl.program_id(0), pl.program_id(1)

        @pl.when(j == 0)
        def _():
            dy, dgpost = _rms_bwd(y_ref[...], gpost_ref[...], dxo_ref[...])
            dy_ref[...] = dy.astype(BF16)
            dgpost_ref[...] = dgpost
            acc_ref[...] = jnp.zeros_like(acc_ref)

        da = _dot(dy_ref[...], wdown_ref[...], tb=True)
        du = (da * (2.0 * jnp.maximum(u_ref[...], 0).astype(F32))).astype(BF16)
        du_ref[...] = du
        acc_ref[...] += _dot(du, wup_ref[...], tb=True)

        @pl.when(j == g - 1)
        def _():
            dx, dgpre = _rms_bwd(x_ref[...], gpre_ref[...], acc_ref[...])
            dx_ref[...] = dxo_ref[...] + dx
            _acc_rows(dgain_ref, i == 0, [dgpre, dgpost_ref[...]])

    row = lambda i, j: (i, 0)
    const = lambda i, j: (0, 0)
    return pl.pallas_call(
        body, name=name, grid=(n // tm, g),
        in_specs=[pl.BlockSpec((tm, d), row), pl.BlockSpec((tm, d), row), pl.BlockSpec((1, d), const),
                  pl.BlockSpec((wc, d), lambda i, j: (j, 0)), pl.BlockSpec((tm, wc), lambda i, j: (i, j)),
                  pl.BlockSpec((None, d, wc), lambda i, j: (j, 0, 0)), pl.BlockSpec((tm, d), row),
                  pl.BlockSpec((1, d), const)],
        out_specs=[pl.BlockSpec((tm, d), row), pl.BlockSpec((tm, d), row),
                   pl.BlockSpec((tm, wc), lambda i, j: (i, j)), pl.BlockSpec((8, d), const)],
        out_shape=[jax.ShapeDtypeStruct((n, d), F32), jax.ShapeDtypeStruct((n, d), BF16),
                   jax.ShapeDtypeStruct((n, g * wc), BF16), jax.ShapeDtypeStruct((8, d), F32)],
        scratch_shapes=[pltpu.VMEM((tm, d), F32), pltpu.VMEM((1, d), F32)],
        compiler_params=_params("arbitrary", "arbitrary"),
    )(dxo, y, g_post, w_down, u, w_up, x_in, g_pre)


QKV = 3 * D_MODEL


def _shifted(x, prev8, s):
    if s == 0:
        return x
    tm = x.shape[0]
    rolled = pltpu.roll(x, s, 0)
    head = pltpu.roll(prev8, s, 0)
    head = jnp.concatenate([head, jnp.zeros((tm - 8, x.shape[1]), x.dtype)], axis=0)
    rows = lax.broadcasted_iota(jnp.int32, x.shape, 0)
    return jnp.where(rows < s, head, rolled)


def _conv(x, prev8, conv_w):
    out = x * conv_w[3:4, :]
    for s in range(1, CONV_WIDTH):
        out = out + _shifted(x, prev8, s) * conv_w[3 - s:4 - s, :]
    return out


def _l2norm(x):
    return x * lax.rsqrt(jnp.sum(x * x, axis=-1, keepdims=True) + EPS)


def _gdn_act_head(cq, ck, cv):
    return _l2norm(_silu(cq)) * (HEAD_DIM ** -0.5), _l2norm(_silu(ck)), _silu(cv)


def _gdn_gates(ba, alog, dtb):
    lane = lax.broadcasted_iota(jnp.int32, ba.shape, 1)
    beta = 1.0 / (1.0 + jnp.exp(-ba))
    t = ba + dtb
    softplus = jnp.maximum(t, 0.0) + jnp.log(1.0 + jnp.exp(-jnp.abs(t)))
    g = -jnp.exp(alog) * softplus
    return jnp.where(lane < N_HEADS, beta, jnp.where(lane < 2 * N_HEADS, g, 0.0))


def _qkv_cols(part, h):
    start = part * D_MODEL + h * HEAD_DIM
    return slice(start, start + HEAD_DIM)


def _prev_rows_spec(tm, cols, colblock, order):
    per = tm // 8
    return pl.BlockSpec((8, cols), lambda i: (jnp.maximum(order(i) * per - 1, 0), colblock))


def _gdn_pre(proj, conv_w, alog, dtb, name, tm=256):
    n = proj.shape[0]

    def body(x_ref, prev_ref, ba_ref, cw_ref, alog_ref, dtb_ref, q_ref, k_ref, v_ref, bg_ref):
        first = pl.program_id(0) == 0
        for h in range(N_HEADS):
            convs = []
            for part in range(3):
                sl = _qkv_cols(part, h)
                prev8 = jnp.where(first, 0.0, prev_ref[:, sl])
                convs.append(_conv(x_ref[:, sl], prev8, cw_ref[:, sl]))
            out = slice(h * HEAD_DIM, (h + 1) * HEAD_DIM)
            q_ref[:, out], k_ref[:, out], v_ref[:, out] = _gdn_act_head(*convs)
        bg_ref[...] = _gdn_gates(ba_ref[...], alog_ref[...], dtb_ref[...])

    row = lambda i: (i, 0)
    const = lambda i: (0, 0)
    ident = lambda i: i
    return pl.pallas_call(
        body, name=name, grid=(n // tm,),
        in_specs=[pl.BlockSpec((tm, QKV), row), _prev_rows_spec(tm, QKV, 0, ident),
                  pl.BlockSpec((tm, 128), lambda i: (i, 4 * D_MODEL // 128)),
                  pl.BlockSpec((CONV_WIDTH, QKV), const), pl.BlockSpec((1, 128), const), pl.BlockSpec((1, 128), const)],
        out_specs=[pl.BlockSpec((tm, D_MODEL), row)] * 3 + [pl.BlockSpec((tm, 128), row)],
        out_shape=[jax.ShapeDtypeStruct((n, D_MODEL), F32)] * 3 + [jax.ShapeDtypeStruct((n, 128), F32)],
        compiler_params=_params("parallel"),
    )(proj, proj, proj, conv_w, alog, dtb)


def _gdn_pre_bwd(proj, conv_w, alog, dtb, dq, dk, dv, dbg, dgate, name, tm=256):
    n = proj.shape[0]
    nt = n // tm

    def body(x_ref, prev_ref, ba_ref, cw_ref, alog_ref, dtb_ref, dq_ref, dk_ref, dv_ref, dbg_ref, dgate_ref,
             dproj_ref, dcw_ref, dgates_ref, carry_ref):
        step = pl.program_id(0)
        tile = nt - 1 - step
        @pl.when(step == 0)
        def _():
            carry_ref[...] = jnp.zeros_like(carry_ref)
            dcw_ref[...] = jnp.zeros_like(dcw_ref)

        rows = lax.broadcasted_iota(jnp.int32, (tm, HEAD_DIM), 0)
        for h in range(N_HEADS):
            cols = [_qkv_cols(part, h) for part in range(3)]
            prevs = [jnp.where(tile == 0, 0.0, prev_ref[:, sl]) for sl in cols]
            convs = [_conv(x_ref[:, sl], prev8, cw_ref[:, sl]) for sl, prev8 in zip(cols, prevs)]
            _, vjp = jax.vjp(_gdn_act_head, *convs)
            out = slice(h * HEAD_DIM, (h + 1) * HEAD_DIM)
            dcs = vjp((dq_ref[:, out], dk_ref[:, out], dv_ref[:, out]))
            for sl, prev8, dc in zip(cols, prevs, dcs):
                x = x_ref[:, sl]
                cw = cw_ref[:, sl]
                dx = dc * cw[3:4, :]
                wrapped = jnp.zeros((8, HEAD_DIM), F32)
                dcw_ref[3:4, sl] += jnp.sum(dc * x, axis=0, keepdims=True)
                for s in range(1, CONV_WIDTH):
                    r = pltpu.roll(dc, tm - s, 0) * cw[3 - s:4 - s, :]
                    dx = dx + jnp.where(rows < tm - s, r, 0.0)
                    wrapped = wrapped + jnp.where(rows[tm - 8:, :] >= tm - s, r[tm - 8:, :], 0.0)
                    dcw_ref[3 - s:4 - s, sl] += jnp.sum(dc * _shifted(x, prev8, s), axis=0, keepdims=True)
                dproj_ref[:, sl] = dx
                dproj_ref[tm - 8:, sl] += carry_ref[:, sl]
                carry_ref[:, sl] = wrapped

        _, vjp = jax.vjp(_gdn_gates, ba_ref[...], alog_ref[...], dtb_ref[...])
        dba, dalog, ddtb = vjp(dbg_ref[...])
        dproj_ref[:, QKV:4 * D_MODEL] = dgate_ref[...]
        dproj_ref[:, 4 * D_MODEL:] = dba
        _acc_rows(dgates_ref, step == 0, [dalog, ddtb])

    rev = lambda i: nt - 1 - i
    row = lambda i: (rev(i), 0)
    const = lambda i: (0, 0)
    return pl.pallas_call(
        body, name=name, grid=(nt,),
        in_specs=[pl.BlockSpec((tm, QKV), row), _prev_rows_spec(tm, QKV, 0, rev),
                  pl.BlockSpec((tm, 128), lambda i: (rev(i), 4 * D_MODEL // 128)),
                  pl.BlockSpec((CONV_WIDTH, QKV), const), pl.BlockSpec((1, 128), const), pl.BlockSpec((1, 128), const),
                  pl.BlockSpec((tm, D_MODEL), row), pl.BlockSpec((tm, D_MODEL), row), pl.BlockSpec((tm, D_MODEL), row),
                  pl.BlockSpec((tm, 128), row), pl.BlockSpec((tm, D_MODEL), row)],
        out_specs=[pl.BlockSpec((tm, GDN_IN_PAD), row), pl.BlockSpec((8, QKV), const), pl.BlockSpec((8, 128), const)],
        out_shape=[jax.ShapeDtypeStruct((n, GDN_IN_PAD), F32), jax.ShapeDtypeStruct((8, QKV), F32),
                   jax.ShapeDtypeStruct((8, 128), F32)],
        scratch_shapes=[pltpu.VMEM((8, QKV), F32)],
        compiler_params=_params("arbitrary"),
    )(proj, proj, proj, conv_w, alog, dtb, dq, dk, dv, dbg, dgate)


@jax.custom_vjp
def _unit_lower_inverse(lower):
    c = lower.shape[0]
    eye = (lax.broadcasted_iota(jnp.int32, (c, c), 0) == lax.broadcasted_iota(jnp.int32, (c, c), 1)).astype(F32)
    y = -lower
    p = eye + y
    for _ in range(int(math.log2(c)) - 1):
        y = _dot_f32(y, y)
        p = p + _dot_f32(p, y)
    return p


def _unit_lower_inverse_fwd(lower):
    t = _unit_lower_inverse(lower)
    return t, t


def _unit_lower_inverse_bwd(t, dt):
    return (-_dot_f32(_dot_f32(t, dt, ta=True), t, tb=True),)


_unit_lower_inverse.defvjp(_unit_lower_inverse_fwd, _unit_lower_inverse_bwd)


def _gdn_chunk(qs, ks, vs, bg, states):
    c = CHUNK
    row = lax.broadcasted_iota(jnp.int32, (c, c), 0)
    col = lax.broadcasted_iota(jnp.int32, (c, c), 1)
    incl, strict, eye = row >= col, row > col, row == col
    lane = lax.broadcasted_iota(jnp.int32, (c, 128), 1)
    rowc = lax.broadcasted_iota(jnp.int32, (c, 1), 0)
    gc_all = _mm_f32(incl.astype(F32), bg, False, False)
    outs, new_states = [], []
    for h in range(N_HEADS):
        q, k, v, state = qs[h], ks[h], vs[h], states[h]
        beta = jnp.sum(jnp.where(lane == h, bg, 0.0), axis=1, keepdims=True)
        gc = jnp.sum(jnp.where(lane == N_HEADS + h, gc_all, 0.0), axis=1, keepdims=True)
        gc_row = jnp.sum(jnp.where(eye, gc, 0.0), axis=0, keepdims=True)
        gc_last = jnp.sum(jnp.where(rowc == c - 1, gc, 0.0), axis=0, keepdims=True)
        decay = jnp.where(incl, jnp.exp(jnp.where(incl, gc - gc_row, 0.0)), 0.0)
        kb = k * beta
        lower = jnp.where(strict, _mm(kb, k, False, True) * decay, 0.0)
        t_mat = _unit_lower_inverse(lower)
        egc = jnp.exp(gc)
        w = _mm(t_mat, kb * egc, False, False)
        u = _mm(t_mat, v * beta, False, False)
        attn = _mm(q, k, False, True) * decay
        qg = q * egc
        kg = k * jnp.exp(gc_last - gc)
        v_new = u - _mm(w, state, False, False)
        outs.append(_mm(qg, state, False, False) + _mm(attn, v_new, False, False))
        new_states.append(state * jnp.exp(gc_last) + _mm(kg, v_new, True, False))
    return tuple(outs), tuple(new_states)


def _head_slices(ref):
    return tuple(ref[:, h * HEAD_DIM:(h + 1) * HEAD_DIM] for h in range(N_HEADS))


def _gdn_scan(qn, kn, v, bg, name):
    n = qn.shape[0]
    nc = n // CHUNK

    def body(q_ref, k_ref, v_ref, bg_ref, o_ref, saved_ref, state_ref):
        @pl.when(pl.program_id(0) == 0)
        def _():
            state_ref[...] = jnp.zeros_like(state_ref)

        states = tuple(state_ref[h] for h in range(N_HEADS))
        for h in range(N_HEADS):
            saved_ref[h] = states[h]
        outs, new_states = _gdn_chunk(_head_slices(q_ref), _head_slices(k_ref), _head_slices(v_ref), bg_ref[...], states)
        for h in range(N_HEADS):
            o_ref[:, h * HEAD_DIM:(h + 1) * HEAD_DIM] = outs[h]
            state_ref[h] = new_states[h]

    row = lambda i: (i, 0)
    return pl.pallas_call(
        body, name=name, grid=(nc,),
        in_specs=[pl.BlockSpec((CHUNK, D_MODEL), row)] * 3 + [pl.BlockSpec((CHUNK, 128), row)],
        out_specs=[pl.BlockSpec((CHUNK, D_MODEL), row),
                   pl.BlockSpec((None, N_HEADS, HEAD_DIM, HEAD_DIM), lambda i: (i, 0, 0, 0))],
        out_shape=[jax.ShapeDtypeStruct((n, D_MODEL), F32),
                   jax.ShapeDtypeStruct((nc, N_HEADS, HEAD_DIM, HEAD_DIM), F32)],
        scratch_shapes=[pltpu.VMEM((N_HEADS, HEAD_DIM, HEAD_DIM), F32)],
        compiler_params=_params("arbitrary"),
    )(qn, kn, v, bg)


def _gdn_scan_bwd(qn, kn, v, bg, saved, do, name):
    n = qn.shape[0]
    nc = n // CHUNK

    def body(q_ref, k_ref, v_ref, bg_ref, saved_ref, do_ref, dq_ref, dk_ref, dv_ref, dbg_ref, dstate_ref):
        @pl.when(pl.program_id(0) == 0)
        def _():
            dstate_ref[...] = jnp.zeros_like(dstate_ref)

        states = tuple(saved_ref[h] for h in range(N_HEADS))
        _, vjp = jax.vjp(_gdn_chunk, _head_slices(q_ref), _head_slices(k_ref), _head_slices(v_ref), bg_ref[...], states)
        dstates = tuple(dstate_ref[h] for h in range(N_HEADS))
        dqs, dks, dvs, dbg, dprev = vjp((_head_slices(do_ref), dstates))
        for h in range(N_HEADS):
            sl = slice(h * HEAD_DIM, (h + 1) * HEAD_DIM)
            dq_ref[:, sl] = dqs[h]
            dk_ref[:, sl] = dks[h]
            dv_ref[:, sl] = dvs[h]
            dstate_ref[h] = dprev[h]
        dbg_ref[...] = dbg

    row = lambda i: (nc - 1 - i, 0)
    return pl.pallas_call(
        body, name=name, grid=(nc,),
        in_specs=[pl.BlockSpec((CHUNK, D_MODEL), row)] * 3 + [pl.BlockSpec((CHUNK, 128), row),
                  pl.BlockSpec((None, N_HEADS, HEAD_DIM, HEAD_DIM), lambda i: (nc - 1 - i, 0, 0, 0)),
                  pl.BlockSpec((CHUNK, D_MODEL), row)],
        out_specs=[pl.BlockSpec((CHUNK, D_MODEL), row)] * 3 + [pl.BlockSpec((CHUNK, 128), row)],
        out_shape=[jax.ShapeDtypeStruct((n, D_MODEL), F32)] * 3 + [jax.ShapeDtypeStruct((n, 128), F32)],
        scratch_shapes=[pltpu.VMEM((N_HEADS, HEAD_DIM, HEAD_DIM), F32)],
        compiler_params=_params("arbitrary"),
    )(qn, kn, v, bg, saved, do)


SB_BQ = 256
SB_BK = 128
SB_SCALE = HEAD_DIM ** -0.5


def _sb_logits(q, k_blk, q_pos0, k_pos0):
    bq, bk = q.shape[0], k_blk.shape[0]
    z = lax.dot_general(q, k_blk, _dims(False, True), preferred_element_type=F32) * SB_SCALE
    q_pos = q_pos0 + lax.broadcasted_iota(jnp.int32, (bq, bk), 0)
    k_pos = k_pos0 + lax.broadcasted_iota(jnp.int32, (bq, bk), 1)
    before = k_pos < q_pos
    e = jnp.exp(-jnp.abs(z))
    log_beta = jnp.minimum(z, 0.0) - jnp.log(1.0 + e)
    log_1m = jnp.where(before, log_beta - z, 0.0)
    return z, e, before, log_beta, log_1m


def _tri(n, cmp):
    r = lax.broadcasted_iota(jnp.int32, (n, n), 0)
    c = lax.broadcasted_iota(jnp.int32, (n, n), 1)
    return cmp(r, c).astype(BF16)


def _sb_attention(q, kv, name, bq=SB_BQ, bk=SB_BK):
    n = q.shape[0]
    bq = min(bq, n)
    per = bq // bk

    def body(q_ref, k_ref, v_ref, o_ref, l_ref):
        i = pl.program_id(1)
        qb = q_ref[...]
        after = _tri(bk, lambda r, c: r > c)

        def step(jj, carry):
            c_sum, acc = carry
            j = (i + 1) * per - 1 - jj
            start = pl.multiple_of(j * bk, bk)
            k_blk = k_ref[pl.ds(start, bk), :]
            v_blk = v_ref[pl.ds(start, bk), :]
            _, _, before, log_beta, log_1m = _sb_logits(qb, k_blk, i * bq, j * bk)
            tail = c_sum + _split_dot(log_1m, after)
            a = jnp.where(before, jnp.exp(log_beta + tail), 0.0)
            acc = acc + jnp.dot(a.astype(BF16), v_blk, preferred_element_type=F32)
            return c_sum + jnp.sum(log_1m, axis=1, keepdims=True), acc

        c_sum, acc = lax.fori_loop(0, (i + 1) * per, step,
                                   (jnp.zeros((bq, 1), F32), jnp.zeros((bq, HEAD_DIM), F32)))
        o_ref[...] = acc.astype(BF16)
        l_ref[...] = jnp.broadcast_to(c_sum, (bq, 128))

    return pl.pallas_call(
        body, name=name, grid=(N_HEADS, n // bq),
        in_specs=[pl.BlockSpec((bq, HEAD_DIM), lambda h, i: (i, h)),
                  pl.BlockSpec((n, HEAD_DIM), lambda h, i: (0, h)),
                  pl.BlockSpec((n, HEAD_DIM), lambda h, i: (0, N_HEADS + h))],
        out_specs=[pl.BlockSpec((bq, HEAD_DIM), lambda h, i: (i, h)),
                   pl.BlockSpec((None, bq, 128), lambda h, i: (h, i, 0))],
        out_shape=[jax.ShapeDtypeStruct((n, D_MODEL), BF16), jax.ShapeDtypeStruct((N_HEADS, n, 128), F32)],
        compiler_params=_params("parallel", "arbitrary"),
    )(q, kv, kv)


def _sb_attention_bwd(q, kv, do, ltot, name, bq=SB_BQ, bk=SB_BK):
    n = q.shape[0]
    bq = min(bq, n)
    per = bq // bk

    def body(q_ref, k_ref, v_ref, do_ref, l_ref, dq_ref, dk_ref, dv_ref):
        i = pl.program_id(1)

        @pl.when(i == 0)
        def _():
            dk_ref[...] = jnp.zeros_like(dk_ref)
            dv_ref[...] = jnp.zeros_like(dv_ref)

        qb = q_ref[...]
        dob = do_ref[...]
        l_tot = jnp.max(l_ref[...], axis=1, keepdims=True)
        upto = _tri(bk, lambda r, c: r <= c)
        below = _tri(bk, lambda r, c: r < c)

        def step(j, carry):
            l_sum, g_sum, dq = carry
            start = pl.multiple_of(j * bk, bk)
            k_blk = k_ref[pl.ds(start, bk), :]
            v_blk = v_ref[pl.ds(start, bk), :]
            z, e, before, log_beta, log_1m = _sb_logits(qb, k_blk, i * bq, j * bk)
            tail = l_tot - (l_sum + _split_dot(log_1m, upto))
            a = jnp.where(before, jnp.exp(log_beta + tail), 0.0)
            da = lax.dot_general(dob, v_blk, _dims(False, True), preferred_element_type=F32)
            g = a * da
            p = g_sum + _split_dot(g, below)
            inv = 1.0 / (1.0 + e)
            beta = jnp.where(z >= 0, inv, e * inv)
            dz = jnp.where(before, g * (1.0 - beta) - p * beta, 0.0) * SB_SCALE
            dzb = dz.astype(BF16)
            dq = dq + jnp.dot(dzb, k_blk, preferred_element_type=F32)
            dk_ref[pl.ds(start, bk), :] += lax.dot_general(dzb, qb, _dims(True, False), preferred_element_type=F32)
            dv_ref[pl.ds(start, bk), :] += lax.dot_general(a.astype(BF16), dob, _dims(True, False),
                                                           preferred_element_type=F32)
            return (l_sum + jnp.sum(log_1m, axis=1, keepdims=True), g_sum + jnp.sum(g, axis=1, keepdims=True), dq)

        zero = jnp.zeros((bq, 1), F32)
        _, _, dq = lax.fori_loop(0, (i + 1) * per, step, (zero, zero, jnp.zeros((bq, HEAD_DIM), F32)))
        dq_ref[...] = dq.astype(BF16)

    return pl.pallas_call(
        body, name=name, grid=(N_HEADS, n // bq),
        in_specs=[pl.BlockSpec((bq, HEAD_DIM), lambda h, i: (i, h)),
                  pl.BlockSpec((n, HEAD_DIM), lambda h, i: (0, h)),
                  pl.BlockSpec((n, HEAD_DIM), lambda h, i: (0, N_HEADS + h)),
                  pl.BlockSpec((bq, HEAD_DIM), lambda h, i: (i, h)),
                  pl.BlockSpec((None, bq, 128), lambda h, i: (h, i, 0))],
        out_specs=[pl.BlockSpec((bq, HEAD_DIM), lambda h, i: (i, h)),
                   pl.BlockSpec((n, HEAD_DIM), lambda h, i: (0, h)),
                   pl.BlockSpec((n, HEAD_DIM), lambda h, i: (0, h))],
        out_shape=[jax.ShapeDtypeStruct((n, D_MODEL), BF16), jax.ShapeDtypeStruct((n, D_MODEL), F32),
                   jax.ShapeDtypeStruct((n, D_MODEL), F32)],
        compiler_params=_params("parallel", "arbitrary"),
    )(q, kv, kv, do, ltot)


def _loss_head(y, target, name, tm=512):
    n, d = y.shape

    def body(y_ref, t_ref, dy_ref, loss_ref):
        err = y_ref[...] - t_ref[...]
        dy_ref[...] = err * (1.0 / d)
        part = 0.5 * jnp.sum(jnp.mean(err * err, axis=-1, keepdims=True), axis=0, keepdims=True)
        _acc_rows(loss_ref, pl.program_id(0) == 0, [jnp.broadcast_to(part, (1, 128))])

    row = lambda i: (i, 0)
    return pl.pallas_call(
        body, name=name, grid=(n // tm,),
        in_specs=[pl.BlockSpec((tm, d), row), pl.BlockSpec((tm, d), row)],
        out_specs=[pl.BlockSpec((tm, d), row), pl.BlockSpec((8, 128), lambda i: (0, 0))],
        out_shape=[jax.ShapeDtypeStruct((n, d), F32), jax.ShapeDtypeStruct((8, 128), F32)],
        compiler_params=_params("arbitrary"),
    )(y, target)


def _local_step(x, target, gains, w):
    n = x.shape[0]
    g = gains
    row = lambda a, i: a[i:i + 1, :]

    proj, h0 = _norm_matmul(x, row(g["mix_pre"], 0), w["gdn_in"], F32, "gdn_in_proj", tn=384)
    qn, kn, v, bg = _gdn_pre(proj, w["conv"], g["alog"], g["dtb"], "gdn_pre")
    o_gdn, saved = _gdn_scan(qn, kn, v, bg, "gdn_scan")
    x1, y_mix0, a_gdn = _out_proj((o_gdn, proj), True, x, w["gdn_out"], row(g["mix_post"], 0), g["out_gain"],
                                  "gdn_out_proj")
    x2, y_mlp0, h_mlp0, u0, a0 = _mlp_fwd(x1, row(g["mlp_pre"], 0), w["up"][0], w["down"][0], row(g["mlp_post"], 0),
                                          "mlp0")
    kv, h_kv = _norm_matmul(x2, g["kv"], w["kv"], BF16, "kv_proj")
    q, h_q = _norm_matmul(x2, row(g["mix_pre"], 1), w["sb_q"], BF16, "sb_q_proj")
    o_sb, ltot = _sb_attention(q, kv, "sb_attention")
    x3, y_mix1 = _out_proj((o_sb,), False, x2, w["sb_o"], row(g["mix_post"], 1), None, "sb_out_proj")
    x4, y_mlp1, h_mlp1, u1, a1 = _mlp_fwd(x3, row(g["mlp_pre"], 1), w["up"][1], w["down"][1], row(g["mlp_post"], 1),
                                          "mlp1")
    dx4, loss = _loss_head(x4, target, "loss_head")

    dx3, dy_mlp1, du1, dg_mlp1 = _mlp_bwd(dx4, y_mlp1, row(g["mlp_post"], 1), w["down"][1], u1, w["up"][1], x3,
                                          row(g["mlp_pre"], 1), "mlp1_bwd")
    d_down1 = _matmul_tn(a1, dy_mlp1, 1, "d_down1")
    d_up1 = _matmul_tn(h_mlp1, du1, N_DEV, "d_up1")
    dy_mix1, dg_post1, do_sb = _out_proj_bwd(dx3, y_mix1, w["sb_o"], row(g["mix_post"], 1), None, "sb_out_proj_bwd")
    d_sb_o = _matmul_tn(o_sb, dy_mix1, 1, "d_sb_o")
    dq, dk, dv = _sb_attention_bwd(q, kv, do_sb, ltot, "sb_attention_bwd")
    d_sb_q = _matmul_tn(h_q, dq, 1, "d_sb_q")
    d_kv = jnp.concatenate([_matmul_tn(h_kv, dk, N_DEV // 2, "d_w_k"), _matmul_tn(h_kv, dv, N_DEV // 2, "d_w_v")], axis=0)
    t, dg_pre1 = _norm_matmul_bwd(dq, w["sb_q"], x2, row(g["mix_pre"], 1), dx3, "sb_q_proj_bwd")
    t, dg_kv_k = _norm_matmul_bwd(dk, w["kv"][:N_DEV // 2], x2, g["kv"], t, "k_proj_bwd", tn=256)
    dx2, dg_kv_v = _norm_matmul_bwd(dv, w["kv"][N_DEV // 2:], x2, g["kv"], t, "v_proj_bwd", tn=256)

    dx1, dy_mlp0, du0, dg_mlp0 = _mlp_bwd(dx2, y_mlp0, row(g["mlp_post"], 0), w["down"][0], u0, w["up"][0], x1,
                                          row(g["mlp_pre"], 0), "mlp0_bwd")
    d_down0 = _matmul_tn(a0, dy_mlp0, 1, "d_down0")
    d_up0 = _matmul_tn(h_mlp0, du0, N_DEV, "d_up0")
    dy_mix0, dg_post0, do_gdn, dgate, d_out_gain = _out_proj_bwd(
        dx1, y_mix0, w["gdn_out"], row(g["mix_post"], 0), (o_gdn, proj, g["out_gain"]), "gdn_out_proj_bwd")
    d_gdn_out = _matmul_tn(a_gdn, dy_mix0, 1, "d_gdn_out")
    dqn, dkn, dvv, dbg = _gdn_scan_bwd(qn, kn, v, bg, saved, do_gdn, "gdn_scan_bwd")
    dproj, d_conv, d_gates = _gdn_pre_bwd(proj, w["conv"], g["alog"], g["dtb"], dqn, dkn, dvv, dbg, dgate, "gdn_pre_bwd")
    d_gdn_in = _matmul_tn(h0, dproj, 1, "d_gdn_in", tb=384)
    grad_x, dg_pre0 = _norm_matmul_bwd(dproj, w["gdn_in"], x, row(g["mix_pre"], 0), dx1, "gdn_in_proj_bwd", tn=384)

    big = {"up": (d_up0, d_up1), "down": (d_down0[0], d_down1[0]), "gdn_in": d_gdn_in[0], "gdn_out": d_gdn_out[0],
           "kv": d_kv, "sb_q": d_sb_q[0], "sb_o": d_sb_o[0]}
    small = {"pre0": dg_pre0, "pre1": dg_pre1, "post0": dg_post0, "post1": dg_post1, "mlp0": dg_mlp0, "mlp1": dg_mlp1,
             "kv_k": dg_kv_k, "kv_v": dg_kv_v, "gates": d_gates, "out_gain": d_out_gain, "conv": d_conv, "loss": loss}
    return grad_x, big, small


def _my_place():
    return lax.axis_index("x"), lax.axis_index("y"), lax.axis_index("c")


def _exchange(arrays, gather, name):
    n_arr = len(arrays)

    def body(*refs):
        ins, outs = refs[:n_arr], refs[n_arr:2 * n_arr]
        send_sems, recv_sems, local_sems = refs[2 * n_arr:]
        x, y, c = _my_place()
        me = 4 * x + 2 * y + c
        local = []
        for a in range(n_arr):
            src = ins[a] if gather[a] else ins[a].at[me]
            local.append(pltpu.make_async_copy(src, outs[a].at[me], local_sems.at[a]))
            local[-1].start()
        remote = []
        for k in range(1, N_DEV):
            px = 1 - x if k & 4 else x
            py = 1 - y if k & 2 else y
            pc = 1 - c if k & 1 else c
            peer = 4 * px + 2 * py + pc
            for a in range(n_arr):
                src = ins[a] if gather[a] else ins[a].at[peer]
                remote.append(pltpu.make_async_remote_copy(
                    src_ref=src, dst_ref=outs[a].at[me], send_sem=send_sems.at[a, k - 1], recv_sem=recv_sems.at[a, k - 1],
                    device_id=(px, py, pc), device_id_type=MESH))
                remote[-1].start()
        for cp in remote:
            cp.wait()
        for cp in local:
            cp.wait()

    def out_struct(a, is_gather):
        shape = (N_DEV,) + a.shape if is_gather else a.shape
        return jax.ShapeDtypeStruct(shape, a.dtype)

    any_spec = pl.BlockSpec(memory_space=pl.ANY)
    return pl.pallas_call(
        body, name=name,
        in_specs=[any_spec] * n_arr, out_specs=[any_spec] * n_arr,
        out_shape=[out_struct(a, gth) for a, gth in zip(arrays, gather)],
        scratch_shapes=[pltpu.SemaphoreType.DMA((n_arr, N_DEV - 1)), pltpu.SemaphoreType.DMA((n_arr, N_DEV - 1)),
                        pltpu.SemaphoreType.DMA((n_arr,))],
    )(*arrays)


def _adamw_math(g, w, m, v):
    m = ADAM_B1 * m + (1.0 - ADAM_B1) * g
    v = ADAM_B2 * v + (1.0 - ADAM_B2) * jnp.square(g)
    m_hat = m / (1.0 - ADAM_B1 ** ADAM_STEP)
    v_hat = v / (1.0 - ADAM_B2 ** ADAM_STEP)
    delta = -ADAM_LR * (m_hat / (jnp.sqrt(v_hat) + ADAM_EPS) + ADAM_WD * w)
    return delta, m, v


def _sum_devices(ref):
    total = ref[0].astype(F32)
    for d in range(1, N_DEV):
        total = total + ref[d].astype(F32)
    return total


def _reduce_adamw(landed, w, m, v, name, tr=256):
    r, c = w.shape
    tr = min(tr, r)
    assert r % tr == 0

    def body(l_ref, w_ref, m_ref, v_ref, g_ref, d_ref, nm_ref, nv_ref):
        g = _sum_devices(l_ref)
        g_ref[...] = g
        d_ref[...], nm_ref[...], nv_ref[...] = _adamw_math(g, w_ref[...], m_ref[...], v_ref[...])

    blk = pl.BlockSpec((tr, c), lambda i: (i, 0))
    return pl.pallas_call(
        body, name=name, grid=(r // tr,),
        in_specs=[pl.BlockSpec((N_DEV, tr, c), lambda i: (0, i, 0)), blk, blk, blk],
        out_specs=[blk] * 4, out_shape=[jax.ShapeDtypeStruct((r, c), F32)] * 4,
        compiler_params=_params("parallel"),
    )(landed, w, m, v)


def _adamw(g, w, m, v, name):
    def body(g_ref, w_ref, m_ref, v_ref, d_ref, nm_ref, nv_ref):
        d_ref[...], nm_ref[...], nv_ref[...] = _adamw_math(g_ref[...], w_ref[...], m_ref[...], v_ref[...])

    return pl.pallas_call(body, name=name, out_shape=[jax.ShapeDtypeStruct(w.shape, F32)] * 3)(g, w, m, v)


SMALL_ROWS = ("mix_pre", "mix_post", "mlp_pre", "mlp_post")


def _small_update(landed, params, name):
    layout = {
        "mix_pre": [("pre0", 0), ("pre1", 0)], "mix_post": [("post0", 0), ("post1", 0)],
        "mlp_pre": [("mlp0", 0), ("mlp1", 0)], "mlp_post": [("mlp0", 1), ("mlp1", 1)],
        "kv": [("kv_k", 0)], "alog": [("gates", 0)], "dtb": [("gates", 1)], "out_gain": [("out_gain", 0)],
    }
    landed_names = sorted(landed)
    param_names = sorted(params)
    n_l, n_p = len(landed_names), len(param_names)

    def body(*refs):
        l_refs = dict(zip(landed_names, refs[:n_l]))
        p_refs = {p: refs[n_l + 3 * i:n_l + 3 * i + 3] for i, p in enumerate(param_names)}
        outs = refs[n_l + 3 * n_p:]
        o_refs = {p: outs[4 * i:4 * i + 4] for i, p in enumerate(param_names)}
        conv_ref, loss_ref = outs[4 * n_p:]
        sums = {nm: _sum_devices(l_refs[nm]) for nm in landed_names}
        sums["kv_k"] = sums["kv_k"] + sums["kv_v"]
        for p in param_names:
            w_ref, m_ref, v_ref = p_refs[p]
            g_ref, d_ref, nm_ref, nv_ref = o_refs[p]
            for r, (src, src_row) in enumerate(layout[p]):
                g = sums[src][src_row:src_row + 1, :]
                g_ref[r:r + 1, :] = g
                d, nm, nv = _adamw_math(g, w_ref[r:r + 1, :], m_ref[r:r + 1, :], v_ref[r:r + 1, :])
                d_ref[r:r + 1, :] = d
                nm_ref[r:r + 1, :] = nm
                nv_ref[r:r + 1, :] = nv
        conv_ref[...] = sums["conv"]
        loss_ref[...] = sums["loss"]

    args = [landed[nm] for nm in landed_names]
    out_shape = []
    for p in param_names:
        args += list(params[p])
        out_shape += [jax.ShapeDtypeStruct(params[p][0].shape, F32)] * 4
    out_shape += [jax.ShapeDtypeStruct(landed["conv"].shape[1:], F32), jax.ShapeDtypeStruct(landed["loss"].shape[1:], F32)]
    outs = pl.pallas_call(body, name=name, out_shape=out_shape)(*args)
    result = {p: tuple(outs[4 * i:4 * i + 4]) for i, p in enumerate(param_names)}
    result["conv"], result["loss"] = outs[4 * n_p], outs[4 * n_p + 1]
    return result


def _lanes(vec, offset):
    return jnp.pad(vec[None, :], ((0, 0), (offset, 128 - offset - vec.shape[0])))


def kernel(x, mix_pre_gain, mix_post_gain, mlp_pre_gain, mlp_post_gain, mlp_w_up, mlp_w_down, gdn_w_in, gdn_conv_w, gdn_a_log, gdn_dt_bias, gdn_out_gain, gdn_w_out, kv_gain, w_kv, sb_w_q, sb_w_o, loss_target, m_mix_pre_gain, m_mix_post_gain, m_mlp_pre_gain, m_mlp_post_gain, m_mlp_w_up, m_mlp_w_down, m_gdn_w_in, m_gdn_conv_w, m_gdn_a_log, m_gdn_dt_bias, m_gdn_out_gain, m_gdn_w_out, m_kv_gain, m_w_kv, m_sb_w_q, m_sb_w_o, v_mix_pre_gain, v_mix_post_gain, v_mlp_pre_gain, v_mlp_post_gain, v_mlp_w_up, v_mlp_w_down, v_gdn_w_in, v_gdn_conv_w, v_gdn_a_log, v_gdn_dt_bias, v_gdn_out_gain, v_gdn_w_out, v_kv_gain, v_w_kv, v_sb_w_q, v_sb_w_o):
    d = D_MODEL
    me = 4 * lax.axis_index("x") + 2 * lax.axis_index("y") + lax.axis_index("c")
    bf = lambda a: a.astype(BF16)

    shards = [bf(mlp_w_up[0]), bf(mlp_w_up[1]), bf(mlp_w_down[0]), bf(mlp_w_down[1]), bf(gdn_w_in[0]), bf(gdn_w_out[0]),
              bf(w_kv), bf(sb_w_q[0]), bf(sb_w_o[0]), gdn_conv_w[0]]
    up0, up1, down0, down1, gdn_in, gdn_out, kv_w, sb_q, sb_o, conv = _exchange(shards, [True] * len(shards), "gather_weights")
    gdn_in = jnp.pad(gdn_in.transpose(1, 0, 2).reshape(d, GDN_IN_COLS), ((0, 0), (0, GDN_IN_PAD - GDN_IN_COLS)))
    weights = {
        "up": (up0, up1), "down": (down0.reshape(D_FF, d), down1.reshape(D_FF, d)), "gdn_in": gdn_in[None],
        "conv": conv.transpose(1, 0, 2).reshape(CONV_WIDTH, QKV), "gdn_out": gdn_out.reshape(d, d), "kv": kv_w,
        "sb_q": sb_q.reshape(1, d, d), "sb_o": sb_o.reshape(d, d),
    }
    gains = {"mix_pre": mix_pre_gain, "mix_post": mix_post_gain, "mlp_pre": mlp_pre_gain, "mlp_post": mlp_post_gain,
             "kv": kv_gain[None, :], "alog": _lanes(gdn_a_log[0], N_HEADS), "dtb": _lanes(gdn_dt_bias[0], N_HEADS),
             "out_gain": gdn_out_gain}

    grad_x, big, small = _local_step(x[0], loss_target[0], gains, weights)

    d_gdn_in = big["gdn_in"][:, :GDN_IN_COLS].reshape(d, N_DEV, GDN_IN_COLS // N_DEV).transpose(1, 0, 2)
    rows = lambda a: a.reshape(N_DEV, a.shape[0] // N_DEV, a.shape[1])
    parts = [bf(big["up"][0]), bf(big["up"][1]), bf(rows(big["down"][0])), bf(rows(big["down"][1])), bf(d_gdn_in),
             bf(rows(big["gdn_out"])), bf(big["kv"]), bf(rows(big["sb_q"])), bf(rows(big["sb_o"]))]
    small_names = ["pre0", "pre1", "post0", "post1", "mlp0", "mlp1", "kv_k", "kv_v", "gates", "out_gain", "conv", "loss"]
    small_parts = [small[nm] for nm in small_names]
    landed = _exchange(parts + small_parts, [False] * len(parts) + [True] * len(small_parts), "exchange_grads")
    big_landed, small_landed = landed[:len(parts)], dict(zip(small_names, landed[len(parts):]))

    results = {}
    big_params = [("mlp_w_up0", mlp_w_up[0], m_mlp_w_up[0], v_mlp_w_up[0]), ("mlp_w_up1", mlp_w_up[1], m_mlp_w_up[1], v_mlp_w_up[1]),
                  ("mlp_w_down0", mlp_w_down[0], m_mlp_w_down[0], v_mlp_w_down[0]),
                  ("mlp_w_down1", mlp_w_down[1], m_mlp_w_down[1], v_mlp_w_down[1]),
                  ("gdn_w_in", gdn_w_in[0], m_gdn_w_in[0], v_gdn_w_in[0]), ("gdn_w_out", gdn_w_out[0], m_gdn_w_out[0], v_gdn_w_out[0]),
                  ("w_kv", w_kv, m_w_kv, v_w_kv), ("sb_w_q", sb_w_q[0], m_sb_w_q[0], v_sb_w_q[0]),
                  ("sb_w_o", sb_w_o[0], m_sb_w_o[0], v_sb_w_o[0])]
    for (nm, w_, m_, v_), land in zip(big_params, big_landed):
        results[nm] = _reduce_adamw(land, w_, m_, v_, "adamw_" + nm)
    lanes8 = lambda a: _lanes(a[0], N_HEADS)
    small_params = {
        "mix_pre": (mix_pre_gain, m_mix_pre_gain, v_mix_pre_gain), "mix_post": (mix_post_gain, m_mix_post_gain, v_mix_post_gain),
        "mlp_pre": (mlp_pre_gain, m_mlp_pre_gain, v_mlp_pre_gain), "mlp_post": (mlp_post_gain, m_mlp_post_gain, v_mlp_post_gain),
        "kv": (kv_gain[None, :], m_kv_gain[None, :], v_kv_gain[None, :]),
        "alog": (lanes8(gdn_a_log), lanes8(m_gdn_a_log), lanes8(v_gdn_a_log)),
        "dtb": (lanes8(gdn_dt_bias), lanes8(m_gdn_dt_bias), lanes8(v_gdn_dt_bias)),
        "out_gain": (gdn_out_gain, m_gdn_out_gain, v_gdn_out_gain),
    }
    sm = _small_update(small_landed, small_params, "small_update")
    conv_cols = QKV // N_DEV
    g_conv = lax.dynamic_slice(sm["conv"], (0, me * conv_cols), (8, conv_cols))[:CONV_WIDTH]
    conv_res = (g_conv,) + tuple(_adamw(g_conv, gdn_conv_w[0], m_gdn_conv_w[0], v_gdn_conv_w[0], "adamw_conv"))

    stack2 = lambda a, b: tuple(jnp.stack([p, q]) for p, q in zip(results[a], results[b]))
    lead = lambda t: tuple(a[None] for a in t)
    heads8 = lambda t: tuple(a[:, N_HEADS:2 * N_HEADS] for a in t)
    per_weight = [
        sm["mix_pre"], sm["mix_post"], sm["mlp_pre"], sm["mlp_post"],
        stack2("mlp_w_up0", "mlp_w_up1"), stack2("mlp_w_down0", "mlp_w_down1"),
        lead(results["gdn_w_in"]), lead(conv_res), heads8(sm["alog"]), heads8(sm["dtb"]), sm["out_gain"],
        lead(results["gdn_w_out"]), tuple(a[0] for a in sm["kv"]), results["w_kv"], lead(results["sb_w_q"]), lead(results["sb_w_o"]),
    ]
    grads, deltas, new_ms, new_vs = zip(*per_weight)
    return (sm["loss"][0, 0], grad_x[None], *grads, *deltas, *new_ms, *new_vs)
```

```python
import functools
import math

import jax
import jax.numpy as jnp
from jax import lax
from jax.experimental import pallas as pl
from jax.experimental.pallas import tpu as pltpu

F32 = jnp.float32
BF16 = jnp.bfloat16

N_DEV = 8
D_MODEL = 1024
D_FF = 4096
N_HEADS = 8
HEAD_DIM = 128
CHUNK = 64
CONV_WIDTH = 4
GDN_IN_COLS = 4 * D_MODEL + 2 * N_HEADS
GDN_IN_PAD = 4 * D_MODEL + 128
EPS = 1e-6

ADAM_LR = 0.001
ADAM_B1 = 0.9
ADAM_B2 = 0.999
ADAM_EPS = 1e-08
ADAM_WD = 0.01
ADAM_STEP = 10

VMEM_LIMIT_BYTES = 56 * 1024 * 1024
MESH = pl.DeviceIdType.MESH


def _params(*semantics):
    return pltpu.CompilerParams(dimension_semantics=semantics, vmem_limit_bytes=VMEM_LIMIT_BYTES)


def _dims(ta, tb):
    return (((0,) if ta else (1,), (1,) if tb else (0,)), ((), ()))


def _dot(a, b, ta=False, tb=False):
    return lax.dot_general(a.astype(BF16), b.astype(BF16), _dims(ta, tb), preferred_element_type=F32)


def _dot_f32(a, b, ta=False, tb=False):
    return lax.dot_general(a, b, _dims(ta, tb), precision=lax.Precision.HIGHEST, preferred_element_type=F32)


def _make_mm(dot):
    @functools.partial(jax.custom_vjp, nondiff_argnums=(2, 3))
    def mm(a, b, ta, tb):
        return dot(a, b, ta, tb)

    def fwd(a, b, ta, tb):
        return dot(a, b, ta, tb), (a, b)

    def bwd(ta, tb, res, g):
        a, b = res
        if not ta and not tb:
            return mm(g, b, False, True), mm(a, g, True, False)
        if not ta and tb:
            return mm(g, b, False, False), mm(g, a, True, False)
        if ta and not tb:
            return mm(b, g, False, True), mm(a, g, False, False)
        raise NotImplementedError

    mm.defvjp(fwd, bwd)
    return mm


_mm = _make_mm(_dot)
_mm_f32 = _make_mm(_dot_f32)


def _split_dot(x, ones_bf16):
    hi = x.astype(BF16)
    lo = (x - hi.astype(F32)).astype(BF16)
    return (jnp.dot(hi, ones_bf16, preferred_element_type=F32)
            + jnp.dot(lo, ones_bf16, preferred_element_type=F32))


def _rms(x, gain):
    r = lax.rsqrt(jnp.mean(x * x, axis=-1, keepdims=True) + EPS)
    return x * r * gain


def _rms_bwd(x, gain, dy):
    r = lax.rsqrt(jnp.mean(x * x, axis=-1, keepdims=True) + EPS)
    xh = x * r
    dgain = jnp.sum(dy * xh, axis=0, keepdims=True)
    dxh = dy * gain
    dx = r * (dxh - xh * jnp.mean(dxh * xh, axis=-1, keepdims=True))
    return dx, dgain


def _silu(x):
    return x / (1.0 + jnp.exp(-x))


def _acc_rows(ref, first, rows):
    @pl.when(first)
    def _():
        ref[...] = jnp.zeros_like(ref)

    for r, val in enumerate(rows):
        ref[r:r + 1, :] += val


def _norm_matmul(x, gain, w, out_dtype, name, tm=512, tn=512):
    n, d = x.shape
    g, _, wc = w.shape
    tn = min(tn, wc)
    per = wc // tn
    assert n % tm == 0 and wc % tn == 0

    def body(x_ref, gain_ref, w_ref, out_ref, h_ref):
        @pl.when(pl.program_id(1) == 0)
        def _():
            h_ref[...] = _rms(x_ref[...], gain_ref[...]).astype(BF16)

        out_ref[...] = jnp.dot(h_ref[...], w_ref[...], preferred_element_type=F32).astype(out_dtype)

    return pl.pallas_call(
        body, name=name, grid=(n // tm, g * per),
        in_specs=[pl.BlockSpec((tm, d), lambda i, j: (i, 0)),
                  pl.BlockSpec((1, d), lambda i, j: (0, 0)),
                  pl.BlockSpec((None, d, tn), lambda i, j: (j // per, 0, j % per))],
        out_specs=[pl.BlockSpec((tm, tn), lambda i, j: (i, j)),
                   pl.BlockSpec((tm, d), lambda i, j: (i, 0))],
        out_shape=[jax.ShapeDtypeStruct((n, g * wc), out_dtype), jax.ShapeDtypeStruct((n, d), BF16)],
        compiler_params=_params("parallel", "arbitrary"),
    )(x, gain, w)


def _norm_matmul_bwd(dout, w, x, gain, add, name, tm=512, tn=512):
    n, d = x.shape
    g, _, wc = w.shape
    tn = min(tn, wc)
    per = wc // tn
    ncol = g * per

    def body(dout_ref, w_ref, x_ref, gain_ref, add_ref, dx_ref, dgain_ref, acc_ref):
        i, j = pl.program_id(0), pl.program_id(1)

        @pl.when(j == 0)
        def _():
            acc_ref[...] = jnp.zeros_like(acc_ref)

        acc_ref[...] += _dot(dout_ref[...], w_ref[...], tb=True)

        @pl.when(j == ncol - 1)
        def _():
            dx, dgain = _rms_bwd(x_ref[...], gain_ref[...], acc_ref[...])
            dx_ref[...] = add_ref[...] + dx
            _acc_rows(dgain_ref, i == 0, [dgain])

    return pl.pallas_call(
        body, name=name, grid=(n // tm, ncol),
        in_specs=[pl.BlockSpec((tm, tn), lambda i, j: (i, j)),
                  pl.BlockSpec((None, d, tn), lambda i, j: (j // per, 0, j % per)),
                  pl.BlockSpec((tm, d), lambda i, j: (i, 0)),
                  pl.BlockSpec((1, d), lambda i, j: (0, 0)),
                  pl.BlockSpec((tm, d), lambda i, j: (i, 0))],
        out_specs=[pl.BlockSpec((tm, d), lambda i, j: (i, 0)),
                   pl.BlockSpec((8, d), lambda i, j: (0, 0))],
        out_shape=[jax.ShapeDtypeStruct((n, d), F32), jax.ShapeDtypeStruct((8, d), F32)],
        scratch_shapes=[pltpu.VMEM((tm, d), F32)],
        compiler_params=_params("arbitrary", "arbitrary"),
    )(dout, w, x, gain, add)


def _matmul_tn(a, b, groups, name, ta=512, tb=512, tk=1024):
    n, ka = a.shape
    _, kb = b.shape
    wc = kb // groups
    ta, tb, tk = min(ta, ka), min(tb, wc), min(tk, n)
    per = wc // tb
    nk = n // tk
    assert ka % ta == 0 and wc % tb == 0 and n % tk == 0

    def body(a_ref, b_ref, out_ref):
        @pl.when(pl.program_id(2) == 0)
        def _():
            out_ref[...] = jnp.zeros_like(out_ref)

        out_ref[...] += _dot(a_ref[...], b_ref[...], ta=True)

    return pl.pallas_call(
        body, name=name, grid=(ka // ta, groups * per, nk),
        in_specs=[pl.BlockSpec((tk, ta), lambda i, j, k: (k, i)),
                  pl.BlockSpec((tk, tb), lambda i, j, k: (k, j))],
        out_specs=pl.BlockSpec((None, ta, tb), lambda i, j, k: (j // per, i, j % per)),
        out_shape=jax.ShapeDtypeStruct((groups, ka, wc), F32),
        compiler_params=_params("parallel", "parallel", "arbitrary"),
    )(a, b)


def _gated_head_norm(o, gate, out_gain):
    parts = []
    for h in range(N_HEADS):
        sl = slice(h * HEAD_DIM, (h + 1) * HEAD_DIM)
        parts.append(_rms(o[:, sl], out_gain) * _silu(gate[:, sl]))
    return parts


def _out_proj(a_inputs, gated, x_in, w, gain, out_gain, name, tm=512):
    n, d = x_in.shape
    k = w.shape[0]

    def body(*refs):
        if gated:
            o_ref, gate_ref, og_ref, x_ref, w_ref, gain_ref, xo_ref, y_ref, a_ref = refs
            parts = _gated_head_norm(o_ref[...], gate_ref[...], og_ref[...])
            for h, part in enumerate(parts):
                a_ref[:, h * HEAD_DIM:(h + 1) * HEAD_DIM] = part.astype(BF16)
            a = a_ref[...]
        else:
            a_in_ref, x_ref, w_ref, gain_ref, xo_ref, y_ref = refs
            a = a_in_ref[...]
        y = jnp.dot(a, w_ref[...], preferred_element_type=F32)
        y_ref[...] = y
        xo_ref[...] = x_ref[...] + _rms(y, gain_ref[...])

    row = lambda i: (i, 0)
    const = lambda i: (0, 0)
    if gated:
        a_specs = [pl.BlockSpec((tm, k), row), pl.BlockSpec((tm, D_MODEL), lambda i: (i, 3)),
                   pl.BlockSpec((1, HEAD_DIM), const)]
        a_args = list(a_inputs) + [out_gain]
    else:
        a_specs = [pl.BlockSpec((tm, k), row)]
        a_args = list(a_inputs)
    out_specs = [pl.BlockSpec((tm, d), row), pl.BlockSpec((tm, d), row)]
    out_shape = [jax.ShapeDtypeStruct((n, d), F32), jax.ShapeDtypeStruct((n, d), F32)]
    if gated:
        out_specs.append(pl.BlockSpec((tm, k), row))
        out_shape.append(jax.ShapeDtypeStruct((n, k), BF16))
    return pl.pallas_call(
        body, name=name, grid=(n // tm,),
        in_specs=a_specs + [pl.BlockSpec((tm, d), row), pl.BlockSpec((k, d), const), pl.BlockSpec((1, d), const)],
        out_specs=out_specs, out_shape=out_shape,
        compiler_params=_params("parallel"),
    )(*a_args, x_in, w, gain)


def _out_proj_bwd(dxo, y, w, gain, gated_inputs, name, tm=512):
    n, d = dxo.shape
    k = w.shape[0]
    gated = gated_inputs is not None

    def body(*refs):
        if gated:
            (dxo_ref, y_ref, w_ref, gain_ref, o_ref, gate_ref, og_ref,
             dy_ref, dgain_ref, do_ref, dgate_ref, dog_ref) = refs
        else:
            dxo_ref, y_ref, w_ref, gain_ref, dy_ref, dgain_ref, da_ref = refs
        first = pl.program_id(0) == 0
        dy, dgain = _rms_bwd(y_ref[...], gain_ref[...], dxo_ref[...])
        dy_ref[...] = dy.astype(BF16)
        _acc_rows(dgain_ref, first, [dgain])
        da = _dot(dy_ref[...], w_ref[...], tb=True)
        if not gated:
            da_ref[...] = da.astype(BF16)
            return
        og = og_ref[...]
        dog = jnp.zeros_like(og)
        for h in range(N_HEADS):
            sl = slice(h * HEAD_DIM, (h + 1) * HEAD_DIM)
            fn = lambda o_h, g_h, gn: _rms(o_h, gn) * _silu(g_h)
            _, vjp = jax.vjp(fn, o_ref[:, sl], gate_ref[:, sl], og)
            do_h, dgate_h, dog_h = vjp(da[:, sl])
            do_ref[:, sl] = do_h
            dgate_ref[:, sl] = dgate_h
            dog = dog + dog_h
        _acc_rows(dog_ref, first, [dog])

    row = lambda i: (i, 0)
    const = lambda i: (0, 0)
    in_specs = [pl.BlockSpec((tm, d), row), pl.BlockSpec((tm, d), row), pl.BlockSpec((k, d), const),
                pl.BlockSpec((1, d), const)]
    args = [dxo, y, w, gain]
    out_specs = [pl.BlockSpec((tm, d), row), pl.BlockSpec((8, d), const)]
    out_shape = [jax.ShapeDtypeStruct((n, d), BF16), jax.ShapeDtypeStruct((8, d), F32)]
    if gated:
        in_specs += [pl.BlockSpec((tm, k), row), pl.BlockSpec((tm, D_MODEL), lambda i: (i, 3)),
                     pl.BlockSpec((1, HEAD_DIM), const)]
        args += list(gated_inputs)
        out_specs += [pl.BlockSpec((tm, k), row), pl.BlockSpec((tm, k), row), pl.BlockSpec((8, HEAD_DIM), const)]
        out_shape += [jax.ShapeDtypeStruct((n, k), F32), jax.ShapeDtypeStruct((n, k), F32),
                      jax.ShapeDtypeStruct((8, HEAD_DIM), F32)]
    else:
        out_specs.append(pl.BlockSpec((tm, k), row))
        out_shape.append(jax.ShapeDtypeStruct((n, k), BF16))
    return pl.pallas_call(
        body, name=name, grid=(n // tm,), in_specs=in_specs, out_specs=out_specs, out_shape=out_shape,
        compiler_params=_params("arbitrary"),
    )(*args)


def _mlp_fwd(x_in, g_pre, w_up, w_down, g_post, name, tm=512):
    n, d = x_in.shape
    g, _, wc = w_up.shape

    def body(x_ref, gpre_ref, wup_ref, wdown_ref, gpost_ref, xo_ref, y_ref, h_ref, u_ref, a_ref, acc_ref):
        j = pl.program_id(1)

        @pl.when(j == 0)
        def _():
            h_ref[...] = _rms(x_ref[...], gpre_ref[...]).astype(BF16)
            acc_ref[...] = jnp.zeros_like(acc_ref)

        u = jnp.dot(h_ref[...], wup_ref[...], preferred_element_type=F32).astype(BF16)
        u_ref[...] = u
        a = jnp.square(jnp.maximum(u, 0))
        a_ref[...] = a
        acc_ref[...] += jnp.dot(a, wdown_ref[...], preferred_element_type=F32)

        @pl.when(j == g - 1)
        def _():
            y = acc_ref[...]
            y_ref[...] = y
            xo_ref[...] = x_ref[...] + _rms(y, gpost_ref[...])

    row = lambda i, j: (i, 0)
    const = lambda i, j: (0, 0)
    return pl.pallas_call(
        body, name=name, grid=(n // tm, g),
        in_specs=[pl.BlockSpec((tm, d), row), pl.BlockSpec((1, d), const),
                  pl.BlockSpec((None, d, wc), lambda i, j: (j, 0, 0)),
                  pl.BlockSpec((wc, d), lambda i, j: (j, 0)), pl.BlockSpec((1, d), const)],
        out_specs=[pl.BlockSpec((tm, d), row), pl.BlockSpec((tm, d), row), pl.BlockSpec((tm, d), row),
                   pl.BlockSpec((tm, wc), lambda i, j: (i, j)), pl.BlockSpec((tm, wc), lambda i, j: (i, j))],
        out_shape=[jax.ShapeDtypeStruct((n, d), F32), jax.ShapeDtypeStruct((n, d), F32),
                   jax.ShapeDtypeStruct((n, d), BF16), jax.ShapeDtypeStruct((n, g * wc), BF16),
                   jax.ShapeDtypeStruct((n, g * wc), BF16)],
        scratch_shapes=[pltpu.VMEM((tm, d), F32)],
        compiler_params=_params("parallel", "arbitrary"),
    )(x_in, g_pre, w_up, w_down, g_post)


def _mlp_bwd(dxo, y, g_post, w_down, u, w_up, x_in, g_pre, name, tm=512):
    n, d = dxo.shape
    g, _, wc = w_up.shape

    def body(dxo_ref, y_ref, gpost_ref, wdown_ref, u_ref, wup_ref, x_ref, gpre_ref,
             dx_ref, dy_ref, du_ref, dgain_ref, acc_ref, dgpost_ref):
        i, j = pl.program_id(0), pl.program_id(1)

        @pl.when(j == 0)
        def _():
            dy, dgpost = _rms_bwd(y_ref[...], gpost_ref[...], dxo_ref[...])
            dy_ref[...] = dy.astype(BF16)
            dgpost_ref[...] = dgpost
            acc_ref[...] = jnp.zeros_like(acc_ref)

        da = _dot(dy_ref[...], wdown_ref[...], tb=True)
        du = (da * (2.0 * jnp.maximum(u_ref[...], 0).astype(F32))).astype(BF16)
        du_ref[...] = du
        acc_ref[...] += _dot(du, wup_ref[...], tb=True)

        @pl.when(j == g - 1)
        def _():
            dx, dgpre = _rms_bwd(x_ref[...], gpre_ref[...], acc_ref[...])
            dx_ref[...] = dxo_ref[...] + dx
            _acc_rows(dgain_ref, i == 0, [dgpre, dgpost_ref[...]])

    row = lambda i, j: (i, 0)
    const = lambda i, j: (0, 0)
    return pl.pallas_call(
        body, name=name, grid=(n // tm, g),
        in_specs=[pl.BlockSpec((tm, d), row), pl.BlockSpec((tm, d), row), pl.BlockSpec((1, d), const),
                  pl.BlockSpec((wc, d), lambda i, j: (j, 0)), pl.BlockSpec((tm, wc), lambda i, j: (i, j)),
                  pl.BlockSpec((None, d, wc), lambda i, j: (j, 0, 0)), pl.BlockSpec((tm, d), row),
                  pl.BlockSpec((1, d), const)],
        out_specs=[pl.BlockSpec((tm, d), row), pl.BlockSpec((tm, d), row),
                   pl.BlockSpec((tm, wc), lambda i, j: (i, j)), pl.BlockSpec((8, d), const)],
        out_shape=[jax.ShapeDtypeStruct((n, d), F32), jax.ShapeDtypeStruct((n, d), BF16),
                   jax.ShapeDtypeStruct((n, g * wc), BF16), jax.ShapeDtypeStruct((8, d), F32)],
        scratch_shapes=[pltpu.VMEM((tm, d), F32), pltpu.VMEM((1, d), F32)],
        compiler_params=_params("arbitrary", "arbitrary"),
    )(dxo, y, g_post, w_down, u, w_up, x_in, g_pre)


QKV = 3 * D_MODEL


def _shifted(x, prev8, s):
    if s == 0:
        return x
    tm = x.shape[0]
    rolled = pltpu.roll(x, s, 0)
    head = pltpu.roll(prev8, s, 0)
    head = jnp.concatenate([head, jnp.zeros((tm - 8, x.shape[1]), x.dtype)], axis=0)
    rows = lax.broadcasted_iota(jnp.int32, x.shape, 0)
    return jnp.where(rows < s, head, rolled)


def _conv(x, prev8, conv_w):
    out = x * conv_w[3:4, :]
    for s in range(1, CONV_WIDTH):
        out = out + _shifted(x, prev8, s) * conv_w[3 - s:4 - s, :]
    return out


def _l2norm(x):
    return x * lax.rsqrt(jnp.sum(x * x, axis=-1, keepdims=True) + EPS)


def _gdn_act_head(cq, ck, cv):
    return _l2norm(_silu(cq)) * (HEAD_DIM ** -0.5), _l2norm(_silu(ck)), _silu(cv)


def _gdn_gates(ba, alog, dtb):
    lane = lax.broadcasted_iota(jnp.int32, ba.shape, 1)
    beta = 1.0 / (1.0 + jnp.exp(-ba))
    t = ba + dtb
    softplus = jnp.maximum(t, 0.0) + jnp.log(1.0 + jnp.exp(-jnp.abs(t)))
    g = -jnp.exp(alog) * softplus
    return jnp.where(lane < N_HEADS, beta, jnp.where(lane < 2 * N_HEADS, g, 0.0))


def _qkv_cols(part, h):
    start = part * D_MODEL + h * HEAD_DIM
    return slice(start, start + HEAD_DIM)


def _prev_rows_spec(tm, cols, colblock, order):
    per = tm // 8
    return pl.BlockSpec((8, cols), lambda i: (jnp.maximum(order(i) * per - 1, 0), colblock))


def _gdn_pre(proj, conv_w, alog, dtb, name, tm=256):
    n = proj.shape[0]

    def body(x_ref, prev_ref, ba_ref, cw_ref, alog_ref, dtb_ref, q_ref, k_ref, v_ref, bg_ref):
        first = pl.program_id(0) == 0
        for h in range(N_HEADS):
            convs = []
            for part in range(3):
                sl = _qkv_cols(part, h)
                prev8 = jnp.where(first, 0.0, prev_ref[:, sl])
                convs.append(_conv(x_ref[:, sl], prev8, cw_ref[:, sl]))
            out = slice(h * HEAD_DIM, (h + 1) * HEAD_DIM)
            q_ref[:, out], k_ref[:, out], v_ref[:, out] = _gdn_act_head(*convs)
        bg_ref[...] = _gdn_gates(ba_ref[...], alog_ref[...], dtb_ref[...])

    row = lambda i: (i, 0)
    const = lambda i: (0, 0)
    ident = lambda i: i
    return pl.pallas_call(
        body, name=name, grid=(n // tm,),
        in_specs=[pl.BlockSpec((tm, QKV), row), _prev_rows_spec(tm, QKV, 0, ident),
                  pl.BlockSpec((tm, 128), lambda i: (i, 4 * D_MODEL // 128)),
                  pl.BlockSpec((CONV_WIDTH, QKV), const), pl.BlockSpec((1, 128), const), pl.BlockSpec((1, 128), const)],
        out_specs=[pl.BlockSpec((tm, D_MODEL), row)] * 3 + [pl.BlockSpec((tm, 128), row)],
        out_shape=[jax.ShapeDtypeStruct((n, D_MODEL), F32)] * 3 + [jax.ShapeDtypeStruct((n, 128), F32)],
        compiler_params=_params("parallel"),
    )(proj, proj, proj, conv_w, alog, dtb)


def _gdn_pre_bwd(proj, conv_w, alog, dtb, dq, dk, dv, dbg, dgate, name, tm=256):
    n = proj.shape[0]
    nt = n // tm

    def body(x_ref, prev_ref, ba_ref, cw_ref, alog_ref, dtb_ref, dq_ref, dk_ref, dv_ref, dbg_ref, dgate_ref,
             dproj_ref, dcw_ref, dgates_ref, carry_ref):
        step = pl.program_id(0)
        tile = nt - 1 - step
        @pl.when(step == 0)
        def _():
            carry_ref[...] = jnp.zeros_like(carry_ref)
            dcw_ref[...] = jnp.zeros_like(dcw_ref)

        rows = lax.broadcasted_iota(jnp.int32, (tm, HEAD_DIM), 0)
        for h in range(N_HEADS):
            cols = [_qkv_cols(part, h) for part in range(3)]
            prevs = [jnp.where(tile == 0, 0.0, prev_ref[:, sl]) for sl in cols]
            convs = [_conv(x_ref[:, sl], prev8, cw_ref[:, sl]) for sl, prev8 in zip(cols, prevs)]
            _, vjp = jax.vjp(_gdn_act_head, *convs)
            out = slice(h * HEAD_DIM, (h + 1) * HEAD_DIM)
            dcs = vjp((dq_ref[:, out], dk_ref[:, out], dv_ref[:, out]))
            for sl, prev8, dc in zip(cols, prevs, dcs):
                x = x_ref[:, sl]
                cw = cw_ref[:, sl]
                dx = dc * cw[3:4, :]
                wrapped = jnp.zeros((8, HEAD_DIM), F32)
                dcw_ref[3:4, sl] += jnp.sum(dc * x, axis=0, keepdims=True)
                for s in range(1, CONV_WIDTH):
                    r = pltpu.roll(dc, tm - s, 0) * cw[3 - s:4 - s, :]
                    dx = dx + jnp.where(rows < tm - s, r, 0.0)
                    wrapped = wrapped + jnp.where(rows[tm - 8:, :] >= tm - s, r[tm - 8:, :], 0.0)
                    dcw_ref[3 - s:4 - s, sl] += jnp.sum(dc * _shifted(x, prev8, s), axis=0, keepdims=True)
                dproj_ref[:, sl] = dx
                dproj_ref[tm - 8:, sl] += carry_ref[:, sl]
                carry_ref[:, sl] = wrapped

        _, vjp = jax.vjp(_gdn_gates, ba_ref[...], alog_ref[...], dtb_ref[...])
        dba, dalog, ddtb = vjp(dbg_ref[...])
        dproj_ref[:, QKV:4 * D_MODEL] = dgate_ref[...]
        dproj_ref[:, 4 * D_MODEL:] = dba
        _acc_rows(dgates_ref, step == 0, [dalog, ddtb])

    rev = lambda i: nt - 1 - i
    row = lambda i: (rev(i), 0)
    const = lambda i: (0, 0)
    return pl.pallas_call(
        body, name=name, grid=(nt,),
        in_specs=[pl.BlockSpec((tm, QKV), row), _prev_rows_spec(tm, QKV, 0, rev),
                  pl.BlockSpec((tm, 128), lambda i: (rev(i), 4 * D_MODEL // 128)),
                  pl.BlockSpec((CONV_WIDTH, QKV), const), pl.BlockSpec((1, 128), const), pl.BlockSpec((1, 128), const),
                  pl.BlockSpec((tm, D_MODEL), row), pl.BlockSpec((tm, D_MODEL), row), pl.BlockSpec((tm, D_MODEL), row),
                  pl.BlockSpec((tm, 128), row), pl.BlockSpec((tm, D_MODEL), row)],
        out_specs=[pl.BlockSpec((tm, GDN_IN_PAD), row), pl.BlockSpec((8, QKV), const), pl.BlockSpec((8, 128), const)],
        out_shape=[jax.ShapeDtypeStruct((n, GDN_IN_PAD), F32), jax.ShapeDtypeStruct((8, QKV), F32),
                   jax.ShapeDtypeStruct((8, 128), F32)],
        scratch_shapes=[pltpu.VMEM((8, QKV), F32)],
        compiler_params=_params("arbitrary"),
    )(proj, proj, proj, conv_w, alog, dtb, dq, dk, dv, dbg, dgate)


@jax.custom_vjp
def _unit_lower_inverses(lowers):
    c = lowers[0].shape[0]
    eye = (lax.broadcasted_iota(jnp.int32, (c, c), 0) == lax.broadcasted_iota(jnp.int32, (c, c), 1)).astype(F32)
    ys = [-low for low in lowers]
    ps = [eye + y for y in ys]
    for _ in range(int(math.log2(c)) - 1):
        ys = [_dot_f32(y, y) for y in ys]
        ps = [p + _dot_f32(p, y) for p, y in zip(ps, ys)]
    return tuple(ps)


def _unit_lower_inverses_fwd(lowers):
    ts = _unit_lower_inverses(lowers)
    return ts, ts


def _unit_lower_inverses_bwd(ts, dts):
    left = [_dot_f32(t, dt, ta=True) for t, dt in zip(ts, dts)]
    return (tuple(-_dot_f32(l, t, tb=True) for l, t in zip(left, ts)),)


_unit_lower_inverses.defvjp(_unit_lower_inverses_fwd, _unit_lower_inverses_bwd)


def _gdn_chunk(qs, ks, vs, bg, states):
    c = CHUNK
    heads = range(N_HEADS)
    row = lax.broadcasted_iota(jnp.int32, (c, c), 0)
    col = lax.broadcasted_iota(jnp.int32, (c, c), 1)
    incl, strict, eye = row >= col, row > col, row == col
    lane = lax.broadcasted_iota(jnp.int32, (c, 128), 1)
    rowc = lax.broadcasted_iota(jnp.int32, (c, 1), 0)
    gc_all = _mm_f32(incl.astype(F32), bg, False, False)
    beta = [jnp.sum(jnp.where(lane == h, bg, 0.0), axis=1, keepdims=True) for h in heads]
    gc = [jnp.sum(jnp.where(lane == N_HEADS + h, gc_all, 0.0), axis=1, keepdims=True) for h in heads]
    gc_row = [jnp.sum(jnp.where(eye, gc[h], 0.0), axis=0, keepdims=True) for h in heads]
    gc_last = [jnp.sum(jnp.where(rowc == c - 1, gc[h], 0.0), axis=0, keepdims=True) for h in heads]
    decay = [jnp.where(incl, jnp.exp(jnp.where(incl, gc[h] - gc_row[h], 0.0)), 0.0) for h in heads]
    kb = [ks[h] * beta[h] for h in heads]
    lower = tuple(jnp.where(strict, _mm(kb[h], ks[h], False, True) * decay[h], 0.0) for h in heads)
    attn = [_mm(qs[h], ks[h], False, True) * decay[h] for h in heads]
    t_mat = _unit_lower_inverses(lower)
    egc = [jnp.exp(gc[h]) for h in heads]
    w = [_mm(t_mat[h], kb[h] * egc[h], False, False) for h in heads]
    u = [_mm(t_mat[h], vs[h] * beta[h], False, False) for h in heads]
    kg = [ks[h] * jnp.exp(gc_last[h] - gc[h]) for h in heads]
    v_new = [u[h] - _mm(w[h], states[h], False, False) for h in heads]
    from_state = [_mm(qs[h] * egc[h], states[h], False, False) for h in heads]
    outs = [from_state[h] + _mm(attn[h], v_new[h], False, False) for h in heads]
    new_states = [states[h] * jnp.exp(gc_last[h]) + _mm(kg[h], v_new[h], True, False) for h in heads]
    return tuple(outs), tuple(new_states)


def _head_slices(ref):
    return tuple(ref[:, h * HEAD_DIM:(h + 1) * HEAD_DIM] for h in range(N_HEADS))


def _gdn_scan(qn, kn, v, bg, name):
    n = qn.shape[0]
    nc = n // CHUNK

    def body(q_ref, k_ref, v_ref, bg_ref, o_ref, saved_ref, state_ref):
        @pl.when(pl.program_id(0) == 0)
        def _():
            state_ref[...] = jnp.zeros_like(state_ref)

        states = tuple(state_ref[h] for h in range(N_HEADS))
        for h in range(N_HEADS):
            saved_ref[h] = states[h]
        outs, new_states = _gdn_chunk(_head_slices(q_ref), _head_slices(k_ref), _head_slices(v_ref), bg_ref[...], states)
        for h in range(N_HEADS):
            o_ref[:, h * HEAD_DIM:(h + 1) * HEAD_DIM] = outs[h]
            state_ref[h] = new_states[h]

    row = lambda i: (i, 0)
    return pl.pallas_call(
        body, name=name, grid=(nc,),
        in_specs=[pl.BlockSpec((CHUNK, D_MODEL), row)] * 3 + [pl.BlockSpec((CHUNK, 128), row)],
        out_specs=[pl.BlockSpec((CHUNK, D_MODEL), row),
                   pl.BlockSpec((None, N_HEADS, HEAD_DIM, HEAD_DIM), lambda i: (i, 0, 0, 0))],
        out_shape=[jax.ShapeDtypeStruct((n, D_MODEL), F32),
                   jax.ShapeDtypeStruct((nc, N_HEADS, HEAD_DIM, HEAD_DIM), F32)],
        scratch_shapes=[pltpu.VMEM((N_HEADS, HEAD_DIM, HEAD_DIM), F32)],
        compiler_params=_params("arbitrary"),
    )(qn, kn, v, bg)


def _gdn_scan_bwd(qn, kn, v, bg, saved, do, name):
    n = qn.shape[0]
    nc = n // CHUNK

    def body(q_ref, k_ref, v_ref, bg_ref, saved_ref, do_ref, dq_ref, dk_ref, dv_ref, dbg_ref, dstate_ref):
        @pl.when(pl.program_id(0) == 0)
        def _():
            dstate_ref[...] = jnp.zeros_like(dstate_ref)

        states = tuple(saved_ref[h] for h in range(N_HEADS))
        _, vjp = jax.vjp(_gdn_chunk, _head_slices(q_ref), _head_slices(k_ref), _head_slices(v_ref), bg_ref[...], states)
        dstates = tuple(dstate_ref[h] for h in range(N_HEADS))
        dqs, dks, dvs, dbg, dprev = vjp((_head_slices(do_ref), dstates))
        for h in range(N_HEADS):
            sl = slice(h * HEAD_DIM, (h + 1) * HEAD_DIM)
            dq_ref[:, sl] = dqs[h]
            dk_ref[:, sl] = dks[h]
            dv_ref[:, sl] = dvs[h]
            dstate_ref[h] = dprev[h]
        dbg_ref[...] = dbg

    row = lambda i: (nc - 1 - i, 0)
    return pl.pallas_call(
        body, name=name, grid=(nc,),
        in_specs=[pl.BlockSpec((CHUNK, D_MODEL), row)] * 3 + [pl.BlockSpec((CHUNK, 128), row),
                  pl.BlockSpec((None, N_HEADS, HEAD_DIM, HEAD_DIM), lambda i: (nc - 1 - i, 0, 0, 0)),
                  pl.BlockSpec((CHUNK, D_MODEL), row)],
        out_specs=[pl.BlockSpec((CHUNK, D_MODEL), row)] * 3 + [pl.BlockSpec((CHUNK, 128), row)],
        out_shape=[jax.ShapeDtypeStruct((n, D_MODEL), F32)] * 3 + [jax.ShapeDtypeStruct((n, 128), F32)],
        scratch_shapes=[pltpu.VMEM((N_HEADS, HEAD_DIM, HEAD_DIM), F32)],
        compiler_params=_params("arbitrary"),
    )(qn, kn, v, bg, saved, do)


SB_BQ = 512
SB_SUB = 128
SB_SCALE = HEAD_DIM ** -0.5


def _sb_terms(z, before):
    e = jnp.exp(-jnp.abs(z))
    log_beta = jnp.minimum(z, 0.0) - jnp.log(1.0 + e)
    log_1m = log_beta - z
    if before is not None:
        log_1m = jnp.where(before, log_1m, 0.0)
    return e, log_beta, log_1m


def _sb_before(bq, s):
    rows = lax.broadcasted_iota(jnp.int32, (bq, SB_SUB), 0)
    cols = lax.broadcasted_iota(jnp.int32, (bq, SB_SUB), 1)
    return cols + s * SB_SUB < rows


def _tri(n, cmp):
    r = lax.broadcasted_iota(jnp.int32, (n, n), 0)
    c = lax.broadcasted_iota(jnp.int32, (n, n), 1)
    return cmp(r, c).astype(BF16)


def _sb_attention(q, kv, name, bq=SB_BQ):
    n = q.shape[0]
    bq = min(bq, n)
    nsub = bq // SB_SUB

    def body(q_ref, k_ref, v_ref, o_ref, l_ref):
        i = pl.program_id(1)
        qb = q_ref[...]
        after = _tri(SB_SUB, lambda r, c: r > c)

        def span(j, c_sum, acc, diagonal):
            start = pl.multiple_of(j * bq, bq)
            k_w = k_ref[pl.ds(start, bq), :]
            v_w = v_ref[pl.ds(start, bq), :]
            z = lax.dot_general(qb, k_w, _dims(False, True), preferred_element_type=F32) * SB_SCALE
            a_parts = [None] * nsub
            for s in reversed(range(nsub)):
                before = _sb_before(bq, s) if diagonal else None
                _, log_beta, log_1m = _sb_terms(z[:, s * SB_SUB:(s + 1) * SB_SUB], before)
                a = jnp.exp(log_beta + c_sum + _split_dot(log_1m, after))
                if diagonal:
                    a = jnp.where(before, a, 0.0)
                a_parts[s] = a.astype(BF16)
                c_sum = c_sum + jnp.sum(log_1m, axis=1, keepdims=True)
            acc = acc + jnp.dot(jnp.concatenate(a_parts, axis=1), v_w, preferred_element_type=F32)
            return c_sum, acc

        carry = span(i, jnp.zeros((bq, 1), F32), jnp.zeros((bq, HEAD_DIM), F32), True)
        c_sum, acc = lax.fori_loop(0, i, lambda jj, c: span(i - 1 - jj, c[0], c[1], False), carry)
        o_ref[...] = acc.astype(BF16)
        l_ref[...] = jnp.broadcast_to(c_sum, (bq, 128))

    return pl.pallas_call(
        body, name=name, grid=(N_HEADS, n // bq),
        in_specs=[pl.BlockSpec((bq, HEAD_DIM), lambda h, i: (i, h)),
                  pl.BlockSpec((n, HEAD_DIM), lambda h, i: (0, h)),
                  pl.BlockSpec((n, HEAD_DIM), lambda h, i: (0, N_HEADS + h))],
        out_specs=[pl.BlockSpec((bq, HEAD_DIM), lambda h, i: (i, h)),
                   pl.BlockSpec((None, bq, 128), lambda h, i: (h, i, 0))],
        out_shape=[jax.ShapeDtypeStruct((n, D_MODEL), BF16), jax.ShapeDtypeStruct((N_HEADS, n, 128), F32)],
        compiler_params=_params("parallel", "arbitrary"),
    )(q, kv, kv)


def _sb_attention_bwd(q, kv, do, ltot, name, bq=SB_BQ):
    n = q.shape[0]
    bq = min(bq, n)
    nsub = bq // SB_SUB

    def body(q_ref, k_ref, v_ref, do_ref, l_ref, dq_ref, dk_ref, dv_ref):
        i = pl.program_id(1)

        @pl.when(i == 0)
        def _():
            dk_ref[...] = jnp.zeros_like(dk_ref)
            dv_ref[...] = jnp.zeros_like(dv_ref)

        qb = q_ref[...]
        dob = do_ref[...]
        l_tot = jnp.max(l_ref[...], axis=1, keepdims=True)
        upto = _tri(SB_SUB, lambda r, c: r <= c)
        below = _tri(SB_SUB, lambda r, c: r < c)

        def span(j, l_sum, g_sum, dq, diagonal):
            start = pl.multiple_of(j * bq, bq)
            k_w = k_ref[pl.ds(start, bq), :]
            v_w = v_ref[pl.ds(start, bq), :]
            z = lax.dot_general(qb, k_w, _dims(False, True), preferred_element_type=F32) * SB_SCALE
            da = lax.dot_general(dob, v_w, _dims(False, True), preferred_element_type=F32)
            a_parts, dz_parts = [], []
            for s in range(nsub):
                cols = slice(s * SB_SUB, (s + 1) * SB_SUB)
                zs = z[:, cols]
                before = _sb_before(bq, s) if diagonal else None
                e, log_beta, log_1m = _sb_terms(zs, before)
                a = jnp.exp(log_beta + (l_tot - (l_sum + _split_dot(log_1m, upto))))
                if diagonal:
                    a = jnp.where(before, a, 0.0)
                g = a * da[:, cols]
                p = g_sum + _split_dot(g, below)
                inv = 1.0 / (1.0 + e)
                beta = jnp.where(zs >= 0, inv, e * inv)
                dz = (g - (g + p) * beta) * SB_SCALE
                if diagonal:
                    dz = jnp.where(before, dz, 0.0)
                a_parts.append(a.astype(BF16))
                dz_parts.append(dz.astype(BF16))
                l_sum = l_sum + jnp.sum(log_1m, axis=1, keepdims=True)
                g_sum = g_sum + jnp.sum(g, axis=1, keepdims=True)
            dz_w = jnp.concatenate(dz_parts, axis=1)
            a_w = jnp.concatenate(a_parts, axis=1)
            dq = dq + jnp.dot(dz_w, k_w, preferred_element_type=F32)
            dk_ref[pl.ds(start, bq), :] += lax.dot_general(dz_w, qb, _dims(True, False), preferred_element_type=F32)
            dv_ref[pl.ds(start, bq), :] += lax.dot_general(a_w, dob, _dims(True, False), preferred_element_type=F32)
            return l_sum, g_sum, dq

        zero = jnp.zeros((bq, 1), F32)
        carry = lax.fori_loop(0, i, lambda j, c: span(j, c[0], c[1], c[2], False),
                              (zero, zero, jnp.zeros((bq, HEAD_DIM), F32)))
        _, _, dq = span(i, carry[0], carry[1], carry[2], True)
        dq_ref[...] = dq.astype(BF16)

    return pl.pallas_call(
        body, name=name, grid=(N_HEADS, n // bq),
        in_specs=[pl.BlockSpec((bq, HEAD_DIM), lambda h, i: (i, h)),
                  pl.BlockSpec((n, HEAD_DIM), lambda h, i: (0, h)),
                  pl.BlockSpec((n, HEAD_DIM), lambda h, i: (0, N_HEADS + h)),
                  pl.BlockSpec((bq, HEAD_DIM), lambda h, i: (i, h)),
                  pl.BlockSpec((None, bq, 128), lambda h, i: (h, i, 0))],
        out_specs=[pl.BlockSpec((bq, HEAD_DIM), lambda h, i: (i, h)),
                   pl.BlockSpec((n, HEAD_DIM), lambda h, i: (0, h)),
                   pl.BlockSpec((n, HEAD_DIM), lambda h, i: (0, h))],
        out_shape=[jax.ShapeDtypeStruct((n, D_MODEL), BF16), jax.ShapeDtypeStruct((n, D_MODEL), F32),
                   jax.ShapeDtypeStruct((n, D_MODEL), F32)],
        compiler_params=_params("parallel", "arbitrary"),
    )(q, kv, kv, do, ltot)


def _loss_head(y, target, name, tm=512):
    n, d = y.shape

    def body(y_ref, t_ref, dy_ref, loss_ref):
        err = y_ref[...] - t_ref[...]
        dy_ref[...] = err * (1.0 / d)
        part = 0.5 * jnp.sum(jnp.mean(err * err, axis=-1, keepdims=True), axis=0, keepdims=True)
        _acc_rows(loss_ref, pl.program_id(0) == 0, [jnp.broadcast_to(part, (1, 128))])

    row = lambda i: (i, 0)
    return pl.pallas_call(
        body, name=name, grid=(n // tm,),
        in_specs=[pl.BlockSpec((tm, d), row), pl.BlockSpec((tm, d), row)],
        out_specs=[pl.BlockSpec((tm, d), row), pl.BlockSpec((8, 128), lambda i: (0, 0))],
        out_shape=[jax.ShapeDtypeStruct((n, d), F32), jax.ShapeDtypeStruct((8, 128), F32)],
        compiler_params=_params("arbitrary"),
    )(y, target)


def _local_step(x, target, gains, w):
    n = x.shape[0]
    g = gains
    row = lambda a, i: a[i:i + 1, :]

    proj, h0 = _norm_matmul(x, row(g["mix_pre"], 0), w["gdn_in"], F32, "gdn_in_proj", tn=384)
    qn, kn, v, bg = _gdn_pre(proj, w["conv"], g["alog"], g["dtb"], "gdn_pre")
    o_gdn, saved = _gdn_scan(qn, kn, v, bg, "gdn_scan")
    x1, y_mix0, a_gdn = _out_proj((o_gdn, proj), True, x, w["gdn_out"], row(g["mix_post"], 0), g["out_gain"],
                                  "gdn_out_proj")
    x2, y_mlp0, h_mlp0, u0, a0 = _mlp_fwd(x1, row(g["mlp_pre"], 0), w["up"][0], w["down"][0], row(g["mlp_post"], 0),
                                          "mlp0")
    kv, h_kv = _norm_matmul(x2, g["kv"], w["kv"], BF16, "kv_proj")
    q, h_q = _norm_matmul(x2, row(g["mix_pre"], 1), w["sb_q"], BF16, "sb_q_proj")
    o_sb, ltot = _sb_attention(q, kv, "sb_attention")
    x3, y_mix1 = _out_proj((o_sb,), False, x2, w["sb_o"], row(g["mix_post"], 1), None, "sb_out_proj")
    x4, y_mlp1, h_mlp1, u1, a1 = _mlp_fwd(x3, row(g["mlp_pre"], 1), w["up"][1], w["down"][1], row(g["mlp_post"], 1),
                                          "mlp1")
    dx4, loss = _loss_head(x4, target, "loss_head")

    dx3, dy_mlp1, du1, dg_mlp1 = _mlp_bwd(dx4, y_mlp1, row(g["mlp_post"], 1), w["down"][1], u1, w["up"][1], x3,
                                          row(g["mlp_pre"], 1), "mlp1_bwd")
    d_down1 = _matmul_tn(a1, dy_mlp1, 1, "d_down1")
    d_up1 = _matmul_tn(h_mlp1, du1, N_DEV, "d_up1")
    dy_mix1, dg_post1, do_sb = _out_proj_bwd(dx3, y_mix1, w["sb_o"], row(g["mix_post"], 1), None, "sb_out_proj_bwd")
    d_sb_o = _matmul_tn(o_sb, dy_mix1, 1, "d_sb_o")
    dq, dk, dv = _sb_attention_bwd(q, kv, do_sb, ltot, "sb_attention_bwd")
    d_sb_q = _matmul_tn(h_q, dq, 1, "d_sb_q")
    d_kv = jnp.concatenate([_matmul_tn(h_kv, dk, N_DEV // 2, "d_w_k"), _matmul_tn(h_kv, dv, N_DEV // 2, "d_w_v")], axis=0)
    t, dg_pre1 = _norm_matmul_bwd(dq, w["sb_q"], x2, row(g["mix_pre"], 1), dx3, "sb_q_proj_bwd")
    t, dg_kv_k = _norm_matmul_bwd(dk, w["kv"][:N_DEV // 2], x2, g["kv"], t, "k_proj_bwd", tn=256)
    dx2, dg_kv_v = _norm_matmul_bwd(dv, w["kv"][N_DEV // 2:], x2, g["kv"], t, "v_proj_bwd", tn=256)

    dx1, dy_mlp0, du0, dg_mlp0 = _mlp_bwd(dx2, y_mlp0, row(g["mlp_post"], 0), w["down"][0], u0, w["up"][0], x1,
                                          row(g["mlp_pre"], 0), "mlp0_bwd")
    d_down0 = _matmul_tn(a0, dy_mlp0, 1, "d_down0")
    d_up0 = _matmul_tn(h_mlp0, du0, N_DEV, "d_up0")
    dy_mix0, dg_post0, do_gdn, dgate, d_out_gain = _out_proj_bwd(
        dx1, y_mix0, w["gdn_out"], row(g["mix_post"], 0), (o_gdn, proj, g["out_gain"]), "gdn_out_proj_bwd")
    d_gdn_out = _matmul_tn(a_gdn, dy_mix0, 1, "d_gdn_out")
    dqn, dkn, dvv, dbg = _gdn_scan_bwd(qn, kn, v, bg, saved, do_gdn, "gdn_scan_bwd")
    dproj, d_conv, d_gates = _gdn_pre_bwd(proj, w["conv"], g["alog"], g["dtb"], dqn, dkn, dvv, dbg, dgate, "gdn_pre_bwd")
    d_gdn_in = _matmul_tn(h0, dproj, 1, "d_gdn_in", tb=384)
    grad_x, dg_pre0 = _norm_matmul_bwd(dproj, w["gdn_in"], x, row(g["mix_pre"], 0), dx1, "gdn_in_proj_bwd", tn=384)

    big = {"up": (d_up0, d_up1), "down": (d_down0[0], d_down1[0]), "gdn_in": d_gdn_in[0], "gdn_out": d_gdn_out[0],
           "kv": d_kv, "sb_q": d_sb_q[0], "sb_o": d_sb_o[0]}
    small = {"pre0": dg_pre0, "pre1": dg_pre1, "post0": dg_post0, "post1": dg_post1, "mlp0": dg_mlp0, "mlp1": dg_mlp1,
             "kv_k": dg_kv_k, "kv_v": dg_kv_v, "gates": d_gates, "out_gain": d_out_gain, "conv": d_conv, "loss": loss}
    return grad_x, big, small


def _my_place():
    return lax.axis_index("x"), lax.axis_index("y"), lax.axis_index("c")


def _exchange(arrays, gather, name):
    n_arr = len(arrays)

    def body(*refs):
        ins, outs = refs[:n_arr], refs[n_arr:2 * n_arr]
        send_sems, recv_sems, local_sems = refs[2 * n_arr:]
        x, y, c = _my_place()
        me = 4 * x + 2 * y + c
        local = []
        for a in range(n_arr):
            src = ins[a] if gather[a] else ins[a].at[me]
            local.append(pltpu.make_async_copy(src, outs[a].at[me], local_sems.at[a]))
            local[-1].start()
        remote = []
        for k in range(1, N_DEV):
            px = 1 - x if k & 4 else x
            py = 1 - y if k & 2 else y
            pc = 1 - c if k & 1 else c
            peer = 4 * px + 2 * py + pc
            for a in range(n_arr):
                src = ins[a] if gather[a] else ins[a].at[peer]
                remote.append(pltpu.make_async_remote_copy(
                    src_ref=src, dst_ref=outs[a].at[me], send_sem=send_sems.at[a, k - 1], recv_sem=recv_sems.at[a, k - 1],
                    device_id=(px, py, pc), device_id_type=MESH))
                remote[-1].start()
        for cp in remote:
            cp.wait()
        for cp in local:
            cp.wait()

    def out_struct(a, is_gather):
        shape = (N_DEV,) + a.shape if is_gather else a.shape
        return jax.ShapeDtypeStruct(shape, a.dtype)

    any_spec = pl.BlockSpec(memory_space=pl.ANY)
    return pl.pallas_call(
        body, name=name,
        in_specs=[any_spec] * n_arr, out_specs=[any_spec] * n_arr,
        out_shape=[out_struct(a, gth) for a, gth in zip(arrays, gather)],
        scratch_shapes=[pltpu.SemaphoreType.DMA((n_arr, N_DEV - 1)), pltpu.SemaphoreType.DMA((n_arr, N_DEV - 1)),
                        pltpu.SemaphoreType.DMA((n_arr,))],
    )(*arrays)


def _adamw_math(g, w, m, v):
    m = ADAM_B1 * m + (1.0 - ADAM_B1) * g
    v = ADAM_B2 * v + (1.0 - ADAM_B2) * jnp.square(g)
    m_hat = m / (1.0 - ADAM_B1 ** ADAM_STEP)
    v_hat = v / (1.0 - ADAM_B2 ** ADAM_STEP)
    delta = -ADAM_LR * (m_hat / (jnp.sqrt(v_hat) + ADAM_EPS) + ADAM_WD * w)
    return delta, m, v


def _sum_devices(ref):
    total = ref[0].astype(F32)
    for d in range(1, N_DEV):
        total = total + ref[d].astype(F32)
    return total


def _reduce_adamw(landed, w, m, v, name, tr=256):
    r, c = w.shape
    tr = min(tr, r)
    assert r % tr == 0

    def body(l_ref, w_ref, m_ref, v_ref, g_ref, d_ref, nm_ref, nv_ref):
        g = _sum_devices(l_ref)
        g_ref[...] = g
        d_ref[...], nm_ref[...], nv_ref[...] = _adamw_math(g, w_ref[...], m_ref[...], v_ref[...])

    blk = pl.BlockSpec((tr, c), lambda i: (i, 0))
    return pl.pallas_call(
        body, name=name, grid=(r // tr,),
        in_specs=[pl.BlockSpec((N_DEV, tr, c), lambda i: (0, i, 0)), blk, blk, blk],
        out_specs=[blk] * 4, out_shape=[jax.ShapeDtypeStruct((r, c), F32)] * 4,
        compiler_params=_params("parallel"),
    )(landed, w, m, v)


def _adamw(g, w, m, v, name):
    def body(g_ref, w_ref, m_ref, v_ref, d_ref, nm_ref, nv_ref):
        d_ref[...], nm_ref[...], nv_ref[...] = _adamw_math(g_ref[...], w_ref[...], m_ref[...], v_ref[...])

    return pl.pallas_call(body, name=name, out_shape=[jax.ShapeDtypeStruct(w.shape, F32)] * 3)(g, w, m, v)


SMALL_ROWS = ("mix_pre", "mix_post", "mlp_pre", "mlp_post")


def _small_update(landed, params, name):
    layout = {
        "mix_pre": [("pre0", 0), ("pre1", 0)], "mix_post": [("post0", 0), ("post1", 0)],
        "mlp_pre": [("mlp0", 0), ("mlp1", 0)], "mlp_post": [("mlp0", 1), ("mlp1", 1)],
        "kv": [("kv_k", 0)], "alog": [("gates", 0)], "dtb": [("gates", 1)], "out_gain": [("out_gain", 0)],
    }
    landed_names = sorted(landed)
    param_names = sorted(params)
    n_l, n_p = len(landed_names), len(param_names)

    def body(*refs):
        l_refs = dict(zip(landed_names, refs[:n_l]))
        p_refs = {p: refs[n_l + 3 * i:n_l + 3 * i + 3] for i, p in enumerate(param_names)}
        outs = refs[n_l + 3 * n_p:]
        o_refs = {p: outs[4 * i:4 * i + 4] for i, p in enumerate(param_names)}
        conv_ref, loss_ref = outs[4 * n_p:]
        sums = {nm: _sum_devices(l_refs[nm]) for nm in landed_names}
        sums["kv_k"] = sums["kv_k"] + sums["kv_v"]
        for p in param_names:
            w_ref, m_ref, v_ref = p_refs[p]
            g_ref, d_ref, nm_ref, nv_ref = o_refs[p]
            for r, (src, src_row) in enumerate(layout[p]):
                g = sums[src][src_row:src_row + 1, :]
                g_ref[r:r + 1, :] = g
                d, nm, nv = _adamw_math(g, w_ref[r:r + 1, :], m_ref[r:r + 1, :], v_ref[r:r + 1, :])
                d_ref[r:r + 1, :] = d
                nm_ref[r:r + 1, :] = nm
                nv_ref[r:r + 1, :] = nv
        conv_ref[...] = sums["conv"]
        loss_ref[...] = sums["loss"]

    args = [landed[nm] for nm in landed_names]
    out_shape = []
    for p in param_names:
        args += list(params[p])
        out_shape += [jax.ShapeDtypeStruct(params[p][0].shape, F32)] * 4
    out_shape += [jax.ShapeDtypeStruct(landed["conv"].shape[1:], F32), jax.ShapeDtypeStruct(landed["loss"].shape[1:], F32)]
    outs = pl.pallas_call(body, name=name, out_shape=out_shape)(*args)
    result = {p: tuple(outs[4 * i:4 * i + 4]) for i, p in enumerate(param_names)}
    result["conv"], result["loss"] = outs[4 * n_p], outs[4 * n_p + 1]
    return result


def _lanes(vec, offset):
    return jnp.pad(vec[None, :], ((0, 0), (offset, 128 - offset - vec.shape[0])))


def kernel(x, mix_pre_gain, mix_post_gain, mlp_pre_gain, mlp_post_gain, mlp_w_up, mlp_w_down, gdn_w_in, gdn_conv_w, gdn_a_log, gdn_dt_bias, gdn_out_gain, gdn_w_out, kv_gain, w_kv, sb_w_q, sb_w_o, loss_target, m_mix_pre_gain, m_mix_post_gain, m_mlp_pre_gain, m_mlp_post_gain, m_mlp_w_up, m_mlp_w_down, m_gdn_w_in, m_gdn_conv_w, m_gdn_a_log, m_gdn_dt_bias, m_gdn_out_gain, m_gdn_w_out, m_kv_gain, m_w_kv, m_sb_w_q, m_sb_w_o, v_mix_pre_gain, v_mix_post_gain, v_mlp_pre_gain, v_mlp_post_gain, v_mlp_w_up, v_mlp_w_down, v_gdn_w_in, v_gdn_conv_w, v_gdn_a_log, v_gdn_dt_bias, v_gdn_out_gain, v_gdn_w_out, v_kv_gain, v_w_kv, v_sb_w_q, v_sb_w_o):
    d = D_MODEL
    me = 4 * lax.axis_index("x") + 2 * lax.axis_index("y") + lax.axis_index("c")
    bf = lambda a: a.astype(BF16)

    shards = [bf(mlp_w_up[0]), bf(mlp_w_up[1]), bf(mlp_w_down[0]), bf(mlp_w_down[1]), bf(gdn_w_in[0]), bf(gdn_w_out[0]),
              bf(w_kv), bf(sb_w_q[0]), bf(sb_w_o[0]), gdn_conv_w[0]]
    up0, up1, down0, down1, gdn_in, gdn_out, kv_w, sb_q, sb_o, conv = _exchange(shards, [True] * len(shards), "gather_weights")
    gdn_in = jnp.pad(gdn_in.transpose(1, 0, 2).reshape(d, GDN_IN_COLS), ((0, 0), (0, GDN_IN_PAD - GDN_IN_COLS)))
    weights = {
        "up": (up0, up1), "down": (down0.reshape(D_FF, d), down1.reshape(D_FF, d)), "gdn_in": gdn_in[None],
        "conv": conv.transpose(1, 0, 2).reshape(CONV_WIDTH, QKV), "gdn_out": gdn_out.reshape(d, d), "kv": kv_w,
        "sb_q": sb_q.reshape(1, d, d), "sb_o": sb_o.reshape(d, d),
    }
    gains = {"mix_pre": mix_pre_gain, "mix_post": mix_post_gain, "mlp_pre": mlp_pre_gain, "mlp_post": mlp_post_gain,
             "kv": kv_gain[None, :], "alog": _lanes(gdn_a_log[0], N_HEADS), "dtb": _lanes(gdn_dt_bias[0], N_HEADS),
             "out_gain": gdn_out_gain}

    grad_x, big, small = _local_step(x[0], loss_target[0], gains, weights)

    d_gdn_in = big["gdn_in"][:, :GDN_IN_COLS].reshape(d, N_DEV, GDN_IN_COLS // N_DEV).transpose(1, 0, 2)
    rows = lambda a: a.reshape(N_DEV, a.shape[0] // N_DEV, a.shape[1])
    parts = [bf(big["up"][0]), bf(big["up"][1]), bf(rows(big["down"][0])), bf(rows(big["down"][1])), bf(d_gdn_in),
             bf(rows(big["gdn_out"])), bf(big["kv"]), bf(rows(big["sb_q"])), bf(rows(big["sb_o"]))]
    small_names = ["pre0", "pre1", "post0", "post1", "mlp0", "mlp1", "kv_k", "kv_v", "gates", "out_gain", "conv", "loss"]
    small_parts = [small[nm] for nm in small_names]
    landed = _exchange(parts + small_parts, [False] * len(parts) + [True] * len(small_parts), "exchange_grads")
    big_landed, small_landed = landed[:len(parts)], dict(zip(small_names, landed[len(parts):]))

    results = {}
    big_params = [("mlp_w_up0", mlp_w_up[0], m_mlp_w_up[0], v_mlp_w_up[0]), ("mlp_w_up1", mlp_w_up[1], m_mlp_w_up[1], v_mlp_w_up[1]),
                  ("mlp_w_down0", mlp_w_down[0], m_mlp_w_down[0], v_mlp_w_down[0]),
                  ("mlp_w_down1", mlp_w_down[1], m_mlp_w_down[1], v_mlp_w_down[1]),
                  ("gdn_w_in", gdn_w_in[0], m_gdn_w_in[0], v_gdn_w_in[0]), ("gdn_w_out", gdn_w_out[0], m_gdn_w_out[0], v_gdn_w_out[0]),
                  ("w_kv", w_kv, m_w_kv, v_w_kv), ("sb_w_q", sb_w_q[0], m_sb_w_q[0], v_sb_w_q[0]),
                  ("sb_w_o", sb_w_o[0], m_sb_w_o[0], v_sb_w_o[0])]
    for (nm, w_, m_, v_), land in zip(big_params, big_landed):
        results[nm] = _reduce_adamw(land, w_, m_, v_, "adamw_" + nm)
    lanes8 = lambda a: _lanes(a[0], N_HEADS)
    small_params = {
        "mix_pre": (mix_pre_gain, m_mix_pre_gain, v_mix_pre_gain), "mix_post": (mix_post_gain, m_mix_post_gain, v_mix_post_gain),
        "mlp_pre": (mlp_pre_gain, m_mlp_pre_gain, v_mlp_pre_gain), "mlp_post": (mlp_post_gain, m_mlp_post_gain, v_mlp_post_gain),
        "kv": (kv_gain[None, :], m_kv_gain[None, :], v_kv_gain[None, :]),
        "alog": (lanes8(gdn_a_log), lanes8(m_gdn_a_log), lanes8(v_gdn_a_log)),
        "dtb": (lanes8(gdn_dt_bias), lanes8(m_gdn_dt_bias), lanes8(v_gdn_dt_bias)),
        "out_gain": (gdn_out_gain, m_gdn_out_gain, v_gdn_out_gain),
    }
    sm = _small_update(small_landed, small_params, "small_update")
    conv_cols = QKV // N_DEV
    g_conv = lax.dynamic_slice(sm["conv"], (0, me * conv_cols), (8, conv_cols))[:CONV_WIDTH]
    conv_res = (g_conv,) + tuple(_adamw(g_conv, gdn_conv_w[0], m_gdn_conv_w[0], v_gdn_conv_w[0], "adamw_conv"))

    stack2 = lambda a, b: tuple(jnp.stack([p, q]) for p, q in zip(results[a], results[b]))
    lead = lambda t: tuple(a[None] for a in t)
    heads8 = lambda t: tuple(a[:, N_HEADS:2 * N_HEADS] for a in t)
    per_weight = [
        sm["mix_pre"], sm["mix_post"], sm["mlp_pre"], sm["mlp_post"],
        stack2("mlp_w_up0", "mlp_w_up1"), stack2("mlp_w_down0", "mlp_w_down1"),
        lead(results["gdn_w_in"]), lead(conv_res), heads8(sm["alog"]), heads8(sm["dtb"]), sm["out_gain"],
        lead(results["gdn_w_out"]), tuple(a[0] for a in sm["kv"]), results["w_kv"], lead(results["sb_w_q"]), lead(results["sb_w_o"]),
    ]
    grads, deltas, new_ms, new_vs = zip(*per_weight)
    return (sm["loss"][0, 0], grad_x[None], *grads, *deltas, *new_ms, *new_vs)
```

```python
import functools
import math

import jax
import jax.numpy as jnp
from jax import lax
from jax.experimental import pallas as pl
from jax.experimental.pallas import tpu as pltpu

F32 = jnp.float32
BF16 = jnp.bfloat16

N_DEV = 8
D_MODEL = 1024
D_FF = 4096
N_HEADS = 8
HEAD_DIM = 128
CHUNK = 64
CONV_WIDTH = 4
GDN_IN_COLS = 4 * D_MODEL + 2 * N_HEADS
GDN_IN_PAD = 4 * D_MODEL + 128
EPS = 1e-6

ADAM_LR = 0.001
ADAM_B1 = 0.9
ADAM_B2 = 0.999
ADAM_EPS = 1e-08
ADAM_WD = 0.01
ADAM_STEP = 10

VMEM_LIMIT_BYTES = 56 * 1024 * 1024
MESH = pl.DeviceIdType.MESH


def _params(*semantics):
    return pltpu.CompilerParams(dimension_semantics=semantics, vmem_limit_bytes=VMEM_LIMIT_BYTES)


def _dims(ta, tb):
    return (((0,) if ta else (1,), (1,) if tb else (0,)), ((), ()))


def _dot(a, b, ta=False, tb=False):
    return lax.dot_general(a.astype(BF16), b.astype(BF16), _dims(ta, tb), preferred_element_type=F32)


def _dot_f32(a, b, ta=False, tb=False):
    return lax.dot_general(a, b, _dims(ta, tb), precision=lax.Precision.HIGHEST, preferred_element_type=F32)


def _make_mm(dot):
    @functools.partial(jax.custom_vjp, nondiff_argnums=(2, 3))
    def mm(a, b, ta, tb):
        return dot(a, b, ta, tb)

    def fwd(a, b, ta, tb):
        return dot(a, b, ta, tb), (a, b)

    def bwd(ta, tb, res, g):
        a, b = res
        if not ta and not tb:
            return mm(g, b, False, True), mm(a, g, True, False)
        if not ta and tb:
            return mm(g, b, False, False), mm(g, a, True, False)
        if ta and not tb:
            return mm(b, g, False, True), mm(a, g, False, False)
        raise NotImplementedError

    mm.defvjp(fwd, bwd)
    return mm


_mm = _make_mm(_dot)
_mm_f32 = _make_mm(_dot_f32)


def _rms(x, gain):
    r = lax.rsqrt(jnp.mean(x * x, axis=-1, keepdims=True) + EPS)
    return x * r * gain


def _rms_bwd(x, gain, dy):
    r = lax.rsqrt(jnp.mean(x * x, axis=-1, keepdims=True) + EPS)
    xh = x * r
    dgain = jnp.sum(dy * xh, axis=0, keepdims=True)
    dxh = dy * gain
    dx = r * (dxh - xh * jnp.mean(dxh * xh, axis=-1, keepdims=True))
    return dx, dgain


def _silu(x):
    return x / (1.0 + jnp.exp(-x))


def _acc_rows(ref, first, rows):
    @pl.when(first)
    def _():
        ref[...] = jnp.zeros_like(ref)

    for r, val in enumerate(rows):
        ref[r:r + 1, :] += val


def _norm_matmul(x, gain, w, out_dtype, name, tm=1024, tn=1024):
    n, d = x.shape
    g, _, wc = w.shape
    tm, tn = min(tm, n), min(tn, wc)
    per = wc // tn
    assert n % tm == 0 and wc % tn == 0

    def body(x_ref, gain_ref, w_ref, out_ref, h_ref):
        @pl.when(pl.program_id(1) == 0)
        def _():
            h_ref[...] = _rms(x_ref[...], gain_ref[...]).astype(BF16)

        out_ref[...] = jnp.dot(h_ref[...], w_ref[...], preferred_element_type=F32).astype(out_dtype)

    return pl.pallas_call(
        body, name=name, grid=(n // tm, g * per),
        in_specs=[pl.BlockSpec((tm, d), lambda i, j: (i, 0)),
                  pl.BlockSpec((1, d), lambda i, j: (0, 0)),
                  pl.BlockSpec((None, d, tn), lambda i, j: (j // per, 0, j % per))],
        out_specs=[pl.BlockSpec((tm, tn), lambda i, j: (i, j)),
                   pl.BlockSpec((tm, d), lambda i, j: (i, 0))],
        out_shape=[jax.ShapeDtypeStruct((n, g * wc), out_dtype), jax.ShapeDtypeStruct((n, d), BF16)],
        compiler_params=_params("parallel", "arbitrary"),
    )(x, gain, w)


def _norm_matmul_bwd(dout, w, x, gain, add, name, tm=1024, tn=512):
    n, d = x.shape
    g, _, wc = w.shape
    tm, tn = min(tm, n), min(tn, wc)
    per = wc // tn
    ncol = g * per

    def body(dout_ref, w_ref, x_ref, gain_ref, add_ref, dx_ref, dgain_ref, acc_ref):
        i, j = pl.program_id(0), pl.program_id(1)

        @pl.when(j == 0)
        def _():
            acc_ref[...] = jnp.zeros_like(acc_ref)

        acc_ref[...] += _dot(dout_ref[...], w_ref[...], tb=True)

        @pl.when(j == ncol - 1)
        def _():
            dx, dgain = _rms_bwd(x_ref[...], gain_ref[...], acc_ref[...])
            dx_ref[...] = add_ref[...] + dx
            _acc_rows(dgain_ref, i == 0, [dgain])

    return pl.pallas_call(
        body, name=name, grid=(n // tm, ncol),
        in_specs=[pl.BlockSpec((tm, tn), lambda i, j: (i, j)),
                  pl.BlockSpec((None, d, tn), lambda i, j: (j // per, 0, j % per)),
                  pl.BlockSpec((tm, d), lambda i, j: (i, 0)),
                  pl.BlockSpec((1, d), lambda i, j: (0, 0)),
                  pl.BlockSpec((tm, d), lambda i, j: (i, 0))],
        out_specs=[pl.BlockSpec((tm, d), lambda i, j: (i, 0)),
                   pl.BlockSpec((8, d), lambda i, j: (0, 0))],
        out_shape=[jax.ShapeDtypeStruct((n, d), F32), jax.ShapeDtypeStruct((8, d), F32)],
        scratch_shapes=[pltpu.VMEM((tm, d), F32)],
        compiler_params=_params("arbitrary", "arbitrary"),
    )(dout, w, x, gain, add)


def _matmul_tn(a, b, groups, name, ta=1024, tb=1024, tk=1024):
    n, ka = a.shape
    _, kb = b.shape
    wc = kb // groups
    ta, tb, tk = min(ta, ka), min(tb, wc), min(tk, n)
    per = wc // tb
    nk = n // tk
    assert ka % ta == 0 and wc % tb == 0 and n % tk == 0

    def body(a_ref, b_ref, out_ref):
        @pl.when(pl.program_id(2) == 0)
        def _():
            out_ref[...] = jnp.zeros_like(out_ref)

        out_ref[...] += _dot(a_ref[...], b_ref[...], ta=True)

    return pl.pallas_call(
        body, name=name, grid=(ka // ta, groups * per, nk),
        in_specs=[pl.BlockSpec((tk, ta), lambda i, j, k: (k, i)),
                  pl.BlockSpec((tk, tb), lambda i, j, k: (k, j))],
        out_specs=pl.BlockSpec((None, ta, tb), lambda i, j, k: (j // per, i, j % per)),
        out_shape=jax.ShapeDtypeStruct((groups, ka, wc), F32),
        compiler_params=_params("parallel", "parallel", "arbitrary"),
    )(a, b)


def _gated_head_norm(o, gate, out_gain):
    parts = []
    for h in range(N_HEADS):
        sl = slice(h * HEAD_DIM, (h + 1) * HEAD_DIM)
        parts.append(_rms(o[:, sl], out_gain) * _silu(gate[:, sl]))
    return parts


def _out_proj(a_inputs, gated, x_in, w, gain, out_gain, name, tm=1024):
    n, d = x_in.shape
    k = w.shape[0]
    tm = min(tm, n)

    def body(*refs):
        if gated:
            o_ref, gate_ref, og_ref, x_ref, w_ref, gain_ref, xo_ref, y_ref, a_ref = refs
            parts = _gated_head_norm(o_ref[...], gate_ref[...], og_ref[...])
            for h, part in enumerate(parts):
                a_ref[:, h * HEAD_DIM:(h + 1) * HEAD_DIM] = part.astype(BF16)
            a = a_ref[...]
        else:
            a_in_ref, x_ref, w_ref, gain_ref, xo_ref, y_ref = refs
            a = a_in_ref[...]
        y = jnp.dot(a, w_ref[...], preferred_element_type=F32)
        y_ref[...] = y
        xo_ref[...] = x_ref[...] + _rms(y, gain_ref[...])

    row = lambda i: (i, 0)
    const = lambda i: (0, 0)
    if gated:
        a_specs = [pl.BlockSpec((tm, k), row), pl.BlockSpec((tm, D_MODEL), lambda i: (i, 3)),
                   pl.BlockSpec((1, HEAD_DIM), const)]
        a_args = list(a_inputs) + [out_gain]
    else:
        a_specs = [pl.BlockSpec((tm, k), row)]
        a_args = list(a_inputs)
    out_specs = [pl.BlockSpec((tm, d), row), pl.BlockSpec((tm, d), row)]
    out_shape = [jax.ShapeDtypeStruct((n, d), F32), jax.ShapeDtypeStruct((n, d), F32)]
    if gated:
        out_specs.append(pl.BlockSpec((tm, k), row))
        out_shape.append(jax.ShapeDtypeStruct((n, k), BF16))
    return pl.pallas_call(
        body, name=name, grid=(n // tm,),
        in_specs=a_specs + [pl.BlockSpec((tm, d), row), pl.BlockSpec((k, d), const), pl.BlockSpec((1, d), const)],
        out_specs=out_specs, out_shape=out_shape,
        compiler_params=_params("parallel"),
    )(*a_args, x_in, w, gain)


def _out_proj_bwd(dxo, y, w, gain, gated_inputs, name, tm=1024):
    n, d = dxo.shape
    k = w.shape[0]
    tm = min(tm, n)
    gated = gated_inputs is not None

    def body(*refs):
        if gated:
            (dxo_ref, y_ref, w_ref, gain_ref, o_ref, gate_ref, og_ref,
             dy_ref, dgain_ref, do_ref, dgate_ref, dog_ref) = refs
        else:
            dxo_ref, y_ref, w_ref, gain_ref, dy_ref, dgain_ref, da_ref = refs
        first = pl.program_id(0) == 0
        dy, dgain = _rms_bwd(y_ref[...], gain_ref[...], dxo_ref[...])
        dy_ref[...] = dy.astype(BF16)
        _acc_rows(dgain_ref, first, [dgain])
        da = _dot(dy_ref[...], w_ref[...], tb=True)
        if not gated:
            da_ref[...] = da.astype(BF16)
            return
        og = og_ref[...]
        dog = jnp.zeros_like(og)
        for h in range(N_HEADS):
            sl = slice(h * HEAD_DIM, (h + 1) * HEAD_DIM)
            fn = lambda o_h, g_h, gn: _rms(o_h, gn) * _silu(g_h)
            _, vjp = jax.vjp(fn, o_ref[:, sl], gate_ref[:, sl], og)
            do_h, dgate_h, dog_h = vjp(da[:, sl])
            do_ref[:, sl] = do_h
            dgate_ref[:, sl] = dgate_h
            dog = dog + dog_h
        _acc_rows(dog_ref, first, [dog])

    row = lambda i: (i, 0)
    const = lambda i: (0, 0)
    in_specs = [pl.BlockSpec((tm, d), row), pl.BlockSpec((tm, d), row), pl.BlockSpec((k, d), const),
                pl.BlockSpec((1, d), const)]
    args = [dxo, y, w, gain]
    out_specs = [pl.BlockSpec((tm, d), row), pl.BlockSpec((8, d), const)]
    out_shape = [jax.ShapeDtypeStruct((n, d), BF16), jax.ShapeDtypeStruct((8, d), F32)]
    if gated:
        in_specs += [pl.BlockSpec((tm, k), row), pl.BlockSpec((tm, D_MODEL), lambda i: (i, 3)),
                     pl.BlockSpec((1, HEAD_DIM), const)]
        args += list(gated_inputs)
        out_specs += [pl.BlockSpec((tm, k), row), pl.BlockSpec((tm, k), row), pl.BlockSpec((8, HEAD_DIM), const)]
        out_shape += [jax.ShapeDtypeStruct((n, k), F32), jax.ShapeDtypeStruct((n, k), F32),
                      jax.ShapeDtypeStruct((8, HEAD_DIM), F32)]
    else:
        out_specs.append(pl.BlockSpec((tm, k), row))
        out_shape.append(jax.ShapeDtypeStruct((n, k), BF16))
    return pl.pallas_call(
        body, name=name, grid=(n // tm,), in_specs=in_specs, out_specs=out_specs, out_shape=out_shape,
        compiler_params=_params("arbitrary"),
    )(*args)


def _mlp_fwd(x_in, g_pre, w_up, w_down, g_post, name, tm=1024):
    n, d = x_in.shape
    g, _, wc = w_up.shape
    tm = min(tm, n)

    def body(x_ref, gpre_ref, wup_ref, wdown_ref, gpost_ref, xo_ref, y_ref, h_ref, u_ref, a_ref, acc_ref):
        j = pl.program_id(1)

        @pl.when(j == 0)
        def _():
            h_ref[...] = _rms(x_ref[...], gpre_ref[...]).astype(BF16)
            acc_ref[...] = jnp.zeros_like(acc_ref)

        u = jnp.dot(h_ref[...], wup_ref[...], preferred_element_type=F32).astype(BF16)
        u_ref[...] = u
        a = jnp.square(jnp.maximum(u, 0))
        a_ref[...] = a
        acc_ref[...] += jnp.dot(a, wdown_ref[...], preferred_element_type=F32)

        @pl.when(j == g - 1)
        def _():
            y = acc_ref[...]
            y_ref[...] = y
            xo_ref[...] = x_ref[...] + _rms(y, gpost_ref[...])

    row = lambda i, j: (i, 0)
    const = lambda i, j: (0, 0)
    return pl.pallas_call(
        body, name=name, grid=(n // tm, g),
        in_specs=[pl.BlockSpec((tm, d), row), pl.BlockSpec((1, d), const),
                  pl.BlockSpec((None, d, wc), lambda i, j: (j, 0, 0)),
                  pl.BlockSpec((wc, d), lambda i, j: (j, 0)), pl.BlockSpec((1, d), const)],
        out_specs=[pl.BlockSpec((tm, d), row), pl.BlockSpec((tm, d), row), pl.BlockSpec((tm, d), row),
                   pl.BlockSpec((tm, wc), lambda i, j: (i, j)), pl.BlockSpec((tm, wc), lambda i, j: (i, j))],
        out_shape=[jax.ShapeDtypeStruct((n, d), F32), jax.ShapeDtypeStruct((n, d), F32),
                   jax.ShapeDtypeStruct((n, d), BF16), jax.ShapeDtypeStruct((n, g * wc), BF16),
                   jax.ShapeDtypeStruct((n, g * wc), BF16)],
        scratch_shapes=[pltpu.VMEM((tm, d), F32)],
        compiler_params=_params("parallel", "arbitrary"),
    )(x_in, g_pre, w_up, w_down, g_post)


def _mlp_bwd(dxo, y, g_post, w_down, u, w_up, x_in, g_pre, name, tm=512):
    n, d = dxo.shape
    g, _, wc = w_up.shape

    def body(dxo_ref, y_ref, gpost_ref, wdown_ref, u_ref, wup_ref, x_ref, gpre_ref,
             dx_ref, dy_ref, du_ref, dgain_ref, acc_ref, dgpost_ref):
        i, j = pl.program_id(0), pl.program_id(1)

        @pl.when(j == 0)
        def _():
            dy, dgpost = _rms_bwd(y_ref[...], gpost_ref[...], dxo_ref[...])
            dy_ref[...] = dy.astype(BF16)
            dgpost_ref[...] = dgpost
            acc_ref[...] = jnp.zeros_like(acc_ref)

        da = _dot(dy_ref[...], wdown_ref[...], tb=True)
        du = (da * (2.0 * jnp.maximum(u_ref[...], 0).astype(F32))).astype(BF16)
        du_ref[...] = du
        acc_ref[...] += _dot(du, wup_ref[...], tb=True)

        @pl.when(j == g - 1)
        def _():
            dx, dgpre = _rms_bwd(x_ref[...], gpre_ref[...], acc_ref[...])
            dx_ref[...] = dxo_ref[...] + dx
            _acc_rows(dgain_ref, i == 0, [dgpre, dgpost_ref[...]])

    row = lambda i, j: (i, 0)
    const = lambda i, j: (0, 0)
    return pl.pallas_call(
        body, name=name, grid=(n // tm, g),
        in_specs=[pl.BlockSpec((tm, d), row), pl.BlockSpec((tm, d), row), pl.BlockSpec((1, d), const),
                  pl.BlockSpec((wc, d), lambda i, j: (j, 0)), pl.BlockSpec((tm, wc), lambda i, j: (i, j)),
                  pl.BlockSpec((None, d, wc), lambda i, j: (j, 0, 0)), pl.BlockSpec((tm, d), row),
                  pl.BlockSpec((1, d), const)],
        out_specs=[pl.BlockSpec((tm, d), row), pl.BlockSpec((tm, d), row),
                   pl.BlockSpec((tm, wc), lambda i, j: (i, j)), pl.BlockSpec((8, d), const)],
        out_shape=[jax.ShapeDtypeStruct((n, d), F32), jax.ShapeDtypeStruct((n, d), BF16),
                   jax.ShapeDtypeStruct((n, g * wc), BF16), jax.ShapeDtypeStruct((8, d), F32)],
        scratch_shapes=[pltpu.VMEM((tm, d), F32), pltpu.VMEM((1, d), F32)],
        compiler_params=_params("arbitrary", "arbitrary"),
    )(dxo, y, g_post, w_down, u, w_up, x_in, g_pre)


QKV = 3 * D_MODEL


def _shifted(x, prev8, s):
    if s == 0:
        return x
    tm = x.shape[0]
    rolled = pltpu.roll(x, s, 0)
    head = pltpu.roll(prev8, s, 0)
    head = jnp.concatenate([head, jnp.zeros((tm - 8, x.shape[1]), x.dtype)], axis=0)
    rows = lax.broadcasted_iota(jnp.int32, x.shape, 0)
    return jnp.where(rows < s, head, rolled)


def _conv(x, prev8, conv_w):
    out = x * conv_w[3:4, :]
    for s in range(1, CONV_WIDTH):
        out = out + _shifted(x, prev8, s) * conv_w[3 - s:4 - s, :]
    return out


def _l2norm(x):
    return x * lax.rsqrt(jnp.sum(x * x, axis=-1, keepdims=True) + EPS)


def _gdn_act_head(cq, ck, cv):
    return _l2norm(_silu(cq)) * (HEAD_DIM ** -0.5), _l2norm(_silu(ck)), _silu(cv)


def _gdn_gates(ba, alog, dtb):
    lane = lax.broadcasted_iota(jnp.int32, ba.shape, 1)
    beta = 1.0 / (1.0 + jnp.exp(-ba))
    t = ba + dtb
    softplus = jnp.maximum(t, 0.0) + jnp.log(1.0 + jnp.exp(-jnp.abs(t)))
    g = -jnp.exp(alog) * softplus
    return jnp.where(lane < N_HEADS, beta, jnp.where(lane < 2 * N_HEADS, g, 0.0))


def _qkv_cols(part, h):
    start = part * D_MODEL + h * HEAD_DIM
    return slice(start, start + HEAD_DIM)


def _prev_rows_spec(tm, cols, colblock, order):
    per = tm // 8
    return pl.BlockSpec((8, cols), lambda i: (jnp.maximum(order(i) * per - 1, 0), colblock))


def _gdn_pre(proj, conv_w, alog, dtb, name, tm=256):
    n = proj.shape[0]

    def body(x_ref, prev_ref, ba_ref, cw_ref, alog_ref, dtb_ref, q_ref, k_ref, v_ref, bg_ref):
        first = pl.program_id(0) == 0
        for h in range(N_HEADS):
            convs = []
            for part in range(3):
                sl = _qkv_cols(part, h)
                prev8 = jnp.where(first, 0.0, prev_ref[:, sl])
                convs.append(_conv(x_ref[:, sl], prev8, cw_ref[:, sl]))
            out = slice(h * HEAD_DIM, (h + 1) * HEAD_DIM)
            q_ref[:, out], k_ref[:, out], v_ref[:, out] = _gdn_act_head(*convs)
        bg_ref[...] = _gdn_gates(ba_ref[...], alog_ref[...], dtb_ref[...])

    row = lambda i: (i, 0)
    const = lambda i: (0, 0)
    ident = lambda i: i
    return pl.pallas_call(
        body, name=name, grid=(n // tm,),
        in_specs=[pl.BlockSpec((tm, QKV), row), _prev_rows_spec(tm, QKV, 0, ident),
                  pl.BlockSpec((tm, 128), lambda i: (i, 4 * D_MODEL // 128)),
                  pl.BlockSpec((CONV_WIDTH, QKV), const), pl.BlockSpec((1, 128), const), pl.BlockSpec((1, 128), const)],
        out_specs=[pl.BlockSpec((tm, D_MODEL), row)] * 3 + [pl.BlockSpec((tm, 128), row)],
        out_shape=[jax.ShapeDtypeStruct((n, D_MODEL), F32)] * 3 + [jax.ShapeDtypeStruct((n, 128), F32)],
        compiler_params=_params("parallel"),
    )(proj, proj, proj, conv_w, alog, dtb)


def _gdn_pre_bwd(proj, conv_w, alog, dtb, dq, dk, dv, dbg, dgate, name, tm=256):
    n = proj.shape[0]
    nt = n // tm

    def body(x_ref, prev_ref, ba_ref, cw_ref, alog_ref, dtb_ref, dq_ref, dk_ref, dv_ref, dbg_ref, dgate_ref,
             dproj_ref, dcw_ref, dgates_ref, carry_ref):
        step = pl.program_id(0)
        tile = nt - 1 - step
        @pl.when(step == 0)
        def _():
            carry_ref[...] = jnp.zeros_like(carry_ref)
            dcw_ref[...] = jnp.zeros_like(dcw_ref)

        rows = lax.broadcasted_iota(jnp.int32, (tm, HEAD_DIM), 0)
        for h in range(N_HEADS):
            cols = [_qkv_cols(part, h) for part in range(3)]
            prevs = [jnp.where(tile == 0, 0.0, prev_ref[:, sl]) for sl in cols]
            convs = [_conv(x_ref[:, sl], prev8, cw_ref[:, sl]) for sl, prev8 in zip(cols, prevs)]
            _, vjp = jax.vjp(_gdn_act_head, *convs)
            out = slice(h * HEAD_DIM, (h + 1) * HEAD_DIM)
            dcs = vjp((dq_ref[:, out], dk_ref[:, out], dv_ref[:, out]))
            for sl, prev8, dc in zip(cols, prevs, dcs):
                x = x_ref[:, sl]
                cw = cw_ref[:, sl]
                dx = dc * cw[3:4, :]
                wrapped = jnp.zeros((8, HEAD_DIM), F32)
                dcw_ref[3:4, sl] += jnp.sum(dc * x, axis=0, keepdims=True)
                for s in range(1, CONV_WIDTH):
                    r = pltpu.roll(dc, tm - s, 0) * cw[3 - s:4 - s, :]
                    dx = dx + jnp.where(rows < tm - s, r, 0.0)
                    wrapped = wrapped + jnp.where(rows[tm - 8:, :] >= tm - s, r[tm - 8:, :], 0.0)
                    dcw_ref[3 - s:4 - s, sl] += jnp.sum(dc * _shifted(x, prev8, s), axis=0, keepdims=True)
                dproj_ref[:, sl] = dx
                dproj_ref[tm - 8:, sl] += carry_ref[:, sl]
                carry_ref[:, sl] = wrapped

        _, vjp = jax.vjp(_gdn_gates, ba_ref[...], alog_ref[...], dtb_ref[...])
        dba, dalog, ddtb = vjp(dbg_ref[...])
        dproj_ref[:, QKV:4 * D_MODEL] = dgate_ref[...]
        dproj_ref[:, 4 * D_MODEL:] = dba
        _acc_rows(dgates_ref, step == 0, [dalog, ddtb])

    rev = lambda i: nt - 1 - i
    row = lambda i: (rev(i), 0)
    const = lambda i: (0, 0)
    return pl.pallas_call(
        body, name=name, grid=(nt,),
        in_specs=[pl.BlockSpec((tm, QKV), row), _prev_rows_spec(tm, QKV, 0, rev),
                  pl.BlockSpec((tm, 128), lambda i: (rev(i), 4 * D_MODEL // 128)),
                  pl.BlockSpec((CONV_WIDTH, QKV), const), pl.BlockSpec((1, 128), const), pl.BlockSpec((1, 128), const),
                  pl.BlockSpec((tm, D_MODEL), row), pl.BlockSpec((tm, D_MODEL), row), pl.BlockSpec((tm, D_MODEL), row),
                  pl.BlockSpec((tm, 128), row), pl.BlockSpec((tm, D_MODEL), row)],
        out_specs=[pl.BlockSpec((tm, GDN_IN_PAD), row), pl.BlockSpec((8, QKV), const), pl.BlockSpec((8, 128), const)],
        out_shape=[jax.ShapeDtypeStruct((n, GDN_IN_PAD), F32), jax.ShapeDtypeStruct((8, QKV), F32),
                   jax.ShapeDtypeStruct((8, 128), F32)],
        scratch_shapes=[pltpu.VMEM((8, QKV), F32)],
        compiler_params=_params("arbitrary"),
    )(proj, proj, proj, conv_w, alog, dtb, dq, dk, dv, dbg, dgate)


@jax.custom_vjp
def _unit_lower_inverses(lowers):
    c = lowers[0].shape[0]
    eye = (lax.broadcasted_iota(jnp.int32, (c, c), 0) == lax.broadcasted_iota(jnp.int32, (c, c), 1)).astype(F32)
    ys = [-low for low in lowers]
    ps = [eye + y for y in ys]
    for _ in range(int(math.log2(c)) - 1):
        ys = [_dot_f32(y, y) for y in ys]
        ps = [p + _dot_f32(p, y) for p, y in zip(ps, ys)]
    return tuple(ps)


def _unit_lower_inverses_fwd(lowers):
    ts = _unit_lower_inverses(lowers)
    return ts, ts


def _unit_lower_inverses_bwd(ts, dts):
    left = [_dot_f32(t, dt, ta=True) for t, dt in zip(ts, dts)]
    return (tuple(-_dot_f32(l, t, tb=True) for l, t in zip(left, ts)),)


_unit_lower_inverses.defvjp(_unit_lower_inverses_fwd, _unit_lower_inverses_bwd)


def _gdn_chunk(qs, ks, vs, bg, states):
    c = CHUNK
    heads = range(N_HEADS)
    row = lax.broadcasted_iota(jnp.int32, (c, c), 0)
    col = lax.broadcasted_iota(jnp.int32, (c, c), 1)
    incl, strict, eye = row >= col, row > col, row == col
    lane = lax.broadcasted_iota(jnp.int32, (c, 128), 1)
    rowc = lax.broadcasted_iota(jnp.int32, (c, 1), 0)
    gc_all = _mm_f32(incl.astype(F32), bg, False, False)
    beta = [jnp.sum(jnp.where(lane == h, bg, 0.0), axis=1, keepdims=True) for h in heads]
    gc = [jnp.sum(jnp.where(lane == N_HEADS + h, gc_all, 0.0), axis=1, keepdims=True) for h in heads]
    gc_row = [jnp.sum(jnp.where(eye, gc[h], 0.0), axis=0, keepdims=True) for h in heads]
    gc_last = [jnp.sum(jnp.where(rowc == c - 1, gc[h], 0.0), axis=0, keepdims=True) for h in heads]
    decay = [jnp.where(incl, jnp.exp(jnp.where(incl, gc[h] - gc_row[h], 0.0)), 0.0) for h in heads]
    kb = [ks[h] * beta[h] for h in heads]
    lower = tuple(jnp.where(strict, _mm(kb[h], ks[h], False, True) * decay[h], 0.0) for h in heads)
    attn = [_mm(qs[h], ks[h], False, True) * decay[h] for h in heads]
    t_mat = _unit_lower_inverses(lower)
    egc = [jnp.exp(gc[h]) for h in heads]
    w = [_mm(t_mat[h], kb[h] * egc[h], False, False) for h in heads]
    u = [_mm(t_mat[h], vs[h] * beta[h], False, False) for h in heads]
    kg = [ks[h] * jnp.exp(gc_last[h] - gc[h]) for h in heads]
    v_new = [u[h] - _mm(w[h], states[h], False, False) for h in heads]
    from_state = [_mm(qs[h] * egc[h], states[h], False, False) for h in heads]
    outs = [from_state[h] + _mm(attn[h], v_new[h], False, False) for h in heads]
    new_states = [states[h] * jnp.exp(gc_last[h]) + _mm(kg[h], v_new[h], True, False) for h in heads]
    return tuple(outs), tuple(new_states)


def _head_slices(ref):
    return tuple(ref[:, h * HEAD_DIM:(h + 1) * HEAD_DIM] for h in range(N_HEADS))


def _gdn_scan(qn, kn, v, bg, name, ride=None):
    n = qn.shape[0]
    nc = n // CHUNK

    def body(q_ref, k_ref, v_ref, bg_ref, o_ref, saved_ref, state_ref):
        @pl.when(pl.program_id(0) == 0)
        def _():
            state_ref[...] = jnp.zeros_like(state_ref)

        states = tuple(state_ref[h] for h in range(N_HEADS))
        for h in range(N_HEADS):
            saved_ref[h] = states[h]
        outs, new_states = _gdn_chunk(_head_slices(q_ref), _head_slices(k_ref), _head_slices(v_ref), bg_ref[...], states)
        for h in range(N_HEADS):
            o_ref[:, h * HEAD_DIM:(h + 1) * HEAD_DIM] = outs[h]
            state_ref[h] = new_states[h]

    row = lambda i: (i, 0)
    return _call(
        body, [qn, kn, v, bg], name=name, grid=(nc,),
        in_specs=[pl.BlockSpec((CHUNK, D_MODEL), row)] * 3 + [pl.BlockSpec((CHUNK, 128), row)],
        out_specs=[pl.BlockSpec((CHUNK, D_MODEL), row),
                   pl.BlockSpec((None, N_HEADS, HEAD_DIM, HEAD_DIM), lambda i: (i, 0, 0, 0))],
        out_shape=[jax.ShapeDtypeStruct((n, D_MODEL), F32),
                   jax.ShapeDtypeStruct((nc, N_HEADS, HEAD_DIM, HEAD_DIM), F32)],
        scratch_shapes=[pltpu.VMEM((N_HEADS, HEAD_DIM, HEAD_DIM), F32)],
        compiler_params=_params("arbitrary"),
        ride=ride, first=lambda: pl.program_id(0) == 0, last=lambda: pl.program_id(0) == nc - 1)


def _gdn_scan_bwd(qn, kn, v, bg, saved, do, name, ride=None):
    n = qn.shape[0]
    nc = n // CHUNK

    def body(q_ref, k_ref, v_ref, bg_ref, saved_ref, do_ref, dq_ref, dk_ref, dv_ref, dbg_ref, dstate_ref):
        @pl.when(pl.program_id(0) == 0)
        def _():
            dstate_ref[...] = jnp.zeros_like(dstate_ref)

        states = tuple(saved_ref[h] for h in range(N_HEADS))
        _, vjp = jax.vjp(_gdn_chunk, _head_slices(q_ref), _head_slices(k_ref), _head_slices(v_ref), bg_ref[...], states)
        dstates = tuple(dstate_ref[h] for h in range(N_HEADS))
        dqs, dks, dvs, dbg, dprev = vjp((_head_slices(do_ref), dstates))
        for h in range(N_HEADS):
            sl = slice(h * HEAD_DIM, (h + 1) * HEAD_DIM)
            dq_ref[:, sl] = dqs[h]
            dk_ref[:, sl] = dks[h]
            dv_ref[:, sl] = dvs[h]
            dstate_ref[h] = dprev[h]
        dbg_ref[...] = dbg

    row = lambda i: (nc - 1 - i, 0)
    return _call(
        body, [qn, kn, v, bg, saved, do], name=name, grid=(nc,),
        in_specs=[pl.BlockSpec((CHUNK, D_MODEL), row)] * 3 + [pl.BlockSpec((CHUNK, 128), row),
                  pl.BlockSpec((None, N_HEADS, HEAD_DIM, HEAD_DIM), lambda i: (nc - 1 - i, 0, 0, 0)),
                  pl.BlockSpec((CHUNK, D_MODEL), row)],
        out_specs=[pl.BlockSpec((CHUNK, D_MODEL), row)] * 3 + [pl.BlockSpec((CHUNK, 128), row)],
        out_shape=[jax.ShapeDtypeStruct((n, D_MODEL), F32)] * 3 + [jax.ShapeDtypeStruct((n, 128), F32)],
        scratch_shapes=[pltpu.VMEM((N_HEADS, HEAD_DIM, HEAD_DIM), F32)],
        compiler_params=_params("arbitrary"),
        ride=ride, first=lambda: pl.program_id(0) == 0, last=lambda: pl.program_id(0) == nc - 1)


SB_BQ = 512
SB_SUB = 128
SB_ROWS = 128
SB_SCALE = HEAD_DIM ** -0.5


def _sb_terms(z, before):
    e = jnp.exp(-jnp.abs(z))
    log_beta = jnp.minimum(z, 0.0) - jnp.log(1.0 + e)
    log_1m = log_beta - z
    if before is not None:
        log_1m = jnp.where(before, log_1m, 0.0)
    return e, log_beta, log_1m


def _tri_ones(n, cmp):
    r = lax.broadcasted_iota(jnp.int32, (n, 2 * n), 0)
    c = lax.broadcasted_iota(jnp.int32, (n, 2 * n), 1)
    return jnp.where((c >= n) | cmp(r, c), 1.0, 0.0).astype(BF16)


def _sums(x, tri_ones):
    both = jnp.dot(x.astype(BF16), tri_ones, preferred_element_type=F32)
    n = x.shape[1]
    return both[:, :n], both[:, n:]


def _sb_chunk_mask(diagonal, r, s):
    if not diagonal or s * SB_SUB + SB_SUB - 1 < r * SB_ROWS:
        return None
    if s * SB_SUB >= r * SB_ROWS + SB_ROWS - 1:
        return "empty"
    rows = r * SB_ROWS + lax.broadcasted_iota(jnp.int32, (SB_ROWS, SB_SUB), 0)
    cols = s * SB_SUB + lax.broadcasted_iota(jnp.int32, (SB_ROWS, SB_SUB), 1)
    return cols < rows


def _sb_attention(q, kv, name, bq=SB_BQ, ride=None):
    n = q.shape[0]
    bq = min(bq, n)
    nsub, nrc = bq // SB_SUB, bq // SB_ROWS

    def body(q_ref, k_ref, v_ref, o_ref, l_ref, z_scr, a_scr):
        i = pl.program_id(1)
        qb = q_ref[...]
        after = _tri_ones(SB_SUB, lambda r, c: r > c)

        def span(j, c_sum, acc, diagonal):
            start = pl.multiple_of(j * bq, bq)
            k_w = k_ref[pl.ds(start, bq), :]
            v_w = v_ref[pl.ds(start, bq), :]
            z_scr[...] = lax.dot_general(qb, k_w, _dims(False, True), preferred_element_type=F32)
            c_rows = [c_sum[r * SB_ROWS:(r + 1) * SB_ROWS] for r in range(nrc)]
            for s in reversed(range(nsub)):
                cols = slice(s * SB_SUB, (s + 1) * SB_SUB)
                for r in range(nrc):
                    rows = slice(r * SB_ROWS, (r + 1) * SB_ROWS)
                    before = _sb_chunk_mask(diagonal, r, s)
                    if isinstance(before, str):
                        a_scr[rows, cols] = jnp.zeros((SB_ROWS, SB_SUB), BF16)
                        continue
                    _, log_beta, log_1m = _sb_terms(z_scr[rows, cols] * SB_SCALE, before)
                    tail, total = _sums(log_1m, after)
                    a = jnp.exp(log_beta + c_rows[r] + tail)
                    if before is not None:
                        a = jnp.where(before, a, 0.0)
                    a_scr[rows, cols] = a.astype(BF16)
                    c_rows[r] = c_rows[r] + total
            acc = acc + jnp.dot(a_scr[...], v_w, preferred_element_type=F32)
            return jnp.concatenate(c_rows, axis=0), acc

        carry = span(i, jnp.zeros((bq, SB_SUB), F32), jnp.zeros((bq, HEAD_DIM), F32), True)
        c_sum, acc = lax.fori_loop(0, i, lambda jj, c: span(i - 1 - jj, c[0], c[1], False), carry)
        o_ref[...] = acc.astype(BF16)
        l_ref[...] = c_sum

    nq = n // bq
    return _call(
        body, [q, kv, kv], name=name, grid=(N_HEADS, nq),
        in_specs=[pl.BlockSpec((bq, HEAD_DIM), lambda h, i: (i, h)),
                  pl.BlockSpec((n, HEAD_DIM), lambda h, i: (0, h)),
                  pl.BlockSpec((n, HEAD_DIM), lambda h, i: (0, N_HEADS + h))],
        out_specs=[pl.BlockSpec((bq, HEAD_DIM), lambda h, i: (i, h)),
                   pl.BlockSpec((None, bq, 128), lambda h, i: (h, i, 0))],
        out_shape=[jax.ShapeDtypeStruct((n, D_MODEL), BF16), jax.ShapeDtypeStruct((N_HEADS, n, 128), F32)],
        scratch_shapes=[pltpu.VMEM((bq, bq), F32), pltpu.VMEM((bq, bq), BF16)],
        compiler_params=_params("arbitrary", "arbitrary"),
        ride=ride, first=lambda: (pl.program_id(0) == 0) & (pl.program_id(1) == 0),
        last=lambda: (pl.program_id(0) == N_HEADS - 1) & (pl.program_id(1) == nq - 1))


def _sb_attention_bwd(q, kv, do, ltot, name, bq=SB_BQ, ride=None):
    n = q.shape[0]
    bq = min(bq, n)
    nsub, nrc = bq // SB_SUB, bq // SB_ROWS

    def body(q_ref, k_ref, v_ref, do_ref, l_ref, dq_ref, dk_ref, dv_ref, z_scr, da_scr, a_scr, dz_scr):
        i = pl.program_id(1)

        @pl.when(i == 0)
        def _():
            dk_ref[...] = jnp.zeros_like(dk_ref)
            dv_ref[...] = jnp.zeros_like(dv_ref)

        qb = q_ref[...]
        dob = do_ref[...]
        lt_rows = [l_ref[r * SB_ROWS:(r + 1) * SB_ROWS, :] for r in range(nrc)]
        upto = _tri_ones(SB_SUB, lambda r, c: r <= c)
        below = _tri_ones(SB_SUB, lambda r, c: r < c)

        def span(j, l_sum, g_sum, dq, diagonal):
            start = pl.multiple_of(j * bq, bq)
            k_w = k_ref[pl.ds(start, bq), :]
            v_w = v_ref[pl.ds(start, bq), :]
            z_scr[...] = lax.dot_general(qb, k_w, _dims(False, True), preferred_element_type=F32)
            da_scr[...] = lax.dot_general(dob, v_w, _dims(False, True), preferred_element_type=F32)
            l_rows = [l_sum[r * SB_ROWS:(r + 1) * SB_ROWS] for r in range(nrc)]
            g_rows = [g_sum[r * SB_ROWS:(r + 1) * SB_ROWS] for r in range(nrc)]
            for s in range(nsub):
                cols = slice(s * SB_SUB, (s + 1) * SB_SUB)
                for r in range(nrc):
                    rows = slice(r * SB_ROWS, (r + 1) * SB_ROWS)
                    before = _sb_chunk_mask(diagonal, r, s)
                    if isinstance(before, str):
                        a_scr[rows, cols] = jnp.zeros((SB_ROWS, SB_SUB), BF16)
                        dz_scr[rows, cols] = jnp.zeros((SB_ROWS, SB_SUB), BF16)
                        continue
                    zs = z_scr[rows, cols] * SB_SCALE
                    e, log_beta, log_1m = _sb_terms(zs, before)
                    l_prefix, l_total = _sums(log_1m, upto)
                    a = jnp.exp(log_beta + (lt_rows[r] - (l_rows[r] + l_prefix)))
                    if before is not None:
                        a = jnp.where(before, a, 0.0)
                    g = a * da_scr[rows, cols]
                    g_prefix, g_total = _sums(g, below)
                    inv = 1.0 / (1.0 + e)
                    beta = jnp.where(zs >= 0, inv, e * inv)
                    dz = (g - (g + g_rows[r] + g_prefix) * beta) * SB_SCALE
                    if before is not None:
                        dz = jnp.where(before, dz, 0.0)
                    a_scr[rows, cols] = a.astype(BF16)
                    dz_scr[rows, cols] = dz.astype(BF16)
                    l_rows[r] = l_rows[r] + l_total
                    g_rows[r] = g_rows[r] + g_total
            dz_w = dz_scr[...]
            dq = dq + jnp.dot(dz_w, k_w, preferred_element_type=F32)
            dk_ref[pl.ds(start, bq), :] += lax.dot_general(dz_w, qb, _dims(True, False), preferred_element_type=F32)
            dv_ref[pl.ds(start, bq), :] += lax.dot_general(a_scr[...], dob, _dims(True, False), preferred_element_type=F32)
            return jnp.concatenate(l_rows, axis=0), jnp.concatenate(g_rows, axis=0), dq

        zero = jnp.zeros((bq, SB_SUB), F32)
        carry = lax.fori_loop(0, i, lambda j, c: span(j, c[0], c[1], c[2], False),
                              (zero, zero, jnp.zeros((bq, HEAD_DIM), F32)))
        _, _, dq = span(i, carry[0], carry[1], carry[2], True)
        dq_ref[...] = dq.astype(BF16)

    nq = n // bq
    return _call(
        body, [q, kv, kv, do, ltot], name=name, grid=(N_HEADS, nq),
        in_specs=[pl.BlockSpec((bq, HEAD_DIM), lambda h, i: (i, h)),
                  pl.BlockSpec((n, HEAD_DIM), lambda h, i: (0, h)),
                  pl.BlockSpec((n, HEAD_DIM), lambda h, i: (0, N_HEADS + h)),
                  pl.BlockSpec((bq, HEAD_DIM), lambda h, i: (i, h)),
                  pl.BlockSpec((None, bq, 128), lambda h, i: (h, i, 0))],
        out_specs=[pl.BlockSpec((bq, HEAD_DIM), lambda h, i: (i, h)),
                   pl.BlockSpec((n, HEAD_DIM), lambda h, i: (0, h)),
                   pl.BlockSpec((n, HEAD_DIM), lambda h, i: (0, h))],
        out_shape=[jax.ShapeDtypeStruct((n, D_MODEL), BF16), jax.ShapeDtypeStruct((n, D_MODEL), F32),
                   jax.ShapeDtypeStruct((n, D_MODEL), F32)],
        scratch_shapes=[pltpu.VMEM((bq, bq), F32), pltpu.VMEM((bq, bq), F32), pltpu.VMEM((bq, bq), BF16),
                        pltpu.VMEM((bq, bq), BF16)],
        compiler_params=_params("arbitrary", "arbitrary"),
        ride=ride, first=lambda: (pl.program_id(0) == 0) & (pl.program_id(1) == 0),
        last=lambda: (pl.program_id(0) == N_HEADS - 1) & (pl.program_id(1) == nq - 1))


def _loss_head(y, target, name, tm=1024):
    n, d = y.shape

    def body(y_ref, t_ref, dy_ref, loss_ref):
        err = y_ref[...] - t_ref[...]
        dy_ref[...] = err * (1.0 / d)
        part = 0.5 * jnp.sum(jnp.mean(err * err, axis=-1, keepdims=True), axis=0, keepdims=True)
        _acc_rows(loss_ref, pl.program_id(0) == 0, [jnp.broadcast_to(part, (1, 128))])

    row = lambda i: (i, 0)
    return pl.pallas_call(
        body, name=name, grid=(n // tm,),
        in_specs=[pl.BlockSpec((tm, d), row), pl.BlockSpec((tm, d), row)],
        out_specs=[pl.BlockSpec((tm, d), row), pl.BlockSpec((8, 128), lambda i: (0, 0))],
        out_shape=[jax.ShapeDtypeStruct((n, d), F32), jax.ShapeDtypeStruct((8, 128), F32)],
        compiler_params=_params("arbitrary"),
    )(y, target)


def _local_step(x, target, gains, comm):
    g = gains
    w = comm.w
    big = {}
    row = lambda a, i: a[i:i + 1, :]

    proj, h0 = _norm_matmul(x, row(g["mix_pre"], 0), w["gdn_in"], F32, "gdn_in_proj", tn=1408)
    qn, kn, v, bg = _gdn_pre(proj, w["conv"], g["alog"], g["dtb"], "gdn_pre")
    (o_gdn, saved), got = _gdn_scan(qn, kn, v, bg, "gdn_scan", ride=comm.ride("scan"))
    comm.done("scan", got)
    x1, y_mix0, a_gdn = _out_proj((o_gdn, proj), True, x, w["gdn_out"], row(g["mix_post"], 0), g["out_gain"],
                                  "gdn_out_proj", tm=512)
    x2, y_mlp0, h_mlp0, u0, a0 = _mlp_fwd(x1, row(g["mlp_pre"], 0), w["up0"], w["down0"], row(g["mlp_post"], 0), "mlp0")
    kv, h_kv = _norm_matmul(x2, g["kv"], w["kv"], BF16, "kv_proj", tm=2048)
    q, h_q = _norm_matmul(x2, row(g["mix_pre"], 1), w["sb_q"], BF16, "sb_q_proj")
    (o_sb, ltot), got = _sb_attention(q, kv, "sb_attention", ride=comm.ride("sb"))
    comm.done("sb", got)
    x3, y_mix1 = _out_proj((o_sb,), False, x2, w["sb_o"], row(g["mix_post"], 1), None, "sb_out_proj")
    x4, y_mlp1, h_mlp1, u1, a1 = _mlp_fwd(x3, row(g["mlp_pre"], 1), w["up1"], w["down1"], row(g["mlp_post"], 1), "mlp1")
    dx4, loss = _loss_head(x4, target, "loss_head")

    dx3, dy_mlp1, du1, dg_mlp1 = _mlp_bwd(dx4, y_mlp1, row(g["mlp_post"], 1), w["down1"], u1, w["up1"], x3,
                                          row(g["mlp_pre"], 1), "mlp1_bwd")
    big["down1"] = _matmul_tn(a1, dy_mlp1, 1, "d_down1")[0]
    big["up1"] = _matmul_tn(h_mlp1, du1, N_DEV, "d_up1", tk=2048)
    dy_mix1, dg_post1, do_sb = _out_proj_bwd(dx3, y_mix1, w["sb_o"], row(g["mix_post"], 1), None, "sb_out_proj_bwd")
    big["sb_o"] = _matmul_tn(o_sb, dy_mix1, 1, "d_sb_o")[0]
    (dq, dk, dv), got = _sb_attention_bwd(q, kv, do_sb, ltot, "sb_attention_bwd", ride=comm.ride("sb_bwd", big))
    comm.done("sb_bwd", got)
    big["sb_q"] = _matmul_tn(h_q, dq, 1, "d_sb_q")[0]
    big["kv"] = jnp.concatenate([_matmul_tn(h_kv, dk, N_DEV // 2, "d_w_k", tk=2048),
                                 _matmul_tn(h_kv, dv, N_DEV // 2, "d_w_v", tk=2048)],
                                axis=0)
    t, dg_pre1 = _norm_matmul_bwd(dq, w["sb_q"], x2, row(g["mix_pre"], 1), dx3, "sb_q_proj_bwd")
    t, dg_kv_k = _norm_matmul_bwd(dk, w["kv"][:N_DEV // 2], x2, g["kv"], t, "k_proj_bwd", tn=256)
    dx2, dg_kv_v = _norm_matmul_bwd(dv, w["kv"][N_DEV // 2:], x2, g["kv"], t, "v_proj_bwd", tn=256)

    dx1, dy_mlp0, du0, dg_mlp0 = _mlp_bwd(dx2, y_mlp0, row(g["mlp_post"], 0), w["down0"], u0, w["up0"], x1,
                                          row(g["mlp_pre"], 0), "mlp0_bwd")
    big["down0"] = _matmul_tn(a0, dy_mlp0, 1, "d_down0")[0]
    big["up0"] = _matmul_tn(h_mlp0, du0, N_DEV, "d_up0", tk=2048)
    dy_mix0, dg_post0, do_gdn, dgate, d_out_gain = _out_proj_bwd(
        dx1, y_mix0, w["gdn_out"], row(g["mix_post"], 0), (o_gdn, proj, g["out_gain"]), "gdn_out_proj_bwd", tm=512)
    big["gdn_out"] = _matmul_tn(a_gdn, dy_mix0, 1, "d_gdn_out")[0]
    (dqn, dkn, dvv, dbg), got = _gdn_scan_bwd(qn, kn, v, bg, saved, do_gdn, "gdn_scan_bwd",
                                              ride=comm.ride("scan_bwd", big))
    comm.done("scan_bwd", got)
    dproj, d_conv, d_gates = _gdn_pre_bwd(proj, w["conv"], g["alog"], g["dtb"], dqn, dkn, dvv, dbg, dgate, "gdn_pre_bwd")
    big["gdn_in"] = _matmul_tn(h0, dproj, 1, "d_gdn_in", tb=1408, tk=512)[0]
    grad_x, dg_pre0 = _norm_matmul_bwd(dproj, w["gdn_in"], x, row(g["mix_pre"], 0), dx1, "gdn_in_proj_bwd", tn=1408)

    small = {"pre0": dg_pre0, "pre1": dg_pre1, "post0": dg_post0, "post1": dg_post1, "mlp0": dg_mlp0, "mlp1": dg_mlp1,
             "kv_k": dg_kv_k, "kv_v": dg_kv_v, "gates": d_gates, "out_gain": d_out_gain, "conv": d_conv, "loss": loss}
    return grad_x, big, small


class _LocalOnly:
    def __init__(self, weights):
        self.w = weights

    def ride(self, stage, grads=None):
        return None

    def done(self, stage, outs):
        pass


GATHER_STAGES = {"scan": ("gdn_out", "up0", "down0", "kv", "sb_q"), "sb": ("sb_o", "up1", "down1")}
SCATTER_STAGES = {"sb_bwd": ("up1", "down1", "sb_o"), "scan_bwd": ("sb_q", "kv", "up0", "down0", "gdn_out")}


def _whole_weight(name, gathered):
    d = D_MODEL
    if name in ("up0", "up1", "kv"):
        return gathered
    if name == "gdn_in":
        whole = gathered.transpose(1, 0, 2).reshape(d, GDN_IN_COLS)
        return jnp.pad(whole, ((0, 0), (0, GDN_IN_PAD - GDN_IN_COLS)))[None]
    if name == "conv":
        return gathered.transpose(1, 0, 2).reshape(CONV_WIDTH, QKV)
    whole = gathered.reshape(gathered.shape[0] * gathered.shape[1], d)
    return whole[None] if name == "sb_q" else whole


def _owner_blocks(name, grad):
    if name in ("up0", "up1", "kv"):
        blocks = grad
    elif name == "gdn_in":
        blocks = grad[:, :GDN_IN_COLS].reshape(D_MODEL, N_DEV, GDN_IN_COLS // N_DEV).transpose(1, 0, 2)
    else:
        blocks = grad.reshape(N_DEV, grad.shape[0] // N_DEV, grad.shape[1])
    return blocks.astype(BF16)


class _Fsdp:
    def __init__(self, shards, weights):
        self.shards, self.w, self.landed = shards, weights, {}

    def ride(self, stage, grads=None):
        if stage in GATHER_STAGES:
            names = GATHER_STAGES[stage]
            return _Exchange([self.shards[nm] for nm in names], [True] * len(names))
        names = SCATTER_STAGES[stage]
        return _Exchange([_owner_blocks(nm, grads[nm]) for nm in names], [False] * len(names))

    def done(self, stage, outs):
        if stage in GATHER_STAGES:
            for nm, out in zip(GATHER_STAGES[stage], outs):
                self.w[nm] = _whole_weight(nm, out)
        else:
            self.landed.update(zip(SCATTER_STAGES[stage], outs))


def _my_place():
    return lax.axis_index("x"), lax.axis_index("y"), lax.axis_index("c")


class _Exchange:
    def __init__(self, arrays, gather):
        self.arrays, self.gather, self.n = list(arrays), list(gather), len(arrays)
        any_spec = pl.BlockSpec(memory_space=pl.ANY)
        self.in_specs = [any_spec] * self.n
        self.out_specs = [any_spec] * self.n
        self.out_shape = [jax.ShapeDtypeStruct(((N_DEV,) + a.shape) if g else a.shape, a.dtype)
                          for a, g in zip(self.arrays, self.gather)]
        self.scratch = [pltpu.SemaphoreType.DMA((self.n, N_DEV - 1)), pltpu.SemaphoreType.DMA((self.n, N_DEV - 1)),
                        pltpu.SemaphoreType.DMA((self.n,))]

    def _copies(self, ins, outs, sems):
        send_sems, recv_sems, local_sems = sems
        x, y, c = _my_place()
        me = 4 * x + 2 * y + c
        copies = []
        for a in range(self.n):
            src = ins[a] if self.gather[a] else ins[a].at[me]
            copies.append(pltpu.make_async_copy(src, outs[a].at[me], local_sems.at[a]))
        for k in range(1, N_DEV):
            px = 1 - x if k & 4 else x
            py = 1 - y if k & 2 else y
            pc = 1 - c if k & 1 else c
            peer = 4 * px + 2 * py + pc
            for a in range(self.n):
                src = ins[a] if self.gather[a] else ins[a].at[peer]
                copies.append(pltpu.make_async_remote_copy(
                    src_ref=src, dst_ref=outs[a].at[me], send_sem=send_sems.at[a, k - 1], recv_sem=recv_sems.at[a, k - 1],
                    device_id=(px, py, pc), device_id_type=MESH))
        return copies

    def start(self, ins, outs, sems):
        for cp in self._copies(ins, outs, sems):
            cp.start()

    def wait(self, ins, outs, sems):
        for cp in self._copies(ins, outs, sems):
            cp.wait()


def _call(body, operands, *, name, out_shape, in_specs, out_specs, grid=(), scratch_shapes=(), compiler_params=None,
          ride=None, first=None, last=None):
    n_in, n_out, n_scr = len(operands), len(out_shape), len(scratch_shapes)
    if ride is None:
        outs = pl.pallas_call(body, name=name, grid=grid, in_specs=in_specs, out_specs=out_specs, out_shape=out_shape,
                              scratch_shapes=scratch_shapes, compiler_params=compiler_params)(*operands)
        return list(outs), []
    r = ride.n

    def riding(*refs):
        ins, r_ins = refs[:n_in], refs[n_in:n_in + r]
        outs, r_outs = refs[n_in + r:n_in + r + n_out], refs[n_in + r + n_out:n_in + 2 * r + n_out]
        scr, sems = refs[n_in + 2 * r + n_out:n_in + 2 * r + n_out + n_scr], refs[n_in + 2 * r + n_out + n_scr:]

        @pl.when(first())
        def _():
            ride.start(r_ins, r_outs, sems)

        body(*ins, *outs, *scr)

        @pl.when(last())
        def _():
            ride.wait(r_ins, r_outs, sems)

    outs = pl.pallas_call(
        riding, name=name, grid=grid, in_specs=list(in_specs) + ride.in_specs, out_specs=list(out_specs) + ride.out_specs,
        out_shape=list(out_shape) + ride.out_shape, scratch_shapes=list(scratch_shapes) + ride.scratch,
        compiler_params=compiler_params)(*operands, *ride.arrays)
    return list(outs[:n_out]), list(outs[n_out:])


def _exchange(arrays, gather, name):
    ride = _Exchange(arrays, gather)

    def body(*refs):
        ins, outs, sems = refs[:ride.n], refs[ride.n:2 * ride.n], refs[2 * ride.n:]
        ride.start(ins, outs, sems)
        ride.wait(ins, outs, sems)

    return pl.pallas_call(body, name=name, in_specs=ride.in_specs, out_specs=ride.out_specs, out_shape=ride.out_shape,
                          scratch_shapes=ride.scratch)(*arrays)


def _adamw_math(g, w, m, v):
    m = ADAM_B1 * m + (1.0 - ADAM_B1) * g
    v = ADAM_B2 * v + (1.0 - ADAM_B2) * jnp.square(g)
    m_hat = m / (1.0 - ADAM_B1 ** ADAM_STEP)
    v_hat = v / (1.0 - ADAM_B2 ** ADAM_STEP)
    delta = -ADAM_LR * (m_hat / (jnp.sqrt(v_hat) + ADAM_EPS) + ADAM_WD * w)
    return delta, m, v


def _sum_devices(ref):
    total = ref[0].astype(F32)
    for d in range(1, N_DEV):
        total = total + ref[d].astype(F32)
    return total


def _reduce_adamw(landed, w, m, v, name, tr=256):
    r, c = w.shape
    tr = min(tr, r)
    assert r % tr == 0

    def body(l_ref, w_ref, m_ref, v_ref, g_ref, d_ref, nm_ref, nv_ref):
        g = _sum_devices(l_ref)
        g_ref[...] = g
        d_ref[...], nm_ref[...], nv_ref[...] = _adamw_math(g, w_ref[...], m_ref[...], v_ref[...])

    blk = pl.BlockSpec((tr, c), lambda i: (i, 0))
    return pl.pallas_call(
        body, name=name, grid=(r // tr,),
        in_specs=[pl.BlockSpec((N_DEV, tr, c), lambda i: (0, i, 0)), blk, blk, blk],
        out_specs=[blk] * 4, out_shape=[jax.ShapeDtypeStruct((r, c), F32)] * 4,
        compiler_params=_params("parallel"),
    )(landed, w, m, v)


def _adamw(g, w, m, v, name):
    def body(g_ref, w_ref, m_ref, v_ref, d_ref, nm_ref, nv_ref):
        d_ref[...], nm_ref[...], nv_ref[...] = _adamw_math(g_ref[...], w_ref[...], m_ref[...], v_ref[...])

    return pl.pallas_call(body, name=name, out_shape=[jax.ShapeDtypeStruct(w.shape, F32)] * 3)(g, w, m, v)


SMALL_ROWS = ("mix_pre", "mix_post", "mlp_pre", "mlp_post")


def _small_update(landed, params, name):
    layout = {
        "mix_pre": [("pre0", 0), ("pre1", 0)], "mix_post": [("post0", 0), ("post1", 0)],
        "mlp_pre": [("mlp0", 0), ("mlp1", 0)], "mlp_post": [("mlp0", 1), ("mlp1", 1)],
        "kv": [("kv_k", 0)], "alog": [("gates", 0)], "dtb": [("gates", 1)], "out_gain": [("out_gain", 0)],
    }
    landed_names = sorted(landed)
    param_names = sorted(params)
    n_l, n_p = len(landed_names), len(param_names)

    def body(*refs):
        l_refs = dict(zip(landed_names, refs[:n_l]))
        p_refs = {p: refs[n_l + 3 * i:n_l + 3 * i + 3] for i, p in enumerate(param_names)}
        outs = refs[n_l + 3 * n_p:]
        o_refs = {p: outs[4 * i:4 * i + 4] for i, p in enumerate(param_names)}
        conv_ref, loss_ref = outs[4 * n_p:]
        sums = {nm: _sum_devices(l_refs[nm]) for nm in landed_names}
        sums["kv_k"] = sums["kv_k"] + sums["kv_v"]
        for p in param_names:
            w_ref, m_ref, v_ref = p_refs[p]
            g_ref, d_ref, nm_ref, nv_ref = o_refs[p]
            for r, (src, src_row) in enumerate(layout[p]):
                g = sums[src][src_row:src_row + 1, :]
                g_ref[r:r + 1, :] = g
                d, nm, nv = _adamw_math(g, w_ref[r:r + 1, :], m_ref[r:r + 1, :], v_ref[r:r + 1, :])
                d_ref[r:r + 1, :] = d
                nm_ref[r:r + 1, :] = nm
                nv_ref[r:r + 1, :] = nv
        conv_ref[...] = sums["conv"]
        loss_ref[...] = sums["loss"]

    args = [landed[nm] for nm in landed_names]
    out_shape = []
    for p in param_names:
        args += list(params[p])
        out_shape += [jax.ShapeDtypeStruct(params[p][0].shape, F32)] * 4
    out_shape += [jax.ShapeDtypeStruct(landed["conv"].shape[1:], F32), jax.ShapeDtypeStruct(landed["loss"].shape[1:], F32)]
    outs = pl.pallas_call(body, name=name, out_shape=out_shape)(*args)
    result = {p: tuple(outs[4 * i:4 * i + 4]) for i, p in enumerate(param_names)}
    result["conv"], result["loss"] = outs[4 * n_p], outs[4 * n_p + 1]
    return result


def _lanes(vec, offset):
    return jnp.pad(vec[None, :], ((0, 0), (offset, 128 - offset - vec.shape[0])))


def kernel(x, mix_pre_gain, mix_post_gain, mlp_pre_gain, mlp_post_gain, mlp_w_up, mlp_w_down, gdn_w_in, gdn_conv_w, gdn_a_log, gdn_dt_bias, gdn_out_gain, gdn_w_out, kv_gain, w_kv, sb_w_q, sb_w_o, loss_target, m_mix_pre_gain, m_mix_post_gain, m_mlp_pre_gain, m_mlp_post_gain, m_mlp_w_up, m_mlp_w_down, m_gdn_w_in, m_gdn_conv_w, m_gdn_a_log, m_gdn_dt_bias, m_gdn_out_gain, m_gdn_w_out, m_kv_gain, m_w_kv, m_sb_w_q, m_sb_w_o, v_mix_pre_gain, v_mix_post_gain, v_mlp_pre_gain, v_mlp_post_gain, v_mlp_w_up, v_mlp_w_down, v_gdn_w_in, v_gdn_conv_w, v_gdn_a_log, v_gdn_dt_bias, v_gdn_out_gain, v_gdn_w_out, v_kv_gain, v_w_kv, v_sb_w_q, v_sb_w_o):
    me = 4 * lax.axis_index("x") + 2 * lax.axis_index("y") + lax.axis_index("c")
    bf = lambda a: a.astype(BF16)

    shards = {"up0": bf(mlp_w_up[0]), "up1": bf(mlp_w_up[1]), "down0": bf(mlp_w_down[0]), "down1": bf(mlp_w_down[1]),
              "gdn_out": bf(gdn_w_out[0]), "kv": bf(w_kv), "sb_q": bf(sb_w_q[0]), "sb_o": bf(sb_w_o[0])}
    gdn_in, conv = _exchange([bf(gdn_w_in[0]), gdn_conv_w[0]], [True, True], "gather_first_weights")
    comm = _Fsdp(shards, {"gdn_in": _whole_weight("gdn_in", gdn_in), "conv": _whole_weight("conv", conv)})
    gains = {"mix_pre": mix_pre_gain, "mix_post": mix_post_gain, "mlp_pre": mlp_pre_gain, "mlp_post": mlp_post_gain,
             "kv": kv_gain[None, :], "alog": _lanes(gdn_a_log[0], N_HEADS), "dtb": _lanes(gdn_dt_bias[0], N_HEADS),
             "out_gain": gdn_out_gain}

    grad_x, big, small = _local_step(x[0], loss_target[0], gains, comm)

    small_names = ["pre0", "pre1", "post0", "post1", "mlp0", "mlp1", "kv_k", "kv_v", "gates", "out_gain", "conv", "loss"]
    landed = _exchange([_owner_blocks("gdn_in", big["gdn_in"])] + [small[nm] for nm in small_names],
                       [False] + [True] * len(small_names), "exchange_last_grads")
    comm.landed["gdn_in"] = landed[0]
    small_landed = dict(zip(small_names, landed[1:]))

    results = {}
    big_params = [("mlp_w_up0", "up0", mlp_w_up[0], m_mlp_w_up[0], v_mlp_w_up[0]),
                  ("mlp_w_up1", "up1", mlp_w_up[1], m_mlp_w_up[1], v_mlp_w_up[1]),
                  ("mlp_w_down0", "down0", mlp_w_down[0], m_mlp_w_down[0], v_mlp_w_down[0]),
                  ("mlp_w_down1", "down1", mlp_w_down[1], m_mlp_w_down[1], v_mlp_w_down[1]),
                  ("gdn_w_in", "gdn_in", gdn_w_in[0], m_gdn_w_in[0], v_gdn_w_in[0]),
                  ("gdn_w_out", "gdn_out", gdn_w_out[0], m_gdn_w_out[0], v_gdn_w_out[0]),
                  ("w_kv", "kv", w_kv, m_w_kv, v_w_kv), ("sb_w_q", "sb_q", sb_w_q[0], m_sb_w_q[0], v_sb_w_q[0]),
                  ("sb_w_o", "sb_o", sb_w_o[0], m_sb_w_o[0], v_sb_w_o[0])]
    for nm, short, w_, m_, v_ in big_params:
        results[nm] = _reduce_adamw(comm.landed[short], w_, m_, v_, "adamw_" + nm)
    lanes8 = lambda a: _lanes(a[0], N_HEADS)
    small_params = {
        "mix_pre": (mix_pre_gain, m_mix_pre_gain, v_mix_pre_gain), "mix_post": (mix_post_gain, m_mix_post_gain, v_mix_post_gain),
        "mlp_pre": (mlp_pre_gain, m_mlp_pre_gain, v_mlp_pre_gain), "mlp_post": (mlp_post_gain, m_mlp_post_gain, v_mlp_post_gain),
        "kv": (kv_gain[None, :], m_kv_gain[None, :], v_kv_gain[None, :]),
        "alog": (lanes8(gdn_a_log), lanes8(m_gdn_a_log), lanes8(v_gdn_a_log)),
        "dtb": (lanes8(gdn_dt_bias), lanes8(m_gdn_dt_bias), lanes8(v_gdn_dt_bias)),
        "out_gain": (gdn_out_gain, m_gdn_out_gain, v_gdn_out_gain),
    }
    sm = _small_update(small_landed, small_params, "small_update")
    conv_cols = QKV // N_DEV
    g_conv = lax.dynamic_slice(sm["conv"], (0, me * conv_cols), (8, conv_cols))[:CONV_WIDTH]
    conv_res = (g_conv,) + tuple(_adamw(g_conv, gdn_conv_w[0], m_gdn_conv_w[0], v_gdn_conv_w[0], "adamw_conv"))

    stack2 = lambda a, b: tuple(jnp.stack([p, q]) for p, q in zip(results[a], results[b]))
    lead = lambda t: tuple(a[None] for a in t)
    heads8 = lambda t: tuple(a[:, N_HEADS:2 * N_HEADS] for a in t)
    per_weight = [
        sm["mix_pre"], sm["mix_post"], sm["mlp_pre"], sm["mlp_post"],
        stack2("mlp_w_up0", "mlp_w_up1"), stack2("mlp_w_down0", "mlp_w_down1"),
        lead(results["gdn_w_in"]), lead(conv_res), heads8(sm["alog"]), heads8(sm["dtb"]), sm["out_gain"],
        lead(results["gdn_w_out"]), tuple(a[0] for a in sm["kv"]), results["w_kv"], lead(results["sb_w_q"]), lead(results["sb_w_o"]),
    ]
    grads, deltas, new_ms, new_vs = zip(*per_weight)
    return (sm["loss"][0, 0], grad_x[None], *grads, *deltas, *new_ms, *new_vs)
```

```python
import functools
import math

import jax
import jax.numpy as jnp
from jax import lax
from jax.experimental import pallas as pl
from jax.experimental.pallas import tpu as pltpu

F32 = jnp.float32
BF16 = jnp.bfloat16

N_DEV = 8
D_MODEL = 1024
D_FF = 4096
N_HEADS = 8
HEAD_DIM = 128
CHUNK = 64
CONV_WIDTH = 4
GDN_IN_COLS = 4 * D_MODEL + 2 * N_HEADS
GDN_IN_PAD = 4 * D_MODEL + 128
EPS = 1e-6

ADAM_LR = 0.001
ADAM_B1 = 0.9
ADAM_B2 = 0.999
ADAM_EPS = 1e-08
ADAM_WD = 0.01
ADAM_STEP = 10

VMEM_LIMIT_BYTES = 56 * 1024 * 1024
MESH = pl.DeviceIdType.MESH


def _params(*semantics):
    return pltpu.CompilerParams(dimension_semantics=semantics, vmem_limit_bytes=VMEM_LIMIT_BYTES)


def _dims(ta, tb):
    return (((0,) if ta else (1,), (1,) if tb else (0,)), ((), ()))


def _dot(a, b, ta=False, tb=False):
    return lax.dot_general(a.astype(BF16), b.astype(BF16), _dims(ta, tb), preferred_element_type=F32)


def _dot_f32(a, b, ta=False, tb=False):
    return lax.dot_general(a, b, _dims(ta, tb), precision=lax.Precision.HIGHEST, preferred_element_type=F32)


def _dot_3x(a, b, ta=False, tb=False):
    return lax.dot_general(a, b, _dims(ta, tb), precision=lax.Precision.HIGH, preferred_element_type=F32)


def _make_mm(dot):
    @functools.partial(jax.custom_vjp, nondiff_argnums=(2, 3))
    def mm(a, b, ta, tb):
        return dot(a, b, ta, tb)

    def fwd(a, b, ta, tb):
        return dot(a, b, ta, tb), (a, b)

    def bwd(ta, tb, res, g):
        a, b = res
        if not ta and not tb:
            return mm(g, b, False, True), mm(a, g, True, False)
        if not ta and tb:
            return mm(g, b, False, False), mm(g, a, True, False)
        if ta and not tb:
            return mm(b, g, False, True), mm(a, g, False, False)
        raise NotImplementedError

    mm.defvjp(fwd, bwd)
    return mm


_mm = _make_mm(_dot)
_mm_f32 = _make_mm(_dot_f32)


def _rms(x, gain):
    r = lax.rsqrt(jnp.mean(x * x, axis=-1, keepdims=True) + EPS)
    return x * r * gain


def _rms_bwd(x, gain, dy):
    r = lax.rsqrt(jnp.mean(x * x, axis=-1, keepdims=True) + EPS)
    xh = x * r
    dgain = jnp.sum(dy * xh, axis=0, keepdims=True)
    dxh = dy * gain
    dx = r * (dxh - xh * jnp.mean(dxh * xh, axis=-1, keepdims=True))
    return dx, dgain


def _silu(x):
    return x / (1.0 + jnp.exp(-x))


def _acc_rows(ref, first, rows):
    @pl.when(first)
    def _():
        ref[...] = jnp.zeros_like(ref)

    for r, val in enumerate(rows):
        ref[r:r + 1, :] += val


def _norm_matmul(x, gain, w, out_dtype, name, tm=1024, tn=1024):
    n, d = x.shape
    g, _, wc = w.shape
    tm, tn = min(tm, n), min(tn, wc)
    per = wc // tn
    assert n % tm == 0 and wc % tn == 0

    def body(x_ref, gain_ref, w_ref, out_ref, h_ref):
        @pl.when(pl.program_id(1) == 0)
        def _():
            h_ref[...] = _rms(x_ref[...], gain_ref[...]).astype(BF16)

        out_ref[...] = jnp.dot(h_ref[...], w_ref[...], preferred_element_type=F32).astype(out_dtype)

    return pl.pallas_call(
        body, name=name, grid=(n // tm, g * per),
        in_specs=[pl.BlockSpec((tm, d), lambda i, j: (i, 0)),
                  pl.BlockSpec((1, d), lambda i, j: (0, 0)),
                  pl.BlockSpec((None, d, tn), lambda i, j: (j // per, 0, j % per))],
        out_specs=[pl.BlockSpec((tm, tn), lambda i, j: (i, j)),
                   pl.BlockSpec((tm, d), lambda i, j: (i, 0))],
        out_shape=[jax.ShapeDtypeStruct((n, g * wc), out_dtype), jax.ShapeDtypeStruct((n, d), BF16)],
        compiler_params=_params("parallel", "arbitrary"),
    )(x, gain, w)


def _norm_matmul_bwd(dout, w, x, gain, add, name, tm=1024, tn=512, ride=None):
    n, d = x.shape
    g, _, wc = w.shape
    tm, tn = min(tm, n), min(tn, wc)
    per = wc // tn
    ncol = g * per

    def body(dout_ref, w_ref, x_ref, gain_ref, add_ref, dx_ref, dgain_ref, acc_ref):
        i, j = pl.program_id(0), pl.program_id(1)

        @pl.when(j == 0)
        def _():
            acc_ref[...] = jnp.zeros_like(acc_ref)

        acc_ref[...] += _dot(dout_ref[...], w_ref[...], tb=True)

        @pl.when(j == ncol - 1)
        def _():
            dx, dgain = _rms_bwd(x_ref[...], gain_ref[...], acc_ref[...])
            dx_ref[...] = add_ref[...] + dx
            _acc_rows(dgain_ref, i == 0, [dgain])

    nrow = n // tm
    (dx, dgain), got = _call(
        body, [dout, w, x, gain, add], name=name, grid=(nrow, ncol),
        in_specs=[pl.BlockSpec((tm, tn), lambda i, j: (i, j)),
                  pl.BlockSpec((None, d, tn), lambda i, j: (j // per, 0, j % per)),
                  pl.BlockSpec((tm, d), lambda i, j: (i, 0)),
                  pl.BlockSpec((1, d), lambda i, j: (0, 0)),
                  pl.BlockSpec((tm, d), lambda i, j: (i, 0))],
        out_specs=[pl.BlockSpec((tm, d), lambda i, j: (i, 0)),
                   pl.BlockSpec((8, d), lambda i, j: (0, 0))],
        out_shape=[jax.ShapeDtypeStruct((n, d), F32), jax.ShapeDtypeStruct((8, d), F32)],
        scratch_shapes=[pltpu.VMEM((tm, d), F32)],
        compiler_params=_params("arbitrary", "arbitrary"),
        ride=ride, first=lambda: (pl.program_id(0) == 0) & (pl.program_id(1) == 0),
        last=lambda: (pl.program_id(0) == nrow - 1) & (pl.program_id(1) == ncol - 1))
    return (dx, dgain) if ride is None else (dx, dgain, got)


def _matmul_tn(a, b, groups, name, ta=1024, tb=1024, tk=1024):
    n, ka = a.shape
    _, kb = b.shape
    wc = kb // groups
    ta, tb, tk = min(ta, ka), min(tb, wc), min(tk, n)
    per = wc // tb
    nk = n // tk
    assert ka % ta == 0 and wc % tb == 0 and n % tk == 0

    def body(a_ref, b_ref, out_ref, acc_ref):
        k = pl.program_id(2)

        @pl.when(k == 0)
        def _():
            acc_ref[...] = jnp.zeros_like(acc_ref)

        acc_ref[...] += _dot(a_ref[...], b_ref[...], ta=True)

        @pl.when(k == nk - 1)
        def _():
            out_ref[...] = acc_ref[...].astype(BF16)

    return pl.pallas_call(
        body, name=name, grid=(ka // ta, groups * per, nk),
        in_specs=[pl.BlockSpec((tk, ta), lambda i, j, k: (k, i)),
                  pl.BlockSpec((tk, tb), lambda i, j, k: (k, j))],
        out_specs=pl.BlockSpec((None, ta, tb), lambda i, j, k: (j // per, i, j % per)),
        out_shape=jax.ShapeDtypeStruct((groups, ka, wc), BF16),
        scratch_shapes=[pltpu.VMEM((ta, tb), F32)],
        compiler_params=_params("parallel", "parallel", "arbitrary"),
    )(a, b)


def _gated_head_norm(o, gate, out_gain):
    parts = []
    for h in range(N_HEADS):
        sl = slice(h * HEAD_DIM, (h + 1) * HEAD_DIM)
        parts.append(_rms(o[:, sl], out_gain) * _silu(gate[:, sl]))
    return parts


def _out_proj(a_inputs, gated, x_in, w, gain, out_gain, name, tm=1024):
    n, d = x_in.shape
    k = w.shape[0]
    tm = min(tm, n)

    def body(*refs):
        if gated:
            o_ref, gate_ref, og_ref, x_ref, w_ref, gain_ref, xo_ref, y_ref, a_ref = refs
            parts = _gated_head_norm(o_ref[...], gate_ref[...], og_ref[...])
            for h, part in enumerate(parts):
                a_ref[:, h * HEAD_DIM:(h + 1) * HEAD_DIM] = part.astype(BF16)
            a = a_ref[...]
        else:
            a_in_ref, x_ref, w_ref, gain_ref, xo_ref, y_ref = refs
            a = a_in_ref[...]
        y = jnp.dot(a, w_ref[...], preferred_element_type=F32)
        y_ref[...] = y
        xo_ref[...] = x_ref[...] + _rms(y, gain_ref[...])

    row = lambda i: (i, 0)
    const = lambda i: (0, 0)
    if gated:
        a_specs = [pl.BlockSpec((tm, k), row), pl.BlockSpec((tm, D_MODEL), lambda i: (i, 3)),
                   pl.BlockSpec((1, HEAD_DIM), const)]
        a_args = list(a_inputs) + [out_gain]
    else:
        a_specs = [pl.BlockSpec((tm, k), row)]
        a_args = list(a_inputs)
    out_specs = [pl.BlockSpec((tm, d), row), pl.BlockSpec((tm, d), row)]
    out_shape = [jax.ShapeDtypeStruct((n, d), F32), jax.ShapeDtypeStruct((n, d), F32)]
    if gated:
        out_specs.append(pl.BlockSpec((tm, k), row))
        out_shape.append(jax.ShapeDtypeStruct((n, k), BF16))
    return pl.pallas_call(
        body, name=name, grid=(n // tm,),
        in_specs=a_specs + [pl.BlockSpec((tm, d), row), pl.BlockSpec((k, d), const), pl.BlockSpec((1, d), const)],
        out_specs=out_specs, out_shape=out_shape,
        compiler_params=_params("parallel"),
    )(*a_args, x_in, w, gain)


def _out_proj_bwd(dxo, y, w, gain, gated_inputs, name, tm=1024):
    n, d = dxo.shape
    k = w.shape[0]
    tm = min(tm, n)
    gated = gated_inputs is not None

    def body(*refs):
        if gated:
            (dxo_ref, y_ref, w_ref, gain_ref, o_ref, gate_ref, og_ref,
             dy_ref, dgain_ref, do_ref, dgate_ref, dog_ref) = refs
        else:
            dxo_ref, y_ref, w_ref, gain_ref, dy_ref, dgain_ref, da_ref = refs
        first = pl.program_id(0) == 0
        dy, dgain = _rms_bwd(y_ref[...], gain_ref[...], dxo_ref[...])
        dy_ref[...] = dy.astype(BF16)
        _acc_rows(dgain_ref, first, [dgain])
        da = _dot(dy_ref[...], w_ref[...], tb=True)
        if not gated:
            da_ref[...] = da.astype(BF16)
            return
        og = og_ref[...]
        dog = jnp.zeros_like(og)
        for h in range(N_HEADS):
            sl = slice(h * HEAD_DIM, (h + 1) * HEAD_DIM)
            fn = lambda o_h, g_h, gn: _rms(o_h, gn) * _silu(g_h)
            _, vjp = jax.vjp(fn, o_ref[:, sl], gate_ref[:, sl], og)
            do_h, dgate_h, dog_h = vjp(da[:, sl])
            do_ref[:, sl] = do_h
            dgate_ref[:, sl] = dgate_h
            dog = dog + dog_h
        _acc_rows(dog_ref, first, [dog])

    row = lambda i: (i, 0)
    const = lambda i: (0, 0)
    in_specs = [pl.BlockSpec((tm, d), row), pl.BlockSpec((tm, d), row), pl.BlockSpec((k, d), const),
                pl.BlockSpec((1, d), const)]
    args = [dxo, y, w, gain]
    out_specs = [pl.BlockSpec((tm, d), row), pl.BlockSpec((8, d), const)]
    out_shape = [jax.ShapeDtypeStruct((n, d), BF16), jax.ShapeDtypeStruct((8, d), F32)]
    if gated:
        in_specs += [pl.BlockSpec((tm, k), row), pl.BlockSpec((tm, D_MODEL), lambda i: (i, 3)),
                     pl.BlockSpec((1, HEAD_DIM), const)]
        args += list(gated_inputs)
        out_specs += [pl.BlockSpec((tm, k), row), pl.BlockSpec((tm, k), row), pl.BlockSpec((8, HEAD_DIM), const)]
        out_shape += [jax.ShapeDtypeStruct((n, k), F32), jax.ShapeDtypeStruct((n, k), F32),
                      jax.ShapeDtypeStruct((8, HEAD_DIM), F32)]
    else:
        out_specs.append(pl.BlockSpec((tm, k), row))
        out_shape.append(jax.ShapeDtypeStruct((n, k), BF16))
    return pl.pallas_call(
        body, name=name, grid=(n // tm,), in_specs=in_specs, out_specs=out_specs, out_shape=out_shape,
        compiler_params=_params("arbitrary"),
    )(*args)


def _mlp_fwd(x_in, g_pre, w_up, w_down, g_post, name, tm=1024):
    n, d = x_in.shape
    g, _, wc = w_up.shape
    tm = min(tm, n)

    def body(x_ref, gpre_ref, wup_ref, wdown_ref, gpost_ref, xo_ref, y_ref, h_ref, u_ref, a_ref, acc_ref):
        j = pl.program_id(1)

        @pl.when(j == 0)
        def _():
            h_ref[...] = _rms(x_ref[...], gpre_ref[...]).astype(BF16)
            acc_ref[...] = jnp.zeros_like(acc_ref)

        u = jnp.dot(h_ref[...], wup_ref[...], preferred_element_type=F32).astype(BF16)
        u_ref[...] = u
        a = jnp.square(jnp.maximum(u, 0))
        a_ref[...] = a
        acc_ref[...] += jnp.dot(a, wdown_ref[...], preferred_element_type=F32)

        @pl.when(j == g - 1)
        def _():
            y = acc_ref[...]
            y_ref[...] = y
            xo_ref[...] = x_ref[...] + _rms(y, gpost_ref[...])

    row = lambda i, j: (i, 0)
    const = lambda i, j: (0, 0)
    return pl.pallas_call(
        body, name=name, grid=(n // tm, g),
        in_specs=[pl.BlockSpec((tm, d), row), pl.BlockSpec((1, d), const),
                  pl.BlockSpec((None, d, wc), lambda i, j: (j, 0, 0)),
                  pl.BlockSpec((wc, d), lambda i, j: (j, 0)), pl.BlockSpec((1, d), const)],
        out_specs=[pl.BlockSpec((tm, d), row), pl.BlockSpec((tm, d), row), pl.BlockSpec((tm, d), row),
                   pl.BlockSpec((tm, wc), lambda i, j: (i, j)), pl.BlockSpec((tm, wc), lambda i, j: (i, j))],
        out_shape=[jax.ShapeDtypeStruct((n, d), F32), jax.ShapeDtypeStruct((n, d), F32),
                   jax.ShapeDtypeStruct((n, d), BF16), jax.ShapeDtypeStruct((n, g * wc), BF16),
                   jax.ShapeDtypeStruct((n, g * wc), BF16)],
        scratch_shapes=[pltpu.VMEM((tm, d), F32)],
        compiler_params=_params("parallel", "arbitrary"),
    )(x_in, g_pre, w_up, w_down, g_post)


def _mlp_bwd(dxo, y, g_post, w_down, u, w_up, x_in, g_pre, name, tm=1024):
    n, d = dxo.shape
    g, _, wc = w_up.shape

    def body(dxo_ref, y_ref, gpost_ref, wdown_ref, u_ref, wup_ref, x_ref, gpre_ref,
             dx_ref, dy_ref, du_ref, dgain_ref, acc_ref, dgpost_ref):
        i, j = pl.program_id(0), pl.program_id(1)

        @pl.when(j == 0)
        def _():
            dy, dgpost = _rms_bwd(y_ref[...], gpost_ref[...], dxo_ref[...])
            dy_ref[...] = dy.astype(BF16)
            dgpost_ref[...] = dgpost
            acc_ref[...] = jnp.zeros_like(acc_ref)

        da = _dot(dy_ref[...], wdown_ref[...], tb=True)
        du = (da * (2.0 * jnp.maximum(u_ref[...], 0).astype(F32))).astype(BF16)
        du_ref[...] = du
        acc_ref[...] += _dot(du, wup_ref[...], tb=True)

        @pl.when(j == g - 1)
        def _():
            dx, dgpre = _rms_bwd(x_ref[...], gpre_ref[...], acc_ref[...])
            dx_ref[...] = dxo_ref[...] + dx
            _acc_rows(dgain_ref, i == 0, [dgpre, dgpost_ref[...]])

    row = lambda i, j: (i, 0)
    const = lambda i, j: (0, 0)
    return pl.pallas_call(
        body, name=name, grid=(n // tm, g),
        in_specs=[pl.BlockSpec((tm, d), row), pl.BlockSpec((tm, d), row), pl.BlockSpec((1, d), const),
                  pl.BlockSpec((wc, d), lambda i, j: (j, 0)), pl.BlockSpec((tm, wc), lambda i, j: (i, j)),
                  pl.BlockSpec((None, d, wc), lambda i, j: (j, 0, 0)), pl.BlockSpec((tm, d), row),
                  pl.BlockSpec((1, d), const)],
        out_specs=[pl.BlockSpec((tm, d), row), pl.BlockSpec((tm, d), row),
                   pl.BlockSpec((tm, wc), lambda i, j: (i, j)), pl.BlockSpec((8, d), const)],
        out_shape=[jax.ShapeDtypeStruct((n, d), F32), jax.ShapeDtypeStruct((n, d), BF16),
                   jax.ShapeDtypeStruct((n, g * wc), BF16), jax.ShapeDtypeStruct((8, d), F32)],
        scratch_shapes=[pltpu.VMEM((tm, d), F32), pltpu.VMEM((1, d), F32)],
        compiler_params=_params("arbitrary", "arbitrary"),
    )(dxo, y, g_post, w_down, u, w_up, x_in, g_pre)


QKV = 3 * D_MODEL


def _shifted(x, prev8, s):
    if s == 0:
        return x
    tm = x.shape[0]
    rolled = pltpu.roll(x, s, 0)
    head = pltpu.roll(prev8, s, 0)
    head = jnp.concatenate([head, jnp.zeros((tm - 8, x.shape[1]), x.dtype)], axis=0)
    rows = lax.broadcasted_iota(jnp.int32, x.shape, 0)
    return jnp.where(rows < s, head, rolled)


def _conv(x, prev8, conv_w):
    out = x * conv_w[3:4, :]
    for s in range(1, CONV_WIDTH):
        out = out + _shifted(x, prev8, s) * conv_w[3 - s:4 - s, :]
    return out


def _l2norm(x):
    return x * lax.rsqrt(jnp.sum(x * x, axis=-1, keepdims=True) + EPS)


def _gdn_act_head(cq, ck, cv):
    return _l2norm(_silu(cq)) * (HEAD_DIM ** -0.5), _l2norm(_silu(ck)), _silu(cv)


def _gdn_gates(ba, alog, dtb):
    lane = lax.broadcasted_iota(jnp.int32, ba.shape, 1)
    beta = 1.0 / (1.0 + jnp.exp(-ba))
    t = ba + dtb
    softplus = jnp.maximum(t, 0.0) + jnp.log(1.0 + jnp.exp(-jnp.abs(t)))
    g = -jnp.exp(alog) * softplus
    return jnp.where(lane < N_HEADS, beta, jnp.where(lane < 2 * N_HEADS, g, 0.0))


def _qkv_cols(part, h):
    start = part * D_MODEL + h * HEAD_DIM
    return slice(start, start + HEAD_DIM)


def _prev_rows_spec(tm, cols, colblock, order):
    per = tm // 8
    return pl.BlockSpec((8, cols), lambda i: (jnp.maximum(order(i) * per - 1, 0), colblock))


def _gdn_pre(proj, conv_w, alog, dtb, name, tm=256):
    n = proj.shape[0]

    def body(x_ref, prev_ref, ba_ref, cw_ref, alog_ref, dtb_ref, q_ref, k_ref, v_ref, bg_ref):
        first = pl.program_id(0) == 0
        for h in range(N_HEADS):
            convs = []
            for part in range(3):
                sl = _qkv_cols(part, h)
                prev8 = jnp.where(first, 0.0, prev_ref[:, sl])
                convs.append(_conv(x_ref[:, sl], prev8, cw_ref[:, sl]))
            out = slice(h * HEAD_DIM, (h + 1) * HEAD_DIM)
            q_ref[:, out], k_ref[:, out], v_ref[:, out] = _gdn_act_head(*convs)
        bg_ref[...] = _gdn_gates(ba_ref[...], alog_ref[...], dtb_ref[...])

    row = lambda i: (i, 0)
    const = lambda i: (0, 0)
    ident = lambda i: i
    return pl.pallas_call(
        body, name=name, grid=(n // tm,),
        in_specs=[pl.BlockSpec((tm, QKV), row), _prev_rows_spec(tm, QKV, 0, ident),
                  pl.BlockSpec((tm, 128), lambda i: (i, 4 * D_MODEL // 128)),
                  pl.BlockSpec((CONV_WIDTH, QKV), const), pl.BlockSpec((1, 128), const), pl.BlockSpec((1, 128), const)],
        out_specs=[pl.BlockSpec((tm, D_MODEL), row)] * 3 + [pl.BlockSpec((tm, 128), row)],
        out_shape=[jax.ShapeDtypeStruct((n, D_MODEL), F32)] * 3 + [jax.ShapeDtypeStruct((n, 128), F32)],
        compiler_params=_params("parallel"),
    )(proj, proj, proj, conv_w, alog, dtb)


def _gdn_pre_bwd(proj, conv_w, alog, dtb, dq, dk, dv, dbg, dgate, name, tm=256):
    n = proj.shape[0]
    nt = n // tm

    def body(x_ref, prev_ref, ba_ref, cw_ref, alog_ref, dtb_ref, dq_ref, dk_ref, dv_ref, dbg_ref, dgate_ref,
             dproj_ref, dcw_ref, dgates_ref, carry_ref):
        step = pl.program_id(0)
        tile = nt - 1 - step
        @pl.when(step == 0)
        def _():
            carry_ref[...] = jnp.zeros_like(carry_ref)
            dcw_ref[...] = jnp.zeros_like(dcw_ref)

        rows = lax.broadcasted_iota(jnp.int32, (tm, HEAD_DIM), 0)
        for h in range(N_HEADS):
            cols = [_qkv_cols(part, h) for part in range(3)]
            prevs = [jnp.where(tile == 0, 0.0, prev_ref[:, sl]) for sl in cols]
            convs = [_conv(x_ref[:, sl], prev8, cw_ref[:, sl]) for sl, prev8 in zip(cols, prevs)]
            _, vjp = jax.vjp(_gdn_act_head, *convs)
            out = slice(h * HEAD_DIM, (h + 1) * HEAD_DIM)
            dcs = vjp((dq_ref[:, out], dk_ref[:, out], dv_ref[:, out]))
            for sl, prev8, dc in zip(cols, prevs, dcs):
                x = x_ref[:, sl]
                cw = cw_ref[:, sl]
                dx = dc * cw[3:4, :]
                wrapped = jnp.zeros((8, HEAD_DIM), F32)
                dcw_ref[3:4, sl] += jnp.sum(dc * x, axis=0, keepdims=True)
                for s in range(1, CONV_WIDTH):
                    r = pltpu.roll(dc, tm - s, 0) * cw[3 - s:4 - s, :]
                    dx = dx + jnp.where(rows < tm - s, r, 0.0)
                    wrapped = wrapped + jnp.where(rows[tm - 8:, :] >= tm - s, r[tm - 8:, :], 0.0)
                    dcw_ref[3 - s:4 - s, sl] += jnp.sum(dc * _shifted(x, prev8, s), axis=0, keepdims=True)
                dproj_ref[:, sl] = dx
                dproj_ref[tm - 8:, sl] += carry_ref[:, sl]
                carry_ref[:, sl] = wrapped

        _, vjp = jax.vjp(_gdn_gates, ba_ref[...], alog_ref[...], dtb_ref[...])
        dba, dalog, ddtb = vjp(dbg_ref[...])
        dproj_ref[:, QKV:4 * D_MODEL] = dgate_ref[...]
        dproj_ref[:, 4 * D_MODEL:] = dba
        _acc_rows(dgates_ref, step == 0, [dalog, ddtb])

    rev = lambda i: nt - 1 - i
    row = lambda i: (rev(i), 0)
    const = lambda i: (0, 0)
    return pl.pallas_call(
        body, name=name, grid=(nt,),
        in_specs=[pl.BlockSpec((tm, QKV), row), _prev_rows_spec(tm, QKV, 0, rev),
                  pl.BlockSpec((tm, 128), lambda i: (rev(i), 4 * D_MODEL // 128)),
                  pl.BlockSpec((CONV_WIDTH, QKV), const), pl.BlockSpec((1, 128), const), pl.BlockSpec((1, 128), const),
                  pl.BlockSpec((tm, D_MODEL), row), pl.BlockSpec((tm, D_MODEL), row), pl.BlockSpec((tm, D_MODEL), row),
                  pl.BlockSpec((tm, 128), row), pl.BlockSpec((tm, D_MODEL), row)],
        out_specs=[pl.BlockSpec((tm, GDN_IN_PAD), row), pl.BlockSpec((8, QKV), const), pl.BlockSpec((8, 128), const)],
        out_shape=[jax.ShapeDtypeStruct((n, GDN_IN_PAD), F32), jax.ShapeDtypeStruct((8, QKV), F32),
                   jax.ShapeDtypeStruct((8, 128), F32)],
        scratch_shapes=[pltpu.VMEM((8, QKV), F32)],
        compiler_params=_params("arbitrary"),
    )(proj, proj, proj, conv_w, alog, dtb, dq, dk, dv, dbg, dgate)


@jax.custom_vjp
def _unit_lower_inverses(lowers):
    c = lowers[0].shape[0]
    eye = (lax.broadcasted_iota(jnp.int32, (c, c), 0) == lax.broadcasted_iota(jnp.int32, (c, c), 1)).astype(F32)
    ys = [-low for low in lowers]
    ps = [eye + y for y in ys]
    for _ in range(int(math.log2(c)) - 1):
        ys = [_dot_3x(y, y) for y in ys]
        ps = [p + _dot_3x(p, y) for p, y in zip(ps, ys)]
    return tuple(ps)


def _unit_lower_inverses_fwd(lowers):
    ts = _unit_lower_inverses(lowers)
    return ts, ts


def _unit_lower_inverses_bwd(ts, dts):
    left = [_dot_3x(t, dt, ta=True) for t, dt in zip(ts, dts)]
    return (tuple(-_dot_3x(l, t, tb=True) for l, t in zip(left, ts)),)


_unit_lower_inverses.defvjp(_unit_lower_inverses_fwd, _unit_lower_inverses_bwd)


def _gdn_chunk(qs, ks, vs, bg, states):
    c = CHUNK
    heads = range(N_HEADS)
    row = lax.broadcasted_iota(jnp.int32, (c, c), 0)
    col = lax.broadcasted_iota(jnp.int32, (c, c), 1)
    incl, strict, eye = row >= col, row > col, row == col
    lane = lax.broadcasted_iota(jnp.int32, (c, 128), 1)
    rowc = lax.broadcasted_iota(jnp.int32, (c, 1), 0)
    gc_all = _mm_f32(incl.astype(F32), bg, False, False)
    beta = [jnp.sum(jnp.where(lane == h, bg, 0.0), axis=1, keepdims=True) for h in heads]
    gc = [jnp.sum(jnp.where(lane == N_HEADS + h, gc_all, 0.0), axis=1, keepdims=True) for h in heads]
    gc_row = [jnp.sum(jnp.where(eye, gc[h], 0.0), axis=0, keepdims=True) for h in heads]
    gc_last = [jnp.sum(jnp.where(rowc == c - 1, gc[h], 0.0), axis=0, keepdims=True) for h in heads]
    decay = [jnp.where(incl, jnp.exp(jnp.where(incl, gc[h] - gc_row[h], 0.0)), 0.0) for h in heads]
    kb = [ks[h] * beta[h] for h in heads]
    lower = tuple(jnp.where(strict, _mm(kb[h], ks[h], False, True) * decay[h], 0.0) for h in heads)
    attn = [_mm(qs[h], ks[h], False, True) * decay[h] for h in heads]
    t_mat = _unit_lower_inverses(lower)
    egc = [jnp.exp(gc[h]) for h in heads]
    w = [_mm(t_mat[h], kb[h] * egc[h], False, False) for h in heads]
    u = [_mm(t_mat[h], vs[h] * beta[h], False, False) for h in heads]
    kg = [ks[h] * jnp.exp(gc_last[h] - gc[h]) for h in heads]
    v_new = [u[h] - _mm(w[h], states[h], False, False) for h in heads]
    from_state = [_mm(qs[h] * egc[h], states[h], False, False) for h in heads]
    outs = [from_state[h] + _mm(attn[h], v_new[h], False, False) for h in heads]
    new_states = [states[h] * jnp.exp(gc_last[h]) + _mm(kg[h], v_new[h], True, False) for h in heads]
    return tuple(outs), tuple(new_states)


def _head_slices(ref):
    return tuple(ref[:, h * HEAD_DIM:(h + 1) * HEAD_DIM] for h in range(N_HEADS))


def _gdn_scan(qn, kn, v, bg, name, ride=None):
    n = qn.shape[0]
    nc = n // CHUNK

    def body(q_ref, k_ref, v_ref, bg_ref, o_ref, saved_ref, state_ref):
        @pl.when(pl.program_id(0) == 0)
        def _():
            state_ref[...] = jnp.zeros_like(state_ref)

        states = tuple(state_ref[h] for h in range(N_HEADS))
        for h in range(N_HEADS):
            saved_ref[h] = states[h]
        outs, new_states = _gdn_chunk(_head_slices(q_ref), _head_slices(k_ref), _head_slices(v_ref), bg_ref[...], states)
        for h in range(N_HEADS):
            o_ref[:, h * HEAD_DIM:(h + 1) * HEAD_DIM] = outs[h]
            state_ref[h] = new_states[h]

    row = lambda i: (i, 0)
    return _call(
        body, [qn, kn, v, bg], name=name, grid=(nc,),
        in_specs=[pl.BlockSpec((CHUNK, D_MODEL), row)] * 3 + [pl.BlockSpec((CHUNK, 128), row)],
        out_specs=[pl.BlockSpec((CHUNK, D_MODEL), row),
                   pl.BlockSpec((None, N_HEADS, HEAD_DIM, HEAD_DIM), lambda i: (i, 0, 0, 0))],
        out_shape=[jax.ShapeDtypeStruct((n, D_MODEL), F32),
                   jax.ShapeDtypeStruct((nc, N_HEADS, HEAD_DIM, HEAD_DIM), F32)],
        scratch_shapes=[pltpu.VMEM((N_HEADS, HEAD_DIM, HEAD_DIM), F32)],
        compiler_params=_params("arbitrary"),
        ride=ride, first=lambda: pl.program_id(0) == 0, last=lambda: pl.program_id(0) == nc - 1)


def _gdn_scan_bwd(qn, kn, v, bg, saved, do, name, ride=None):
    n = qn.shape[0]
    nc = n // CHUNK

    def body(q_ref, k_ref, v_ref, bg_ref, saved_ref, do_ref, dq_ref, dk_ref, dv_ref, dbg_ref, dstate_ref):
        @pl.when(pl.program_id(0) == 0)
        def _():
            dstate_ref[...] = jnp.zeros_like(dstate_ref)

        states = tuple(saved_ref[h] for h in range(N_HEADS))
        _, vjp = jax.vjp(_gdn_chunk, _head_slices(q_ref), _head_slices(k_ref), _head_slices(v_ref), bg_ref[...], states)
        dstates = tuple(dstate_ref[h] for h in range(N_HEADS))
        dqs, dks, dvs, dbg, dprev = vjp((_head_slices(do_ref), dstates))
        for h in range(N_HEADS):
            sl = slice(h * HEAD_DIM, (h + 1) * HEAD_DIM)
            dq_ref[:, sl] = dqs[h]
            dk_ref[:, sl] = dks[h]
            dv_ref[:, sl] = dvs[h]
            dstate_ref[h] = dprev[h]
        dbg_ref[...] = dbg

    row = lambda i: (nc - 1 - i, 0)
    return _call(
        body, [qn, kn, v, bg, saved, do], name=name, grid=(nc,),
        in_specs=[pl.BlockSpec((CHUNK, D_MODEL), row)] * 3 + [pl.BlockSpec((CHUNK, 128), row),
                  pl.BlockSpec((None, N_HEADS, HEAD_DIM, HEAD_DIM), lambda i: (nc - 1 - i, 0, 0, 0)),
                  pl.BlockSpec((CHUNK, D_MODEL), row)],
        out_specs=[pl.BlockSpec((CHUNK, D_MODEL), row)] * 3 + [pl.BlockSpec((CHUNK, 128), row)],
        out_shape=[jax.ShapeDtypeStruct((n, D_MODEL), F32)] * 3 + [jax.ShapeDtypeStruct((n, 128), F32)],
        scratch_shapes=[pltpu.VMEM((N_HEADS, HEAD_DIM, HEAD_DIM), F32)],
        compiler_params=_params("arbitrary"),
        ride=ride, first=lambda: pl.program_id(0) == 0, last=lambda: pl.program_id(0) == nc - 1)


SB_BQ = 512
SB_SUB = 128
SB_ROWS = 128
SB_SCALE = HEAD_DIM ** -0.5


def _sb_terms(z, before):
    e = jnp.exp(-jnp.abs(z))
    log_beta = jnp.minimum(z, 0.0) - jnp.log(1.0 + e)
    log_1m = log_beta - z
    if before is not None:
        log_1m = jnp.where(before, log_1m, 0.0)
    return e, log_beta, log_1m


def _tri_ones(n, cmp):
    r = lax.broadcasted_iota(jnp.int32, (n, 2 * n), 0)
    c = lax.broadcasted_iota(jnp.int32, (n, 2 * n), 1)
    return jnp.where((c >= n) | cmp(r, c), 1.0, 0.0).astype(BF16)


def _sums(x, tri_ones):
    both = jnp.dot(x.astype(BF16), tri_ones, preferred_element_type=F32)
    n = x.shape[1]
    return both[:, :n], both[:, n:]


def _sb_chunk_mask(diagonal, r, s):
    if not diagonal or s * SB_SUB + SB_SUB - 1 < r * SB_ROWS:
        return None
    if s * SB_SUB >= r * SB_ROWS + SB_ROWS - 1:
        return "empty"
    rows = r * SB_ROWS + lax.broadcasted_iota(jnp.int32, (SB_ROWS, SB_SUB), 0)
    cols = s * SB_SUB + lax.broadcasted_iota(jnp.int32, (SB_ROWS, SB_SUB), 1)
    return cols < rows


def _sb_attention(q, kv, name, bq=SB_BQ, ride=None):
    n = q.shape[0]
    bq = min(bq, n)
    nsub, nrc = bq // SB_SUB, bq // SB_ROWS

    def body(q_ref, k_ref, v_ref, o_ref, l_ref, z_scr, a_scr):
        i = pl.program_id(1)
        qb = q_ref[...]
        after = _tri_ones(SB_SUB, lambda r, c: r > c)

        def span(j, c_sum, acc, diagonal):
            start = pl.multiple_of(j * bq, bq)
            k_w = k_ref[pl.ds(start, bq), :]
            v_w = v_ref[pl.ds(start, bq), :]
            z_scr[...] = lax.dot_general(qb, k_w, _dims(False, True), preferred_element_type=F32)
            c_rows = [c_sum[r * SB_ROWS:(r + 1) * SB_ROWS] for r in range(nrc)]
            for s in reversed(range(nsub)):
                cols = slice(s * SB_SUB, (s + 1) * SB_SUB)
                for r in range(nrc):
                    rows = slice(r * SB_ROWS, (r + 1) * SB_ROWS)
                    before = _sb_chunk_mask(diagonal, r, s)
                    if isinstance(before, str):
                        a_scr[rows, cols] = jnp.zeros((SB_ROWS, SB_SUB), BF16)
                        continue
                    _, log_beta, log_1m = _sb_terms(z_scr[rows, cols] * SB_SCALE, before)
                    tail, total = _sums(log_1m, after)
                    a = jnp.exp(log_beta + c_rows[r] + tail)
                    if before is not None:
                        a = jnp.where(before, a, 0.0)
                    a_scr[rows, cols] = a.astype(BF16)
                    c_rows[r] = c_rows[r] + total
            acc = acc + jnp.dot(a_scr[...], v_w, preferred_element_type=F32)
            return jnp.concatenate(c_rows, axis=0), acc

        carry = span(i, jnp.zeros((bq, SB_SUB), F32), jnp.zeros((bq, HEAD_DIM), F32), True)
        c_sum, acc = lax.fori_loop(0, i, lambda jj, c: span(i - 1 - jj, c[0], c[1], False), carry)
        o_ref[...] = acc.astype(BF16)
        l_ref[...] = c_sum

    nq = n // bq
    return _call(
        body, [q, kv, kv], name=name, grid=(N_HEADS, nq),
        in_specs=[pl.BlockSpec((bq, HEAD_DIM), lambda h, i: (i, h)),
                  pl.BlockSpec((n, HEAD_DIM), lambda h, i: (0, h)),
                  pl.BlockSpec((n, HEAD_DIM), lambda h, i: (0, N_HEADS + h))],
        out_specs=[pl.BlockSpec((bq, HEAD_DIM), lambda h, i: (i, h)),
                   pl.BlockSpec((None, bq, 128), lambda h, i: (h, i, 0))],
        out_shape=[jax.ShapeDtypeStruct((n, D_MODEL), BF16), jax.ShapeDtypeStruct((N_HEADS, n, 128), F32)],
        scratch_shapes=[pltpu.VMEM((bq, bq), F32), pltpu.VMEM((bq, bq), BF16)],
        compiler_params=_params("arbitrary", "arbitrary"),
        ride=ride, first=lambda: (pl.program_id(0) == 0) & (pl.program_id(1) == 0),
        last=lambda: (pl.program_id(0) == N_HEADS - 1) & (pl.program_id(1) == nq - 1))


def _sb_attention_bwd(q, kv, do, ltot, name, bq=SB_BQ, ride=None):
    n = q.shape[0]
    bq = min(bq, n)
    nsub, nrc = bq // SB_SUB, bq // SB_ROWS

    def body(q_ref, k_ref, v_ref, do_ref, l_ref, dq_ref, dk_ref, dv_ref, z_scr, da_scr, a_scr, dz_scr):
        i = pl.program_id(1)

        @pl.when(i == 0)
        def _():
            dk_ref[...] = jnp.zeros_like(dk_ref)
            dv_ref[...] = jnp.zeros_like(dv_ref)

        qb = q_ref[...]
        dob = do_ref[...]
        lt_rows = [l_ref[r * SB_ROWS:(r + 1) * SB_ROWS, :] for r in range(nrc)]
        upto = _tri_ones(SB_SUB, lambda r, c: r <= c)
        below = _tri_ones(SB_SUB, lambda r, c: r < c)

        def span(j, l_sum, g_sum, dq, diagonal):
            start = pl.multiple_of(j * bq, bq)
            k_w = k_ref[pl.ds(start, bq), :]
            v_w = v_ref[pl.ds(start, bq), :]
            z_scr[...] = lax.dot_general(qb, k_w, _dims(False, True), preferred_element_type=F32)
            da_scr[...] = lax.dot_general(dob, v_w, _dims(False, True), preferred_element_type=F32)
            l_rows = [l_sum[r * SB_ROWS:(r + 1) * SB_ROWS] for r in range(nrc)]
            g_rows = [g_sum[r * SB_ROWS:(r + 1) * SB_ROWS] for r in range(nrc)]
            for s in range(nsub):
                cols = slice(s * SB_SUB, (s + 1) * SB_SUB)
                for r in range(nrc):
                    rows = slice(r * SB_ROWS, (r + 1) * SB_ROWS)
                    before = _sb_chunk_mask(diagonal, r, s)
                    if isinstance(before, str):
                        a_scr[rows, cols] = jnp.zeros((SB_ROWS, SB_SUB), BF16)
                        dz_scr[rows, cols] = jnp.zeros((SB_ROWS, SB_SUB), BF16)
                        continue
                    zs = z_scr[rows, cols] * SB_SCALE
                    e, log_beta, log_1m = _sb_terms(zs, before)
                    l_prefix, l_total = _sums(log_1m, upto)
                    a = jnp.exp(log_beta + (lt_rows[r] - (l_rows[r] + l_prefix)))
                    if before is not None:
                        a = jnp.where(before, a, 0.0)
                    g = a * da_scr[rows, cols]
                    g_prefix, g_total = _sums(g, below)
                    inv = 1.0 / (1.0 + e)
                    beta = jnp.where(zs >= 0, inv, e * inv)
                    dz = (g - (g + g_rows[r] + g_prefix) * beta) * SB_SCALE
                    if before is not None:
                        dz = jnp.where(before, dz, 0.0)
                    a_scr[rows, cols] = a.astype(BF16)
                    dz_scr[rows, cols] = dz.astype(BF16)
                    l_rows[r] = l_rows[r] + l_total
                    g_rows[r] = g_rows[r] + g_total
            dz_w = dz_scr[...]
            dq = dq + jnp.dot(dz_w, k_w, preferred_element_type=F32)
            dk_ref[pl.ds(start, bq), :] += lax.dot_general(dz_w, qb, _dims(True, False), preferred_element_type=F32)
            dv_ref[pl.ds(start, bq), :] += lax.dot_general(a_scr[...], dob, _dims(True, False), preferred_element_type=F32)
            return jnp.concatenate(l_rows, axis=0), jnp.concatenate(g_rows, axis=0), dq

        zero = jnp.zeros((bq, SB_SUB), F32)
        carry = lax.fori_loop(0, i, lambda j, c: span(j, c[0], c[1], c[2], False),
                              (zero, zero, jnp.zeros((bq, HEAD_DIM), F32)))
        _, _, dq = span(i, carry[0], carry[1], carry[2], True)
        dq_ref[...] = dq.astype(BF16)

    nq = n // bq
    return _call(
        body, [q, kv, kv, do, ltot], name=name, grid=(N_HEADS, nq),
        in_specs=[pl.BlockSpec((bq, HEAD_DIM), lambda h, i: (i, h)),
                  pl.BlockSpec((n, HEAD_DIM), lambda h, i: (0, h)),
                  pl.BlockSpec((n, HEAD_DIM), lambda h, i: (0, N_HEADS + h)),
                  pl.BlockSpec((bq, HEAD_DIM), lambda h, i: (i, h)),
                  pl.BlockSpec((None, bq, 128), lambda h, i: (h, i, 0))],
        out_specs=[pl.BlockSpec((bq, HEAD_DIM), lambda h, i: (i, h)),
                   pl.BlockSpec((n, HEAD_DIM), lambda h, i: (0, h)),
                   pl.BlockSpec((n, HEAD_DIM), lambda h, i: (0, h))],
        out_shape=[jax.ShapeDtypeStruct((n, D_MODEL), BF16), jax.ShapeDtypeStruct((n, D_MODEL), F32),
                   jax.ShapeDtypeStruct((n, D_MODEL), F32)],
        scratch_shapes=[pltpu.VMEM((bq, bq), F32), pltpu.VMEM((bq, bq), F32), pltpu.VMEM((bq, bq), BF16),
                        pltpu.VMEM((bq, bq), BF16)],
        compiler_params=_params("arbitrary", "arbitrary"),
        ride=ride, first=lambda: (pl.program_id(0) == 0) & (pl.program_id(1) == 0),
        last=lambda: (pl.program_id(0) == N_HEADS - 1) & (pl.program_id(1) == nq - 1))


def _loss_head(y, target, name, tm=1024):
    n, d = y.shape

    def body(y_ref, t_ref, dy_ref, loss_ref):
        err = y_ref[...] - t_ref[...]
        dy_ref[...] = err * (1.0 / d)
        part = 0.5 * jnp.sum(jnp.mean(err * err, axis=-1, keepdims=True), axis=0, keepdims=True)
        _acc_rows(loss_ref, pl.program_id(0) == 0, [jnp.broadcast_to(part, (1, 128))])

    row = lambda i: (i, 0)
    return pl.pallas_call(
        body, name=name, grid=(n // tm,),
        in_specs=[pl.BlockSpec((tm, d), row), pl.BlockSpec((tm, d), row)],
        out_specs=[pl.BlockSpec((tm, d), row), pl.BlockSpec((8, 128), lambda i: (0, 0))],
        out_shape=[jax.ShapeDtypeStruct((n, d), F32), jax.ShapeDtypeStruct((8, 128), F32)],
        compiler_params=_params("arbitrary"),
    )(y, target)


def _local_step(x, target, gains, comm):
    g = gains
    w = comm.w
    big = {}
    row = lambda a, i: a[i:i + 1, :]

    proj, h0 = _norm_matmul(x, row(g["mix_pre"], 0), w["gdn_in"], F32, "gdn_in_proj", tn=1408)
    qn, kn, v, bg = _gdn_pre(proj, w["conv"], g["alog"], g["dtb"], "gdn_pre")
    (o_gdn, saved), got = _gdn_scan(qn, kn, v, bg, "gdn_scan", ride=comm.ride("scan"))
    comm.done("scan", got)
    x1, y_mix0, a_gdn = _out_proj((o_gdn, proj), True, x, w["gdn_out"], row(g["mix_post"], 0), g["out_gain"],
                                  "gdn_out_proj", tm=512)
    x2, y_mlp0, h_mlp0, u0, a0 = _mlp_fwd(x1, row(g["mlp_pre"], 0), w["up0"], w["down0"], row(g["mlp_post"], 0), "mlp0")
    kv, h_kv = _norm_matmul(x2, g["kv"], w["kv"], BF16, "kv_proj", tm=2048)
    q, h_q = _norm_matmul(x2, row(g["mix_pre"], 1), w["sb_q"], BF16, "sb_q_proj")
    (o_sb, ltot), got = _sb_attention(q, kv, "sb_attention", ride=comm.ride("sb"))
    comm.done("sb", got)
    x3, y_mix1 = _out_proj((o_sb,), False, x2, w["sb_o"], row(g["mix_post"], 1), None, "sb_out_proj")
    x4, y_mlp1, h_mlp1, u1, a1 = _mlp_fwd(x3, row(g["mlp_pre"], 1), w["up1"], w["down1"], row(g["mlp_post"], 1), "mlp1")
    dx4, loss = _loss_head(x4, target, "loss_head")

    dx3, dy_mlp1, du1, dg_mlp1 = _mlp_bwd(dx4, y_mlp1, row(g["mlp_post"], 1), w["down1"], u1, w["up1"], x3,
                                          row(g["mlp_pre"], 1), "mlp1_bwd")
    big["down1"] = _matmul_tn(a1, dy_mlp1, 1, "d_down1")[0]
    big["up1"] = _matmul_tn(h_mlp1, du1, N_DEV, "d_up1", tk=2048)
    dy_mix1, dg_post1, do_sb = _out_proj_bwd(dx3, y_mix1, w["sb_o"], row(g["mix_post"], 1), None, "sb_out_proj_bwd")
    big["sb_o"] = _matmul_tn(o_sb, dy_mix1, 1, "d_sb_o")[0]
    (dq, dk, dv), got = _sb_attention_bwd(q, kv, do_sb, ltot, "sb_attention_bwd", ride=comm.ride("sb_bwd", big))
    comm.done("sb_bwd", got)
    big["sb_q"] = _matmul_tn(h_q, dq, 1, "d_sb_q")[0]
    big["kv"] = jnp.concatenate([_matmul_tn(h_kv, dk, N_DEV // 2, "d_w_k", tk=2048),
                                 _matmul_tn(h_kv, dv, N_DEV // 2, "d_w_v", tk=2048)],
                                axis=0)
    t, dg_pre1 = _norm_matmul_bwd(dq, w["sb_q"], x2, row(g["mix_pre"], 1), dx3, "sb_q_proj_bwd")
    t, dg_kv_k = _norm_matmul_bwd(dk, w["kv"][:N_DEV // 2], x2, g["kv"], t, "k_proj_bwd", tn=256)
    dx2, dg_kv_v = _norm_matmul_bwd(dv, w["kv"][N_DEV // 2:], x2, g["kv"], t, "v_proj_bwd", tn=256)

    dx1, dy_mlp0, du0, dg_mlp0 = _mlp_bwd(dx2, y_mlp0, row(g["mlp_post"], 0), w["down0"], u0, w["up0"], x1,
                                          row(g["mlp_pre"], 0), "mlp0_bwd")
    big["down0"] = _matmul_tn(a0, dy_mlp0, 1, "d_down0")[0]
    big["up0"] = _matmul_tn(h_mlp0, du0, N_DEV, "d_up0", tk=2048)
    dy_mix0, dg_post0, do_gdn, dgate, d_out_gain = _out_proj_bwd(
        dx1, y_mix0, w["gdn_out"], row(g["mix_post"], 0), (o_gdn, proj, g["out_gain"]), "gdn_out_proj_bwd", tm=512)
    big["gdn_out"] = _matmul_tn(a_gdn, dy_mix0, 1, "d_gdn_out")[0]
    (dqn, dkn, dvv, dbg), got = _gdn_scan_bwd(qn, kn, v, bg, saved, do_gdn, "gdn_scan_bwd",
                                              ride=comm.ride("scan_bwd", big))
    comm.done("scan_bwd", got)
    dproj, d_conv, d_gates = _gdn_pre_bwd(proj, w["conv"], g["alog"], g["dtb"], dqn, dkn, dvv, dbg, dgate, "gdn_pre_bwd")
    big["gdn_in"] = _matmul_tn(h0, dproj, 1, "d_gdn_in", tb=1408, tk=512)[0]
    ride = comm.ride("in_bwd", big)
    grad_x, dg_pre0, *got = _norm_matmul_bwd(dproj, w["gdn_in"], x, row(g["mix_pre"], 0), dx1, "gdn_in_proj_bwd", tn=1408,
                                             ride=ride)
    comm.done("in_bwd", got[0] if got else [])

    small = {"pre0": dg_pre0, "pre1": dg_pre1, "post0": dg_post0, "post1": dg_post1, "mlp0": dg_mlp0, "mlp1": dg_mlp1,
             "kv_k": dg_kv_k, "kv_v": dg_kv_v, "gates": d_gates, "out_gain": d_out_gain, "conv": d_conv, "loss": loss}
    return grad_x, big, small


class _LocalOnly:
    def __init__(self, weights):
        self.w = weights

    def ride(self, stage, grads=None):
        return None

    def done(self, stage, outs):
        pass


GATHER_STAGES = {"scan": ("gdn_out", "up0", "down0", "kv", "sb_q"), "sb": ("sb_o", "up1", "down1")}
SCATTER_STAGES = {"sb_bwd": ("up1", "down1", "sb_o"), "scan_bwd": ("sb_q", "kv", "up0", "down0", "gdn_out"),
                  "in_bwd": ("gdn_in",)}


def _whole_weight(name, gathered):
    d = D_MODEL
    if name in ("up0", "up1", "kv"):
        return gathered
    if name == "gdn_in":
        whole = gathered.transpose(1, 0, 2).reshape(d, GDN_IN_COLS)
        return jnp.pad(whole, ((0, 0), (0, GDN_IN_PAD - GDN_IN_COLS)))[None]
    if name == "conv":
        return gathered.transpose(1, 0, 2).reshape(CONV_WIDTH, QKV)
    whole = gathered.reshape(gathered.shape[0] * gathered.shape[1], d)
    return whole[None] if name == "sb_q" else whole


def _owner_blocks(name, grad):
    if name in ("up0", "up1", "kv"):
        blocks = grad
    elif name == "gdn_in":
        blocks = grad[:, :GDN_IN_COLS].reshape(D_MODEL, N_DEV, GDN_IN_COLS // N_DEV).transpose(1, 0, 2)
    else:
        blocks = grad.reshape(N_DEV, grad.shape[0] // N_DEV, grad.shape[1])
    return blocks.astype(BF16)


class _Fsdp:
    def __init__(self, shards, weights):
        self.shards, self.w, self.landed = shards, weights, {}

    def ride(self, stage, grads=None):
        if stage in GATHER_STAGES:
            names = GATHER_STAGES[stage]
            return _Exchange([self.shards[nm] for nm in names], [True] * len(names))
        names = SCATTER_STAGES[stage]
        return _Exchange([_owner_blocks(nm, grads[nm]) for nm in names], [False] * len(names))

    def done(self, stage, outs):
        if stage in GATHER_STAGES:
            for nm, out in zip(GATHER_STAGES[stage], outs):
                self.w[nm] = _whole_weight(nm, out)
        else:
            self.landed.update(zip(SCATTER_STAGES[stage], outs))


def _my_place():
    return lax.axis_index("x"), lax.axis_index("y"), lax.axis_index("c")


class _Exchange:
    def __init__(self, arrays, gather):
        self.arrays, self.gather, self.n = list(arrays), list(gather), len(arrays)
        any_spec = pl.BlockSpec(memory_space=pl.ANY)
        self.in_specs = [any_spec] * self.n
        self.out_specs = [any_spec] * self.n
        self.out_shape = [jax.ShapeDtypeStruct(((N_DEV,) + a.shape) if g else a.shape, a.dtype)
                          for a, g in zip(self.arrays, self.gather)]
        self.scratch = [pltpu.SemaphoreType.DMA((self.n, N_DEV - 1)), pltpu.SemaphoreType.DMA((self.n, N_DEV - 1)),
                        pltpu.SemaphoreType.DMA((self.n,))]

    def _copies(self, ins, outs, sems):
        send_sems, recv_sems, local_sems = sems
        x, y, c = _my_place()
        me = 4 * x + 2 * y + c
        copies = []
        for a in range(self.n):
            src = ins[a] if self.gather[a] else ins[a].at[me]
            copies.append(pltpu.make_async_copy(src, outs[a].at[me], local_sems.at[a]))
        for k in range(1, N_DEV):
            px = 1 - x if k & 4 else x
            py = 1 - y if k & 2 else y
            pc = 1 - c if k & 1 else c
            peer = 4 * px + 2 * py + pc
            for a in range(self.n):
                src = ins[a] if self.gather[a] else ins[a].at[peer]
                copies.append(pltpu.make_async_remote_copy(
                    src_ref=src, dst_ref=outs[a].at[me], send_sem=send_sems.at[a, k - 1], recv_sem=recv_sems.at[a, k - 1],
                    device_id=(px, py, pc), device_id_type=MESH))
        return copies

    def start(self, ins, outs, sems):
        for cp in self._copies(ins, outs, sems):
            cp.start()

    def wait(self, ins, outs, sems):
        for cp in self._copies(ins, outs, sems):
            cp.wait()


def _call(body, operands, *, name, out_shape, in_specs, out_specs, grid=(), scratch_shapes=(), compiler_params=None,
          ride=None, first=None, last=None):
    n_in, n_out, n_scr = len(operands), len(out_shape), len(scratch_shapes)
    if ride is None:
        outs = pl.pallas_call(body, name=name, grid=grid, in_specs=in_specs, out_specs=out_specs, out_shape=out_shape,
                              scratch_shapes=scratch_shapes, compiler_params=compiler_params)(*operands)
        return list(outs), []
    r = ride.n

    def riding(*refs):
        ins, r_ins = refs[:n_in], refs[n_in:n_in + r]
        outs, r_outs = refs[n_in + r:n_in + r + n_out], refs[n_in + r + n_out:n_in + 2 * r + n_out]
        scr, sems = refs[n_in + 2 * r + n_out:n_in + 2 * r + n_out + n_scr], refs[n_in + 2 * r + n_out + n_scr:]

        @pl.when(first())
        def _():
            ride.start(r_ins, r_outs, sems)

        body(*ins, *outs, *scr)

        @pl.when(last())
        def _():
            ride.wait(r_ins, r_outs, sems)

    outs = pl.pallas_call(
        riding, name=name, grid=grid, in_specs=list(in_specs) + ride.in_specs, out_specs=list(out_specs) + ride.out_specs,
        out_shape=list(out_shape) + ride.out_shape, scratch_shapes=list(scratch_shapes) + ride.scratch,
        compiler_params=compiler_params)(*operands, *ride.arrays)
    return list(outs[:n_out]), list(outs[n_out:])


def _exchange(arrays, gather, name):
    ride = _Exchange(arrays, gather)

    def body(*refs):
        ins, outs, sems = refs[:ride.n], refs[ride.n:2 * ride.n], refs[2 * ride.n:]
        ride.start(ins, outs, sems)
        ride.wait(ins, outs, sems)

    return pl.pallas_call(body, name=name, in_specs=ride.in_specs, out_specs=ride.out_specs, out_shape=ride.out_shape,
                          scratch_shapes=ride.scratch)(*arrays)


def _adamw_math(g, w, m, v):
    m = ADAM_B1 * m + (1.0 - ADAM_B1) * g
    v = ADAM_B2 * v + (1.0 - ADAM_B2) * jnp.square(g)
    m_hat = m / (1.0 - ADAM_B1 ** ADAM_STEP)
    v_hat = v / (1.0 - ADAM_B2 ** ADAM_STEP)
    delta = -ADAM_LR * (m_hat / (jnp.sqrt(v_hat) + ADAM_EPS) + ADAM_WD * w)
    return delta, m, v


def _sum_devices(ref):
    total = ref[0].astype(F32)
    for d in range(1, N_DEV):
        total = total + ref[d].astype(F32)
    return total


def _reduce_adamw(landed, w, m, v, name, tr=256):
    r, c = w.shape
    tr = min(tr, r)
    assert r % tr == 0

    def body(l_ref, w_ref, m_ref, v_ref, g_ref, d_ref, nm_ref, nv_ref):
        g = _sum_devices(l_ref)
        g_ref[...] = g
        d_ref[...], nm_ref[...], nv_ref[...] = _adamw_math(g, w_ref[...], m_ref[...], v_ref[...])

    blk = pl.BlockSpec((tr, c), lambda i: (i, 0))
    return pl.pallas_call(
        body, name=name, grid=(r // tr,),
        in_specs=[pl.BlockSpec((N_DEV, tr, c), lambda i: (0, i, 0)), blk, blk, blk],
        out_specs=[blk] * 4, out_shape=[jax.ShapeDtypeStruct((r, c), F32)] * 4,
        compiler_params=_params("parallel"),
    )(landed, w, m, v)


def _adamw(g, w, m, v, name):
    def body(g_ref, w_ref, m_ref, v_ref, d_ref, nm_ref, nv_ref):
        d_ref[...], nm_ref[...], nv_ref[...] = _adamw_math(g_ref[...], w_ref[...], m_ref[...], v_ref[...])

    return pl.pallas_call(body, name=name, out_shape=[jax.ShapeDtypeStruct(w.shape, F32)] * 3)(g, w, m, v)


SMALL_ROWS = ("mix_pre", "mix_post", "mlp_pre", "mlp_post")


def _small_update(landed, params, name):
    layout = {
        "mix_pre": [("pre0", 0), ("pre1", 0)], "mix_post": [("post0", 0), ("post1", 0)],
        "mlp_pre": [("mlp0", 0), ("mlp1", 0)], "mlp_post": [("mlp0", 1), ("mlp1", 1)],
        "kv": [("kv_k", 0)], "alog": [("gates", 0)], "dtb": [("gates", 1)], "out_gain": [("out_gain", 0)],
    }
    landed_names = sorted(landed)
    param_names = sorted(params)
    n_l, n_p = len(landed_names), len(param_names)

    def body(*refs):
        l_refs = dict(zip(landed_names, refs[:n_l]))
        p_refs = {p: refs[n_l + 3 * i:n_l + 3 * i + 3] for i, p in enumerate(param_names)}
        outs = refs[n_l + 3 * n_p:]
        o_refs = {p: outs[4 * i:4 * i + 4] for i, p in enumerate(param_names)}
        conv_ref, loss_ref = outs[4 * n_p:]
        sums = {nm: _sum_devices(l_refs[nm]) for nm in landed_names}
        sums["kv_k"] = sums["kv_k"] + sums["kv_v"]
        for p in param_names:
            w_ref, m_ref, v_ref = p_refs[p]
            g_ref, d_ref, nm_ref, nv_ref = o_refs[p]
            for r, (src, src_row) in enumerate(layout[p]):
                g = sums[src][src_row:src_row + 1, :]
                g_ref[r:r + 1, :] = g
                d, nm, nv = _adamw_math(g, w_ref[r:r + 1, :], m_ref[r:r + 1, :], v_ref[r:r + 1, :])
                d_ref[r:r + 1, :] = d
                nm_ref[r:r + 1, :] = nm
                nv_ref[r:r + 1, :] = nv
        conv_ref[...] = sums["conv"]
        loss_ref[...] = sums["loss"]

    args = [landed[nm] for nm in landed_names]
    out_shape = []
    for p in param_names:
        args += list(params[p])
        out_shape += [jax.ShapeDtypeStruct(params[p][0].shape, F32)] * 4
    out_shape += [jax.ShapeDtypeStruct(landed["conv"].shape[1:], F32), jax.ShapeDtypeStruct(landed["loss"].shape[1:], F32)]
    outs = pl.pallas_call(body, name=name, out_shape=out_shape)(*args)
    result = {p: tuple(outs[4 * i:4 * i + 4]) for i, p in enumerate(param_names)}
    result["conv"], result["loss"] = outs[4 * n_p], outs[4 * n_p + 1]
    return result


def _lanes(vec, offset):
    return jnp.pad(vec[None, :], ((0, 0), (offset, 128 - offset - vec.shape[0])))


def kernel(x, mix_pre_gain, mix_post_gain, mlp_pre_gain, mlp_post_gain, mlp_w_up, mlp_w_down, gdn_w_in, gdn_conv_w, gdn_a_log, gdn_dt_bias, gdn_out_gain, gdn_w_out, kv_gain, w_kv, sb_w_q, sb_w_o, loss_target, m_mix_pre_gain, m_mix_post_gain, m_mlp_pre_gain, m_mlp_post_gain, m_mlp_w_up, m_mlp_w_down, m_gdn_w_in, m_gdn_conv_w, m_gdn_a_log, m_gdn_dt_bias, m_gdn_out_gain, m_gdn_w_out, m_kv_gain, m_w_kv, m_sb_w_q, m_sb_w_o, v_mix_pre_gain, v_mix_post_gain, v_mlp_pre_gain, v_mlp_post_gain, v_mlp_w_up, v_mlp_w_down, v_gdn_w_in, v_gdn_conv_w, v_gdn_a_log, v_gdn_dt_bias, v_gdn_out_gain, v_gdn_w_out, v_kv_gain, v_w_kv, v_sb_w_q, v_sb_w_o):
    me = 4 * lax.axis_index("x") + 2 * lax.axis_index("y") + lax.axis_index("c")
    bf = lambda a: a.astype(BF16)

    shards = {"up0": bf(mlp_w_up[0]), "up1": bf(mlp_w_up[1]), "down0": bf(mlp_w_down[0]), "down1": bf(mlp_w_down[1]),
              "gdn_out": bf(gdn_w_out[0]), "kv": bf(w_kv), "sb_q": bf(sb_w_q[0]), "sb_o": bf(sb_w_o[0])}
    gdn_in, conv = _exchange([bf(gdn_w_in[0]), gdn_conv_w[0]], [True, True], "gather_first_weights")
    comm = _Fsdp(shards, {"gdn_in": _whole_weight("gdn_in", gdn_in), "conv": _whole_weight("conv", conv)})
    gains = {"mix_pre": mix_pre_gain, "mix_post": mix_post_gain, "mlp_pre": mlp_pre_gain, "mlp_post": mlp_post_gain,
             "kv": kv_gain[None, :], "alog": _lanes(gdn_a_log[0], N_HEADS), "dtb": _lanes(gdn_dt_bias[0], N_HEADS),
             "out_gain": gdn_out_gain}

    grad_x, big, small = _local_step(x[0], loss_target[0], gains, comm)

    small_names = ["pre0", "pre1", "post0", "post1", "mlp0", "mlp1", "kv_k", "kv_v", "gates", "out_gain", "conv", "loss"]
    landed = _exchange([small[nm] for nm in small_names], [True] * len(small_names), "exchange_small_grads")
    small_landed = dict(zip(small_names, landed))

    results = {}
    big_params = [("mlp_w_up0", "up0", mlp_w_up[0], m_mlp_w_up[0], v_mlp_w_up[0]),
                  ("mlp_w_up1", "up1", mlp_w_up[1], m_mlp_w_up[1], v_mlp_w_up[1]),
                  ("mlp_w_down0", "down0", mlp_w_down[0], m_mlp_w_down[0], v_mlp_w_down[0]),
                  ("mlp_w_down1", "down1", mlp_w_down[1], m_mlp_w_down[1], v_mlp_w_down[1]),
                  ("gdn_w_in", "gdn_in", gdn_w_in[0], m_gdn_w_in[0], v_gdn_w_in[0]),
                  ("gdn_w_out", "gdn_out", gdn_w_out[0], m_gdn_w_out[0], v_gdn_w_out[0]),
                  ("w_kv", "kv", w_kv, m_w_kv, v_w_kv), ("sb_w_q", "sb_q", sb_w_q[0], m_sb_w_q[0], v_sb_w_q[0]),
                  ("sb_w_o", "sb_o", sb_w_o[0], m_sb_w_o[0], v_sb_w_o[0])]
    for nm, short, w_, m_, v_ in big_params:
        results[nm] = _reduce_adamw(comm.landed[short], w_, m_, v_, "adamw_" + nm)
    lanes8 = lambda a: _lanes(a[0], N_HEADS)
    small_params = {
        "mix_pre": (mix_pre_gain, m_mix_pre_gain, v_mix_pre_gain), "mix_post": (mix_post_gain, m_mix_post_gain, v_mix_post_gain),
        "mlp_pre": (mlp_pre_gain, m_mlp_pre_gain, v_mlp_pre_gain), "mlp_post": (mlp_post_gain, m_mlp_post_gain, v_mlp_post_gain),
        "kv": (kv_gain[None, :], m_kv_gain[None, :], v_kv_gain[None, :]),
        "alog": (lanes8(gdn_a_log), lanes8(m_gdn_a_log), lanes8(v_gdn_a_log)),
        "dtb": (lanes8(gdn_dt_bias), lanes8(m_gdn_dt_bias), lanes8(v_gdn_dt_bias)),
        "out_gain": (gdn_out_gain, m_gdn_out_gain, v_gdn_out_gain),
    }
    sm = _small_update(small_landed, small_params, "small_update")
    conv_cols = QKV // N_DEV
    g_conv = lax.dynamic_slice(sm["conv"], (0, me * conv_cols), (8, conv_cols))[:CONV_WIDTH]
    conv_res = (g_conv,) + tuple(_adamw(g_conv, gdn_conv_w[0], m_gdn_conv_w[0], v_gdn_conv_w[0], "adamw_conv"))

    stack2 = lambda a, b: tuple(jnp.stack([p, q]) for p, q in zip(results[a], results[b]))
    lead = lambda t: tuple(a[None] for a in t)
    heads8 = lambda t: tuple(a[:, N_HEADS:2 * N_HEADS] for a in t)
    per_weight = [
        sm["mix_pre"], sm["mix_post"], sm["mlp_pre"], sm["mlp_post"],
        stack2("mlp_w_up0", "mlp_w_up1"), stack2("mlp_w_down0", "mlp_w_down1"),
        lead(results["gdn_w_in"]), lead(conv_res), heads8(sm["alog"]), heads8(sm["dtb"]), sm["out_gain"],
        lead(results["gdn_w_out"]), tuple(a[0] for a in sm["kv"]), results["w_kv"], lead(results["sb_w_q"]), lead(results["sb_w_o"]),
    ]
    grads, deltas, new_ms, new_vs = zip(*per_weight)
    return (sm["loss"][0, 0], grad_x[None], *grads, *deltas, *new_ms, *new_vs)
```

```python
import functools
import math

import jax
import jax.numpy as jnp
from jax import lax
from jax.experimental import pallas as pl
from jax.experimental.pallas import tpu as pltpu

F32 = jnp.float32
BF16 = jnp.bfloat16

N_DEV = 8
D_MODEL = 1024
D_FF = 4096
N_HEADS = 8
HEAD_DIM = 128
CHUNK = 64
CONV_WIDTH = 4
GDN_IN_COLS = 4 * D_MODEL + 2 * N_HEADS
GDN_IN_PAD = 4 * D_MODEL + 128
EPS = 1e-6

ADAM_LR = 0.001
ADAM_B1 = 0.9
ADAM_B2 = 0.999
ADAM_EPS = 1e-08
ADAM_WD = 0.01
ADAM_STEP = 10

VMEM_LIMIT_BYTES = 56 * 1024 * 1024
MESH = pl.DeviceIdType.MESH


def _params(*semantics):
    return pltpu.CompilerParams(dimension_semantics=semantics, vmem_limit_bytes=VMEM_LIMIT_BYTES)


def _dims(ta, tb):
    return (((0,) if ta else (1,), (1,) if tb else (0,)), ((), ()))


def _dot(a, b, ta=False, tb=False):
    return lax.dot_general(a.astype(BF16), b.astype(BF16), _dims(ta, tb), preferred_element_type=F32)


def _dot_f32(a, b, ta=False, tb=False):
    return lax.dot_general(a, b, _dims(ta, tb), precision=lax.Precision.HIGHEST, preferred_element_type=F32)


def _dot_3x(a, b, ta=False, tb=False):
    return lax.dot_general(a, b, _dims(ta, tb), precision=lax.Precision.HIGH, preferred_element_type=F32)


def _make_mm(dot):
    @functools.partial(jax.custom_vjp, nondiff_argnums=(2, 3))
    def mm(a, b, ta, tb):
        return dot(a, b, ta, tb)

    def fwd(a, b, ta, tb):
        return dot(a, b, ta, tb), (a, b)

    def bwd(ta, tb, res, g):
        a, b = res
        if not ta and not tb:
            return mm(g, b, False, True), mm(a, g, True, False)
        if not ta and tb:
            return mm(g, b, False, False), mm(g, a, True, False)
        if ta and not tb:
            return mm(b, g, False, True), mm(a, g, False, False)
        raise NotImplementedError

    mm.defvjp(fwd, bwd)
    return mm


_mm = _make_mm(_dot)
_mm_f32 = _make_mm(_dot_f32)


def _rms(x, gain):
    r = lax.rsqrt(jnp.mean(x * x, axis=-1, keepdims=True) + EPS)
    return x * r * gain


def _rms_bwd(x, gain, dy):
    r = lax.rsqrt(jnp.mean(x * x, axis=-1, keepdims=True) + EPS)
    xh = x * r
    dgain = jnp.sum(dy * xh, axis=0, keepdims=True)
    dxh = dy * gain
    dx = r * (dxh - xh * jnp.mean(dxh * xh, axis=-1, keepdims=True))
    return dx, dgain


def _silu(x):
    return x / (1.0 + jnp.exp(-x))


def _acc_rows(ref, first, rows):
    @pl.when(first)
    def _():
        ref[...] = jnp.zeros_like(ref)

    for r, val in enumerate(rows):
        ref[r:r + 1, :] += val


def _norm_matmul(x, gain, w, out_dtype, name, tm=1024, tn=1024):
    n, d = x.shape
    g, _, wc = w.shape
    tm, tn = min(tm, n), min(tn, wc)
    per = wc // tn
    assert n % tm == 0 and wc % tn == 0

    def body(x_ref, gain_ref, w_ref, out_ref, h_ref):
        @pl.when(pl.program_id(1) == 0)
        def _():
            h_ref[...] = _rms(x_ref[...], gain_ref[...]).astype(BF16)

        out_ref[...] = jnp.dot(h_ref[...], w_ref[...], preferred_element_type=F32).astype(out_dtype)

    return pl.pallas_call(
        body, name=name, grid=(n // tm, g * per),
        in_specs=[pl.BlockSpec((tm, d), lambda i, j: (i, 0)),
                  pl.BlockSpec((1, d), lambda i, j: (0, 0)),
                  pl.BlockSpec((None, d, tn), lambda i, j: (j // per, 0, j % per))],
        out_specs=[pl.BlockSpec((tm, tn), lambda i, j: (i, j)),
                   pl.BlockSpec((tm, d), lambda i, j: (i, 0))],
        out_shape=[jax.ShapeDtypeStruct((n, g * wc), out_dtype), jax.ShapeDtypeStruct((n, d), BF16)],
        compiler_params=_params("parallel", "arbitrary"),
    )(x, gain, w)


def _norm_matmul_bwd(dout, w, x, gain, add, name, tm=1024, tn=512, ride=None):
    n, d = x.shape
    g, _, wc = w.shape
    tm, tn = min(tm, n), min(tn, wc)
    per = wc // tn
    ncol = g * per

    def body(dout_ref, w_ref, x_ref, gain_ref, add_ref, dx_ref, dgain_ref, acc_ref):
        i, j = pl.program_id(0), pl.program_id(1)

        @pl.when(j == 0)
        def _():
            acc_ref[...] = jnp.zeros_like(acc_ref)

        acc_ref[...] += _dot(dout_ref[...], w_ref[...], tb=True)

        @pl.when(j == ncol - 1)
        def _():
            dx, dgain = _rms_bwd(x_ref[...], gain_ref[...], acc_ref[...])
            dx_ref[...] = add_ref[...] + dx
            _acc_rows(dgain_ref, i == 0, [dgain])

    nrow = n // tm
    (dx, dgain), got = _call(
        body, [dout, w, x, gain, add], name=name, grid=(nrow, ncol),
        in_specs=[pl.BlockSpec((tm, tn), lambda i, j: (i, j)),
                  pl.BlockSpec((None, d, tn), lambda i, j: (j // per, 0, j % per)),
                  pl.BlockSpec((tm, d), lambda i, j: (i, 0)),
                  pl.BlockSpec((1, d), lambda i, j: (0, 0)),
                  pl.BlockSpec((tm, d), lambda i, j: (i, 0))],
        out_specs=[pl.BlockSpec((tm, d), lambda i, j: (i, 0)),
                   pl.BlockSpec((8, d), lambda i, j: (0, 0))],
        out_shape=[jax.ShapeDtypeStruct((n, d), F32), jax.ShapeDtypeStruct((8, d), F32)],
        scratch_shapes=[pltpu.VMEM((tm, d), F32)],
        compiler_params=_params("arbitrary", "arbitrary"),
        ride=ride, first=lambda: (pl.program_id(0) == 0) & (pl.program_id(1) == 0),
        last=lambda: (pl.program_id(0) == nrow - 1) & (pl.program_id(1) == ncol - 1))
    return (dx, dgain) if ride is None else (dx, dgain, got)


def _matmul_tn(a, b, groups, name, ta=1024, tb=1024, tk=1024):
    n, ka = a.shape
    _, kb = b.shape
    wc = kb // groups
    ta, tb, tk = min(ta, ka), min(tb, wc), min(tk, n)
    per = wc // tb
    nk = n // tk
    assert ka % ta == 0 and wc % tb == 0 and n % tk == 0

    def body(a_ref, b_ref, out_ref, acc_ref):
        k = pl.program_id(2)

        @pl.when(k == 0)
        def _():
            acc_ref[...] = jnp.zeros_like(acc_ref)

        acc_ref[...] += _dot(a_ref[...], b_ref[...], ta=True)

        @pl.when(k == nk - 1)
        def _():
            out_ref[...] = acc_ref[...].astype(BF16)

    return pl.pallas_call(
        body, name=name, grid=(ka // ta, groups * per, nk),
        in_specs=[pl.BlockSpec((tk, ta), lambda i, j, k: (k, i)),
                  pl.BlockSpec((tk, tb), lambda i, j, k: (k, j))],
        out_specs=pl.BlockSpec((None, ta, tb), lambda i, j, k: (j // per, i, j % per)),
        out_shape=jax.ShapeDtypeStruct((groups, ka, wc), BF16),
        scratch_shapes=[pltpu.VMEM((ta, tb), F32)],
        compiler_params=_params("parallel", "parallel", "arbitrary"),
    )(a, b)


def _gated_head_norm(o, gate, out_gain):
    parts = []
    for h in range(N_HEADS):
        sl = slice(h * HEAD_DIM, (h + 1) * HEAD_DIM)
        parts.append(_rms(o[:, sl], out_gain) * _silu(gate[:, sl]))
    return parts


def _out_proj(a_inputs, gated, x_in, w, gain, out_gain, name, tm=1024):
    n, d = x_in.shape
    k = w.shape[0]
    tm = min(tm, n)

    def body(*refs):
        if gated:
            o_ref, gate_ref, og_ref, x_ref, w_ref, gain_ref, xo_ref, y_ref, a_ref = refs
            parts = _gated_head_norm(o_ref[...], gate_ref[...], og_ref[...])
            for h, part in enumerate(parts):
                a_ref[:, h * HEAD_DIM:(h + 1) * HEAD_DIM] = part.astype(BF16)
            a = a_ref[...]
        else:
            a_in_ref, x_ref, w_ref, gain_ref, xo_ref, y_ref = refs
            a = a_in_ref[...]
        y = jnp.dot(a, w_ref[...], preferred_element_type=F32)
        y_ref[...] = y
        xo_ref[...] = x_ref[...] + _rms(y, gain_ref[...])

    row = lambda i: (i, 0)
    const = lambda i: (0, 0)
    if gated:
        a_specs = [pl.BlockSpec((tm, k), row), pl.BlockSpec((tm, D_MODEL), lambda i: (i, 3)),
                   pl.BlockSpec((1, HEAD_DIM), const)]
        a_args = list(a_inputs) + [out_gain]
    else:
        a_specs = [pl.BlockSpec((tm, k), row)]
        a_args = list(a_inputs)
    out_specs = [pl.BlockSpec((tm, d), row), pl.BlockSpec((tm, d), row)]
    out_shape = [jax.ShapeDtypeStruct((n, d), F32), jax.ShapeDtypeStruct((n, d), F32)]
    if gated:
        out_specs.append(pl.BlockSpec((tm, k), row))
        out_shape.append(jax.ShapeDtypeStruct((n, k), BF16))
    return pl.pallas_call(
        body, name=name, grid=(n // tm,),
        in_specs=a_specs + [pl.BlockSpec((tm, d), row), pl.BlockSpec((k, d), const), pl.BlockSpec((1, d), const)],
        out_specs=out_specs, out_shape=out_shape,
        compiler_params=_params("parallel"),
    )(*a_args, x_in, w, gain)


def _out_proj_bwd(dxo, y, w, gain, gated_inputs, name, tm=1024):
    n, d = dxo.shape
    k = w.shape[0]
    tm = min(tm, n)
    gated = gated_inputs is not None

    def body(*refs):
        if gated:
            (dxo_ref, y_ref, w_ref, gain_ref, o_ref, gate_ref, og_ref,
             dy_ref, dgain_ref, do_ref, dgate_ref, dog_ref) = refs
        else:
            dxo_ref, y_ref, w_ref, gain_ref, dy_ref, dgain_ref, da_ref = refs
        first = pl.program_id(0) == 0
        dy, dgain = _rms_bwd(y_ref[...], gain_ref[...], dxo_ref[...])
        dy_ref[...] = dy.astype(BF16)
        _acc_rows(dgain_ref, first, [dgain])
        da = _dot(dy_ref[...], w_ref[...], tb=True)
        if not gated:
            da_ref[...] = da.astype(BF16)
            return
        og = og_ref[...]
        dog = jnp.zeros_like(og)
        for h in range(N_HEADS):
            sl = slice(h * HEAD_DIM, (h + 1) * HEAD_DIM)
            fn = lambda o_h, g_h, gn: _rms(o_h, gn) * _silu(g_h)
            _, vjp = jax.vjp(fn, o_ref[:, sl], gate_ref[:, sl], og)
            do_h, dgate_h, dog_h = vjp(da[:, sl])
            do_ref[:, sl] = do_h
            dgate_ref[:, sl] = dgate_h
            dog = dog + dog_h
        _acc_rows(dog_ref, first, [dog])

    row = lambda i: (i, 0)
    const = lambda i: (0, 0)
    in_specs = [pl.BlockSpec((tm, d), row), pl.BlockSpec((tm, d), row), pl.BlockSpec((k, d), const),
                pl.BlockSpec((1, d), const)]
    args = [dxo, y, w, gain]
    out_specs = [pl.BlockSpec((tm, d), row), pl.BlockSpec((8, d), const)]
    out_shape = [jax.ShapeDtypeStruct((n, d), BF16), jax.ShapeDtypeStruct((8, d), F32)]
    if gated:
        in_specs += [pl.BlockSpec((tm, k), row), pl.BlockSpec((tm, D_MODEL), lambda i: (i, 3)),
                     pl.BlockSpec((1, HEAD_DIM), const)]
        args += list(gated_inputs)
        out_specs += [pl.BlockSpec((tm, k), row), pl.BlockSpec((tm, k), row), pl.BlockSpec((8, HEAD_DIM), const)]
        out_shape += [jax.ShapeDtypeStruct((n, k), F32), jax.ShapeDtypeStruct((n, k), F32),
                      jax.ShapeDtypeStruct((8, HEAD_DIM), F32)]
    else:
        out_specs.append(pl.BlockSpec((tm, k), row))
        out_shape.append(jax.ShapeDtypeStruct((n, k), BF16))
    return pl.pallas_call(
        body, name=name, grid=(n // tm,), in_specs=in_specs, out_specs=out_specs, out_shape=out_shape,
        compiler_params=_params("arbitrary"),
    )(*args)


def _mlp_fwd(x_in, g_pre, w_up, w_down, g_post, name, tm=1024):
    n, d = x_in.shape
    g, _, wc = w_up.shape
    tm = min(tm, n)

    def body(x_ref, gpre_ref, wup_ref, wdown_ref, gpost_ref, xo_ref, y_ref, h_ref, u_ref, a_ref, acc_ref):
        j = pl.program_id(1)

        @pl.when(j == 0)
        def _():
            h_ref[...] = _rms(x_ref[...], gpre_ref[...]).astype(BF16)
            acc_ref[...] = jnp.zeros_like(acc_ref)

        u = jnp.dot(h_ref[...], wup_ref[...], preferred_element_type=F32).astype(BF16)
        u_ref[...] = u
        a = jnp.square(jnp.maximum(u, 0))
        a_ref[...] = a
        acc_ref[...] += jnp.dot(a, wdown_ref[...], preferred_element_type=F32)

        @pl.when(j == g - 1)
        def _():
            y = acc_ref[...]
            y_ref[...] = y
            xo_ref[...] = x_ref[...] + _rms(y, gpost_ref[...])

    row = lambda i, j: (i, 0)
    const = lambda i, j: (0, 0)
    return pl.pallas_call(
        body, name=name, grid=(n // tm, g),
        in_specs=[pl.BlockSpec((tm, d), row), pl.BlockSpec((1, d), const),
                  pl.BlockSpec((None, d, wc), lambda i, j: (j, 0, 0)),
                  pl.BlockSpec((wc, d), lambda i, j: (j, 0)), pl.BlockSpec((1, d), const)],
        out_specs=[pl.BlockSpec((tm, d), row), pl.BlockSpec((tm, d), row), pl.BlockSpec((tm, d), row),
                   pl.BlockSpec((tm, wc), lambda i, j: (i, j)), pl.BlockSpec((tm, wc), lambda i, j: (i, j))],
        out_shape=[jax.ShapeDtypeStruct((n, d), F32), jax.ShapeDtypeStruct((n, d), F32),
                   jax.ShapeDtypeStruct((n, d), BF16), jax.ShapeDtypeStruct((n, g * wc), BF16),
                   jax.ShapeDtypeStruct((n, g * wc), BF16)],
        scratch_shapes=[pltpu.VMEM((tm, d), F32)],
        compiler_params=_params("parallel", "arbitrary"),
    )(x_in, g_pre, w_up, w_down, g_post)


def _mlp_bwd(dxo, y, g_post, w_down, u, w_up, x_in, g_pre, name, tm=1024):
    n, d = dxo.shape
    g, _, wc = w_up.shape

    def body(dxo_ref, y_ref, gpost_ref, wdown_ref, u_ref, wup_ref, x_ref, gpre_ref,
             dx_ref, dy_ref, du_ref, dgain_ref, acc_ref, dgpost_ref):
        i, j = pl.program_id(0), pl.program_id(1)

        @pl.when(j == 0)
        def _():
            dy, dgpost = _rms_bwd(y_ref[...], gpost_ref[...], dxo_ref[...])
            dy_ref[...] = dy.astype(BF16)
            dgpost_ref[...] = dgpost
            acc_ref[...] = jnp.zeros_like(acc_ref)

        da = _dot(dy_ref[...], wdown_ref[...], tb=True)
        du = (da * (2.0 * jnp.maximum(u_ref[...], 0).astype(F32))).astype(BF16)
        du_ref[...] = du
        acc_ref[...] += _dot(du, wup_ref[...], tb=True)

        @pl.when(j == g - 1)
        def _():
            dx, dgpre = _rms_bwd(x_ref[...], gpre_ref[...], acc_ref[...])
            dx_ref[...] = dxo_ref[...] + dx
            _acc_rows(dgain_ref, i == 0, [dgpre, dgpost_ref[...]])

    row = lambda i, j: (i, 0)
    const = lambda i, j: (0, 0)
    return pl.pallas_call(
        body, name=name, grid=(n // tm, g),
        in_specs=[pl.BlockSpec((tm, d), row), pl.BlockSpec((tm, d), row), pl.BlockSpec((1, d), const),
                  pl.BlockSpec((wc, d), lambda i, j: (j, 0)), pl.BlockSpec((tm, wc), lambda i, j: (i, j)),
                  pl.BlockSpec((None, d, wc), lambda i, j: (j, 0, 0)), pl.BlockSpec((tm, d), row),
                  pl.BlockSpec((1, d), const)],
        out_specs=[pl.BlockSpec((tm, d), row), pl.BlockSpec((tm, d), row),
                   pl.BlockSpec((tm, wc), lambda i, j: (i, j)), pl.BlockSpec((8, d), const)],
        out_shape=[jax.ShapeDtypeStruct((n, d), F32), jax.ShapeDtypeStruct((n, d), BF16),
                   jax.ShapeDtypeStruct((n, g * wc), BF16), jax.ShapeDtypeStruct((8, d), F32)],
        scratch_shapes=[pltpu.VMEM((tm, d), F32), pltpu.VMEM((1, d), F32)],
        compiler_params=_params("arbitrary", "arbitrary"),
    )(dxo, y, g_post, w_down, u, w_up, x_in, g_pre)


QKV = 3 * D_MODEL


def _shifted(x, prev8, s):
    if s == 0:
        return x
    tm = x.shape[0]
    rolled = pltpu.roll(x, s, 0)
    head = pltpu.roll(prev8, s, 0)
    head = jnp.concatenate([head, jnp.zeros((tm - 8, x.shape[1]), x.dtype)], axis=0)
    rows = lax.broadcasted_iota(jnp.int32, x.shape, 0)
    return jnp.where(rows < s, head, rolled)


def _conv(x, prev8, conv_w):
    out = x * conv_w[3:4, :]
    for s in range(1, CONV_WIDTH):
        out = out + _shifted(x, prev8, s) * conv_w[3 - s:4 - s, :]
    return out


def _l2norm(x):
    return x * lax.rsqrt(jnp.sum(x * x, axis=-1, keepdims=True) + EPS)


def _gdn_act_head(cq, ck, cv):
    return _l2norm(_silu(cq)) * (HEAD_DIM ** -0.5), _l2norm(_silu(ck)), _silu(cv)


def _gdn_gates(ba, alog, dtb):
    lane = lax.broadcasted_iota(jnp.int32, ba.shape, 1)
    beta = 1.0 / (1.0 + jnp.exp(-ba))
    t = ba + dtb
    softplus = jnp.maximum(t, 0.0) + jnp.log(1.0 + jnp.exp(-jnp.abs(t)))
    g = -jnp.exp(alog) * softplus
    return jnp.where(lane < N_HEADS, beta, jnp.where(lane < 2 * N_HEADS, g, 0.0))


def _qkv_cols(part, h):
    start = part * D_MODEL + h * HEAD_DIM
    return slice(start, start + HEAD_DIM)


def _prev_rows_spec(tm, cols, colblock, order):
    per = tm // 8
    return pl.BlockSpec((8, cols), lambda i: (jnp.maximum(order(i) * per - 1, 0), colblock))


def _gdn_pre(proj, conv_w, alog, dtb, name, tm=256):
    n = proj.shape[0]

    def body(x_ref, prev_ref, ba_ref, cw_ref, alog_ref, dtb_ref, q_ref, k_ref, v_ref, bg_ref):
        first = pl.program_id(0) == 0
        for h in range(N_HEADS):
            convs = []
            for part in range(3):
                sl = _qkv_cols(part, h)
                prev8 = jnp.where(first, 0.0, prev_ref[:, sl])
                convs.append(_conv(x_ref[:, sl], prev8, cw_ref[:, sl]))
            out = slice(h * HEAD_DIM, (h + 1) * HEAD_DIM)
            q_ref[:, out], k_ref[:, out], v_ref[:, out] = _gdn_act_head(*convs)
        bg_ref[...] = _gdn_gates(ba_ref[...], alog_ref[...], dtb_ref[...])

    row = lambda i: (i, 0)
    const = lambda i: (0, 0)
    ident = lambda i: i
    return pl.pallas_call(
        body, name=name, grid=(n // tm,),
        in_specs=[pl.BlockSpec((tm, QKV), row), _prev_rows_spec(tm, QKV, 0, ident),
                  pl.BlockSpec((tm, 128), lambda i: (i, 4 * D_MODEL // 128)),
                  pl.BlockSpec((CONV_WIDTH, QKV), const), pl.BlockSpec((1, 128), const), pl.BlockSpec((1, 128), const)],
        out_specs=[pl.BlockSpec((tm, D_MODEL), row)] * 3 + [pl.BlockSpec((tm, 128), row)],
        out_shape=[jax.ShapeDtypeStruct((n, D_MODEL), F32)] * 3 + [jax.ShapeDtypeStruct((n, 128), F32)],
        compiler_params=_params("parallel"),
    )(proj, proj, proj, conv_w, alog, dtb)


def _gdn_pre_bwd(proj, conv_w, alog, dtb, dq, dk, dv, dbg, dgate, name, tm=256):
    n = proj.shape[0]
    nt = n // tm

    def body(x_ref, prev_ref, ba_ref, cw_ref, alog_ref, dtb_ref, dq_ref, dk_ref, dv_ref, dbg_ref, dgate_ref,
             dproj_ref, dcw_ref, dgates_ref, carry_ref):
        step = pl.program_id(0)
        tile = nt - 1 - step
        @pl.when(step == 0)
        def _():
            carry_ref[...] = jnp.zeros_like(carry_ref)
            dcw_ref[...] = jnp.zeros_like(dcw_ref)

        rows = lax.broadcasted_iota(jnp.int32, (tm, HEAD_DIM), 0)
        for h in range(N_HEADS):
            cols = [_qkv_cols(part, h) for part in range(3)]
            prevs = [jnp.where(tile == 0, 0.0, prev_ref[:, sl]) for sl in cols]
            convs = [_conv(x_ref[:, sl], prev8, cw_ref[:, sl]) for sl, prev8 in zip(cols, prevs)]
            _, vjp = jax.vjp(_gdn_act_head, *convs)
            out = slice(h * HEAD_DIM, (h + 1) * HEAD_DIM)
            dcs = vjp((dq_ref[:, out], dk_ref[:, out], dv_ref[:, out]))
            for sl, prev8, dc in zip(cols, prevs, dcs):
                x = x_ref[:, sl]
                cw = cw_ref[:, sl]
                dx = dc * cw[3:4, :]
                wrapped = jnp.zeros((8, HEAD_DIM), F32)
                dcw_ref[3:4, sl] += jnp.sum(dc * x, axis=0, keepdims=True)
                for s in range(1, CONV_WIDTH):
                    r = pltpu.roll(dc, tm - s, 0) * cw[3 - s:4 - s, :]
                    dx = dx + jnp.where(rows < tm - s, r, 0.0)
                    wrapped = wrapped + jnp.where(rows[tm - 8:, :] >= tm - s, r[tm - 8:, :], 0.0)
                    dcw_ref[3 - s:4 - s, sl] += jnp.sum(dc * _shifted(x, prev8, s), axis=0, keepdims=True)
                dproj_ref[:, sl] = dx
                dproj_ref[tm - 8:, sl] += carry_ref[:, sl]
                carry_ref[:, sl] = wrapped

        _, vjp = jax.vjp(_gdn_gates, ba_ref[...], alog_ref[...], dtb_ref[...])
        dba, dalog, ddtb = vjp(dbg_ref[...])
        dproj_ref[:, QKV:4 * D_MODEL] = dgate_ref[...]
        dproj_ref[:, 4 * D_MODEL:] = dba
        _acc_rows(dgates_ref, step == 0, [dalog, ddtb])

    rev = lambda i: nt - 1 - i
    row = lambda i: (rev(i), 0)
    const = lambda i: (0, 0)
    return pl.pallas_call(
        body, name=name, grid=(nt,),
        in_specs=[pl.BlockSpec((tm, QKV), row), _prev_rows_spec(tm, QKV, 0, rev),
                  pl.BlockSpec((tm, 128), lambda i: (rev(i), 4 * D_MODEL // 128)),
                  pl.BlockSpec((CONV_WIDTH, QKV), const), pl.BlockSpec((1, 128), const), pl.BlockSpec((1, 128), const),
                  pl.BlockSpec((tm, D_MODEL), row), pl.BlockSpec((tm, D_MODEL), row), pl.BlockSpec((tm, D_MODEL), row),
                  pl.BlockSpec((tm, 128), row), pl.BlockSpec((tm, D_MODEL), row)],
        out_specs=[pl.BlockSpec((tm, GDN_IN_PAD), row), pl.BlockSpec((8, QKV), const), pl.BlockSpec((8, 128), const)],
        out_shape=[jax.ShapeDtypeStruct((n, GDN_IN_PAD), F32), jax.ShapeDtypeStruct((8, QKV), F32),
                   jax.ShapeDtypeStruct((8, 128), F32)],
        scratch_shapes=[pltpu.VMEM((8, QKV), F32)],
        compiler_params=_params("arbitrary"),
    )(proj, proj, proj, conv_w, alog, dtb, dq, dk, dv, dbg, dgate)


@jax.custom_vjp
def _unit_lower_inverses(lowers):
    c = lowers[0].shape[0]
    eye = (lax.broadcasted_iota(jnp.int32, (c, c), 0) == lax.broadcasted_iota(jnp.int32, (c, c), 1)).astype(F32)
    ys = [-low for low in lowers]
    ps = [eye + y for y in ys]
    for _ in range(int(math.log2(c)) - 1):
        ys = [_dot_3x(y, y) for y in ys]
        ps = [p + _dot_3x(p, y) for p, y in zip(ps, ys)]
    return tuple(ps)


def _unit_lower_inverses_fwd(lowers):
    ts = _unit_lower_inverses(lowers)
    return ts, ts


def _unit_lower_inverses_bwd(ts, dts):
    left = [_dot_3x(t, dt, ta=True) for t, dt in zip(ts, dts)]
    return (tuple(-_dot_3x(l, t, tb=True) for l, t in zip(left, ts)),)


_unit_lower_inverses.defvjp(_unit_lower_inverses_fwd, _unit_lower_inverses_bwd)


def _gdn_chunk(qs, ks, vs, bg, states):
    c = CHUNK
    heads = range(N_HEADS)
    row = lax.broadcasted_iota(jnp.int32, (c, c), 0)
    col = lax.broadcasted_iota(jnp.int32, (c, c), 1)
    incl, strict, eye = row >= col, row > col, row == col
    lane = lax.broadcasted_iota(jnp.int32, (c, 128), 1)
    rowc = lax.broadcasted_iota(jnp.int32, (c, 1), 0)
    gc_all = _mm_f32(incl.astype(F32), bg, False, False)
    beta = [jnp.sum(jnp.where(lane == h, bg, 0.0), axis=1, keepdims=True) for h in heads]
    gc = [jnp.sum(jnp.where(lane == N_HEADS + h, gc_all, 0.0), axis=1, keepdims=True) for h in heads]
    gc_row = [jnp.sum(jnp.where(eye, gc[h], 0.0), axis=0, keepdims=True) for h in heads]
    gc_last = [jnp.sum(jnp.where(rowc == c - 1, gc[h], 0.0), axis=0, keepdims=True) for h in heads]
    decay = [jnp.where(incl, jnp.exp(jnp.where(incl, gc[h] - gc_row[h], 0.0)), 0.0) for h in heads]
    kb = [ks[h] * beta[h] for h in heads]
    lower = tuple(jnp.where(strict, _mm(kb[h], ks[h], False, True) * decay[h], 0.0) for h in heads)
    attn = [_mm(qs[h], ks[h], False, True) * decay[h] for h in heads]
    t_mat = _unit_lower_inverses(lower)
    egc = [jnp.exp(gc[h]) for h in heads]
    w = [_mm(t_mat[h], kb[h] * egc[h], False, False) for h in heads]
    u = [_mm(t_mat[h], vs[h] * beta[h], False, False) for h in heads]
    kg = [ks[h] * jnp.exp(gc_last[h] - gc[h]) for h in heads]
    v_new = [u[h] - _mm(w[h], states[h], False, False) for h in heads]
    from_state = [_mm(qs[h] * egc[h], states[h], False, False) for h in heads]
    outs = [from_state[h] + _mm(attn[h], v_new[h], False, False) for h in heads]
    new_states = [states[h] * jnp.exp(gc_last[h]) + _mm(kg[h], v_new[h], True, False) for h in heads]
    return tuple(outs), tuple(new_states)


def _head_slices(ref):
    return tuple(ref[:, h * HEAD_DIM:(h + 1) * HEAD_DIM] for h in range(N_HEADS))


def _gdn_scan(qn, kn, v, bg, name, ride=None):
    n = qn.shape[0]
    nc = n // CHUNK

    def body(q_ref, k_ref, v_ref, bg_ref, o_ref, saved_ref, state_ref):
        @pl.when(pl.program_id(0) == 0)
        def _():
            state_ref[...] = jnp.zeros_like(state_ref)

        states = tuple(state_ref[h] for h in range(N_HEADS))
        for h in range(N_HEADS):
            saved_ref[h] = states[h]
        outs, new_states = _gdn_chunk(_head_slices(q_ref), _head_slices(k_ref), _head_slices(v_ref), bg_ref[...], states)
        for h in range(N_HEADS):
            o_ref[:, h * HEAD_DIM:(h + 1) * HEAD_DIM] = outs[h]
            state_ref[h] = new_states[h]

    row = lambda i: (i, 0)
    return _call(
        body, [qn, kn, v, bg], name=name, grid=(nc,),
        in_specs=[pl.BlockSpec((CHUNK, D_MODEL), row)] * 3 + [pl.BlockSpec((CHUNK, 128), row)],
        out_specs=[pl.BlockSpec((CHUNK, D_MODEL), row),
                   pl.BlockSpec((None, N_HEADS, HEAD_DIM, HEAD_DIM), lambda i: (i, 0, 0, 0))],
        out_shape=[jax.ShapeDtypeStruct((n, D_MODEL), F32),
                   jax.ShapeDtypeStruct((nc, N_HEADS, HEAD_DIM, HEAD_DIM), F32)],
        scratch_shapes=[pltpu.VMEM((N_HEADS, HEAD_DIM, HEAD_DIM), F32)],
        compiler_params=_params("arbitrary"),
        ride=ride, first=lambda: pl.program_id(0) == 0, last=lambda: pl.program_id(0) == nc - 1)


def _gdn_scan_bwd(qn, kn, v, bg, saved, do, name, ride=None):
    n = qn.shape[0]
    nc = n // CHUNK

    def body(q_ref, k_ref, v_ref, bg_ref, saved_ref, do_ref, dq_ref, dk_ref, dv_ref, dbg_ref, dstate_ref):
        @pl.when(pl.program_id(0) == 0)
        def _():
            dstate_ref[...] = jnp.zeros_like(dstate_ref)

        states = tuple(saved_ref[h] for h in range(N_HEADS))
        _, vjp = jax.vjp(_gdn_chunk, _head_slices(q_ref), _head_slices(k_ref), _head_slices(v_ref), bg_ref[...], states)
        dstates = tuple(dstate_ref[h] for h in range(N_HEADS))
        dqs, dks, dvs, dbg, dprev = vjp((_head_slices(do_ref), dstates))
        for h in range(N_HEADS):
            sl = slice(h * HEAD_DIM, (h + 1) * HEAD_DIM)
            dq_ref[:, sl] = dqs[h]
            dk_ref[:, sl] = dks[h]
            dv_ref[:, sl] = dvs[h]
            dstate_ref[h] = dprev[h]
        dbg_ref[...] = dbg

    row = lambda i: (nc - 1 - i, 0)
    return _call(
        body, [qn, kn, v, bg, saved, do], name=name, grid=(nc,),
        in_specs=[pl.BlockSpec((CHUNK, D_MODEL), row)] * 3 + [pl.BlockSpec((CHUNK, 128), row),
                  pl.BlockSpec((None, N_HEADS, HEAD_DIM, HEAD_DIM), lambda i: (nc - 1 - i, 0, 0, 0)),
                  pl.BlockSpec((CHUNK, D_MODEL), row)],
        out_specs=[pl.BlockSpec((CHUNK, D_MODEL), row)] * 3 + [pl.BlockSpec((CHUNK, 128), row)],
        out_shape=[jax.ShapeDtypeStruct((n, D_MODEL), F32)] * 3 + [jax.ShapeDtypeStruct((n, 128), F32)],
        scratch_shapes=[pltpu.VMEM((N_HEADS, HEAD_DIM, HEAD_DIM), F32)],
        compiler_params=_params("arbitrary"),
        ride=ride, first=lambda: pl.program_id(0) == 0, last=lambda: pl.program_id(0) == nc - 1)


SB_BQ = 512
SB_SUB = 128
SB_ROWS = 128
SB_SCALE = HEAD_DIM ** -0.5
SB_DEAD = -105.0


def _sb_terms(z, before):
    e = jnp.exp(-jnp.abs(z))
    log_beta = jnp.minimum(z, 0.0) - jnp.log(1.0 + e)
    log_1m = log_beta - z
    if before is not None:
        log_1m = jnp.where(before, log_1m, 0.0)
    return e, log_beta, log_1m


def _tri_ones(n, cmp):
    r = lax.broadcasted_iota(jnp.int32, (n, 2 * n), 0)
    c = lax.broadcasted_iota(jnp.int32, (n, 2 * n), 1)
    return jnp.where((c >= n) | cmp(r, c), 1.0, 0.0).astype(BF16)


def _sums(x, tri_ones):
    both = jnp.dot(x.astype(BF16), tri_ones, preferred_element_type=F32)
    n = x.shape[1]
    return both[:, :n], both[:, n:]


def _sb_chunk_mask(diagonal, r, s):
    if not diagonal or s * SB_SUB + SB_SUB - 1 < r * SB_ROWS:
        return None
    if s * SB_SUB >= r * SB_ROWS + SB_ROWS - 1:
        return "empty"
    rows = r * SB_ROWS + lax.broadcasted_iota(jnp.int32, (SB_ROWS, SB_SUB), 0)
    cols = s * SB_SUB + lax.broadcasted_iota(jnp.int32, (SB_ROWS, SB_SUB), 1)
    return cols < rows


def _sb_attention(q, kv, name, bq=SB_BQ, ride=None):
    n = q.shape[0]
    bq = min(bq, n)
    nsub, nrc = bq // SB_SUB, bq // SB_ROWS

    def body(q_ref, k_ref, v_ref, o_ref, l_ref, n_ref, z_scr, a_scr):
        i = pl.program_id(1)
        qb = q_ref[...]
        after = _tri_ones(SB_SUB, lambda r, c: r > c)

        def span(j, c_sum, acc, diagonal):
            start = pl.multiple_of(j * bq, bq)
            k_w = k_ref[pl.ds(start, bq), :]
            v_w = v_ref[pl.ds(start, bq), :]
            z_scr[...] = lax.dot_general(qb, k_w, _dims(False, True), preferred_element_type=F32)
            c_rows = [c_sum[r * SB_ROWS:(r + 1) * SB_ROWS] for r in range(nrc)]
            for s in reversed(range(nsub)):
                cols = slice(s * SB_SUB, (s + 1) * SB_SUB)
                for r in range(nrc):
                    rows = slice(r * SB_ROWS, (r + 1) * SB_ROWS)
                    before = _sb_chunk_mask(diagonal, r, s)
                    if isinstance(before, str):
                        a_scr[rows, cols] = jnp.zeros((SB_ROWS, SB_SUB), BF16)
                        continue
                    _, log_beta, log_1m = _sb_terms(z_scr[rows, cols] * SB_SCALE, before)
                    tail, total = _sums(log_1m, after)
                    a = jnp.exp(log_beta + c_rows[r] + tail)
                    if before is not None:
                        a = jnp.where(before, a, 0.0)
                    a_scr[rows, cols] = a.astype(BF16)
                    c_rows[r] = c_rows[r] + total
            acc = acc + jnp.dot(a_scr[...], v_w, preferred_element_type=F32)
            return jnp.concatenate(c_rows, axis=0), acc

        c_sum, acc = span(i, jnp.zeros((bq, SB_SUB), F32), jnp.zeros((bq, HEAD_DIM), F32), True)

        def more(state):
            done, c, _ = state
            return (done < i) & (jnp.max(c) > SB_DEAD)

        def step(state):
            done, c, a = state
            c, a = span(i - 1 - done, c, a, False)
            return done + 1, c, a

        done, c_sum, acc = lax.while_loop(more, step, (jnp.int32(0), c_sum, acc))
        o_ref[...] = acc.astype(BF16)
        l_ref[...] = c_sum
        n_ref[...] = jnp.full((8, 128), done.astype(F32), F32)

    nq = n // bq
    return _call(
        body, [q, kv, kv], name=name, grid=(N_HEADS, nq),
        in_specs=[pl.BlockSpec((bq, HEAD_DIM), lambda h, i: (i, h)),
                  pl.BlockSpec((n, HEAD_DIM), lambda h, i: (0, h)),
                  pl.BlockSpec((n, HEAD_DIM), lambda h, i: (0, N_HEADS + h))],
        out_specs=[pl.BlockSpec((bq, HEAD_DIM), lambda h, i: (i, h)),
                   pl.BlockSpec((None, bq, 128), lambda h, i: (h, i, 0)),
                   pl.BlockSpec((None, None, 8, 128), lambda h, i: (h, i, 0, 0))],
        out_shape=[jax.ShapeDtypeStruct((n, D_MODEL), BF16), jax.ShapeDtypeStruct((N_HEADS, n, 128), F32),
                   jax.ShapeDtypeStruct((N_HEADS, nq, 8, 128), F32)],
        scratch_shapes=[pltpu.VMEM((bq, bq), F32), pltpu.VMEM((bq, bq), BF16)],
        compiler_params=_params("arbitrary", "arbitrary"),
        ride=ride, first=lambda: (pl.program_id(0) == 0) & (pl.program_id(1) == 0),
        last=lambda: (pl.program_id(0) == N_HEADS - 1) & (pl.program_id(1) == nq - 1))


def _sb_attention_bwd(q, kv, do, lsum, spans, name, bq=SB_BQ, ride=None):
    n = q.shape[0]
    bq = min(bq, n)
    nsub, nrc = bq // SB_SUB, bq // SB_ROWS

    def body(q_ref, k_ref, v_ref, do_ref, l_ref, n_ref, dq_ref, dk_ref, dv_ref, z_scr, da_scr, a_scr, dz_scr):
        i = pl.program_id(1)

        @pl.when(i == 0)
        def _():
            dk_ref[...] = jnp.zeros_like(dk_ref)
            dv_ref[...] = jnp.zeros_like(dv_ref)

        qb = q_ref[...]
        dob = do_ref[...]
        lt_rows = [l_ref[r * SB_ROWS:(r + 1) * SB_ROWS, :] for r in range(nrc)]
        upto = _tri_ones(SB_SUB, lambda r, c: r <= c)
        below = _tri_ones(SB_SUB, lambda r, c: r < c)

        def span(j, l_sum, g_sum, dq, diagonal):
            start = pl.multiple_of(j * bq, bq)
            k_w = k_ref[pl.ds(start, bq), :]
            v_w = v_ref[pl.ds(start, bq), :]
            z_scr[...] = lax.dot_general(qb, k_w, _dims(False, True), preferred_element_type=F32)
            da_scr[...] = lax.dot_general(dob, v_w, _dims(False, True), preferred_element_type=F32)
            l_rows = [l_sum[r * SB_ROWS:(r + 1) * SB_ROWS] for r in range(nrc)]
            g_rows = [g_sum[r * SB_ROWS:(r + 1) * SB_ROWS] for r in range(nrc)]
            for s in range(nsub):
                cols = slice(s * SB_SUB, (s + 1) * SB_SUB)
                for r in range(nrc):
                    rows = slice(r * SB_ROWS, (r + 1) * SB_ROWS)
                    before = _sb_chunk_mask(diagonal, r, s)
                    if isinstance(before, str):
                        a_scr[rows, cols] = jnp.zeros((SB_ROWS, SB_SUB), BF16)
                        dz_scr[rows, cols] = jnp.zeros((SB_ROWS, SB_SUB), BF16)
                        continue
                    zs = z_scr[rows, cols] * SB_SCALE
                    e, log_beta, log_1m = _sb_terms(zs, before)
                    l_prefix, l_total = _sums(log_1m, upto)
                    a = jnp.exp(log_beta + (lt_rows[r] - (l_rows[r] + l_prefix)))
                    if before is not None:
                        a = jnp.where(before, a, 0.0)
                    g = a * da_scr[rows, cols]
                    g_prefix, g_total = _sums(g, below)
                    inv = 1.0 / (1.0 + e)
                    beta = jnp.where(zs >= 0, inv, e * inv)
                    dz = (g - (g + g_rows[r] + g_prefix) * beta) * SB_SCALE
                    if before is not None:
                        dz = jnp.where(before, dz, 0.0)
                    a_scr[rows, cols] = a.astype(BF16)
                    dz_scr[rows, cols] = dz.astype(BF16)
                    l_rows[r] = l_rows[r] + l_total
                    g_rows[r] = g_rows[r] + g_total
            dz_w = dz_scr[...]
            dq = dq + jnp.dot(dz_w, k_w, preferred_element_type=F32)
            dk_ref[pl.ds(start, bq), :] += lax.dot_general(dz_w, qb, _dims(True, False), preferred_element_type=F32)
            dv_ref[pl.ds(start, bq), :] += lax.dot_general(a_scr[...], dob, _dims(True, False), preferred_element_type=F32)
            return jnp.concatenate(l_rows, axis=0), jnp.concatenate(g_rows, axis=0), dq

        taken = jnp.clip(jnp.max(n_ref[...]).astype(jnp.int32), 0, i)
        zero = jnp.zeros((bq, SB_SUB), F32)
        carry = lax.fori_loop(i - taken, i, lambda j, c: span(j, c[0], c[1], c[2], False),
                              (zero, zero, jnp.zeros((bq, HEAD_DIM), F32)))
        _, _, dq = span(i, carry[0], carry[1], carry[2], True)
        dq_ref[...] = dq.astype(BF16)

    nq = n // bq
    return _call(
        body, [q, kv, kv, do, lsum, spans], name=name, grid=(N_HEADS, nq),
        in_specs=[pl.BlockSpec((bq, HEAD_DIM), lambda h, i: (i, h)),
                  pl.BlockSpec((n, HEAD_DIM), lambda h, i: (0, h)),
                  pl.BlockSpec((n, HEAD_DIM), lambda h, i: (0, N_HEADS + h)),
                  pl.BlockSpec((bq, HEAD_DIM), lambda h, i: (i, h)),
                  pl.BlockSpec((None, bq, 128), lambda h, i: (h, i, 0)),
                  pl.BlockSpec((None, None, 8, 128), lambda h, i: (h, i, 0, 0))],
        out_specs=[pl.BlockSpec((bq, HEAD_DIM), lambda h, i: (i, h)),
                   pl.BlockSpec((n, HEAD_DIM), lambda h, i: (0, h)),
                   pl.BlockSpec((n, HEAD_DIM), lambda h, i: (0, h))],
        out_shape=[jax.ShapeDtypeStruct((n, D_MODEL), BF16), jax.ShapeDtypeStruct((n, D_MODEL), F32),
                   jax.ShapeDtypeStruct((n, D_MODEL), F32)],
        scratch_shapes=[pltpu.VMEM((bq, bq), F32), pltpu.VMEM((bq, bq), F32), pltpu.VMEM((bq, bq), BF16),
                        pltpu.VMEM((bq, bq), BF16)],
        compiler_params=_params("arbitrary", "arbitrary"),
        ride=ride, first=lambda: (pl.program_id(0) == 0) & (pl.program_id(1) == 0),
        last=lambda: (pl.program_id(0) == N_HEADS - 1) & (pl.program_id(1) == nq - 1))


def _loss_head(y, target, name, tm=1024):
    n, d = y.shape

    def body(y_ref, t_ref, dy_ref, loss_ref):
        err = y_ref[...] - t_ref[...]
        dy_ref[...] = err * (1.0 / d)
        part = 0.5 * jnp.sum(jnp.mean(err * err, axis=-1, keepdims=True), axis=0, keepdims=True)
        _acc_rows(loss_ref, pl.program_id(0) == 0, [jnp.broadcast_to(part, (1, 128))])

    row = lambda i: (i, 0)
    return pl.pallas_call(
        body, name=name, grid=(n // tm,),
        in_specs=[pl.BlockSpec((tm, d), row), pl.BlockSpec((tm, d), row)],
        out_specs=[pl.BlockSpec((tm, d), row), pl.BlockSpec((8, 128), lambda i: (0, 0))],
        out_shape=[jax.ShapeDtypeStruct((n, d), F32), jax.ShapeDtypeStruct((8, 128), F32)],
        compiler_params=_params("arbitrary"),
    )(y, target)


def _local_step(x, target, gains, comm):
    g = gains
    w = comm.w
    big = {}
    row = lambda a, i: a[i:i + 1, :]

    proj, h0 = _norm_matmul(x, row(g["mix_pre"], 0), w["gdn_in"], F32, "gdn_in_proj", tn=1408)
    qn, kn, v, bg = _gdn_pre(proj, w["conv"], g["alog"], g["dtb"], "gdn_pre")
    (o_gdn, saved), got = _gdn_scan(qn, kn, v, bg, "gdn_scan", ride=comm.ride("scan"))
    comm.done("scan", got)
    x1, y_mix0, a_gdn = _out_proj((o_gdn, proj), True, x, w["gdn_out"], row(g["mix_post"], 0), g["out_gain"],
                                  "gdn_out_proj", tm=512)
    x2, y_mlp0, h_mlp0, u0, a0 = _mlp_fwd(x1, row(g["mlp_pre"], 0), w["up0"], w["down0"], row(g["mlp_post"], 0), "mlp0")
    kv, h_kv = _norm_matmul(x2, g["kv"], w["kv"], BF16, "kv_proj", tm=2048)
    q, h_q = _norm_matmul(x2, row(g["mix_pre"], 1), w["sb_q"], BF16, "sb_q_proj")
    (o_sb, lsum, spans), got = _sb_attention(q, kv, "sb_attention", ride=comm.ride("sb"))
    comm.done("sb", got)
    x3, y_mix1 = _out_proj((o_sb,), False, x2, w["sb_o"], row(g["mix_post"], 1), None, "sb_out_proj")
    x4, y_mlp1, h_mlp1, u1, a1 = _mlp_fwd(x3, row(g["mlp_pre"], 1), w["up1"], w["down1"], row(g["mlp_post"], 1), "mlp1")
    dx4, loss = _loss_head(x4, target, "loss_head")

    dx3, dy_mlp1, du1, dg_mlp1 = _mlp_bwd(dx4, y_mlp1, row(g["mlp_post"], 1), w["down1"], u1, w["up1"], x3,
                                          row(g["mlp_pre"], 1), "mlp1_bwd")
    big["down1"] = _matmul_tn(a1, dy_mlp1, 1, "d_down1")[0]
    big["up1"] = _matmul_tn(h_mlp1, du1, N_DEV, "d_up1", tk=2048)
    dy_mix1, dg_post1, do_sb = _out_proj_bwd(dx3, y_mix1, w["sb_o"], row(g["mix_post"], 1), None, "sb_out_proj_bwd")
    big["sb_o"] = _matmul_tn(o_sb, dy_mix1, 1, "d_sb_o")[0]
    (dq, dk, dv), got = _sb_attention_bwd(q, kv, do_sb, lsum, spans, "sb_attention_bwd", ride=comm.ride("sb_bwd", big))
    comm.done("sb_bwd", got)
    big["sb_q"] = _matmul_tn(h_q, dq, 1, "d_sb_q")[0]
    big["kv"] = jnp.concatenate([_matmul_tn(h_kv, dk, N_DEV // 2, "d_w_k", tk=2048),
                                 _matmul_tn(h_kv, dv, N_DEV // 2, "d_w_v", tk=2048)],
                                axis=0)
    t, dg_pre1 = _norm_matmul_bwd(dq, w["sb_q"], x2, row(g["mix_pre"], 1), dx3, "sb_q_proj_bwd")
    t, dg_kv_k = _norm_matmul_bwd(dk, w["kv"][:N_DEV // 2], x2, g["kv"], t, "k_proj_bwd", tn=256)
    dx2, dg_kv_v = _norm_matmul_bwd(dv, w["kv"][N_DEV // 2:], x2, g["kv"], t, "v_proj_bwd", tn=256)

    dx1, dy_mlp0, du0, dg_mlp0 = _mlp_bwd(dx2, y_mlp0, row(g["mlp_post"], 0), w["down0"], u0, w["up0"], x1,
                                          row(g["mlp_pre"], 0), "mlp0_bwd")
    big["down0"] = _matmul_tn(a0, dy_mlp0, 1, "d_down0")[0]
    big["up0"] = _matmul_tn(h_mlp0, du0, N_DEV, "d_up0", tk=2048)
    dy_mix0, dg_post0, do_gdn, dgate, d_out_gain = _out_proj_bwd(
        dx1, y_mix0, w["gdn_out"], row(g["mix_post"], 0), (o_gdn, proj, g["out_gain"]), "gdn_out_proj_bwd", tm=512)
    big["gdn_out"] = _matmul_tn(a_gdn, dy_mix0, 1, "d_gdn_out")[0]
    (dqn, dkn, dvv, dbg), got = _gdn_scan_bwd(qn, kn, v, bg, saved, do_gdn, "gdn_scan_bwd",
                                              ride=comm.ride("scan_bwd", big))
    comm.done("scan_bwd", got)
    dproj, d_conv, d_gates = _gdn_pre_bwd(proj, w["conv"], g["alog"], g["dtb"], dqn, dkn, dvv, dbg, dgate, "gdn_pre_bwd")
    big["gdn_in"] = _matmul_tn(h0, dproj, 1, "d_gdn_in", tb=1408, tk=512)[0]
    ride = comm.ride("in_bwd", big)
    grad_x, dg_pre0, *got = _norm_matmul_bwd(dproj, w["gdn_in"], x, row(g["mix_pre"], 0), dx1, "gdn_in_proj_bwd", tn=1408,
                                             ride=ride)
    comm.done("in_bwd", got[0] if got else [])

    small = {"pre0": dg_pre0, "pre1": dg_pre1, "post0": dg_post0, "post1": dg_post1, "mlp0": dg_mlp0, "mlp1": dg_mlp1,
             "kv_k": dg_kv_k, "kv_v": dg_kv_v, "gates": d_gates, "out_gain": d_out_gain, "conv": d_conv, "loss": loss}
    return grad_x, big, small


class _LocalOnly:
    def __init__(self, weights):
        self.w = weights

    def ride(self, stage, grads=None):
        return None

    def done(self, stage, outs):
        pass


GATHER_STAGES = {"scan": ("gdn_out", "up0", "down0", "kv", "sb_q"), "sb": ("sb_o", "up1", "down1")}
SCATTER_STAGES = {"sb_bwd": ("up1", "down1", "sb_o"), "scan_bwd": ("sb_q", "kv", "up0", "down0", "gdn_out"),
                  "in_bwd": ("gdn_in",)}


def _whole_weight(name, gathered):
    d = D_MODEL
    if name in ("up0", "up1", "kv"):
        return gathered
    if name == "gdn_in":
        whole = gathered.transpose(1, 0, 2).reshape(d, GDN_IN_COLS)
        return jnp.pad(whole, ((0, 0), (0, GDN_IN_PAD - GDN_IN_COLS)))[None]
    if name == "conv":
        return gathered.transpose(1, 0, 2).reshape(CONV_WIDTH, QKV)
    whole = gathered.reshape(gathered.shape[0] * gathered.shape[1], d)
    return whole[None] if name == "sb_q" else whole


def _owner_blocks(name, grad):
    if name in ("up0", "up1", "kv"):
        blocks = grad
    elif name == "gdn_in":
        blocks = grad[:, :GDN_IN_COLS].reshape(D_MODEL, N_DEV, GDN_IN_COLS // N_DEV).transpose(1, 0, 2)
    else:
        blocks = grad.reshape(N_DEV, grad.shape[0] // N_DEV, grad.shape[1])
    return blocks.astype(BF16)


class _Fsdp:
    def __init__(self, shards, weights):
        self.shards, self.w, self.landed = shards, weights, {}

    def ride(self, stage, grads=None):
        if stage in GATHER_STAGES:
            names = GATHER_STAGES[stage]
            return _Exchange([self.shards[nm] for nm in names], [True] * len(names))
        names = SCATTER_STAGES[stage]
        return _Exchange([_owner_blocks(nm, grads[nm]) for nm in names], [False] * len(names))

    def done(self, stage, outs):
        if stage in GATHER_STAGES:
            for nm, out in zip(GATHER_STAGES[stage], outs):
                self.w[nm] = _whole_weight(nm, out)
        else:
            self.landed.update(zip(SCATTER_STAGES[stage], outs))


def _my_place():
    return lax.axis_index("x"), lax.axis_index("y"), lax.axis_index("c")


class _Exchange:
    def __init__(self, arrays, gather):
        self.arrays, self.gather, self.n = list(arrays), list(gather), len(arrays)
        any_spec = pl.BlockSpec(memory_space=pl.ANY)
        self.in_specs = [any_spec] * self.n
        self.out_specs = [any_spec] * self.n
        self.out_shape = [jax.ShapeDtypeStruct(((N_DEV,) + a.shape) if g else a.shape, a.dtype)
                          for a, g in zip(self.arrays, self.gather)]
        self.scratch = [pltpu.SemaphoreType.DMA((self.n, N_DEV - 1)), pltpu.SemaphoreType.DMA((self.n, N_DEV - 1)),
                        pltpu.SemaphoreType.DMA((self.n,))]

    def _copies(self, ins, outs, sems):
        send_sems, recv_sems, local_sems = sems
        x, y, c = _my_place()
        me = 4 * x + 2 * y + c
        copies = []
        for a in range(self.n):
            src = ins[a] if self.gather[a] else ins[a].at[me]
            copies.append(pltpu.make_async_copy(src, outs[a].at[me], local_sems.at[a]))
        for k in range(1, N_DEV):
            px = 1 - x if k & 4 else x
            py = 1 - y if k & 2 else y
            pc = 1 - c if k & 1 else c
            peer = 4 * px + 2 * py + pc
            for a in range(self.n):
                src = ins[a] if self.gather[a] else ins[a].at[peer]
                copies.append(pltpu.make_async_remote_copy(
                    src_ref=src, dst_ref=outs[a].at[me], send_sem=send_sems.at[a, k - 1], recv_sem=recv_sems.at[a, k - 1],
                    device_id=(px, py, pc), device_id_type=MESH))
        return copies

    def start(self, ins, outs, sems):
        for cp in self._copies(ins, outs, sems):
            cp.start()

    def wait(self, ins, outs, sems):
        for cp in self._copies(ins, outs, sems):
            cp.wait()


def _call(body, operands, *, name, out_shape, in_specs, out_specs, grid=(), scratch_shapes=(), compiler_params=None,
          ride=None, first=None, last=None):
    n_in, n_out, n_scr = len(operands), len(out_shape), len(scratch_shapes)
    if ride is None:
        outs = pl.pallas_call(body, name=name, grid=grid, in_specs=in_specs, out_specs=out_specs, out_shape=out_shape,
                              scratch_shapes=scratch_shapes, compiler_params=compiler_params)(*operands)
        return list(outs), []
    r = ride.n

    def riding(*refs):
        ins, r_ins = refs[:n_in], refs[n_in:n_in + r]
        outs, r_outs = refs[n_in + r:n_in + r + n_out], refs[n_in + r + n_out:n_in + 2 * r + n_out]
        scr, sems = refs[n_in + 2 * r + n_out:n_in + 2 * r + n_out + n_scr], refs[n_in + 2 * r + n_out + n_scr:]

        @pl.when(first())
        def _():
            ride.start(r_ins, r_outs, sems)

        body(*ins, *outs, *scr)

        @pl.when(last())
        def _():
            ride.wait(r_ins, r_outs, sems)

    outs = pl.pallas_call(
        riding, name=name, grid=grid, in_specs=list(in_specs) + ride.in_specs, out_specs=list(out_specs) + ride.out_specs,
        out_shape=list(out_shape) + ride.out_shape, scratch_shapes=list(scratch_shapes) + ride.scratch,
        compiler_params=compiler_params)(*operands, *ride.arrays)
    return list(outs[:n_out]), list(outs[n_out:])


def _exchange(arrays, gather, name):
    ride = _Exchange(arrays, gather)

    def body(*refs):
        ins, outs, sems = refs[:ride.n], refs[ride.n:2 * ride.n], refs[2 * ride.n:]
        ride.start(ins, outs, sems)
        ride.wait(ins, outs, sems)

    return pl.pallas_call(body, name=name, in_specs=ride.in_specs, out_specs=ride.out_specs, out_shape=ride.out_shape,
                          scratch_shapes=ride.scratch)(*arrays)


def _adamw_math(g, w, m, v):
    m = ADAM_B1 * m + (1.0 - ADAM_B1) * g
    v = ADAM_B2 * v + (1.0 - ADAM_B2) * jnp.square(g)
    m_hat = m / (1.0 - ADAM_B1 ** ADAM_STEP)
    v_hat = v / (1.0 - ADAM_B2 ** ADAM_STEP)
    delta = -ADAM_LR * (m_hat / (jnp.sqrt(v_hat) + ADAM_EPS) + ADAM_WD * w)
    return delta, m, v


def _sum_devices(ref):
    total = ref[0].astype(F32)
    for d in range(1, N_DEV):
        total = total + ref[d].astype(F32)
    return total


def _reduce_adamw(landed, w, m, v, name, tr=256):
    r, c = w.shape
    tr = min(tr, r)
    assert r % tr == 0

    def body(l_ref, w_ref, m_ref, v_ref, g_ref, d_ref, nm_ref, nv_ref):
        g = _sum_devices(l_ref)
        g_ref[...] = g
        d_ref[...], nm_ref[...], nv_ref[...] = _adamw_math(g, w_ref[...], m_ref[...], v_ref[...])

    blk = pl.BlockSpec((tr, c), lambda i: (i, 0))
    return pl.pallas_call(
        body, name=name, grid=(r // tr,),
        in_specs=[pl.BlockSpec((N_DEV, tr, c), lambda i: (0, i, 0)), blk, blk, blk],
        out_specs=[blk] * 4, out_shape=[jax.ShapeDtypeStruct((r, c), F32)] * 4,
        compiler_params=_params("parallel"),
    )(landed, w, m, v)


def _adamw(g, w, m, v, name):
    def body(g_ref, w_ref, m_ref, v_ref, d_ref, nm_ref, nv_ref):
        d_ref[...], nm_ref[...], nv_ref[...] = _adamw_math(g_ref[...], w_ref[...], m_ref[...], v_ref[...])

    return pl.pallas_call(body, name=name, out_shape=[jax.ShapeDtypeStruct(w.shape, F32)] * 3)(g, w, m, v)


SMALL_ROWS = ("mix_pre", "mix_post", "mlp_pre", "mlp_post")


def _small_update(landed, params, name):
    layout = {
        "mix_pre": [("pre0", 0), ("pre1", 0)], "mix_post": [("post0", 0), ("post1", 0)],
        "mlp_pre": [("mlp0", 0), ("mlp1", 0)], "mlp_post": [("mlp0", 1), ("mlp1", 1)],
        "kv": [("kv_k", 0)], "alog": [("gates", 0)], "dtb": [("gates", 1)], "out_gain": [("out_gain", 0)],
    }
    landed_names = sorted(landed)
    param_names = sorted(params)
    n_l, n_p = len(landed_names), len(param_names)

    def body(*refs):
        l_refs = dict(zip(landed_names, refs[:n_l]))
        p_refs = {p: refs[n_l + 3 * i:n_l + 3 * i + 3] for i, p in enumerate(param_names)}
        outs = refs[n_l + 3 * n_p:]
        o_refs = {p: outs[4 * i:4 * i + 4] for i, p in enumerate(param_names)}
        conv_ref, loss_ref = outs[4 * n_p:]
        sums = {nm: _sum_devices(l_refs[nm]) for nm in landed_names}
        sums["kv_k"] = sums["kv_k"] + sums["kv_v"]
        for p in param_names:
            w_ref, m_ref, v_ref = p_refs[p]
            g_ref, d_ref, nm_ref, nv_ref = o_refs[p]
            for r, (src, src_row) in enumerate(layout[p]):
                g = sums[src][src_row:src_row + 1, :]
                g_ref[r:r + 1, :] = g
                d, nm, nv = _adamw_math(g, w_ref[r:r + 1, :], m_ref[r:r + 1, :], v_ref[r:r + 1, :])
                d_ref[r:r + 1, :] = d
                nm_ref[r:r + 1, :] = nm
                nv_ref[r:r + 1, :] = nv
        conv_ref[...] = sums["conv"]
        loss_ref[...] = sums["loss"]

    args = [landed[nm] for nm in landed_names]
    out_shape = []
    for p in param_names:
        args += list(params[p])
        out_shape += [jax.ShapeDtypeStruct(params[p][0].shape, F32)] * 4
    out_shape += [jax.ShapeDtypeStruct(landed["conv"].shape[1:], F32), jax.ShapeDtypeStruct(landed["loss"].shape[1:], F32)]
    outs = pl.pallas_call(body, name=name, out_shape=out_shape)(*args)
    result = {p: tuple(outs[4 * i:4 * i + 4]) for i, p in enumerate(param_names)}
    result["conv"], result["loss"] = outs[4 * n_p], outs[4 * n_p + 1]
    return result


def _lanes(vec, offset):
    return jnp.pad(vec[None, :], ((0, 0), (offset, 128 - offset - vec.shape[0])))


def kernel(x, mix_pre_gain, mix_post_gain, mlp_pre_gain, mlp_post_gain, mlp_w_up, mlp_w_down, gdn_w_in, gdn_conv_w, gdn_a_log, gdn_dt_bias, gdn_out_gain, gdn_w_out, kv_gain, w_kv, sb_w_q, sb_w_o, loss_target, m_mix_pre_gain, m_mix_post_gain, m_mlp_pre_gain, m_mlp_post_gain, m_mlp_w_up, m_mlp_w_down, m_gdn_w_in, m_gdn_conv_w, m_gdn_a_log, m_gdn_dt_bias, m_gdn_out_gain, m_gdn_w_out, m_kv_gain, m_w_kv, m_sb_w_q, m_sb_w_o, v_mix_pre_gain, v_mix_post_gain, v_mlp_pre_gain, v_mlp_post_gain, v_mlp_w_up, v_mlp_w_down, v_gdn_w_in, v_gdn_conv_w, v_gdn_a_log, v_gdn_dt_bias, v_gdn_out_gain, v_gdn_w_out, v_kv_gain, v_w_kv, v_sb_w_q, v_sb_w_o):
    me = 4 * lax.axis_index("x") + 2 * lax.axis_index("y") + lax.axis_index("c")
    bf = lambda a: a.astype(BF16)

    shards = {"up0": bf(mlp_w_up[0]), "up1": bf(mlp_w_up[1]), "down0": bf(mlp_w_down[0]), "down1": bf(mlp_w_down[1]),
              "gdn_out": bf(gdn_w_out[0]), "kv": bf(w_kv), "sb_q": bf(sb_w_q[0]), "sb_o": bf(sb_w_o[0])}
    gdn_in, conv = _exchange([bf(gdn_w_in[0]), gdn_conv_w[0]], [True, True], "gather_first_weights")
    comm = _Fsdp(shards, {"gdn_in": _whole_weight("gdn_in", gdn_in), "conv": _whole_weight("conv", conv)})
    gains = {"mix_pre": mix_pre_gain, "mix_post": mix_post_gain, "mlp_pre": mlp_pre_gain, "mlp_post": mlp_post_gain,
             "kv": kv_gain[None, :], "alog": _lanes(gdn_a_log[0], N_HEADS), "dtb": _lanes(gdn_dt_bias[0], N_HEADS),
             "out_gain": gdn_out_gain}

    grad_x, big, small = _local_step(x[0], loss_target[0], gains, comm)

    small_names = ["pre0", "pre1", "post0", "post1", "mlp0", "mlp1", "kv_k", "kv_v", "gates", "out_gain", "conv", "loss"]
    landed = _exchange([small[nm] for nm in small_names], [True] * len(small_names), "exchange_small_grads")
    small_landed = dict(zip(small_names, landed))

    results = {}
    big_params = [("mlp_w_up0", "up0", mlp_w_up[0], m_mlp_w_up[0], v_mlp_w_up[0]),
                  ("mlp_w_up1", "up1", mlp_w_up[1], m_mlp_w_up[1], v_mlp_w_up[1]),
                  ("mlp_w_down0", "down0", mlp_w_down[0], m_mlp_w_down[0], v_mlp_w_down[0]),
                  ("mlp_w_down1", "down1", mlp_w_down[1], m_mlp_w_down[1], v_mlp_w_down[1]),
                  ("gdn_w_in", "gdn_in", gdn_w_in[0], m_gdn_w_in[0], v_gdn_w_in[0]),
                  ("gdn_w_out", "gdn_out", gdn_w_out[0], m_gdn_w_out[0], v_gdn_w_out[0]),
                  ("w_kv", "kv", w_kv, m_w_kv, v_w_kv), ("sb_w_q", "sb_q", sb_w_q[0], m_sb_w_q[0], v_sb_w_q[0]),
                  ("sb_w_o", "sb_o", sb_w_o[0], m_sb_w_o[0], v_sb_w_o[0])]
    for nm, short, w_, m_, v_ in big_params:
        results[nm] = _reduce_adamw(comm.landed[short], w_, m_, v_, "adamw_" + nm)
    lanes8 = lambda a: _lanes(a[0], N_HEADS)
    small_params = {
        "mix_pre": (mix_pre_gain, m_mix_pre_gain, v_mix_pre_gain), "mix_post": (mix_post_gain, m_mix_post_gain, v_mix_post_gain),
        "mlp_pre": (mlp_pre_gain, m_mlp_pre_gain, v_mlp_pre_gain), "mlp_post": (mlp_post_gain, m_mlp_post_gain, v_mlp_post_gain),
        "kv": (kv_gain[None, :], m_kv_gain[None, :], v_kv_gain[None, :]),
        "alog": (lanes8(gdn_a_log), lanes8(m_gdn_a_log), lanes8(v_gdn_a_log)),
        "dtb": (lanes8(gdn_dt_bias), lanes8(m_gdn_dt_bias), lanes8(v_gdn_dt_bias)),
        "out_gain": (gdn_out_gain, m_gdn_out_gain, v_gdn_out_gain),
    }
    sm = _small_update(small_landed, small_params, "small_update")
    conv_cols = QKV // N_DEV
    g_conv = lax.dynamic_slice(sm["conv"], (0, me * conv_cols), (8, conv_cols))[:CONV_WIDTH]
    conv_res = (g_conv,) + tuple(_adamw(g_conv, gdn_conv_w[0], m_gdn_conv_w[0], v_gdn_conv_w[0], "adamw_conv"))

    stack2 = lambda a, b: tuple(jnp.stack([p, q]) for p, q in zip(results[a], results[b]))
    lead = lambda t: tuple(a[None] for a in t)
    heads8 = lambda t: tuple(a[:, N_HEADS:2 * N_HEADS] for a in t)
    per_weight = [
        sm["mix_pre"], sm["mix_post"], sm["mlp_pre"], sm["mlp_post"],
        stack2("mlp_w_up0", "mlp_w_up1"), stack2("mlp_w_down0", "mlp_w_down1"),
        lead(results["gdn_w_in"]), lead(conv_res), heads8(sm["alog"]), heads8(sm["dtb"]), sm["out_gain"],
        lead(results["gdn_w_out"]), tuple(a[0] for a in sm["kv"]), results["w_kv"], lead(results["sb_w_q"]), lead(results["sb_w_o"]),
    ]
    grads, deltas, new_ms, new_vs = zip(*per_weight)
    return (sm["loss"][0, 0], grad_x[None], *grads, *deltas, *new_ms, *new_vs)
```

```python
import functools
import math

import jax
import jax.numpy as jnp
from jax import lax
from jax.experimental import pallas as pl
from jax.experimental.pallas import tpu as pltpu

F32 = jnp.float32
BF16 = jnp.bfloat16

N_DEV = 8
D_MODEL = 1024
D_FF = 4096
N_HEADS = 8
HEAD_DIM = 128
CHUNK = 64
CONV_WIDTH = 4
GDN_IN_COLS = 4 * D_MODEL + 2 * N_HEADS
GDN_IN_PAD = 4 * D_MODEL + 128
EPS = 1e-6

ADAM_LR = 0.001
ADAM_B1 = 0.9
ADAM_B2 = 0.999
ADAM_EPS = 1e-08
ADAM_WD = 0.01
ADAM_STEP = 10

VMEM_LIMIT_BYTES = 56 * 1024 * 1024
MESH = pl.DeviceIdType.MESH


def _params(*semantics):
    return pltpu.CompilerParams(dimension_semantics=semantics, vmem_limit_bytes=VMEM_LIMIT_BYTES)


def _dims(ta, tb):
    return (((0,) if ta else (1,), (1,) if tb else (0,)), ((), ()))


def _dot(a, b, ta=False, tb=False):
    return lax.dot_general(a.astype(BF16), b.astype(BF16), _dims(ta, tb), preferred_element_type=F32)


def _dot_f32(a, b, ta=False, tb=False):
    return lax.dot_general(a, b, _dims(ta, tb), precision=lax.Precision.HIGHEST, preferred_element_type=F32)


def _dot_3x(a, b, ta=False, tb=False):
    return lax.dot_general(a, b, _dims(ta, tb), precision=lax.Precision.HIGH, preferred_element_type=F32)


def _make_mm(dot):
    @functools.partial(jax.custom_vjp, nondiff_argnums=(2, 3))
    def mm(a, b, ta, tb):
        return dot(a, b, ta, tb)

    def fwd(a, b, ta, tb):
        return dot(a, b, ta, tb), (a, b)

    def bwd(ta, tb, res, g):
        a, b = res
        if not ta and not tb:
            return mm(g, b, False, True), mm(a, g, True, False)
        if not ta and tb:
            return mm(g, b, False, False), mm(g, a, True, False)
        if ta and not tb:
            return mm(b, g, False, True), mm(a, g, False, False)
        raise NotImplementedError

    mm.defvjp(fwd, bwd)
    return mm


_mm = _make_mm(_dot)
_mm_f32 = _make_mm(_dot_f32)


def _rms(x, gain):
    r = lax.rsqrt(jnp.mean(x * x, axis=-1, keepdims=True) + EPS)
    return x * r * gain


def _rms_bwd(x, gain, dy):
    r = lax.rsqrt(jnp.mean(x * x, axis=-1, keepdims=True) + EPS)
    xh = x * r
    dgain = jnp.sum(dy * xh, axis=0, keepdims=True)
    dxh = dy * gain
    dx = r * (dxh - xh * jnp.mean(dxh * xh, axis=-1, keepdims=True))
    return dx, dgain


def _silu(x):
    return x / (1.0 + jnp.exp(-x))


def _acc_rows(ref, first, rows):
    @pl.when(first)
    def _():
        ref[...] = jnp.zeros_like(ref)

    for r, val in enumerate(rows):
        ref[r:r + 1, :] += val


def _norm_matmul(x, gain, w, out_dtype, name, tm=1024, tn=1024):
    n, d = x.shape
    g, _, wc = w.shape
    tm, tn = min(tm, n), min(tn, wc)
    per = wc // tn
    assert n % tm == 0 and wc % tn == 0

    def body(x_ref, gain_ref, w_ref, out_ref, h_ref):
        @pl.when(pl.program_id(1) == 0)
        def _():
            h_ref[...] = _rms(x_ref[...], gain_ref[...]).astype(BF16)

        out_ref[...] = jnp.dot(h_ref[...], w_ref[...], preferred_element_type=F32).astype(out_dtype)

    return pl.pallas_call(
        body, name=name, grid=(n // tm, g * per),
        in_specs=[pl.BlockSpec((tm, d), lambda i, j: (i, 0)),
                  pl.BlockSpec((1, d), lambda i, j: (0, 0)),
                  pl.BlockSpec((None, d, tn), lambda i, j: (j // per, 0, j % per))],
        out_specs=[pl.BlockSpec((tm, tn), lambda i, j: (i, j)),
                   pl.BlockSpec((tm, d), lambda i, j: (i, 0))],
        out_shape=[jax.ShapeDtypeStruct((n, g * wc), out_dtype), jax.ShapeDtypeStruct((n, d), BF16)],
        compiler_params=_params("parallel", "arbitrary"),
    )(x, gain, w)


def _norm_matmul_bwd(dout, w, x, gain, add, name, tm=1024, tn=512, ride=None):
    n, d = x.shape
    g, _, wc = w.shape
    tm, tn = min(tm, n), min(tn, wc)
    per = wc // tn
    ncol = g * per

    def body(dout_ref, w_ref, x_ref, gain_ref, add_ref, dx_ref, dgain_ref, acc_ref):
        i, j = pl.program_id(0), pl.program_id(1)

        @pl.when(j == 0)
        def _():
            acc_ref[...] = jnp.zeros_like(acc_ref)

        acc_ref[...] += _dot(dout_ref[...], w_ref[...], tb=True)

        @pl.when(j == ncol - 1)
        def _():
            dx, dgain = _rms_bwd(x_ref[...], gain_ref[...], acc_ref[...])
            dx_ref[...] = add_ref[...] + dx
            _acc_rows(dgain_ref, i == 0, [dgain])

    nrow = n // tm
    (dx, dgain), got = _call(
        body, [dout, w, x, gain, add], name=name, grid=(nrow, ncol),
        in_specs=[pl.BlockSpec((tm, tn), lambda i, j: (i, j)),
                  pl.BlockSpec((None, d, tn), lambda i, j: (j // per, 0, j % per)),
                  pl.BlockSpec((tm, d), lambda i, j: (i, 0)),
                  pl.BlockSpec((1, d), lambda i, j: (0, 0)),
                  pl.BlockSpec((tm, d), lambda i, j: (i, 0))],
        out_specs=[pl.BlockSpec((tm, d), lambda i, j: (i, 0)),
                   pl.BlockSpec((8, d), lambda i, j: (0, 0))],
        out_shape=[jax.ShapeDtypeStruct((n, d), F32), jax.ShapeDtypeStruct((8, d), F32)],
        scratch_shapes=[pltpu.VMEM((tm, d), F32)],
        compiler_params=_params("arbitrary", "arbitrary"),
        ride=ride, first=lambda: (pl.program_id(0) == 0) & (pl.program_id(1) == 0),
        last=lambda: (pl.program_id(0) == nrow - 1) & (pl.program_id(1) == ncol - 1))
    return (dx, dgain) if ride is None else (dx, dgain, got)


def _matmul_tn(a, b, groups, name, ta=1024, tb=1024, tk=1024):
    n, ka = a.shape
    _, kb = b.shape
    wc = kb // groups
    ta, tb, tk = min(ta, ka), min(tb, wc), min(tk, n)
    per = wc // tb
    nk = n // tk
    assert ka % ta == 0 and wc % tb == 0 and n % tk == 0

    def body(a_ref, b_ref, out_ref, acc_ref):
        k = pl.program_id(2)

        @pl.when(k == 0)
        def _():
            acc_ref[...] = jnp.zeros_like(acc_ref)

        acc_ref[...] += _dot(a_ref[...], b_ref[...], ta=True)

        @pl.when(k == nk - 1)
        def _():
            out_ref[...] = acc_ref[...].astype(BF16)

    return pl.pallas_call(
        body, name=name, grid=(ka // ta, groups * per, nk),
        in_specs=[pl.BlockSpec((tk, ta), lambda i, j, k: (k, i)),
                  pl.BlockSpec((tk, tb), lambda i, j, k: (k, j))],
        out_specs=pl.BlockSpec((None, ta, tb), lambda i, j, k: (j // per, i, j % per)),
        out_shape=jax.ShapeDtypeStruct((groups, ka, wc), BF16),
        scratch_shapes=[pltpu.VMEM((ta, tb), F32)],
        compiler_params=_params("parallel", "parallel", "arbitrary"),
    )(a, b)


def _gated_head_norm(o, gate, out_gain):
    parts = []
    for h in range(N_HEADS):
        sl = slice(h * HEAD_DIM, (h + 1) * HEAD_DIM)
        parts.append(_rms(o[:, sl], out_gain) * _silu(gate[:, sl]))
    return parts


def _out_proj(a_inputs, gated, x_in, w, gain, out_gain, name, tm=1024):
    n, d = x_in.shape
    k = w.shape[0]
    tm = min(tm, n)

    def body(*refs):
        if gated:
            o_ref, gate_ref, og_ref, x_ref, w_ref, gain_ref, xo_ref, y_ref, a_ref = refs
            parts = _gated_head_norm(o_ref[...], gate_ref[...], og_ref[...])
            for h, part in enumerate(parts):
                a_ref[:, h * HEAD_DIM:(h + 1) * HEAD_DIM] = part.astype(BF16)
            a = a_ref[...]
        else:
            a_in_ref, x_ref, w_ref, gain_ref, xo_ref, y_ref = refs
            a = a_in_ref[...]
        y = jnp.dot(a, w_ref[...], preferred_element_type=F32)
        y_ref[...] = y
        xo_ref[...] = x_ref[...] + _rms(y, gain_ref[...])

    row = lambda i: (i, 0)
    const = lambda i: (0, 0)
    if gated:
        a_specs = [pl.BlockSpec((tm, k), row), pl.BlockSpec((tm, D_MODEL), lambda i: (i, 3)),
                   pl.BlockSpec((1, HEAD_DIM), const)]
        a_args = list(a_inputs) + [out_gain]
    else:
        a_specs = [pl.BlockSpec((tm, k), row)]
        a_args = list(a_inputs)
    out_specs = [pl.BlockSpec((tm, d), row), pl.BlockSpec((tm, d), row)]
    out_shape = [jax.ShapeDtypeStruct((n, d), F32), jax.ShapeDtypeStruct((n, d), F32)]
    if gated:
        out_specs.append(pl.BlockSpec((tm, k), row))
        out_shape.append(jax.ShapeDtypeStruct((n, k), BF16))
    return pl.pallas_call(
        body, name=name, grid=(n // tm,),
        in_specs=a_specs + [pl.BlockSpec((tm, d), row), pl.BlockSpec((k, d), const), pl.BlockSpec((1, d), const)],
        out_specs=out_specs, out_shape=out_shape,
        compiler_params=_params("parallel"),
    )(*a_args, x_in, w, gain)


def _out_proj_bwd(dxo, y, w, gain, gated_inputs, name, tm=1024):
    n, d = dxo.shape
    k = w.shape[0]
    tm = min(tm, n)
    gated = gated_inputs is not None

    def body(*refs):
        if gated:
            (dxo_ref, y_ref, w_ref, gain_ref, o_ref, gate_ref, og_ref,
             dy_ref, dgain_ref, do_ref, dgate_ref, dog_ref) = refs
        else:
            dxo_ref, y_ref, w_ref, gain_ref, dy_ref, dgain_ref, da_ref = refs
        first = pl.program_id(0) == 0
        dy, dgain = _rms_bwd(y_ref[...], gain_ref[...], dxo_ref[...])
        dy_ref[...] = dy.astype(BF16)
        _acc_rows(dgain_ref, first, [dgain])
        da = _dot(dy_ref[...], w_ref[...], tb=True)
        if not gated:
            da_ref[...] = da.astype(BF16)
            return
        og = og_ref[...]
        dog = jnp.zeros_like(og)
        for h in range(N_HEADS):
            sl = slice(h * HEAD_DIM, (h + 1) * HEAD_DIM)
            fn = lambda o_h, g_h, gn: _rms(o_h, gn) * _silu(g_h)
            _, vjp = jax.vjp(fn, o_ref[:, sl], gate_ref[:, sl], og)
            do_h, dgate_h, dog_h = vjp(da[:, sl])
            do_ref[:, sl] = do_h
            dgate_ref[:, sl] = dgate_h
            dog = dog + dog_h
        _acc_rows(dog_ref, first, [dog])

    row = lambda i: (i, 0)
    const = lambda i: (0, 0)
    in_specs = [pl.BlockSpec((tm, d), row), pl.BlockSpec((tm, d), row), pl.BlockSpec((k, d), const),
                pl.BlockSpec((1, d), const)]
    args = [dxo, y, w, gain]
    out_specs = [pl.BlockSpec((tm, d), row), pl.BlockSpec((8, d), const)]
    out_shape = [jax.ShapeDtypeStruct((n, d), BF16), jax.ShapeDtypeStruct((8, d), F32)]
    if gated:
        in_specs += [pl.BlockSpec((tm, k), row), pl.BlockSpec((tm, D_MODEL), lambda i: (i, 3)),
                     pl.BlockSpec((1, HEAD_DIM), const)]
        args += list(gated_inputs)
        out_specs += [pl.BlockSpec((tm, k), row), pl.BlockSpec((tm, k), row), pl.BlockSpec((8, HEAD_DIM), const)]
        out_shape += [jax.ShapeDtypeStruct((n, k), F32), jax.ShapeDtypeStruct((n, k), F32),
                      jax.ShapeDtypeStruct((8, HEAD_DIM), F32)]
    else:
        out_specs.append(pl.BlockSpec((tm, k), row))
        out_shape.append(jax.ShapeDtypeStruct((n, k), BF16))
    return pl.pallas_call(
        body, name=name, grid=(n // tm,), in_specs=in_specs, out_specs=out_specs, out_shape=out_shape,
        compiler_params=_params("arbitrary"),
    )(*args)


def _mlp_fwd(x_in, g_pre, w_up, w_down, g_post, name, tm=1024):
    n, d = x_in.shape
    g, _, wc = w_up.shape
    tm = min(tm, n)

    def body(x_ref, gpre_ref, wup_ref, wdown_ref, gpost_ref, xo_ref, y_ref, h_ref, u_ref, a_ref, acc_ref):
        j = pl.program_id(1)

        @pl.when(j == 0)
        def _():
            h_ref[...] = _rms(x_ref[...], gpre_ref[...]).astype(BF16)
            acc_ref[...] = jnp.zeros_like(acc_ref)

        u = jnp.dot(h_ref[...], wup_ref[...], preferred_element_type=F32).astype(BF16)
        u_ref[...] = u
        a = jnp.square(jnp.maximum(u, 0))
        a_ref[...] = a
        acc_ref[...] += jnp.dot(a, wdown_ref[...], preferred_element_type=F32)

        @pl.when(j == g - 1)
        def _():
            y = acc_ref[...]
            y_ref[...] = y
            xo_ref[...] = x_ref[...] + _rms(y, gpost_ref[...])

    row = lambda i, j: (i, 0)
    const = lambda i, j: (0, 0)
    return pl.pallas_call(
        body, name=name, grid=(n // tm, g),
        in_specs=[pl.BlockSpec((tm, d), row), pl.BlockSpec((1, d), const),
                  pl.BlockSpec((None, d, wc), lambda i, j: (j, 0, 0)),
                  pl.BlockSpec((wc, d), lambda i, j: (j, 0)), pl.BlockSpec((1, d), const)],
        out_specs=[pl.BlockSpec((tm, d), row), pl.BlockSpec((tm, d), row), pl.BlockSpec((tm, d), row),
                   pl.BlockSpec((tm, wc), lambda i, j: (i, j)), pl.BlockSpec((tm, wc), lambda i, j: (i, j))],
        out_shape=[jax.ShapeDtypeStruct((n, d), F32), jax.ShapeDtypeStruct((n, d), F32),
                   jax.ShapeDtypeStruct((n, d), BF16), jax.ShapeDtypeStruct((n, g * wc), BF16),
                   jax.ShapeDtypeStruct((n, g * wc), BF16)],
        scratch_shapes=[pltpu.VMEM((tm, d), F32)],
        compiler_params=_params("parallel", "arbitrary"),
    )(x_in, g_pre, w_up, w_down, g_post)


def _mlp_bwd(dxo, y, g_post, w_down, u, w_up, x_in, g_pre, name, tm=1024):
    n, d = dxo.shape
    g, _, wc = w_up.shape

    def body(dxo_ref, y_ref, gpost_ref, wdown_ref, u_ref, wup_ref, x_ref, gpre_ref,
             dx_ref, dy_ref, du_ref, dgain_ref, acc_ref, dgpost_ref):
        i, j = pl.program_id(0), pl.program_id(1)

        @pl.when(j == 0)
        def _():
            dy, dgpost = _rms_bwd(y_ref[...], gpost_ref[...], dxo_ref[...])
            dy_ref[...] = dy.astype(BF16)
            dgpost_ref[...] = dgpost
            acc_ref[...] = jnp.zeros_like(acc_ref)

        da = _dot(dy_ref[...], wdown_ref[...], tb=True)
        du = (da * (2.0 * jnp.maximum(u_ref[...], 0).astype(F32))).astype(BF16)
        du_ref[...] = du
        acc_ref[...] += _dot(du, wup_ref[...], tb=True)

        @pl.when(j == g - 1)
        def _():
            dx, dgpre = _rms_bwd(x_ref[...], gpre_ref[...], acc_ref[...])
            dx_ref[...] = dxo_ref[...] + dx
            _acc_rows(dgain_ref, i == 0, [dgpre, dgpost_ref[...]])

    row = lambda i, j: (i, 0)
    const = lambda i, j: (0, 0)
    return pl.pallas_call(
        body, name=name, grid=(n // tm, g),
        in_specs=[pl.BlockSpec((tm, d), row), pl.BlockSpec((tm, d), row), pl.BlockSpec((1, d), const),
                  pl.BlockSpec((wc, d), lambda i, j: (j, 0)), pl.BlockSpec((tm, wc), lambda i, j: (i, j)),
                  pl.BlockSpec((None, d, wc), lambda i, j: (j, 0, 0)), pl.BlockSpec((tm, d), row),
                  pl.BlockSpec((1, d), const)],
        out_specs=[pl.BlockSpec((tm, d), row), pl.BlockSpec((tm, d), row),
                   pl.BlockSpec((tm, wc), lambda i, j: (i, j)), pl.BlockSpec((8, d), const)],
        out_shape=[jax.ShapeDtypeStruct((n, d), F32), jax.ShapeDtypeStruct((n, d), BF16),
                   jax.ShapeDtypeStruct((n, g * wc), BF16), jax.ShapeDtypeStruct((8, d), F32)],
        scratch_shapes=[pltpu.VMEM((tm, d), F32), pltpu.VMEM((1, d), F32)],
        compiler_params=_params("arbitrary", "arbitrary"),
    )(dxo, y, g_post, w_down, u, w_up, x_in, g_pre)


QKV = 3 * D_MODEL


def _shifted(x, prev8, s):
    if s == 0:
        return x
    tm = x.shape[0]
    rolled = pltpu.roll(x, s, 0)
    head = pltpu.roll(prev8, s, 0)
    head = jnp.concatenate([head, jnp.zeros((tm - 8, x.shape[1]), x.dtype)], axis=0)
    rows = lax.broadcasted_iota(jnp.int32, x.shape, 0)
    return jnp.where(rows < s, head, rolled)


def _conv(x, prev8, conv_w):
    out = x * conv_w[3:4, :]
    for s in range(1, CONV_WIDTH):
        out = out + _shifted(x, prev8, s) * conv_w[3 - s:4 - s, :]
    return out


def _l2norm(x):
    return x * lax.rsqrt(jnp.sum(x * x, axis=-1, keepdims=True) + EPS)


def _gdn_act_head(cq, ck, cv):
    return _l2norm(_silu(cq)) * (HEAD_DIM ** -0.5), _l2norm(_silu(ck)), _silu(cv)


def _gdn_gates(ba, alog, dtb):
    lane = lax.broadcasted_iota(jnp.int32, ba.shape, 1)
    beta = 1.0 / (1.0 + jnp.exp(-ba))
    t = ba + dtb
    softplus = jnp.maximum(t, 0.0) + jnp.log(1.0 + jnp.exp(-jnp.abs(t)))
    g = -jnp.exp(alog) * softplus
    return jnp.where(lane < N_HEADS, beta, jnp.where(lane < 2 * N_HEADS, g, 0.0))


def _qkv_cols(part, h):
    start = part * D_MODEL + h * HEAD_DIM
    return slice(start, start + HEAD_DIM)


def _prev_rows_spec(tm, cols, colblock, order):
    per = tm // 8
    return pl.BlockSpec((8, cols), lambda i: (jnp.maximum(order(i) * per - 1, 0), colblock))


def _gdn_pre(proj, conv_w, alog, dtb, name, tm=256):
    n = proj.shape[0]

    def body(x_ref, prev_ref, ba_ref, cw_ref, alog_ref, dtb_ref, q_ref, k_ref, v_ref, bg_ref):
        first = pl.program_id(0) == 0
        for h in range(N_HEADS):
            convs = []
            for part in range(3):
                sl = _qkv_cols(part, h)
                prev8 = jnp.where(first, 0.0, prev_ref[:, sl])
                convs.append(_conv(x_ref[:, sl], prev8, cw_ref[:, sl]))
            out = slice(h * HEAD_DIM, (h + 1) * HEAD_DIM)
            q_ref[:, out], k_ref[:, out], v_ref[:, out] = _gdn_act_head(*convs)
        bg_ref[...] = _gdn_gates(ba_ref[...], alog_ref[...], dtb_ref[...])

    row = lambda i: (i, 0)
    const = lambda i: (0, 0)
    ident = lambda i: i
    return pl.pallas_call(
        body, name=name, grid=(n // tm,),
        in_specs=[pl.BlockSpec((tm, QKV), row), _prev_rows_spec(tm, QKV, 0, ident),
                  pl.BlockSpec((tm, 128), lambda i: (i, 4 * D_MODEL // 128)),
                  pl.BlockSpec((CONV_WIDTH, QKV), const), pl.BlockSpec((1, 128), const), pl.BlockSpec((1, 128), const)],
        out_specs=[pl.BlockSpec((tm, D_MODEL), row)] * 3 + [pl.BlockSpec((tm, 128), row)],
        out_shape=[jax.ShapeDtypeStruct((n, D_MODEL), F32)] * 3 + [jax.ShapeDtypeStruct((n, 128), F32)],
        compiler_params=_params("parallel"),
    )(proj, proj, proj, conv_w, alog, dtb)


def _gdn_pre_bwd(proj, conv_w, alog, dtb, dq, dk, dv, dbg, dgate, name, tm=256):
    n = proj.shape[0]
    nt = n // tm

    def body(x_ref, prev_ref, ba_ref, cw_ref, alog_ref, dtb_ref, dq_ref, dk_ref, dv_ref, dbg_ref, dgate_ref,
             dproj_ref, dcw_ref, dgates_ref, carry_ref):
        step = pl.program_id(0)
        tile = nt - 1 - step
        @pl.when(step == 0)
        def _():
            carry_ref[...] = jnp.zeros_like(carry_ref)
            dcw_ref[...] = jnp.zeros_like(dcw_ref)

        rows = lax.broadcasted_iota(jnp.int32, (tm, HEAD_DIM), 0)
        for h in range(N_HEADS):
            cols = [_qkv_cols(part, h) for part in range(3)]
            prevs = [jnp.where(tile == 0, 0.0, prev_ref[:, sl]) for sl in cols]
            convs = [_conv(x_ref[:, sl], prev8, cw_ref[:, sl]) for sl, prev8 in zip(cols, prevs)]
            _, vjp = jax.vjp(_gdn_act_head, *convs)
            out = slice(h * HEAD_DIM, (h + 1) * HEAD_DIM)
            dcs = vjp((dq_ref[:, out], dk_ref[:, out], dv_ref[:, out]))
            for sl, prev8, dc in zip(cols, prevs, dcs):
                x = x_ref[:, sl]
                cw = cw_ref[:, sl]
                dx = dc * cw[3:4, :]
                wrapped = jnp.zeros((8, HEAD_DIM), F32)
                dcw_ref[3:4, sl] += jnp.sum(dc * x, axis=0, keepdims=True)
                for s in range(1, CONV_WIDTH):
                    r = pltpu.roll(dc, tm - s, 0) * cw[3 - s:4 - s, :]
                    dx = dx + jnp.where(rows < tm - s, r, 0.0)
                    wrapped = wrapped + jnp.where(rows[tm - 8:, :] >= tm - s, r[tm - 8:, :], 0.0)
                    dcw_ref[3 - s:4 - s, sl] += jnp.sum(dc * _shifted(x, prev8, s), axis=0, keepdims=True)
                dproj_ref[:, sl] = dx
                dproj_ref[tm - 8:, sl] += carry_ref[:, sl]
                carry_ref[:, sl] = wrapped

        _, vjp = jax.vjp(_gdn_gates, ba_ref[...], alog_ref[...], dtb_ref[...])
        dba, dalog, ddtb = vjp(dbg_ref[...])
        dproj_ref[:, QKV:4 * D_MODEL] = dgate_ref[...]
        dproj_ref[:, 4 * D_MODEL:] = dba
        _acc_rows(dgates_ref, step == 0, [dalog, ddtb])

    rev = lambda i: nt - 1 - i
    row = lambda i: (rev(i), 0)
    const = lambda i: (0, 0)
    return pl.pallas_call(
        body, name=name, grid=(nt,),
        in_specs=[pl.BlockSpec((tm, QKV), row), _prev_rows_spec(tm, QKV, 0, rev),
                  pl.BlockSpec((tm, 128), lambda i: (rev(i), 4 * D_MODEL // 128)),
                  pl.BlockSpec((CONV_WIDTH, QKV), const), pl.BlockSpec((1, 128), const), pl.BlockSpec((1, 128), const),
                  pl.BlockSpec((tm, D_MODEL), row), pl.BlockSpec((tm, D_MODEL), row), pl.BlockSpec((tm, D_MODEL), row),
                  pl.BlockSpec((tm, 128), row), pl.BlockSpec((tm, D_MODEL), row)],
        out_specs=[pl.BlockSpec((tm, GDN_IN_PAD), row), pl.BlockSpec((8, QKV), const), pl.BlockSpec((8, 128), const)],
        out_shape=[jax.ShapeDtypeStruct((n, GDN_IN_PAD), F32), jax.ShapeDtypeStruct((8, QKV), F32),
                   jax.ShapeDtypeStruct((8, 128), F32)],
        scratch_shapes=[pltpu.VMEM((8, QKV), F32)],
        compiler_params=_params("arbitrary"),
    )(proj, proj, proj, conv_w, alog, dtb, dq, dk, dv, dbg, dgate)


@jax.custom_vjp
def _unit_lower_inverses(lowers):
    n = lowers[0].shape[0]
    eye = (lax.broadcasted_iota(jnp.int32, (n, n), 0) == lax.broadcasted_iota(jnp.int32, (n, n), 1)).astype(F32)
    ys = [-low for low in lowers]
    ps = [eye + y for y in ys]
    for _ in range(int(math.log2(CHUNK)) - 1):
        ys = [_dot_3x(y, y) for y in ys]
        ps = [p + _dot_3x(p, y) for p, y in zip(ps, ys)]
    return tuple(ps)


def _unit_lower_inverses_fwd(lowers):
    ts = _unit_lower_inverses(lowers)
    return ts, ts


def _unit_lower_inverses_bwd(ts, dts):
    left = [_dot_3x(t, dt, ta=True) for t, dt in zip(ts, dts)]
    return (tuple(-_dot_3x(l, t, tb=True) for l, t in zip(left, ts)),)


_unit_lower_inverses.defvjp(_unit_lower_inverses_fwd, _unit_lower_inverses_bwd)


GDN_STEP_CHUNKS = 2


def _gdn_chunks(qs, ks, vs, bgs, states):
    c = CHUNK
    heads = range(N_HEADS)
    items = [(j, h) for j in range(len(bgs)) for h in heads]
    row = lax.broadcasted_iota(jnp.int32, (c, c), 0)
    col = lax.broadcasted_iota(jnp.int32, (c, c), 1)
    incl, strict, eye = row >= col, row > col, row == col
    lane = lax.broadcasted_iota(jnp.int32, (c, 128), 1)
    rowc = lax.broadcasted_iota(jnp.int32, (c, 1), 0)
    gc_all = [_mm_f32(incl.astype(F32), bg, False, False) for bg in bgs]
    q, k, v = ([xs[j][h] for j, h in items] for xs in (qs, ks, vs))
    n = range(len(items))
    beta = [jnp.sum(jnp.where(lane == h, bgs[j], 0.0), axis=1, keepdims=True) for j, h in items]
    gc = [jnp.sum(jnp.where(lane == N_HEADS + h, gc_all[j], 0.0), axis=1, keepdims=True) for j, h in items]
    gc_row = [jnp.sum(jnp.where(eye, gc[i], 0.0), axis=0, keepdims=True) for i in n]
    gc_last = [jnp.sum(jnp.where(rowc == c - 1, gc[i], 0.0), axis=0, keepdims=True) for i in n]
    decay = [jnp.where(incl, jnp.exp(jnp.where(incl, gc[i] - gc_row[i], 0.0)), 0.0) for i in n]
    kb = [k[i] * beta[i] for i in n]
    lower = tuple(jnp.where(strict, _mm(kb[i], k[i], False, True) * decay[i], 0.0) for i in n)
    attn = [_mm(q[i], k[i], False, True) * decay[i] for i in n]
    t_mat = _unit_lower_inverses(lower)
    egc = [jnp.exp(gc[i]) for i in n]
    w = [_mm(t_mat[i], kb[i] * egc[i], False, False) for i in n]
    u = [_mm(t_mat[i], v[i] * beta[i], False, False) for i in n]
    qg = [q[i] * egc[i] for i in n]
    kg = [k[i] * jnp.exp(gc_last[i] - gc[i]) for i in n]
    outs, cur = [], list(states)
    for j in range(len(bgs)):
        at = lambda h: j * N_HEADS + h
        v_new = [u[at(h)] - _mm(w[at(h)], cur[h], False, False) for h in heads]
        from_state = [_mm(qg[at(h)], cur[h], False, False) for h in heads]
        outs.append(tuple(from_state[h] + _mm(attn[at(h)], v_new[h], False, False) for h in heads))
        cur = [cur[h] * jnp.exp(gc_last[at(h)]) + _mm(kg[at(h)], v_new[h], True, False) for h in heads]
    return tuple(outs), tuple(cur)


def _chunk_head_slices(ref, nch):
    return tuple(tuple(ref[j * CHUNK:(j + 1) * CHUNK, h * HEAD_DIM:(h + 1) * HEAD_DIM] for h in range(N_HEADS))
                 for j in range(nch))


def _store_chunk_heads(ref, values):
    for j, chunk in enumerate(values):
        for h, val in enumerate(chunk):
            ref[j * CHUNK:(j + 1) * CHUNK, h * HEAD_DIM:(h + 1) * HEAD_DIM] = val


def _gdn_scan(qn, kn, v, bg, name, ride=None):
    n = qn.shape[0]
    nch = GDN_STEP_CHUNKS
    rows = nch * CHUNK
    nc = n // rows

    def body(q_ref, k_ref, v_ref, bg_ref, o_ref, saved_ref, state_ref):
        @pl.when(pl.program_id(0) == 0)
        def _():
            state_ref[...] = jnp.zeros_like(state_ref)

        states = tuple(state_ref[h] for h in range(N_HEADS))
        for h in range(N_HEADS):
            saved_ref[h] = states[h]
        bgs = tuple(bg_ref[j * CHUNK:(j + 1) * CHUNK, :] for j in range(nch))
        outs, new_states = _gdn_chunks(_chunk_head_slices(q_ref, nch), _chunk_head_slices(k_ref, nch),
                                       _chunk_head_slices(v_ref, nch), bgs, states)
        _store_chunk_heads(o_ref, outs)
        for h in range(N_HEADS):
            state_ref[h] = new_states[h]

    row = lambda i: (i, 0)
    return _call(
        body, [qn, kn, v, bg], name=name, grid=(nc,),
        in_specs=[pl.BlockSpec((rows, D_MODEL), row)] * 3 + [pl.BlockSpec((rows, 128), row)],
        out_specs=[pl.BlockSpec((rows, D_MODEL), row),
                   pl.BlockSpec((None, N_HEADS, HEAD_DIM, HEAD_DIM), lambda i: (i, 0, 0, 0))],
        out_shape=[jax.ShapeDtypeStruct((n, D_MODEL), F32),
                   jax.ShapeDtypeStruct((nc, N_HEADS, HEAD_DIM, HEAD_DIM), F32)],
        scratch_shapes=[pltpu.VMEM((N_HEADS, HEAD_DIM, HEAD_DIM), F32)],
        compiler_params=_params("arbitrary"),
        ride=ride, first=lambda: pl.program_id(0) == 0, last=lambda: pl.program_id(0) == nc - 1)


def _gdn_scan_bwd(qn, kn, v, bg, saved, do, name, ride=None):
    n = qn.shape[0]
    nch = GDN_STEP_CHUNKS
    rows = nch * CHUNK
    nc = n // rows

    def body(q_ref, k_ref, v_ref, bg_ref, saved_ref, do_ref, dq_ref, dk_ref, dv_ref, dbg_ref, dstate_ref):
        @pl.when(pl.program_id(0) == 0)
        def _():
            dstate_ref[...] = jnp.zeros_like(dstate_ref)

        states = tuple(saved_ref[h] for h in range(N_HEADS))
        bgs = tuple(bg_ref[j * CHUNK:(j + 1) * CHUNK, :] for j in range(nch))
        _, vjp = jax.vjp(_gdn_chunks, _chunk_head_slices(q_ref, nch), _chunk_head_slices(k_ref, nch),
                         _chunk_head_slices(v_ref, nch), bgs, states)
        dstates = tuple(dstate_ref[h] for h in range(N_HEADS))
        dqs, dks, dvs, dbgs, dprev = vjp((_chunk_head_slices(do_ref, nch), dstates))
        _store_chunk_heads(dq_ref, dqs)
        _store_chunk_heads(dk_ref, dks)
        _store_chunk_heads(dv_ref, dvs)
        for h in range(N_HEADS):
            dstate_ref[h] = dprev[h]
        for j in range(nch):
            dbg_ref[j * CHUNK:(j + 1) * CHUNK, :] = dbgs[j]

    row = lambda i: (nc - 1 - i, 0)
    return _call(
        body, [qn, kn, v, bg, saved, do], name=name, grid=(nc,),
        in_specs=[pl.BlockSpec((rows, D_MODEL), row)] * 3 + [pl.BlockSpec((rows, 128), row),
                  pl.BlockSpec((None, N_HEADS, HEAD_DIM, HEAD_DIM), lambda i: (nc - 1 - i, 0, 0, 0)),
                  pl.BlockSpec((rows, D_MODEL), row)],
        out_specs=[pl.BlockSpec((rows, D_MODEL), row)] * 3 + [pl.BlockSpec((rows, 128), row)],
        out_shape=[jax.ShapeDtypeStruct((n, D_MODEL), F32)] * 3 + [jax.ShapeDtypeStruct((n, 128), F32)],
        scratch_shapes=[pltpu.VMEM((N_HEADS, HEAD_DIM, HEAD_DIM), F32)],
        compiler_params=_params("arbitrary"),
        ride=ride, first=lambda: pl.program_id(0) == 0, last=lambda: pl.program_id(0) == nc - 1)


SB_BQ = 512
SB_BK = 256
SB_SUB = 128
SB_ROWS = 128
SB_SCALE = HEAD_DIM ** -0.5
SB_DEAD = -105.0


def _sb_terms(z, before):
    e = jnp.exp(-jnp.abs(z))
    log_beta = jnp.minimum(z, 0.0) - jnp.log(1.0 + e)
    log_1m = log_beta - z
    if before is not None:
        log_1m = jnp.where(before, log_1m, 0.0)
    return e, log_beta, log_1m


def _tri_ones(n, cmp):
    r = lax.broadcasted_iota(jnp.int32, (n, 2 * n), 0)
    c = lax.broadcasted_iota(jnp.int32, (n, 2 * n), 1)
    return jnp.where((c >= n) | cmp(r, c), 1.0, 0.0).astype(BF16)


def _sums(x, tri_ones):
    both = jnp.dot(x.astype(BF16), tri_ones, preferred_element_type=F32)
    n = x.shape[1]
    return both[:, :n], both[:, n:]


def _sb_chunk_mask(diagonal, r, s):
    if not diagonal or s * SB_SUB + SB_SUB - 1 < r * SB_ROWS:
        return None
    if s * SB_SUB >= r * SB_ROWS + SB_ROWS - 1:
        return "empty"
    rows = r * SB_ROWS + lax.broadcasted_iota(jnp.int32, (SB_ROWS, SB_SUB), 0)
    cols = s * SB_SUB + lax.broadcasted_iota(jnp.int32, (SB_ROWS, SB_SUB), 1)
    return cols < rows


def _sb_attention(q, kv, name, bq=SB_BQ, ride=None):
    n = q.shape[0]
    bq = min(bq, n)
    bk = min(SB_BK, bq)
    nrc = bq // SB_ROWS

    def body(q_ref, k_ref, v_ref, o_ref, l_ref, n_ref, z_scr, a_scr):
        i = pl.program_id(1)
        qb = q_ref[...]
        after = _tri_ones(SB_SUB, lambda r, c: r > c)

        def span(start, kw, c_sum, acc, diagonal):
            start = pl.multiple_of(start, kw)
            k_w = k_ref[pl.ds(start, kw), :]
            v_w = v_ref[pl.ds(start, kw), :]
            z_scr[:, :kw] = lax.dot_general(qb, k_w, _dims(False, True), preferred_element_type=F32)
            c_rows = [c_sum[r * SB_ROWS:(r + 1) * SB_ROWS] for r in range(nrc)]
            for s in reversed(range(kw // SB_SUB)):
                cols = slice(s * SB_SUB, (s + 1) * SB_SUB)
                for r in range(nrc):
                    rows = slice(r * SB_ROWS, (r + 1) * SB_ROWS)
                    before = _sb_chunk_mask(diagonal, r, s)
                    if isinstance(before, str):
                        a_scr[rows, cols] = jnp.zeros((SB_ROWS, SB_SUB), BF16)
                        continue
                    _, log_beta, log_1m = _sb_terms(z_scr[rows, cols] * SB_SCALE, before)
                    tail, total = _sums(log_1m, after)
                    a = jnp.exp(log_beta + c_rows[r] + tail)
                    if before is not None:
                        a = jnp.where(before, a, 0.0)
                    a_scr[rows, cols] = a.astype(BF16)
                    c_rows[r] = c_rows[r] + total
            acc = acc + jnp.dot(a_scr[:, :kw], v_w, preferred_element_type=F32)
            return jnp.concatenate(c_rows, axis=0), acc

        c_sum, acc = span(i * bq, bq, jnp.zeros((bq, SB_SUB), F32), jnp.zeros((bq, HEAD_DIM), F32), True)

        def more(state):
            done, c, _ = state
            return (done < i * (bq // bk)) & (jnp.max(c) > SB_DEAD)

        def step(state):
            done, c, a = state
            c, a = span(i * bq - (done + 1) * bk, bk, c, a, False)
            return done + 1, c, a

        done, c_sum, acc = lax.while_loop(more, step, (jnp.int32(0), c_sum, acc))
        o_ref[...] = acc.astype(BF16)
        l_ref[...] = c_sum
        n_ref[...] = jnp.full((8, 128), done.astype(F32), F32)

    nq = n // bq
    return _call(
        body, [q, kv, kv], name=name, grid=(N_HEADS, nq),
        in_specs=[pl.BlockSpec((bq, HEAD_DIM), lambda h, i: (i, h)),
                  pl.BlockSpec((n, HEAD_DIM), lambda h, i: (0, h)),
                  pl.BlockSpec((n, HEAD_DIM), lambda h, i: (0, N_HEADS + h))],
        out_specs=[pl.BlockSpec((bq, HEAD_DIM), lambda h, i: (i, h)),
                   pl.BlockSpec((None, bq, 128), lambda h, i: (h, i, 0)),
                   pl.BlockSpec((None, None, 8, 128), lambda h, i: (h, i, 0, 0))],
        out_shape=[jax.ShapeDtypeStruct((n, D_MODEL), BF16), jax.ShapeDtypeStruct((N_HEADS, n, 128), F32),
                   jax.ShapeDtypeStruct((N_HEADS, nq, 8, 128), F32)],
        scratch_shapes=[pltpu.VMEM((bq, bq), F32), pltpu.VMEM((bq, bq), BF16)],
        compiler_params=_params("arbitrary", "arbitrary"),
        ride=ride, first=lambda: (pl.program_id(0) == 0) & (pl.program_id(1) == 0),
        last=lambda: (pl.program_id(0) == N_HEADS - 1) & (pl.program_id(1) == nq - 1))


def _sb_attention_bwd(q, kv, do, lsum, spans, name, bq=SB_BQ, ride=None):
    n = q.shape[0]
    bq = min(bq, n)
    bk = min(SB_BK, bq)
    nrc = bq // SB_ROWS

    def body(q_ref, k_ref, v_ref, do_ref, l_ref, n_ref, dq_ref, dk_ref, dv_ref, z_scr, da_scr, a_scr, dz_scr):
        i = pl.program_id(1)

        @pl.when(i == 0)
        def _():
            dk_ref[...] = jnp.zeros_like(dk_ref)
            dv_ref[...] = jnp.zeros_like(dv_ref)

        qb = q_ref[...]
        dob = do_ref[...]
        lt_rows = [l_ref[r * SB_ROWS:(r + 1) * SB_ROWS, :] for r in range(nrc)]
        upto = _tri_ones(SB_SUB, lambda r, c: r <= c)
        below = _tri_ones(SB_SUB, lambda r, c: r < c)

        def span(start, kw, l_sum, g_sum, dq, diagonal):
            start = pl.multiple_of(start, kw)
            k_w = k_ref[pl.ds(start, kw), :]
            v_w = v_ref[pl.ds(start, kw), :]
            z_scr[:, :kw] = lax.dot_general(qb, k_w, _dims(False, True), preferred_element_type=F32)
            da_scr[:, :kw] = lax.dot_general(dob, v_w, _dims(False, True), preferred_element_type=F32)
            l_rows = [l_sum[r * SB_ROWS:(r + 1) * SB_ROWS] for r in range(nrc)]
            g_rows = [g_sum[r * SB_ROWS:(r + 1) * SB_ROWS] for r in range(nrc)]
            for s in range(kw // SB_SUB):
                cols = slice(s * SB_SUB, (s + 1) * SB_SUB)
                for r in range(nrc):
                    rows = slice(r * SB_ROWS, (r + 1) * SB_ROWS)
                    before = _sb_chunk_mask(diagonal, r, s)
                    if isinstance(before, str):
                        a_scr[rows, cols] = jnp.zeros((SB_ROWS, SB_SUB), BF16)
                        dz_scr[rows, cols] = jnp.zeros((SB_ROWS, SB_SUB), BF16)
                        continue
                    zs = z_scr[rows, cols] * SB_SCALE
                    e, log_beta, log_1m = _sb_terms(zs, before)
                    l_prefix, l_total = _sums(log_1m, upto)
                    a = jnp.exp(log_beta + (lt_rows[r] - (l_rows[r] + l_prefix)))
                    if before is not None:
                        a = jnp.where(before, a, 0.0)
                    g = a * da_scr[rows, cols]
                    g_prefix, g_total = _sums(g, below)
                    inv = 1.0 / (1.0 + e)
                    beta = jnp.where(zs >= 0, inv, e * inv)
                    dz = (g - (g + g_rows[r] + g_prefix) * beta) * SB_SCALE
                    if before is not None:
                        dz = jnp.where(before, dz, 0.0)
                    a_scr[rows, cols] = a.astype(BF16)
                    dz_scr[rows, cols] = dz.astype(BF16)
                    l_rows[r] = l_rows[r] + l_total
                    g_rows[r] = g_rows[r] + g_total
            dz_w = dz_scr[:, :kw]
            dq = dq + jnp.dot(dz_w, k_w, preferred_element_type=F32)
            dk_ref[pl.ds(start, kw), :] += lax.dot_general(dz_w, qb, _dims(True, False), preferred_element_type=F32)
            dv_ref[pl.ds(start, kw), :] += lax.dot_general(a_scr[:, :kw], dob, _dims(True, False),
                                                           preferred_element_type=F32)
            return jnp.concatenate(l_rows, axis=0), jnp.concatenate(g_rows, axis=0), dq

        taken = jnp.clip(jnp.max(n_ref[...]).astype(jnp.int32), 0, i * (bq // bk))
        zero = jnp.zeros((bq, SB_SUB), F32)
        carry = lax.fori_loop(0, taken, lambda j, c: span(i * bq - (taken - j) * bk, bk, c[0], c[1], c[2], False),
                              (zero, zero, jnp.zeros((bq, HEAD_DIM), F32)))
        _, _, dq = span(i * bq, bq, carry[0], carry[1], carry[2], True)
        dq_ref[...] = dq.astype(BF16)

    nq = n // bq
    return _call(
        body, [q, kv, kv, do, lsum, spans], name=name, grid=(N_HEADS, nq),
        in_specs=[pl.BlockSpec((bq, HEAD_DIM), lambda h, i: (i, h)),
                  pl.BlockSpec((n, HEAD_DIM), lambda h, i: (0, h)),
                  pl.BlockSpec((n, HEAD_DIM), lambda h, i: (0, N_HEADS + h)),
                  pl.BlockSpec((bq, HEAD_DIM), lambda h, i: (i, h)),
                  pl.BlockSpec((None, bq, 128), lambda h, i: (h, i, 0)),
                  pl.BlockSpec((None, None, 8, 128), lambda h, i: (h, i, 0, 0))],
        out_specs=[pl.BlockSpec((bq, HEAD_DIM), lambda h, i: (i, h)),
                   pl.BlockSpec((n, HEAD_DIM), lambda h, i: (0, h)),
                   pl.BlockSpec((n, HEAD_DIM), lambda h, i: (0, h))],
        out_shape=[jax.ShapeDtypeStruct((n, D_MODEL), BF16), jax.ShapeDtypeStruct((n, D_MODEL), F32),
                   jax.ShapeDtypeStruct((n, D_MODEL), F32)],
        scratch_shapes=[pltpu.VMEM((bq, bq), F32), pltpu.VMEM((bq, bq), F32), pltpu.VMEM((bq, bq), BF16),
                        pltpu.VMEM((bq, bq), BF16)],
        compiler_params=_params("arbitrary", "arbitrary"),
        ride=ride, first=lambda: (pl.program_id(0) == 0) & (pl.program_id(1) == 0),
        last=lambda: (pl.program_id(0) == N_HEADS - 1) & (pl.program_id(1) == nq - 1))


def _loss_head(y, target, name, tm=1024):
    n, d = y.shape

    def body(y_ref, t_ref, dy_ref, loss_ref):
        err = y_ref[...] - t_ref[...]
        dy_ref[...] = err * (1.0 / d)
        part = 0.5 * jnp.sum(jnp.mean(err * err, axis=-1, keepdims=True), axis=0, keepdims=True)
        _acc_rows(loss_ref, pl.program_id(0) == 0, [jnp.broadcast_to(part, (1, 128))])

    row = lambda i: (i, 0)
    return pl.pallas_call(
        body, name=name, grid=(n // tm,),
        in_specs=[pl.BlockSpec((tm, d), row), pl.BlockSpec((tm, d), row)],
        out_specs=[pl.BlockSpec((tm, d), row), pl.BlockSpec((8, 128), lambda i: (0, 0))],
        out_shape=[jax.ShapeDtypeStruct((n, d), F32), jax.ShapeDtypeStruct((8, 128), F32)],
        compiler_params=_params("arbitrary"),
    )(y, target)


def _local_step(x, target, gains, comm):
    g = gains
    w = comm.w
    big = {}
    row = lambda a, i: a[i:i + 1, :]

    proj, h0 = _norm_matmul(x, row(g["mix_pre"], 0), w["gdn_in"], F32, "gdn_in_proj", tn=1408)
    qn, kn, v, bg = _gdn_pre(proj, w["conv"], g["alog"], g["dtb"], "gdn_pre")
    (o_gdn, saved), got = _gdn_scan(qn, kn, v, bg, "gdn_scan", ride=comm.ride("scan"))
    comm.done("scan", got)
    x1, y_mix0, a_gdn = _out_proj((o_gdn, proj), True, x, w["gdn_out"], row(g["mix_post"], 0), g["out_gain"],
                                  "gdn_out_proj", tm=512)
    x2, y_mlp0, h_mlp0, u0, a0 = _mlp_fwd(x1, row(g["mlp_pre"], 0), w["up0"], w["down0"], row(g["mlp_post"], 0), "mlp0")
    kv, h_kv = _norm_matmul(x2, g["kv"], w["kv"], BF16, "kv_proj", tm=2048)
    q, h_q = _norm_matmul(x2, row(g["mix_pre"], 1), w["sb_q"], BF16, "sb_q_proj")
    (o_sb, lsum, spans), got = _sb_attention(q, kv, "sb_attention", ride=comm.ride("sb"))
    comm.done("sb", got)
    x3, y_mix1 = _out_proj((o_sb,), False, x2, w["sb_o"], row(g["mix_post"], 1), None, "sb_out_proj")
    x4, y_mlp1, h_mlp1, u1, a1 = _mlp_fwd(x3, row(g["mlp_pre"], 1), w["up1"], w["down1"], row(g["mlp_post"], 1), "mlp1")
    dx4, loss = _loss_head(x4, target, "loss_head")

    dx3, dy_mlp1, du1, dg_mlp1 = _mlp_bwd(dx4, y_mlp1, row(g["mlp_post"], 1), w["down1"], u1, w["up1"], x3,
                                          row(g["mlp_pre"], 1), "mlp1_bwd")
    big["down1"] = _matmul_tn(a1, dy_mlp1, 1, "d_down1")[0]
    big["up1"] = _matmul_tn(h_mlp1, du1, N_DEV, "d_up1", tk=2048)
    dy_mix1, dg_post1, do_sb = _out_proj_bwd(dx3, y_mix1, w["sb_o"], row(g["mix_post"], 1), None, "sb_out_proj_bwd")
    big["sb_o"] = _matmul_tn(o_sb, dy_mix1, 1, "d_sb_o")[0]
    (dq, dk, dv), got = _sb_attention_bwd(q, kv, do_sb, lsum, spans, "sb_attention_bwd", ride=comm.ride("sb_bwd", big))
    comm.done("sb_bwd", got)
    big["sb_q"] = _matmul_tn(h_q, dq, 1, "d_sb_q")[0]
    big["kv"] = jnp.concatenate([_matmul_tn(h_kv, dk, N_DEV // 2, "d_w_k", tk=2048),
                                 _matmul_tn(h_kv, dv, N_DEV // 2, "d_w_v", tk=2048)],
                                axis=0)
    t, dg_pre1 = _norm_matmul_bwd(dq, w["sb_q"], x2, row(g["mix_pre"], 1), dx3, "sb_q_proj_bwd")
    t, dg_kv_k = _norm_matmul_bwd(dk, w["kv"][:N_DEV // 2], x2, g["kv"], t, "k_proj_bwd", tn=256)
    dx2, dg_kv_v = _norm_matmul_bwd(dv, w["kv"][N_DEV // 2:], x2, g["kv"], t, "v_proj_bwd", tn=256)

    dx1, dy_mlp0, du0, dg_mlp0 = _mlp_bwd(dx2, y_mlp0, row(g["mlp_post"], 0), w["down0"], u0, w["up0"], x1,
                                          row(g["mlp_pre"], 0), "mlp0_bwd")
    big["down0"] = _matmul_tn(a0, dy_mlp0, 1, "d_down0")[0]
    big["up0"] = _matmul_tn(h_mlp0, du0, N_DEV, "d_up0", tk=2048)
    dy_mix0, dg_post0, do_gdn, dgate, d_out_gain = _out_proj_bwd(
        dx1, y_mix0, w["gdn_out"], row(g["mix_post"], 0), (o_gdn, proj, g["out_gain"]), "gdn_out_proj_bwd", tm=512)
    big["gdn_out"] = _matmul_tn(a_gdn, dy_mix0, 1, "d_gdn_out")[0]
    (dqn, dkn, dvv, dbg), got = _gdn_scan_bwd(qn, kn, v, bg, saved, do_gdn, "gdn_scan_bwd",
                                              ride=comm.ride("scan_bwd", big))
    comm.done("scan_bwd", got)
    dproj, d_conv, d_gates = _gdn_pre_bwd(proj, w["conv"], g["alog"], g["dtb"], dqn, dkn, dvv, dbg, dgate, "gdn_pre_bwd")
    big["gdn_in"] = _matmul_tn(h0, dproj, 1, "d_gdn_in", tb=1408, tk=512)[0]
    ride = comm.ride("in_bwd", big)
    grad_x, dg_pre0, *got = _norm_matmul_bwd(dproj, w["gdn_in"], x, row(g["mix_pre"], 0), dx1, "gdn_in_proj_bwd", tn=1408,
                                             ride=ride)
    comm.done("in_bwd", got[0] if got else [])

    small = {"pre0": dg_pre0, "pre1": dg_pre1, "post0": dg_post0, "post1": dg_post1, "mlp0": dg_mlp0, "mlp1": dg_mlp1,
             "kv_k": dg_kv_k, "kv_v": dg_kv_v, "gates": d_gates, "out_gain": d_out_gain, "conv": d_conv, "loss": loss}
    return grad_x, big, small


class _LocalOnly:
    def __init__(self, weights):
        self.w = weights

    def ride(self, stage, grads=None):
        return None

    def done(self, stage, outs):
        pass


GATHER_STAGES = {"scan": ("gdn_out", "up0", "down0", "kv", "sb_q"), "sb": ("sb_o", "up1", "down1")}
SCATTER_STAGES = {"sb_bwd": ("up1", "down1", "sb_o"), "scan_bwd": ("sb_q", "kv", "up0", "down0", "gdn_out"),
                  "in_bwd": ("gdn_in",)}


def _whole_weight(name, gathered):
    d = D_MODEL
    if name in ("up0", "up1", "kv"):
        return gathered
    if name == "gdn_in":
        whole = gathered.transpose(1, 0, 2).reshape(d, GDN_IN_COLS)
        return jnp.pad(whole, ((0, 0), (0, GDN_IN_PAD - GDN_IN_COLS)))[None]
    if name == "conv":
        return gathered.transpose(1, 0, 2).reshape(CONV_WIDTH, QKV)
    whole = gathered.reshape(gathered.shape[0] * gathered.shape[1], d)
    return whole[None] if name == "sb_q" else whole


def _owner_blocks(name, grad):
    if name in ("up0", "up1", "kv"):
        blocks = grad
    elif name == "gdn_in":
        blocks = grad[:, :GDN_IN_COLS].reshape(D_MODEL, N_DEV, GDN_IN_COLS // N_DEV).transpose(1, 0, 2)
    else:
        blocks = grad.reshape(N_DEV, grad.shape[0] // N_DEV, grad.shape[1])
    return blocks.astype(BF16)


class _Fsdp:
    def __init__(self, shards, weights):
        self.shards, self.w, self.landed = shards, weights, {}

    def ride(self, stage, grads=None):
        if stage in GATHER_STAGES:
            names = GATHER_STAGES[stage]
            return _Exchange([self.shards[nm] for nm in names], [True] * len(names))
        names = SCATTER_STAGES[stage]
        return _Exchange([_owner_blocks(nm, grads[nm]) for nm in names], [False] * len(names))

    def done(self, stage, outs):
        if stage in GATHER_STAGES:
            for nm, out in zip(GATHER_STAGES[stage], outs):
                self.w[nm] = _whole_weight(nm, out)
        else:
            self.landed.update(zip(SCATTER_STAGES[stage], outs))


def _my_place():
    return lax.axis_index("x"), lax.axis_index("y"), lax.axis_index("c")


class _Exchange:
    def __init__(self, arrays, gather):
        self.arrays, self.gather, self.n = list(arrays), list(gather), len(arrays)
        any_spec = pl.BlockSpec(memory_space=pl.ANY)
        self.in_specs = [any_spec] * self.n
        self.out_specs = [any_spec] * self.n
        self.out_shape = [jax.ShapeDtypeStruct(((N_DEV,) + a.shape) if g else a.shape, a.dtype)
                          for a, g in zip(self.arrays, self.gather)]
        self.scratch = [pltpu.SemaphoreType.DMA((self.n, N_DEV - 1)), pltpu.SemaphoreType.DMA((self.n, N_DEV - 1)),
                        pltpu.SemaphoreType.DMA((self.n,))]

    def _copies(self, ins, outs, sems):
        send_sems, recv_sems, local_sems = sems
        x, y, c = _my_place()
        me = 4 * x + 2 * y + c
        copies = []
        for a in range(self.n):
            src = ins[a] if self.gather[a] else ins[a].at[me]
            copies.append(pltpu.make_async_copy(src, outs[a].at[me], local_sems.at[a]))
        for k in range(1, N_DEV):
            px = 1 - x if k & 4 else x
            py = 1 - y if k & 2 else y
            pc = 1 - c if k & 1 else c
            peer = 4 * px + 2 * py + pc
            for a in range(self.n):
                src = ins[a] if self.gather[a] else ins[a].at[peer]
                copies.append(pltpu.make_async_remote_copy(
                    src_ref=src, dst_ref=outs[a].at[me], send_sem=send_sems.at[a, k - 1], recv_sem=recv_sems.at[a, k - 1],
                    device_id=(px, py, pc), device_id_type=MESH))
        return copies

    def start(self, ins, outs, sems):
        for cp in self._copies(ins, outs, sems):
            cp.start()

    def wait(self, ins, outs, sems):
        for cp in self._copies(ins, outs, sems):
            cp.wait()


def _call(body, operands, *, name, out_shape, in_specs, out_specs, grid=(), scratch_shapes=(), compiler_params=None,
          ride=None, first=None, last=None):
    n_in, n_out, n_scr = len(operands), len(out_shape), len(scratch_shapes)
    if ride is None:
        outs = pl.pallas_call(body, name=name, grid=grid, in_specs=in_specs, out_specs=out_specs, out_shape=out_shape,
                              scratch_shapes=scratch_shapes, compiler_params=compiler_params)(*operands)
        return list(outs), []
    r = ride.n

    def riding(*refs):
        ins, r_ins = refs[:n_in], refs[n_in:n_in + r]
        outs, r_outs = refs[n_in + r:n_in + r + n_out], refs[n_in + r + n_out:n_in + 2 * r + n_out]
        scr, sems = refs[n_in + 2 * r + n_out:n_in + 2 * r + n_out + n_scr], refs[n_in + 2 * r + n_out + n_scr:]

        @pl.when(first())
        def _():
            ride.start(r_ins, r_outs, sems)

        body(*ins, *outs, *scr)

        @pl.when(last())
        def _():
            ride.wait(r_ins, r_outs, sems)

    outs = pl.pallas_call(
        riding, name=name, grid=grid, in_specs=list(in_specs) + ride.in_specs, out_specs=list(out_specs) + ride.out_specs,
        out_shape=list(out_shape) + ride.out_shape, scratch_shapes=list(scratch_shapes) + ride.scratch,
        compiler_params=compiler_params)(*operands, *ride.arrays)
    return list(outs[:n_out]), list(outs[n_out:])


def _exchange(arrays, gather, name):
    ride = _Exchange(arrays, gather)

    def body(*refs):
        ins, outs, sems = refs[:ride.n], refs[ride.n:2 * ride.n], refs[2 * ride.n:]
        ride.start(ins, outs, sems)
        ride.wait(ins, outs, sems)

    return pl.pallas_call(body, name=name, in_specs=ride.in_specs, out_specs=ride.out_specs, out_shape=ride.out_shape,
                          scratch_shapes=ride.scratch)(*arrays)


def _adamw_math(g, w, m, v):
    m = ADAM_B1 * m + (1.0 - ADAM_B1) * g
    v = ADAM_B2 * v + (1.0 - ADAM_B2) * jnp.square(g)
    m_hat = m / (1.0 - ADAM_B1 ** ADAM_STEP)
    v_hat = v / (1.0 - ADAM_B2 ** ADAM_STEP)
    delta = -ADAM_LR * (m_hat / (jnp.sqrt(v_hat) + ADAM_EPS) + ADAM_WD * w)
    return delta, m, v


def _sum_devices(ref):
    total = ref[0].astype(F32)
    for d in range(1, N_DEV):
        total = total + ref[d].astype(F32)
    return total


def _reduce_adamw(landed, w, m, v, name, tr=256):
    r, c = w.shape
    tr = min(tr, r)
    assert r % tr == 0

    def body(l_ref, w_ref, m_ref, v_ref, g_ref, d_ref, nm_ref, nv_ref):
        g = _sum_devices(l_ref)
        g_ref[...] = g
        d_ref[...], nm_ref[...], nv_ref[...] = _adamw_math(g, w_ref[...], m_ref[...], v_ref[...])

    blk = pl.BlockSpec((tr, c), lambda i: (i, 0))
    return pl.pallas_call(
        body, name=name, grid=(r // tr,),
        in_specs=[pl.BlockSpec((N_DEV, tr, c), lambda i: (0, i, 0)), blk, blk, blk],
        out_specs=[blk] * 4, out_shape=[jax.ShapeDtypeStruct((r, c), F32)] * 4,
        compiler_params=_params("parallel"),
    )(landed, w, m, v)


def _adamw(g, w, m, v, name):
    def body(g_ref, w_ref, m_ref, v_ref, d_ref, nm_ref, nv_ref):
        d_ref[...], nm_ref[...], nv_ref[...] = _adamw_math(g_ref[...], w_ref[...], m_ref[...], v_ref[...])

    return pl.pallas_call(body, name=name, out_shape=[jax.ShapeDtypeStruct(w.shape, F32)] * 3)(g, w, m, v)


SMALL_ROWS = ("mix_pre", "mix_post", "mlp_pre", "mlp_post")


def _small_update(landed, params, name):
    layout = {
        "mix_pre": [("pre0", 0), ("pre1", 0)], "mix_post": [("post0", 0), ("post1", 0)],
        "mlp_pre": [("mlp0", 0), ("mlp1", 0)], "mlp_post": [("mlp0", 1), ("mlp1", 1)],
        "kv": [("kv_k", 0)], "alog": [("gates", 0)], "dtb": [("gates", 1)], "out_gain": [("out_gain", 0)],
    }
    landed_names = sorted(landed)
    param_names = sorted(params)
    n_l, n_p = len(landed_names), len(param_names)

    def body(*refs):
        l_refs = dict(zip(landed_names, refs[:n_l]))
        p_refs = {p: refs[n_l + 3 * i:n_l + 3 * i + 3] for i, p in enumerate(param_names)}
        outs = refs[n_l + 3 * n_p:]
        o_refs = {p: outs[4 * i:4 * i + 4] for i, p in enumerate(param_names)}
        conv_ref, loss_ref = outs[4 * n_p:]
        sums = {nm: _sum_devices(l_refs[nm]) for nm in landed_names}
        sums["kv_k"] = sums["kv_k"] + sums["kv_v"]
        for p in param_names:
            w_ref, m_ref, v_ref = p_refs[p]
            g_ref, d_ref, nm_ref, nv_ref = o_refs[p]
            for r, (src, src_row) in enumerate(layout[p]):
                g = sums[src][src_row:src_row + 1, :]
                g_ref[r:r + 1, :] = g
                d, nm, nv = _adamw_math(g, w_ref[r:r + 1, :], m_ref[r:r + 1, :], v_ref[r:r + 1, :])
                d_ref[r:r + 1, :] = d
                nm_ref[r:r + 1, :] = nm
                nv_ref[r:r + 1, :] = nv
        conv_ref[...] = sums["conv"]
        loss_ref[...] = sums["loss"]

    args = [landed[nm] for nm in landed_names]
    out_shape = []
    for p in param_names:
        args += list(params[p])
        out_shape += [jax.ShapeDtypeStruct(params[p][0].shape, F32)] * 4
    out_shape += [jax.ShapeDtypeStruct(landed["conv"].shape[1:], F32), jax.ShapeDtypeStruct(landed["loss"].shape[1:], F32)]
    outs = pl.pallas_call(body, name=name, out_shape=out_shape)(*args)
    result = {p: tuple(outs[4 * i:4 * i + 4]) for i, p in enumerate(param_names)}
    result["conv"], result["loss"] = outs[4 * n_p], outs[4 * n_p + 1]
    return result


def _lanes(vec, offset):
    return jnp.pad(vec[None, :], ((0, 0), (offset, 128 - offset - vec.shape[0])))


def kernel(x, mix_pre_gain, mix_post_gain, mlp_pre_gain, mlp_post_gain, mlp_w_up, mlp_w_down, gdn_w_in, gdn_conv_w, gdn_a_log, gdn_dt_bias, gdn_out_gain, gdn_w_out, kv_gain, w_kv, sb_w_q, sb_w_o, loss_target, m_mix_pre_gain, m_mix_post_gain, m_mlp_pre_gain, m_mlp_post_gain, m_mlp_w_up, m_mlp_w_down, m_gdn_w_in, m_gdn_conv_w, m_gdn_a_log, m_gdn_dt_bias, m_gdn_out_gain, m_gdn_w_out, m_kv_gain, m_w_kv, m_sb_w_q, m_sb_w_o, v_mix_pre_gain, v_mix_post_gain, v_mlp_pre_gain, v_mlp_post_gain, v_mlp_w_up, v_mlp_w_down, v_gdn_w_in, v_gdn_conv_w, v_gdn_a_log, v_gdn_dt_bias, v_gdn_out_gain, v_gdn_w_out, v_kv_gain, v_w_kv, v_sb_w_q, v_sb_w_o):
    me = 4 * lax.axis_index("x") + 2 * lax.axis_index("y") + lax.axis_index("c")
    bf = lambda a: a.astype(BF16)

    shards = {"up0": bf(mlp_w_up[0]), "up1": bf(mlp_w_up[1]), "down0": bf(mlp_w_down[0]), "down1": bf(mlp_w_down[1]),
              "gdn_out": bf(gdn_w_out[0]), "kv": bf(w_kv), "sb_q": bf(sb_w_q[0]), "sb_o": bf(sb_w_o[0])}
    gdn_in, conv = _exchange([bf(gdn_w_in[0]), gdn_conv_w[0]], [True, True], "gather_first_weights")
    comm = _Fsdp(shards, {"gdn_in": _whole_weight("gdn_in", gdn_in), "conv": _whole_weight("conv", conv)})
    gains = {"mix_pre": mix_pre_gain, "mix_post": mix_post_gain, "mlp_pre": mlp_pre_gain, "mlp_post": mlp_post_gain,
             "kv": kv_gain[None, :], "alog": _lanes(gdn_a_log[0], N_HEADS), "dtb": _lanes(gdn_dt_bias[0], N_HEADS),
             "out_gain": gdn_out_gain}

    grad_x, big, small = _local_step(x[0], loss_target[0], gains, comm)

    small_names = ["pre0", "pre1", "post0", "post1", "mlp0", "mlp1", "kv_k", "kv_v", "gates", "out_gain", "conv", "loss"]
    landed = _exchange([small[nm] for nm in small_names], [True] * len(small_names), "exchange_small_grads")
    small_landed = dict(zip(small_names, landed))

    results = {}
    big_params = [("mlp_w_up0", "up0", mlp_w_up[0], m_mlp_w_up[0], v_mlp_w_up[0]),
                  ("mlp_w_up1", "up1", mlp_w_up[1], m_mlp_w_up[1], v_mlp_w_up[1]),
                  ("mlp_w_down0", "down0", mlp_w_down[0], m_mlp_w_down[0], v_mlp_w_down[0]),
                  ("mlp_w_down1", "down1", mlp_w_down[1], m_mlp_w_down[1], v_mlp_w_down[1]),
                  ("gdn_w_in", "gdn_in", gdn_w_in[0], m_gdn_w_in[0], v_gdn_w_in[0]),
                  ("gdn_w_out", "gdn_out", gdn_w_out[0], m_gdn_w_out[0], v_gdn_w_out[0]),
                  ("w_kv", "kv", w_kv, m_w_kv, v_w_kv), ("sb_w_q", "sb_q", sb_w_q[0], m_sb_w_q[0], v_sb_w_q[0]),
                  ("sb_w_o", "sb_o", sb_w_o[0], m_sb_w_o[0], v_sb_w_o[0])]
    for nm, short, w_, m_, v_ in big_params:
        results[nm] = _reduce_adamw(comm.landed[short], w_, m_, v_, "adamw_" + nm)
    lanes8 = lambda a: _lanes(a[0], N_HEADS)
    small_params = {
        "mix_pre": (mix_pre_gain, m_mix_pre_gain, v_mix_pre_gain), "mix_post": (mix_post_gain, m_mix_post_gain, v_mix_post_gain),
        "mlp_pre": (mlp_pre_gain, m_mlp_pre_gain, v_mlp_pre_gain), "mlp_post": (mlp_post_gain, m_mlp_post_gain, v_mlp_post_gain),
        "kv": (kv_gain[None, :], m_kv_gain[None, :], v_kv_gain[None, :]),
        "alog": (lanes8(gdn_a_log), lanes8(m_gdn_a_log), lanes8(v_gdn_a_log)),
        "dtb": (lanes8(gdn_dt_bias), lanes8(m_gdn_dt_bias), lanes8(v_gdn_dt_bias)),
        "out_gain": (gdn_out_gain, m_gdn_out_gain, v_gdn_out_gain),
    }
    sm = _small_update(small_landed, small_params, "small_update")
    conv_cols = QKV // N_DEV
    g_conv = lax.dynamic_slice(sm["conv"], (0, me * conv_cols), (8, conv_cols))[:CONV_WIDTH]
    conv_res = (g_conv,) + tuple(_adamw(g_conv, gdn_conv_w[0], m_gdn_conv_w[0], v_gdn_conv_w[0], "adamw_conv"))

    stack2 = lambda a, b: tuple(jnp.stack([p, q]) for p, q in zip(results[a], results[b]))
    lead = lambda t: tuple(a[None] for a in t)
    heads8 = lambda t: tuple(a[:, N_HEADS:2 * N_HEADS] for a in t)
    per_weight = [
        sm["mix_pre"], sm["mix_post"], sm["mlp_pre"], sm["mlp_post"],
        stack2("mlp_w_up0", "mlp_w_up1"), stack2("mlp_w_down0", "mlp_w_down1"),
        lead(results["gdn_w_in"]), lead(conv_res), heads8(sm["alog"]), heads8(sm["dtb"]), sm["out_gain"],
        lead(results["gdn_w_out"]), tuple(a[0] for a in sm["kv"]), results["w_kv"], lead(results["sb_w_q"]), lead(results["sb_w_o"]),
    ]
    grads, deltas, new_ms, new_vs = zip(*per_weight)
    return (sm["loss"][0, 0], grad_x[None], *grads, *deltas, *new_ms, *new_vs)
```

```python
import functools
import math

import jax
import jax.numpy as jnp
from jax import lax
from jax.experimental import pallas as pl
from jax.experimental.pallas import tpu as pltpu

F32 = jnp.float32
BF16 = jnp.bfloat16

N_DEV = 8
D_MODEL = 1024
D_FF = 4096
N_HEADS = 8
HEAD_DIM = 128
CHUNK = 64
CONV_WIDTH = 4
GDN_IN_COLS = 4 * D_MODEL + 2 * N_HEADS
GDN_IN_PAD = 4 * D_MODEL + 128
EPS = 1e-6

ADAM_LR = 0.001
ADAM_B1 = 0.9
ADAM_B2 = 0.999
ADAM_EPS = 1e-08
ADAM_WD = 0.01
ADAM_STEP = 10

VMEM_LIMIT_BYTES = 56 * 1024 * 1024
MESH = pl.DeviceIdType.MESH


def _params(*semantics):
    return pltpu.CompilerParams(dimension_semantics=semantics, vmem_limit_bytes=VMEM_LIMIT_BYTES)


def _dims(ta, tb):
    return (((0,) if ta else (1,), (1,) if tb else (0,)), ((), ()))


def _dot(a, b, ta=False, tb=False):
    return lax.dot_general(a.astype(BF16), b.astype(BF16), _dims(ta, tb), preferred_element_type=F32)


def _dot_f32(a, b, ta=False, tb=False):
    return lax.dot_general(a, b, _dims(ta, tb), precision=lax.Precision.HIGHEST, preferred_element_type=F32)


def _dot_3x(a, b, ta=False, tb=False):
    return lax.dot_general(a, b, _dims(ta, tb), precision=lax.Precision.HIGH, preferred_element_type=F32)


def _make_mm(dot):
    @functools.partial(jax.custom_vjp, nondiff_argnums=(2, 3))
    def mm(a, b, ta, tb):
        return dot(a, b, ta, tb)

    def fwd(a, b, ta, tb):
        return dot(a, b, ta, tb), (a, b)

    def bwd(ta, tb, res, g):
        a, b = res
        if not ta and not tb:
            return mm(g, b, False, True), mm(a, g, True, False)
        if not ta and tb:
            return mm(g, b, False, False), mm(g, a, True, False)
        if ta and not tb:
            return mm(b, g, False, True), mm(a, g, False, False)
        raise NotImplementedError

    mm.defvjp(fwd, bwd)
    return mm


_mm = _make_mm(_dot)
_mm_f32 = _make_mm(_dot_f32)


def _rms(x, gain):
    r = lax.rsqrt(jnp.mean(x * x, axis=-1, keepdims=True) + EPS)
    return x * r * gain


def _rms_bwd(x, gain, dy):
    r = lax.rsqrt(jnp.mean(x * x, axis=-1, keepdims=True) + EPS)
    xh = x * r
    dgain = jnp.sum(dy * xh, axis=0, keepdims=True)
    dxh = dy * gain
    dx = r * (dxh - xh * jnp.mean(dxh * xh, axis=-1, keepdims=True))
    return dx, dgain


def _silu(x):
    return x / (1.0 + jnp.exp(-x))


def _acc_rows(ref, first, rows):
    @pl.when(first)
    def _():
        ref[...] = jnp.zeros_like(ref)

    for r, val in enumerate(rows):
        ref[r:r + 1, :] += val


def _norm_matmul(x, gain, w, out_dtype, name, tm=1024, tn=1024):
    n, d = x.shape
    g, _, wc = w.shape
    tm, tn = min(tm, n), min(tn, wc)
    per = wc // tn
    assert n % tm == 0 and wc % tn == 0

    def body(x_ref, gain_ref, w_ref, out_ref, h_ref):
        @pl.when(pl.program_id(1) == 0)
        def _():
            h_ref[...] = _rms(x_ref[...], gain_ref[...]).astype(BF16)

        out_ref[...] = jnp.dot(h_ref[...], w_ref[...], preferred_element_type=F32).astype(out_dtype)

    return pl.pallas_call(
        body, name=name, grid=(n // tm, g * per),
        in_specs=[pl.BlockSpec((tm, d), lambda i, j: (i, 0)),
                  pl.BlockSpec((1, d), lambda i, j: (0, 0)),
                  pl.BlockSpec((None, d, tn), lambda i, j: (j // per, 0, j % per))],
        out_specs=[pl.BlockSpec((tm, tn), lambda i, j: (i, j)),
                   pl.BlockSpec((tm, d), lambda i, j: (i, 0))],
        out_shape=[jax.ShapeDtypeStruct((n, g * wc), out_dtype), jax.ShapeDtypeStruct((n, d), BF16)],
        compiler_params=_params("parallel", "arbitrary"),
    )(x, gain, w)


def _norm_matmul_bwd(branches, x, add, name, tm=1024, ride=None):
    n, d = x.shape
    tm = min(tm, n)
    tns = [min(tn, w.shape[2]) for _, w, _, tn in branches]
    pers = [w.shape[2] // tn for (_, w, _, _), tn in zip(branches, tns)]
    ncols = [w.shape[0] * per for (_, w, _, _), per in zip(branches, pers)]
    offs = [sum(ncols[:b]) for b in range(len(branches))]
    total = sum(ncols)
    nb = len(branches)

    def body(*refs):
        dout_refs, w_refs, gain_refs = refs[:nb], refs[nb:2 * nb], refs[2 * nb:3 * nb]
        x_ref, add_ref, dx_ref, dgain_ref, acc_ref, sum_ref = refs[3 * nb:]
        i, j = pl.program_id(0), pl.program_id(1)

        @pl.when((i == 0) & (j == 0))
        def _():
            dgain_ref[...] = jnp.zeros_like(dgain_ref)

        for b in range(nb):
            first, last = offs[b], offs[b] + ncols[b] - 1

            @pl.when((j >= first) & (j <= last))
            def _(b=b, first=first):
                part = _dot(dout_refs[b][...], w_refs[b][...], tb=True)

                @pl.when(j == first)
                def _():
                    acc_ref[...] = part

                @pl.when(j > first)
                def _():
                    acc_ref[...] += part

            @pl.when(j == last)
            def _(b=b):
                xv = x_ref[...]
                xh = xv * lax.rsqrt(jnp.mean(xv * xv, axis=-1, keepdims=True) + EPS)
                dh = acc_ref[...]
                dgain_ref[b:b + 1, :] += jnp.sum(dh * xh, axis=0, keepdims=True)
                scaled = dh * gain_refs[b][...]
                if b == 0:
                    sum_ref[...] = scaled
                else:
                    sum_ref[...] += scaled

        @pl.when(j == total - 1)
        def _():
            xv = x_ref[...]
            r = lax.rsqrt(jnp.mean(xv * xv, axis=-1, keepdims=True) + EPS)
            xh = xv * r
            dxh = sum_ref[...]
            dx_ref[...] = add_ref[...] + r * (dxh - xh * jnp.mean(dxh * xh, axis=-1, keepdims=True))

    def dout_spec(b):
        return pl.BlockSpec((tm, tns[b]), lambda i, j: (i, jnp.clip(j - offs[b], 0, ncols[b] - 1)))

    def w_spec(b):
        def index(i, j):
            c = jnp.clip(j - offs[b], 0, ncols[b] - 1)
            return (c // pers[b], 0, c % pers[b])
        return pl.BlockSpec((None, d, tns[b]), index)

    row = lambda i, j: (i, 0)
    const = lambda i, j: (0, 0)
    nrow = n // tm
    (dx, dgain), got = _call(
        body, [b[0] for b in branches] + [b[1] for b in branches] + [b[2] for b in branches] + [x, add],
        name=name, grid=(nrow, total),
        in_specs=[dout_spec(b) for b in range(nb)] + [w_spec(b) for b in range(nb)]
                 + [pl.BlockSpec((1, d), const)] * nb + [pl.BlockSpec((tm, d), row), pl.BlockSpec((tm, d), row)],
        out_specs=[pl.BlockSpec((tm, d), row), pl.BlockSpec((8, d), const)],
        out_shape=[jax.ShapeDtypeStruct((n, d), F32), jax.ShapeDtypeStruct((8, d), F32)],
        scratch_shapes=[pltpu.VMEM((tm, d), F32), pltpu.VMEM((tm, d), F32)],
        compiler_params=_params("arbitrary", "arbitrary"),
        ride=ride, first=lambda: (pl.program_id(0) == 0) & (pl.program_id(1) == 0),
        last=lambda: (pl.program_id(0) == nrow - 1) & (pl.program_id(1) == total - 1))
    return (dx, dgain) if ride is None else (dx, dgain, got)


def _matmul_tn(a, b, groups, name, ta=1024, tb=1024, tk=1024):
    n, ka = a.shape
    _, kb = b.shape
    wc = kb // groups
    ta, tb, tk = min(ta, ka), min(tb, wc), min(tk, n)
    per = wc // tb
    nk = n // tk
    assert ka % ta == 0 and wc % tb == 0 and n % tk == 0

    def body(a_ref, b_ref, out_ref, acc_ref):
        k = pl.program_id(2)

        @pl.when(k == 0)
        def _():
            acc_ref[...] = jnp.zeros_like(acc_ref)

        acc_ref[...] += _dot(a_ref[...], b_ref[...], ta=True)

        @pl.when(k == nk - 1)
        def _():
            out_ref[...] = acc_ref[...].astype(BF16)

    return pl.pallas_call(
        body, name=name, grid=(ka // ta, groups * per, nk),
        in_specs=[pl.BlockSpec((tk, ta), lambda i, j, k: (k, i)),
                  pl.BlockSpec((tk, tb), lambda i, j, k: (k, j))],
        out_specs=pl.BlockSpec((None, ta, tb), lambda i, j, k: (j // per, i, j % per)),
        out_shape=jax.ShapeDtypeStruct((groups, ka, wc), BF16),
        scratch_shapes=[pltpu.VMEM((ta, tb), F32)],
        compiler_params=_params("parallel", "parallel", "arbitrary"),
    )(a, b)


def _gated_head_norm(o, gate, out_gain):
    parts = []
    for h in range(N_HEADS):
        sl = slice(h * HEAD_DIM, (h + 1) * HEAD_DIM)
        parts.append(_rms(o[:, sl], out_gain) * _silu(gate[:, sl]))
    return parts


def _out_proj(a_inputs, gated, x_in, w, gain, out_gain, name, tm=1024):
    n, d = x_in.shape
    k = w.shape[0]
    tm = min(tm, n)

    def body(*refs):
        if gated:
            o_ref, gate_ref, og_ref, x_ref, w_ref, gain_ref, xo_ref, y_ref, a_ref = refs
            parts = _gated_head_norm(o_ref[...], gate_ref[...], og_ref[...])
            for h, part in enumerate(parts):
                a_ref[:, h * HEAD_DIM:(h + 1) * HEAD_DIM] = part.astype(BF16)
            a = a_ref[...]
        else:
            a_in_ref, x_ref, w_ref, gain_ref, xo_ref, y_ref = refs
            a = a_in_ref[...]
        y = jnp.dot(a, w_ref[...], preferred_element_type=F32)
        y_ref[...] = y
        xo_ref[...] = x_ref[...] + _rms(y, gain_ref[...])

    row = lambda i: (i, 0)
    const = lambda i: (0, 0)
    if gated:
        a_specs = [pl.BlockSpec((tm, k), row), pl.BlockSpec((tm, D_MODEL), lambda i: (i, 3)),
                   pl.BlockSpec((1, HEAD_DIM), const)]
        a_args = list(a_inputs) + [out_gain]
    else:
        a_specs = [pl.BlockSpec((tm, k), row)]
        a_args = list(a_inputs)
    out_specs = [pl.BlockSpec((tm, d), row), pl.BlockSpec((tm, d), row)]
    out_shape = [jax.ShapeDtypeStruct((n, d), F32), jax.ShapeDtypeStruct((n, d), F32)]
    if gated:
        out_specs.append(pl.BlockSpec((tm, k), row))
        out_shape.append(jax.ShapeDtypeStruct((n, k), BF16))
    return pl.pallas_call(
        body, name=name, grid=(n // tm,),
        in_specs=a_specs + [pl.BlockSpec((tm, d), row), pl.BlockSpec((k, d), const), pl.BlockSpec((1, d), const)],
        out_specs=out_specs, out_shape=out_shape,
        compiler_params=_params("parallel"),
    )(*a_args, x_in, w, gain)


def _out_proj_bwd(dxo, y, w, gain, gated_inputs, name, tm=1024):
    n, d = dxo.shape
    k = w.shape[0]
    tm = min(tm, n)
    gated = gated_inputs is not None

    def body(*refs):
        if gated:
            (dxo_ref, y_ref, w_ref, gain_ref, o_ref, gate_ref, og_ref,
             dy_ref, dgain_ref, do_ref, dgate_ref, dog_ref) = refs
        else:
            dxo_ref, y_ref, w_ref, gain_ref, dy_ref, dgain_ref, da_ref = refs
        first = pl.program_id(0) == 0
        dy, dgain = _rms_bwd(y_ref[...], gain_ref[...], dxo_ref[...])
        dy_ref[...] = dy.astype(BF16)
        _acc_rows(dgain_ref, first, [dgain])
        da = _dot(dy_ref[...], w_ref[...], tb=True)
        if not gated:
            da_ref[...] = da.astype(BF16)
            return
        og = og_ref[...]
        dog = jnp.zeros_like(og)
        for h in range(N_HEADS):
            sl = slice(h * HEAD_DIM, (h + 1) * HEAD_DIM)
            fn = lambda o_h, g_h, gn: _rms(o_h, gn) * _silu(g_h)
            _, vjp = jax.vjp(fn, o_ref[:, sl], gate_ref[:, sl], og)
            do_h, dgate_h, dog_h = vjp(da[:, sl])
            do_ref[:, sl] = do_h
            dgate_ref[:, sl] = dgate_h
            dog = dog + dog_h
        _acc_rows(dog_ref, first, [dog])

    row = lambda i: (i, 0)
    const = lambda i: (0, 0)
    in_specs = [pl.BlockSpec((tm, d), row), pl.BlockSpec((tm, d), row), pl.BlockSpec((k, d), const),
                pl.BlockSpec((1, d), const)]
    args = [dxo, y, w, gain]
    out_specs = [pl.BlockSpec((tm, d), row), pl.BlockSpec((8, d), const)]
    out_shape = [jax.ShapeDtypeStruct((n, d), BF16), jax.ShapeDtypeStruct((8, d), F32)]
    if gated:
        in_specs += [pl.BlockSpec((tm, k), row), pl.BlockSpec((tm, D_MODEL), lambda i: (i, 3)),
                     pl.BlockSpec((1, HEAD_DIM), const)]
        args += list(gated_inputs)
        out_specs += [pl.BlockSpec((tm, k), row), pl.BlockSpec((tm, k), row), pl.BlockSpec((8, HEAD_DIM), const)]
        out_shape += [jax.ShapeDtypeStruct((n, k), F32), jax.ShapeDtypeStruct((n, k), F32),
                      jax.ShapeDtypeStruct((8, HEAD_DIM), F32)]
    else:
        out_specs.append(pl.BlockSpec((tm, k), row))
        out_shape.append(jax.ShapeDtypeStruct((n, k), BF16))
    return pl.pallas_call(
        body, name=name, grid=(n // tm,), in_specs=in_specs, out_specs=out_specs, out_shape=out_shape,
        compiler_params=_params("arbitrary"),
    )(*args)


def _mlp_fwd(x_in, g_pre, w_up, w_down, g_post, name, tm=1024, target=None):
    n, d = x_in.shape
    g, _, wc = w_up.shape
    tm = min(tm, n)
    with_loss = target is not None

    def body(*refs):
        if with_loss:
            x_ref, gpre_ref, wup_ref, wdown_ref, gpost_ref, t_ref, xo_ref, y_ref, h_ref, u_ref, a_ref, loss_ref, acc_ref = refs
        else:
            x_ref, gpre_ref, wup_ref, wdown_ref, gpost_ref, xo_ref, y_ref, h_ref, u_ref, a_ref, acc_ref = refs
        i, j = pl.program_id(0), pl.program_id(1)

        @pl.when(j == 0)
        def _():
            h_ref[...] = _rms(x_ref[...], gpre_ref[...]).astype(BF16)
            acc_ref[...] = jnp.zeros_like(acc_ref)

        u = jnp.dot(h_ref[...], wup_ref[...], preferred_element_type=F32).astype(BF16)
        u_ref[...] = u
        a = jnp.square(jnp.maximum(u, 0))
        a_ref[...] = a
        acc_ref[...] += jnp.dot(a, wdown_ref[...], preferred_element_type=F32)

        @pl.when(j == g - 1)
        def _():
            y = acc_ref[...]
            y_ref[...] = y
            xo = x_ref[...] + _rms(y, gpost_ref[...])
            if with_loss:
                err = xo - t_ref[...]
                xo_ref[...] = err * (1.0 / d)
                part = 0.5 * jnp.sum(jnp.mean(err * err, axis=-1, keepdims=True), axis=0, keepdims=True)
                _acc_rows(loss_ref, i == 0, [jnp.broadcast_to(part, (1, 128))])
            else:
                xo_ref[...] = xo

    row = lambda i, j: (i, 0)
    const = lambda i, j: (0, 0)
    in_specs = [pl.BlockSpec((tm, d), row), pl.BlockSpec((1, d), const),
                pl.BlockSpec((None, d, wc), lambda i, j: (j, 0, 0)),
                pl.BlockSpec((wc, d), lambda i, j: (j, 0)), pl.BlockSpec((1, d), const)]
    out_specs = [pl.BlockSpec((tm, d), row), pl.BlockSpec((tm, d), row), pl.BlockSpec((tm, d), row),
                 pl.BlockSpec((tm, wc), lambda i, j: (i, j)), pl.BlockSpec((tm, wc), lambda i, j: (i, j))]
    out_shape = [jax.ShapeDtypeStruct((n, d), F32), jax.ShapeDtypeStruct((n, d), F32),
                 jax.ShapeDtypeStruct((n, d), BF16), jax.ShapeDtypeStruct((n, g * wc), BF16),
                 jax.ShapeDtypeStruct((n, g * wc), BF16)]
    args = [x_in, g_pre, w_up, w_down, g_post]
    if with_loss:
        in_specs.append(pl.BlockSpec((tm, d), row))
        out_specs.append(pl.BlockSpec((8, 128), const))
        out_shape.append(jax.ShapeDtypeStruct((8, 128), F32))
        args.append(target)
    return pl.pallas_call(
        body, name=name, grid=(n // tm, g), in_specs=in_specs, out_specs=out_specs, out_shape=out_shape,
        scratch_shapes=[pltpu.VMEM((tm, d), F32)],
        compiler_params=_params("arbitrary", "arbitrary"),
    )(*args)


def _mlp_bwd(dxo, y, g_post, w_down, u, w_up, x_in, g_pre, name, tm=1024):
    n, d = dxo.shape
    g, _, wc = w_up.shape

    def body(dxo_ref, y_ref, gpost_ref, wdown_ref, u_ref, wup_ref, x_ref, gpre_ref,
             dx_ref, dy_ref, du_ref, dgain_ref, acc_ref, dgpost_ref):
        i, j = pl.program_id(0), pl.program_id(1)

        @pl.when(j == 0)
        def _():
            dy, dgpost = _rms_bwd(y_ref[...], gpost_ref[...], dxo_ref[...])
            dy_ref[...] = dy.astype(BF16)
            dgpost_ref[...] = dgpost
            acc_ref[...] = jnp.zeros_like(acc_ref)

        da = _dot(dy_ref[...], wdown_ref[...], tb=True)
        du = (da * (2.0 * jnp.maximum(u_ref[...], 0).astype(F32))).astype(BF16)
        du_ref[...] = du
        acc_ref[...] += _dot(du, wup_ref[...], tb=True)

        @pl.when(j == g - 1)
        def _():
            dx, dgpre = _rms_bwd(x_ref[...], gpre_ref[...], acc_ref[...])
            dx_ref[...] = dxo_ref[...] + dx
            _acc_rows(dgain_ref, i == 0, [dgpre, dgpost_ref[...]])

    row = lambda i, j: (i, 0)
    const = lambda i, j: (0, 0)
    return pl.pallas_call(
        body, name=name, grid=(n // tm, g),
        in_specs=[pl.BlockSpec((tm, d), row), pl.BlockSpec((tm, d), row), pl.BlockSpec((1, d), const),
                  pl.BlockSpec((wc, d), lambda i, j: (j, 0)), pl.BlockSpec((tm, wc), lambda i, j: (i, j)),
                  pl.BlockSpec((None, d, wc), lambda i, j: (j, 0, 0)), pl.BlockSpec((tm, d), row),
                  pl.BlockSpec((1, d), const)],
        out_specs=[pl.BlockSpec((tm, d), row), pl.BlockSpec((tm, d), row),
                   pl.BlockSpec((tm, wc), lambda i, j: (i, j)), pl.BlockSpec((8, d), const)],
        out_shape=[jax.ShapeDtypeStruct((n, d), F32), jax.ShapeDtypeStruct((n, d), BF16),
                   jax.ShapeDtypeStruct((n, g * wc), BF16), jax.ShapeDtypeStruct((8, d), F32)],
        scratch_shapes=[pltpu.VMEM((tm, d), F32), pltpu.VMEM((1, d), F32)],
        compiler_params=_params("arbitrary", "arbitrary"),
    )(dxo, y, g_post, w_down, u, w_up, x_in, g_pre)


QKV = 3 * D_MODEL


def _shifted(x, prev8, s):
    if s == 0:
        return x
    tm = x.shape[0]
    rolled = pltpu.roll(x, s, 0)
    head = pltpu.roll(prev8, s, 0)
    head = jnp.concatenate([head, jnp.zeros((tm - 8, x.shape[1]), x.dtype)], axis=0)
    rows = lax.broadcasted_iota(jnp.int32, x.shape, 0)
    return jnp.where(rows < s, head, rolled)


def _conv(x, prev8, conv_w):
    out = x * conv_w[3:4, :]
    for s in range(1, CONV_WIDTH):
        out = out + _shifted(x, prev8, s) * conv_w[3 - s:4 - s, :]
    return out


def _l2norm(x):
    return x * lax.rsqrt(jnp.sum(x * x, axis=-1, keepdims=True) + EPS)


def _gdn_act_head(cq, ck, cv):
    return _l2norm(_silu(cq)) * (HEAD_DIM ** -0.5), _l2norm(_silu(ck)), _silu(cv)


def _gdn_gates(ba, alog, dtb):
    lane = lax.broadcasted_iota(jnp.int32, ba.shape, 1)
    beta = 1.0 / (1.0 + jnp.exp(-ba))
    t = ba + dtb
    softplus = jnp.maximum(t, 0.0) + jnp.log(1.0 + jnp.exp(-jnp.abs(t)))
    g = -jnp.exp(alog) * softplus
    return jnp.where(lane < N_HEADS, beta, jnp.where(lane < 2 * N_HEADS, g, 0.0))


def _qkv_cols(part, h):
    start = part * D_MODEL + h * HEAD_DIM
    return slice(start, start + HEAD_DIM)


def _prev_rows_spec(tm, cols, colblock, order):
    per = tm // 8
    return pl.BlockSpec((8, cols), lambda i: (jnp.maximum(order(i) * per - 1, 0), colblock))


def _gdn_pre(proj, conv_w, alog, dtb, name, tm=256):
    n = proj.shape[0]

    def body(x_ref, prev_ref, ba_ref, cw_ref, alog_ref, dtb_ref, q_ref, k_ref, v_ref, bg_ref):
        first = pl.program_id(0) == 0
        for h in range(N_HEADS):
            convs = []
            for part in range(3):
                sl = _qkv_cols(part, h)
                prev8 = jnp.where(first, 0.0, prev_ref[:, sl])
                convs.append(_conv(x_ref[:, sl], prev8, cw_ref[:, sl]))
            out = slice(h * HEAD_DIM, (h + 1) * HEAD_DIM)
            q_ref[:, out], k_ref[:, out], v_ref[:, out] = _gdn_act_head(*convs)
        bg_ref[...] = _gdn_gates(ba_ref[...], alog_ref[...], dtb_ref[...])

    row = lambda i: (i, 0)
    const = lambda i: (0, 0)
    ident = lambda i: i
    return pl.pallas_call(
        body, name=name, grid=(n // tm,),
        in_specs=[pl.BlockSpec((tm, QKV), row), _prev_rows_spec(tm, QKV, 0, ident),
                  pl.BlockSpec((tm, 128), lambda i: (i, 4 * D_MODEL // 128)),
                  pl.BlockSpec((CONV_WIDTH, QKV), const), pl.BlockSpec((1, 128), const), pl.BlockSpec((1, 128), const)],
        out_specs=[pl.BlockSpec((tm, D_MODEL), row)] * 3 + [pl.BlockSpec((tm, 128), row)],
        out_shape=[jax.ShapeDtypeStruct((n, D_MODEL), F32)] * 3 + [jax.ShapeDtypeStruct((n, 128), F32)],
        compiler_params=_params("parallel"),
    )(proj, proj, proj, conv_w, alog, dtb)


def _gdn_pre_bwd(proj, conv_w, alog, dtb, dq, dk, dv, dbg, dgate, name, tm=256):
    n = proj.shape[0]
    nt = n // tm

    def body(x_ref, prev_ref, ba_ref, cw_ref, alog_ref, dtb_ref, dq_ref, dk_ref, dv_ref, dbg_ref, dgate_ref,
             dproj_ref, dcw_ref, dgates_ref, carry_ref):
        step = pl.program_id(0)
        tile = nt - 1 - step
        @pl.when(step == 0)
        def _():
            carry_ref[...] = jnp.zeros_like(carry_ref)
            dcw_ref[...] = jnp.zeros_like(dcw_ref)

        rows = lax.broadcasted_iota(jnp.int32, (tm, HEAD_DIM), 0)
        for h in range(N_HEADS):
            cols = [_qkv_cols(part, h) for part in range(3)]
            prevs = [jnp.where(tile == 0, 0.0, prev_ref[:, sl]) for sl in cols]
            convs = [_conv(x_ref[:, sl], prev8, cw_ref[:, sl]) for sl, prev8 in zip(cols, prevs)]
            _, vjp = jax.vjp(_gdn_act_head, *convs)
            out = slice(h * HEAD_DIM, (h + 1) * HEAD_DIM)
            dcs = vjp((dq_ref[:, out], dk_ref[:, out], dv_ref[:, out]))
            for sl, prev8, dc in zip(cols, prevs, dcs):
                x = x_ref[:, sl]
                cw = cw_ref[:, sl]
                dx = dc * cw[3:4, :]
                wrapped = jnp.zeros((8, HEAD_DIM), F32)
                dcw_ref[3:4, sl] += jnp.sum(dc * x, axis=0, keepdims=True)
                for s in range(1, CONV_WIDTH):
                    r = pltpu.roll(dc, tm - s, 0) * cw[3 - s:4 - s, :]
                    dx = dx + jnp.where(rows < tm - s, r, 0.0)
                    wrapped = wrapped + jnp.where(rows[tm - 8:, :] >= tm - s, r[tm - 8:, :], 0.0)
                    dcw_ref[3 - s:4 - s, sl] += jnp.sum(dc * _shifted(x, prev8, s), axis=0, keepdims=True)
                dproj_ref[:, sl] = dx
                dproj_ref[tm - 8:, sl] += carry_ref[:, sl]
                carry_ref[:, sl] = wrapped

        _, vjp = jax.vjp(_gdn_gates, ba_ref[...], alog_ref[...], dtb_ref[...])
        dba, dalog, ddtb = vjp(dbg_ref[...])
        dproj_ref[:, QKV:4 * D_MODEL] = dgate_ref[...]
        dproj_ref[:, 4 * D_MODEL:] = dba
        _acc_rows(dgates_ref, step == 0, [dalog, ddtb])

    rev = lambda i: nt - 1 - i
    row = lambda i: (rev(i), 0)
    const = lambda i: (0, 0)
    return pl.pallas_call(
        body, name=name, grid=(nt,),
        in_specs=[pl.BlockSpec((tm, QKV), row), _prev_rows_spec(tm, QKV, 0, rev),
                  pl.BlockSpec((tm, 128), lambda i: (rev(i), 4 * D_MODEL // 128)),
                  pl.BlockSpec((CONV_WIDTH, QKV), const), pl.BlockSpec((1, 128), const), pl.BlockSpec((1, 128), const),
                  pl.BlockSpec((tm, D_MODEL), row), pl.BlockSpec((tm, D_MODEL), row), pl.BlockSpec((tm, D_MODEL), row),
                  pl.BlockSpec((tm, 128), row), pl.BlockSpec((tm, D_MODEL), row)],
        out_specs=[pl.BlockSpec((tm, GDN_IN_PAD), row), pl.BlockSpec((8, QKV), const), pl.BlockSpec((8, 128), const)],
        out_shape=[jax.ShapeDtypeStruct((n, GDN_IN_PAD), F32), jax.ShapeDtypeStruct((8, QKV), F32),
                   jax.ShapeDtypeStruct((8, 128), F32)],
        scratch_shapes=[pltpu.VMEM((8, QKV), F32)],
        compiler_params=_params("arbitrary"),
    )(proj, proj, proj, conv_w, alog, dtb, dq, dk, dv, dbg, dgate)


@jax.custom_vjp
def _unit_lower_inverses(lowers):
    n = lowers[0].shape[0]
    eye = (lax.broadcasted_iota(jnp.int32, (n, n), 0) == lax.broadcasted_iota(jnp.int32, (n, n), 1)).astype(F32)
    ys = [-low for low in lowers]
    ps = [eye + y for y in ys]
    for _ in range(int(math.log2(CHUNK)) - 1):
        ys = [_dot_3x(y, y) for y in ys]
        ps = [p + _dot_3x(p, y) for p, y in zip(ps, ys)]
    return tuple(ps)


def _unit_lower_inverses_fwd(lowers):
    ts = _unit_lower_inverses(lowers)
    return ts, ts


def _unit_lower_inverses_bwd(ts, dts):
    left = [_dot_3x(t, dt, ta=True) for t, dt in zip(ts, dts)]
    return (tuple(-_dot_3x(l, t, tb=True) for l, t in zip(left, ts)),)


_unit_lower_inverses.defvjp(_unit_lower_inverses_fwd, _unit_lower_inverses_bwd)


@jax.custom_vjp
def _known_inverses(lowers, inverses):
    return inverses


def _known_inverses_bwd(ts, dts):
    (d_lowers,) = _unit_lower_inverses_bwd(ts, dts)
    return d_lowers, tuple(jnp.zeros_like(t) for t in ts)


_known_inverses.defvjp(lambda lowers, inverses: (inverses, inverses), _known_inverses_bwd)


GDN_STEP_CHUNKS = 2


def _gdn_chunks(qs, ks, vs, bgs, states, inverses=None, keep_inverses=False):
    c = CHUNK
    heads = range(N_HEADS)
    items = [(j, h) for j in range(len(bgs)) for h in heads]
    row = lax.broadcasted_iota(jnp.int32, (c, c), 0)
    col = lax.broadcasted_iota(jnp.int32, (c, c), 1)
    incl, strict, eye = row >= col, row > col, row == col
    lane = lax.broadcasted_iota(jnp.int32, (c, 128), 1)
    rowc = lax.broadcasted_iota(jnp.int32, (c, 1), 0)
    gc_all = [_mm_f32(incl.astype(F32), bg, False, False) for bg in bgs]
    q, k, v = ([xs[j][h] for j, h in items] for xs in (qs, ks, vs))
    n = range(len(items))
    beta = [jnp.sum(jnp.where(lane == h, bgs[j], 0.0), axis=1, keepdims=True) for j, h in items]
    gc = [jnp.sum(jnp.where(lane == N_HEADS + h, gc_all[j], 0.0), axis=1, keepdims=True) for j, h in items]
    gc_row = [jnp.sum(jnp.where(eye, gc[i], 0.0), axis=0, keepdims=True) for i in n]
    gc_last = [jnp.sum(jnp.where(rowc == c - 1, gc[i], 0.0), axis=0, keepdims=True) for i in n]
    decay = [jnp.where(incl, jnp.exp(jnp.where(incl, gc[i] - gc_row[i], 0.0)), 0.0) for i in n]
    kb = [k[i] * beta[i] for i in n]
    lower = tuple(jnp.where(strict, _mm(kb[i], k[i], False, True) * decay[i], 0.0) for i in n)
    attn = [_mm(q[i], k[i], False, True) * decay[i] for i in n]
    t_mat = _unit_lower_inverses(lower) if inverses is None else _known_inverses(lower, inverses)
    egc = [jnp.exp(gc[i]) for i in n]
    w = [_mm(t_mat[i], kb[i] * egc[i], False, False) for i in n]
    u = [_mm(t_mat[i], v[i] * beta[i], False, False) for i in n]
    qg = [q[i] * egc[i] for i in n]
    kg = [k[i] * jnp.exp(gc_last[i] - gc[i]) for i in n]
    outs, cur = [], list(states)
    for j in range(len(bgs)):
        at = lambda h: j * N_HEADS + h
        v_new = [u[at(h)] - _mm(w[at(h)], cur[h], False, False) for h in heads]
        from_state = [_mm(qg[at(h)], cur[h], False, False) for h in heads]
        outs.append(tuple(from_state[h] + _mm(attn[at(h)], v_new[h], False, False) for h in heads))
        cur = [cur[h] * jnp.exp(gc_last[at(h)]) + _mm(kg[at(h)], v_new[h], True, False) for h in heads]
    return (tuple(outs), tuple(cur), t_mat) if keep_inverses else (tuple(outs), tuple(cur))


def _chunk_head_slices(ref, nch):
    return tuple(tuple(ref[j * CHUNK:(j + 1) * CHUNK, h * HEAD_DIM:(h + 1) * HEAD_DIM] for h in range(N_HEADS))
                 for j in range(nch))


def _store_chunk_heads(ref, values):
    for j, chunk in enumerate(values):
        for h, val in enumerate(chunk):
            ref[j * CHUNK:(j + 1) * CHUNK, h * HEAD_DIM:(h + 1) * HEAD_DIM] = val


def _gdn_scan(qn, kn, v, bg, name, ride=None):
    n = qn.shape[0]
    nch = GDN_STEP_CHUNKS
    rows = nch * CHUNK
    nc = n // rows

    def body(q_ref, k_ref, v_ref, bg_ref, o_ref, saved_ref, inv_ref, state_ref):
        @pl.when(pl.program_id(0) == 0)
        def _():
            state_ref[...] = jnp.zeros_like(state_ref)

        states = tuple(state_ref[h] for h in range(N_HEADS))
        for h in range(N_HEADS):
            saved_ref[h] = states[h]
        bgs = tuple(bg_ref[j * CHUNK:(j + 1) * CHUNK, :] for j in range(nch))
        outs, new_states, inverses = _gdn_chunks(_chunk_head_slices(q_ref, nch), _chunk_head_slices(k_ref, nch),
                                                 _chunk_head_slices(v_ref, nch), bgs, states, keep_inverses=True)
        _store_chunk_heads(o_ref, outs)
        for h in range(N_HEADS):
            state_ref[h] = new_states[h]
        for item, inverse in enumerate(inverses):
            inv_ref[item] = inverse

    row = lambda i: (i, 0)
    return _call(
        body, [qn, kn, v, bg], name=name, grid=(nc,),
        in_specs=[pl.BlockSpec((rows, D_MODEL), row)] * 3 + [pl.BlockSpec((rows, 128), row)],
        out_specs=[pl.BlockSpec((rows, D_MODEL), row),
                   pl.BlockSpec((None, N_HEADS, HEAD_DIM, HEAD_DIM), lambda i: (i, 0, 0, 0)),
                   pl.BlockSpec((None, nch * N_HEADS, CHUNK, CHUNK), lambda i: (i, 0, 0, 0))],
        out_shape=[jax.ShapeDtypeStruct((n, D_MODEL), F32),
                   jax.ShapeDtypeStruct((nc, N_HEADS, HEAD_DIM, HEAD_DIM), F32),
                   jax.ShapeDtypeStruct((nc, nch * N_HEADS, CHUNK, CHUNK), F32)],
        scratch_shapes=[pltpu.VMEM((N_HEADS, HEAD_DIM, HEAD_DIM), F32)],
        compiler_params=_params("arbitrary"),
        ride=ride, first=lambda: pl.program_id(0) == 0, last=lambda: pl.program_id(0) == nc - 1)


def _gdn_scan_bwd(qn, kn, v, bg, saved, inverses, do, name, ride=None):
    n = qn.shape[0]
    nch = GDN_STEP_CHUNKS
    rows = nch * CHUNK
    nc = n // rows

    def body(q_ref, k_ref, v_ref, bg_ref, saved_ref, inv_ref, do_ref, dq_ref, dk_ref, dv_ref, dbg_ref, dstate_ref):
        @pl.when(pl.program_id(0) == 0)
        def _():
            dstate_ref[...] = jnp.zeros_like(dstate_ref)

        states = tuple(saved_ref[h] for h in range(N_HEADS))
        bgs = tuple(bg_ref[j * CHUNK:(j + 1) * CHUNK, :] for j in range(nch))
        known = tuple(inv_ref[item] for item in range(nch * N_HEADS))
        _, vjp = jax.vjp(functools.partial(_gdn_chunks, inverses=known), _chunk_head_slices(q_ref, nch),
                         _chunk_head_slices(k_ref, nch), _chunk_head_slices(v_ref, nch), bgs, states)
        dstates = tuple(dstate_ref[h] for h in range(N_HEADS))
        dqs, dks, dvs, dbgs, dprev = vjp((_chunk_head_slices(do_ref, nch), dstates))
        _store_chunk_heads(dq_ref, dqs)
        _store_chunk_heads(dk_ref, dks)
        _store_chunk_heads(dv_ref, dvs)
        for h in range(N_HEADS):
            dstate_ref[h] = dprev[h]
        for j in range(nch):
            dbg_ref[j * CHUNK:(j + 1) * CHUNK, :] = dbgs[j]

    row = lambda i: (nc - 1 - i, 0)
    return _call(
        body, [qn, kn, v, bg, saved, inverses, do], name=name, grid=(nc,),
        in_specs=[pl.BlockSpec((rows, D_MODEL), row)] * 3 + [pl.BlockSpec((rows, 128), row),
                  pl.BlockSpec((None, N_HEADS, HEAD_DIM, HEAD_DIM), lambda i: (nc - 1 - i, 0, 0, 0)),
                  pl.BlockSpec((None, nch * N_HEADS, CHUNK, CHUNK), lambda i: (nc - 1 - i, 0, 0, 0)),
                  pl.BlockSpec((rows, D_MODEL), row)],
        out_specs=[pl.BlockSpec((rows, D_MODEL), row)] * 3 + [pl.BlockSpec((rows, 128), row)],
        out_shape=[jax.ShapeDtypeStruct((n, D_MODEL), F32)] * 3 + [jax.ShapeDtypeStruct((n, 128), F32)],
        scratch_shapes=[pltpu.VMEM((N_HEADS, HEAD_DIM, HEAD_DIM), F32)],
        compiler_params=_params("arbitrary"),
        ride=ride, first=lambda: pl.program_id(0) == 0, last=lambda: pl.program_id(0) == nc - 1)


SB_BQ = 512
SB_BK = 256
SB_SUB = 128
SB_ROWS = 128
SB_SCALE = HEAD_DIM ** -0.5
SB_DEAD = -105.0


def _sb_terms(z, before):
    e = jnp.exp(-jnp.abs(z))
    log_beta = jnp.minimum(z, 0.0) - jnp.log(1.0 + e)
    log_1m = log_beta - z
    if before is not None:
        log_1m = jnp.where(before, log_1m, 0.0)
    return e, log_beta, log_1m


def _tri_ones(n, cmp):
    r = lax.broadcasted_iota(jnp.int32, (n, 2 * n), 0)
    c = lax.broadcasted_iota(jnp.int32, (n, 2 * n), 1)
    return jnp.where((c >= n) | cmp(r, c), 1.0, 0.0).astype(BF16)


def _sums(x, tri_ones):
    both = jnp.dot(x.astype(BF16), tri_ones, preferred_element_type=F32)
    n = x.shape[1]
    return both[:, :n], both[:, n:]


def _sb_chunk_mask(diagonal, r, s):
    if not diagonal or s * SB_SUB + SB_SUB - 1 < r * SB_ROWS:
        return None
    if s * SB_SUB >= r * SB_ROWS + SB_ROWS - 1:
        return "empty"
    rows = r * SB_ROWS + lax.broadcasted_iota(jnp.int32, (SB_ROWS, SB_SUB), 0)
    cols = s * SB_SUB + lax.broadcasted_iota(jnp.int32, (SB_ROWS, SB_SUB), 1)
    return cols < rows


def _sb_attention(q, kv, name, bq=SB_BQ, ride=None):
    n = q.shape[0]
    bq = min(bq, n)
    bk = min(SB_BK, bq)
    nrc = bq // SB_ROWS

    def body(q_ref, k_ref, v_ref, o_ref, l_ref, n_ref, z_scr, a_scr):
        i = pl.program_id(1)
        qb = q_ref[...]
        after = _tri_ones(SB_SUB, lambda r, c: r > c)

        def span(start, kw, c_sum, acc, diagonal):
            start = pl.multiple_of(start, kw)
            k_w = k_ref[pl.ds(start, kw), :]
            v_w = v_ref[pl.ds(start, kw), :]
            z_scr[:, :kw] = lax.dot_general(qb, k_w, _dims(False, True), preferred_element_type=F32)
            c_rows = [c_sum[r * SB_ROWS:(r + 1) * SB_ROWS] for r in range(nrc)]
            for s in reversed(range(kw // SB_SUB)):
                cols = slice(s * SB_SUB, (s + 1) * SB_SUB)
                for r in range(nrc):
                    rows = slice(r * SB_ROWS, (r + 1) * SB_ROWS)
                    before = _sb_chunk_mask(diagonal, r, s)
                    if isinstance(before, str):
                        a_scr[rows, cols] = jnp.zeros((SB_ROWS, SB_SUB), BF16)
                        continue
                    _, log_beta, log_1m = _sb_terms(z_scr[rows, cols] * SB_SCALE, before)
                    tail, total = _sums(log_1m, after)
                    a = jnp.exp(log_beta + c_rows[r] + tail)
                    if before is not None:
                        a = jnp.where(before, a, 0.0)
                    a_scr[rows, cols] = a.astype(BF16)
                    c_rows[r] = c_rows[r] + total
            acc = acc + jnp.dot(a_scr[:, :kw], v_w, preferred_element_type=F32)
            return jnp.concatenate(c_rows, axis=0), acc

        c_sum, acc = span(i * bq, bq, jnp.zeros((bq, SB_SUB), F32), jnp.zeros((bq, HEAD_DIM), F32), True)

        def more(state):
            done, c, _ = state
            return (done < i * (bq // bk)) & (jnp.max(c) > SB_DEAD)

        def step(state):
            done, c, a = state
            c, a = span(i * bq - (done + 1) * bk, bk, c, a, False)
            return done + 1, c, a

        done, c_sum, acc = lax.while_loop(more, step, (jnp.int32(0), c_sum, acc))
        o_ref[...] = acc.astype(BF16)
        l_ref[...] = c_sum
        n_ref[...] = jnp.full((8, 128), done.astype(F32), F32)

    nq = n // bq
    return _call(
        body, [q, kv, kv], name=name, grid=(N_HEADS, nq),
        in_specs=[pl.BlockSpec((bq, HEAD_DIM), lambda h, i: (i, h)),
                  pl.BlockSpec((n, HEAD_DIM), lambda h, i: (0, h)),
                  pl.BlockSpec((n, HEAD_DIM), lambda h, i: (0, N_HEADS + h))],
        out_specs=[pl.BlockSpec((bq, HEAD_DIM), lambda h, i: (i, h)),
                   pl.BlockSpec((None, bq, 128), lambda h, i: (h, i, 0)),
                   pl.BlockSpec((None, None, 8, 128), lambda h, i: (h, i, 0, 0))],
        out_shape=[jax.ShapeDtypeStruct((n, D_MODEL), BF16), jax.ShapeDtypeStruct((N_HEADS, n, 128), F32),
                   jax.ShapeDtypeStruct((N_HEADS, nq, 8, 128), F32)],
        scratch_shapes=[pltpu.VMEM((bq, bq), F32), pltpu.VMEM((bq, bq), BF16)],
        compiler_params=_params("arbitrary", "arbitrary"),
        ride=ride, first=lambda: (pl.program_id(0) == 0) & (pl.program_id(1) == 0),
        last=lambda: (pl.program_id(0) == N_HEADS - 1) & (pl.program_id(1) == nq - 1))


def _sb_attention_bwd(q, kv, do, lsum, spans, name, bq=SB_BQ, ride=None):
    n = q.shape[0]
    bq = min(bq, n)
    bk = min(SB_BK, bq)
    nrc = bq // SB_ROWS

    def body(q_ref, k_ref, v_ref, do_ref, l_ref, n_ref, dq_ref, dk_ref, dv_ref, z_scr, da_scr, a_scr, dz_scr):
        i = pl.program_id(1)

        @pl.when(i == 0)
        def _():
            dk_ref[...] = jnp.zeros_like(dk_ref)
            dv_ref[...] = jnp.zeros_like(dv_ref)

        qb = q_ref[...]
        dob = do_ref[...]
        lt_rows = [l_ref[r * SB_ROWS:(r + 1) * SB_ROWS, :] for r in range(nrc)]
        upto = _tri_ones(SB_SUB, lambda r, c: r <= c)
        below = _tri_ones(SB_SUB, lambda r, c: r < c)

        def span(start, kw, l_sum, g_sum, dq, diagonal):
            start = pl.multiple_of(start, kw)
            k_w = k_ref[pl.ds(start, kw), :]
            v_w = v_ref[pl.ds(start, kw), :]
            z_scr[:, :kw] = lax.dot_general(qb, k_w, _dims(False, True), preferred_element_type=F32)
            da_scr[:, :kw] = lax.dot_general(dob, v_w, _dims(False, True), preferred_element_type=F32)
            l_rows = [l_sum[r * SB_ROWS:(r + 1) * SB_ROWS] for r in range(nrc)]
            g_rows = [g_sum[r * SB_ROWS:(r + 1) * SB_ROWS] for r in range(nrc)]
            for s in range(kw // SB_SUB):
                cols = slice(s * SB_SUB, (s + 1) * SB_SUB)
                for r in range(nrc):
                    rows = slice(r * SB_ROWS, (r + 1) * SB_ROWS)
                    before = _sb_chunk_mask(diagonal, r, s)
                    if isinstance(before, str):
                        a_scr[rows, cols] = jnp.zeros((SB_ROWS, SB_SUB), BF16)
                        dz_scr[rows, cols] = jnp.zeros((SB_ROWS, SB_SUB), BF16)
                        continue
                    zs = z_scr[rows, cols] * SB_SCALE
                    e, log_beta, log_1m = _sb_terms(zs, before)
                    l_prefix, l_total = _sums(log_1m, upto)
                    a = jnp.exp(log_beta + (lt_rows[r] - (l_rows[r] + l_prefix)))
                    if before is not None:
                        a = jnp.where(before, a, 0.0)
                    g = a * da_scr[rows, cols]
                    g_prefix, g_total = _sums(g, below)
                    inv = 1.0 / (1.0 + e)
                    beta = jnp.where(zs >= 0, inv, e * inv)
                    dz = (g - (g + g_rows[r] + g_prefix) * beta) * SB_SCALE
                    if before is not None:
                        dz = jnp.where(before, dz, 0.0)
                    a_scr[rows, cols] = a.astype(BF16)
                    dz_scr[rows, cols] = dz.astype(BF16)
                    l_rows[r] = l_rows[r] + l_total
                    g_rows[r] = g_rows[r] + g_total
            dz_w = dz_scr[:, :kw]
            dq = dq + jnp.dot(dz_w, k_w, preferred_element_type=F32)
            dk_ref[pl.ds(start, kw), :] += lax.dot_general(dz_w, qb, _dims(True, False), preferred_element_type=F32)
            dv_ref[pl.ds(start, kw), :] += lax.dot_general(a_scr[:, :kw], dob, _dims(True, False),
                                                           preferred_element_type=F32)
            return jnp.concatenate(l_rows, axis=0), jnp.concatenate(g_rows, axis=0), dq

        taken = jnp.clip(jnp.max(n_ref[...]).astype(jnp.int32), 0, i * (bq // bk))
        zero = jnp.zeros((bq, SB_SUB), F32)
        carry = lax.fori_loop(0, taken, lambda j, c: span(i * bq - (taken - j) * bk, bk, c[0], c[1], c[2], False),
                              (zero, zero, jnp.zeros((bq, HEAD_DIM), F32)))
        _, _, dq = span(i * bq, bq, carry[0], carry[1], carry[2], True)
        dq_ref[...] = dq.astype(BF16)

    nq = n // bq
    return _call(
        body, [q, kv, kv, do, lsum, spans], name=name, grid=(N_HEADS, nq),
        in_specs=[pl.BlockSpec((bq, HEAD_DIM), lambda h, i: (i, h)),
                  pl.BlockSpec((n, HEAD_DIM), lambda h, i: (0, h)),
                  pl.BlockSpec((n, HEAD_DIM), lambda h, i: (0, N_HEADS + h)),
                  pl.BlockSpec((bq, HEAD_DIM), lambda h, i: (i, h)),
                  pl.BlockSpec((None, bq, 128), lambda h, i: (h, i, 0)),
                  pl.BlockSpec((None, None, 8, 128), lambda h, i: (h, i, 0, 0))],
        out_specs=[pl.BlockSpec((bq, HEAD_DIM), lambda h, i: (i, h)),
                   pl.BlockSpec((n, HEAD_DIM), lambda h, i: (0, h)),
                   pl.BlockSpec((n, HEAD_DIM), lambda h, i: (0, h))],
        out_shape=[jax.ShapeDtypeStruct((n, D_MODEL), BF16), jax.ShapeDtypeStruct((n, D_MODEL), F32),
                   jax.ShapeDtypeStruct((n, D_MODEL), F32)],
        scratch_shapes=[pltpu.VMEM((bq, bq), F32), pltpu.VMEM((bq, bq), F32), pltpu.VMEM((bq, bq), BF16),
                        pltpu.VMEM((bq, bq), BF16)],
        compiler_params=_params("arbitrary", "arbitrary"),
        ride=ride, first=lambda: (pl.program_id(0) == 0) & (pl.program_id(1) == 0),
        last=lambda: (pl.program_id(0) == N_HEADS - 1) & (pl.program_id(1) == nq - 1))


def _local_step(x, target, gains, comm):
    g = gains
    w = comm.w
    big = {}
    row = lambda a, i: a[i:i + 1, :]

    proj, h0 = _norm_matmul(x, row(g["mix_pre"], 0), w["gdn_in"], F32, "gdn_in_proj", tn=1408)
    qn, kn, v, bg = _gdn_pre(proj, w["conv"], g["alog"], g["dtb"], "gdn_pre")
    (o_gdn, saved, inverses), got = _gdn_scan(qn, kn, v, bg, "gdn_scan", ride=comm.ride("scan"))
    comm.done("scan", got)
    x1, y_mix0, a_gdn = _out_proj((o_gdn, proj), True, x, w["gdn_out"], row(g["mix_post"], 0), g["out_gain"],
                                  "gdn_out_proj", tm=512)
    x2, y_mlp0, h_mlp0, u0, a0 = _mlp_fwd(x1, row(g["mlp_pre"], 0), w["up0"], w["down0"], row(g["mlp_post"], 0), "mlp0")
    kv, h_kv = _norm_matmul(x2, g["kv"], w["kv"], BF16, "kv_proj", tm=2048)
    q, h_q = _norm_matmul(x2, row(g["mix_pre"], 1), w["sb_q"], BF16, "sb_q_proj")
    (o_sb, lsum, spans), got = _sb_attention(q, kv, "sb_attention", ride=comm.ride("sb"))
    comm.done("sb", got)
    x3, y_mix1 = _out_proj((o_sb,), False, x2, w["sb_o"], row(g["mix_post"], 1), None, "sb_out_proj")
    dx4, y_mlp1, h_mlp1, u1, a1, loss = _mlp_fwd(x3, row(g["mlp_pre"], 1), w["up1"], w["down1"], row(g["mlp_post"], 1),
                                                 "mlp1_loss", target=target)

    dx3, dy_mlp1, du1, dg_mlp1 = _mlp_bwd(dx4, y_mlp1, row(g["mlp_post"], 1), w["down1"], u1, w["up1"], x3,
                                          row(g["mlp_pre"], 1), "mlp1_bwd")
    big["down1"] = _matmul_tn(a1, dy_mlp1, 1, "d_down1")[0]
    big["up1"] = _matmul_tn(h_mlp1, du1, N_DEV, "d_up1", tk=2048)
    dy_mix1, dg_post1, do_sb = _out_proj_bwd(dx3, y_mix1, w["sb_o"], row(g["mix_post"], 1), None, "sb_out_proj_bwd")
    big["sb_o"] = _matmul_tn(o_sb, dy_mix1, 1, "d_sb_o")[0]
    (dq, dk, dv), got = _sb_attention_bwd(q, kv, do_sb, lsum, spans, "sb_attention_bwd", ride=comm.ride("sb_bwd", big))
    comm.done("sb_bwd", got)
    big["sb_q"] = _matmul_tn(h_q, dq, 1, "d_sb_q")[0]
    big["kv"] = jnp.concatenate([_matmul_tn(h_kv, dk, N_DEV // 2, "d_w_k", tk=2048),
                                 _matmul_tn(h_kv, dv, N_DEV // 2, "d_w_v", tk=2048)],
                                axis=0)
    dx2, dg_x2 = _norm_matmul_bwd([(dq, w["sb_q"], row(g["mix_pre"], 1), 512), (dk, w["kv"][:N_DEV // 2], g["kv"], 256),
                                   (dv, w["kv"][N_DEV // 2:], g["kv"], 256)], x2, dx3, "qkv_proj_bwd")

    dx1, dy_mlp0, du0, dg_mlp0 = _mlp_bwd(dx2, y_mlp0, row(g["mlp_post"], 0), w["down0"], u0, w["up0"], x1,
                                          row(g["mlp_pre"], 0), "mlp0_bwd")
    big["down0"] = _matmul_tn(a0, dy_mlp0, 1, "d_down0")[0]
    big["up0"] = _matmul_tn(h_mlp0, du0, N_DEV, "d_up0", tk=2048)
    dy_mix0, dg_post0, do_gdn, dgate, d_out_gain = _out_proj_bwd(
        dx1, y_mix0, w["gdn_out"], row(g["mix_post"], 0), (o_gdn, proj, g["out_gain"]), "gdn_out_proj_bwd", tm=512)
    big["gdn_out"] = _matmul_tn(a_gdn, dy_mix0, 1, "d_gdn_out")[0]
    (dqn, dkn, dvv, dbg), got = _gdn_scan_bwd(qn, kn, v, bg, saved, inverses, do_gdn, "gdn_scan_bwd",
                                              ride=comm.ride("scan_bwd", big))
    comm.done("scan_bwd", got)
    dproj, d_conv, d_gates = _gdn_pre_bwd(proj, w["conv"], g["alog"], g["dtb"], dqn, dkn, dvv, dbg, dgate, "gdn_pre_bwd")
    big["gdn_in"] = _matmul_tn(h0, dproj, 1, "d_gdn_in", tb=1408, tk=512)[0]
    ride = comm.ride("in_bwd", big)
    grad_x, dg_pre0, *got = _norm_matmul_bwd([(dproj, w["gdn_in"], row(g["mix_pre"], 0), 1408)], x, dx1, "gdn_in_proj_bwd",
                                             ride=ride)
    comm.done("in_bwd", got[0] if got else [])

    small = {"pre0": dg_pre0, "x2": dg_x2, "post0": dg_post0, "post1": dg_post1, "mlp0": dg_mlp0, "mlp1": dg_mlp1,
             "gates": d_gates, "out_gain": d_out_gain, "conv": d_conv, "loss": loss}
    return grad_x, big, small


class _LocalOnly:
    def __init__(self, weights):
        self.w = weights

    def ride(self, stage, grads=None):
        return None

    def done(self, stage, outs):
        pass


GATHER_STAGES = {"scan": ("gdn_out", "up0", "down0", "kv", "sb_q"), "sb": ("sb_o", "up1", "down1")}
SCATTER_STAGES = {"sb_bwd": ("up1", "down1", "sb_o"), "scan_bwd": ("sb_q", "kv", "up0", "down0", "gdn_out"),
                  "in_bwd": ("gdn_in",)}


def _whole_weight(name, gathered):
    d = D_MODEL
    if name in ("up0", "up1", "kv"):
        return gathered
    if name == "gdn_in":
        whole = gathered.transpose(1, 0, 2).reshape(d, GDN_IN_COLS)
        return jnp.pad(whole, ((0, 0), (0, GDN_IN_PAD - GDN_IN_COLS)))[None]
    if name == "conv":
        return gathered.transpose(1, 0, 2).reshape(CONV_WIDTH, QKV)
    whole = gathered.reshape(gathered.shape[0] * gathered.shape[1], d)
    return whole[None] if name == "sb_q" else whole


def _owner_blocks(name, grad):
    if name in ("up0", "up1", "kv"):
        blocks = grad
    elif name == "gdn_in":
        blocks = grad[:, :GDN_IN_COLS].reshape(D_MODEL, N_DEV, GDN_IN_COLS // N_DEV).transpose(1, 0, 2)
    else:
        blocks = grad.reshape(N_DEV, grad.shape[0] // N_DEV, grad.shape[1])
    return blocks.astype(BF16)


class _Fsdp:
    def __init__(self, shards, weights):
        self.shards, self.w, self.landed = shards, weights, {}

    def ride(self, stage, grads=None):
        if stage in GATHER_STAGES:
            names = GATHER_STAGES[stage]
            return _Exchange([self.shards[nm] for nm in names], [True] * len(names))
        names = SCATTER_STAGES[stage]
        return _Exchange([_owner_blocks(nm, grads[nm]) for nm in names], [False] * len(names))

    def done(self, stage, outs):
        if stage in GATHER_STAGES:
            for nm, out in zip(GATHER_STAGES[stage], outs):
                self.w[nm] = _whole_weight(nm, out)
        else:
            self.landed.update(zip(SCATTER_STAGES[stage], outs))


def _my_place():
    return lax.axis_index("x"), lax.axis_index("y"), lax.axis_index("c")


class _Exchange:
    def __init__(self, arrays, gather):
        self.arrays, self.gather, self.n = list(arrays), list(gather), len(arrays)
        any_spec = pl.BlockSpec(memory_space=pl.ANY)
        self.in_specs = [any_spec] * self.n
        self.out_specs = [any_spec] * self.n
        self.out_shape = [jax.ShapeDtypeStruct(((N_DEV,) + a.shape) if g else a.shape, a.dtype)
                          for a, g in zip(self.arrays, self.gather)]
        self.scratch = [pltpu.SemaphoreType.DMA((self.n, N_DEV - 1)), pltpu.SemaphoreType.DMA((self.n, N_DEV - 1)),
                        pltpu.SemaphoreType.DMA((self.n,))]

    def _copies(self, ins, outs, sems):
        send_sems, recv_sems, local_sems = sems
        x, y, c = _my_place()
        me = 4 * x + 2 * y + c
        copies = []
        for a in range(self.n):
            src = ins[a] if self.gather[a] else ins[a].at[me]
            copies.append(pltpu.make_async_copy(src, outs[a].at[me], local_sems.at[a]))
        for k in range(1, N_DEV):
            px = 1 - x if k & 4 else x
            py = 1 - y if k & 2 else y
            pc = 1 - c if k & 1 else c
            peer = 4 * px + 2 * py + pc
            for a in range(self.n):
                src = ins[a] if self.gather[a] else ins[a].at[peer]
                copies.append(pltpu.make_async_remote_copy(
                    src_ref=src, dst_ref=outs[a].at[me], send_sem=send_sems.at[a, k - 1], recv_sem=recv_sems.at[a, k - 1],
                    device_id=(px, py, pc), device_id_type=MESH))
        return copies

    def start(self, ins, outs, sems):
        for cp in self._copies(ins, outs, sems):
            cp.start()

    def wait(self, ins, outs, sems):
        for cp in self._copies(ins, outs, sems):
            cp.wait()


def _call(body, operands, *, name, out_shape, in_specs, out_specs, grid=(), scratch_shapes=(), compiler_params=None,
          ride=None, first=None, last=None):
    n_in, n_out, n_scr = len(operands), len(out_shape), len(scratch_shapes)
    if ride is None:
        outs = pl.pallas_call(body, name=name, grid=grid, in_specs=in_specs, out_specs=out_specs, out_shape=out_shape,
                              scratch_shapes=scratch_shapes, compiler_params=compiler_params)(*operands)
        return list(outs), []
    r = ride.n

    def riding(*refs):
        ins, r_ins = refs[:n_in], refs[n_in:n_in + r]
        outs, r_outs = refs[n_in + r:n_in + r + n_out], refs[n_in + r + n_out:n_in + 2 * r + n_out]
        scr, sems = refs[n_in + 2 * r + n_out:n_in + 2 * r + n_out + n_scr], refs[n_in + 2 * r + n_out + n_scr:]

        @pl.when(first())
        def _():
            ride.start(r_ins, r_outs, sems)

        body(*ins, *outs, *scr)

        @pl.when(last())
        def _():
            ride.wait(r_ins, r_outs, sems)

    outs = pl.pallas_call(
        riding, name=name, grid=grid, in_specs=list(in_specs) + ride.in_specs, out_specs=list(out_specs) + ride.out_specs,
        out_shape=list(out_shape) + ride.out_shape, scratch_shapes=list(scratch_shapes) + ride.scratch,
        compiler_params=compiler_params)(*operands, *ride.arrays)
    return list(outs[:n_out]), list(outs[n_out:])


def _exchange(arrays, gather, name):
    ride = _Exchange(arrays, gather)

    def body(*refs):
        ins, outs, sems = refs[:ride.n], refs[ride.n:2 * ride.n], refs[2 * ride.n:]
        ride.start(ins, outs, sems)
        ride.wait(ins, outs, sems)

    return pl.pallas_call(body, name=name, in_specs=ride.in_specs, out_specs=ride.out_specs, out_shape=ride.out_shape,
                          scratch_shapes=ride.scratch)(*arrays)


def _adamw_math(g, w, m, v):
    m = ADAM_B1 * m + (1.0 - ADAM_B1) * g
    v = ADAM_B2 * v + (1.0 - ADAM_B2) * jnp.square(g)
    m_hat = m / (1.0 - ADAM_B1 ** ADAM_STEP)
    v_hat = v / (1.0 - ADAM_B2 ** ADAM_STEP)
    delta = -ADAM_LR * (m_hat / (jnp.sqrt(v_hat) + ADAM_EPS) + ADAM_WD * w)
    return delta, m, v


def _sum_devices(ref):
    total = ref[0].astype(F32)
    for d in range(1, N_DEV):
        total = total + ref[d].astype(F32)
    return total


def _reduce_adamw(landed, w, m, v, name, tr=256):
    r, c = w.shape
    tr = min(tr, r)
    assert r % tr == 0

    def body(l_ref, w_ref, m_ref, v_ref, g_ref, d_ref, nm_ref, nv_ref):
        g = _sum_devices(l_ref)
        g_ref[...] = g
        d_ref[...], nm_ref[...], nv_ref[...] = _adamw_math(g, w_ref[...], m_ref[...], v_ref[...])

    blk = pl.BlockSpec((tr, c), lambda i: (i, 0))
    return pl.pallas_call(
        body, name=name, grid=(r // tr,),
        in_specs=[pl.BlockSpec((N_DEV, tr, c), lambda i: (0, i, 0)), blk, blk, blk],
        out_specs=[blk] * 4, out_shape=[jax.ShapeDtypeStruct((r, c), F32)] * 4,
        compiler_params=_params("parallel"),
    )(landed, w, m, v)


def _adamw(g, w, m, v, name):
    def body(g_ref, w_ref, m_ref, v_ref, d_ref, nm_ref, nv_ref):
        d_ref[...], nm_ref[...], nv_ref[...] = _adamw_math(g_ref[...], w_ref[...], m_ref[...], v_ref[...])

    return pl.pallas_call(body, name=name, out_shape=[jax.ShapeDtypeStruct(w.shape, F32)] * 3)(g, w, m, v)


def _small_update(landed, params, name):
    layout = {
        "mix_pre": [("pre0", 0), ("x2", 0)], "mix_post": [("post0", 0), ("post1", 0)],
        "mlp_pre": [("mlp0", 0), ("mlp1", 0)], "mlp_post": [("mlp0", 1), ("mlp1", 1)],
        "kv": [("kv", 0)], "alog": [("gates", 0)], "dtb": [("gates", 1)], "out_gain": [("out_gain", 0)],
    }
    landed_names = sorted(landed)
    param_names = sorted(params)
    n_l, n_p = len(landed_names), len(param_names)

    def body(*refs):
        l_refs = dict(zip(landed_names, refs[:n_l]))
        p_refs = {p: refs[n_l + 3 * i:n_l + 3 * i + 3] for i, p in enumerate(param_names)}
        outs = refs[n_l + 3 * n_p:]
        o_refs = {p: outs[4 * i:4 * i + 4] for i, p in enumerate(param_names)}
        conv_ref, loss_ref = outs[4 * n_p:]
        sums = {nm: _sum_devices(l_refs[nm]) for nm in landed_names}
        sums["kv"] = sums["x2"][1:2, :] + sums["x2"][2:3, :]
        for p in param_names:
            w_ref, m_ref, v_ref = p_refs[p]
            g_ref, d_ref, nm_ref, nv_ref = o_refs[p]
            for r, (src, src_row) in enumerate(layout[p]):
                g = sums[src][src_row:src_row + 1, :]
                g_ref[r:r + 1, :] = g
                d, nm, nv = _adamw_math(g, w_ref[r:r + 1, :], m_ref[r:r + 1, :], v_ref[r:r + 1, :])
                d_ref[r:r + 1, :] = d
                nm_ref[r:r + 1, :] = nm
                nv_ref[r:r + 1, :] = nv
        conv_ref[...] = sums["conv"]
        loss_ref[...] = sums["loss"]

    args = [landed[nm] for nm in landed_names]
    out_shape = []
    for p in param_names:
        args += list(params[p])
        out_shape += [jax.ShapeDtypeStruct(params[p][0].shape, F32)] * 4
    out_shape += [jax.ShapeDtypeStruct(landed["conv"].shape[1:], F32), jax.ShapeDtypeStruct(landed["loss"].shape[1:], F32)]
    outs = pl.pallas_call(body, name=name, out_shape=out_shape)(*args)
    result = {p: tuple(outs[4 * i:4 * i + 4]) for i, p in enumerate(param_names)}
    result["conv"], result["loss"] = outs[4 * n_p], outs[4 * n_p + 1]
    return result


def _lanes(vec, offset):
    return jnp.pad(vec[None, :], ((0, 0), (offset, 128 - offset - vec.shape[0])))


def kernel(x, mix_pre_gain, mix_post_gain, mlp_pre_gain, mlp_post_gain, mlp_w_up, mlp_w_down, gdn_w_in, gdn_conv_w, gdn_a_log, gdn_dt_bias, gdn_out_gain, gdn_w_out, kv_gain, w_kv, sb_w_q, sb_w_o, loss_target, m_mix_pre_gain, m_mix_post_gain, m_mlp_pre_gain, m_mlp_post_gain, m_mlp_w_up, m_mlp_w_down, m_gdn_w_in, m_gdn_conv_w, m_gdn_a_log, m_gdn_dt_bias, m_gdn_out_gain, m_gdn_w_out, m_kv_gain, m_w_kv, m_sb_w_q, m_sb_w_o, v_mix_pre_gain, v_mix_post_gain, v_mlp_pre_gain, v_mlp_post_gain, v_mlp_w_up, v_mlp_w_down, v_gdn_w_in, v_gdn_conv_w, v_gdn_a_log, v_gdn_dt_bias, v_gdn_out_gain, v_gdn_w_out, v_kv_gain, v_w_kv, v_sb_w_q, v_sb_w_o):
    me = 4 * lax.axis_index("x") + 2 * lax.axis_index("y") + lax.axis_index("c")
    bf = lambda a: a.astype(BF16)

    shards = {"up0": bf(mlp_w_up[0]), "up1": bf(mlp_w_up[1]), "down0": bf(mlp_w_down[0]), "down1": bf(mlp_w_down[1]),
              "gdn_out": bf(gdn_w_out[0]), "kv": bf(w_kv), "sb_q": bf(sb_w_q[0]), "sb_o": bf(sb_w_o[0])}
    gdn_in, conv = _exchange([bf(gdn_w_in[0]), gdn_conv_w[0]], [True, True], "gather_first_weights")
    comm = _Fsdp(shards, {"gdn_in": _whole_weight("gdn_in", gdn_in), "conv": _whole_weight("conv", conv)})
    gains = {"mix_pre": mix_pre_gain, "mix_post": mix_post_gain, "mlp_pre": mlp_pre_gain, "mlp_post": mlp_post_gain,
             "kv": kv_gain[None, :], "alog": _lanes(gdn_a_log[0], N_HEADS), "dtb": _lanes(gdn_dt_bias[0], N_HEADS),
             "out_gain": gdn_out_gain}

    grad_x, big, small = _local_step(x[0], loss_target[0], gains, comm)

    small_names = ["pre0", "x2", "post0", "post1", "mlp0", "mlp1", "gates", "out_gain", "conv", "loss"]
    landed = _exchange([small[nm] for nm in small_names], [True] * len(small_names), "exchange_small_grads")
    small_landed = dict(zip(small_names, landed))

    results = {}
    big_params = [("mlp_w_up0", "up0", mlp_w_up[0], m_mlp_w_up[0], v_mlp_w_up[0]),
                  ("mlp_w_up1", "up1", mlp_w_up[1], m_mlp_w_up[1], v_mlp_w_up[1]),
                  ("mlp_w_down0", "down0", mlp_w_down[0], m_mlp_w_down[0], v_mlp_w_down[0]),
                  ("mlp_w_down1", "down1", mlp_w_down[1], m_mlp_w_down[1], v_mlp_w_down[1]),
                  ("gdn_w_in", "gdn_in", gdn_w_in[0], m_gdn_w_in[0], v_gdn_w_in[0]),
                  ("gdn_w_out", "gdn_out", gdn_w_out[0], m_gdn_w_out[0], v_gdn_w_out[0]),
                  ("w_kv", "kv", w_kv, m_w_kv, v_w_kv), ("sb_w_q", "sb_q", sb_w_q[0], m_sb_w_q[0], v_sb_w_q[0]),
                  ("sb_w_o", "sb_o", sb_w_o[0], m_sb_w_o[0], v_sb_w_o[0])]
    for nm, short, w_, m_, v_ in big_params:
        results[nm] = _reduce_adamw(comm.landed[short], w_, m_, v_, "adamw_" + nm)
    lanes8 = lambda a: _lanes(a[0], N_HEADS)
    small_params = {
        "mix_pre": (mix_pre_gain, m_mix_pre_gain, v_mix_pre_gain), "mix_post": (mix_post_gain, m_mix_post_gain, v_mix_post_gain),
        "mlp_pre": (mlp_pre_gain, m_mlp_pre_gain, v_mlp_pre_gain), "mlp_post": (mlp_post_gain, m_mlp_post_gain, v_mlp_post_gain),
        "kv": (kv_gain[None, :], m_kv_gain[None, :], v_kv_gain[None, :]),
        "alog": (lanes8(gdn_a_log), lanes8(m_gdn_a_log), lanes8(v_gdn_a_log)),
        "dtb": (lanes8(gdn_dt_bias), lanes8(m_gdn_dt_bias), lanes8(v_gdn_dt_bias)),
        "out_gain": (gdn_out_gain, m_gdn_out_gain, v_gdn_out_gain),
    }
    sm = _small_update(small_landed, small_params, "small_update")
    conv_cols = QKV // N_DEV
    g_conv = lax.dynamic_slice(sm["conv"], (0, me * conv_cols), (8, conv_cols))[:CONV_WIDTH]
    conv_res = (g_conv,) + tuple(_adamw(g_conv, gdn_conv_w[0], m_gdn_conv_w[0], v_gdn_conv_w[0], "adamw_conv"))

    stack2 = lambda a, b: tuple(jnp.stack([p, q]) for p, q in zip(results[a], results[b]))
    lead = lambda t: tuple(a[None] for a in t)
    heads8 = lambda t: tuple(a[:, N_HEADS:2 * N_HEADS] for a in t)
    per_weight = [
        sm["mix_pre"], sm["mix_post"], sm["mlp_pre"], sm["mlp_post"],
        stack2("mlp_w_up0", "mlp_w_up1"), stack2("mlp_w_down0", "mlp_w_down1"),
        lead(results["gdn_w_in"]), lead(conv_res), heads8(sm["alog"]), heads8(sm["dtb"]), sm["out_gain"],
        lead(results["gdn_w_out"]), tuple(a[0] for a in sm["kv"]), results["w_kv"], lead(results["sb_w_q"]), lead(results["sb_w_o"]),
    ]
    grads, deltas, new_ms, new_vs = zip(*per_weight)
    return (sm["loss"][0, 0], grad_x[None], *grads, *deltas, *new_ms, *new_vs)
```

```python
import functools
import math

import jax
import jax.numpy as jnp
from jax import lax
from jax.experimental import pallas as pl
from jax.experimental.pallas import tpu as pltpu

F32 = jnp.float32
BF16 = jnp.bfloat16

N_DEV = 8
D_MODEL = 1024
D_FF = 4096
N_HEADS = 8
HEAD_DIM = 128
CHUNK = 64
CONV_WIDTH = 4
GDN_IN_COLS = 4 * D_MODEL + 2 * N_HEADS
GDN_IN_PAD = 4 * D_MODEL + 128
EPS = 1e-6

ADAM_LR = 0.001
ADAM_B1 = 0.9
ADAM_B2 = 0.999
ADAM_EPS = 1e-08
ADAM_WD = 0.01
ADAM_STEP = 10

VMEM_LIMIT_BYTES = 56 * 1024 * 1024
MESH = pl.DeviceIdType.MESH


def _params(*semantics):
    return pltpu.CompilerParams(dimension_semantics=semantics, vmem_limit_bytes=VMEM_LIMIT_BYTES)


def _dims(ta, tb):
    return (((0,) if ta else (1,), (1,) if tb else (0,)), ((), ()))


def _dot(a, b, ta=False, tb=False):
    return lax.dot_general(a.astype(BF16), b.astype(BF16), _dims(ta, tb), preferred_element_type=F32)


def _dot_f32(a, b, ta=False, tb=False):
    return lax.dot_general(a, b, _dims(ta, tb), precision=lax.Precision.HIGHEST, preferred_element_type=F32)


def _dot_3x(a, b, ta=False, tb=False):
    return lax.dot_general(a, b, _dims(ta, tb), precision=lax.Precision.HIGH, preferred_element_type=F32)


def _make_mm(dot):
    @functools.partial(jax.custom_vjp, nondiff_argnums=(2, 3))
    def mm(a, b, ta, tb):
        return dot(a, b, ta, tb)

    def fwd(a, b, ta, tb):
        return dot(a, b, ta, tb), (a, b)

    def bwd(ta, tb, res, g):
        a, b = res
        if not ta and not tb:
            return mm(g, b, False, True), mm(a, g, True, False)
        if not ta and tb:
            return mm(g, b, False, False), mm(g, a, True, False)
        if ta and not tb:
            return mm(b, g, False, True), mm(a, g, False, False)
        raise NotImplementedError

    mm.defvjp(fwd, bwd)
    return mm


_mm = _make_mm(_dot)
_mm_f32 = _make_mm(_dot_f32)


def _rms(x, gain):
    r = lax.rsqrt(jnp.mean(x * x, axis=-1, keepdims=True) + EPS)
    return x * r * gain


def _rms_bwd(x, gain, dy):
    r = lax.rsqrt(jnp.mean(x * x, axis=-1, keepdims=True) + EPS)
    xh = x * r
    dgain = jnp.sum(dy * xh, axis=0, keepdims=True)
    dxh = dy * gain
    dx = r * (dxh - xh * jnp.mean(dxh * xh, axis=-1, keepdims=True))
    return dx, dgain


def _silu(x):
    return x / (1.0 + jnp.exp(-x))


def _acc_rows(ref, first, rows):
    @pl.when(first)
    def _():
        ref[...] = jnp.zeros_like(ref)

    for r, val in enumerate(rows):
        ref[r:r + 1, :] += val


def _norm_matmul(x, gain, w, out_dtype, name, tm=1024, tn=1024):
    n, d = x.shape
    g, _, wc = w.shape
    tm, tn = min(tm, n), min(tn, wc)
    per = wc // tn
    assert n % tm == 0 and wc % tn == 0

    def body(x_ref, gain_ref, w_ref, out_ref, h_ref):
        @pl.when(pl.program_id(1) == 0)
        def _():
            h_ref[...] = _rms(x_ref[...], gain_ref[...]).astype(BF16)

        out_ref[...] = jnp.dot(h_ref[...], w_ref[...], preferred_element_type=F32).astype(out_dtype)

    return pl.pallas_call(
        body, name=name, grid=(n // tm, g * per),
        in_specs=[pl.BlockSpec((tm, d), lambda i, j: (i, 0)),
                  pl.BlockSpec((1, d), lambda i, j: (0, 0)),
                  pl.BlockSpec((None, d, tn), lambda i, j: (j // per, 0, j % per))],
        out_specs=[pl.BlockSpec((tm, tn), lambda i, j: (i, j)),
                   pl.BlockSpec((tm, d), lambda i, j: (i, 0))],
        out_shape=[jax.ShapeDtypeStruct((n, g * wc), out_dtype), jax.ShapeDtypeStruct((n, d), BF16)],
        compiler_params=_params("parallel", "arbitrary"),
    )(x, gain, w)


def _norm_matmul_bwd(branches, x, add, name, tm=1024, ride=None):
    n, d = x.shape
    tm = min(tm, n)
    tns = [min(tn, w.shape[2]) for _, w, _, tn in branches]
    pers = [w.shape[2] // tn for (_, w, _, _), tn in zip(branches, tns)]
    ncols = [w.shape[0] * per for (_, w, _, _), per in zip(branches, pers)]
    offs = [sum(ncols[:b]) for b in range(len(branches))]
    total = sum(ncols)
    nb = len(branches)

    def body(*refs):
        dout_refs, w_refs, gain_refs = refs[:nb], refs[nb:2 * nb], refs[2 * nb:3 * nb]
        x_ref, add_ref, dx_ref, dgain_ref, acc_ref, sum_ref = refs[3 * nb:]
        i, j = pl.program_id(0), pl.program_id(1)

        @pl.when((i == 0) & (j == 0))
        def _():
            dgain_ref[...] = jnp.zeros_like(dgain_ref)

        for b in range(nb):
            first, last = offs[b], offs[b] + ncols[b] - 1

            @pl.when((j >= first) & (j <= last))
            def _(b=b, first=first):
                part = _dot(dout_refs[b][...], w_refs[b][...], tb=True)

                @pl.when(j == first)
                def _():
                    acc_ref[...] = part

                @pl.when(j > first)
                def _():
                    acc_ref[...] += part

            @pl.when(j == last)
            def _(b=b):
                xv = x_ref[...]
                xh = xv * lax.rsqrt(jnp.mean(xv * xv, axis=-1, keepdims=True) + EPS)
                dh = acc_ref[...]
                dgain_ref[b:b + 1, :] += jnp.sum(dh * xh, axis=0, keepdims=True)
                scaled = dh * gain_refs[b][...]
                if b == 0:
                    sum_ref[...] = scaled
                else:
                    sum_ref[...] += scaled

        @pl.when(j == total - 1)
        def _():
            xv = x_ref[...]
            r = lax.rsqrt(jnp.mean(xv * xv, axis=-1, keepdims=True) + EPS)
            xh = xv * r
            dxh = sum_ref[...]
            dx_ref[...] = add_ref[...] + r * (dxh - xh * jnp.mean(dxh * xh, axis=-1, keepdims=True))

    def dout_spec(b):
        return pl.BlockSpec((tm, tns[b]), lambda i, j: (i, jnp.clip(j - offs[b], 0, ncols[b] - 1)))

    def w_spec(b):
        def index(i, j):
            c = jnp.clip(j - offs[b], 0, ncols[b] - 1)
            return (c // pers[b], 0, c % pers[b])
        return pl.BlockSpec((None, d, tns[b]), index)

    row = lambda i, j: (i, 0)
    const = lambda i, j: (0, 0)
    nrow = n // tm
    (dx, dgain), got = _call(
        body, [b[0] for b in branches] + [b[1] for b in branches] + [b[2] for b in branches] + [x, add],
        name=name, grid=(nrow, total),
        in_specs=[dout_spec(b) for b in range(nb)] + [w_spec(b) for b in range(nb)]
                 + [pl.BlockSpec((1, d), const)] * nb + [pl.BlockSpec((tm, d), row), pl.BlockSpec((tm, d), row)],
        out_specs=[pl.BlockSpec((tm, d), row), pl.BlockSpec((8, d), const)],
        out_shape=[jax.ShapeDtypeStruct((n, d), F32), jax.ShapeDtypeStruct((8, d), F32)],
        scratch_shapes=[pltpu.VMEM((tm, d), F32), pltpu.VMEM((tm, d), F32)],
        compiler_params=_params("arbitrary", "arbitrary"),
        ride=ride, first=lambda: (pl.program_id(0) == 0) & (pl.program_id(1) == 0),
        last=lambda: (pl.program_id(0) == nrow - 1) & (pl.program_id(1) == total - 1))
    return (dx, dgain) if ride is None else (dx, dgain, got)


def _matmul_tn(a, b, groups, name, ta=1024, tb=1024, tk=1024):
    n, ka = a.shape
    _, kb = b.shape
    wc = kb // groups
    ta, tb, tk = min(ta, ka), min(tb, wc), min(tk, n)
    per = wc // tb
    nk = n // tk
    assert ka % ta == 0 and wc % tb == 0 and n % tk == 0

    def body(a_ref, b_ref, out_ref, acc_ref):
        k = pl.program_id(2)

        @pl.when(k == 0)
        def _():
            acc_ref[...] = jnp.zeros_like(acc_ref)

        acc_ref[...] += _dot(a_ref[...], b_ref[...], ta=True)

        @pl.when(k == nk - 1)
        def _():
            out_ref[...] = acc_ref[...].astype(BF16)

    return pl.pallas_call(
        body, name=name, grid=(ka // ta, groups * per, nk),
        in_specs=[pl.BlockSpec((tk, ta), lambda i, j, k: (k, i)),
                  pl.BlockSpec((tk, tb), lambda i, j, k: (k, j))],
        out_specs=pl.BlockSpec((None, ta, tb), lambda i, j, k: (j // per, i, j % per)),
        out_shape=jax.ShapeDtypeStruct((groups, ka, wc), BF16),
        scratch_shapes=[pltpu.VMEM((ta, tb), F32)],
        compiler_params=_params("parallel", "parallel", "arbitrary"),
    )(a, b)


def _gated_head_norm(o, gate, out_gain):
    parts = []
    for h in range(N_HEADS):
        sl = slice(h * HEAD_DIM, (h + 1) * HEAD_DIM)
        parts.append(_rms(o[:, sl], out_gain) * _silu(gate[:, sl]))
    return parts


def _out_proj(a_inputs, gated, x_in, w, gain, out_gain, name, tm=1024):
    n, d = x_in.shape
    k = w.shape[0]
    tm = min(tm, n)

    def body(*refs):
        if gated:
            o_ref, gate_ref, og_ref, x_ref, w_ref, gain_ref, xo_ref, y_ref, a_ref = refs
            parts = _gated_head_norm(o_ref[...], gate_ref[...], og_ref[...])
            for h, part in enumerate(parts):
                a_ref[:, h * HEAD_DIM:(h + 1) * HEAD_DIM] = part.astype(BF16)
            a = a_ref[...]
        else:
            a_in_ref, x_ref, w_ref, gain_ref, xo_ref, y_ref = refs
            a = a_in_ref[...]
        y = jnp.dot(a, w_ref[...], preferred_element_type=F32)
        y_ref[...] = y
        xo_ref[...] = x_ref[...] + _rms(y, gain_ref[...])

    row = lambda i: (i, 0)
    const = lambda i: (0, 0)
    if gated:
        a_specs = [pl.BlockSpec((tm, k), row), pl.BlockSpec((tm, D_MODEL), lambda i: (i, 3)),
                   pl.BlockSpec((1, HEAD_DIM), const)]
        a_args = list(a_inputs) + [out_gain]
    else:
        a_specs = [pl.BlockSpec((tm, k), row)]
        a_args = list(a_inputs)
    out_specs = [pl.BlockSpec((tm, d), row), pl.BlockSpec((tm, d), row)]
    out_shape = [jax.ShapeDtypeStruct((n, d), F32), jax.ShapeDtypeStruct((n, d), F32)]
    if gated:
        out_specs.append(pl.BlockSpec((tm, k), row))
        out_shape.append(jax.ShapeDtypeStruct((n, k), BF16))
    return pl.pallas_call(
        body, name=name, grid=(n // tm,),
        in_specs=a_specs + [pl.BlockSpec((tm, d), row), pl.BlockSpec((k, d), const), pl.BlockSpec((1, d), const)],
        out_specs=out_specs, out_shape=out_shape,
        compiler_params=_params("parallel"),
    )(*a_args, x_in, w, gain)


def _out_proj_bwd(dxo, y, w, gain, gated_inputs, name, tm=1024):
    n, d = dxo.shape
    k = w.shape[0]
    tm = min(tm, n)
    gated = gated_inputs is not None

    def body(*refs):
        if gated:
            (dxo_ref, y_ref, w_ref, gain_ref, o_ref, gate_ref, og_ref,
             dy_ref, dgain_ref, do_ref, dgate_ref, dog_ref) = refs
        else:
            dxo_ref, y_ref, w_ref, gain_ref, dy_ref, dgain_ref, da_ref = refs
        first = pl.program_id(0) == 0
        dy, dgain = _rms_bwd(y_ref[...], gain_ref[...], dxo_ref[...])
        dy_ref[...] = dy.astype(BF16)
        _acc_rows(dgain_ref, first, [dgain])
        da = _dot(dy_ref[...], w_ref[...], tb=True)
        if not gated:
            da_ref[...] = da.astype(BF16)
            return
        og = og_ref[...]
        dog = jnp.zeros_like(og)
        for h in range(N_HEADS):
            sl = slice(h * HEAD_DIM, (h + 1) * HEAD_DIM)
            fn = lambda o_h, g_h, gn: _rms(o_h, gn) * _silu(g_h)
            _, vjp = jax.vjp(fn, o_ref[:, sl], gate_ref[:, sl], og)
            do_h, dgate_h, dog_h = vjp(da[:, sl])
            do_ref[:, sl] = do_h
            dgate_ref[:, sl] = dgate_h
            dog = dog + dog_h
        _acc_rows(dog_ref, first, [dog])

    row = lambda i: (i, 0)
    const = lambda i: (0, 0)
    in_specs = [pl.BlockSpec((tm, d), row), pl.BlockSpec((tm, d), row), pl.BlockSpec((k, d), const),
                pl.BlockSpec((1, d), const)]
    args = [dxo, y, w, gain]
    out_specs = [pl.BlockSpec((tm, d), row), pl.BlockSpec((8, d), const)]
    out_shape = [jax.ShapeDtypeStruct((n, d), BF16), jax.ShapeDtypeStruct((8, d), F32)]
    if gated:
        in_specs += [pl.BlockSpec((tm, k), row), pl.BlockSpec((tm, D_MODEL), lambda i: (i, 3)),
                     pl.BlockSpec((1, HEAD_DIM), const)]
        args += list(gated_inputs)
        out_specs += [pl.BlockSpec((tm, k), row), pl.BlockSpec((tm, k), row), pl.BlockSpec((8, HEAD_DIM), const)]
        out_shape += [jax.ShapeDtypeStruct((n, k), F32), jax.ShapeDtypeStruct((n, k), F32),
                      jax.ShapeDtypeStruct((8, HEAD_DIM), F32)]
    else:
        out_specs.append(pl.BlockSpec((tm, k), row))
        out_shape.append(jax.ShapeDtypeStruct((n, k), BF16))
    return pl.pallas_call(
        body, name=name, grid=(n // tm,), in_specs=in_specs, out_specs=out_specs, out_shape=out_shape,
        compiler_params=_params("arbitrary"),
    )(*args)


def _mlp_fwd(x_in, g_pre, w_up, w_down, g_post, name, tm=1024, target=None):
    n, d = x_in.shape
    g, _, wc = w_up.shape
    tm = min(tm, n)
    with_loss = target is not None

    def body(*refs):
        if with_loss:
            x_ref, gpre_ref, wup_ref, wdown_ref, gpost_ref, t_ref, xo_ref, y_ref, h_ref, u_ref, a_ref, loss_ref, acc_ref = refs
        else:
            x_ref, gpre_ref, wup_ref, wdown_ref, gpost_ref, xo_ref, y_ref, h_ref, u_ref, a_ref, acc_ref = refs
        i, j = pl.program_id(0), pl.program_id(1)

        @pl.when(j == 0)
        def _():
            h_ref[...] = _rms(x_ref[...], gpre_ref[...]).astype(BF16)
            acc_ref[...] = jnp.zeros_like(acc_ref)

        u = jnp.dot(h_ref[...], wup_ref[...], preferred_element_type=F32).astype(BF16)
        u_ref[...] = u
        a = jnp.square(jnp.maximum(u, 0))
        a_ref[...] = a
        acc_ref[...] += jnp.dot(a, wdown_ref[...], preferred_element_type=F32)

        @pl.when(j == g - 1)
        def _():
            y = acc_ref[...]
            y_ref[...] = y
            xo = x_ref[...] + _rms(y, gpost_ref[...])
            if with_loss:
                err = xo - t_ref[...]
                xo_ref[...] = err * (1.0 / d)
                part = 0.5 * jnp.sum(jnp.mean(err * err, axis=-1, keepdims=True), axis=0, keepdims=True)
                _acc_rows(loss_ref, i == 0, [jnp.broadcast_to(part, (1, 128))])
            else:
                xo_ref[...] = xo

    row = lambda i, j: (i, 0)
    const = lambda i, j: (0, 0)
    in_specs = [pl.BlockSpec((tm, d), row), pl.BlockSpec((1, d), const),
                pl.BlockSpec((None, d, wc), lambda i, j: (j, 0, 0)),
                pl.BlockSpec((wc, d), lambda i, j: (j, 0)), pl.BlockSpec((1, d), const)]
    out_specs = [pl.BlockSpec((tm, d), row), pl.BlockSpec((tm, d), row), pl.BlockSpec((tm, d), row),
                 pl.BlockSpec((tm, wc), lambda i, j: (i, j)), pl.BlockSpec((tm, wc), lambda i, j: (i, j))]
    out_shape = [jax.ShapeDtypeStruct((n, d), F32), jax.ShapeDtypeStruct((n, d), F32),
                 jax.ShapeDtypeStruct((n, d), BF16), jax.ShapeDtypeStruct((n, g * wc), BF16),
                 jax.ShapeDtypeStruct((n, g * wc), BF16)]
    args = [x_in, g_pre, w_up, w_down, g_post]
    if with_loss:
        in_specs.append(pl.BlockSpec((tm, d), row))
        out_specs.append(pl.BlockSpec((8, 128), const))
        out_shape.append(jax.ShapeDtypeStruct((8, 128), F32))
        args.append(target)
    return pl.pallas_call(
        body, name=name, grid=(n // tm, g), in_specs=in_specs, out_specs=out_specs, out_shape=out_shape,
        scratch_shapes=[pltpu.VMEM((tm, d), F32)],
        compiler_params=_params("arbitrary", "arbitrary"),
    )(*args)


def _mlp_bwd(dxo, y, g_post, w_down, u, w_up, x_in, g_pre, name, tm=1024):
    n, d = dxo.shape
    g, _, wc = w_up.shape

    def body(dxo_ref, y_ref, gpost_ref, wdown_ref, u_ref, wup_ref, x_ref, gpre_ref,
             dx_ref, dy_ref, du_ref, dgain_ref, acc_ref, dgpost_ref):
        i, j = pl.program_id(0), pl.program_id(1)

        @pl.when(j == 0)
        def _():
            dy, dgpost = _rms_bwd(y_ref[...], gpost_ref[...], dxo_ref[...])
            dy_ref[...] = dy.astype(BF16)
            dgpost_ref[...] = dgpost
            acc_ref[...] = jnp.zeros_like(acc_ref)

        da = _dot(dy_ref[...], wdown_ref[...], tb=True)
        du = (da * (2.0 * jnp.maximum(u_ref[...], 0).astype(F32))).astype(BF16)
        du_ref[...] = du
        acc_ref[...] += _dot(du, wup_ref[...], tb=True)

        @pl.when(j == g - 1)
        def _():
            dx, dgpre = _rms_bwd(x_ref[...], gpre_ref[...], acc_ref[...])
            dx_ref[...] = dxo_ref[...] + dx
            _acc_rows(dgain_ref, i == 0, [dgpre, dgpost_ref[...]])

    row = lambda i, j: (i, 0)
    const = lambda i, j: (0, 0)
    return pl.pallas_call(
        body, name=name, grid=(n // tm, g),
        in_specs=[pl.BlockSpec((tm, d), row), pl.BlockSpec((tm, d), row), pl.BlockSpec((1, d), const),
                  pl.BlockSpec((wc, d), lambda i, j: (j, 0)), pl.BlockSpec((tm, wc), lambda i, j: (i, j)),
                  pl.BlockSpec((None, d, wc), lambda i, j: (j, 0, 0)), pl.BlockSpec((tm, d), row),
                  pl.BlockSpec((1, d), const)],
        out_specs=[pl.BlockSpec((tm, d), row), pl.BlockSpec((tm, d), row),
                   pl.BlockSpec((tm, wc), lambda i, j: (i, j)), pl.BlockSpec((8, d), const)],
        out_shape=[jax.ShapeDtypeStruct((n, d), F32), jax.ShapeDtypeStruct((n, d), BF16),
                   jax.ShapeDtypeStruct((n, g * wc), BF16), jax.ShapeDtypeStruct((8, d), F32)],
        scratch_shapes=[pltpu.VMEM((tm, d), F32), pltpu.VMEM((1, d), F32)],
        compiler_params=_params("arbitrary", "arbitrary"),
    )(dxo, y, g_post, w_down, u, w_up, x_in, g_pre)


QKV = 3 * D_MODEL


def _shifted(x, prev8, s):
    if s == 0:
        return x
    tm = x.shape[0]
    rolled = pltpu.roll(x, s, 0)
    head = pltpu.roll(prev8, s, 0)
    head = jnp.concatenate([head, jnp.zeros((tm - 8, x.shape[1]), x.dtype)], axis=0)
    rows = lax.broadcasted_iota(jnp.int32, x.shape, 0)
    return jnp.where(rows < s, head, rolled)


def _conv(x, prev8, conv_w):
    out = x * conv_w[3:4, :]
    for s in range(1, CONV_WIDTH):
        out = out + _shifted(x, prev8, s) * conv_w[3 - s:4 - s, :]
    return out


def _l2norm(x):
    return x * lax.rsqrt(jnp.sum(x * x, axis=-1, keepdims=True) + EPS)


def _gdn_act_head(cq, ck, cv):
    return _l2norm(_silu(cq)) * (HEAD_DIM ** -0.5), _l2norm(_silu(ck)), _silu(cv)


def _gdn_gates(ba, alog, dtb):
    lane = lax.broadcasted_iota(jnp.int32, ba.shape, 1)
    beta = 1.0 / (1.0 + jnp.exp(-ba))
    t = ba + dtb
    softplus = jnp.maximum(t, 0.0) + jnp.log(1.0 + jnp.exp(-jnp.abs(t)))
    g = -jnp.exp(alog) * softplus
    return jnp.where(lane < N_HEADS, beta, jnp.where(lane < 2 * N_HEADS, g, 0.0))


def _qkv_cols(part, h):
    start = part * D_MODEL + h * HEAD_DIM
    return slice(start, start + HEAD_DIM)


def _prev_rows_spec(tm, cols, colblock, order):
    per = tm // 8
    return pl.BlockSpec((8, cols), lambda i: (jnp.maximum(order(i) * per - 1, 0), colblock))


def _gdn_pre(proj, conv_w, alog, dtb, name, tm=256):
    n = proj.shape[0]

    def body(x_ref, prev_ref, ba_ref, cw_ref, alog_ref, dtb_ref, q_ref, k_ref, v_ref, bg_ref):
        first = pl.program_id(0) == 0
        for h in range(N_HEADS):
            convs = []
            for part in range(3):
                sl = _qkv_cols(part, h)
                prev8 = jnp.where(first, 0.0, prev_ref[:, sl])
                convs.append(_conv(x_ref[:, sl], prev8, cw_ref[:, sl]))
            out = slice(h * HEAD_DIM, (h + 1) * HEAD_DIM)
            q_ref[:, out], k_ref[:, out], v_ref[:, out] = _gdn_act_head(*convs)
        bg_ref[...] = _gdn_gates(ba_ref[...], alog_ref[...], dtb_ref[...])

    row = lambda i: (i, 0)
    const = lambda i: (0, 0)
    ident = lambda i: i
    return pl.pallas_call(
        body, name=name, grid=(n // tm,),
        in_specs=[pl.BlockSpec((tm, QKV), row), _prev_rows_spec(tm, QKV, 0, ident),
                  pl.BlockSpec((tm, 128), lambda i: (i, 4 * D_MODEL // 128)),
                  pl.BlockSpec((CONV_WIDTH, QKV), const), pl.BlockSpec((1, 128), const), pl.BlockSpec((1, 128), const)],
        out_specs=[pl.BlockSpec((tm, D_MODEL), row)] * 3 + [pl.BlockSpec((tm, 128), row)],
        out_shape=[jax.ShapeDtypeStruct((n, D_MODEL), F32)] * 3 + [jax.ShapeDtypeStruct((n, 128), F32)],
        compiler_params=_params("parallel"),
    )(proj, proj, proj, conv_w, alog, dtb)


def _gdn_pre_bwd(proj, conv_w, alog, dtb, dq, dk, dv, dbg, dgate, name, tm=256):
    n = proj.shape[0]
    nt = n // tm

    def body(x_ref, prev_ref, ba_ref, cw_ref, alog_ref, dtb_ref, dq_ref, dk_ref, dv_ref, dbg_ref, dgate_ref,
             dproj_ref, dcw_ref, dgates_ref, carry_ref):
        step = pl.program_id(0)
        tile = nt - 1 - step
        @pl.when(step == 0)
        def _():
            carry_ref[...] = jnp.zeros_like(carry_ref)
            dcw_ref[...] = jnp.zeros_like(dcw_ref)

        rows = lax.broadcasted_iota(jnp.int32, (tm, HEAD_DIM), 0)
        for h in range(N_HEADS):
            cols = [_qkv_cols(part, h) for part in range(3)]
            prevs = [jnp.where(tile == 0, 0.0, prev_ref[:, sl]) for sl in cols]
            convs = [_conv(x_ref[:, sl], prev8, cw_ref[:, sl]) for sl, prev8 in zip(cols, prevs)]
            _, vjp = jax.vjp(_gdn_act_head, *convs)
            out = slice(h * HEAD_DIM, (h + 1) * HEAD_DIM)
            dcs = vjp((dq_ref[:, out], dk_ref[:, out], dv_ref[:, out]))
            for sl, prev8, dc in zip(cols, prevs, dcs):
                x = x_ref[:, sl]
                cw = cw_ref[:, sl]
                dx = dc * cw[3:4, :]
                wrapped = jnp.zeros((8, HEAD_DIM), F32)
                dcw_ref[3:4, sl] += jnp.sum(dc * x, axis=0, keepdims=True)
                for s in range(1, CONV_WIDTH):
                    r = pltpu.roll(dc, tm - s, 0) * cw[3 - s:4 - s, :]
                    dx = dx + jnp.where(rows < tm - s, r, 0.0)
                    wrapped = wrapped + jnp.where(rows[tm - 8:, :] >= tm - s, r[tm - 8:, :], 0.0)
                    dcw_ref[3 - s:4 - s, sl] += jnp.sum(dc * _shifted(x, prev8, s), axis=0, keepdims=True)
                dproj_ref[:, sl] = dx
                dproj_ref[tm - 8:, sl] += carry_ref[:, sl]
                carry_ref[:, sl] = wrapped

        _, vjp = jax.vjp(_gdn_gates, ba_ref[...], alog_ref[...], dtb_ref[...])
        dba, dalog, ddtb = vjp(dbg_ref[...])
        dproj_ref[:, QKV:4 * D_MODEL] = dgate_ref[...]
        dproj_ref[:, 4 * D_MODEL:] = dba
        _acc_rows(dgates_ref, step == 0, [dalog, ddtb])

    rev = lambda i: nt - 1 - i
    row = lambda i: (rev(i), 0)
    const = lambda i: (0, 0)
    return pl.pallas_call(
        body, name=name, grid=(nt,),
        in_specs=[pl.BlockSpec((tm, QKV), row), _prev_rows_spec(tm, QKV, 0, rev),
                  pl.BlockSpec((tm, 128), lambda i: (rev(i), 4 * D_MODEL // 128)),
                  pl.BlockSpec((CONV_WIDTH, QKV), const), pl.BlockSpec((1, 128), const), pl.BlockSpec((1, 128), const),
                  pl.BlockSpec((tm, D_MODEL), row), pl.BlockSpec((tm, D_MODEL), row), pl.BlockSpec((tm, D_MODEL), row),
                  pl.BlockSpec((tm, 128), row), pl.BlockSpec((tm, D_MODEL), row)],
        out_specs=[pl.BlockSpec((tm, GDN_IN_PAD), row), pl.BlockSpec((8, QKV), const), pl.BlockSpec((8, 128), const)],
        out_shape=[jax.ShapeDtypeStruct((n, GDN_IN_PAD), F32), jax.ShapeDtypeStruct((8, QKV), F32),
                   jax.ShapeDtypeStruct((8, 128), F32)],
        scratch_shapes=[pltpu.VMEM((8, QKV), F32)],
        compiler_params=_params("arbitrary"),
    )(proj, proj, proj, conv_w, alog, dtb, dq, dk, dv, dbg, dgate)


@jax.custom_vjp
def _unit_lower_inverses(lowers):
    n = lowers[0].shape[0]
    eye = (lax.broadcasted_iota(jnp.int32, (n, n), 0) == lax.broadcasted_iota(jnp.int32, (n, n), 1)).astype(F32)
    ys = [-low for low in lowers]
    ps = [eye + y for y in ys]
    for _ in range(int(math.log2(CHUNK)) - 1):
        ys = [_dot_3x(y, y) for y in ys]
        ps = [p + _dot_3x(p, y) for p, y in zip(ps, ys)]
    return tuple(ps)


def _unit_lower_inverses_fwd(lowers):
    ts = _unit_lower_inverses(lowers)
    return ts, ts


def _unit_lower_inverses_bwd(ts, dts):
    left = [_dot_3x(t, dt, ta=True) for t, dt in zip(ts, dts)]
    return (tuple(-_dot_3x(l, t, tb=True) for l, t in zip(left, ts)),)


_unit_lower_inverses.defvjp(_unit_lower_inverses_fwd, _unit_lower_inverses_bwd)


@jax.custom_vjp
def _known_inverses(lowers, inverses):
    return inverses


def _known_inverses_bwd(ts, dts):
    (d_lowers,) = _unit_lower_inverses_bwd(ts, dts)
    return d_lowers, tuple(jnp.zeros_like(t) for t in ts)


_known_inverses.defvjp(lambda lowers, inverses: (inverses, inverses), _known_inverses_bwd)


GDN_STEP_CHUNKS = 2


def _gdn_chunks(qs, ks, vs, bgs, states, inverses=None, keep_inverses=False):
    c = CHUNK
    heads = range(N_HEADS)
    items = [(j, h) for j in range(len(bgs)) for h in heads]
    row = lax.broadcasted_iota(jnp.int32, (c, c), 0)
    col = lax.broadcasted_iota(jnp.int32, (c, c), 1)
    incl, strict, eye = row >= col, row > col, row == col
    lane = lax.broadcasted_iota(jnp.int32, (c, 128), 1)
    rowc = lax.broadcasted_iota(jnp.int32, (c, 1), 0)
    gc_all = [_mm_f32(incl.astype(F32), bg, False, False) for bg in bgs]
    q, k, v = ([xs[j][h] for j, h in items] for xs in (qs, ks, vs))
    n = range(len(items))
    beta = [jnp.sum(jnp.where(lane == h, bgs[j], 0.0), axis=1, keepdims=True) for j, h in items]
    gc = [jnp.sum(jnp.where(lane == N_HEADS + h, gc_all[j], 0.0), axis=1, keepdims=True) for j, h in items]
    gc_row = [jnp.sum(jnp.where(eye, gc[i], 0.0), axis=0, keepdims=True) for i in n]
    gc_last = [jnp.sum(jnp.where(rowc == c - 1, gc[i], 0.0), axis=0, keepdims=True) for i in n]
    decay = [jnp.where(incl, jnp.exp(jnp.where(incl, gc[i] - gc_row[i], 0.0)), 0.0) for i in n]
    kb = [k[i] * beta[i] for i in n]
    lower = tuple(jnp.where(strict, _mm(kb[i], k[i], False, True) * decay[i], 0.0) for i in n)
    attn = [_mm(q[i], k[i], False, True) * decay[i] for i in n]
    t_mat = _unit_lower_inverses(lower) if inverses is None else _known_inverses(lower, inverses)
    egc = [jnp.exp(gc[i]) for i in n]
    w = [_mm(t_mat[i], kb[i] * egc[i], False, False) for i in n]
    u = [_mm(t_mat[i], v[i] * beta[i], False, False) for i in n]
    qg = [q[i] * egc[i] for i in n]
    kg = [k[i] * jnp.exp(gc_last[i] - gc[i]) for i in n]
    outs, cur = [], list(states)
    for j in range(len(bgs)):
        at = lambda h: j * N_HEADS + h
        v_new = [u[at(h)] - _mm(w[at(h)], cur[h], False, False) for h in heads]
        from_state = [_mm(qg[at(h)], cur[h], False, False) for h in heads]
        outs.append(tuple(from_state[h] + _mm(attn[at(h)], v_new[h], False, False) for h in heads))
        cur = [cur[h] * jnp.exp(gc_last[at(h)]) + _mm(kg[at(h)], v_new[h], True, False) for h in heads]
    return (tuple(outs), tuple(cur), t_mat) if keep_inverses else (tuple(outs), tuple(cur))


def _chunk_head_slices(ref, nch):
    return tuple(tuple(ref[j * CHUNK:(j + 1) * CHUNK, h * HEAD_DIM:(h + 1) * HEAD_DIM] for h in range(N_HEADS))
                 for j in range(nch))


def _store_chunk_heads(ref, values):
    for j, chunk in enumerate(values):
        for h, val in enumerate(chunk):
            ref[j * CHUNK:(j + 1) * CHUNK, h * HEAD_DIM:(h + 1) * HEAD_DIM] = val


def _gdn_scan(qn, kn, v, bg, name, ride=None):
    n = qn.shape[0]
    nch = GDN_STEP_CHUNKS
    rows = nch * CHUNK
    nc = n // rows

    def body(q_ref, k_ref, v_ref, bg_ref, o_ref, saved_ref, inv_ref, state_ref):
        @pl.when(pl.program_id(0) == 0)
        def _():
            state_ref[...] = jnp.zeros_like(state_ref)

        states = tuple(state_ref[h] for h in range(N_HEADS))
        for h in range(N_HEADS):
            saved_ref[h] = states[h]
        bgs = tuple(bg_ref[j * CHUNK:(j + 1) * CHUNK, :] for j in range(nch))
        outs, new_states, inverses = _gdn_chunks(_chunk_head_slices(q_ref, nch), _chunk_head_slices(k_ref, nch),
                                                 _chunk_head_slices(v_ref, nch), bgs, states, keep_inverses=True)
        _store_chunk_heads(o_ref, outs)
        for h in range(N_HEADS):
            state_ref[h] = new_states[h]
        for item, inverse in enumerate(inverses):
            inv_ref[item] = inverse

    row = lambda i: (i, 0)
    return _call(
        body, [qn, kn, v, bg], name=name, grid=(nc,),
        in_specs=[pl.BlockSpec((rows, D_MODEL), row)] * 3 + [pl.BlockSpec((rows, 128), row)],
        out_specs=[pl.BlockSpec((rows, D_MODEL), row),
                   pl.BlockSpec((None, N_HEADS, HEAD_DIM, HEAD_DIM), lambda i: (i, 0, 0, 0)),
                   pl.BlockSpec((None, nch * N_HEADS, CHUNK, CHUNK), lambda i: (i, 0, 0, 0))],
        out_shape=[jax.ShapeDtypeStruct((n, D_MODEL), F32),
                   jax.ShapeDtypeStruct((nc, N_HEADS, HEAD_DIM, HEAD_DIM), F32),
                   jax.ShapeDtypeStruct((nc, nch * N_HEADS, CHUNK, CHUNK), F32)],
        scratch_shapes=[pltpu.VMEM((N_HEADS, HEAD_DIM, HEAD_DIM), F32)],
        compiler_params=_params("arbitrary"),
        ride=ride, first=lambda: pl.program_id(0) == 0, last=lambda: pl.program_id(0) == nc - 1)


def _gdn_scan_bwd(qn, kn, v, bg, saved, inverses, do, name, ride=None):
    n = qn.shape[0]
    nch = GDN_STEP_CHUNKS
    rows = nch * CHUNK
    nc = n // rows

    def body(q_ref, k_ref, v_ref, bg_ref, saved_ref, inv_ref, do_ref, dq_ref, dk_ref, dv_ref, dbg_ref, dstate_ref):
        @pl.when(pl.program_id(0) == 0)
        def _():
            dstate_ref[...] = jnp.zeros_like(dstate_ref)

        states = tuple(saved_ref[h] for h in range(N_HEADS))
        bgs = tuple(bg_ref[j * CHUNK:(j + 1) * CHUNK, :] for j in range(nch))
        known = tuple(inv_ref[item] for item in range(nch * N_HEADS))
        _, vjp = jax.vjp(functools.partial(_gdn_chunks, inverses=known), _chunk_head_slices(q_ref, nch),
                         _chunk_head_slices(k_ref, nch), _chunk_head_slices(v_ref, nch), bgs, states)
        dstates = tuple(dstate_ref[h] for h in range(N_HEADS))
        dqs, dks, dvs, dbgs, dprev = vjp((_chunk_head_slices(do_ref, nch), dstates))
        _store_chunk_heads(dq_ref, dqs)
        _store_chunk_heads(dk_ref, dks)
        _store_chunk_heads(dv_ref, dvs)
        for h in range(N_HEADS):
            dstate_ref[h] = dprev[h]
        for j in range(nch):
            dbg_ref[j * CHUNK:(j + 1) * CHUNK, :] = dbgs[j]

    row = lambda i: (nc - 1 - i, 0)
    return _call(
        body, [qn, kn, v, bg, saved, inverses, do], name=name, grid=(nc,),
        in_specs=[pl.BlockSpec((rows, D_MODEL), row)] * 3 + [pl.BlockSpec((rows, 128), row),
                  pl.BlockSpec((None, N_HEADS, HEAD_DIM, HEAD_DIM), lambda i: (nc - 1 - i, 0, 0, 0)),
                  pl.BlockSpec((None, nch * N_HEADS, CHUNK, CHUNK), lambda i: (nc - 1 - i, 0, 0, 0)),
                  pl.BlockSpec((rows, D_MODEL), row)],
        out_specs=[pl.BlockSpec((rows, D_MODEL), row)] * 3 + [pl.BlockSpec((rows, 128), row)],
        out_shape=[jax.ShapeDtypeStruct((n, D_MODEL), F32)] * 3 + [jax.ShapeDtypeStruct((n, 128), F32)],
        scratch_shapes=[pltpu.VMEM((N_HEADS, HEAD_DIM, HEAD_DIM), F32)],
        compiler_params=_params("arbitrary"),
        ride=ride, first=lambda: pl.program_id(0) == 0, last=lambda: pl.program_id(0) == nc - 1)


SB_BQ = 512
SB_BK = 256
SB_SUB = 128
SB_ROWS = 128
SB_SCALE = HEAD_DIM ** -0.5
SB_DEAD = -105.0


def _sb_terms(z, before):
    e = jnp.exp(-jnp.abs(z))
    log_beta = jnp.minimum(z, 0.0) - jnp.log(1.0 + e)
    log_1m = log_beta - z
    if before is not None:
        log_1m = jnp.where(before, log_1m, 0.0)
    return e, log_beta, log_1m


def _tri_ones(n, cmp):
    r = lax.broadcasted_iota(jnp.int32, (n, 2 * n), 0)
    c = lax.broadcasted_iota(jnp.int32, (n, 2 * n), 1)
    return jnp.where((c >= n) | cmp(r, c), 1.0, 0.0).astype(BF16)


def _sums(x, tri_ones):
    both = jnp.dot(x.astype(BF16), tri_ones, preferred_element_type=F32)
    n = x.shape[1]
    return both[:, :n], both[:, n:]


def _sb_chunk_mask(diagonal, r, s):
    if not diagonal or s * SB_SUB + SB_SUB - 1 < r * SB_ROWS:
        return None
    if s * SB_SUB >= r * SB_ROWS + SB_ROWS - 1:
        return "empty"
    rows = r * SB_ROWS + lax.broadcasted_iota(jnp.int32, (SB_ROWS, SB_SUB), 0)
    cols = s * SB_SUB + lax.broadcasted_iota(jnp.int32, (SB_ROWS, SB_SUB), 1)
    return cols < rows


def _sb_attention(q, kv, name, bq=SB_BQ, ride=None):
    n = q.shape[0]
    bq = min(bq, n)
    bk = min(SB_BK, bq)
    nrc = bq // SB_ROWS

    def body(q_ref, k_ref, v_ref, o_ref, l_ref, n_ref, z_scr, a_scr):
        i = pl.program_id(1)
        qb = q_ref[...]
        after = _tri_ones(SB_SUB, lambda r, c: r > c)

        def span(start, kw, c_sum, acc, diagonal):
            start = pl.multiple_of(start, kw)
            k_w = k_ref[pl.ds(start, kw), :]
            v_w = v_ref[pl.ds(start, kw), :]
            z_scr[:, :kw] = lax.dot_general(qb, k_w, _dims(False, True), preferred_element_type=F32)
            c_rows = [c_sum[r * SB_ROWS:(r + 1) * SB_ROWS] for r in range(nrc)]
            for s in reversed(range(kw // SB_SUB)):
                cols = slice(s * SB_SUB, (s + 1) * SB_SUB)
                for r in range(nrc):
                    rows = slice(r * SB_ROWS, (r + 1) * SB_ROWS)
                    before = _sb_chunk_mask(diagonal, r, s)
                    if isinstance(before, str):
                        a_scr[rows, cols] = jnp.zeros((SB_ROWS, SB_SUB), BF16)
                        continue
                    _, log_beta, log_1m = _sb_terms(z_scr[rows, cols] * SB_SCALE, before)
                    tail, total = _sums(log_1m, after)
                    a = jnp.exp(log_beta + c_rows[r] + tail)
                    if before is not None:
                        a = jnp.where(before, a, 0.0)
                    a_scr[rows, cols] = a.astype(BF16)
                    c_rows[r] = c_rows[r] + total
            acc = acc + jnp.dot(a_scr[:, :kw], v_w, preferred_element_type=F32)
            return jnp.concatenate(c_rows, axis=0), acc

        c_sum, acc = span(i * bq, bq, jnp.zeros((bq, SB_SUB), F32), jnp.zeros((bq, HEAD_DIM), F32), True)

        def more(state):
            done, c, _ = state
            return (done < i * (bq // bk)) & (jnp.max(c) > SB_DEAD)

        def step(state):
            done, c, a = state
            c, a = span(i * bq - (done + 1) * bk, bk, c, a, False)
            return done + 1, c, a

        done, c_sum, acc = lax.while_loop(more, step, (jnp.int32(0), c_sum, acc))
        o_ref[...] = acc.astype(BF16)
        l_ref[...] = c_sum
        n_ref[...] = jnp.full((8, 128), done.astype(F32), F32)

    nq = n // bq
    return _call(
        body, [q, kv, kv], name=name, grid=(N_HEADS, nq),
        in_specs=[pl.BlockSpec((bq, HEAD_DIM), lambda h, i: (i, h)),
                  pl.BlockSpec((n, HEAD_DIM), lambda h, i: (0, h)),
                  pl.BlockSpec((n, HEAD_DIM), lambda h, i: (0, N_HEADS + h))],
        out_specs=[pl.BlockSpec((bq, HEAD_DIM), lambda h, i: (i, h)),
                   pl.BlockSpec((None, bq, 128), lambda h, i: (h, i, 0)),
                   pl.BlockSpec((None, None, 8, 128), lambda h, i: (h, i, 0, 0))],
        out_shape=[jax.ShapeDtypeStruct((n, D_MODEL), BF16), jax.ShapeDtypeStruct((N_HEADS, n, 128), F32),
                   jax.ShapeDtypeStruct((N_HEADS, nq, 8, 128), F32)],
        scratch_shapes=[pltpu.VMEM((bq, bq), F32), pltpu.VMEM((bq, bq), BF16)],
        compiler_params=_params("arbitrary", "arbitrary"),
        ride=ride, first=lambda: (pl.program_id(0) == 0) & (pl.program_id(1) == 0),
        last=lambda: (pl.program_id(0) == N_HEADS - 1) & (pl.program_id(1) == nq - 1))


def _sb_attention_bwd(q, kv, do, lsum, spans, name, bq=SB_BQ, ride=None):
    n = q.shape[0]
    bq = min(bq, n)
    bk = min(SB_BK, bq)
    nrc = bq // SB_ROWS

    def body(q_ref, k_ref, v_ref, do_ref, l_ref, n_ref, dq_ref, dk_ref, dv_ref, z_scr, da_scr, a_scr, dz_scr):
        i = pl.program_id(1)

        @pl.when(i == 0)
        def _():
            dk_ref[...] = jnp.zeros_like(dk_ref)
            dv_ref[...] = jnp.zeros_like(dv_ref)

        qb = q_ref[...]
        dob = do_ref[...]
        lt_rows = [l_ref[r * SB_ROWS:(r + 1) * SB_ROWS, :] for r in range(nrc)]
        upto = _tri_ones(SB_SUB, lambda r, c: r <= c)
        below = _tri_ones(SB_SUB, lambda r, c: r < c)

        def span(start, kw, l_sum, g_sum, dq, diagonal):
            start = pl.multiple_of(start, kw)
            k_w = k_ref[pl.ds(start, kw), :]
            v_w = v_ref[pl.ds(start, kw), :]
            z_scr[:, :kw] = lax.dot_general(qb, k_w, _dims(False, True), preferred_element_type=F32)
            da_scr[:, :kw] = lax.dot_general(dob, v_w, _dims(False, True), preferred_element_type=F32)
            l_rows = [l_sum[r * SB_ROWS:(r + 1) * SB_ROWS] for r in range(nrc)]
            g_rows = [g_sum[r * SB_ROWS:(r + 1) * SB_ROWS] for r in range(nrc)]
            for s in range(kw // SB_SUB):
                cols = slice(s * SB_SUB, (s + 1) * SB_SUB)
                for r in range(nrc):
                    rows = slice(r * SB_ROWS, (r + 1) * SB_ROWS)
                    before = _sb_chunk_mask(diagonal, r, s)
                    if isinstance(before, str):
                        a_scr[rows, cols] = jnp.zeros((SB_ROWS, SB_SUB), BF16)
                        dz_scr[rows, cols] = jnp.zeros((SB_ROWS, SB_SUB), BF16)
                        continue
                    zs = z_scr[rows, cols] * SB_SCALE
                    e, log_beta, log_1m = _sb_terms(zs, before)
                    l_prefix, l_total = _sums(log_1m, upto)
                    a = jnp.exp(log_beta + (lt_rows[r] - (l_rows[r] + l_prefix)))
                    if before is not None:
                        a = jnp.where(before, a, 0.0)
                    g = a * da_scr[rows, cols]
                    g_prefix, g_total = _sums(g, below)
                    inv = 1.0 / (1.0 + e)
                    beta = jnp.where(zs >= 0, inv, e * inv)
                    dz = (g - (g + g_rows[r] + g_prefix) * beta) * SB_SCALE
                    if before is not None:
                        dz = jnp.where(before, dz, 0.0)
                    a_scr[rows, cols] = a.astype(BF16)
                    dz_scr[rows, cols] = dz.astype(BF16)
                    l_rows[r] = l_rows[r] + l_total
                    g_rows[r] = g_rows[r] + g_total
            dz_w = dz_scr[:, :kw]
            dq = dq + jnp.dot(dz_w, k_w, preferred_element_type=F32)
            dk_ref[pl.ds(start, kw), :] += lax.dot_general(dz_w, qb, _dims(True, False), preferred_element_type=F32)
            dv_ref[pl.ds(start, kw), :] += lax.dot_general(a_scr[:, :kw], dob, _dims(True, False),
                                                           preferred_element_type=F32)
            return jnp.concatenate(l_rows, axis=0), jnp.concatenate(g_rows, axis=0), dq

        taken = jnp.clip(jnp.max(n_ref[...]).astype(jnp.int32), 0, i * (bq // bk))
        zero = jnp.zeros((bq, SB_SUB), F32)
        carry = lax.fori_loop(0, taken, lambda j, c: span(i * bq - (taken - j) * bk, bk, c[0], c[1], c[2], False),
                              (zero, zero, jnp.zeros((bq, HEAD_DIM), F32)))
        _, _, dq = span(i * bq, bq, carry[0], carry[1], carry[2], True)
        dq_ref[...] = dq.astype(BF16)

    nq = n // bq
    return _call(
        body, [q, kv, kv, do, lsum, spans], name=name, grid=(N_HEADS, nq),
        in_specs=[pl.BlockSpec((bq, HEAD_DIM), lambda h, i: (i, h)),
                  pl.BlockSpec((n, HEAD_DIM), lambda h, i: (0, h)),
                  pl.BlockSpec((n, HEAD_DIM), lambda h, i: (0, N_HEADS + h)),
                  pl.BlockSpec((bq, HEAD_DIM), lambda h, i: (i, h)),
                  pl.BlockSpec((None, bq, 128), lambda h, i: (h, i, 0)),
                  pl.BlockSpec((None, None, 8, 128), lambda h, i: (h, i, 0, 0))],
        out_specs=[pl.BlockSpec((bq, HEAD_DIM), lambda h, i: (i, h)),
                   pl.BlockSpec((n, HEAD_DIM), lambda h, i: (0, h)),
                   pl.BlockSpec((n, HEAD_DIM), lambda h, i: (0, h))],
        out_shape=[jax.ShapeDtypeStruct((n, D_MODEL), BF16), jax.ShapeDtypeStruct((n, D_MODEL), F32),
                   jax.ShapeDtypeStruct((n, D_MODEL), F32)],
        scratch_shapes=[pltpu.VMEM((bq, bq), F32), pltpu.VMEM((bq, bq), F32), pltpu.VMEM((bq, bq), BF16),
                        pltpu.VMEM((bq, bq), BF16)],
        compiler_params=_params("arbitrary", "arbitrary"),
        ride=ride, first=lambda: (pl.program_id(0) == 0) & (pl.program_id(1) == 0),
        last=lambda: (pl.program_id(0) == N_HEADS - 1) & (pl.program_id(1) == nq - 1))


def _local_step(x, target, gains, comm):
    g = gains
    w = comm.w
    big = {}
    row = lambda a, i: a[i:i + 1, :]

    proj, h0 = _norm_matmul(x, row(g["mix_pre"], 0), w["gdn_in"], F32, "gdn_in_proj", tn=1408)
    qn, kn, v, bg = _gdn_pre(proj, w["conv"], g["alog"], g["dtb"], "gdn_pre")
    (o_gdn, saved, inverses), got = _gdn_scan(qn, kn, v, bg, "gdn_scan", ride=comm.ride("scan"))
    comm.done("scan", got)
    x1, y_mix0, a_gdn = _out_proj((o_gdn, proj), True, x, w["gdn_out"], row(g["mix_post"], 0), g["out_gain"],
                                  "gdn_out_proj", tm=512)
    x2, y_mlp0, h_mlp0, u0, a0 = _mlp_fwd(x1, row(g["mlp_pre"], 0), w["up0"], w["down0"], row(g["mlp_post"], 0), "mlp0")
    kv, h_kv = _norm_matmul(x2, g["kv"], w["kv"], BF16, "kv_proj", tm=2048)
    q, h_q = _norm_matmul(x2, row(g["mix_pre"], 1), w["sb_q"], BF16, "sb_q_proj")
    (o_sb, lsum, spans), got = _sb_attention(q, kv, "sb_attention", ride=comm.ride("sb"))
    comm.done("sb", got)
    x3, y_mix1 = _out_proj((o_sb,), False, x2, w["sb_o"], row(g["mix_post"], 1), None, "sb_out_proj")
    dx4, y_mlp1, h_mlp1, u1, a1, loss = _mlp_fwd(x3, row(g["mlp_pre"], 1), w["up1"], w["down1"], row(g["mlp_post"], 1),
                                                 "mlp1_loss", target=target)

    dx3, dy_mlp1, du1, dg_mlp1 = _mlp_bwd(dx4, y_mlp1, row(g["mlp_post"], 1), w["down1"], u1, w["up1"], x3,
                                          row(g["mlp_pre"], 1), "mlp1_bwd")
    big["down1"] = _matmul_tn(a1, dy_mlp1, 1, "d_down1")[0]
    big["up1"] = _matmul_tn(h_mlp1, du1, N_DEV, "d_up1", tk=2048)
    dy_mix1, dg_post1, do_sb = _out_proj_bwd(dx3, y_mix1, w["sb_o"], row(g["mix_post"], 1), None, "sb_out_proj_bwd")
    big["sb_o"] = _matmul_tn(o_sb, dy_mix1, 1, "d_sb_o")[0]
    (dq, dk, dv), got = _sb_attention_bwd(q, kv, do_sb, lsum, spans, "sb_attention_bwd", ride=comm.ride("sb_bwd", big))
    comm.done("sb_bwd", got)
    big["sb_q"] = _matmul_tn(h_q, dq, 1, "d_sb_q")[0]
    big["kv"] = jnp.concatenate([_matmul_tn(h_kv, dk, N_DEV // 2, "d_w_k", tk=2048),
                                 _matmul_tn(h_kv, dv, N_DEV // 2, "d_w_v", tk=2048)],
                                axis=0)
    dx2, dg_x2 = _norm_matmul_bwd([(dq, w["sb_q"], row(g["mix_pre"], 1), 1024), (dk, w["kv"][0:1], g["kv"], 1024),
                                   (dv, w["kv"][1:2], g["kv"], 1024)], x2, dx3, "qkv_proj_bwd", tm=512)

    dx1, dy_mlp0, du0, dg_mlp0 = _mlp_bwd(dx2, y_mlp0, row(g["mlp_post"], 0), w["down0"], u0, w["up0"], x1,
                                          row(g["mlp_pre"], 0), "mlp0_bwd")
    big["down0"] = _matmul_tn(a0, dy_mlp0, 1, "d_down0")[0]
    big["up0"] = _matmul_tn(h_mlp0, du0, N_DEV, "d_up0", tk=2048)
    dy_mix0, dg_post0, do_gdn, dgate, d_out_gain = _out_proj_bwd(
        dx1, y_mix0, w["gdn_out"], row(g["mix_post"], 0), (o_gdn, proj, g["out_gain"]), "gdn_out_proj_bwd", tm=512)
    big["gdn_out"] = _matmul_tn(a_gdn, dy_mix0, 1, "d_gdn_out")[0]
    (dqn, dkn, dvv, dbg), got = _gdn_scan_bwd(qn, kn, v, bg, saved, inverses, do_gdn, "gdn_scan_bwd",
                                              ride=comm.ride("scan_bwd", big))
    comm.done("scan_bwd", got)
    dproj, d_conv, d_gates = _gdn_pre_bwd(proj, w["conv"], g["alog"], g["dtb"], dqn, dkn, dvv, dbg, dgate, "gdn_pre_bwd")
    big["gdn_in"] = _matmul_tn(h0, dproj, 1, "d_gdn_in", tb=1408, tk=512)[0]
    ride = comm.ride("in_bwd", big)
    grad_x, dg_pre0, *got = _norm_matmul_bwd([(dproj, w["gdn_in"], row(g["mix_pre"], 0), 1408)], x, dx1, "gdn_in_proj_bwd",
                                             ride=ride)
    comm.done("in_bwd", got[0] if got else [])

    small = {"pre0": dg_pre0, "x2": dg_x2, "post0": dg_post0, "post1": dg_post1, "mlp0": dg_mlp0, "mlp1": dg_mlp1,
             "gates": d_gates, "out_gain": d_out_gain, "conv": d_conv, "loss": loss}
    return grad_x, big, small


class _LocalOnly:
    def __init__(self, weights):
        self.w = weights

    def ride(self, stage, grads=None):
        return None

    def done(self, stage, outs):
        pass


GATHER_STAGES = {"scan": ("gdn_out", "up0", "down0", "kv", "sb_q"), "sb": ("sb_o", "up1", "down1")}
SCATTER_STAGES = {"sb_bwd": ("up1", "down1", "sb_o"), "scan_bwd": ("sb_q", "kv", "up0", "down0", "gdn_out"),
                  "in_bwd": ("gdn_in",)}


def _whole_weight(name, gathered):
    d = D_MODEL
    if name in ("up0", "up1"):
        return gathered
    if name == "kv":
        return gathered.reshape(2, N_DEV // 2, d, gathered.shape[2]).transpose(0, 2, 1, 3).reshape(2, d, d)
    if name == "gdn_in":
        whole = gathered.transpose(1, 0, 2).reshape(d, GDN_IN_COLS)
        return jnp.pad(whole, ((0, 0), (0, GDN_IN_PAD - GDN_IN_COLS)))[None]
    if name == "conv":
        return gathered.transpose(1, 0, 2).reshape(CONV_WIDTH, QKV)
    whole = gathered.reshape(gathered.shape[0] * gathered.shape[1], d)
    return whole[None] if name == "sb_q" else whole


def _owner_blocks(name, grad):
    if name in ("up0", "up1", "kv"):
        blocks = grad
    elif name == "gdn_in":
        blocks = grad[:, :GDN_IN_COLS].reshape(D_MODEL, N_DEV, GDN_IN_COLS // N_DEV).transpose(1, 0, 2)
    else:
        blocks = grad.reshape(N_DEV, grad.shape[0] // N_DEV, grad.shape[1])
    return blocks.astype(BF16)


class _Fsdp:
    def __init__(self, shards, weights):
        self.shards, self.w, self.landed = shards, weights, {}

    def ride(self, stage, grads=None):
        if stage in GATHER_STAGES:
            names = GATHER_STAGES[stage]
            return _Exchange([self.shards[nm] for nm in names], [True] * len(names))
        names = SCATTER_STAGES[stage]
        return _Exchange([_owner_blocks(nm, grads[nm]) for nm in names], [False] * len(names))

    def done(self, stage, outs):
        if stage in GATHER_STAGES:
            for nm, out in zip(GATHER_STAGES[stage], outs):
                self.w[nm] = _whole_weight(nm, out)
        else:
            self.landed.update(zip(SCATTER_STAGES[stage], outs))


def _my_place():
    return lax.axis_index("x"), lax.axis_index("y"), lax.axis_index("c")


class _Exchange:
    def __init__(self, arrays, gather):
        self.arrays, self.gather, self.n = list(arrays), list(gather), len(arrays)
        any_spec = pl.BlockSpec(memory_space=pl.ANY)
        self.in_specs = [any_spec] * self.n
        self.out_specs = [any_spec] * self.n
        self.out_shape = [jax.ShapeDtypeStruct(((N_DEV,) + a.shape) if g else a.shape, a.dtype)
                          for a, g in zip(self.arrays, self.gather)]
        self.scratch = [pltpu.SemaphoreType.DMA((self.n, N_DEV - 1)), pltpu.SemaphoreType.DMA((self.n, N_DEV - 1)),
                        pltpu.SemaphoreType.DMA((self.n,))]

    def _copies(self, ins, outs, sems):
        send_sems, recv_sems, local_sems = sems
        x, y, c = _my_place()
        me = 4 * x + 2 * y + c
        copies = []
        for a in range(self.n):
            src = ins[a] if self.gather[a] else ins[a].at[me]
            copies.append(pltpu.make_async_copy(src, outs[a].at[me], local_sems.at[a]))
        for k in range(1, N_DEV):
            px = 1 - x if k & 4 else x
            py = 1 - y if k & 2 else y
            pc = 1 - c if k & 1 else c
            peer = 4 * px + 2 * py + pc
            for a in range(self.n):
                src = ins[a] if self.gather[a] else ins[a].at[peer]
                copies.append(pltpu.make_async_remote_copy(
                    src_ref=src, dst_ref=outs[a].at[me], send_sem=send_sems.at[a, k - 1], recv_sem=recv_sems.at[a, k - 1],
                    device_id=(px, py, pc), device_id_type=MESH))
        return copies

    def start(self, ins, outs, sems):
        for cp in self._copies(ins, outs, sems):
            cp.start()

    def wait(self, ins, outs, sems):
        for cp in self._copies(ins, outs, sems):
            cp.wait()


def _call(body, operands, *, name, out_shape, in_specs, out_specs, grid=(), scratch_shapes=(), compiler_params=None,
          ride=None, first=None, last=None):
    n_in, n_out, n_scr = len(operands), len(out_shape), len(scratch_shapes)
    if ride is None:
        outs = pl.pallas_call(body, name=name, grid=grid, in_specs=in_specs, out_specs=out_specs, out_shape=out_shape,
                              scratch_shapes=scratch_shapes, compiler_params=compiler_params)(*operands)
        return list(outs), []
    r = ride.n

    def riding(*refs):
        ins, r_ins = refs[:n_in], refs[n_in:n_in + r]
        outs, r_outs = refs[n_in + r:n_in + r + n_out], refs[n_in + r + n_out:n_in + 2 * r + n_out]
        scr, sems = refs[n_in + 2 * r + n_out:n_in + 2 * r + n_out + n_scr], refs[n_in + 2 * r + n_out + n_scr:]

        @pl.when(first())
        def _():
            ride.start(r_ins, r_outs, sems)

        body(*ins, *outs, *scr)

        @pl.when(last())
        def _():
            ride.wait(r_ins, r_outs, sems)

    outs = pl.pallas_call(
        riding, name=name, grid=grid, in_specs=list(in_specs) + ride.in_specs, out_specs=list(out_specs) + ride.out_specs,
        out_shape=list(out_shape) + ride.out_shape, scratch_shapes=list(scratch_shapes) + ride.scratch,
        compiler_params=compiler_params)(*operands, *ride.arrays)
    return list(outs[:n_out]), list(outs[n_out:])


def _exchange(arrays, gather, name):
    ride = _Exchange(arrays, gather)

    def body(*refs):
        ins, outs, sems = refs[:ride.n], refs[ride.n:2 * ride.n], refs[2 * ride.n:]
        ride.start(ins, outs, sems)
        ride.wait(ins, outs, sems)

    return pl.pallas_call(body, name=name, in_specs=ride.in_specs, out_specs=ride.out_specs, out_shape=ride.out_shape,
                          scratch_shapes=ride.scratch)(*arrays)


def _gather_two_level(arrays, name):
    n_arr = len(arrays)

    def body(*refs):
        ins, outs = refs[:n_arr], refs[n_arr:2 * n_arr]
        send_sems, recv_sems, local_sems = refs[2 * n_arr:]
        x, y, c = _my_place()
        sibling = (x, y, 1 - c)
        chips = [(1 - x, y), (x, 1 - y), (1 - x, 1 - y)]
        index = lambda px, py, pc: 4 * px + 2 * py + pc

        def copy(a, k, block, to, src=None):
            dst = outs[a].at[index(*block)]
            return pltpu.make_async_remote_copy(src_ref=dst if src is None else src, dst_ref=dst,
                                                send_sem=send_sems.at[a, k], recv_sem=recv_sems.at[a, k],
                                                device_id=to, device_id_type=MESH)

        mine = [pltpu.make_async_copy(ins[a], outs[a].at[index(x, y, c)], local_sems.at[a]) for a in range(n_arr)]
        first = [copy(a, 0, (x, y, c), sibling, src=ins[a]) for a in range(n_arr)]
        first += [copy(a, 1 + j, (x, y, c), (*chip, c), src=ins[a]) for j, chip in enumerate(chips) for a in range(n_arr)]
        for cp in mine + first:
            cp.start()
        passed = []
        for j, chip in enumerate(chips):
            for a in range(n_arr):
                copy(a, 1 + j, (*chip, c), (x, y, c)).wait_recv()
                passed.append(copy(a, 4 + j, (*chip, c), sibling))
                passed[-1].start()
        for a in range(n_arr):
            copy(a, 0, sibling, (x, y, c)).wait_recv()
            for j, chip in enumerate(chips):
                copy(a, 4 + j, (*chip, 1 - c), (x, y, c)).wait_recv()
        for cp in first + passed:
            cp.wait_send()
        for cp in mine:
            cp.wait()

    any_spec = pl.BlockSpec(memory_space=pl.ANY)
    return pl.pallas_call(
        body, name=name, in_specs=[any_spec] * n_arr, out_specs=[any_spec] * n_arr,
        out_shape=[jax.ShapeDtypeStruct((N_DEV,) + a.shape, a.dtype) for a in arrays],
        scratch_shapes=[pltpu.SemaphoreType.DMA((n_arr, N_DEV - 1)), pltpu.SemaphoreType.DMA((n_arr, N_DEV - 1)),
                        pltpu.SemaphoreType.DMA((n_arr,))])(*arrays)


def _adamw_math(g, w, m, v):
    m = ADAM_B1 * m + (1.0 - ADAM_B1) * g
    v = ADAM_B2 * v + (1.0 - ADAM_B2) * jnp.square(g)
    m_hat = m / (1.0 - ADAM_B1 ** ADAM_STEP)
    v_hat = v / (1.0 - ADAM_B2 ** ADAM_STEP)
    delta = -ADAM_LR * (m_hat / (jnp.sqrt(v_hat) + ADAM_EPS) + ADAM_WD * w)
    return delta, m, v


def _sum_devices(ref):
    total = ref[0].astype(F32)
    for d in range(1, N_DEV):
        total = total + ref[d].astype(F32)
    return total


def _reduce_adamw(landed, w, m, v, name, tr=256):
    r, c = w.shape
    tr = min(tr, r)
    assert r % tr == 0

    def body(l_ref, w_ref, m_ref, v_ref, g_ref, d_ref, nm_ref, nv_ref):
        g = _sum_devices(l_ref)
        g_ref[...] = g
        d_ref[...], nm_ref[...], nv_ref[...] = _adamw_math(g, w_ref[...], m_ref[...], v_ref[...])

    blk = pl.BlockSpec((tr, c), lambda i: (i, 0))
    return pl.pallas_call(
        body, name=name, grid=(r // tr,),
        in_specs=[pl.BlockSpec((N_DEV, tr, c), lambda i: (0, i, 0)), blk, blk, blk],
        out_specs=[blk] * 4, out_shape=[jax.ShapeDtypeStruct((r, c), F32)] * 4,
        compiler_params=_params("parallel"),
    )(landed, w, m, v)


def _adamw(g, w, m, v, name):
    def body(g_ref, w_ref, m_ref, v_ref, d_ref, nm_ref, nv_ref):
        d_ref[...], nm_ref[...], nv_ref[...] = _adamw_math(g_ref[...], w_ref[...], m_ref[...], v_ref[...])

    return pl.pallas_call(body, name=name, out_shape=[jax.ShapeDtypeStruct(w.shape, F32)] * 3)(g, w, m, v)


def _small_update(landed, params, name):
    layout = {
        "mix_pre": [("pre0", 0), ("x2", 0)], "mix_post": [("post0", 0), ("post1", 0)],
        "mlp_pre": [("mlp0", 0), ("mlp1", 0)], "mlp_post": [("mlp0", 1), ("mlp1", 1)],
        "kv": [("kv", 0)], "alog": [("gates", 0)], "dtb": [("gates", 1)], "out_gain": [("out_gain", 0)],
    }
    landed_names = sorted(landed)
    param_names = sorted(params)
    n_l, n_p = len(landed_names), len(param_names)

    def body(*refs):
        l_refs = dict(zip(landed_names, refs[:n_l]))
        p_refs = {p: refs[n_l + 3 * i:n_l + 3 * i + 3] for i, p in enumerate(param_names)}
        outs = refs[n_l + 3 * n_p:]
        o_refs = {p: outs[4 * i:4 * i + 4] for i, p in enumerate(param_names)}
        conv_ref, loss_ref = outs[4 * n_p:]
        sums = {nm: _sum_devices(l_refs[nm]) for nm in landed_names}
        sums["kv"] = sums["x2"][1:2, :] + sums["x2"][2:3, :]
        for p in param_names:
            w_ref, m_ref, v_ref = p_refs[p]
            g_ref, d_ref, nm_ref, nv_ref = o_refs[p]
            for r, (src, src_row) in enumerate(layout[p]):
                g = sums[src][src_row:src_row + 1, :]
                g_ref[r:r + 1, :] = g
                d, nm, nv = _adamw_math(g, w_ref[r:r + 1, :], m_ref[r:r + 1, :], v_ref[r:r + 1, :])
                d_ref[r:r + 1, :] = d
                nm_ref[r:r + 1, :] = nm
                nv_ref[r:r + 1, :] = nv
        conv_ref[...] = sums["conv"]
        loss_ref[...] = sums["loss"]

    args = [landed[nm] for nm in landed_names]
    out_shape = []
    for p in param_names:
        args += list(params[p])
        out_shape += [jax.ShapeDtypeStruct(params[p][0].shape, F32)] * 4
    out_shape += [jax.ShapeDtypeStruct(landed["conv"].shape[1:], F32), jax.ShapeDtypeStruct(landed["loss"].shape[1:], F32)]
    outs = pl.pallas_call(body, name=name, out_shape=out_shape)(*args)
    result = {p: tuple(outs[4 * i:4 * i + 4]) for i, p in enumerate(param_names)}
    result["conv"], result["loss"] = outs[4 * n_p], outs[4 * n_p + 1]
    return result


def _lanes(vec, offset):
    return jnp.pad(vec[None, :], ((0, 0), (offset, 128 - offset - vec.shape[0])))


def kernel(x, mix_pre_gain, mix_post_gain, mlp_pre_gain, mlp_post_gain, mlp_w_up, mlp_w_down, gdn_w_in, gdn_conv_w, gdn_a_log, gdn_dt_bias, gdn_out_gain, gdn_w_out, kv_gain, w_kv, sb_w_q, sb_w_o, loss_target, m_mix_pre_gain, m_mix_post_gain, m_mlp_pre_gain, m_mlp_post_gain, m_mlp_w_up, m_mlp_w_down, m_gdn_w_in, m_gdn_conv_w, m_gdn_a_log, m_gdn_dt_bias, m_gdn_out_gain, m_gdn_w_out, m_kv_gain, m_w_kv, m_sb_w_q, m_sb_w_o, v_mix_pre_gain, v_mix_post_gain, v_mlp_pre_gain, v_mlp_post_gain, v_mlp_w_up, v_mlp_w_down, v_gdn_w_in, v_gdn_conv_w, v_gdn_a_log, v_gdn_dt_bias, v_gdn_out_gain, v_gdn_w_out, v_kv_gain, v_w_kv, v_sb_w_q, v_sb_w_o):
    me = 4 * lax.axis_index("x") + 2 * lax.axis_index("y") + lax.axis_index("c")
    bf = lambda a: a.astype(BF16)

    shards = {"up0": bf(mlp_w_up[0]), "up1": bf(mlp_w_up[1]), "down0": bf(mlp_w_down[0]), "down1": bf(mlp_w_down[1]),
              "gdn_out": bf(gdn_w_out[0]), "kv": bf(w_kv), "sb_q": bf(sb_w_q[0]), "sb_o": bf(sb_w_o[0])}
    gdn_in, conv = _gather_two_level([bf(gdn_w_in[0]), gdn_conv_w[0]], "gather_first_weights")
    comm = _Fsdp(shards, {"gdn_in": _whole_weight("gdn_in", gdn_in), "conv": _whole_weight("conv", conv)})
    gains = {"mix_pre": mix_pre_gain, "mix_post": mix_post_gain, "mlp_pre": mlp_pre_gain, "mlp_post": mlp_post_gain,
             "kv": kv_gain[None, :], "alog": _lanes(gdn_a_log[0], N_HEADS), "dtb": _lanes(gdn_dt_bias[0], N_HEADS),
             "out_gain": gdn_out_gain}

    grad_x, big, small = _local_step(x[0], loss_target[0], gains, comm)

    small_names = ["pre0", "x2", "post0", "post1", "mlp0", "mlp1", "gates", "out_gain", "conv", "loss"]
    landed = _exchange([small[nm] for nm in small_names], [True] * len(small_names), "exchange_small_grads")
    small_landed = dict(zip(small_names, landed))

    results = {}
    big_params = [("mlp_w_up0", "up0", mlp_w_up[0], m_mlp_w_up[0], v_mlp_w_up[0]),
                  ("mlp_w_up1", "up1", mlp_w_up[1], m_mlp_w_up[1], v_mlp_w_up[1]),
                  ("mlp_w_down0", "down0", mlp_w_down[0], m_mlp_w_down[0], v_mlp_w_down[0]),
                  ("mlp_w_down1", "down1", mlp_w_down[1], m_mlp_w_down[1], v_mlp_w_down[1]),
                  ("gdn_w_in", "gdn_in", gdn_w_in[0], m_gdn_w_in[0], v_gdn_w_in[0]),
                  ("gdn_w_out", "gdn_out", gdn_w_out[0], m_gdn_w_out[0], v_gdn_w_out[0]),
                  ("w_kv", "kv", w_kv, m_w_kv, v_w_kv), ("sb_w_q", "sb_q", sb_w_q[0], m_sb_w_q[0], v_sb_w_q[0]),
                  ("sb_w_o", "sb_o", sb_w_o[0], m_sb_w_o[0], v_sb_w_o[0])]
    for nm, short, w_, m_, v_ in big_params:
        results[nm] = _reduce_adamw(comm.landed[short], w_, m_, v_, "adamw_" + nm)
    lanes8 = lambda a: _lanes(a[0], N_HEADS)
    small_params = {
        "mix_pre": (mix_pre_gain, m_mix_pre_gain, v_mix_pre_gain), "mix_post": (mix_post_gain, m_mix_post_gain, v_mix_post_gain),
        "mlp_pre": (mlp_pre_gain, m_mlp_pre_gain, v_mlp_pre_gain), "mlp_post": (mlp_post_gain, m_mlp_post_gain, v_mlp_post_gain),
        "kv": (kv_gain[None, :], m_kv_gain[None, :], v_kv_gain[None, :]),
        "alog": (lanes8(gdn_a_log), lanes8(m_gdn_a_log), lanes8(v_gdn_a_log)),
        "dtb": (lanes8(gdn_dt_bias), lanes8(m_gdn_dt_bias), lanes8(v_gdn_dt_bias)),
        "out_gain": (gdn_out_gain, m_gdn_out_gain, v_gdn_out_gain),
    }
    sm = _small_update(small_landed, small_params, "small_update")
    conv_cols = QKV // N_DEV
    g_conv = lax.dynamic_slice(sm["conv"], (0, me * conv_cols), (8, conv_cols))[:CONV_WIDTH]
    conv_res = (g_conv,) + tuple(_adamw(g_conv, gdn_conv_w[0], m_gdn_conv_w[0], v_gdn_conv_w[0], "adamw_conv"))

    stack2 = lambda a, b: tuple(jnp.stack([p, q]) for p, q in zip(results[a], results[b]))
    lead = lambda t: tuple(a[None] for a in t)
    heads8 = lambda t: tuple(a[:, N_HEADS:2 * N_HEADS] for a in t)
    per_weight = [
        sm["mix_pre"], sm["mix_post"], sm["mlp_pre"], sm["mlp_post"],
        stack2("mlp_w_up0", "mlp_w_up1"), stack2("mlp_w_down0", "mlp_w_down1"),
        lead(results["gdn_w_in"]), lead(conv_res), heads8(sm["alog"]), heads8(sm["dtb"]), sm["out_gain"],
        lead(results["gdn_w_out"]), tuple(a[0] for a in sm["kv"]), results["w_kv"], lead(results["sb_w_q"]), lead(results["sb_w_o"]),
    ]
    grads, deltas, new_ms, new_vs = zip(*per_weight)
    return (sm["loss"][0, 0], grad_x[None], *grads, *deltas, *new_ms, *new_vs)
```

```python
import functools
import math

import jax
import jax.numpy as jnp
from jax import lax
from jax.experimental import pallas as pl
from jax.experimental.pallas import tpu as pltpu

F32 = jnp.float32
BF16 = jnp.bfloat16

N_DEV = 8
D_MODEL = 1024
D_FF = 4096
N_HEADS = 8
HEAD_DIM = 128
CHUNK = 64
CONV_WIDTH = 4
GDN_IN_COLS = 4 * D_MODEL + 2 * N_HEADS
GDN_IN_PAD = 4 * D_MODEL + 128
EPS = 1e-6

ADAM_LR = 0.001
ADAM_B1 = 0.9
ADAM_B2 = 0.999
ADAM_EPS = 1e-08
ADAM_WD = 0.01
ADAM_STEP = 10

VMEM_LIMIT_BYTES = 56 * 1024 * 1024
MESH = pl.DeviceIdType.MESH


def _params(*semantics):
    return pltpu.CompilerParams(dimension_semantics=semantics, vmem_limit_bytes=VMEM_LIMIT_BYTES)


def _dims(ta, tb):
    return (((0,) if ta else (1,), (1,) if tb else (0,)), ((), ()))


def _dot(a, b, ta=False, tb=False):
    return lax.dot_general(a.astype(BF16), b.astype(BF16), _dims(ta, tb), preferred_element_type=F32)


def _dot_f32(a, b, ta=False, tb=False):
    return lax.dot_general(a, b, _dims(ta, tb), precision=lax.Precision.HIGHEST, preferred_element_type=F32)


def _dot_3x(a, b, ta=False, tb=False):
    return lax.dot_general(a, b, _dims(ta, tb), precision=lax.Precision.HIGH, preferred_element_type=F32)


def _make_mm(dot):
    @functools.partial(jax.custom_vjp, nondiff_argnums=(2, 3))
    def mm(a, b, ta, tb):
        return dot(a, b, ta, tb)

    def fwd(a, b, ta, tb):
        return dot(a, b, ta, tb), (a, b)

    def bwd(ta, tb, res, g):
        a, b = res
        if not ta and not tb:
            return mm(g, b, False, True), mm(a, g, True, False)
        if not ta and tb:
            return mm(g, b, False, False), mm(g, a, True, False)
        if ta and not tb:
            return mm(b, g, False, True), mm(a, g, False, False)
        raise NotImplementedError

    mm.defvjp(fwd, bwd)
    return mm


_mm = _make_mm(_dot)
_mm_f32 = _make_mm(_dot_f32)


def _rms(x, gain):
    r = lax.rsqrt(jnp.mean(x * x, axis=-1, keepdims=True) + EPS)
    return x * r * gain


def _rms_bwd(x, gain, dy):
    r = lax.rsqrt(jnp.mean(x * x, axis=-1, keepdims=True) + EPS)
    xh = x * r
    dgain = jnp.sum(dy * xh, axis=0, keepdims=True)
    dxh = dy * gain
    dx = r * (dxh - xh * jnp.mean(dxh * xh, axis=-1, keepdims=True))
    return dx, dgain


def _silu(x):
    return x / (1.0 + jnp.exp(-x))


def _acc_rows(ref, first, rows):
    @pl.when(first)
    def _():
        ref[...] = jnp.zeros_like(ref)

    for r, val in enumerate(rows):
        ref[r:r + 1, :] += val


def _norm_matmul(x, gain, w, out_dtype, name, tm=1024, tn=1024):
    n, d = x.shape
    g, _, wc = w.shape
    tm, tn = min(tm, n), min(tn, wc)
    per = wc // tn
    assert n % tm == 0 and wc % tn == 0

    def body(x_ref, gain_ref, w_ref, out_ref, h_ref):
        @pl.when(pl.program_id(1) == 0)
        def _():
            h_ref[...] = _rms(x_ref[...], gain_ref[...]).astype(BF16)

        out_ref[...] = jnp.dot(h_ref[...], w_ref[...], preferred_element_type=F32).astype(out_dtype)

    return pl.pallas_call(
        body, name=name, grid=(n // tm, g * per),
        in_specs=[pl.BlockSpec((tm, d), lambda i, j: (i, 0)),
                  pl.BlockSpec((1, d), lambda i, j: (0, 0)),
                  pl.BlockSpec((None, d, tn), lambda i, j: (j // per, 0, j % per))],
        out_specs=[pl.BlockSpec((tm, tn), lambda i, j: (i, j)),
                   pl.BlockSpec((tm, d), lambda i, j: (i, 0))],
        out_shape=[jax.ShapeDtypeStruct((n, g * wc), out_dtype), jax.ShapeDtypeStruct((n, d), BF16)],
        compiler_params=_params("parallel", "arbitrary"),
    )(x, gain, w)


def _norm_matmul_bwd(branches, x, add, name, tm=1024, ride=None):
    n, d = x.shape
    tm = min(tm, n)
    tns = [min(tn, w.shape[2]) for _, w, _, tn in branches]
    pers = [w.shape[2] // tn for (_, w, _, _), tn in zip(branches, tns)]
    ncols = [w.shape[0] * per for (_, w, _, _), per in zip(branches, pers)]
    offs = [sum(ncols[:b]) for b in range(len(branches))]
    total = sum(ncols)
    nb = len(branches)

    def body(*refs):
        dout_refs, w_refs, gain_refs = refs[:nb], refs[nb:2 * nb], refs[2 * nb:3 * nb]
        x_ref, add_ref, dx_ref, dgain_ref, acc_ref, sum_ref = refs[3 * nb:]
        i, j = pl.program_id(0), pl.program_id(1)

        @pl.when((i == 0) & (j == 0))
        def _():
            dgain_ref[...] = jnp.zeros_like(dgain_ref)

        for b in range(nb):
            first, last = offs[b], offs[b] + ncols[b] - 1

            @pl.when((j >= first) & (j <= last))
            def _(b=b, first=first):
                part = _dot(dout_refs[b][...], w_refs[b][...], tb=True)

                @pl.when(j == first)
                def _():
                    acc_ref[...] = part

                @pl.when(j > first)
                def _():
                    acc_ref[...] += part

            @pl.when(j == last)
            def _(b=b):
                xv = x_ref[...]
                xh = xv * lax.rsqrt(jnp.mean(xv * xv, axis=-1, keepdims=True) + EPS)
                dh = acc_ref[...]
                dgain_ref[b:b + 1, :] += jnp.sum(dh * xh, axis=0, keepdims=True)
                scaled = dh * gain_refs[b][...]
                if b == 0:
                    sum_ref[...] = scaled
                else:
                    sum_ref[...] += scaled

        @pl.when(j == total - 1)
        def _():
            xv = x_ref[...]
            r = lax.rsqrt(jnp.mean(xv * xv, axis=-1, keepdims=True) + EPS)
            xh = xv * r
            dxh = sum_ref[...]
            dx_ref[...] = add_ref[...] + r * (dxh - xh * jnp.mean(dxh * xh, axis=-1, keepdims=True))

    def dout_spec(b):
        return pl.BlockSpec((tm, tns[b]), lambda i, j: (i, jnp.clip(j - offs[b], 0, ncols[b] - 1)))

    def w_spec(b):
        def index(i, j):
            c = jnp.clip(j - offs[b], 0, ncols[b] - 1)
            return (c // pers[b], 0, c % pers[b])
        return pl.BlockSpec((None, d, tns[b]), index)

    row = lambda i, j: (i, 0)
    const = lambda i, j: (0, 0)
    nrow = n // tm
    (dx, dgain), got = _call(
        body, [b[0] for b in branches] + [b[1] for b in branches] + [b[2] for b in branches] + [x, add],
        name=name, grid=(nrow, total),
        in_specs=[dout_spec(b) for b in range(nb)] + [w_spec(b) for b in range(nb)]
                 + [pl.BlockSpec((1, d), const)] * nb + [pl.BlockSpec((tm, d), row), pl.BlockSpec((tm, d), row)],
        out_specs=[pl.BlockSpec((tm, d), row), pl.BlockSpec((8, d), const)],
        out_shape=[jax.ShapeDtypeStruct((n, d), F32), jax.ShapeDtypeStruct((8, d), F32)],
        scratch_shapes=[pltpu.VMEM((tm, d), F32), pltpu.VMEM((tm, d), F32)],
        compiler_params=_params("arbitrary", "arbitrary"),
        ride=ride, first=lambda: (pl.program_id(0) == 0) & (pl.program_id(1) == 0),
        last=lambda: (pl.program_id(0) == nrow - 1) & (pl.program_id(1) == total - 1))
    return (dx, dgain) if ride is None else (dx, dgain, got)


def _matmul_tn(a, b, groups, name, ta=1024, tb=1024, tk=1024):
    n, ka = a.shape
    _, kb = b.shape
    wc = kb // groups
    ta, tb, tk = min(ta, ka), min(tb, wc), min(tk, n)
    per = wc // tb
    nk = n // tk
    assert ka % ta == 0 and wc % tb == 0 and n % tk == 0

    def body(a_ref, b_ref, out_ref, acc_ref):
        k = pl.program_id(2)

        @pl.when(k == 0)
        def _():
            acc_ref[...] = jnp.zeros_like(acc_ref)

        acc_ref[...] += _dot(a_ref[...], b_ref[...], ta=True)

        @pl.when(k == nk - 1)
        def _():
            out_ref[...] = acc_ref[...].astype(BF16)

    return pl.pallas_call(
        body, name=name, grid=(ka // ta, groups * per, nk),
        in_specs=[pl.BlockSpec((tk, ta), lambda i, j, k: (k, i)),
                  pl.BlockSpec((tk, tb), lambda i, j, k: (k, j))],
        out_specs=pl.BlockSpec((None, ta, tb), lambda i, j, k: (j // per, i, j % per)),
        out_shape=jax.ShapeDtypeStruct((groups, ka, wc), BF16),
        scratch_shapes=[pltpu.VMEM((ta, tb), F32)],
        compiler_params=_params("parallel", "parallel", "arbitrary"),
    )(a, b)


def _gated_head_norm(o, gate, out_gain):
    parts = []
    for h in range(N_HEADS):
        sl = slice(h * HEAD_DIM, (h + 1) * HEAD_DIM)
        parts.append(_rms(o[:, sl], out_gain) * _silu(gate[:, sl]))
    return parts


def _out_proj(a_inputs, gated, x_in, w, gain, out_gain, name, tm=1024):
    n, d = x_in.shape
    k = w.shape[0]
    tm = min(tm, n)

    def body(*refs):
        if gated:
            o_ref, gate_ref, og_ref, x_ref, w_ref, gain_ref, xo_ref, y_ref, a_ref = refs
            parts = _gated_head_norm(o_ref[...], gate_ref[...], og_ref[...])
            for h, part in enumerate(parts):
                a_ref[:, h * HEAD_DIM:(h + 1) * HEAD_DIM] = part.astype(BF16)
            a = a_ref[...]
        else:
            a_in_ref, x_ref, w_ref, gain_ref, xo_ref, y_ref = refs
            a = a_in_ref[...]
        y = jnp.dot(a, w_ref[...], preferred_element_type=F32)
        y_ref[...] = y
        xo_ref[...] = x_ref[...] + _rms(y, gain_ref[...])

    row = lambda i: (i, 0)
    const = lambda i: (0, 0)
    if gated:
        a_specs = [pl.BlockSpec((tm, k), row), pl.BlockSpec((tm, D_MODEL), lambda i: (i, 3)),
                   pl.BlockSpec((1, HEAD_DIM), const)]
        a_args = list(a_inputs) + [out_gain]
    else:
        a_specs = [pl.BlockSpec((tm, k), row)]
        a_args = list(a_inputs)
    out_specs = [pl.BlockSpec((tm, d), row), pl.BlockSpec((tm, d), row)]
    out_shape = [jax.ShapeDtypeStruct((n, d), F32), jax.ShapeDtypeStruct((n, d), F32)]
    if gated:
        out_specs.append(pl.BlockSpec((tm, k), row))
        out_shape.append(jax.ShapeDtypeStruct((n, k), BF16))
    return pl.pallas_call(
        body, name=name, grid=(n // tm,),
        in_specs=a_specs + [pl.BlockSpec((tm, d), row), pl.BlockSpec((k, d), const), pl.BlockSpec((1, d), const)],
        out_specs=out_specs, out_shape=out_shape,
        compiler_params=_params("parallel"),
    )(*a_args, x_in, w, gain)


def _out_proj_bwd(dxo, y, w, gain, gated_inputs, name, tm=1024):
    n, d = dxo.shape
    k = w.shape[0]
    tm = min(tm, n)
    gated = gated_inputs is not None

    def body(*refs):
        if gated:
            (dxo_ref, y_ref, w_ref, gain_ref, o_ref, gate_ref, og_ref,
             dy_ref, dgain_ref, do_ref, dgate_ref, dog_ref) = refs
        else:
            dxo_ref, y_ref, w_ref, gain_ref, dy_ref, dgain_ref, da_ref = refs
        first = pl.program_id(0) == 0
        dy, dgain = _rms_bwd(y_ref[...], gain_ref[...], dxo_ref[...])
        dy_ref[...] = dy.astype(BF16)
        _acc_rows(dgain_ref, first, [dgain])
        da = _dot(dy_ref[...], w_ref[...], tb=True)
        if not gated:
            da_ref[...] = da.astype(BF16)
            return
        og = og_ref[...]
        dog = jnp.zeros_like(og)
        for h in range(N_HEADS):
            sl = slice(h * HEAD_DIM, (h + 1) * HEAD_DIM)
            fn = lambda o_h, g_h, gn: _rms(o_h, gn) * _silu(g_h)
            _, vjp = jax.vjp(fn, o_ref[:, sl], gate_ref[:, sl], og)
            do_h, dgate_h, dog_h = vjp(da[:, sl])
            do_ref[:, sl] = do_h
            dgate_ref[:, sl] = dgate_h
            dog = dog + dog_h
        _acc_rows(dog_ref, first, [dog])

    row = lambda i: (i, 0)
    const = lambda i: (0, 0)
    in_specs = [pl.BlockSpec((tm, d), row), pl.BlockSpec((tm, d), row), pl.BlockSpec((k, d), const),
                pl.BlockSpec((1, d), const)]
    args = [dxo, y, w, gain]
    out_specs = [pl.BlockSpec((tm, d), row), pl.BlockSpec((8, d), const)]
    out_shape = [jax.ShapeDtypeStruct((n, d), BF16), jax.ShapeDtypeStruct((8, d), F32)]
    if gated:
        in_specs += [pl.BlockSpec((tm, k), row), pl.BlockSpec((tm, D_MODEL), lambda i: (i, 3)),
                     pl.BlockSpec((1, HEAD_DIM), const)]
        args += list(gated_inputs)
        out_specs += [pl.BlockSpec((tm, k), row), pl.BlockSpec((tm, k), row), pl.BlockSpec((8, HEAD_DIM), const)]
        out_shape += [jax.ShapeDtypeStruct((n, k), F32), jax.ShapeDtypeStruct((n, k), F32),
                      jax.ShapeDtypeStruct((8, HEAD_DIM), F32)]
    else:
        out_specs.append(pl.BlockSpec((tm, k), row))
        out_shape.append(jax.ShapeDtypeStruct((n, k), BF16))
    return pl.pallas_call(
        body, name=name, grid=(n // tm,), in_specs=in_specs, out_specs=out_specs, out_shape=out_shape,
        compiler_params=_params("arbitrary"),
    )(*args)


def _mlp_fwd(x_in, g_pre, w_up, w_down, g_post, name, tm=1024, target=None):
    n, d = x_in.shape
    g, _, wc = w_up.shape
    tm = min(tm, n)
    with_loss = target is not None

    def body(*refs):
        if with_loss:
            x_ref, gpre_ref, wup_ref, wdown_ref, gpost_ref, t_ref, xo_ref, y_ref, h_ref, u_ref, a_ref, loss_ref, acc_ref = refs
        else:
            x_ref, gpre_ref, wup_ref, wdown_ref, gpost_ref, xo_ref, y_ref, h_ref, u_ref, a_ref, acc_ref = refs
        i, j = pl.program_id(0), pl.program_id(1)

        @pl.when(j == 0)
        def _():
            h_ref[...] = _rms(x_ref[...], gpre_ref[...]).astype(BF16)
            acc_ref[...] = jnp.zeros_like(acc_ref)

        u = jnp.dot(h_ref[...], wup_ref[...], preferred_element_type=F32).astype(BF16)
        u_ref[...] = u
        a = jnp.square(jnp.maximum(u, 0))
        a_ref[...] = a
        acc_ref[...] += jnp.dot(a, wdown_ref[...], preferred_element_type=F32)

        @pl.when(j == g - 1)
        def _():
            y = acc_ref[...]
            y_ref[...] = y
            xo = x_ref[...] + _rms(y, gpost_ref[...])
            if with_loss:
                err = xo - t_ref[...]
                xo_ref[...] = err * (1.0 / d)
                part = 0.5 * jnp.sum(jnp.mean(err * err, axis=-1, keepdims=True), axis=0, keepdims=True)
                _acc_rows(loss_ref, i == 0, [jnp.broadcast_to(part, (1, 128))])
            else:
                xo_ref[...] = xo

    row = lambda i, j: (i, 0)
    const = lambda i, j: (0, 0)
    in_specs = [pl.BlockSpec((tm, d), row), pl.BlockSpec((1, d), const),
                pl.BlockSpec((None, d, wc), lambda i, j: (j, 0, 0)),
                pl.BlockSpec((wc, d), lambda i, j: (j, 0)), pl.BlockSpec((1, d), const)]
    out_specs = [pl.BlockSpec((tm, d), row), pl.BlockSpec((tm, d), row), pl.BlockSpec((tm, d), row),
                 pl.BlockSpec((tm, wc), lambda i, j: (i, j)), pl.BlockSpec((tm, wc), lambda i, j: (i, j))]
    out_shape = [jax.ShapeDtypeStruct((n, d), F32), jax.ShapeDtypeStruct((n, d), F32),
                 jax.ShapeDtypeStruct((n, d), BF16), jax.ShapeDtypeStruct((n, g * wc), BF16),
                 jax.ShapeDtypeStruct((n, g * wc), BF16)]
    args = [x_in, g_pre, w_up, w_down, g_post]
    if with_loss:
        in_specs.append(pl.BlockSpec((tm, d), row))
        out_specs.append(pl.BlockSpec((8, 128), const))
        out_shape.append(jax.ShapeDtypeStruct((8, 128), F32))
        args.append(target)
    return pl.pallas_call(
        body, name=name, grid=(n // tm, g), in_specs=in_specs, out_specs=out_specs, out_shape=out_shape,
        scratch_shapes=[pltpu.VMEM((tm, d), F32)],
        compiler_params=_params("arbitrary", "arbitrary"),
    )(*args)


def _mlp_bwd(dxo, y, g_post, w_down, u, w_up, x_in, g_pre, name, tm=1024):
    n, d = dxo.shape
    g, _, wc = w_up.shape

    def body(dxo_ref, y_ref, gpost_ref, wdown_ref, u_ref, wup_ref, x_ref, gpre_ref,
             dx_ref, dy_ref, du_ref, dgain_ref, acc_ref, dgpost_ref):
        i, j = pl.program_id(0), pl.program_id(1)

        @pl.when(j == 0)
        def _():
            dy, dgpost = _rms_bwd(y_ref[...], gpost_ref[...], dxo_ref[...])
            dy_ref[...] = dy.astype(BF16)
            dgpost_ref[...] = dgpost
            acc_ref[...] = jnp.zeros_like(acc_ref)

        da = _dot(dy_ref[...], wdown_ref[...], tb=True)
        du = (da * (2.0 * jnp.maximum(u_ref[...], 0).astype(F32))).astype(BF16)
        du_ref[...] = du
        acc_ref[...] += _dot(du, wup_ref[...], tb=True)

        @pl.when(j == g - 1)
        def _():
            dx, dgpre = _rms_bwd(x_ref[...], gpre_ref[...], acc_ref[...])
            dx_ref[...] = dxo_ref[...] + dx
            _acc_rows(dgain_ref, i == 0, [dgpre, dgpost_ref[...]])

    row = lambda i, j: (i, 0)
    const = lambda i, j: (0, 0)
    return pl.pallas_call(
        body, name=name, grid=(n // tm, g),
        in_specs=[pl.BlockSpec((tm, d), row), pl.BlockSpec((tm, d), row), pl.BlockSpec((1, d), const),
                  pl.BlockSpec((wc, d), lambda i, j: (j, 0)), pl.BlockSpec((tm, wc), lambda i, j: (i, j)),
                  pl.BlockSpec((None, d, wc), lambda i, j: (j, 0, 0)), pl.BlockSpec((tm, d), row),
                  pl.BlockSpec((1, d), const)],
        out_specs=[pl.BlockSpec((tm, d), row), pl.BlockSpec((tm, d), row),
                   pl.BlockSpec((tm, wc), lambda i, j: (i, j)), pl.BlockSpec((8, d), const)],
        out_shape=[jax.ShapeDtypeStruct((n, d), F32), jax.ShapeDtypeStruct((n, d), BF16),
                   jax.ShapeDtypeStruct((n, g * wc), BF16), jax.ShapeDtypeStruct((8, d), F32)],
        scratch_shapes=[pltpu.VMEM((tm, d), F32), pltpu.VMEM((1, d), F32)],
        compiler_params=_params("arbitrary", "arbitrary"),
    )(dxo, y, g_post, w_down, u, w_up, x_in, g_pre)


QKV = 3 * D_MODEL


def _shifted(x, prev8, s):
    if s == 0:
        return x
    tm = x.shape[0]
    rolled = pltpu.roll(x, s, 0)
    head = pltpu.roll(prev8, s, 0)
    head = jnp.concatenate([head, jnp.zeros((tm - 8, x.shape[1]), x.dtype)], axis=0)
    rows = lax.broadcasted_iota(jnp.int32, x.shape, 0)
    return jnp.where(rows < s, head, rolled)


def _conv(x, prev8, conv_w):
    out = x * conv_w[3:4, :]
    for s in range(1, CONV_WIDTH):
        out = out + _shifted(x, prev8, s) * conv_w[3 - s:4 - s, :]
    return out


def _l2norm(x):
    return x * lax.rsqrt(jnp.sum(x * x, axis=-1, keepdims=True) + EPS)


def _gdn_act_head(cq, ck, cv):
    return _l2norm(_silu(cq)) * (HEAD_DIM ** -0.5), _l2norm(_silu(ck)), _silu(cv)


def _gdn_gates(ba, alog, dtb):
    lane = lax.broadcasted_iota(jnp.int32, ba.shape, 1)
    beta = 1.0 / (1.0 + jnp.exp(-ba))
    t = ba + dtb
    softplus = jnp.maximum(t, 0.0) + jnp.log(1.0 + jnp.exp(-jnp.abs(t)))
    g = -jnp.exp(alog) * softplus
    return jnp.where(lane < N_HEADS, beta, jnp.where(lane < 2 * N_HEADS, g, 0.0))


def _qkv_cols(part, h):
    start = part * D_MODEL + h * HEAD_DIM
    return slice(start, start + HEAD_DIM)


def _prev_rows_spec(tm, cols, colblock, order):
    per = tm // 8
    return pl.BlockSpec((8, cols), lambda i: (jnp.maximum(order(i) * per - 1, 0), colblock))


def _gdn_pre(proj, conv_w, alog, dtb, name, tm=256):
    n = proj.shape[0]

    def body(x_ref, prev_ref, ba_ref, cw_ref, alog_ref, dtb_ref, q_ref, k_ref, v_ref, bg_ref):
        first = pl.program_id(0) == 0
        for h in range(N_HEADS):
            convs = []
            for part in range(3):
                sl = _qkv_cols(part, h)
                prev8 = jnp.where(first, 0.0, prev_ref[:, sl])
                convs.append(_conv(x_ref[:, sl], prev8, cw_ref[:, sl]))
            out = slice(h * HEAD_DIM, (h + 1) * HEAD_DIM)
            q_ref[:, out], k_ref[:, out], v_ref[:, out] = _gdn_act_head(*convs)
        bg_ref[...] = _gdn_gates(ba_ref[...], alog_ref[...], dtb_ref[...])

    row = lambda i: (i, 0)
    const = lambda i: (0, 0)
    ident = lambda i: i
    return pl.pallas_call(
        body, name=name, grid=(n // tm,),
        in_specs=[pl.BlockSpec((tm, QKV), row), _prev_rows_spec(tm, QKV, 0, ident),
                  pl.BlockSpec((tm, 128), lambda i: (i, 4 * D_MODEL // 128)),
                  pl.BlockSpec((CONV_WIDTH, QKV), const), pl.BlockSpec((1, 128), const), pl.BlockSpec((1, 128), const)],
        out_specs=[pl.BlockSpec((tm, D_MODEL), row)] * 3 + [pl.BlockSpec((tm, 128), row)],
        out_shape=[jax.ShapeDtypeStruct((n, D_MODEL), F32)] * 3 + [jax.ShapeDtypeStruct((n, 128), F32)],
        compiler_params=_params("parallel"),
    )(proj, proj, proj, conv_w, alog, dtb)


def _gdn_pre_bwd(proj, conv_w, alog, dtb, dq, dk, dv, dbg, dgate, name, tm=256):
    n = proj.shape[0]
    nt = n // tm

    def body(x_ref, prev_ref, ba_ref, cw_ref, alog_ref, dtb_ref, dq_ref, dk_ref, dv_ref, dbg_ref, dgate_ref,
             dproj_ref, dcw_ref, dgates_ref, carry_ref):
        step = pl.program_id(0)
        tile = nt - 1 - step
        @pl.when(step == 0)
        def _():
            carry_ref[...] = jnp.zeros_like(carry_ref)
            dcw_ref[...] = jnp.zeros_like(dcw_ref)

        rows = lax.broadcasted_iota(jnp.int32, (tm, HEAD_DIM), 0)
        for h in range(N_HEADS):
            cols = [_qkv_cols(part, h) for part in range(3)]
            prevs = [jnp.where(tile == 0, 0.0, prev_ref[:, sl]) for sl in cols]
            convs = [_conv(x_ref[:, sl], prev8, cw_ref[:, sl]) for sl, prev8 in zip(cols, prevs)]
            _, vjp = jax.vjp(_gdn_act_head, *convs)
            out = slice(h * HEAD_DIM, (h + 1) * HEAD_DIM)
            dcs = vjp((dq_ref[:, out], dk_ref[:, out], dv_ref[:, out]))
            for sl, prev8, dc in zip(cols, prevs, dcs):
                x = x_ref[:, sl]
                cw = cw_ref[:, sl]
                dx = dc * cw[3:4, :]
                wrapped = jnp.zeros((8, HEAD_DIM), F32)
                dcw_ref[3:4, sl] += jnp.sum(dc * x, axis=0, keepdims=True)
                for s in range(1, CONV_WIDTH):
                    r = pltpu.roll(dc, tm - s, 0) * cw[3 - s:4 - s, :]
                    dx = dx + jnp.where(rows < tm - s, r, 0.0)
                    wrapped = wrapped + jnp.where(rows[tm - 8:, :] >= tm - s, r[tm - 8:, :], 0.0)
                    dcw_ref[3 - s:4 - s, sl] += jnp.sum(dc * _shifted(x, prev8, s), axis=0, keepdims=True)
                dproj_ref[:, sl] = dx
                dproj_ref[tm - 8:, sl] += carry_ref[:, sl]
                carry_ref[:, sl] = wrapped

        _, vjp = jax.vjp(_gdn_gates, ba_ref[...], alog_ref[...], dtb_ref[...])
        dba, dalog, ddtb = vjp(dbg_ref[...])
        dproj_ref[:, QKV:4 * D_MODEL] = dgate_ref[...]
        dproj_ref[:, 4 * D_MODEL:] = dba
        _acc_rows(dgates_ref, step == 0, [dalog, ddtb])

    rev = lambda i: nt - 1 - i
    row = lambda i: (rev(i), 0)
    const = lambda i: (0, 0)
    return pl.pallas_call(
        body, name=name, grid=(nt,),
        in_specs=[pl.BlockSpec((tm, QKV), row), _prev_rows_spec(tm, QKV, 0, rev),
                  pl.BlockSpec((tm, 128), lambda i: (rev(i), 4 * D_MODEL // 128)),
                  pl.BlockSpec((CONV_WIDTH, QKV), const), pl.BlockSpec((1, 128), const), pl.BlockSpec((1, 128), const),
                  pl.BlockSpec((tm, D_MODEL), row), pl.BlockSpec((tm, D_MODEL), row), pl.BlockSpec((tm, D_MODEL), row),
                  pl.BlockSpec((tm, 128), row), pl.BlockSpec((tm, D_MODEL), row)],
        out_specs=[pl.BlockSpec((tm, GDN_IN_PAD), row), pl.BlockSpec((8, QKV), const), pl.BlockSpec((8, 128), const)],
        out_shape=[jax.ShapeDtypeStruct((n, GDN_IN_PAD), F32), jax.ShapeDtypeStruct((8, QKV), F32),
                   jax.ShapeDtypeStruct((8, 128), F32)],
        scratch_shapes=[pltpu.VMEM((8, QKV), F32)],
        compiler_params=_params("arbitrary"),
    )(proj, proj, proj, conv_w, alog, dtb, dq, dk, dv, dbg, dgate)


@jax.custom_vjp
def _unit_lower_inverses(lowers):
    n = lowers[0].shape[0]
    eye = (lax.broadcasted_iota(jnp.int32, (n, n), 0) == lax.broadcasted_iota(jnp.int32, (n, n), 1)).astype(F32)
    ys = [-low for low in lowers]
    ps = [eye + y for y in ys]
    for _ in range(int(math.log2(CHUNK)) - 1):
        ys = [_dot_3x(y, y) for y in ys]
        ps = [p + _dot_3x(p, y) for p, y in zip(ps, ys)]
    return tuple(ps)


def _unit_lower_inverses_fwd(lowers):
    ts = _unit_lower_inverses(lowers)
    return ts, ts


def _unit_lower_inverses_bwd(ts, dts):
    left = [_dot_3x(t, dt, ta=True) for t, dt in zip(ts, dts)]
    return (tuple(-_dot_3x(l, t, tb=True) for l, t in zip(left, ts)),)


_unit_lower_inverses.defvjp(_unit_lower_inverses_fwd, _unit_lower_inverses_bwd)


@jax.custom_vjp
def _known_inverses(lowers, inverses):
    return inverses


def _known_inverses_bwd(ts, dts):
    (d_lowers,) = _unit_lower_inverses_bwd(ts, dts)
    return d_lowers, tuple(jnp.zeros_like(t) for t in ts)


_known_inverses.defvjp(lambda lowers, inverses: (inverses, inverses), _known_inverses_bwd)


GDN_STEP_CHUNKS = 2


def _gdn_chunks(qs, ks, vs, bgs, states, inverses=None, keep_inverses=False):
    c = CHUNK
    heads = range(N_HEADS)
    items = [(j, h) for j in range(len(bgs)) for h in heads]
    row = lax.broadcasted_iota(jnp.int32, (c, c), 0)
    col = lax.broadcasted_iota(jnp.int32, (c, c), 1)
    incl, strict, eye = row >= col, row > col, row == col
    lane = lax.broadcasted_iota(jnp.int32, (c, 128), 1)
    rowc = lax.broadcasted_iota(jnp.int32, (c, 1), 0)
    gc_all = [_mm_f32(incl.astype(F32), bg, False, False) for bg in bgs]
    q, k, v = ([xs[j][h] for j, h in items] for xs in (qs, ks, vs))
    n = range(len(items))
    beta = [jnp.sum(jnp.where(lane == h, bgs[j], 0.0), axis=1, keepdims=True) for j, h in items]
    gc = [jnp.sum(jnp.where(lane == N_HEADS + h, gc_all[j], 0.0), axis=1, keepdims=True) for j, h in items]
    gc_row = [jnp.sum(jnp.where(eye, gc[i], 0.0), axis=0, keepdims=True) for i in n]
    gc_last = [jnp.sum(jnp.where(rowc == c - 1, gc[i], 0.0), axis=0, keepdims=True) for i in n]
    decay = [jnp.where(incl, jnp.exp(jnp.where(incl, gc[i] - gc_row[i], 0.0)), 0.0) for i in n]
    kb = [k[i] * beta[i] for i in n]
    lower = tuple(jnp.where(strict, _mm(kb[i], k[i], False, True) * decay[i], 0.0) for i in n)
    attn = [_mm(q[i], k[i], False, True) * decay[i] for i in n]
    t_mat = _unit_lower_inverses(lower) if inverses is None else _known_inverses(lower, inverses)
    egc = [jnp.exp(gc[i]) for i in n]
    w = [_mm(t_mat[i], kb[i] * egc[i], False, False) for i in n]
    u = [_mm(t_mat[i], v[i] * beta[i], False, False) for i in n]
    qg = [q[i] * egc[i] for i in n]
    kg = [k[i] * jnp.exp(gc_last[i] - gc[i]) for i in n]
    outs, cur = [], list(states)
    for j in range(len(bgs)):
        at = lambda h: j * N_HEADS + h
        v_new = [u[at(h)] - _mm(w[at(h)], cur[h], False, False) for h in heads]
        from_state = [_mm(qg[at(h)], cur[h], False, False) for h in heads]
        outs.append(tuple(from_state[h] + _mm(attn[at(h)], v_new[h], False, False) for h in heads))
        cur = [cur[h] * jnp.exp(gc_last[at(h)]) + _mm(kg[at(h)], v_new[h], True, False) for h in heads]
    return (tuple(outs), tuple(cur), t_mat) if keep_inverses else (tuple(outs), tuple(cur))


def _chunk_head_slices(ref, nch):
    return tuple(tuple(ref[j * CHUNK:(j + 1) * CHUNK, h * HEAD_DIM:(h + 1) * HEAD_DIM] for h in range(N_HEADS))
                 for j in range(nch))


def _store_chunk_heads(ref, values):
    for j, chunk in enumerate(values):
        for h, val in enumerate(chunk):
            ref[j * CHUNK:(j + 1) * CHUNK, h * HEAD_DIM:(h + 1) * HEAD_DIM] = val


def _gdn_scan(qn, kn, v, bg, name, ride=None):
    n = qn.shape[0]
    nch = GDN_STEP_CHUNKS
    rows = nch * CHUNK
    nc = n // rows

    def body(q_ref, k_ref, v_ref, bg_ref, o_ref, saved_ref, inv_ref, state_ref):
        @pl.when(pl.program_id(0) == 0)
        def _():
            state_ref[...] = jnp.zeros_like(state_ref)

        states = tuple(state_ref[h] for h in range(N_HEADS))
        for h in range(N_HEADS):
            saved_ref[h] = states[h]
        bgs = tuple(bg_ref[j * CHUNK:(j + 1) * CHUNK, :] for j in range(nch))
        outs, new_states, inverses = _gdn_chunks(_chunk_head_slices(q_ref, nch), _chunk_head_slices(k_ref, nch),
                                                 _chunk_head_slices(v_ref, nch), bgs, states, keep_inverses=True)
        _store_chunk_heads(o_ref, outs)
        for h in range(N_HEADS):
            state_ref[h] = new_states[h]
        for item, inverse in enumerate(inverses):
            inv_ref[item] = inverse

    row = lambda i: (i, 0)
    return _call(
        body, [qn, kn, v, bg], name=name, grid=(nc,),
        in_specs=[pl.BlockSpec((rows, D_MODEL), row)] * 3 + [pl.BlockSpec((rows, 128), row)],
        out_specs=[pl.BlockSpec((rows, D_MODEL), row),
                   pl.BlockSpec((None, N_HEADS, HEAD_DIM, HEAD_DIM), lambda i: (i, 0, 0, 0)),
                   pl.BlockSpec((None, nch * N_HEADS, CHUNK, CHUNK), lambda i: (i, 0, 0, 0))],
        out_shape=[jax.ShapeDtypeStruct((n, D_MODEL), F32),
                   jax.ShapeDtypeStruct((nc, N_HEADS, HEAD_DIM, HEAD_DIM), F32),
                   jax.ShapeDtypeStruct((nc, nch * N_HEADS, CHUNK, CHUNK), F32)],
        scratch_shapes=[pltpu.VMEM((N_HEADS, HEAD_DIM, HEAD_DIM), F32)],
        compiler_params=_params("arbitrary"),
        ride=ride, first=lambda: pl.program_id(0) == 0, last=lambda: pl.program_id(0) == nc - 1)


def _gdn_scan_bwd(qn, kn, v, bg, saved, inverses, do, name, ride=None):
    n = qn.shape[0]
    nch = GDN_STEP_CHUNKS
    rows = nch * CHUNK
    nc = n // rows

    def body(q_ref, k_ref, v_ref, bg_ref, saved_ref, inv_ref, do_ref, dq_ref, dk_ref, dv_ref, dbg_ref, dstate_ref):
        @pl.when(pl.program_id(0) == 0)
        def _():
            dstate_ref[...] = jnp.zeros_like(dstate_ref)

        states = tuple(saved_ref[h] for h in range(N_HEADS))
        bgs = tuple(bg_ref[j * CHUNK:(j + 1) * CHUNK, :] for j in range(nch))
        known = tuple(inv_ref[item] for item in range(nch * N_HEADS))
        _, vjp = jax.vjp(functools.partial(_gdn_chunks, inverses=known), _chunk_head_slices(q_ref, nch),
                         _chunk_head_slices(k_ref, nch), _chunk_head_slices(v_ref, nch), bgs, states)
        dstates = tuple(dstate_ref[h] for h in range(N_HEADS))
        dqs, dks, dvs, dbgs, dprev = vjp((_chunk_head_slices(do_ref, nch), dstates))
        _store_chunk_heads(dq_ref, dqs)
        _store_chunk_heads(dk_ref, dks)
        _store_chunk_heads(dv_ref, dvs)
        for h in range(N_HEADS):
            dstate_ref[h] = dprev[h]
        for j in range(nch):
            dbg_ref[j * CHUNK:(j + 1) * CHUNK, :] = dbgs[j]

    row = lambda i: (nc - 1 - i, 0)
    return _call(
        body, [qn, kn, v, bg, saved, inverses, do], name=name, grid=(nc,),
        in_specs=[pl.BlockSpec((rows, D_MODEL), row)] * 3 + [pl.BlockSpec((rows, 128), row),
                  pl.BlockSpec((None, N_HEADS, HEAD_DIM, HEAD_DIM), lambda i: (nc - 1 - i, 0, 0, 0)),
                  pl.BlockSpec((None, nch * N_HEADS, CHUNK, CHUNK), lambda i: (nc - 1 - i, 0, 0, 0)),
                  pl.BlockSpec((rows, D_MODEL), row)],
        out_specs=[pl.BlockSpec((rows, D_MODEL), row)] * 3 + [pl.BlockSpec((rows, 128), row)],
        out_shape=[jax.ShapeDtypeStruct((n, D_MODEL), F32)] * 3 + [jax.ShapeDtypeStruct((n, 128), F32)],
        scratch_shapes=[pltpu.VMEM((N_HEADS, HEAD_DIM, HEAD_DIM), F32)],
        compiler_params=_params("arbitrary"),
        ride=ride, first=lambda: pl.program_id(0) == 0, last=lambda: pl.program_id(0) == nc - 1)


SB_BQ = 512
SB_BK = 256
SB_SUB = 128
SB_ROWS = 128
SB_SCALE = HEAD_DIM ** -0.5
SB_DEAD = -105.0


def _sb_terms(z, before):
    e = jnp.exp(-jnp.abs(z))
    log_beta = jnp.minimum(z, 0.0) - jnp.log(1.0 + e)
    log_1m = log_beta - z
    if before is not None:
        log_1m = jnp.where(before, log_1m, 0.0)
    return e, log_beta, log_1m


def _tri_ones(n, cmp):
    r = lax.broadcasted_iota(jnp.int32, (n, 2 * n), 0)
    c = lax.broadcasted_iota(jnp.int32, (n, 2 * n), 1)
    return jnp.where((c >= n) | cmp(r, c), 1.0, 0.0).astype(BF16)


def _sums(x, tri_ones):
    both = jnp.dot(x.astype(BF16), tri_ones, preferred_element_type=F32)
    n = x.shape[1]
    return both[:, :n], both[:, n:]


def _sb_chunk_mask(diagonal, r, s):
    if not diagonal or s * SB_SUB + SB_SUB - 1 < r * SB_ROWS:
        return None
    if s * SB_SUB >= r * SB_ROWS + SB_ROWS - 1:
        return "empty"
    rows = r * SB_ROWS + lax.broadcasted_iota(jnp.int32, (SB_ROWS, SB_SUB), 0)
    cols = s * SB_SUB + lax.broadcasted_iota(jnp.int32, (SB_ROWS, SB_SUB), 1)
    return cols < rows


def _sb_attention(q, kv, name, bq=SB_BQ, ride=None):
    n = q.shape[0]
    bq = min(bq, n)
    bk = min(SB_BK, bq)
    nrc = bq // SB_ROWS

    def body(q_ref, k_ref, v_ref, o_ref, l_ref, n_ref, z_scr, a_scr):
        i = pl.program_id(1)
        qb = q_ref[...]
        after = _tri_ones(SB_SUB, lambda r, c: r > c)

        def span(start, kw, c_sum, acc, diagonal):
            start = pl.multiple_of(start, kw)
            k_w = k_ref[pl.ds(start, kw), :]
            v_w = v_ref[pl.ds(start, kw), :]
            z_scr[:, :kw] = lax.dot_general(qb, k_w, _dims(False, True), preferred_element_type=F32)
            c_rows = [c_sum[r * SB_ROWS:(r + 1) * SB_ROWS] for r in range(nrc)]
            for s in reversed(range(kw // SB_SUB)):
                cols = slice(s * SB_SUB, (s + 1) * SB_SUB)
                for r in range(nrc):
                    rows = slice(r * SB_ROWS, (r + 1) * SB_ROWS)
                    before = _sb_chunk_mask(diagonal, r, s)
                    if isinstance(before, str):
                        a_scr[rows, cols] = jnp.zeros((SB_ROWS, SB_SUB), BF16)
                        continue
                    _, log_beta, log_1m = _sb_terms(z_scr[rows, cols] * SB_SCALE, before)
                    tail, total = _sums(log_1m, after)
                    a = jnp.exp(log_beta + c_rows[r] + tail)
                    if before is not None:
                        a = jnp.where(before, a, 0.0)
                    a_scr[rows, cols] = a.astype(BF16)
                    c_rows[r] = c_rows[r] + total
            acc = acc + jnp.dot(a_scr[:, :kw], v_w, preferred_element_type=F32)
            return jnp.concatenate(c_rows, axis=0), acc

        c_sum, acc = span(i * bq, bq, jnp.zeros((bq, SB_SUB), F32), jnp.zeros((bq, HEAD_DIM), F32), True)

        def more(state):
            done, c, _ = state
            return (done < i * (bq // bk)) & (jnp.max(c) > SB_DEAD)

        def step(state):
            done, c, a = state
            c, a = span(i * bq - (done + 1) * bk, bk, c, a, False)
            return done + 1, c, a

        done, c_sum, acc = lax.while_loop(more, step, (jnp.int32(0), c_sum, acc))
        o_ref[...] = acc.astype(BF16)
        l_ref[...] = c_sum
        n_ref[...] = jnp.full((8, 128), done.astype(F32), F32)

    nq = n // bq
    return _call(
        body, [q, kv, kv], name=name, grid=(N_HEADS, nq),
        in_specs=[pl.BlockSpec((bq, HEAD_DIM), lambda h, i: (i, h)),
                  pl.BlockSpec((n, HEAD_DIM), lambda h, i: (0, h)),
                  pl.BlockSpec((n, HEAD_DIM), lambda h, i: (0, N_HEADS + h))],
        out_specs=[pl.BlockSpec((bq, HEAD_DIM), lambda h, i: (i, h)),
                   pl.BlockSpec((None, bq, 128), lambda h, i: (h, i, 0)),
                   pl.BlockSpec((None, None, 8, 128), lambda h, i: (h, i, 0, 0))],
        out_shape=[jax.ShapeDtypeStruct((n, D_MODEL), BF16), jax.ShapeDtypeStruct((N_HEADS, n, 128), F32),
                   jax.ShapeDtypeStruct((N_HEADS, nq, 8, 128), F32)],
        scratch_shapes=[pltpu.VMEM((bq, bq), F32), pltpu.VMEM((bq, bq), BF16)],
        compiler_params=_params("arbitrary", "arbitrary"),
        ride=ride, first=lambda: (pl.program_id(0) == 0) & (pl.program_id(1) == 0),
        last=lambda: (pl.program_id(0) == N_HEADS - 1) & (pl.program_id(1) == nq - 1))


def _sb_attention_bwd(q, kv, do, lsum, spans, name, bq=SB_BQ, ride=None):
    n = q.shape[0]
    bq = min(bq, n)
    bk = min(SB_BK, bq)
    nrc = bq // SB_ROWS

    def body(q_ref, k_ref, v_ref, do_ref, l_ref, n_ref, dq_ref, dk_ref, dv_ref, z_scr, da_scr, a_scr, dz_scr):
        i = pl.program_id(1)

        @pl.when(i == 0)
        def _():
            dk_ref[...] = jnp.zeros_like(dk_ref)
            dv_ref[...] = jnp.zeros_like(dv_ref)

        qb = q_ref[...]
        dob = do_ref[...]
        lt_rows = [l_ref[r * SB_ROWS:(r + 1) * SB_ROWS, :] for r in range(nrc)]
        upto = _tri_ones(SB_SUB, lambda r, c: r <= c)
        below = _tri_ones(SB_SUB, lambda r, c: r < c)

        def span(start, kw, l_sum, g_sum, dq, diagonal):
            start = pl.multiple_of(start, kw)
            k_w = k_ref[pl.ds(start, kw), :]
            v_w = v_ref[pl.ds(start, kw), :]
            z_scr[:, :kw] = lax.dot_general(qb, k_w, _dims(False, True), preferred_element_type=F32)
            da_scr[:, :kw] = lax.dot_general(dob, v_w, _dims(False, True), preferred_element_type=F32)
            l_rows = [l_sum[r * SB_ROWS:(r + 1) * SB_ROWS] for r in range(nrc)]
            g_rows = [g_sum[r * SB_ROWS:(r + 1) * SB_ROWS] for r in range(nrc)]
            for s in range(kw // SB_SUB):
                cols = slice(s * SB_SUB, (s + 1) * SB_SUB)
                for r in range(nrc):
                    rows = slice(r * SB_ROWS, (r + 1) * SB_ROWS)
                    before = _sb_chunk_mask(diagonal, r, s)
                    if isinstance(before, str):
                        a_scr[rows, cols] = jnp.zeros((SB_ROWS, SB_SUB), BF16)
                        dz_scr[rows, cols] = jnp.zeros((SB_ROWS, SB_SUB), BF16)
                        continue
                    zs = z_scr[rows, cols] * SB_SCALE
                    e, log_beta, log_1m = _sb_terms(zs, before)
                    l_prefix, l_total = _sums(log_1m, upto)
                    a = jnp.exp(log_beta + (lt_rows[r] - (l_rows[r] + l_prefix)))
                    if before is not None:
                        a = jnp.where(before, a, 0.0)
                    g = a * da_scr[rows, cols]
                    g_prefix, g_total = _sums(g, below)
                    inv = 1.0 / (1.0 + e)
                    beta = jnp.where(zs >= 0, inv, e * inv)
                    dz = (g - (g + g_rows[r] + g_prefix) * beta) * SB_SCALE
                    if before is not None:
                        dz = jnp.where(before, dz, 0.0)
                    a_scr[rows, cols] = a.astype(BF16)
                    dz_scr[rows, cols] = dz.astype(BF16)
                    l_rows[r] = l_rows[r] + l_total
                    g_rows[r] = g_rows[r] + g_total
            dz_w = dz_scr[:, :kw]
            dq = dq + jnp.dot(dz_w, k_w, preferred_element_type=F32)
            dk_ref[pl.ds(start, kw), :] += lax.dot_general(dz_w, qb, _dims(True, False), preferred_element_type=F32)
            dv_ref[pl.ds(start, kw), :] += lax.dot_general(a_scr[:, :kw], dob, _dims(True, False),
                                                           preferred_element_type=F32)
            return jnp.concatenate(l_rows, axis=0), jnp.concatenate(g_rows, axis=0), dq

        taken = jnp.clip(jnp.max(n_ref[...]).astype(jnp.int32), 0, i * (bq // bk))
        zero = jnp.zeros((bq, SB_SUB), F32)
        carry = lax.fori_loop(0, taken, lambda j, c: span(i * bq - (taken - j) * bk, bk, c[0], c[1], c[2], False),
                              (zero, zero, jnp.zeros((bq, HEAD_DIM), F32)))
        _, _, dq = span(i * bq, bq, carry[0], carry[1], carry[2], True)
        dq_ref[...] = dq.astype(BF16)

    nq = n // bq
    return _call(
        body, [q, kv, kv, do, lsum, spans], name=name, grid=(N_HEADS, nq),
        in_specs=[pl.BlockSpec((bq, HEAD_DIM), lambda h, i: (i, h)),
                  pl.BlockSpec((n, HEAD_DIM), lambda h, i: (0, h)),
                  pl.BlockSpec((n, HEAD_DIM), lambda h, i: (0, N_HEADS + h)),
                  pl.BlockSpec((bq, HEAD_DIM), lambda h, i: (i, h)),
                  pl.BlockSpec((None, bq, 128), lambda h, i: (h, i, 0)),
                  pl.BlockSpec((None, None, 8, 128), lambda h, i: (h, i, 0, 0))],
        out_specs=[pl.BlockSpec((bq, HEAD_DIM), lambda h, i: (i, h)),
                   pl.BlockSpec((n, HEAD_DIM), lambda h, i: (0, h)),
                   pl.BlockSpec((n, HEAD_DIM), lambda h, i: (0, h))],
        out_shape=[jax.ShapeDtypeStruct((n, D_MODEL), BF16), jax.ShapeDtypeStruct((n, D_MODEL), F32),
                   jax.ShapeDtypeStruct((n, D_MODEL), F32)],
        scratch_shapes=[pltpu.VMEM((bq, bq), F32), pltpu.VMEM((bq, bq), F32), pltpu.VMEM((bq, bq), BF16),
                        pltpu.VMEM((bq, bq), BF16)],
        compiler_params=_params("arbitrary", "arbitrary"),
        ride=ride, first=lambda: (pl.program_id(0) == 0) & (pl.program_id(1) == 0),
        last=lambda: (pl.program_id(0) == N_HEADS - 1) & (pl.program_id(1) == nq - 1))


def _local_step(x, target, gains, comm):
    g = gains
    w = comm.w
    big = {}
    row = lambda a, i: a[i:i + 1, :]

    proj, h0 = _norm_matmul(x, row(g["mix_pre"], 0), w["gdn_in"], F32, "gdn_in_proj", tn=1408)
    qn, kn, v, bg = _gdn_pre(proj, w["conv"], g["alog"], g["dtb"], "gdn_pre")
    (o_gdn, saved, inverses), got = _gdn_scan(qn, kn, v, bg, "gdn_scan", ride=comm.ride("scan"))
    comm.done("scan", got)
    x1, y_mix0, a_gdn = _out_proj((o_gdn, proj), True, x, w["gdn_out"], row(g["mix_post"], 0), g["out_gain"],
                                  "gdn_out_proj", tm=512)
    x2, y_mlp0, h_mlp0, u0, a0 = _mlp_fwd(x1, row(g["mlp_pre"], 0), w["up0"], w["down0"], row(g["mlp_post"], 0), "mlp0")
    kv, h_kv = _norm_matmul(x2, g["kv"], w["kv"], BF16, "kv_proj", tm=2048)
    q, h_q = _norm_matmul(x2, row(g["mix_pre"], 1), w["sb_q"], BF16, "sb_q_proj")
    (o_sb, lsum, spans), got = _sb_attention(q, kv, "sb_attention", ride=comm.ride("sb"))
    comm.done("sb", got)
    x3, y_mix1 = _out_proj((o_sb,), False, x2, w["sb_o"], row(g["mix_post"], 1), None, "sb_out_proj")
    dx4, y_mlp1, h_mlp1, u1, a1, loss = _mlp_fwd(x3, row(g["mlp_pre"], 1), w["up1"], w["down1"], row(g["mlp_post"], 1),
                                                 "mlp1_loss", target=target)

    dx3, dy_mlp1, du1, dg_mlp1 = _mlp_bwd(dx4, y_mlp1, row(g["mlp_post"], 1), w["down1"], u1, w["up1"], x3,
                                          row(g["mlp_pre"], 1), "mlp1_bwd")
    big["down1"] = _matmul_tn(a1, dy_mlp1, 1, "d_down1")[0]
    big["up1"] = _matmul_tn(h_mlp1, du1, N_DEV, "d_up1", tk=2048)
    dy_mix1, dg_post1, do_sb = _out_proj_bwd(dx3, y_mix1, w["sb_o"], row(g["mix_post"], 1), None, "sb_out_proj_bwd")
    big["sb_o"] = _matmul_tn(o_sb, dy_mix1, 1, "d_sb_o")[0]
    (dq, dk, dv), got = _sb_attention_bwd(q, kv, do_sb, lsum, spans, "sb_attention_bwd", ride=comm.ride("sb_bwd", big))
    comm.done("sb_bwd", got)
    big["sb_q"] = _matmul_tn(h_q, dq, 1, "d_sb_q")[0]
    big["kv"] = jnp.concatenate([_matmul_tn(h_kv, dk, N_DEV // 2, "d_w_k", tk=2048),
                                 _matmul_tn(h_kv, dv, N_DEV // 2, "d_w_v", tk=2048)],
                                axis=0)
    dx2, dg_x2 = _norm_matmul_bwd([(dq, w["sb_q"], row(g["mix_pre"], 1), 1024), (dk, w["kv"][0:1], g["kv"], 1024),
                                   (dv, w["kv"][1:2], g["kv"], 1024)], x2, dx3, "qkv_proj_bwd", tm=512)

    dx1, dy_mlp0, du0, dg_mlp0 = _mlp_bwd(dx2, y_mlp0, row(g["mlp_post"], 0), w["down0"], u0, w["up0"], x1,
                                          row(g["mlp_pre"], 0), "mlp0_bwd")
    big["down0"] = _matmul_tn(a0, dy_mlp0, 1, "d_down0")[0]
    big["up0"] = _matmul_tn(h_mlp0, du0, N_DEV, "d_up0", tk=2048)
    dy_mix0, dg_post0, do_gdn, dgate, d_out_gain = _out_proj_bwd(
        dx1, y_mix0, w["gdn_out"], row(g["mix_post"], 0), (o_gdn, proj, g["out_gain"]), "gdn_out_proj_bwd", tm=512)
    big["gdn_out"] = _matmul_tn(a_gdn, dy_mix0, 1, "d_gdn_out")[0]
    (dqn, dkn, dvv, dbg), got = _gdn_scan_bwd(qn, kn, v, bg, saved, inverses, do_gdn, "gdn_scan_bwd",
                                              ride=comm.ride("scan_bwd", big))
    comm.done("scan_bwd", got)
    dproj, d_conv, d_gates = _gdn_pre_bwd(proj, w["conv"], g["alog"], g["dtb"], dqn, dkn, dvv, dbg, dgate, "gdn_pre_bwd")
    big["gdn_in"] = _matmul_tn(h0, dproj, 1, "d_gdn_in", tb=1408, tk=512)[0]
    ride = comm.ride("in_bwd", big)
    grad_x, dg_pre0, *got = _norm_matmul_bwd([(dproj, w["gdn_in"], row(g["mix_pre"], 0), 1408)], x, dx1, "gdn_in_proj_bwd",
                                             ride=ride)
    comm.done("in_bwd", got[0] if got else [])

    small = {"pre0": dg_pre0, "x2": dg_x2, "post0": dg_post0, "post1": dg_post1, "mlp0": dg_mlp0, "mlp1": dg_mlp1,
             "gates": d_gates, "out_gain": d_out_gain, "conv": d_conv, "loss": loss}
    return grad_x, big, small


class _LocalOnly:
    def __init__(self, weights):
        self.w = weights

    def ride(self, stage, grads=None):
        return None

    def done(self, stage, outs):
        pass


GATHER_STAGES = {"scan": ("gdn_out", "up0", "down0", "kv", "sb_q"), "sb": ("sb_o", "up1", "down1")}
SCATTER_STAGES = {"sb_bwd": ("up1", "down1", "sb_o"), "scan_bwd": ("sb_q", "kv", "up0", "down0", "gdn_out"),
                  "in_bwd": ("gdn_in",)}


def _whole_weight(name, gathered):
    d = D_MODEL
    if name in ("up0", "up1"):
        return gathered
    if name == "kv":
        return gathered.reshape(2, N_DEV // 2, d, gathered.shape[2]).transpose(0, 2, 1, 3).reshape(2, d, d)
    if name == "gdn_in":
        whole = gathered.transpose(1, 0, 2).reshape(d, GDN_IN_COLS)
        return jnp.pad(whole, ((0, 0), (0, GDN_IN_PAD - GDN_IN_COLS)))[None]
    if name == "conv":
        return gathered.transpose(1, 0, 2).reshape(CONV_WIDTH, QKV)
    whole = gathered.reshape(gathered.shape[0] * gathered.shape[1], d)
    return whole[None] if name == "sb_q" else whole


def _owner_blocks(name, grad):
    if name in ("up0", "up1", "kv"):
        blocks = grad
    elif name == "gdn_in":
        blocks = grad[:, :GDN_IN_COLS].reshape(D_MODEL, N_DEV, GDN_IN_COLS // N_DEV).transpose(1, 0, 2)
    else:
        blocks = grad.reshape(N_DEV, grad.shape[0] // N_DEV, grad.shape[1])
    return blocks.astype(BF16)


class _Fsdp:
    def __init__(self, shards, weights):
        self.shards, self.w, self.landed = shards, weights, {}

    def ride(self, stage, grads=None):
        if stage in GATHER_STAGES:
            names = GATHER_STAGES[stage]
            return _Exchange([self.shards[nm] for nm in names], [True] * len(names))
        names = SCATTER_STAGES[stage]
        return _Exchange([_owner_blocks(nm, grads[nm]) for nm in names], [False] * len(names))

    def done(self, stage, outs):
        if stage in GATHER_STAGES:
            for nm, out in zip(GATHER_STAGES[stage], outs):
                self.w[nm] = _whole_weight(nm, out)
        else:
            self.landed.update(zip(SCATTER_STAGES[stage], outs))


def _my_place():
    return lax.axis_index("x"), lax.axis_index("y"), lax.axis_index("c")


class _Exchange:
    def __init__(self, arrays, gather):
        self.arrays, self.gather, self.n = list(arrays), list(gather), len(arrays)
        any_spec = pl.BlockSpec(memory_space=pl.ANY)
        self.in_specs = [any_spec] * self.n
        self.out_specs = [any_spec] * self.n
        self.out_shape = [jax.ShapeDtypeStruct(((N_DEV,) + a.shape) if g else a.shape, a.dtype)
                          for a, g in zip(self.arrays, self.gather)]
        self.scratch = [pltpu.SemaphoreType.DMA((self.n, N_DEV - 1)), pltpu.SemaphoreType.DMA((self.n, N_DEV - 1)),
                        pltpu.SemaphoreType.DMA((self.n,))]

    def _copies(self, ins, outs, sems):
        send_sems, recv_sems, local_sems = sems
        x, y, c = _my_place()
        me = 4 * x + 2 * y + c
        copies = []
        for a in range(self.n):
            src = ins[a] if self.gather[a] else ins[a].at[me]
            copies.append(pltpu.make_async_copy(src, outs[a].at[me], local_sems.at[a]))
        for k in range(1, N_DEV):
            px = 1 - x if k & 4 else x
            py = 1 - y if k & 2 else y
            pc = 1 - c if k & 1 else c
            peer = 4 * px + 2 * py + pc
            for a in range(self.n):
                src = ins[a] if self.gather[a] else ins[a].at[peer]
                copies.append(pltpu.make_async_remote_copy(
                    src_ref=src, dst_ref=outs[a].at[me], send_sem=send_sems.at[a, k - 1], recv_sem=recv_sems.at[a, k - 1],
                    device_id=(px, py, pc), device_id_type=MESH))
        return copies

    def start(self, ins, outs, sems):
        for cp in self._copies(ins, outs, sems):
            cp.start()

    def wait(self, ins, outs, sems):
        for cp in self._copies(ins, outs, sems):
            cp.wait()


def _call(body, operands, *, name, out_shape, in_specs, out_specs, grid=(), scratch_shapes=(), compiler_params=None,
          ride=None, first=None, last=None):
    n_in, n_out, n_scr = len(operands), len(out_shape), len(scratch_shapes)
    if ride is None:
        outs = pl.pallas_call(body, name=name, grid=grid, in_specs=in_specs, out_specs=out_specs, out_shape=out_shape,
                              scratch_shapes=scratch_shapes, compiler_params=compiler_params)(*operands)
        return list(outs), []
    r = ride.n

    def riding(*refs):
        ins, r_ins = refs[:n_in], refs[n_in:n_in + r]
        outs, r_outs = refs[n_in + r:n_in + r + n_out], refs[n_in + r + n_out:n_in + 2 * r + n_out]
        scr, sems = refs[n_in + 2 * r + n_out:n_in + 2 * r + n_out + n_scr], refs[n_in + 2 * r + n_out + n_scr:]

        @pl.when(first())
        def _():
            ride.start(r_ins, r_outs, sems)

        body(*ins, *outs, *scr)

        @pl.when(last())
        def _():
            ride.wait(r_ins, r_outs, sems)

    outs = pl.pallas_call(
        riding, name=name, grid=grid, in_specs=list(in_specs) + ride.in_specs, out_specs=list(out_specs) + ride.out_specs,
        out_shape=list(out_shape) + ride.out_shape, scratch_shapes=list(scratch_shapes) + ride.scratch,
        compiler_params=compiler_params)(*operands, *ride.arrays)
    return list(outs[:n_out]), list(outs[n_out:])


def _exchange(arrays, gather, name):
    ride = _Exchange(arrays, gather)

    def body(*refs):
        ins, outs, sems = refs[:ride.n], refs[ride.n:2 * ride.n], refs[2 * ride.n:]
        ride.start(ins, outs, sems)
        ride.wait(ins, outs, sems)

    return pl.pallas_call(body, name=name, in_specs=ride.in_specs, out_specs=ride.out_specs, out_shape=ride.out_shape,
                          scratch_shapes=ride.scratch)(*arrays)


def _gather_two_level(arrays, name):
    n_arr = len(arrays)

    def body(*refs):
        ins, outs = refs[:n_arr], refs[n_arr:2 * n_arr]
        send_sems, recv_sems, local_sems = refs[2 * n_arr:]
        x, y, c = _my_place()
        sibling = (x, y, 1 - c)
        chips = [(1 - x, y), (x, 1 - y), (1 - x, 1 - y)]
        index = lambda px, py, pc: 4 * px + 2 * py + pc

        def copy(a, k, block, to, src=None):
            dst = outs[a].at[index(*block)]
            return pltpu.make_async_remote_copy(src_ref=dst if src is None else src, dst_ref=dst,
                                                send_sem=send_sems.at[a, k], recv_sem=recv_sems.at[a, k],
                                                device_id=to, device_id_type=MESH)

        mine = [pltpu.make_async_copy(ins[a], outs[a].at[index(x, y, c)], local_sems.at[a]) for a in range(n_arr)]
        first = [copy(a, 0, (x, y, c), sibling, src=ins[a]) for a in range(n_arr)]
        first += [copy(a, 1 + j, (x, y, c), (*chip, c), src=ins[a]) for j, chip in enumerate(chips) for a in range(n_arr)]
        for cp in mine + first:
            cp.start()
        passed = []
        for j, chip in enumerate(chips):
            for a in range(n_arr):
                copy(a, 1 + j, (*chip, c), (x, y, c)).wait_recv()
                passed.append(copy(a, 4 + j, (*chip, c), sibling))
                passed[-1].start()
        for a in range(n_arr):
            copy(a, 0, sibling, (x, y, c)).wait_recv()
            for j, chip in enumerate(chips):
                copy(a, 4 + j, (*chip, 1 - c), (x, y, c)).wait_recv()
        for cp in first + passed:
            cp.wait_send()
        for cp in mine:
            cp.wait()

    any_spec = pl.BlockSpec(memory_space=pl.ANY)
    return pl.pallas_call(
        body, name=name, in_specs=[any_spec] * n_arr, out_specs=[any_spec] * n_arr,
        out_shape=[jax.ShapeDtypeStruct((N_DEV,) + a.shape, a.dtype) for a in arrays],
        scratch_shapes=[pltpu.SemaphoreType.DMA((n_arr, N_DEV - 1)), pltpu.SemaphoreType.DMA((n_arr, N_DEV - 1)),
                        pltpu.SemaphoreType.DMA((n_arr,))])(*arrays)


def _adamw_math(g, w, m, v):
    m = ADAM_B1 * m + (1.0 - ADAM_B1) * g
    v = ADAM_B2 * v + (1.0 - ADAM_B2) * jnp.square(g)
    m_hat = m / (1.0 - ADAM_B1 ** ADAM_STEP)
    v_hat = v / (1.0 - ADAM_B2 ** ADAM_STEP)
    delta = -ADAM_LR * (m_hat / (jnp.sqrt(v_hat) + ADAM_EPS) + ADAM_WD * w)
    return delta, m, v


def _sum_devices(ref):
    total = ref[0].astype(F32)
    for d in range(1, N_DEV):
        total = total + ref[d].astype(F32)
    return total


def _reduce_adamw(landed, w, m, v, name, tr=256):
    r, c = w.shape
    tr = min(tr, r)
    assert r % tr == 0

    def body(l_ref, w_ref, m_ref, v_ref, g_ref, d_ref, nm_ref, nv_ref):
        g = _sum_devices(l_ref)
        g_ref[...] = g
        d_ref[...], nm_ref[...], nv_ref[...] = _adamw_math(g, w_ref[...], m_ref[...], v_ref[...])

    blk = pl.BlockSpec((tr, c), lambda i: (i, 0))
    return pl.pallas_call(
        body, name=name, grid=(r // tr,),
        in_specs=[pl.BlockSpec((N_DEV, tr, c), lambda i: (0, i, 0)), blk, blk, blk],
        out_specs=[blk] * 4, out_shape=[jax.ShapeDtypeStruct((r, c), F32)] * 4,
        compiler_params=_params("parallel"),
    )(landed, w, m, v)


def _reduce_adamw_layers(landed, w, m, v, name, tr=256):
    _, r, c = w.shape
    tr = min(tr, r)
    nr = r // tr
    assert r % tr == 0

    def body(l0_ref, l1_ref, w_ref, m_ref, v_ref, g_ref, d_ref, nm_ref, nv_ref):
        g = jnp.where(pl.program_id(0) == 0, _sum_devices(l0_ref), _sum_devices(l1_ref))
        g_ref[...] = g
        d_ref[...], nm_ref[...], nv_ref[...] = _adamw_math(g, w_ref[...], m_ref[...], v_ref[...])

    blk = pl.BlockSpec((None, tr, c), lambda l, i: (l, i, 0))
    return pl.pallas_call(
        body, name=name, grid=(2, nr),
        in_specs=[pl.BlockSpec((N_DEV, tr, c), lambda l, i: (0, jnp.where(l == 0, i, nr - 1), 0)),
                  pl.BlockSpec((N_DEV, tr, c), lambda l, i: (0, jnp.where(l == 1, i, 0), 0)), blk, blk, blk],
        out_specs=[blk] * 4, out_shape=[jax.ShapeDtypeStruct(w.shape, F32)] * 4,
        compiler_params=_params("arbitrary", "arbitrary"),
    )(landed[0], landed[1], w, m, v)


def _adamw(g, w, m, v, name):
    def body(g_ref, w_ref, m_ref, v_ref, d_ref, nm_ref, nv_ref):
        d_ref[...], nm_ref[...], nv_ref[...] = _adamw_math(g_ref[...], w_ref[...], m_ref[...], v_ref[...])

    return pl.pallas_call(body, name=name, out_shape=[jax.ShapeDtypeStruct(w.shape, F32)] * 3)(g, w, m, v)


def _small_update(landed, params, name):
    layout = {
        "mix_pre": [("pre0", 0), ("x2", 0)], "mix_post": [("post0", 0), ("post1", 0)],
        "mlp_pre": [("mlp0", 0), ("mlp1", 0)], "mlp_post": [("mlp0", 1), ("mlp1", 1)],
        "kv": [("kv", 0)], "alog": [("gates", 0)], "dtb": [("gates", 1)], "out_gain": [("out_gain", 0)],
    }
    landed_names = sorted(landed)
    param_names = sorted(params)
    n_l, n_p = len(landed_names), len(param_names)

    def body(*refs):
        l_refs = dict(zip(landed_names, refs[:n_l]))
        p_refs = {p: refs[n_l + 3 * i:n_l + 3 * i + 3] for i, p in enumerate(param_names)}
        outs = refs[n_l + 3 * n_p:]
        o_refs = {p: outs[4 * i:4 * i + 4] for i, p in enumerate(param_names)}
        conv_ref, loss_ref = outs[4 * n_p:]
        sums = {nm: _sum_devices(l_refs[nm]) for nm in landed_names}
        sums["kv"] = sums["x2"][1:2, :] + sums["x2"][2:3, :]
        for p in param_names:
            w_ref, m_ref, v_ref = p_refs[p]
            g_ref, d_ref, nm_ref, nv_ref = o_refs[p]
            for r, (src, src_row) in enumerate(layout[p]):
                g = sums[src][src_row:src_row + 1, :]
                g_ref[r:r + 1, :] = g
                d, nm, nv = _adamw_math(g, w_ref[r:r + 1, :], m_ref[r:r + 1, :], v_ref[r:r + 1, :])
                d_ref[r:r + 1, :] = d
                nm_ref[r:r + 1, :] = nm
                nv_ref[r:r + 1, :] = nv
        conv_ref[...] = sums["conv"]
        loss_ref[...] = sums["loss"]

    args = [landed[nm] for nm in landed_names]
    out_shape = []
    for p in param_names:
        args += list(params[p])
        out_shape += [jax.ShapeDtypeStruct(params[p][0].shape, F32)] * 4
    out_shape += [jax.ShapeDtypeStruct(landed["conv"].shape[1:], F32), jax.ShapeDtypeStruct(landed["loss"].shape[1:], F32)]
    outs = pl.pallas_call(body, name=name, out_shape=out_shape)(*args)
    result = {p: tuple(outs[4 * i:4 * i + 4]) for i, p in enumerate(param_names)}
    result["conv"], result["loss"] = outs[4 * n_p], outs[4 * n_p + 1]
    return result


def _lanes(vec, offset):
    return jnp.pad(vec[None, :], ((0, 0), (offset, 128 - offset - vec.shape[0])))


def kernel(x, mix_pre_gain, mix_post_gain, mlp_pre_gain, mlp_post_gain, mlp_w_up, mlp_w_down, gdn_w_in, gdn_conv_w, gdn_a_log, gdn_dt_bias, gdn_out_gain, gdn_w_out, kv_gain, w_kv, sb_w_q, sb_w_o, loss_target, m_mix_pre_gain, m_mix_post_gain, m_mlp_pre_gain, m_mlp_post_gain, m_mlp_w_up, m_mlp_w_down, m_gdn_w_in, m_gdn_conv_w, m_gdn_a_log, m_gdn_dt_bias, m_gdn_out_gain, m_gdn_w_out, m_kv_gain, m_w_kv, m_sb_w_q, m_sb_w_o, v_mix_pre_gain, v_mix_post_gain, v_mlp_pre_gain, v_mlp_post_gain, v_mlp_w_up, v_mlp_w_down, v_gdn_w_in, v_gdn_conv_w, v_gdn_a_log, v_gdn_dt_bias, v_gdn_out_gain, v_gdn_w_out, v_kv_gain, v_w_kv, v_sb_w_q, v_sb_w_o):
    me = 4 * lax.axis_index("x") + 2 * lax.axis_index("y") + lax.axis_index("c")
    bf = lambda a: a.astype(BF16)

    shards = {"up0": bf(mlp_w_up[0]), "up1": bf(mlp_w_up[1]), "down0": bf(mlp_w_down[0]), "down1": bf(mlp_w_down[1]),
              "gdn_out": bf(gdn_w_out[0]), "kv": bf(w_kv), "sb_q": bf(sb_w_q[0]), "sb_o": bf(sb_w_o[0])}
    gdn_in, conv = _gather_two_level([bf(gdn_w_in[0]), gdn_conv_w[0]], "gather_first_weights")
    comm = _Fsdp(shards, {"gdn_in": _whole_weight("gdn_in", gdn_in), "conv": _whole_weight("conv", conv)})
    gains = {"mix_pre": mix_pre_gain, "mix_post": mix_post_gain, "mlp_pre": mlp_pre_gain, "mlp_post": mlp_post_gain,
             "kv": kv_gain[None, :], "alog": _lanes(gdn_a_log[0], N_HEADS), "dtb": _lanes(gdn_dt_bias[0], N_HEADS),
             "out_gain": gdn_out_gain}

    grad_x, big, small = _local_step(x[0], loss_target[0], gains, comm)

    small_names = ["pre0", "x2", "post0", "post1", "mlp0", "mlp1", "gates", "out_gain", "conv", "loss"]
    landed = _exchange([small[nm] for nm in small_names], [True] * len(small_names), "exchange_small_grads")
    small_landed = dict(zip(small_names, landed))

    results = {}
    results["mlp_w_up"] = _reduce_adamw_layers((comm.landed["up0"], comm.landed["up1"]), mlp_w_up, m_mlp_w_up, v_mlp_w_up,
                                               "adamw_mlp_w_up")
    results["mlp_w_down"] = _reduce_adamw_layers((comm.landed["down0"], comm.landed["down1"]), mlp_w_down, m_mlp_w_down,
                                                 v_mlp_w_down, "adamw_mlp_w_down")
    big_params = [("gdn_w_in", "gdn_in", gdn_w_in[0], m_gdn_w_in[0], v_gdn_w_in[0]),
                  ("gdn_w_out", "gdn_out", gdn_w_out[0], m_gdn_w_out[0], v_gdn_w_out[0]),
                  ("w_kv", "kv", w_kv, m_w_kv, v_w_kv), ("sb_w_q", "sb_q", sb_w_q[0], m_sb_w_q[0], v_sb_w_q[0]),
                  ("sb_w_o", "sb_o", sb_w_o[0], m_sb_w_o[0], v_sb_w_o[0])]
    for nm, short, w_, m_, v_ in big_params:
        results[nm] = _reduce_adamw(comm.landed[short], w_, m_, v_, "adamw_" + nm)
    lanes8 = lambda a: _lanes(a[0], N_HEADS)
    small_params = {
        "mix_pre": (mix_pre_gain, m_mix_pre_gain, v_mix_pre_gain), "mix_post": (mix_post_gain, m_mix_post_gain, v_mix_post_gain),
        "mlp_pre": (mlp_pre_gain, m_mlp_pre_gain, v_mlp_pre_gain), "mlp_post": (mlp_post_gain, m_mlp_post_gain, v_mlp_post_gain),
        "kv": (kv_gain[None, :], m_kv_gain[None, :], v_kv_gain[None, :]),
        "alog": (lanes8(gdn_a_log), lanes8(m_gdn_a_log), lanes8(v_gdn_a_log)),
        "dtb": (lanes8(gdn_dt_bias), lanes8(m_gdn_dt_bias), lanes8(v_gdn_dt_bias)),
        "out_gain": (gdn_out_gain, m_gdn_out_gain, v_gdn_out_gain),
    }
    sm = _small_update(small_landed, small_params, "small_update")
    conv_cols = QKV // N_DEV
    g_conv = lax.dynamic_slice(sm["conv"], (0, me * conv_cols), (8, conv_cols))[:CONV_WIDTH]
    conv_res = (g_conv,) + tuple(_adamw(g_conv, gdn_conv_w[0], m_gdn_conv_w[0], v_gdn_conv_w[0], "adamw_conv"))

    lead = lambda t: tuple(a[None] for a in t)
    heads8 = lambda t: tuple(a[:, N_HEADS:2 * N_HEADS] for a in t)
    per_weight = [
        sm["mix_pre"], sm["mix_post"], sm["mlp_pre"], sm["mlp_post"],
        tuple(results["mlp_w_up"]), tuple(results["mlp_w_down"]),
        lead(results["gdn_w_in"]), lead(conv_res), heads8(sm["alog"]), heads8(sm["dtb"]), sm["out_gain"],
        lead(results["gdn_w_out"]), tuple(a[0] for a in sm["kv"]), results["w_kv"], lead(results["sb_w_q"]), lead(results["sb_w_o"]),
    ]
    grads, deltas, new_ms, new_vs = zip(*per_weight)
    return (sm["loss"][0, 0], grad_x[None], *grads, *deltas, *new_ms, *new_vs)
```

```python
import functools
import math

import jax
import jax.numpy as jnp
from jax import lax
from jax.experimental import pallas as pl
from jax.experimental.pallas import tpu as pltpu

F32 = jnp.float32
BF16 = jnp.bfloat16

N_DEV = 8
D_MODEL = 1024
D_FF = 4096
N_HEADS = 8
HEAD_DIM = 128
CHUNK = 64
CONV_WIDTH = 4
GDN_IN_COLS = 4 * D_MODEL + 2 * N_HEADS
GDN_IN_PAD = 4 * D_MODEL + 128
EPS = 1e-6

ADAM_LR = 0.001
ADAM_B1 = 0.9
ADAM_B2 = 0.999
ADAM_EPS = 1e-08
ADAM_WD = 0.01
ADAM_STEP = 10

VMEM_LIMIT_BYTES = 56 * 1024 * 1024
MESH = pl.DeviceIdType.MESH


def _params(*semantics):
    return pltpu.CompilerParams(dimension_semantics=semantics, vmem_limit_bytes=VMEM_LIMIT_BYTES)


def _dims(ta, tb):
    return (((0,) if ta else (1,), (1,) if tb else (0,)), ((), ()))


def _dot(a, b, ta=False, tb=False):
    return lax.dot_general(a.astype(BF16), b.astype(BF16), _dims(ta, tb), preferred_element_type=F32)


def _dot_f32(a, b, ta=False, tb=False):
    return lax.dot_general(a, b, _dims(ta, tb), precision=lax.Precision.HIGHEST, preferred_element_type=F32)


def _dot_3x(a, b, ta=False, tb=False):
    return lax.dot_general(a, b, _dims(ta, tb), precision=lax.Precision.HIGH, preferred_element_type=F32)


def _make_mm(dot):
    @functools.partial(jax.custom_vjp, nondiff_argnums=(2, 3))
    def mm(a, b, ta, tb):
        return dot(a, b, ta, tb)

    def fwd(a, b, ta, tb):
        return dot(a, b, ta, tb), (a, b)

    def bwd(ta, tb, res, g):
        a, b = res
        if not ta and not tb:
            return mm(g, b, False, True), mm(a, g, True, False)
        if not ta and tb:
            return mm(g, b, False, False), mm(g, a, True, False)
        if ta and not tb:
            return mm(b, g, False, True), mm(a, g, False, False)
        raise NotImplementedError

    mm.defvjp(fwd, bwd)
    return mm


_mm = _make_mm(_dot)
_mm_f32 = _make_mm(_dot_f32)


def _rms(x, gain):
    r = lax.rsqrt(jnp.mean(x * x, axis=-1, keepdims=True) + EPS)
    return x * r * gain


def _rms_bwd(x, gain, dy):
    r = lax.rsqrt(jnp.mean(x * x, axis=-1, keepdims=True) + EPS)
    xh = x * r
    dgain = jnp.sum(dy * xh, axis=0, keepdims=True)
    dxh = dy * gain
    dx = r * (dxh - xh * jnp.mean(dxh * xh, axis=-1, keepdims=True))
    return dx, dgain


def _silu(x):
    return x / (1.0 + jnp.exp(-x))


def _acc_rows(ref, first, rows):
    @pl.when(first)
    def _():
        ref[...] = jnp.zeros_like(ref)

    for r, val in enumerate(rows):
        ref[r:r + 1, :] += val


def _norm_matmul(x, gain, w, out_dtype, name, tm=1024, tn=1024):
    n, d = x.shape
    g, _, wc = w.shape
    tm, tn = min(tm, n), min(tn, wc)
    per = wc // tn
    assert n % tm == 0 and wc % tn == 0

    def body(x_ref, gain_ref, w_ref, out_ref, h_ref):
        @pl.when(pl.program_id(1) == 0)
        def _():
            h_ref[...] = _rms(x_ref[...], gain_ref[...]).astype(BF16)

        out_ref[...] = jnp.dot(h_ref[...], w_ref[...], preferred_element_type=F32).astype(out_dtype)

    return pl.pallas_call(
        body, name=name, grid=(n // tm, g * per),
        in_specs=[pl.BlockSpec((tm, d), lambda i, j: (i, 0)),
                  pl.BlockSpec((1, d), lambda i, j: (0, 0)),
                  pl.BlockSpec((None, d, tn), lambda i, j: (j // per, 0, j % per))],
        out_specs=[pl.BlockSpec((tm, tn), lambda i, j: (i, j)),
                   pl.BlockSpec((tm, d), lambda i, j: (i, 0))],
        out_shape=[jax.ShapeDtypeStruct((n, g * wc), out_dtype), jax.ShapeDtypeStruct((n, d), BF16)],
        compiler_params=_params("parallel", "arbitrary"),
    )(x, gain, w)


def _norm_matmul_bwd(branches, x, add, name, tm=1024, ride=None):
    n, d = x.shape
    tm = min(tm, n)
    tns = [min(tn, w.shape[2]) for _, w, _, tn in branches]
    pers = [w.shape[2] // tn for (_, w, _, _), tn in zip(branches, tns)]
    ncols = [w.shape[0] * per for (_, w, _, _), per in zip(branches, pers)]
    offs = [sum(ncols[:b]) for b in range(len(branches))]
    total = sum(ncols)
    nb = len(branches)

    def body(*refs):
        dout_refs, w_refs, gain_refs = refs[:nb], refs[nb:2 * nb], refs[2 * nb:3 * nb]
        x_ref, add_ref, dx_ref, dgain_ref, acc_ref, sum_ref = refs[3 * nb:]
        i, j = pl.program_id(0), pl.program_id(1)

        @pl.when((i == 0) & (j == 0))
        def _():
            dgain_ref[...] = jnp.zeros_like(dgain_ref)

        for b in range(nb):
            first, last = offs[b], offs[b] + ncols[b] - 1

            @pl.when((j >= first) & (j <= last))
            def _(b=b, first=first):
                part = _dot(dout_refs[b][...], w_refs[b][...], tb=True)

                @pl.when(j == first)
                def _():
                    acc_ref[...] = part

                @pl.when(j > first)
                def _():
                    acc_ref[...] += part

            @pl.when(j == last)
            def _(b=b):
                xv = x_ref[...]
                xh = xv * lax.rsqrt(jnp.mean(xv * xv, axis=-1, keepdims=True) + EPS)
                dh = acc_ref[...]
                dgain_ref[b:b + 1, :] += jnp.sum(dh * xh, axis=0, keepdims=True)
                scaled = dh * gain_refs[b][...]
                if b == 0:
                    sum_ref[...] = scaled
                else:
                    sum_ref[...] += scaled

        @pl.when(j == total - 1)
        def _():
            xv = x_ref[...]
            r = lax.rsqrt(jnp.mean(xv * xv, axis=-1, keepdims=True) + EPS)
            xh = xv * r
            dxh = sum_ref[...]
            dx_ref[...] = add_ref[...] + r * (dxh - xh * jnp.mean(dxh * xh, axis=-1, keepdims=True))

    def dout_spec(b):
        return pl.BlockSpec((tm, tns[b]), lambda i, j: (i, jnp.clip(j - offs[b], 0, ncols[b] - 1)))

    def w_spec(b):
        def index(i, j):
            c = jnp.clip(j - offs[b], 0, ncols[b] - 1)
            return (c // pers[b], 0, c % pers[b])
        return pl.BlockSpec((None, d, tns[b]), index)

    row = lambda i, j: (i, 0)
    const = lambda i, j: (0, 0)
    nrow = n // tm
    (dx, dgain), got = _call(
        body, [b[0] for b in branches] + [b[1] for b in branches] + [b[2] for b in branches] + [x, add],
        name=name, grid=(nrow, total),
        in_specs=[dout_spec(b) for b in range(nb)] + [w_spec(b) for b in range(nb)]
                 + [pl.BlockSpec((1, d), const)] * nb + [pl.BlockSpec((tm, d), row), pl.BlockSpec((tm, d), row)],
        out_specs=[pl.BlockSpec((tm, d), row), pl.BlockSpec((8, d), const)],
        out_shape=[jax.ShapeDtypeStruct((n, d), F32), jax.ShapeDtypeStruct((8, d), F32)],
        scratch_shapes=[pltpu.VMEM((tm, d), F32), pltpu.VMEM((tm, d), F32)],
        compiler_params=_params("arbitrary", "arbitrary"),
        ride=ride, first=lambda: (pl.program_id(0) == 0) & (pl.program_id(1) == 0),
        last=lambda: (pl.program_id(0) == nrow - 1) & (pl.program_id(1) == total - 1))
    return (dx, dgain) if ride is None else (dx, dgain, got)


def _matmul_tn(a, b, groups, name, ta=1024, tb=1024, tk=1024):
    n, ka = a.shape
    _, kb = b.shape
    wc = kb // groups
    ta, tb, tk = min(ta, ka), min(tb, wc), min(tk, n)
    per = wc // tb
    nk = n // tk
    assert ka % ta == 0 and wc % tb == 0 and n % tk == 0

    def body(a_ref, b_ref, out_ref, acc_ref):
        k = pl.program_id(2)

        @pl.when(k == 0)
        def _():
            acc_ref[...] = jnp.zeros_like(acc_ref)

        acc_ref[...] += _dot(a_ref[...], b_ref[...], ta=True)

        @pl.when(k == nk - 1)
        def _():
            out_ref[...] = acc_ref[...].astype(BF16)

    return pl.pallas_call(
        body, name=name, grid=(ka // ta, groups * per, nk),
        in_specs=[pl.BlockSpec((tk, ta), lambda i, j, k: (k, i)),
                  pl.BlockSpec((tk, tb), lambda i, j, k: (k, j))],
        out_specs=pl.BlockSpec((None, ta, tb), lambda i, j, k: (j // per, i, j % per)),
        out_shape=jax.ShapeDtypeStruct((groups, ka, wc), BF16),
        scratch_shapes=[pltpu.VMEM((ta, tb), F32)],
        compiler_params=_params("parallel", "parallel", "arbitrary"),
    )(a, b)


def _gated_head_norm(o, gate, out_gain):
    parts = []
    for h in range(N_HEADS):
        sl = slice(h * HEAD_DIM, (h + 1) * HEAD_DIM)
        parts.append(_rms(o[:, sl], out_gain) * _silu(gate[:, sl]))
    return parts


def _out_proj(a_inputs, gated, x_in, w, gain, out_gain, name, tm=1024):
    n, d = x_in.shape
    k = w.shape[0]
    tm = min(tm, n)

    def body(*refs):
        if gated:
            o_ref, gate_ref, og_ref, x_ref, w_ref, gain_ref, xo_ref, y_ref, a_ref = refs
            parts = _gated_head_norm(o_ref[...], gate_ref[...], og_ref[...])
            for h, part in enumerate(parts):
                a_ref[:, h * HEAD_DIM:(h + 1) * HEAD_DIM] = part.astype(BF16)
            a = a_ref[...]
        else:
            a_in_ref, x_ref, w_ref, gain_ref, xo_ref, y_ref = refs
            a = a_in_ref[...]
        y = jnp.dot(a, w_ref[...], preferred_element_type=F32)
        y_ref[...] = y
        xo_ref[...] = x_ref[...] + _rms(y, gain_ref[...])

    row = lambda i: (i, 0)
    const = lambda i: (0, 0)
    if gated:
        a_specs = [pl.BlockSpec((tm, k), row), pl.BlockSpec((tm, D_MODEL), lambda i: (i, 3)),
                   pl.BlockSpec((1, HEAD_DIM), const)]
        a_args = list(a_inputs) + [out_gain]
    else:
        a_specs = [pl.BlockSpec((tm, k), row)]
        a_args = list(a_inputs)
    out_specs = [pl.BlockSpec((tm, d), row), pl.BlockSpec((tm, d), row)]
    out_shape = [jax.ShapeDtypeStruct((n, d), F32), jax.ShapeDtypeStruct((n, d), F32)]
    if gated:
        out_specs.append(pl.BlockSpec((tm, k), row))
        out_shape.append(jax.ShapeDtypeStruct((n, k), BF16))
    return pl.pallas_call(
        body, name=name, grid=(n // tm,),
        in_specs=a_specs + [pl.BlockSpec((tm, d), row), pl.BlockSpec((k, d), const), pl.BlockSpec((1, d), const)],
        out_specs=out_specs, out_shape=out_shape,
        compiler_params=_params("parallel"),
    )(*a_args, x_in, w, gain)


def _out_proj_bwd(dxo, y, w, gain, gated_inputs, name, tm=1024):
    n, d = dxo.shape
    k = w.shape[0]
    tm = min(tm, n)
    gated = gated_inputs is not None

    def body(*refs):
        if gated:
            (dxo_ref, y_ref, w_ref, gain_ref, o_ref, gate_ref, og_ref,
             dy_ref, dgain_ref, do_ref, dgate_ref, dog_ref) = refs
        else:
            dxo_ref, y_ref, w_ref, gain_ref, dy_ref, dgain_ref, da_ref = refs
        first = pl.program_id(0) == 0
        dy, dgain = _rms_bwd(y_ref[...], gain_ref[...], dxo_ref[...])
        dy_ref[...] = dy.astype(BF16)
        _acc_rows(dgain_ref, first, [dgain])
        da = _dot(dy_ref[...], w_ref[...], tb=True)
        if not gated:
            da_ref[...] = da.astype(BF16)
            return
        og = og_ref[...]
        dog = jnp.zeros_like(og)
        for h in range(N_HEADS):
            sl = slice(h * HEAD_DIM, (h + 1) * HEAD_DIM)
            fn = lambda o_h, g_h, gn: _rms(o_h, gn) * _silu(g_h)
            _, vjp = jax.vjp(fn, o_ref[:, sl], gate_ref[:, sl], og)
            do_h, dgate_h, dog_h = vjp(da[:, sl])
            do_ref[:, sl] = do_h
            dgate_ref[:, sl] = dgate_h
            dog = dog + dog_h
        _acc_rows(dog_ref, first, [dog])

    row = lambda i: (i, 0)
    const = lambda i: (0, 0)
    in_specs = [pl.BlockSpec((tm, d), row), pl.BlockSpec((tm, d), row), pl.BlockSpec((k, d), const),
                pl.BlockSpec((1, d), const)]
    args = [dxo, y, w, gain]
    out_specs = [pl.BlockSpec((tm, d), row), pl.BlockSpec((8, d), const)]
    out_shape = [jax.ShapeDtypeStruct((n, d), BF16), jax.ShapeDtypeStruct((8, d), F32)]
    if gated:
        in_specs += [pl.BlockSpec((tm, k), row), pl.BlockSpec((tm, D_MODEL), lambda i: (i, 3)),
                     pl.BlockSpec((1, HEAD_DIM), const)]
        args += list(gated_inputs)
        out_specs += [pl.BlockSpec((tm, k), row), pl.BlockSpec((tm, k), row), pl.BlockSpec((8, HEAD_DIM), const)]
        out_shape += [jax.ShapeDtypeStruct((n, k), F32), jax.ShapeDtypeStruct((n, k), F32),
                      jax.ShapeDtypeStruct((8, HEAD_DIM), F32)]
    else:
        out_specs.append(pl.BlockSpec((tm, k), row))
        out_shape.append(jax.ShapeDtypeStruct((n, k), BF16))
    return pl.pallas_call(
        body, name=name, grid=(n // tm,), in_specs=in_specs, out_specs=out_specs, out_shape=out_shape,
        compiler_params=_params("arbitrary"),
    )(*args)


def _mlp_fwd(x_in, g_pre, w_up, w_down, g_post, name, tm=1024, target=None):
    n, d = x_in.shape
    g, _, wc = w_up.shape
    tm = min(tm, n)
    with_loss = target is not None

    def body(*refs):
        if with_loss:
            x_ref, gpre_ref, wup_ref, wdown_ref, gpost_ref, t_ref, xo_ref, y_ref, h_ref, u_ref, a_ref, loss_ref, acc_ref = refs
        else:
            x_ref, gpre_ref, wup_ref, wdown_ref, gpost_ref, xo_ref, y_ref, h_ref, u_ref, a_ref, acc_ref = refs
        i, j = pl.program_id(0), pl.program_id(1)

        @pl.when(j == 0)
        def _():
            h_ref[...] = _rms(x_ref[...], gpre_ref[...]).astype(BF16)
            acc_ref[...] = jnp.zeros_like(acc_ref)

        u = jnp.dot(h_ref[...], wup_ref[...], preferred_element_type=F32).astype(BF16)
        u_ref[...] = u
        a = jnp.square(jnp.maximum(u, 0))
        a_ref[...] = a
        acc_ref[...] += jnp.dot(a, wdown_ref[...], preferred_element_type=F32)

        @pl.when(j == g - 1)
        def _():
            y = acc_ref[...]
            y_ref[...] = y
            xo = x_ref[...] + _rms(y, gpost_ref[...])
            if with_loss:
                err = xo - t_ref[...]
                xo_ref[...] = err * (1.0 / d)
                part = 0.5 * jnp.sum(jnp.mean(err * err, axis=-1, keepdims=True), axis=0, keepdims=True)
                _acc_rows(loss_ref, i == 0, [jnp.broadcast_to(part, (1, 128))])
            else:
                xo_ref[...] = xo

    row = lambda i, j: (i, 0)
    const = lambda i, j: (0, 0)
    in_specs = [pl.BlockSpec((tm, d), row), pl.BlockSpec((1, d), const),
                pl.BlockSpec((None, d, wc), lambda i, j: (j, 0, 0)),
                pl.BlockSpec((wc, d), lambda i, j: (j, 0)), pl.BlockSpec((1, d), const)]
    out_specs = [pl.BlockSpec((tm, d), row), pl.BlockSpec((tm, d), row), pl.BlockSpec((tm, d), row),
                 pl.BlockSpec((tm, wc), lambda i, j: (i, j)), pl.BlockSpec((tm, wc), lambda i, j: (i, j))]
    out_shape = [jax.ShapeDtypeStruct((n, d), F32), jax.ShapeDtypeStruct((n, d), F32),
                 jax.ShapeDtypeStruct((n, d), BF16), jax.ShapeDtypeStruct((n, g * wc), BF16),
                 jax.ShapeDtypeStruct((n, g * wc), BF16)]
    args = [x_in, g_pre, w_up, w_down, g_post]
    if with_loss:
        in_specs.append(pl.BlockSpec((tm, d), row))
        out_specs.append(pl.BlockSpec((8, 128), const))
        out_shape.append(jax.ShapeDtypeStruct((8, 128), F32))
        args.append(target)
    return pl.pallas_call(
        body, name=name, grid=(n // tm, g), in_specs=in_specs, out_specs=out_specs, out_shape=out_shape,
        scratch_shapes=[pltpu.VMEM((tm, d), F32)],
        compiler_params=_params("arbitrary", "arbitrary"),
    )(*args)


def _mlp_bwd(dxo, y, g_post, w_down, u, w_up, x_in, g_pre, name, tm=1024):
    n, d = dxo.shape
    g, _, wc = w_up.shape

    def body(dxo_ref, y_ref, gpost_ref, wdown_ref, u_ref, wup_ref, x_ref, gpre_ref,
             dx_ref, dy_ref, du_ref, dgain_ref, acc_ref, dgpost_ref):
        i, j = pl.program_id(0), pl.program_id(1)

        @pl.when(j == 0)
        def _():
            dy, dgpost = _rms_bwd(y_ref[...], gpost_ref[...], dxo_ref[...])
            dy_ref[...] = dy.astype(BF16)
            dgpost_ref[...] = dgpost
            acc_ref[...] = jnp.zeros_like(acc_ref)

        da = _dot(dy_ref[...], wdown_ref[...], tb=True)
        du = (da * (2.0 * jnp.maximum(u_ref[...], 0).astype(F32))).astype(BF16)
        du_ref[...] = du
        acc_ref[...] += _dot(du, wup_ref[...], tb=True)

        @pl.when(j == g - 1)
        def _():
            dx, dgpre = _rms_bwd(x_ref[...], gpre_ref[...], acc_ref[...])
            dx_ref[...] = dxo_ref[...] + dx
            _acc_rows(dgain_ref, i == 0, [dgpre, dgpost_ref[...]])

    row = lambda i, j: (i, 0)
    const = lambda i, j: (0, 0)
    return pl.pallas_call(
        body, name=name, grid=(n // tm, g),
        in_specs=[pl.BlockSpec((tm, d), row), pl.BlockSpec((tm, d), row), pl.BlockSpec((1, d), const),
                  pl.BlockSpec((wc, d), lambda i, j: (j, 0)), pl.BlockSpec((tm, wc), lambda i, j: (i, j)),
                  pl.BlockSpec((None, d, wc), lambda i, j: (j, 0, 0)), pl.BlockSpec((tm, d), row),
                  pl.BlockSpec((1, d), const)],
        out_specs=[pl.BlockSpec((tm, d), row), pl.BlockSpec((tm, d), row),
                   pl.BlockSpec((tm, wc), lambda i, j: (i, j)), pl.BlockSpec((8, d), const)],
        out_shape=[jax.ShapeDtypeStruct((n, d), F32), jax.ShapeDtypeStruct((n, d), BF16),
                   jax.ShapeDtypeStruct((n, g * wc), BF16), jax.ShapeDtypeStruct((8, d), F32)],
        scratch_shapes=[pltpu.VMEM((tm, d), F32), pltpu.VMEM((1, d), F32)],
        compiler_params=_params("arbitrary", "arbitrary"),
    )(dxo, y, g_post, w_down, u, w_up, x_in, g_pre)


QKV = 3 * D_MODEL


def _shifted(x, prev8, s):
    if s == 0:
        return x
    tm = x.shape[0]
    rolled = pltpu.roll(x, s, 0)
    head = pltpu.roll(prev8, s, 0)
    head = jnp.concatenate([head, jnp.zeros((tm - 8, x.shape[1]), x.dtype)], axis=0)
    rows = lax.broadcasted_iota(jnp.int32, x.shape, 0)
    return jnp.where(rows < s, head, rolled)


def _conv(x, prev8, conv_w):
    out = x * conv_w[3:4, :]
    for s in range(1, CONV_WIDTH):
        out = out + _shifted(x, prev8, s) * conv_w[3 - s:4 - s, :]
    return out


def _l2norm(x):
    return x * lax.rsqrt(jnp.sum(x * x, axis=-1, keepdims=True) + EPS)


def _gdn_act_head(cq, ck, cv):
    return _l2norm(_silu(cq)) * (HEAD_DIM ** -0.5), _l2norm(_silu(ck)), _silu(cv)


def _gdn_gates(ba, alog, dtb):
    lane = lax.broadcasted_iota(jnp.int32, ba.shape, 1)
    beta = 1.0 / (1.0 + jnp.exp(-ba))
    t = ba + dtb
    softplus = jnp.maximum(t, 0.0) + jnp.log(1.0 + jnp.exp(-jnp.abs(t)))
    g = -jnp.exp(alog) * softplus
    return jnp.where(lane < N_HEADS, beta, jnp.where(lane < 2 * N_HEADS, g, 0.0))


def _qkv_cols(part, h):
    start = part * D_MODEL + h * HEAD_DIM
    return slice(start, start + HEAD_DIM)


def _prev_rows_spec(tm, cols, colblock, order):
    per = tm // 8
    return pl.BlockSpec((8, cols), lambda i: (jnp.maximum(order(i) * per - 1, 0), colblock))


def _gdn_pre(proj, conv_w, alog, dtb, name, tm=256):
    n = proj.shape[0]

    def body(x_ref, prev_ref, ba_ref, cw_ref, alog_ref, dtb_ref, q_ref, k_ref, v_ref, bg_ref):
        first = pl.program_id(0) == 0
        for h in range(N_HEADS):
            convs = []
            for part in range(3):
                sl = _qkv_cols(part, h)
                prev8 = jnp.where(first, 0.0, prev_ref[:, sl])
                convs.append(_conv(x_ref[:, sl], prev8, cw_ref[:, sl]))
            out = slice(h * HEAD_DIM, (h + 1) * HEAD_DIM)
            q_ref[:, out], k_ref[:, out], v_ref[:, out] = _gdn_act_head(*convs)
        bg_ref[...] = _gdn_gates(ba_ref[...], alog_ref[...], dtb_ref[...])

    row = lambda i: (i, 0)
    const = lambda i: (0, 0)
    ident = lambda i: i
    return pl.pallas_call(
        body, name=name, grid=(n // tm,),
        in_specs=[pl.BlockSpec((tm, QKV), row), _prev_rows_spec(tm, QKV, 0, ident),
                  pl.BlockSpec((tm, 128), lambda i: (i, 4 * D_MODEL // 128)),
                  pl.BlockSpec((CONV_WIDTH, QKV), const), pl.BlockSpec((1, 128), const), pl.BlockSpec((1, 128), const)],
        out_specs=[pl.BlockSpec((tm, D_MODEL), row)] * 3 + [pl.BlockSpec((tm, 128), row)],
        out_shape=[jax.ShapeDtypeStruct((n, D_MODEL), F32)] * 3 + [jax.ShapeDtypeStruct((n, 128), F32)],
        compiler_params=_params("parallel"),
    )(proj, proj, proj, conv_w, alog, dtb)


def _gdn_pre_bwd(proj, conv_w, alog, dtb, dq, dk, dv, dbg, dgate, name, tm=256):
    n = proj.shape[0]
    nt = n // tm

    def body(x_ref, prev_ref, ba_ref, cw_ref, alog_ref, dtb_ref, dq_ref, dk_ref, dv_ref, dbg_ref, dgate_ref,
             dproj_ref, dcw_ref, dgates_ref, carry_ref):
        step = pl.program_id(0)
        tile = nt - 1 - step
        @pl.when(step == 0)
        def _():
            carry_ref[...] = jnp.zeros_like(carry_ref)
            dcw_ref[...] = jnp.zeros_like(dcw_ref)

        rows = lax.broadcasted_iota(jnp.int32, (tm, HEAD_DIM), 0)
        for h in range(N_HEADS):
            cols = [_qkv_cols(part, h) for part in range(3)]
            prevs = [jnp.where(tile == 0, 0.0, prev_ref[:, sl]) for sl in cols]
            convs = [_conv(x_ref[:, sl], prev8, cw_ref[:, sl]) for sl, prev8 in zip(cols, prevs)]
            _, vjp = jax.vjp(_gdn_act_head, *convs)
            out = slice(h * HEAD_DIM, (h + 1) * HEAD_DIM)
            dcs = vjp((dq_ref[:, out], dk_ref[:, out], dv_ref[:, out]))
            for sl, prev8, dc in zip(cols, prevs, dcs):
                x = x_ref[:, sl]
                cw = cw_ref[:, sl]
                dx = dc * cw[3:4, :]
                wrapped = jnp.zeros((8, HEAD_DIM), F32)
                dcw_ref[3:4, sl] += jnp.sum(dc * x, axis=0, keepdims=True)
                for s in range(1, CONV_WIDTH):
                    r = pltpu.roll(dc, tm - s, 0) * cw[3 - s:4 - s, :]
                    dx = dx + jnp.where(rows < tm - s, r, 0.0)
                    wrapped = wrapped + jnp.where(rows[tm - 8:, :] >= tm - s, r[tm - 8:, :], 0.0)
                    dcw_ref[3 - s:4 - s, sl] += jnp.sum(dc * _shifted(x, prev8, s), axis=0, keepdims=True)
                dproj_ref[:, sl] = dx
                dproj_ref[tm - 8:, sl] += carry_ref[:, sl]
                carry_ref[:, sl] = wrapped

        _, vjp = jax.vjp(_gdn_gates, ba_ref[...], alog_ref[...], dtb_ref[...])
        dba, dalog, ddtb = vjp(dbg_ref[...])
        dproj_ref[:, QKV:4 * D_MODEL] = dgate_ref[...]
        dproj_ref[:, 4 * D_MODEL:] = dba
        _acc_rows(dgates_ref, step == 0, [dalog, ddtb])

    rev = lambda i: nt - 1 - i
    row = lambda i: (rev(i), 0)
    const = lambda i: (0, 0)
    return pl.pallas_call(
        body, name=name, grid=(nt,),
        in_specs=[pl.BlockSpec((tm, QKV), row), _prev_rows_spec(tm, QKV, 0, rev),
                  pl.BlockSpec((tm, 128), lambda i: (rev(i), 4 * D_MODEL // 128)),
                  pl.BlockSpec((CONV_WIDTH, QKV), const), pl.BlockSpec((1, 128), const), pl.BlockSpec((1, 128), const),
                  pl.BlockSpec((tm, D_MODEL), row), pl.BlockSpec((tm, D_MODEL), row), pl.BlockSpec((tm, D_MODEL), row),
                  pl.BlockSpec((tm, 128), row), pl.BlockSpec((tm, D_MODEL), row)],
        out_specs=[pl.BlockSpec((tm, GDN_IN_PAD), row), pl.BlockSpec((8, QKV), const), pl.BlockSpec((8, 128), const)],
        out_shape=[jax.ShapeDtypeStruct((n, GDN_IN_PAD), F32), jax.ShapeDtypeStruct((8, QKV), F32),
                   jax.ShapeDtypeStruct((8, 128), F32)],
        scratch_shapes=[pltpu.VMEM((8, QKV), F32)],
        compiler_params=_params("arbitrary"),
    )(proj, proj, proj, conv_w, alog, dtb, dq, dk, dv, dbg, dgate)


@jax.custom_vjp
def _unit_lower_inverses(lowers):
    n = lowers[0].shape[0]
    eye = (lax.broadcasted_iota(jnp.int32, (n, n), 0) == lax.broadcasted_iota(jnp.int32, (n, n), 1)).astype(F32)
    xs = [-low for low in lowers]
    ps = [eye + x for x in xs]
    ys = [_dot_3x(x, x) for x in xs]
    for _ in range(int(math.log2(CHUNK)) - 2):
        both = [_dot_3x(y, jnp.concatenate([y, p], axis=1)) for y, p in zip(ys, ps)]
        ys = [b[:, :n] for b in both]
        ps = [p + b[:, n:] for p, b in zip(ps, both)]
    return tuple(p + _dot_3x(y, p) for y, p in zip(ys, ps))


def _unit_lower_inverses_fwd(lowers):
    ts = _unit_lower_inverses(lowers)
    return ts, ts


def _unit_lower_inverses_bwd(ts, dts):
    left = [_dot_3x(t, dt, ta=True) for t, dt in zip(ts, dts)]
    return (tuple(-_dot_3x(l, t, tb=True) for l, t in zip(left, ts)),)


_unit_lower_inverses.defvjp(_unit_lower_inverses_fwd, _unit_lower_inverses_bwd)


@jax.custom_vjp
def _known_inverses(lowers, inverses):
    return inverses


def _known_inverses_bwd(ts, dts):
    (d_lowers,) = _unit_lower_inverses_bwd(ts, dts)
    return d_lowers, tuple(jnp.zeros_like(t) for t in ts)


_known_inverses.defvjp(lambda lowers, inverses: (inverses, inverses), _known_inverses_bwd)


@functools.partial(jax.custom_vjp, nondiff_argnums=(1,))
def _halves(x, axis):
    half = x.shape[axis] // 2
    return (x[:half], x[half:]) if axis == 0 else (x[:, :half], x[:, half:])


_halves.defvjp(lambda x, axis: (_halves(x, axis), None), lambda axis, _, g: (jnp.concatenate(g, axis=axis),))

GDN_STEP_CHUNKS = 2


def _gdn_chunks(qs, ks, vs, bgs, states, inverses=None, keep_inverses=False):
    c = CHUNK
    heads = range(N_HEADS)
    items = [(j, h) for j in range(len(bgs)) for h in heads]
    row = lax.broadcasted_iota(jnp.int32, (c, c), 0)
    col = lax.broadcasted_iota(jnp.int32, (c, c), 1)
    incl, strict, eye = row >= col, row > col, row == col
    lane = lax.broadcasted_iota(jnp.int32, (c, 128), 1)
    rowc = lax.broadcasted_iota(jnp.int32, (c, 1), 0)
    gc_all = [_mm_f32(incl.astype(F32), bg, False, False) for bg in bgs]
    q, k, v = ([xs[j][h] for j, h in items] for xs in (qs, ks, vs))
    n = range(len(items))
    beta = [jnp.sum(jnp.where(lane == h, bgs[j], 0.0), axis=1, keepdims=True) for j, h in items]
    gc = [jnp.sum(jnp.where(lane == N_HEADS + h, gc_all[j], 0.0), axis=1, keepdims=True) for j, h in items]
    gc_row = [jnp.sum(jnp.where(eye, gc[i], 0.0), axis=0, keepdims=True) for i in n]
    gc_last = [jnp.sum(jnp.where(rowc == c - 1, gc[i], 0.0), axis=0, keepdims=True) for i in n]
    decay = [jnp.where(incl, jnp.exp(jnp.where(incl, gc[i] - gc_row[i], 0.0)), 0.0) for i in n]
    kb = [k[i] * beta[i] for i in n]
    kb_q = [_halves(_mm(jnp.concatenate([kb[i], q[i]], axis=0), k[i], False, True), 0) for i in n]
    lower = tuple(jnp.where(strict, kb_q[i][0] * decay[i], 0.0) for i in n)
    attn = [kb_q[i][1] * decay[i] for i in n]
    t_mat = _unit_lower_inverses(lower) if inverses is None else _known_inverses(lower, inverses)
    egc = [jnp.exp(gc[i]) for i in n]
    w_u = [_halves(_mm(t_mat[i], jnp.concatenate([kb[i] * egc[i], v[i] * beta[i]], axis=1), False, False), 1)
           for i in n]
    w, u = [x[0] for x in w_u], [x[1] for x in w_u]
    qg = [q[i] * egc[i] for i in n]
    kg = [k[i] * jnp.exp(gc_last[i] - gc[i]) for i in n]
    outs, cur = [], list(states)
    for j in range(len(bgs)):
        at = lambda h: j * N_HEADS + h
        w_qg = [_halves(_mm(jnp.concatenate([w[at(h)], qg[at(h)]], axis=0), cur[h], False, False), 0)
                for h in heads]
        v_new = [u[at(h)] - w_qg[h][0] for h in heads]
        outs.append(tuple(w_qg[h][1] + _mm(attn[at(h)], v_new[h], False, False) for h in heads))
        cur = [cur[h] * jnp.exp(gc_last[at(h)]) + _mm(kg[at(h)], v_new[h], True, False) for h in heads]
    return (tuple(outs), tuple(cur), t_mat) if keep_inverses else (tuple(outs), tuple(cur))


def _chunk_head_slices(ref, nch):
    return tuple(tuple(ref[j * CHUNK:(j + 1) * CHUNK, h * HEAD_DIM:(h + 1) * HEAD_DIM] for h in range(N_HEADS))
                 for j in range(nch))


def _store_chunk_heads(ref, values):
    for j, chunk in enumerate(values):
        for h, val in enumerate(chunk):
            ref[j * CHUNK:(j + 1) * CHUNK, h * HEAD_DIM:(h + 1) * HEAD_DIM] = val


def _gdn_scan(qn, kn, v, bg, name, ride=None):
    n = qn.shape[0]
    nch = GDN_STEP_CHUNKS
    rows = nch * CHUNK
    nc = n // rows

    def body(q_ref, k_ref, v_ref, bg_ref, o_ref, saved_ref, inv_ref, state_ref):
        @pl.when(pl.program_id(0) == 0)
        def _():
            state_ref[...] = jnp.zeros_like(state_ref)

        states = tuple(state_ref[h] for h in range(N_HEADS))
        for h in range(N_HEADS):
            saved_ref[h] = states[h]
        bgs = tuple(bg_ref[j * CHUNK:(j + 1) * CHUNK, :] for j in range(nch))
        outs, new_states, inverses = _gdn_chunks(_chunk_head_slices(q_ref, nch), _chunk_head_slices(k_ref, nch),
                                                 _chunk_head_slices(v_ref, nch), bgs, states, keep_inverses=True)
        _store_chunk_heads(o_ref, outs)
        for h in range(N_HEADS):
            state_ref[h] = new_states[h]
        for item, inverse in enumerate(inverses):
            inv_ref[item] = inverse

    row = lambda i: (i, 0)
    return _call(
        body, [qn, kn, v, bg], name=name, grid=(nc,),
        in_specs=[pl.BlockSpec((rows, D_MODEL), row)] * 3 + [pl.BlockSpec((rows, 128), row)],
        out_specs=[pl.BlockSpec((rows, D_MODEL), row),
                   pl.BlockSpec((None, N_HEADS, HEAD_DIM, HEAD_DIM), lambda i: (i, 0, 0, 0)),
                   pl.BlockSpec((None, nch * N_HEADS, CHUNK, CHUNK), lambda i: (i, 0, 0, 0))],
        out_shape=[jax.ShapeDtypeStruct((n, D_MODEL), F32),
                   jax.ShapeDtypeStruct((nc, N_HEADS, HEAD_DIM, HEAD_DIM), F32),
                   jax.ShapeDtypeStruct((nc, nch * N_HEADS, CHUNK, CHUNK), F32)],
        scratch_shapes=[pltpu.VMEM((N_HEADS, HEAD_DIM, HEAD_DIM), F32)],
        compiler_params=_params("arbitrary"),
        ride=ride, first=lambda: pl.program_id(0) == 0, last=lambda: pl.program_id(0) == nc - 1)


def _gdn_scan_bwd(qn, kn, v, bg, saved, inverses, do, name, ride=None):
    n = qn.shape[0]
    nch = GDN_STEP_CHUNKS
    rows = nch * CHUNK
    nc = n // rows

    def body(q_ref, k_ref, v_ref, bg_ref, saved_ref, inv_ref, do_ref, dq_ref, dk_ref, dv_ref, dbg_ref, dstate_ref):
        @pl.when(pl.program_id(0) == 0)
        def _():
            dstate_ref[...] = jnp.zeros_like(dstate_ref)

        states = tuple(saved_ref[h] for h in range(N_HEADS))
        bgs = tuple(bg_ref[j * CHUNK:(j + 1) * CHUNK, :] for j in range(nch))
        known = tuple(inv_ref[item] for item in range(nch * N_HEADS))
        _, vjp = jax.vjp(functools.partial(_gdn_chunks, inverses=known), _chunk_head_slices(q_ref, nch),
                         _chunk_head_slices(k_ref, nch), _chunk_head_slices(v_ref, nch), bgs, states)
        dstates = tuple(dstate_ref[h] for h in range(N_HEADS))
        dqs, dks, dvs, dbgs, dprev = vjp((_chunk_head_slices(do_ref, nch), dstates))
        _store_chunk_heads(dq_ref, dqs)
        _store_chunk_heads(dk_ref, dks)
        _store_chunk_heads(dv_ref, dvs)
        for h in range(N_HEADS):
            dstate_ref[h] = dprev[h]
        for j in range(nch):
            dbg_ref[j * CHUNK:(j + 1) * CHUNK, :] = dbgs[j]

    row = lambda i: (nc - 1 - i, 0)
    return _call(
        body, [qn, kn, v, bg, saved, inverses, do], name=name, grid=(nc,),
        in_specs=[pl.BlockSpec((rows, D_MODEL), row)] * 3 + [pl.BlockSpec((rows, 128), row),
                  pl.BlockSpec((None, N_HEADS, HEAD_DIM, HEAD_DIM), lambda i: (nc - 1 - i, 0, 0, 0)),
                  pl.BlockSpec((None, nch * N_HEADS, CHUNK, CHUNK), lambda i: (nc - 1 - i, 0, 0, 0)),
                  pl.BlockSpec((rows, D_MODEL), row)],
        out_specs=[pl.BlockSpec((rows, D_MODEL), row)] * 3 + [pl.BlockSpec((rows, 128), row)],
        out_shape=[jax.ShapeDtypeStruct((n, D_MODEL), F32)] * 3 + [jax.ShapeDtypeStruct((n, 128), F32)],
        scratch_shapes=[pltpu.VMEM((N_HEADS, HEAD_DIM, HEAD_DIM), F32)],
        compiler_params=_params("arbitrary"),
        ride=ride, first=lambda: pl.program_id(0) == 0, last=lambda: pl.program_id(0) == nc - 1)


SB_BQ = 512
SB_BK = 256
SB_SUB = 128
SB_ROWS = 128
SB_SCALE = HEAD_DIM ** -0.5
SB_DEAD = -105.0


def _sb_terms(z, before):
    e = jnp.exp(-jnp.abs(z))
    log_beta = jnp.minimum(z, 0.0) - jnp.log(1.0 + e)
    log_1m = log_beta - z
    if before is not None:
        log_1m = jnp.where(before, log_1m, 0.0)
    return e, log_beta, log_1m


def _tri_ones(n, cmp):
    r = lax.broadcasted_iota(jnp.int32, (n, 2 * n), 0)
    c = lax.broadcasted_iota(jnp.int32, (n, 2 * n), 1)
    return jnp.where((c >= n) | cmp(r, c), 1.0, 0.0).astype(BF16)


def _sums(x, tri_ones):
    both = jnp.dot(x.astype(BF16), tri_ones, preferred_element_type=F32)
    n = x.shape[1]
    return both[:, :n], both[:, n:]


def _sb_chunk_mask(diagonal, r, s):
    if not diagonal or s * SB_SUB + SB_SUB - 1 < r * SB_ROWS:
        return None
    if s * SB_SUB >= r * SB_ROWS + SB_ROWS - 1:
        return "empty"
    rows = r * SB_ROWS + lax.broadcasted_iota(jnp.int32, (SB_ROWS, SB_SUB), 0)
    cols = s * SB_SUB + lax.broadcasted_iota(jnp.int32, (SB_ROWS, SB_SUB), 1)
    return cols < rows


def _sb_attention(q, kv, name, bq=SB_BQ, ride=None):
    n = q.shape[0]
    bq = min(bq, n)
    bk = min(SB_BK, bq)
    nrc = bq // SB_ROWS

    def body(q_ref, k_ref, v_ref, o_ref, l_ref, n_ref, z_scr, a_scr):
        i = pl.program_id(1)
        qb = q_ref[...]
        after = _tri_ones(SB_SUB, lambda r, c: r > c)

        def span(start, kw, c_sum, acc, diagonal):
            start = pl.multiple_of(start, kw)
            k_w = k_ref[pl.ds(start, kw), :]
            v_w = v_ref[pl.ds(start, kw), :]
            z_scr[:, :kw] = lax.dot_general(qb, k_w, _dims(False, True), preferred_element_type=F32)
            c_rows = [c_sum[r * SB_ROWS:(r + 1) * SB_ROWS] for r in range(nrc)]
            for s in reversed(range(kw // SB_SUB)):
                cols = slice(s * SB_SUB, (s + 1) * SB_SUB)
                for r in range(nrc):
                    rows = slice(r * SB_ROWS, (r + 1) * SB_ROWS)
                    before = _sb_chunk_mask(diagonal, r, s)
                    if isinstance(before, str):
                        a_scr[rows, cols] = jnp.zeros((SB_ROWS, SB_SUB), BF16)
                        continue
                    _, log_beta, log_1m = _sb_terms(z_scr[rows, cols] * SB_SCALE, before)
                    tail, total = _sums(log_1m, after)
                    a = jnp.exp(log_beta + c_rows[r] + tail)
                    if before is not None:
                        a = jnp.where(before, a, 0.0)
                    a_scr[rows, cols] = a.astype(BF16)
                    c_rows[r] = c_rows[r] + total
            acc = acc + jnp.dot(a_scr[:, :kw], v_w, preferred_element_type=F32)
            return jnp.concatenate(c_rows, axis=0), acc

        c_sum, acc = span(i * bq, bq, jnp.zeros((bq, SB_SUB), F32), jnp.zeros((bq, HEAD_DIM), F32), True)

        def more(state):
            done, c, _ = state
            return (done < i * (bq // bk)) & (jnp.max(c) > SB_DEAD)

        def step(state):
            done, c, a = state
            c, a = span(i * bq - (done + 1) * bk, bk, c, a, False)
            return done + 1, c, a

        done, c_sum, acc = lax.while_loop(more, step, (jnp.int32(0), c_sum, acc))
        o_ref[...] = acc.astype(BF16)
        l_ref[...] = c_sum
        n_ref[...] = jnp.full((8, 128), done.astype(F32), F32)

    nq = n // bq
    return _call(
        body, [q, kv, kv], name=name, grid=(N_HEADS, nq),
        in_specs=[pl.BlockSpec((bq, HEAD_DIM), lambda h, i: (i, h)),
                  pl.BlockSpec((n, HEAD_DIM), lambda h, i: (0, h)),
                  pl.BlockSpec((n, HEAD_DIM), lambda h, i: (0, N_HEADS + h))],
        out_specs=[pl.BlockSpec((bq, HEAD_DIM), lambda h, i: (i, h)),
                   pl.BlockSpec((None, bq, 128), lambda h, i: (h, i, 0)),
                   pl.BlockSpec((None, None, 8, 128), lambda h, i: (h, i, 0, 0))],
        out_shape=[jax.ShapeDtypeStruct((n, D_MODEL), BF16), jax.ShapeDtypeStruct((N_HEADS, n, 128), F32),
                   jax.ShapeDtypeStruct((N_HEADS, nq, 8, 128), F32)],
        scratch_shapes=[pltpu.VMEM((bq, bq), F32), pltpu.VMEM((bq, bq), BF16)],
        compiler_params=_params("arbitrary", "arbitrary"),
        ride=ride, first=lambda: (pl.program_id(0) == 0) & (pl.program_id(1) == 0),
        last=lambda: (pl.program_id(0) == N_HEADS - 1) & (pl.program_id(1) == nq - 1))


def _sb_attention_bwd(q, kv, do, lsum, spans, name, bq=SB_BQ, ride=None):
    n = q.shape[0]
    bq = min(bq, n)
    bk = min(SB_BK, bq)
    nrc = bq // SB_ROWS

    def body(q_ref, k_ref, v_ref, do_ref, l_ref, n_ref, dq_ref, dk_ref, dv_ref, z_scr, da_scr, a_scr, dz_scr):
        i = pl.program_id(1)

        @pl.when(i == 0)
        def _():
            dk_ref[...] = jnp.zeros_like(dk_ref)
            dv_ref[...] = jnp.zeros_like(dv_ref)

        qb = q_ref[...]
        dob = do_ref[...]
        lt_rows = [l_ref[r * SB_ROWS:(r + 1) * SB_ROWS, :] for r in range(nrc)]
        upto = _tri_ones(SB_SUB, lambda r, c: r <= c)
        below = _tri_ones(SB_SUB, lambda r, c: r < c)

        def span(start, kw, l_sum, g_sum, dq, diagonal):
            start = pl.multiple_of(start, kw)
            k_w = k_ref[pl.ds(start, kw), :]
            v_w = v_ref[pl.ds(start, kw), :]
            z_scr[:, :kw] = lax.dot_general(qb, k_w, _dims(False, True), preferred_element_type=F32)
            da_scr[:, :kw] = lax.dot_general(dob, v_w, _dims(False, True), preferred_element_type=F32)
            l_rows = [l_sum[r * SB_ROWS:(r + 1) * SB_ROWS] for r in range(nrc)]
            g_rows = [g_sum[r * SB_ROWS:(r + 1) * SB_ROWS] for r in range(nrc)]
            for s in range(kw // SB_SUB):
                cols = slice(s * SB_SUB, (s + 1) * SB_SUB)
                for r in range(nrc):
                    rows = slice(r * SB_ROWS, (r + 1) * SB_ROWS)
                    before = _sb_chunk_mask(diagonal, r, s)
                    if isinstance(before, str):
                        a_scr[rows, cols] = jnp.zeros((SB_ROWS, SB_SUB), BF16)
                        dz_scr[rows, cols] = jnp.zeros((SB_ROWS, SB_SUB), BF16)
                        continue
                    zs = z_scr[rows, cols] * SB_SCALE
                    e, log_beta, log_1m = _sb_terms(zs, before)
                    l_prefix, l_total = _sums(log_1m, upto)
                    a = jnp.exp(log_beta + (lt_rows[r] - (l_rows[r] + l_prefix)))
                    if before is not None:
                        a = jnp.where(before, a, 0.0)
                    g = a * da_scr[rows, cols]
                    g_prefix, g_total = _sums(g, below)
                    inv = 1.0 / (1.0 + e)
                    beta = jnp.where(zs >= 0, inv, e * inv)
                    dz = (g - (g + g_rows[r] + g_prefix) * beta) * SB_SCALE
                    if before is not None:
                        dz = jnp.where(before, dz, 0.0)
                    a_scr[rows, cols] = a.astype(BF16)
                    dz_scr[rows, cols] = dz.astype(BF16)
                    l_rows[r] = l_rows[r] + l_total
                    g_rows[r] = g_rows[r] + g_total
            dz_w = dz_scr[:, :kw]
            dq = dq + jnp.dot(dz_w, k_w, preferred_element_type=F32)
            dk_ref[pl.ds(start, kw), :] += lax.dot_general(dz_w, qb, _dims(True, False), preferred_element_type=F32)
            dv_ref[pl.ds(start, kw), :] += lax.dot_general(a_scr[:, :kw], dob, _dims(True, False),
                                                           preferred_element_type=F32)
            return jnp.concatenate(l_rows, axis=0), jnp.concatenate(g_rows, axis=0), dq

        taken = jnp.clip(jnp.max(n_ref[...]).astype(jnp.int32), 0, i * (bq // bk))
        zero = jnp.zeros((bq, SB_SUB), F32)
        carry = lax.fori_loop(0, taken, lambda j, c: span(i * bq - (taken - j) * bk, bk, c[0], c[1], c[2], False),
                              (zero, zero, jnp.zeros((bq, HEAD_DIM), F32)))
        _, _, dq = span(i * bq, bq, carry[0], carry[1], carry[2], True)
        dq_ref[...] = dq.astype(BF16)

    nq = n // bq
    return _call(
        body, [q, kv, kv, do, lsum, spans], name=name, grid=(N_HEADS, nq),
        in_specs=[pl.BlockSpec((bq, HEAD_DIM), lambda h, i: (i, h)),
                  pl.BlockSpec((n, HEAD_DIM), lambda h, i: (0, h)),
                  pl.BlockSpec((n, HEAD_DIM), lambda h, i: (0, N_HEADS + h)),
                  pl.BlockSpec((bq, HEAD_DIM), lambda h, i: (i, h)),
                  pl.BlockSpec((None, bq, 128), lambda h, i: (h, i, 0)),
                  pl.BlockSpec((None, None, 8, 128), lambda h, i: (h, i, 0, 0))],
        out_specs=[pl.BlockSpec((bq, HEAD_DIM), lambda h, i: (i, h)),
                   pl.BlockSpec((n, HEAD_DIM), lambda h, i: (0, h)),
                   pl.BlockSpec((n, HEAD_DIM), lambda h, i: (0, h))],
        out_shape=[jax.ShapeDtypeStruct((n, D_MODEL), BF16), jax.ShapeDtypeStruct((n, D_MODEL), F32),
                   jax.ShapeDtypeStruct((n, D_MODEL), F32)],
        scratch_shapes=[pltpu.VMEM((bq, bq), F32), pltpu.VMEM((bq, bq), F32), pltpu.VMEM((bq, bq), BF16),
                        pltpu.VMEM((bq, bq), BF16)],
        compiler_params=_params("arbitrary", "arbitrary"),
        ride=ride, first=lambda: (pl.program_id(0) == 0) & (pl.program_id(1) == 0),
        last=lambda: (pl.program_id(0) == N_HEADS - 1) & (pl.program_id(1) == nq - 1))


def _local_step(x, target, gains, comm):
    g = gains
    w = comm.w
    big = {}
    row = lambda a, i: a[i:i + 1, :]

    proj, h0 = _norm_matmul(x, row(g["mix_pre"], 0), w["gdn_in"], F32, "gdn_in_proj", tn=1408)
    qn, kn, v, bg = _gdn_pre(proj, w["conv"], g["alog"], g["dtb"], "gdn_pre")
    (o_gdn, saved, inverses), got = _gdn_scan(qn, kn, v, bg, "gdn_scan", ride=comm.ride("scan"))
    comm.done("scan", got)
    x1, y_mix0, a_gdn = _out_proj((o_gdn, proj), True, x, w["gdn_out"], row(g["mix_post"], 0), g["out_gain"],
                                  "gdn_out_proj", tm=512)
    x2, y_mlp0, h_mlp0, u0, a0 = _mlp_fwd(x1, row(g["mlp_pre"], 0), w["up0"], w["down0"], row(g["mlp_post"], 0), "mlp0")
    kv, h_kv = _norm_matmul(x2, g["kv"], w["kv"], BF16, "kv_proj", tm=2048)
    q, h_q = _norm_matmul(x2, row(g["mix_pre"], 1), w["sb_q"], BF16, "sb_q_proj")
    (o_sb, lsum, spans), got = _sb_attention(q, kv, "sb_attention", ride=comm.ride("sb"))
    comm.done("sb", got)
    x3, y_mix1 = _out_proj((o_sb,), False, x2, w["sb_o"], row(g["mix_post"], 1), None, "sb_out_proj")
    dx4, y_mlp1, h_mlp1, u1, a1, loss = _mlp_fwd(x3, row(g["mlp_pre"], 1), w["up1"], w["down1"], row(g["mlp_post"], 1),
                                                 "mlp1_loss", target=target)

    dx3, dy_mlp1, du1, dg_mlp1 = _mlp_bwd(dx4, y_mlp1, row(g["mlp_post"], 1), w["down1"], u1, w["up1"], x3,
                                          row(g["mlp_pre"], 1), "mlp1_bwd")
    big["down1"] = _matmul_tn(a1, dy_mlp1, 1, "d_down1")[0]
    big["up1"] = _matmul_tn(h_mlp1, du1, N_DEV, "d_up1", tk=2048)
    dy_mix1, dg_post1, do_sb = _out_proj_bwd(dx3, y_mix1, w["sb_o"], row(g["mix_post"], 1), None, "sb_out_proj_bwd")
    big["sb_o"] = _matmul_tn(o_sb, dy_mix1, 1, "d_sb_o")[0]
    (dq, dk, dv), got = _sb_attention_bwd(q, kv, do_sb, lsum, spans, "sb_attention_bwd", ride=comm.ride("sb_bwd", big))
    comm.done("sb_bwd", got)
    big["sb_q"] = _matmul_tn(h_q, dq, 1, "d_sb_q")[0]
    big["kv"] = jnp.concatenate([_matmul_tn(h_kv, dk, N_DEV // 2, "d_w_k", tk=2048),
                                 _matmul_tn(h_kv, dv, N_DEV // 2, "d_w_v", tk=2048)],
                                axis=0)
    dx2, dg_x2 = _norm_matmul_bwd([(dq, w["sb_q"], row(g["mix_pre"], 1), 1024), (dk, w["kv"][0:1], g["kv"], 1024),
                                   (dv, w["kv"][1:2], g["kv"], 1024)], x2, dx3, "qkv_proj_bwd", tm=512)

    dx1, dy_mlp0, du0, dg_mlp0 = _mlp_bwd(dx2, y_mlp0, row(g["mlp_post"], 0), w["down0"], u0, w["up0"], x1,
                                          row(g["mlp_pre"], 0), "mlp0_bwd")
    big["down0"] = _matmul_tn(a0, dy_mlp0, 1, "d_down0")[0]
    big["up0"] = _matmul_tn(h_mlp0, du0, N_DEV, "d_up0", tk=2048)
    dy_mix0, dg_post0, do_gdn, dgate, d_out_gain = _out_proj_bwd(
        dx1, y_mix0, w["gdn_out"], row(g["mix_post"], 0), (o_gdn, proj, g["out_gain"]), "gdn_out_proj_bwd", tm=512)
    big["gdn_out"] = _matmul_tn(a_gdn, dy_mix0, 1, "d_gdn_out")[0]
    (dqn, dkn, dvv, dbg), got = _gdn_scan_bwd(qn, kn, v, bg, saved, inverses, do_gdn, "gdn_scan_bwd",
                                              ride=comm.ride("scan_bwd", big))
    comm.done("scan_bwd", got)
    dproj, d_conv, d_gates = _gdn_pre_bwd(proj, w["conv"], g["alog"], g["dtb"], dqn, dkn, dvv, dbg, dgate, "gdn_pre_bwd")
    big["gdn_in"] = _matmul_tn(h0, dproj, 1, "d_gdn_in", tb=1408, tk=512)[0]
    ride = comm.ride("in_bwd", big)
    grad_x, dg_pre0, *got = _norm_matmul_bwd([(dproj, w["gdn_in"], row(g["mix_pre"], 0), 1408)], x, dx1, "gdn_in_proj_bwd",
                                             ride=ride)
    comm.done("in_bwd", got[0] if got else [])

    small = {"pre0": dg_pre0, "x2": dg_x2, "post0": dg_post0, "post1": dg_post1, "mlp0": dg_mlp0, "mlp1": dg_mlp1,
             "gates": d_gates, "out_gain": d_out_gain, "conv": d_conv, "loss": loss}
    return grad_x, big, small


class _LocalOnly:
    def __init__(self, weights):
        self.w = weights

    def ride(self, stage, grads=None):
        return None

    def done(self, stage, outs):
        pass


GATHER_STAGES = {"scan": ("gdn_out", "up0", "down0", "kv", "sb_q"), "sb": ("sb_o", "up1", "down1")}
SCATTER_STAGES = {"sb_bwd": ("up1", "down1", "sb_o"), "scan_bwd": ("sb_q", "kv", "up0", "down0", "gdn_out"),
                  "in_bwd": ("gdn_in",)}


def _whole_weight(name, gathered):
    d = D_MODEL
    if name in ("up0", "up1"):
        return gathered
    if name == "kv":
        return gathered.reshape(2, N_DEV // 2, d, gathered.shape[2]).transpose(0, 2, 1, 3).reshape(2, d, d)
    if name == "gdn_in":
        whole = gathered.transpose(1, 0, 2).reshape(d, GDN_IN_COLS)
        return jnp.pad(whole, ((0, 0), (0, GDN_IN_PAD - GDN_IN_COLS)))[None]
    if name == "conv":
        return gathered.transpose(1, 0, 2).reshape(CONV_WIDTH, QKV)
    whole = gathered.reshape(gathered.shape[0] * gathered.shape[1], d)
    return whole[None] if name == "sb_q" else whole


def _owner_blocks(name, grad):
    if name in ("up0", "up1", "kv"):
        blocks = grad
    elif name == "gdn_in":
        blocks = grad[:, :GDN_IN_COLS].reshape(D_MODEL, N_DEV, GDN_IN_COLS // N_DEV).transpose(1, 0, 2)
    else:
        blocks = grad.reshape(N_DEV, grad.shape[0] // N_DEV, grad.shape[1])
    return blocks.astype(BF16)


class _Fsdp:
    def __init__(self, shards, weights):
        self.shards, self.w, self.landed = shards, weights, {}

    def ride(self, stage, grads=None):
        if stage in GATHER_STAGES:
            names = GATHER_STAGES[stage]
            return _Exchange([self.shards[nm] for nm in names], [True] * len(names))
        names = SCATTER_STAGES[stage]
        return _Exchange([_owner_blocks(nm, grads[nm]) for nm in names], [False] * len(names))

    def done(self, stage, outs):
        if stage in GATHER_STAGES:
            for nm, out in zip(GATHER_STAGES[stage], outs):
                self.w[nm] = _whole_weight(nm, out)
        else:
            self.landed.update(zip(SCATTER_STAGES[stage], outs))


def _my_place():
    return lax.axis_index("x"), lax.axis_index("y"), lax.axis_index("c")


class _Exchange:
    def __init__(self, arrays, gather):
        self.arrays, self.gather, self.n = list(arrays), list(gather), len(arrays)
        any_spec = pl.BlockSpec(memory_space=pl.ANY)
        self.in_specs = [any_spec] * self.n
        self.out_specs = [any_spec] * self.n
        self.out_shape = [jax.ShapeDtypeStruct(((N_DEV,) + a.shape) if g else a.shape, a.dtype)
                          for a, g in zip(self.arrays, self.gather)]
        self.scratch = [pltpu.SemaphoreType.DMA((self.n, N_DEV - 1)), pltpu.SemaphoreType.DMA((self.n, N_DEV - 1)),
                        pltpu.SemaphoreType.DMA((self.n,))]

    def _copies(self, ins, outs, sems):
        send_sems, recv_sems, local_sems = sems
        x, y, c = _my_place()
        me = 4 * x + 2 * y + c
        copies = []
        for a in range(self.n):
            src = ins[a] if self.gather[a] else ins[a].at[me]
            copies.append(pltpu.make_async_copy(src, outs[a].at[me], local_sems.at[a]))
        for k in range(1, N_DEV):
            px = 1 - x if k & 4 else x
            py = 1 - y if k & 2 else y
            pc = 1 - c if k & 1 else c
            peer = 4 * px + 2 * py + pc
            for a in range(self.n):
                src = ins[a] if self.gather[a] else ins[a].at[peer]
                copies.append(pltpu.make_async_remote_copy(
                    src_ref=src, dst_ref=outs[a].at[me], send_sem=send_sems.at[a, k - 1], recv_sem=recv_sems.at[a, k - 1],
                    device_id=(px, py, pc), device_id_type=MESH))
        return copies

    def start(self, ins, outs, sems):
        for cp in self._copies(ins, outs, sems):
            cp.start()

    def wait(self, ins, outs, sems):
        for cp in self._copies(ins, outs, sems):
            cp.wait()


def _call(body, operands, *, name, out_shape, in_specs, out_specs, grid=(), scratch_shapes=(), compiler_params=None,
          ride=None, first=None, last=None):
    n_in, n_out, n_scr = len(operands), len(out_shape), len(scratch_shapes)
    if ride is None:
        outs = pl.pallas_call(body, name=name, grid=grid, in_specs=in_specs, out_specs=out_specs, out_shape=out_shape,
                              scratch_shapes=scratch_shapes, compiler_params=compiler_params)(*operands)
        return list(outs), []
    r = ride.n

    def riding(*refs):
        ins, r_ins = refs[:n_in], refs[n_in:n_in + r]
        outs, r_outs = refs[n_in + r:n_in + r + n_out], refs[n_in + r + n_out:n_in + 2 * r + n_out]
        scr, sems = refs[n_in + 2 * r + n_out:n_in + 2 * r + n_out + n_scr], refs[n_in + 2 * r + n_out + n_scr:]

        @pl.when(first())
        def _():
            ride.start(r_ins, r_outs, sems)

        body(*ins, *outs, *scr)

        @pl.when(last())
        def _():
            ride.wait(r_ins, r_outs, sems)

    outs = pl.pallas_call(
        riding, name=name, grid=grid, in_specs=list(in_specs) + ride.in_specs, out_specs=list(out_specs) + ride.out_specs,
        out_shape=list(out_shape) + ride.out_shape, scratch_shapes=list(scratch_shapes) + ride.scratch,
        compiler_params=compiler_params)(*operands, *ride.arrays)
    return list(outs[:n_out]), list(outs[n_out:])


def _exchange(arrays, gather, name):
    ride = _Exchange(arrays, gather)

    def body(*refs):
        ins, outs, sems = refs[:ride.n], refs[ride.n:2 * ride.n], refs[2 * ride.n:]
        ride.start(ins, outs, sems)
        ride.wait(ins, outs, sems)

    return pl.pallas_call(body, name=name, in_specs=ride.in_specs, out_specs=ride.out_specs, out_shape=ride.out_shape,
                          scratch_shapes=ride.scratch)(*arrays)


def _gather_two_level(arrays, name):
    n_arr = len(arrays)

    def body(*refs):
        ins, outs = refs[:n_arr], refs[n_arr:2 * n_arr]
        send_sems, recv_sems, local_sems = refs[2 * n_arr:]
        x, y, c = _my_place()
        sibling = (x, y, 1 - c)
        chips = [(1 - x, y), (x, 1 - y), (1 - x, 1 - y)]
        index = lambda px, py, pc: 4 * px + 2 * py + pc

        def copy(a, k, block, to, src=None):
            dst = outs[a].at[index(*block)]
            return pltpu.make_async_remote_copy(src_ref=dst if src is None else src, dst_ref=dst,
                                                send_sem=send_sems.at[a, k], recv_sem=recv_sems.at[a, k],
                                                device_id=to, device_id_type=MESH)

        mine = [pltpu.make_async_copy(ins[a], outs[a].at[index(x, y, c)], local_sems.at[a]) for a in range(n_arr)]
        first = [copy(a, 0, (x, y, c), sibling, src=ins[a]) for a in range(n_arr)]
        first += [copy(a, 1 + j, (x, y, c), (*chip, c), src=ins[a]) for j, chip in enumerate(chips) for a in range(n_arr)]
        for cp in mine + first:
            cp.start()
        passed = []
        for j, chip in enumerate(chips):
            for a in range(n_arr):
                copy(a, 1 + j, (*chip, c), (x, y, c)).wait_recv()
                passed.append(copy(a, 4 + j, (*chip, c), sibling))
                passed[-1].start()
        for a in range(n_arr):
            copy(a, 0, sibling, (x, y, c)).wait_recv()
            for j, chip in enumerate(chips):
                copy(a, 4 + j, (*chip, 1 - c), (x, y, c)).wait_recv()
        for cp in first + passed:
            cp.wait_send()
        for cp in mine:
            cp.wait()

    any_spec = pl.BlockSpec(memory_space=pl.ANY)
    return pl.pallas_call(
        body, name=name, in_specs=[any_spec] * n_arr, out_specs=[any_spec] * n_arr,
        out_shape=[jax.ShapeDtypeStruct((N_DEV,) + a.shape, a.dtype) for a in arrays],
        scratch_shapes=[pltpu.SemaphoreType.DMA((n_arr, N_DEV - 1)), pltpu.SemaphoreType.DMA((n_arr, N_DEV - 1)),
                        pltpu.SemaphoreType.DMA((n_arr,))])(*arrays)


def _adamw_math(g, w, m, v):
    m = ADAM_B1 * m + (1.0 - ADAM_B1) * g
    v = ADAM_B2 * v + (1.0 - ADAM_B2) * jnp.square(g)
    m_hat = m / (1.0 - ADAM_B1 ** ADAM_STEP)
    v_hat = v / (1.0 - ADAM_B2 ** ADAM_STEP)
    delta = -ADAM_LR * (m_hat / (jnp.sqrt(v_hat) + ADAM_EPS) + ADAM_WD * w)
    return delta, m, v


def _sum_devices(ref):
    total = ref[0].astype(F32)
    for d in range(1, N_DEV):
        total = total + ref[d].astype(F32)
    return total


def _reduce_adamw(landed, w, m, v, name, tr=256):
    r, c = w.shape
    tr = min(tr, r)
    assert r % tr == 0

    def body(l_ref, w_ref, m_ref, v_ref, g_ref, d_ref, nm_ref, nv_ref):
        g = _sum_devices(l_ref)
        g_ref[...] = g
        d_ref[...], nm_ref[...], nv_ref[...] = _adamw_math(g, w_ref[...], m_ref[...], v_ref[...])

    blk = pl.BlockSpec((tr, c), lambda i: (i, 0))
    return pl.pallas_call(
        body, name=name, grid=(r // tr,),
        in_specs=[pl.BlockSpec((N_DEV, tr, c), lambda i: (0, i, 0)), blk, blk, blk],
        out_specs=[blk] * 4, out_shape=[jax.ShapeDtypeStruct((r, c), F32)] * 4,
        compiler_params=_params("parallel"),
    )(landed, w, m, v)


def _reduce_adamw_layers(landed, w, m, v, name, tr=256):
    _, r, c = w.shape
    tr = min(tr, r)
    nr = r // tr
    assert r % tr == 0

    def body(l0_ref, l1_ref, w_ref, m_ref, v_ref, g_ref, d_ref, nm_ref, nv_ref):
        g = jnp.where(pl.program_id(0) == 0, _sum_devices(l0_ref), _sum_devices(l1_ref))
        g_ref[...] = g
        d_ref[...], nm_ref[...], nv_ref[...] = _adamw_math(g, w_ref[...], m_ref[...], v_ref[...])

    blk = pl.BlockSpec((None, tr, c), lambda l, i: (l, i, 0))
    return pl.pallas_call(
        body, name=name, grid=(2, nr),
        in_specs=[pl.BlockSpec((N_DEV, tr, c), lambda l, i: (0, jnp.where(l == 0, i, nr - 1), 0)),
                  pl.BlockSpec((N_DEV, tr, c), lambda l, i: (0, jnp.where(l == 1, i, 0), 0)), blk, blk, blk],
        out_specs=[blk] * 4, out_shape=[jax.ShapeDtypeStruct(w.shape, F32)] * 4,
        compiler_params=_params("arbitrary", "arbitrary"),
    )(landed[0], landed[1], w, m, v)


def _adamw(g, w, m, v, name):
    def body(g_ref, w_ref, m_ref, v_ref, d_ref, nm_ref, nv_ref):
        d_ref[...], nm_ref[...], nv_ref[...] = _adamw_math(g_ref[...], w_ref[...], m_ref[...], v_ref[...])

    return pl.pallas_call(body, name=name, out_shape=[jax.ShapeDtypeStruct(w.shape, F32)] * 3)(g, w, m, v)


def _small_update(landed, params, name):
    layout = {
        "mix_pre": [("pre0", 0), ("x2", 0)], "mix_post": [("post0", 0), ("post1", 0)],
        "mlp_pre": [("mlp0", 0), ("mlp1", 0)], "mlp_post": [("mlp0", 1), ("mlp1", 1)],
        "kv": [("kv", 0)], "alog": [("gates", 0)], "dtb": [("gates", 1)], "out_gain": [("out_gain", 0)],
    }
    landed_names = sorted(landed)
    param_names = sorted(params)
    n_l, n_p = len(landed_names), len(param_names)

    def body(*refs):
        l_refs = dict(zip(landed_names, refs[:n_l]))
        p_refs = {p: refs[n_l + 3 * i:n_l + 3 * i + 3] for i, p in enumerate(param_names)}
        outs = refs[n_l + 3 * n_p:]
        o_refs = {p: outs[4 * i:4 * i + 4] for i, p in enumerate(param_names)}
        conv_ref, loss_ref = outs[4 * n_p:]
        sums = {nm: _sum_devices(l_refs[nm]) for nm in landed_names}
        sums["kv"] = sums["x2"][1:2, :] + sums["x2"][2:3, :]
        for p in param_names:
            w_ref, m_ref, v_ref = p_refs[p]
            g_ref, d_ref, nm_ref, nv_ref = o_refs[p]
            for r, (src, src_row) in enumerate(layout[p]):
                g = sums[src][src_row:src_row + 1, :]
                g_ref[r:r + 1, :] = g
                d, nm, nv = _adamw_math(g, w_ref[r:r + 1, :], m_ref[r:r + 1, :], v_ref[r:r + 1, :])
                d_ref[r:r + 1, :] = d
                nm_ref[r:r + 1, :] = nm
                nv_ref[r:r + 1, :] = nv
        conv_ref[...] = sums["conv"]
        loss_ref[...] = sums["loss"]

    args = [landed[nm] for nm in landed_names]
    out_shape = []
    for p in param_names:
        args += list(params[p])
        out_shape += [jax.ShapeDtypeStruct(params[p][0].shape, F32)] * 4
    out_shape += [jax.ShapeDtypeStruct(landed["conv"].shape[1:], F32), jax.ShapeDtypeStruct(landed["loss"].shape[1:], F32)]
    outs = pl.pallas_call(body, name=name, out_shape=out_shape)(*args)
    result = {p: tuple(outs[4 * i:4 * i + 4]) for i, p in enumerate(param_names)}
    result["conv"], result["loss"] = outs[4 * n_p], outs[4 * n_p + 1]
    return result


def _lanes(vec, offset):
    return jnp.pad(vec[None, :], ((0, 0), (offset, 128 - offset - vec.shape[0])))


def kernel(x, mix_pre_gain, mix_post_gain, mlp_pre_gain, mlp_post_gain, mlp_w_up, mlp_w_down, gdn_w_in, gdn_conv_w, gdn_a_log, gdn_dt_bias, gdn_out_gain, gdn_w_out, kv_gain, w_kv, sb_w_q, sb_w_o, loss_target, m_mix_pre_gain, m_mix_post_gain, m_mlp_pre_gain, m_mlp_post_gain, m_mlp_w_up, m_mlp_w_down, m_gdn_w_in, m_gdn_conv_w, m_gdn_a_log, m_gdn_dt_bias, m_gdn_out_gain, m_gdn_w_out, m_kv_gain, m_w_kv, m_sb_w_q, m_sb_w_o, v_mix_pre_gain, v_mix_post_gain, v_mlp_pre_gain, v_mlp_post_gain, v_mlp_w_up, v_mlp_w_down, v_gdn_w_in, v_gdn_conv_w, v_gdn_a_log, v_gdn_dt_bias, v_gdn_out_gain, v_gdn_w_out, v_kv_gain, v_w_kv, v_sb_w_q, v_sb_w_o):
    me = 4 * lax.axis_index("x") + 2 * lax.axis_index("y") + lax.axis_index("c")
    bf = lambda a: a.astype(BF16)

    shards = {"up0": bf(mlp_w_up[0]), "up1": bf(mlp_w_up[1]), "down0": bf(mlp_w_down[0]), "down1": bf(mlp_w_down[1]),
              "gdn_out": bf(gdn_w_out[0]), "kv": bf(w_kv), "sb_q": bf(sb_w_q[0]), "sb_o": bf(sb_w_o[0])}
    gdn_in, conv = _gather_two_level([bf(gdn_w_in[0]), gdn_conv_w[0]], "gather_first_weights")
    comm = _Fsdp(shards, {"gdn_in": _whole_weight("gdn_in", gdn_in), "conv": _whole_weight("conv", conv)})
    gains = {"mix_pre": mix_pre_gain, "mix_post": mix_post_gain, "mlp_pre": mlp_pre_gain, "mlp_post": mlp_post_gain,
             "kv": kv_gain[None, :], "alog": _lanes(gdn_a_log[0], N_HEADS), "dtb": _lanes(gdn_dt_bias[0], N_HEADS),
             "out_gain": gdn_out_gain}

    grad_x, big, small = _local_step(x[0], loss_target[0], gains, comm)

    small_names = ["pre0", "x2", "post0", "post1", "mlp0", "mlp1", "gates", "out_gain", "conv", "loss"]
    landed = _exchange([small[nm] for nm in small_names], [True] * len(small_names), "exchange_small_grads")
    small_landed = dict(zip(small_names, landed))

    results = {}
    results["mlp_w_up"] = _reduce_adamw_layers((comm.landed["up0"], comm.landed["up1"]), mlp_w_up, m_mlp_w_up, v_mlp_w_up,
                                               "adamw_mlp_w_up")
    results["mlp_w_down"] = _reduce_adamw_layers((comm.landed["down0"], comm.landed["down1"]), mlp_w_down, m_mlp_w_down,
                                                 v_mlp_w_down, "adamw_mlp_w_down")
    big_params = [("gdn_w_in", "gdn_in", gdn_w_in[0], m_gdn_w_in[0], v_gdn_w_in[0]),
                  ("gdn_w_out", "gdn_out", gdn_w_out[0], m_gdn_w_out[0], v_gdn_w_out[0]),
                  ("w_kv", "kv", w_kv, m_w_kv, v_w_kv), ("sb_w_q", "sb_q", sb_w_q[0], m_sb_w_q[0], v_sb_w_q[0]),
                  ("sb_w_o", "sb_o", sb_w_o[0], m_sb_w_o[0], v_sb_w_o[0])]
    for nm, short, w_, m_, v_ in big_params:
        results[nm] = _reduce_adamw(comm.landed[short], w_, m_, v_, "adamw_" + nm)
    lanes8 = lambda a: _lanes(a[0], N_HEADS)
    small_params = {
        "mix_pre": (mix_pre_gain, m_mix_pre_gain, v_mix_pre_gain), "mix_post": (mix_post_gain, m_mix_post_gain, v_mix_post_gain),
        "mlp_pre": (mlp_pre_gain, m_mlp_pre_gain, v_mlp_pre_gain), "mlp_post": (mlp_post_gain, m_mlp_post_gain, v_mlp_post_gain),
        "kv": (kv_gain[None, :], m_kv_gain[None, :], v_kv_gain[None, :]),
        "alog": (lanes8(gdn_a_log), lanes8(m_gdn_a_log), lanes8(v_gdn_a_log)),
        "dtb": (lanes8(gdn_dt_bias), lanes8(m_gdn_dt_bias), lanes8(v_gdn_dt_bias)),
        "out_gain": (gdn_out_gain, m_gdn_out_gain, v_gdn_out_gain),
    }
    sm = _small_update(small_landed, small_params, "small_update")
    conv_cols = QKV // N_DEV
    g_conv = lax.dynamic_slice(sm["conv"], (0, me * conv_cols), (8, conv_cols))[:CONV_WIDTH]
    conv_res = (g_conv,) + tuple(_adamw(g_conv, gdn_conv_w[0], m_gdn_conv_w[0], v_gdn_conv_w[0], "adamw_conv"))

    lead = lambda t: tuple(a[None] for a in t)
    heads8 = lambda t: tuple(a[:, N_HEADS:2 * N_HEADS] for a in t)
    per_weight = [
        sm["mix_pre"], sm["mix_post"], sm["mlp_pre"], sm["mlp_post"],
        tuple(results["mlp_w_up"]), tuple(results["mlp_w_down"]),
        lead(results["gdn_w_in"]), lead(conv_res), heads8(sm["alog"]), heads8(sm["dtb"]), sm["out_gain"],
        lead(results["gdn_w_out"]), tuple(a[0] for a in sm["kv"]), results["w_kv"], lead(results["sb_w_q"]), lead(results["sb_w_o"]),
    ]
    grads, deltas, new_ms, new_vs = zip(*per_weight)
    return (sm["loss"][0, 0], grad_x[None], *grads, *deltas, *new_ms, *new_vs)
```

```python
import functools
import math

import jax
import jax.numpy as jnp
from jax import lax
from jax.experimental import pallas as pl
from jax.experimental.pallas import tpu as pltpu

F32 = jnp.float32
BF16 = jnp.bfloat16

N_DEV = 8
D_MODEL = 1024
D_FF = 4096
N_HEADS = 8
HEAD_DIM = 128
CHUNK = 64
CONV_WIDTH = 4
GDN_IN_COLS = 4 * D_MODEL + 2 * N_HEADS
GDN_IN_PAD = 4 * D_MODEL + 128
EPS = 1e-6

ADAM_LR = 0.001
ADAM_B1 = 0.9
ADAM_B2 = 0.999
ADAM_EPS = 1e-08
ADAM_WD = 0.01
ADAM_STEP = 10

VMEM_LIMIT_BYTES = 56 * 1024 * 1024
MESH = pl.DeviceIdType.MESH


def _params(*semantics):
    return pltpu.CompilerParams(dimension_semantics=semantics, vmem_limit_bytes=VMEM_LIMIT_BYTES)


def _dims(ta, tb):
    return (((0,) if ta else (1,), (1,) if tb else (0,)), ((), ()))


def _dot(a, b, ta=False, tb=False):
    return lax.dot_general(a.astype(BF16), b.astype(BF16), _dims(ta, tb), preferred_element_type=F32)


def _dot_f32(a, b, ta=False, tb=False):
    return lax.dot_general(a, b, _dims(ta, tb), precision=lax.Precision.HIGHEST, preferred_element_type=F32)


def _dot_3x(a, b, ta=False, tb=False):
    return lax.dot_general(a, b, _dims(ta, tb), precision=lax.Precision.HIGH, preferred_element_type=F32)


def _make_mm(dot):
    @functools.partial(jax.custom_vjp, nondiff_argnums=(2, 3))
    def mm(a, b, ta, tb):
        return dot(a, b, ta, tb)

    def fwd(a, b, ta, tb):
        return dot(a, b, ta, tb), (a, b)

    def bwd(ta, tb, res, g):
        a, b = res
        if not ta and not tb:
            return mm(g, b, False, True), mm(a, g, True, False)
        if not ta and tb:
            return mm(g, b, False, False), mm(g, a, True, False)
        if ta and not tb:
            return mm(b, g, False, True), mm(a, g, False, False)
        raise NotImplementedError

    mm.defvjp(fwd, bwd)
    return mm


_mm = _make_mm(_dot)
_mm_f32 = _make_mm(_dot_f32)


def _rms(x, gain):
    r = lax.rsqrt(jnp.mean(x * x, axis=-1, keepdims=True) + EPS)
    return x * r * gain


def _rms_bwd(x, gain, dy):
    r = lax.rsqrt(jnp.mean(x * x, axis=-1, keepdims=True) + EPS)
    xh = x * r
    dgain = jnp.sum(dy * xh, axis=0, keepdims=True)
    dxh = dy * gain
    dx = r * (dxh - xh * jnp.mean(dxh * xh, axis=-1, keepdims=True))
    return dx, dgain


def _silu(x):
    return x / (1.0 + jnp.exp(-x))


def _acc_rows(ref, first, rows):
    @pl.when(first)
    def _():
        ref[...] = jnp.zeros_like(ref)

    for r, val in enumerate(rows):
        ref[r:r + 1, :] += val


def _norm_matmul(x, gain, w, out_dtype, name, tm=1024, tn=1024):
    n, d = x.shape
    g, _, wc = w.shape
    tm, tn = min(tm, n), min(tn, wc)
    per = wc // tn
    assert n % tm == 0 and wc % tn == 0

    def body(x_ref, gain_ref, w_ref, out_ref, h_ref):
        @pl.when(pl.program_id(1) == 0)
        def _():
            h_ref[...] = _rms(x_ref[...], gain_ref[...]).astype(BF16)

        out_ref[...] = jnp.dot(h_ref[...], w_ref[...], preferred_element_type=F32).astype(out_dtype)

    return pl.pallas_call(
        body, name=name, grid=(n // tm, g * per),
        in_specs=[pl.BlockSpec((tm, d), lambda i, j: (i, 0)),
                  pl.BlockSpec((1, d), lambda i, j: (0, 0)),
                  pl.BlockSpec((None, d, tn), lambda i, j: (j // per, 0, j % per))],
        out_specs=[pl.BlockSpec((tm, tn), lambda i, j: (i, j)),
                   pl.BlockSpec((tm, d), lambda i, j: (i, 0))],
        out_shape=[jax.ShapeDtypeStruct((n, g * wc), out_dtype), jax.ShapeDtypeStruct((n, d), BF16)],
        compiler_params=_params("parallel", "arbitrary"),
    )(x, gain, w)


def _norm_matmul_bwd(branches, x, add, name, tm=1024, ride=None):
    n, d = x.shape
    tm = min(tm, n)
    tns = [min(tn, w.shape[2]) for _, w, _, tn in branches]
    pers = [w.shape[2] // tn for (_, w, _, _), tn in zip(branches, tns)]
    ncols = [w.shape[0] * per for (_, w, _, _), per in zip(branches, pers)]
    offs = [sum(ncols[:b]) for b in range(len(branches))]
    total = sum(ncols)
    nb = len(branches)

    def body(*refs):
        dout_refs, w_refs, gain_refs = refs[:nb], refs[nb:2 * nb], refs[2 * nb:3 * nb]
        x_ref, add_ref, dx_ref, dgain_ref, acc_ref, sum_ref = refs[3 * nb:]
        i, j = pl.program_id(0), pl.program_id(1)

        @pl.when((i == 0) & (j == 0))
        def _():
            dgain_ref[...] = jnp.zeros_like(dgain_ref)

        for b in range(nb):
            first, last = offs[b], offs[b] + ncols[b] - 1

            @pl.when((j >= first) & (j <= last))
            def _(b=b, first=first):
                part = _dot(dout_refs[b][...], w_refs[b][...], tb=True)

                @pl.when(j == first)
                def _():
                    acc_ref[...] = part

                @pl.when(j > first)
                def _():
                    acc_ref[...] += part

            @pl.when(j == last)
            def _(b=b):
                xv = x_ref[...]
                xh = xv * lax.rsqrt(jnp.mean(xv * xv, axis=-1, keepdims=True) + EPS)
                dh = acc_ref[...]
                dgain_ref[b:b + 1, :] += jnp.sum(dh * xh, axis=0, keepdims=True)
                scaled = dh * gain_refs[b][...]
                if b == 0:
                    sum_ref[...] = scaled
                else:
                    sum_ref[...] += scaled

        @pl.when(j == total - 1)
        def _():
            xv = x_ref[...]
            r = lax.rsqrt(jnp.mean(xv * xv, axis=-1, keepdims=True) + EPS)
            xh = xv * r
            dxh = sum_ref[...]
            dx_ref[...] = add_ref[...] + r * (dxh - xh * jnp.mean(dxh * xh, axis=-1, keepdims=True))

    def dout_spec(b):
        return pl.BlockSpec((tm, tns[b]), lambda i, j: (i, jnp.clip(j - offs[b], 0, ncols[b] - 1)))

    def w_spec(b):
        def index(i, j):
            c = jnp.clip(j - offs[b], 0, ncols[b] - 1)
            return (c // pers[b], 0, c % pers[b])
        return pl.BlockSpec((None, d, tns[b]), index)

    row = lambda i, j: (i, 0)
    const = lambda i, j: (0, 0)
    nrow = n // tm
    (dx, dgain), got = _call(
        body, [b[0] for b in branches] + [b[1] for b in branches] + [b[2] for b in branches] + [x, add],
        name=name, grid=(nrow, total),
        in_specs=[dout_spec(b) for b in range(nb)] + [w_spec(b) for b in range(nb)]
                 + [pl.BlockSpec((1, d), const)] * nb + [pl.BlockSpec((tm, d), row), pl.BlockSpec((tm, d), row)],
        out_specs=[pl.BlockSpec((tm, d), row), pl.BlockSpec((8, d), const)],
        out_shape=[jax.ShapeDtypeStruct((n, d), F32), jax.ShapeDtypeStruct((8, d), F32)],
        scratch_shapes=[pltpu.VMEM((tm, d), F32), pltpu.VMEM((tm, d), F32)],
        compiler_params=_params("arbitrary", "arbitrary"),
        ride=ride, first=lambda: (pl.program_id(0) == 0) & (pl.program_id(1) == 0),
        last=lambda: (pl.program_id(0) == nrow - 1) & (pl.program_id(1) == total - 1))
    return (dx, dgain) if ride is None else (dx, dgain, got)


def _matmul_tn(a, b, groups, name, ta=1024, tb=1024, tk=1024):
    n, ka = a.shape
    _, kb = b.shape
    wc = kb // groups
    ta, tb, tk = min(ta, ka), min(tb, wc), min(tk, n)
    per = wc // tb
    nk = n // tk
    assert ka % ta == 0 and wc % tb == 0 and n % tk == 0

    def body(a_ref, b_ref, out_ref, acc_ref):
        k = pl.program_id(2)

        @pl.when(k == 0)
        def _():
            acc_ref[...] = jnp.zeros_like(acc_ref)

        acc_ref[...] += _dot(a_ref[...], b_ref[...], ta=True)

        @pl.when(k == nk - 1)
        def _():
            out_ref[...] = acc_ref[...].astype(BF16)

    return pl.pallas_call(
        body, name=name, grid=(ka // ta, groups * per, nk),
        in_specs=[pl.BlockSpec((tk, ta), lambda i, j, k: (k, i)),
                  pl.BlockSpec((tk, tb), lambda i, j, k: (k, j))],
        out_specs=pl.BlockSpec((None, ta, tb), lambda i, j, k: (j // per, i, j % per)),
        out_shape=jax.ShapeDtypeStruct((groups, ka, wc), BF16),
        scratch_shapes=[pltpu.VMEM((ta, tb), F32)],
        compiler_params=_params("parallel", "parallel", "arbitrary"),
    )(a, b)


def _gated_head_norm(o, gate, out_gain):
    parts = []
    for h in range(N_HEADS):
        sl = slice(h * HEAD_DIM, (h + 1) * HEAD_DIM)
        parts.append(_rms(o[:, sl], out_gain) * _silu(gate[:, sl]))
    return parts


def _out_proj(a_inputs, gated, x_in, w, gain, out_gain, name, tm=1024):
    n, d = x_in.shape
    k = w.shape[0]
    tm = min(tm, n)

    def body(*refs):
        if gated:
            o_ref, gate_ref, og_ref, x_ref, w_ref, gain_ref, xo_ref, y_ref, a_ref = refs
            parts = _gated_head_norm(o_ref[...], gate_ref[...], og_ref[...])
            for h, part in enumerate(parts):
                a_ref[:, h * HEAD_DIM:(h + 1) * HEAD_DIM] = part.astype(BF16)
            a = a_ref[...]
        else:
            a_in_ref, x_ref, w_ref, gain_ref, xo_ref, y_ref = refs
            a = a_in_ref[...]
        y = jnp.dot(a, w_ref[...], preferred_element_type=F32)
        y_ref[...] = y
        xo_ref[...] = x_ref[...] + _rms(y, gain_ref[...])

    row = lambda i: (i, 0)
    const = lambda i: (0, 0)
    if gated:
        a_specs = [pl.BlockSpec((tm, k), row), pl.BlockSpec((tm, D_MODEL), lambda i: (i, 3)),
                   pl.BlockSpec((1, HEAD_DIM), const)]
        a_args = list(a_inputs) + [out_gain]
    else:
        a_specs = [pl.BlockSpec((tm, k), row)]
        a_args = list(a_inputs)
    out_specs = [pl.BlockSpec((tm, d), row), pl.BlockSpec((tm, d), row)]
    out_shape = [jax.ShapeDtypeStruct((n, d), F32), jax.ShapeDtypeStruct((n, d), F32)]
    if gated:
        out_specs.append(pl.BlockSpec((tm, k), row))
        out_shape.append(jax.ShapeDtypeStruct((n, k), BF16))
    return pl.pallas_call(
        body, name=name, grid=(n // tm,),
        in_specs=a_specs + [pl.BlockSpec((tm, d), row), pl.BlockSpec((k, d), const), pl.BlockSpec((1, d), const)],
        out_specs=out_specs, out_shape=out_shape,
        compiler_params=_params("parallel"),
    )(*a_args, x_in, w, gain)


def _out_proj_bwd(dxo, y, w, gain, gated_inputs, name, tm=1024):
    n, d = dxo.shape
    k = w.shape[0]
    tm = min(tm, n)
    gated = gated_inputs is not None

    def body(*refs):
        if gated:
            (dxo_ref, y_ref, w_ref, gain_ref, o_ref, gate_ref, og_ref,
             dy_ref, dgain_ref, do_ref, dgate_ref, dog_ref) = refs
        else:
            dxo_ref, y_ref, w_ref, gain_ref, dy_ref, dgain_ref, da_ref = refs
        first = pl.program_id(0) == 0
        dy, dgain = _rms_bwd(y_ref[...], gain_ref[...], dxo_ref[...])
        dy_ref[...] = dy.astype(BF16)
        _acc_rows(dgain_ref, first, [dgain])
        da = _dot(dy_ref[...], w_ref[...], tb=True)
        if not gated:
            da_ref[...] = da.astype(BF16)
            return
        og = og_ref[...]
        dog = jnp.zeros_like(og)
        for h in range(N_HEADS):
            sl = slice(h * HEAD_DIM, (h + 1) * HEAD_DIM)
            fn = lambda o_h, g_h, gn: _rms(o_h, gn) * _silu(g_h)
            _, vjp = jax.vjp(fn, o_ref[:, sl], gate_ref[:, sl], og)
            do_h, dgate_h, dog_h = vjp(da[:, sl])
            do_ref[:, sl] = do_h
            dgate_ref[:, sl] = dgate_h
            dog = dog + dog_h
        _acc_rows(dog_ref, first, [dog])

    row = lambda i: (i, 0)
    const = lambda i: (0, 0)
    in_specs = [pl.BlockSpec((tm, d), row), pl.BlockSpec((tm, d), row), pl.BlockSpec((k, d), const),
                pl.BlockSpec((1, d), const)]
    args = [dxo, y, w, gain]
    out_specs = [pl.BlockSpec((tm, d), row), pl.BlockSpec((8, d), const)]
    out_shape = [jax.ShapeDtypeStruct((n, d), BF16), jax.ShapeDtypeStruct((8, d), F32)]
    if gated:
        in_specs += [pl.BlockSpec((tm, k), row), pl.BlockSpec((tm, D_MODEL), lambda i: (i, 3)),
                     pl.BlockSpec((1, HEAD_DIM), const)]
        args += list(gated_inputs)
        out_specs += [pl.BlockSpec((tm, k), row), pl.BlockSpec((tm, k), row), pl.BlockSpec((8, HEAD_DIM), const)]
        out_shape += [jax.ShapeDtypeStruct((n, k), F32), jax.ShapeDtypeStruct((n, k), F32),
                      jax.ShapeDtypeStruct((8, HEAD_DIM), F32)]
    else:
        out_specs.append(pl.BlockSpec((tm, k), row))
        out_shape.append(jax.ShapeDtypeStruct((n, k), BF16))
    return pl.pallas_call(
        body, name=name, grid=(n // tm,), in_specs=in_specs, out_specs=out_specs, out_shape=out_shape,
        compiler_params=_params("arbitrary"),
    )(*args)


def _mlp_fwd(x_in, g_pre, w_up, w_down, g_post, name, tm=1024, target=None, ride=None):
    n, d = x_in.shape
    g, _, wc = w_up.shape
    tm = min(tm, n)
    with_loss = target is not None

    def body(*refs):
        if with_loss:
            x_ref, gpre_ref, wup_ref, wdown_ref, gpost_ref, t_ref, xo_ref, y_ref, h_ref, u_ref, a_ref, loss_ref, acc_ref = refs
        else:
            x_ref, gpre_ref, wup_ref, wdown_ref, gpost_ref, xo_ref, y_ref, h_ref, u_ref, a_ref, acc_ref = refs
        i, j = pl.program_id(0), pl.program_id(1)

        @pl.when(j == 0)
        def _():
            h_ref[...] = _rms(x_ref[...], gpre_ref[...]).astype(BF16)
            acc_ref[...] = jnp.zeros_like(acc_ref)

        u = jnp.dot(h_ref[...], wup_ref[...], preferred_element_type=F32).astype(BF16)
        u_ref[...] = u
        a = jnp.square(jnp.maximum(u, 0))
        a_ref[...] = a
        acc_ref[...] += jnp.dot(a, wdown_ref[...], preferred_element_type=F32)

        @pl.when(j == g - 1)
        def _():
            y = acc_ref[...]
            y_ref[...] = y
            xo = x_ref[...] + _rms(y, gpost_ref[...])
            if with_loss:
                err = xo - t_ref[...]
                xo_ref[...] = err * (1.0 / d)
                part = 0.5 * jnp.sum(jnp.mean(err * err, axis=-1, keepdims=True), axis=0, keepdims=True)
                _acc_rows(loss_ref, i == 0, [jnp.broadcast_to(part, (1, 128))])
            else:
                xo_ref[...] = xo

    row = lambda i, j: (i, 0)
    const = lambda i, j: (0, 0)
    in_specs = [pl.BlockSpec((tm, d), row), pl.BlockSpec((1, d), const),
                pl.BlockSpec((None, d, wc), lambda i, j: (j, 0, 0)),
                pl.BlockSpec((wc, d), lambda i, j: (j, 0)), pl.BlockSpec((1, d), const)]
    out_specs = [pl.BlockSpec((tm, d), row), pl.BlockSpec((tm, d), row), pl.BlockSpec((tm, d), row),
                 pl.BlockSpec((tm, wc), lambda i, j: (i, j)), pl.BlockSpec((tm, wc), lambda i, j: (i, j))]
    out_shape = [jax.ShapeDtypeStruct((n, d), F32), jax.ShapeDtypeStruct((n, d), F32),
                 jax.ShapeDtypeStruct((n, d), BF16), jax.ShapeDtypeStruct((n, g * wc), BF16),
                 jax.ShapeDtypeStruct((n, g * wc), BF16)]
    args = [x_in, g_pre, w_up, w_down, g_post]
    if with_loss:
        in_specs.append(pl.BlockSpec((tm, d), row))
        out_specs.append(pl.BlockSpec((8, 128), const))
        out_shape.append(jax.ShapeDtypeStruct((8, 128), F32))
        args.append(target)
    nrow = n // tm
    outs, got = _call(
        body, args, name=name, grid=(nrow, g), in_specs=in_specs, out_specs=out_specs, out_shape=out_shape,
        scratch_shapes=[pltpu.VMEM((tm, d), F32)], compiler_params=_params("arbitrary", "arbitrary"),
        ride=ride, first=lambda: (pl.program_id(0) == 0) & (pl.program_id(1) == 0),
        last=lambda: (pl.program_id(0) == nrow - 1) & (pl.program_id(1) == g - 1))
    return outs if ride is None else (outs, got)


def _mlp_bwd(dxo, y, g_post, w_down, u, w_up, x_in, g_pre, name, tm=1024, ride=None):
    n, d = dxo.shape
    g, _, wc = w_up.shape
    tm = min(tm, n)

    def body(dxo_ref, y_ref, gpost_ref, wdown_ref, u_ref, wup_ref, x_ref, gpre_ref,
             dx_ref, dy_ref, du_ref, dgain_ref, acc_ref, dgpost_ref):
        i, j = pl.program_id(0), pl.program_id(1)

        @pl.when(j == 0)
        def _():
            dy, dgpost = _rms_bwd(y_ref[...], gpost_ref[...], dxo_ref[...])
            dy_ref[...] = dy.astype(BF16)
            dgpost_ref[...] = dgpost
            acc_ref[...] = jnp.zeros_like(acc_ref)

        da = _dot(dy_ref[...], wdown_ref[...], tb=True)
        du = (da * (2.0 * jnp.maximum(u_ref[...], 0).astype(F32))).astype(BF16)
        du_ref[...] = du
        acc_ref[...] += _dot(du, wup_ref[...], tb=True)

        @pl.when(j == g - 1)
        def _():
            dx, dgpre = _rms_bwd(x_ref[...], gpre_ref[...], acc_ref[...])
            dx_ref[...] = dxo_ref[...] + dx
            _acc_rows(dgain_ref, i == 0, [dgpre, dgpost_ref[...]])

    row = lambda i, j: (i, 0)
    const = lambda i, j: (0, 0)
    nrow = n // tm
    outs, got = _call(
        body, [dxo, y, g_post, w_down, u, w_up, x_in, g_pre], name=name, grid=(nrow, g),
        in_specs=[pl.BlockSpec((tm, d), row), pl.BlockSpec((tm, d), row), pl.BlockSpec((1, d), const),
                  pl.BlockSpec((wc, d), lambda i, j: (j, 0)), pl.BlockSpec((tm, wc), lambda i, j: (i, j)),
                  pl.BlockSpec((None, d, wc), lambda i, j: (j, 0, 0)), pl.BlockSpec((tm, d), row),
                  pl.BlockSpec((1, d), const)],
        out_specs=[pl.BlockSpec((tm, d), row), pl.BlockSpec((tm, d), row),
                   pl.BlockSpec((tm, wc), lambda i, j: (i, j)), pl.BlockSpec((8, d), const)],
        out_shape=[jax.ShapeDtypeStruct((n, d), F32), jax.ShapeDtypeStruct((n, d), BF16),
                   jax.ShapeDtypeStruct((n, g * wc), BF16), jax.ShapeDtypeStruct((8, d), F32)],
        scratch_shapes=[pltpu.VMEM((tm, d), F32), pltpu.VMEM((1, d), F32)],
        compiler_params=_params("arbitrary", "arbitrary"),
        ride=ride, first=lambda: (pl.program_id(0) == 0) & (pl.program_id(1) == 0),
        last=lambda: (pl.program_id(0) == nrow - 1) & (pl.program_id(1) == g - 1))
    return outs if ride is None else (outs, got)


QKV = 3 * D_MODEL


def _shifted(x, prev8, s):
    if s == 0:
        return x
    tm = x.shape[0]
    rolled = pltpu.roll(x, s, 0)
    head = pltpu.roll(prev8, s, 0)
    head = jnp.concatenate([head, jnp.zeros((tm - 8, x.shape[1]), x.dtype)], axis=0)
    rows = lax.broadcasted_iota(jnp.int32, x.shape, 0)
    return jnp.where(rows < s, head, rolled)


def _conv(x, prev8, conv_w):
    out = x * conv_w[3:4, :]
    for s in range(1, CONV_WIDTH):
        out = out + _shifted(x, prev8, s) * conv_w[3 - s:4 - s, :]
    return out


def _l2norm(x):
    return x * lax.rsqrt(jnp.sum(x * x, axis=-1, keepdims=True) + EPS)


def _gdn_act_head(cq, ck, cv):
    return _l2norm(_silu(cq)) * (HEAD_DIM ** -0.5), _l2norm(_silu(ck)), _silu(cv)


def _gdn_gates(ba, alog, dtb):
    lane = lax.broadcasted_iota(jnp.int32, ba.shape, 1)
    beta = 1.0 / (1.0 + jnp.exp(-ba))
    t = ba + dtb
    softplus = jnp.maximum(t, 0.0) + jnp.log(1.0 + jnp.exp(-jnp.abs(t)))
    g = -jnp.exp(alog) * softplus
    return jnp.where(lane < N_HEADS, beta, jnp.where(lane < 2 * N_HEADS, g, 0.0))


def _qkv_cols(part, h):
    start = part * D_MODEL + h * HEAD_DIM
    return slice(start, start + HEAD_DIM)


def _prev_rows_spec(tm, cols, colblock, order):
    per = tm // 8
    return pl.BlockSpec((8, cols), lambda i: (jnp.maximum(order(i) * per - 1, 0), colblock))


def _gdn_pre(proj, conv_w, alog, dtb, name, tm=256):
    n = proj.shape[0]

    def body(x_ref, prev_ref, ba_ref, cw_ref, alog_ref, dtb_ref, q_ref, k_ref, v_ref, bg_ref):
        first = pl.program_id(0) == 0
        for h in range(N_HEADS):
            convs = []
            for part in range(3):
                sl = _qkv_cols(part, h)
                prev8 = jnp.where(first, 0.0, prev_ref[:, sl])
                convs.append(_conv(x_ref[:, sl], prev8, cw_ref[:, sl]))
            out = slice(h * HEAD_DIM, (h + 1) * HEAD_DIM)
            q_ref[:, out], k_ref[:, out], v_ref[:, out] = _gdn_act_head(*convs)
        bg_ref[...] = _gdn_gates(ba_ref[...], alog_ref[...], dtb_ref[...])

    row = lambda i: (i, 0)
    const = lambda i: (0, 0)
    ident = lambda i: i
    return pl.pallas_call(
        body, name=name, grid=(n // tm,),
        in_specs=[pl.BlockSpec((tm, QKV), row), _prev_rows_spec(tm, QKV, 0, ident),
                  pl.BlockSpec((tm, 128), lambda i: (i, 4 * D_MODEL // 128)),
                  pl.BlockSpec((CONV_WIDTH, QKV), const), pl.BlockSpec((1, 128), const), pl.BlockSpec((1, 128), const)],
        out_specs=[pl.BlockSpec((tm, D_MODEL), row)] * 3 + [pl.BlockSpec((tm, 128), row)],
        out_shape=[jax.ShapeDtypeStruct((n, D_MODEL), F32)] * 3 + [jax.ShapeDtypeStruct((n, 128), F32)],
        compiler_params=_params("parallel"),
    )(proj, proj, proj, conv_w, alog, dtb)


def _gdn_pre_bwd(proj, conv_w, alog, dtb, dq, dk, dv, dbg, dgate, name, tm=256):
    n = proj.shape[0]
    nt = n // tm

    def body(x_ref, prev_ref, ba_ref, cw_ref, alog_ref, dtb_ref, dq_ref, dk_ref, dv_ref, dbg_ref, dgate_ref,
             dproj_ref, dcw_ref, dgates_ref, carry_ref):
        step = pl.program_id(0)
        tile = nt - 1 - step
        @pl.when(step == 0)
        def _():
            carry_ref[...] = jnp.zeros_like(carry_ref)
            dcw_ref[...] = jnp.zeros_like(dcw_ref)

        rows = lax.broadcasted_iota(jnp.int32, (tm, HEAD_DIM), 0)
        for h in range(N_HEADS):
            cols = [_qkv_cols(part, h) for part in range(3)]
            prevs = [jnp.where(tile == 0, 0.0, prev_ref[:, sl]) for sl in cols]
            convs = [_conv(x_ref[:, sl], prev8, cw_ref[:, sl]) for sl, prev8 in zip(cols, prevs)]
            _, vjp = jax.vjp(_gdn_act_head, *convs)
            out = slice(h * HEAD_DIM, (h + 1) * HEAD_DIM)
            dcs = vjp((dq_ref[:, out], dk_ref[:, out], dv_ref[:, out]))
            for sl, prev8, dc in zip(cols, prevs, dcs):
                x = x_ref[:, sl]
                cw = cw_ref[:, sl]
                dx = dc * cw[3:4, :]
                wrapped = jnp.zeros((8, HEAD_DIM), F32)
                dcw_ref[3:4, sl] += jnp.sum(dc * x, axis=0, keepdims=True)
                for s in range(1, CONV_WIDTH):
                    r = pltpu.roll(dc, tm - s, 0) * cw[3 - s:4 - s, :]
                    dx = dx + jnp.where(rows < tm - s, r, 0.0)
                    wrapped = wrapped + jnp.where(rows[tm - 8:, :] >= tm - s, r[tm - 8:, :], 0.0)
                    dcw_ref[3 - s:4 - s, sl] += jnp.sum(dc * _shifted(x, prev8, s), axis=0, keepdims=True)
                dproj_ref[:, sl] = dx
                dproj_ref[tm - 8:, sl] += carry_ref[:, sl]
                carry_ref[:, sl] = wrapped

        _, vjp = jax.vjp(_gdn_gates, ba_ref[...], alog_ref[...], dtb_ref[...])
        dba, dalog, ddtb = vjp(dbg_ref[...])
        dproj_ref[:, QKV:4 * D_MODEL] = dgate_ref[...]
        dproj_ref[:, 4 * D_MODEL:] = dba
        _acc_rows(dgates_ref, step == 0, [dalog, ddtb])

    rev = lambda i: nt - 1 - i
    row = lambda i: (rev(i), 0)
    const = lambda i: (0, 0)
    return pl.pallas_call(
        body, name=name, grid=(nt,),
        in_specs=[pl.BlockSpec((tm, QKV), row), _prev_rows_spec(tm, QKV, 0, rev),
                  pl.BlockSpec((tm, 128), lambda i: (rev(i), 4 * D_MODEL // 128)),
                  pl.BlockSpec((CONV_WIDTH, QKV), const), pl.BlockSpec((1, 128), const), pl.BlockSpec((1, 128), const),
                  pl.BlockSpec((tm, D_MODEL), row), pl.BlockSpec((tm, D_MODEL), row), pl.BlockSpec((tm, D_MODEL), row),
                  pl.BlockSpec((tm, 128), row), pl.BlockSpec((tm, D_MODEL), row)],
        out_specs=[pl.BlockSpec((tm, GDN_IN_PAD), row), pl.BlockSpec((8, QKV), const), pl.BlockSpec((8, 128), const)],
        out_shape=[jax.ShapeDtypeStruct((n, GDN_IN_PAD), F32), jax.ShapeDtypeStruct((8, QKV), F32),
                   jax.ShapeDtypeStruct((8, 128), F32)],
        scratch_shapes=[pltpu.VMEM((8, QKV), F32)],
        compiler_params=_params("arbitrary"),
    )(proj, proj, proj, conv_w, alog, dtb, dq, dk, dv, dbg, dgate)


@jax.custom_vjp
def _unit_lower_inverses(lowers):
    n = lowers[0].shape[0]
    eye = (lax.broadcasted_iota(jnp.int32, (n, n), 0) == lax.broadcasted_iota(jnp.int32, (n, n), 1)).astype(F32)
    xs = [-low for low in lowers]
    ps = [eye + x for x in xs]
    ys = [_dot_3x(x, x) for x in xs]
    for _ in range(int(math.log2(CHUNK)) - 2):
        both = [_dot_3x(y, jnp.concatenate([y, p], axis=1)) for y, p in zip(ys, ps)]
        ys = [b[:, :n] for b in both]
        ps = [p + b[:, n:] for p, b in zip(ps, both)]
    return tuple(p + _dot_3x(y, p) for y, p in zip(ys, ps))


def _unit_lower_inverses_fwd(lowers):
    ts = _unit_lower_inverses(lowers)
    return ts, ts


def _unit_lower_inverses_bwd(ts, dts):
    left = [_dot_3x(t, dt, ta=True) for t, dt in zip(ts, dts)]
    return (tuple(-_dot_3x(l, t, tb=True) for l, t in zip(left, ts)),)


_unit_lower_inverses.defvjp(_unit_lower_inverses_fwd, _unit_lower_inverses_bwd)


@jax.custom_vjp
def _known_inverses(lowers, inverses):
    return inverses


def _known_inverses_bwd(ts, dts):
    (d_lowers,) = _unit_lower_inverses_bwd(ts, dts)
    return d_lowers, tuple(jnp.zeros_like(t) for t in ts)


_known_inverses.defvjp(lambda lowers, inverses: (inverses, inverses), _known_inverses_bwd)


@functools.partial(jax.custom_vjp, nondiff_argnums=(1,))
def _halves(x, axis):
    half = x.shape[axis] // 2
    return (x[:half], x[half:]) if axis == 0 else (x[:, :half], x[:, half:])


_halves.defvjp(lambda x, axis: (_halves(x, axis), None), lambda axis, _, g: (jnp.concatenate(g, axis=axis),))

GDN_STEP_CHUNKS = 2


def _gdn_chunks(qs, ks, vs, bgs, states, inverses=None, keep_inverses=False):
    c = CHUNK
    heads = range(N_HEADS)
    items = [(j, h) for j in range(len(bgs)) for h in heads]
    row = lax.broadcasted_iota(jnp.int32, (c, c), 0)
    col = lax.broadcasted_iota(jnp.int32, (c, c), 1)
    incl, strict, eye = row >= col, row > col, row == col
    lane = lax.broadcasted_iota(jnp.int32, (c, 128), 1)
    rowc = lax.broadcasted_iota(jnp.int32, (c, 1), 0)
    gc_all = [_mm_f32(incl.astype(F32), bg, False, False) for bg in bgs]
    q, k, v = ([xs[j][h] for j, h in items] for xs in (qs, ks, vs))
    n = range(len(items))
    beta = [jnp.sum(jnp.where(lane == h, bgs[j], 0.0), axis=1, keepdims=True) for j, h in items]
    gc = [jnp.sum(jnp.where(lane == N_HEADS + h, gc_all[j], 0.0), axis=1, keepdims=True) for j, h in items]
    gc_row = [jnp.sum(jnp.where(eye, gc[i], 0.0), axis=0, keepdims=True) for i in n]
    gc_last = [jnp.sum(jnp.where(rowc == c - 1, gc[i], 0.0), axis=0, keepdims=True) for i in n]
    decay = [jnp.where(incl, jnp.exp(jnp.where(incl, gc[i] - gc_row[i], 0.0)), 0.0) for i in n]
    kb = [k[i] * beta[i] for i in n]
    kb_q = [_halves(_mm(jnp.concatenate([kb[i], q[i]], axis=0), k[i], False, True), 0) for i in n]
    lower = tuple(jnp.where(strict, kb_q[i][0] * decay[i], 0.0) for i in n)
    attn = [kb_q[i][1] * decay[i] for i in n]
    t_mat = _unit_lower_inverses(lower) if inverses is None else _known_inverses(lower, inverses)
    egc = [jnp.exp(gc[i]) for i in n]
    w_u = [_halves(_mm(t_mat[i], jnp.concatenate([kb[i] * egc[i], v[i] * beta[i]], axis=1), False, False), 1)
           for i in n]
    w, u = [x[0] for x in w_u], [x[1] for x in w_u]
    qg = [q[i] * egc[i] for i in n]
    kg = [k[i] * jnp.exp(gc_last[i] - gc[i]) for i in n]
    outs, cur = [], list(states)
    for j in range(len(bgs)):
        at = lambda h: j * N_HEADS + h
        w_qg = [_halves(_mm(jnp.concatenate([w[at(h)], qg[at(h)]], axis=0), cur[h], False, False), 0)
                for h in heads]
        v_new = [u[at(h)] - w_qg[h][0] for h in heads]
        outs.append(tuple(w_qg[h][1] + _mm(attn[at(h)], v_new[h], False, False) for h in heads))
        cur = [cur[h] * jnp.exp(gc_last[at(h)]) + _mm(kg[at(h)], v_new[h], True, False) for h in heads]
    return (tuple(outs), tuple(cur), t_mat) if keep_inverses else (tuple(outs), tuple(cur))


def _chunk_head_slices(ref, nch):
    return tuple(tuple(ref[j * CHUNK:(j + 1) * CHUNK, h * HEAD_DIM:(h + 1) * HEAD_DIM] for h in range(N_HEADS))
                 for j in range(nch))


def _store_chunk_heads(ref, values):
    for j, chunk in enumerate(values):
        for h, val in enumerate(chunk):
            ref[j * CHUNK:(j + 1) * CHUNK, h * HEAD_DIM:(h + 1) * HEAD_DIM] = val


def _gdn_scan(qn, kn, v, bg, name, ride=None):
    n = qn.shape[0]
    nch = GDN_STEP_CHUNKS
    rows = nch * CHUNK
    nc = n // rows

    def body(q_ref, k_ref, v_ref, bg_ref, o_ref, saved_ref, inv_ref, state_ref):
        @pl.when(pl.program_id(0) == 0)
        def _():
            state_ref[...] = jnp.zeros_like(state_ref)

        states = tuple(state_ref[h] for h in range(N_HEADS))
        for h in range(N_HEADS):
            saved_ref[h] = states[h]
        bgs = tuple(bg_ref[j * CHUNK:(j + 1) * CHUNK, :] for j in range(nch))
        outs, new_states, inverses = _gdn_chunks(_chunk_head_slices(q_ref, nch), _chunk_head_slices(k_ref, nch),
                                                 _chunk_head_slices(v_ref, nch), bgs, states, keep_inverses=True)
        _store_chunk_heads(o_ref, outs)
        for h in range(N_HEADS):
            state_ref[h] = new_states[h]
        for item, inverse in enumerate(inverses):
            inv_ref[item] = inverse

    row = lambda i: (i, 0)
    return _call(
        body, [qn, kn, v, bg], name=name, grid=(nc,),
        in_specs=[pl.BlockSpec((rows, D_MODEL), row)] * 3 + [pl.BlockSpec((rows, 128), row)],
        out_specs=[pl.BlockSpec((rows, D_MODEL), row),
                   pl.BlockSpec((None, N_HEADS, HEAD_DIM, HEAD_DIM), lambda i: (i, 0, 0, 0)),
                   pl.BlockSpec((None, nch * N_HEADS, CHUNK, CHUNK), lambda i: (i, 0, 0, 0))],
        out_shape=[jax.ShapeDtypeStruct((n, D_MODEL), F32),
                   jax.ShapeDtypeStruct((nc, N_HEADS, HEAD_DIM, HEAD_DIM), F32),
                   jax.ShapeDtypeStruct((nc, nch * N_HEADS, CHUNK, CHUNK), F32)],
        scratch_shapes=[pltpu.VMEM((N_HEADS, HEAD_DIM, HEAD_DIM), F32)],
        compiler_params=_params("arbitrary"),
        ride=ride, first=lambda: pl.program_id(0) == 0, last=lambda: pl.program_id(0) == nc - 1)


def _gdn_scan_bwd(qn, kn, v, bg, saved, inverses, do, name, ride=None):
    n = qn.shape[0]
    nch = GDN_STEP_CHUNKS
    rows = nch * CHUNK
    nc = n // rows

    def body(q_ref, k_ref, v_ref, bg_ref, saved_ref, inv_ref, do_ref, dq_ref, dk_ref, dv_ref, dbg_ref, dstate_ref):
        @pl.when(pl.program_id(0) == 0)
        def _():
            dstate_ref[...] = jnp.zeros_like(dstate_ref)

        states = tuple(saved_ref[h] for h in range(N_HEADS))
        bgs = tuple(bg_ref[j * CHUNK:(j + 1) * CHUNK, :] for j in range(nch))
        known = tuple(inv_ref[item] for item in range(nch * N_HEADS))
        _, vjp = jax.vjp(functools.partial(_gdn_chunks, inverses=known), _chunk_head_slices(q_ref, nch),
                         _chunk_head_slices(k_ref, nch), _chunk_head_slices(v_ref, nch), bgs, states)
        dstates = tuple(dstate_ref[h] for h in range(N_HEADS))
        dqs, dks, dvs, dbgs, dprev = vjp((_chunk_head_slices(do_ref, nch), dstates))
        _store_chunk_heads(dq_ref, dqs)
        _store_chunk_heads(dk_ref, dks)
        _store_chunk_heads(dv_ref, dvs)
        for h in range(N_HEADS):
            dstate_ref[h] = dprev[h]
        for j in range(nch):
            dbg_ref[j * CHUNK:(j + 1) * CHUNK, :] = dbgs[j]

    row = lambda i: (nc - 1 - i, 0)
    return _call(
        body, [qn, kn, v, bg, saved, inverses, do], name=name, grid=(nc,),
        in_specs=[pl.BlockSpec((rows, D_MODEL), row)] * 3 + [pl.BlockSpec((rows, 128), row),
                  pl.BlockSpec((None, N_HEADS, HEAD_DIM, HEAD_DIM), lambda i: (nc - 1 - i, 0, 0, 0)),
                  pl.BlockSpec((None, nch * N_HEADS, CHUNK, CHUNK), lambda i: (nc - 1 - i, 0, 0, 0)),
                  pl.BlockSpec((rows, D_MODEL), row)],
        out_specs=[pl.BlockSpec((rows, D_MODEL), row)] * 3 + [pl.BlockSpec((rows, 128), row)],
        out_shape=[jax.ShapeDtypeStruct((n, D_MODEL), F32)] * 3 + [jax.ShapeDtypeStruct((n, 128), F32)],
        scratch_shapes=[pltpu.VMEM((N_HEADS, HEAD_DIM, HEAD_DIM), F32)],
        compiler_params=_params("arbitrary"),
        ride=ride, first=lambda: pl.program_id(0) == 0, last=lambda: pl.program_id(0) == nc - 1)


SB_BQ = 512
SB_BK = 256
SB_SUB = 128
SB_ROWS = 128
SB_SCALE = HEAD_DIM ** -0.5
SB_DEAD = -105.0


def _sb_terms(z, before):
    e = jnp.exp(-jnp.abs(z))
    log_beta = jnp.minimum(z, 0.0) - jnp.log(1.0 + e)
    log_1m = log_beta - z
    if before is not None:
        log_1m = jnp.where(before, log_1m, 0.0)
    return e, log_beta, log_1m


def _tri_ones(n, cmp):
    r = lax.broadcasted_iota(jnp.int32, (n, 2 * n), 0)
    c = lax.broadcasted_iota(jnp.int32, (n, 2 * n), 1)
    return jnp.where((c >= n) | cmp(r, c), 1.0, 0.0).astype(BF16)


def _sums(x, tri_ones):
    both = jnp.dot(x.astype(BF16), tri_ones, preferred_element_type=F32)
    n = x.shape[1]
    return both[:, :n], both[:, n:]


def _sb_chunk_mask(diagonal, r, s):
    if not diagonal or s * SB_SUB + SB_SUB - 1 < r * SB_ROWS:
        return None
    if s * SB_SUB >= r * SB_ROWS + SB_ROWS - 1:
        return "empty"
    rows = r * SB_ROWS + lax.broadcasted_iota(jnp.int32, (SB_ROWS, SB_SUB), 0)
    cols = s * SB_SUB + lax.broadcasted_iota(jnp.int32, (SB_ROWS, SB_SUB), 1)
    return cols < rows


def _sb_attention(q, kv, name, bq=SB_BQ, ride=None):
    n = q.shape[0]
    bq = min(bq, n)
    bk = min(SB_BK, bq)
    nrc = bq // SB_ROWS

    def body(q_ref, k_ref, v_ref, o_ref, l_ref, n_ref, z_scr, a_scr):
        i = pl.program_id(1)
        qb = q_ref[...]
        after = _tri_ones(SB_SUB, lambda r, c: r > c)

        def span(start, kw, c_sum, acc, diagonal):
            start = pl.multiple_of(start, kw)
            k_w = k_ref[pl.ds(start, kw), :]
            v_w = v_ref[pl.ds(start, kw), :]
            z_scr[:, :kw] = lax.dot_general(qb, k_w, _dims(False, True), preferred_element_type=F32)
            c_rows = [c_sum[r * SB_ROWS:(r + 1) * SB_ROWS] for r in range(nrc)]
            for s in reversed(range(kw // SB_SUB)):
                cols = slice(s * SB_SUB, (s + 1) * SB_SUB)
                for r in range(nrc):
                    rows = slice(r * SB_ROWS, (r + 1) * SB_ROWS)
                    before = _sb_chunk_mask(diagonal, r, s)
                    if isinstance(before, str):
                        a_scr[rows, cols] = jnp.zeros((SB_ROWS, SB_SUB), BF16)
                        continue
                    _, log_beta, log_1m = _sb_terms(z_scr[rows, cols] * SB_SCALE, before)
                    tail, total = _sums(log_1m, after)
                    a = jnp.exp(log_beta + c_rows[r] + tail)
                    if before is not None:
                        a = jnp.where(before, a, 0.0)
                    a_scr[rows, cols] = a.astype(BF16)
                    c_rows[r] = c_rows[r] + total
            acc = acc + jnp.dot(a_scr[:, :kw], v_w, preferred_element_type=F32)
            return jnp.concatenate(c_rows, axis=0), acc

        c_sum, acc = span(i * bq, bq, jnp.zeros((bq, SB_SUB), F32), jnp.zeros((bq, HEAD_DIM), F32), True)

        def more(state):
            done, c, _ = state
            return (done < i * (bq // bk)) & (jnp.max(c) > SB_DEAD)

        def step(state):
            done, c, a = state
            c, a = span(i * bq - (done + 1) * bk, bk, c, a, False)
            return done + 1, c, a

        done, c_sum, acc = lax.while_loop(more, step, (jnp.int32(0), c_sum, acc))
        o_ref[...] = acc.astype(BF16)
        l_ref[...] = c_sum
        n_ref[...] = jnp.full((8, 128), done.astype(F32), F32)

    nq = n // bq
    return _call(
        body, [q, kv, kv], name=name, grid=(N_HEADS, nq),
        in_specs=[pl.BlockSpec((bq, HEAD_DIM), lambda h, i: (i, h)),
                  pl.BlockSpec((n, HEAD_DIM), lambda h, i: (0, h)),
                  pl.BlockSpec((n, HEAD_DIM), lambda h, i: (0, N_HEADS + h))],
        out_specs=[pl.BlockSpec((bq, HEAD_DIM), lambda h, i: (i, h)),
                   pl.BlockSpec((None, bq, 128), lambda h, i: (h, i, 0)),
                   pl.BlockSpec((None, None, 8, 128), lambda h, i: (h, i, 0, 0))],
        out_shape=[jax.ShapeDtypeStruct((n, D_MODEL), BF16), jax.ShapeDtypeStruct((N_HEADS, n, 128), F32),
                   jax.ShapeDtypeStruct((N_HEADS, nq, 8, 128), F32)],
        scratch_shapes=[pltpu.VMEM((bq, bq), F32), pltpu.VMEM((bq, bq), BF16)],
        compiler_params=_params("arbitrary", "arbitrary"),
        ride=ride, first=lambda: (pl.program_id(0) == 0) & (pl.program_id(1) == 0),
        last=lambda: (pl.program_id(0) == N_HEADS - 1) & (pl.program_id(1) == nq - 1))


def _sb_attention_bwd(q, kv, do, lsum, spans, name, bq=SB_BQ, ride=None):
    n = q.shape[0]
    bq = min(bq, n)
    bk = min(SB_BK, bq)
    nrc = bq // SB_ROWS

    def body(q_ref, k_ref, v_ref, do_ref, l_ref, n_ref, dq_ref, dk_ref, dv_ref, z_scr, da_scr, a_scr, dz_scr):
        i = pl.program_id(1)

        @pl.when(i == 0)
        def _():
            dk_ref[...] = jnp.zeros_like(dk_ref)
            dv_ref[...] = jnp.zeros_like(dv_ref)

        qb = q_ref[...]
        dob = do_ref[...]
        lt_rows = [l_ref[r * SB_ROWS:(r + 1) * SB_ROWS, :] for r in range(nrc)]
        upto = _tri_ones(SB_SUB, lambda r, c: r <= c)
        below = _tri_ones(SB_SUB, lambda r, c: r < c)

        def span(start, kw, l_sum, g_sum, dq, diagonal):
            start = pl.multiple_of(start, kw)
            k_w = k_ref[pl.ds(start, kw), :]
            v_w = v_ref[pl.ds(start, kw), :]
            z_scr[:, :kw] = lax.dot_general(qb, k_w, _dims(False, True), preferred_element_type=F32)
            da_scr[:, :kw] = lax.dot_general(dob, v_w, _dims(False, True), preferred_element_type=F32)
            l_rows = [l_sum[r * SB_ROWS:(r + 1) * SB_ROWS] for r in range(nrc)]
            g_rows = [g_sum[r * SB_ROWS:(r + 1) * SB_ROWS] for r in range(nrc)]
            for s in range(kw // SB_SUB):
                cols = slice(s * SB_SUB, (s + 1) * SB_SUB)
                for r in range(nrc):
                    rows = slice(r * SB_ROWS, (r + 1) * SB_ROWS)
                    before = _sb_chunk_mask(diagonal, r, s)
                    if isinstance(before, str):
                        a_scr[rows, cols] = jnp.zeros((SB_ROWS, SB_SUB), BF16)
                        dz_scr[rows, cols] = jnp.zeros((SB_ROWS, SB_SUB), BF16)
                        continue
                    zs = z_scr[rows, cols] * SB_SCALE
                    e, log_beta, log_1m = _sb_terms(zs, before)
                    l_prefix, l_total = _sums(log_1m, upto)
                    a = jnp.exp(log_beta + (lt_rows[r] - (l_rows[r] + l_prefix)))
                    if before is not None:
                        a = jnp.where(before, a, 0.0)
                    g = a * da_scr[rows, cols]
                    g_prefix, g_total = _sums(g, below)
                    inv = 1.0 / (1.0 + e)
                    beta = jnp.where(zs >= 0, inv, e * inv)
                    dz = (g - (g + g_rows[r] + g_prefix) * beta) * SB_SCALE
                    if before is not None:
                        dz = jnp.where(before, dz, 0.0)
                    a_scr[rows, cols] = a.astype(BF16)
                    dz_scr[rows, cols] = dz.astype(BF16)
                    l_rows[r] = l_rows[r] + l_total
                    g_rows[r] = g_rows[r] + g_total
            dz_w = dz_scr[:, :kw]
            dq = dq + jnp.dot(dz_w, k_w, preferred_element_type=F32)
            dk_ref[pl.ds(start, kw), :] += lax.dot_general(dz_w, qb, _dims(True, False), preferred_element_type=F32)
            dv_ref[pl.ds(start, kw), :] += lax.dot_general(a_scr[:, :kw], dob, _dims(True, False),
                                                           preferred_element_type=F32)
            return jnp.concatenate(l_rows, axis=0), jnp.concatenate(g_rows, axis=0), dq

        taken = jnp.clip(jnp.max(n_ref[...]).astype(jnp.int32), 0, i * (bq // bk))
        zero = jnp.zeros((bq, SB_SUB), F32)
        carry = lax.fori_loop(0, taken, lambda j, c: span(i * bq - (taken - j) * bk, bk, c[0], c[1], c[2], False),
                              (zero, zero, jnp.zeros((bq, HEAD_DIM), F32)))
        _, _, dq = span(i * bq, bq, carry[0], carry[1], carry[2], True)
        dq_ref[...] = dq.astype(BF16)

    nq = n // bq
    return _call(
        body, [q, kv, kv, do, lsum, spans], name=name, grid=(N_HEADS, nq),
        in_specs=[pl.BlockSpec((bq, HEAD_DIM), lambda h, i: (i, h)),
                  pl.BlockSpec((n, HEAD_DIM), lambda h, i: (0, h)),
                  pl.BlockSpec((n, HEAD_DIM), lambda h, i: (0, N_HEADS + h)),
                  pl.BlockSpec((bq, HEAD_DIM), lambda h, i: (i, h)),
                  pl.BlockSpec((None, bq, 128), lambda h, i: (h, i, 0)),
                  pl.BlockSpec((None, None, 8, 128), lambda h, i: (h, i, 0, 0))],
        out_specs=[pl.BlockSpec((bq, HEAD_DIM), lambda h, i: (i, h)),
                   pl.BlockSpec((n, HEAD_DIM), lambda h, i: (0, h)),
                   pl.BlockSpec((n, HEAD_DIM), lambda h, i: (0, h))],
        out_shape=[jax.ShapeDtypeStruct((n, D_MODEL), BF16), jax.ShapeDtypeStruct((n, D_MODEL), F32),
                   jax.ShapeDtypeStruct((n, D_MODEL), F32)],
        scratch_shapes=[pltpu.VMEM((bq, bq), F32), pltpu.VMEM((bq, bq), F32), pltpu.VMEM((bq, bq), BF16),
                        pltpu.VMEM((bq, bq), BF16)],
        compiler_params=_params("arbitrary", "arbitrary"),
        ride=ride, first=lambda: (pl.program_id(0) == 0) & (pl.program_id(1) == 0),
        last=lambda: (pl.program_id(0) == N_HEADS - 1) & (pl.program_id(1) == nq - 1))


def _local_step(x, target, gains, comm):
    g = gains
    w = comm.w
    big = {}
    row = lambda a, i: a[i:i + 1, :]

    proj, h0 = _norm_matmul(x, row(g["mix_pre"], 0), w["gdn_in"], F32, "gdn_in_proj", tn=1408)
    qn, kn, v, bg = _gdn_pre(proj, w["conv"], g["alog"], g["dtb"], "gdn_pre")
    (o_gdn, saved, inverses), got = _gdn_scan(qn, kn, v, bg, "gdn_scan", ride=comm.ride("scan"))
    comm.done("scan", got)
    x1, y_mix0, a_gdn = _out_proj((o_gdn, proj), True, x, w["gdn_out"], row(g["mix_post"], 0), g["out_gain"],
                                  "gdn_out_proj", tm=512)
    ride = comm.ride("mlp0")
    res = _mlp_fwd(x1, row(g["mlp_pre"], 0), w["up0"], w["down0"], row(g["mlp_post"], 0), "mlp0", ride=ride)
    (x2, y_mlp0, h_mlp0, u0, a0), got = res if ride is not None else (res, [])
    comm.done("mlp0", got)
    kv, h_kv = _norm_matmul(x2, g["kv"], w["kv"], BF16, "kv_proj", tm=2048)
    q, h_q = _norm_matmul(x2, row(g["mix_pre"], 1), w["sb_q"], BF16, "sb_q_proj")
    (o_sb, lsum, spans), got = _sb_attention(q, kv, "sb_attention", ride=comm.ride("sb"))
    comm.done("sb", got)
    x3, y_mix1 = _out_proj((o_sb,), False, x2, w["sb_o"], row(g["mix_post"], 1), None, "sb_out_proj")
    dx4, y_mlp1, h_mlp1, u1, a1, loss = _mlp_fwd(x3, row(g["mlp_pre"], 1), w["up1"], w["down1"], row(g["mlp_post"], 1),
                                                 "mlp1_loss", target=target)

    dx3, dy_mlp1, du1, dg_mlp1 = _mlp_bwd(dx4, y_mlp1, row(g["mlp_post"], 1), w["down1"], u1, w["up1"], x3,
                                          row(g["mlp_pre"], 1), "mlp1_bwd")
    big["down1"] = _matmul_tn(a1, dy_mlp1, 1, "d_down1")[0]
    big["up1"] = _matmul_tn(h_mlp1, du1, N_DEV, "d_up1", tk=2048)
    dy_mix1, dg_post1, do_sb = _out_proj_bwd(dx3, y_mix1, w["sb_o"], row(g["mix_post"], 1), None, "sb_out_proj_bwd")
    big["sb_o"] = _matmul_tn(o_sb, dy_mix1, 1, "d_sb_o")[0]
    (dq, dk, dv), got = _sb_attention_bwd(q, kv, do_sb, lsum, spans, "sb_attention_bwd", ride=comm.ride("sb_bwd", big))
    comm.done("sb_bwd", got)
    big["sb_q"] = _matmul_tn(h_q, dq, 1, "d_sb_q")[0]
    big["kv"] = jnp.concatenate([_matmul_tn(h_kv, dk, N_DEV // 2, "d_w_k", tk=2048),
                                 _matmul_tn(h_kv, dv, N_DEV // 2, "d_w_v", tk=2048)],
                                axis=0)
    dx2, dg_x2 = _norm_matmul_bwd([(dq, w["sb_q"], row(g["mix_pre"], 1), 1024), (dk, w["kv"][0:1], g["kv"], 1024),
                                   (dv, w["kv"][1:2], g["kv"], 1024)], x2, dx3, "qkv_proj_bwd", tm=512)

    ride = comm.ride("mlp0_bwd", big)
    res = _mlp_bwd(dx2, y_mlp0, row(g["mlp_post"], 0), w["down0"], u0, w["up0"], x1, row(g["mlp_pre"], 0), "mlp0_bwd",
                   ride=ride)
    (dx1, dy_mlp0, du0, dg_mlp0), got = res if ride is not None else (res, [])
    comm.done("mlp0_bwd", got)
    big["down0"] = _matmul_tn(a0, dy_mlp0, 1, "d_down0")[0]
    big["up0"] = _matmul_tn(h_mlp0, du0, N_DEV, "d_up0", tk=2048)
    dy_mix0, dg_post0, do_gdn, dgate, d_out_gain = _out_proj_bwd(
        dx1, y_mix0, w["gdn_out"], row(g["mix_post"], 0), (o_gdn, proj, g["out_gain"]), "gdn_out_proj_bwd", tm=512)
    big["gdn_out"] = _matmul_tn(a_gdn, dy_mix0, 1, "d_gdn_out")[0]
    (dqn, dkn, dvv, dbg), got = _gdn_scan_bwd(qn, kn, v, bg, saved, inverses, do_gdn, "gdn_scan_bwd",
                                              ride=comm.ride("scan_bwd", big))
    comm.done("scan_bwd", got)
    dproj, d_conv, d_gates = _gdn_pre_bwd(proj, w["conv"], g["alog"], g["dtb"], dqn, dkn, dvv, dbg, dgate, "gdn_pre_bwd")
    big["gdn_in"] = _matmul_tn(h0, dproj, 1, "d_gdn_in", tb=1408, tk=512)[0]
    ride = comm.ride("in_bwd", big)
    grad_x, dg_pre0, *got = _norm_matmul_bwd([(dproj, w["gdn_in"], row(g["mix_pre"], 0), 1408)], x, dx1, "gdn_in_proj_bwd",
                                             ride=ride)
    comm.done("in_bwd", got[0] if got else [])

    small = {"pre0": dg_pre0, "x2": dg_x2, "post0": dg_post0, "post1": dg_post1, "mlp0": dg_mlp0, "mlp1": dg_mlp1,
             "gates": d_gates, "out_gain": d_out_gain, "conv": d_conv, "loss": loss}
    return grad_x, big, small


class _LocalOnly:
    def __init__(self, weights):
        self.w = weights

    def ride(self, stage, grads=None):
        return None

    def done(self, stage, outs):
        pass


GATHER_STAGES = {"scan": ("gdn_out", "up0", "down0"), "mlp0": ("kv", "sb_q"), "sb": ("sb_o", "up1", "down1")}
SCATTER_STAGES = {"sb_bwd": ("up1", "down1", "sb_o"), "mlp0_bwd": ("sb_q", "kv"), "scan_bwd": ("up0", "down0", "gdn_out"),
                  "in_bwd": ("gdn_in",)}


def _whole_weight(name, gathered):
    d = D_MODEL
    if name in ("up0", "up1"):
        return gathered
    if name == "kv":
        return gathered.reshape(2, N_DEV // 2, d, gathered.shape[2]).transpose(0, 2, 1, 3).reshape(2, d, d)
    if name == "gdn_in":
        whole = gathered.transpose(1, 0, 2).reshape(d, GDN_IN_COLS)
        return jnp.pad(whole, ((0, 0), (0, GDN_IN_PAD - GDN_IN_COLS)))[None]
    if name == "conv":
        return gathered.transpose(1, 0, 2).reshape(CONV_WIDTH, QKV)
    whole = gathered.reshape(gathered.shape[0] * gathered.shape[1], d)
    return whole[None] if name == "sb_q" else whole


def _owner_blocks(name, grad):
    if name in ("up0", "up1", "kv"):
        blocks = grad
    elif name == "gdn_in":
        blocks = grad[:, :GDN_IN_COLS].reshape(D_MODEL, N_DEV, GDN_IN_COLS // N_DEV).transpose(1, 0, 2)
    else:
        blocks = grad.reshape(N_DEV, grad.shape[0] // N_DEV, grad.shape[1])
    return blocks.astype(BF16)


class _Fsdp:
    def __init__(self, shards, weights):
        self.shards, self.w, self.landed = shards, weights, {}

    def ride(self, stage, grads=None):
        if stage in GATHER_STAGES:
            names = GATHER_STAGES[stage]
            return _Exchange([self.shards[nm] for nm in names], [True] * len(names))
        names = SCATTER_STAGES[stage]
        return _Exchange([_owner_blocks(nm, grads[nm]) for nm in names], [False] * len(names))

    def done(self, stage, outs):
        if stage in GATHER_STAGES:
            for nm, out in zip(GATHER_STAGES[stage], outs):
                self.w[nm] = _whole_weight(nm, out)
        else:
            self.landed.update(zip(SCATTER_STAGES[stage], outs))


def _my_place():
    return lax.axis_index("x"), lax.axis_index("y"), lax.axis_index("c")


class _Exchange:
    def __init__(self, arrays, gather):
        self.arrays, self.gather, self.n = list(arrays), list(gather), len(arrays)
        any_spec = pl.BlockSpec(memory_space=pl.ANY)
        self.in_specs = [any_spec] * self.n
        self.out_specs = [any_spec] * self.n
        self.out_shape = [jax.ShapeDtypeStruct(((N_DEV,) + a.shape) if g else a.shape, a.dtype)
                          for a, g in zip(self.arrays, self.gather)]
        self.scratch = [pltpu.SemaphoreType.DMA((self.n, N_DEV - 1)), pltpu.SemaphoreType.DMA((self.n, N_DEV - 1)),
                        pltpu.SemaphoreType.DMA((self.n,))]

    def _copies(self, ins, outs, sems):
        send_sems, recv_sems, local_sems = sems
        x, y, c = _my_place()
        me = 4 * x + 2 * y + c
        copies = []
        for a in range(self.n):
            src = ins[a] if self.gather[a] else ins[a].at[me]
            copies.append(pltpu.make_async_copy(src, outs[a].at[me], local_sems.at[a]))
        for k in range(1, N_DEV):
            px = 1 - x if k & 4 else x
            py = 1 - y if k & 2 else y
            pc = 1 - c if k & 1 else c
            peer = 4 * px + 2 * py + pc
            for a in range(self.n):
                src = ins[a] if self.gather[a] else ins[a].at[peer]
                copies.append(pltpu.make_async_remote_copy(
                    src_ref=src, dst_ref=outs[a].at[me], send_sem=send_sems.at[a, k - 1], recv_sem=recv_sems.at[a, k - 1],
                    device_id=(px, py, pc), device_id_type=MESH))
        return copies

    def start(self, ins, outs, sems):
        for cp in self._copies(ins, outs, sems):
            cp.start()

    def wait(self, ins, outs, sems):
        for cp in self._copies(ins, outs, sems):
            cp.wait()


def _call(body, operands, *, name, out_shape, in_specs, out_specs, grid=(), scratch_shapes=(), compiler_params=None,
          ride=None, first=None, last=None):
    n_in, n_out, n_scr = len(operands), len(out_shape), len(scratch_shapes)
    if ride is None:
        outs = pl.pallas_call(body, name=name, grid=grid, in_specs=in_specs, out_specs=out_specs, out_shape=out_shape,
                              scratch_shapes=scratch_shapes, compiler_params=compiler_params)(*operands)
        return list(outs), []
    r = ride.n

    def riding(*refs):
        ins, r_ins = refs[:n_in], refs[n_in:n_in + r]
        outs, r_outs = refs[n_in + r:n_in + r + n_out], refs[n_in + r + n_out:n_in + 2 * r + n_out]
        scr, sems = refs[n_in + 2 * r + n_out:n_in + 2 * r + n_out + n_scr], refs[n_in + 2 * r + n_out + n_scr:]

        @pl.when(first())
        def _():
            ride.start(r_ins, r_outs, sems)

        body(*ins, *outs, *scr)

        @pl.when(last())
        def _():
            ride.wait(r_ins, r_outs, sems)

    outs = pl.pallas_call(
        riding, name=name, grid=grid, in_specs=list(in_specs) + ride.in_specs, out_specs=list(out_specs) + ride.out_specs,
        out_shape=list(out_shape) + ride.out_shape, scratch_shapes=list(scratch_shapes) + ride.scratch,
        compiler_params=compiler_params)(*operands, *ride.arrays)
    return list(outs[:n_out]), list(outs[n_out:])


def _exchange(arrays, gather, name):
    ride = _Exchange(arrays, gather)

    def body(*refs):
        ins, outs, sems = refs[:ride.n], refs[ride.n:2 * ride.n], refs[2 * ride.n:]
        ride.start(ins, outs, sems)
        ride.wait(ins, outs, sems)

    return pl.pallas_call(body, name=name, in_specs=ride.in_specs, out_specs=ride.out_specs, out_shape=ride.out_shape,
                          scratch_shapes=ride.scratch)(*arrays)


def _gather_two_level(arrays, name):
    n_arr = len(arrays)

    def body(*refs):
        ins, outs = refs[:n_arr], refs[n_arr:2 * n_arr]
        send_sems, recv_sems, local_sems = refs[2 * n_arr:]
        x, y, c = _my_place()
        sibling = (x, y, 1 - c)
        chips = [(1 - x, y), (x, 1 - y), (1 - x, 1 - y)]
        index = lambda px, py, pc: 4 * px + 2 * py + pc

        def copy(a, k, block, to, src=None):
            dst = outs[a].at[index(*block)]
            return pltpu.make_async_remote_copy(src_ref=dst if src is None else src, dst_ref=dst,
                                                send_sem=send_sems.at[a, k], recv_sem=recv_sems.at[a, k],
                                                device_id=to, device_id_type=MESH)

        mine = [pltpu.make_async_copy(ins[a], outs[a].at[index(x, y, c)], local_sems.at[a]) for a in range(n_arr)]
        first = [copy(a, 0, (x, y, c), sibling, src=ins[a]) for a in range(n_arr)]
        first += [copy(a, 1 + j, (x, y, c), (*chip, c), src=ins[a]) for j, chip in enumerate(chips) for a in range(n_arr)]
        for cp in mine + first:
            cp.start()
        passed = []
        for j, chip in enumerate(chips):
            for a in range(n_arr):
                copy(a, 1 + j, (*chip, c), (x, y, c)).wait_recv()
                passed.append(copy(a, 4 + j, (*chip, c), sibling))
                passed[-1].start()
        for a in range(n_arr):
            copy(a, 0, sibling, (x, y, c)).wait_recv()
            for j, chip in enumerate(chips):
                copy(a, 4 + j, (*chip, 1 - c), (x, y, c)).wait_recv()
        for cp in first + passed:
            cp.wait_send()
        for cp in mine:
            cp.wait()

    any_spec = pl.BlockSpec(memory_space=pl.ANY)
    return pl.pallas_call(
        body, name=name, in_specs=[any_spec] * n_arr, out_specs=[any_spec] * n_arr,
        out_shape=[jax.ShapeDtypeStruct((N_DEV,) + a.shape, a.dtype) for a in arrays],
        scratch_shapes=[pltpu.SemaphoreType.DMA((n_arr, N_DEV - 1)), pltpu.SemaphoreType.DMA((n_arr, N_DEV - 1)),
                        pltpu.SemaphoreType.DMA((n_arr,))])(*arrays)


def _adamw_math(g, w, m, v):
    m = ADAM_B1 * m + (1.0 - ADAM_B1) * g
    v = ADAM_B2 * v + (1.0 - ADAM_B2) * jnp.square(g)
    m_hat = m / (1.0 - ADAM_B1 ** ADAM_STEP)
    v_hat = v / (1.0 - ADAM_B2 ** ADAM_STEP)
    delta = -ADAM_LR * (m_hat / (jnp.sqrt(v_hat) + ADAM_EPS) + ADAM_WD * w)
    return delta, m, v


def _sum_devices(ref):
    total = ref[0].astype(F32)
    for d in range(1, N_DEV):
        total = total + ref[d].astype(F32)
    return total


def _reduce_adamw(landed, w, m, v, name, tr=256):
    r, c = w.shape
    tr = min(tr, r)
    assert r % tr == 0

    def body(l_ref, w_ref, m_ref, v_ref, g_ref, d_ref, nm_ref, nv_ref):
        g = _sum_devices(l_ref)
        g_ref[...] = g
        d_ref[...], nm_ref[...], nv_ref[...] = _adamw_math(g, w_ref[...], m_ref[...], v_ref[...])

    blk = pl.BlockSpec((tr, c), lambda i: (i, 0))
    return pl.pallas_call(
        body, name=name, grid=(r // tr,),
        in_specs=[pl.BlockSpec((N_DEV, tr, c), lambda i: (0, i, 0)), blk, blk, blk],
        out_specs=[blk] * 4, out_shape=[jax.ShapeDtypeStruct((r, c), F32)] * 4,
        compiler_params=_params("parallel"),
    )(landed, w, m, v)


def _reduce_adamw_layers(landed, w, m, v, name, tr=256):
    _, r, c = w.shape
    tr = min(tr, r)
    nr = r // tr
    assert r % tr == 0

    def body(l0_ref, l1_ref, w_ref, m_ref, v_ref, g_ref, d_ref, nm_ref, nv_ref):
        g = jnp.where(pl.program_id(0) == 0, _sum_devices(l0_ref), _sum_devices(l1_ref))
        g_ref[...] = g
        d_ref[...], nm_ref[...], nv_ref[...] = _adamw_math(g, w_ref[...], m_ref[...], v_ref[...])

    blk = pl.BlockSpec((None, tr, c), lambda l, i: (l, i, 0))
    return pl.pallas_call(
        body, name=name, grid=(2, nr),
        in_specs=[pl.BlockSpec((N_DEV, tr, c), lambda l, i: (0, jnp.where(l == 0, i, nr - 1), 0)),
                  pl.BlockSpec((N_DEV, tr, c), lambda l, i: (0, jnp.where(l == 1, i, 0), 0)), blk, blk, blk],
        out_specs=[blk] * 4, out_shape=[jax.ShapeDtypeStruct(w.shape, F32)] * 4,
        compiler_params=_params("arbitrary", "arbitrary"),
    )(landed[0], landed[1], w, m, v)


def _adamw(g, w, m, v, name):
    def body(g_ref, w_ref, m_ref, v_ref, d_ref, nm_ref, nv_ref):
        d_ref[...], nm_ref[...], nv_ref[...] = _adamw_math(g_ref[...], w_ref[...], m_ref[...], v_ref[...])

    return pl.pallas_call(body, name=name, out_shape=[jax.ShapeDtypeStruct(w.shape, F32)] * 3)(g, w, m, v)


def _small_update(landed, params, name):
    layout = {
        "mix_pre": [("pre0", 0), ("x2", 0)], "mix_post": [("post0", 0), ("post1", 0)],
        "mlp_pre": [("mlp0", 0), ("mlp1", 0)], "mlp_post": [("mlp0", 1), ("mlp1", 1)],
        "kv": [("kv", 0)], "alog": [("gates", 0)], "dtb": [("gates", 1)], "out_gain": [("out_gain", 0)],
    }
    landed_names = sorted(landed)
    param_names = sorted(params)
    n_l, n_p = len(landed_names), len(param_names)

    def body(*refs):
        l_refs = dict(zip(landed_names, refs[:n_l]))
        p_refs = {p: refs[n_l + 3 * i:n_l + 3 * i + 3] for i, p in enumerate(param_names)}
        outs = refs[n_l + 3 * n_p:]
        o_refs = {p: outs[4 * i:4 * i + 4] for i, p in enumerate(param_names)}
        conv_ref, loss_ref = outs[4 * n_p:]
        sums = {nm: _sum_devices(l_refs[nm]) for nm in landed_names}
        sums["kv"] = sums["x2"][1:2, :] + sums["x2"][2:3, :]
        for p in param_names:
            w_ref, m_ref, v_ref = p_refs[p]
            g_ref, d_ref, nm_ref, nv_ref = o_refs[p]
            for r, (src, src_row) in enumerate(layout[p]):
                g = sums[src][src_row:src_row + 1, :]
                g_ref[r:r + 1, :] = g
                d, nm, nv = _adamw_math(g, w_ref[r:r + 1, :], m_ref[r:r + 1, :], v_ref[r:r + 1, :])
                d_ref[r:r + 1, :] = d
                nm_ref[r:r + 1, :] = nm
                nv_ref[r:r + 1, :] = nv
        conv_ref[...] = sums["conv"]
        loss_ref[...] = sums["loss"]

    args = [landed[nm] for nm in landed_names]
    out_shape = []
    for p in param_names:
        args += list(params[p])
        out_shape += [jax.ShapeDtypeStruct(params[p][0].shape, F32)] * 4
    out_shape += [jax.ShapeDtypeStruct(landed["conv"].shape[1:], F32), jax.ShapeDtypeStruct(landed["loss"].shape[1:], F32)]
    outs = pl.pallas_call(body, name=name, out_shape=out_shape)(*args)
    result = {p: tuple(outs[4 * i:4 * i + 4]) for i, p in enumerate(param_names)}
    result["conv"], result["loss"] = outs[4 * n_p], outs[4 * n_p + 1]
    return result


def _lanes(vec, offset):
    return jnp.pad(vec[None, :], ((0, 0), (offset, 128 - offset - vec.shape[0])))


def kernel(x, mix_pre_gain, mix_post_gain, mlp_pre_gain, mlp_post_gain, mlp_w_up, mlp_w_down, gdn_w_in, gdn_conv_w, gdn_a_log, gdn_dt_bias, gdn_out_gain, gdn_w_out, kv_gain, w_kv, sb_w_q, sb_w_o, loss_target, m_mix_pre_gain, m_mix_post_gain, m_mlp_pre_gain, m_mlp_post_gain, m_mlp_w_up, m_mlp_w_down, m_gdn_w_in, m_gdn_conv_w, m_gdn_a_log, m_gdn_dt_bias, m_gdn_out_gain, m_gdn_w_out, m_kv_gain, m_w_kv, m_sb_w_q, m_sb_w_o, v_mix_pre_gain, v_mix_post_gain, v_mlp_pre_gain, v_mlp_post_gain, v_mlp_w_up, v_mlp_w_down, v_gdn_w_in, v_gdn_conv_w, v_gdn_a_log, v_gdn_dt_bias, v_gdn_out_gain, v_gdn_w_out, v_kv_gain, v_w_kv, v_sb_w_q, v_sb_w_o):
    me = 4 * lax.axis_index("x") + 2 * lax.axis_index("y") + lax.axis_index("c")
    bf = lambda a: a.astype(BF16)

    shards = {"up0": bf(mlp_w_up[0]), "up1": bf(mlp_w_up[1]), "down0": bf(mlp_w_down[0]), "down1": bf(mlp_w_down[1]),
              "gdn_out": bf(gdn_w_out[0]), "kv": bf(w_kv), "sb_q": bf(sb_w_q[0]), "sb_o": bf(sb_w_o[0])}
    gdn_in, conv = _gather_two_level([bf(gdn_w_in[0]), gdn_conv_w[0]], "gather_first_weights")
    comm = _Fsdp(shards, {"gdn_in": _whole_weight("gdn_in", gdn_in), "conv": _whole_weight("conv", conv)})
    gains = {"mix_pre": mix_pre_gain, "mix_post": mix_post_gain, "mlp_pre": mlp_pre_gain, "mlp_post": mlp_post_gain,
             "kv": kv_gain[None, :], "alog": _lanes(gdn_a_log[0], N_HEADS), "dtb": _lanes(gdn_dt_bias[0], N_HEADS),
             "out_gain": gdn_out_gain}

    grad_x, big, small = _local_step(x[0], loss_target[0], gains, comm)

    small_names = ["pre0", "x2", "post0", "post1", "mlp0", "mlp1", "gates", "out_gain", "conv", "loss"]
    landed = _exchange([small[nm] for nm in small_names], [True] * len(small_names), "exchange_small_grads")
    small_landed = dict(zip(small_names, landed))

    results = {}
    results["mlp_w_up"] = _reduce_adamw_layers((comm.landed["up0"], comm.landed["up1"]), mlp_w_up, m_mlp_w_up, v_mlp_w_up,
                                               "adamw_mlp_w_up")
    results["mlp_w_down"] = _reduce_adamw_layers((comm.landed["down0"], comm.landed["down1"]), mlp_w_down, m_mlp_w_down,
                                                 v_mlp_w_down, "adamw_mlp_w_down")
    big_params = [("gdn_w_in", "gdn_in", gdn_w_in[0], m_gdn_w_in[0], v_gdn_w_in[0]),
                  ("gdn_w_out", "gdn_out", gdn_w_out[0], m_gdn_w_out[0], v_gdn_w_out[0]),
                  ("w_kv", "kv", w_kv, m_w_kv, v_w_kv), ("sb_w_q", "sb_q", sb_w_q[0], m_sb_w_q[0], v_sb_w_q[0]),
                  ("sb_w_o", "sb_o", sb_w_o[0], m_sb_w_o[0], v_sb_w_o[0])]
    for nm, short, w_, m_, v_ in big_params:
        results[nm] = _reduce_adamw(comm.landed[short], w_, m_, v_, "adamw_" + nm)
    lanes8 = lambda a: _lanes(a[0], N_HEADS)
    small_params = {
        "mix_pre": (mix_pre_gain, m_mix_pre_gain, v_mix_pre_gain), "mix_post": (mix_post_gain, m_mix_post_gain, v_mix_post_gain),
        "mlp_pre": (mlp_pre_gain, m_mlp_pre_gain, v_mlp_pre_gain), "mlp_post": (mlp_post_gain, m_mlp_post_gain, v_mlp_post_gain),
        "kv": (kv_gain[None, :], m_kv_gain[None, :], v_kv_gain[None, :]),
        "alog": (lanes8(gdn_a_log), lanes8(m_gdn_a_log), lanes8(v_gdn_a_log)),
        "dtb": (lanes8(gdn_dt_bias), lanes8(m_gdn_dt_bias), lanes8(v_gdn_dt_bias)),
        "out_gain": (gdn_out_gain, m_gdn_out_gain, v_gdn_out_gain),
    }
    sm = _small_update(small_landed, small_params, "small_update")
    conv_cols = QKV // N_DEV
    g_conv = lax.dynamic_slice(sm["conv"], (0, me * conv_cols), (8, conv_cols))[:CONV_WIDTH]
    conv_res = (g_conv,) + tuple(_adamw(g_conv, gdn_conv_w[0], m_gdn_conv_w[0], v_gdn_conv_w[0], "adamw_conv"))

    lead = lambda t: tuple(a[None] for a in t)
    heads8 = lambda t: tuple(a[:, N_HEADS:2 * N_HEADS] for a in t)
    per_weight = [
        sm["mix_pre"], sm["mix_post"], sm["mlp_pre"], sm["mlp_post"],
        tuple(results["mlp_w_up"]), tuple(results["mlp_w_down"]),
        lead(results["gdn_w_in"]), lead(conv_res), heads8(sm["alog"]), heads8(sm["dtb"]), sm["out_gain"],
        lead(results["gdn_w_out"]), tuple(a[0] for a in sm["kv"]), results["w_kv"], lead(results["sb_w_q"]), lead(results["sb_w_o"]),
    ]
    grads, deltas, new_ms, new_vs = zip(*per_weight)
    return (sm["loss"][0, 0], grad_x[None], *grads, *deltas, *new_ms, *new_vs)
```

```python
import functools
import math

import jax
import jax.numpy as jnp
from jax import lax
from jax.experimental import pallas as pl
from jax.experimental.pallas import tpu as pltpu

F32 = jnp.float32
BF16 = jnp.bfloat16

N_DEV = 8
D_MODEL = 1024
D_FF = 4096
N_HEADS = 8
HEAD_DIM = 128
CHUNK = 64
CONV_WIDTH = 4
GDN_IN_COLS = 4 * D_MODEL + 2 * N_HEADS
GDN_IN_PAD = 4 * D_MODEL + 128
EPS = 1e-6

ADAM_LR = 0.001
ADAM_B1 = 0.9
ADAM_B2 = 0.999
ADAM_EPS = 1e-08
ADAM_WD = 0.01
ADAM_STEP = 10

VMEM_LIMIT_BYTES = 56 * 1024 * 1024
MESH = pl.DeviceIdType.MESH


def _params(*semantics):
    return pltpu.CompilerParams(dimension_semantics=semantics, vmem_limit_bytes=VMEM_LIMIT_BYTES)


def _dims(ta, tb):
    return (((0,) if ta else (1,), (1,) if tb else (0,)), ((), ()))


def _dot(a, b, ta=False, tb=False):
    return lax.dot_general(a.astype(BF16), b.astype(BF16), _dims(ta, tb), preferred_element_type=F32)


def _dot_f32(a, b, ta=False, tb=False):
    return lax.dot_general(a, b, _dims(ta, tb), precision=lax.Precision.HIGHEST, preferred_element_type=F32)


def _dot_3x(a, b, ta=False, tb=False):
    return lax.dot_general(a, b, _dims(ta, tb), precision=lax.Precision.HIGH, preferred_element_type=F32)


def _make_mm(dot):
    @functools.partial(jax.custom_vjp, nondiff_argnums=(2, 3))
    def mm(a, b, ta, tb):
        return dot(a, b, ta, tb)

    def fwd(a, b, ta, tb):
        return dot(a, b, ta, tb), (a, b)

    def bwd(ta, tb, res, g):
        a, b = res
        if not ta and not tb:
            return mm(g, b, False, True), mm(a, g, True, False)
        if not ta and tb:
            return mm(g, b, False, False), mm(g, a, True, False)
        if ta and not tb:
            return mm(b, g, False, True), mm(a, g, False, False)
        raise NotImplementedError

    mm.defvjp(fwd, bwd)
    return mm


_mm = _make_mm(_dot)
_mm_f32 = _make_mm(_dot_f32)


def _rms(x, gain):
    r = lax.rsqrt(jnp.mean(x * x, axis=-1, keepdims=True) + EPS)
    return x * r * gain


def _rms_bwd(x, gain, dy):
    r = lax.rsqrt(jnp.mean(x * x, axis=-1, keepdims=True) + EPS)
    xh = x * r
    dgain = jnp.sum(dy * xh, axis=0, keepdims=True)
    dxh = dy * gain
    dx = r * (dxh - xh * jnp.mean(dxh * xh, axis=-1, keepdims=True))
    return dx, dgain


def _silu(x):
    return x / (1.0 + jnp.exp(-x))


def _acc_rows(ref, first, rows):
    @pl.when(first)
    def _():
        ref[...] = jnp.zeros_like(ref)

    for r, val in enumerate(rows):
        ref[r:r + 1, :] += val


def _norm_matmul(x, gain, w, out_dtype, name, tm=1024, tn=1024):
    n, d = x.shape
    g, _, wc = w.shape
    tm, tn = min(tm, n), min(tn, wc)
    per = wc // tn
    assert n % tm == 0 and wc % tn == 0

    def body(x_ref, gain_ref, w_ref, out_ref, h_ref):
        @pl.when(pl.program_id(1) == 0)
        def _():
            h_ref[...] = _rms(x_ref[...], gain_ref[...]).astype(BF16)

        out_ref[...] = jnp.dot(h_ref[...], w_ref[...], preferred_element_type=F32).astype(out_dtype)

    return pl.pallas_call(
        body, name=name, grid=(n // tm, g * per),
        in_specs=[pl.BlockSpec((tm, d), lambda i, j: (i, 0)),
                  pl.BlockSpec((1, d), lambda i, j: (0, 0)),
                  pl.BlockSpec((None, d, tn), lambda i, j: (j // per, 0, j % per))],
        out_specs=[pl.BlockSpec((tm, tn), lambda i, j: (i, j)),
                   pl.BlockSpec((tm, d), lambda i, j: (i, 0))],
        out_shape=[jax.ShapeDtypeStruct((n, g * wc), out_dtype), jax.ShapeDtypeStruct((n, d), BF16)],
        compiler_params=_params("parallel", "arbitrary"),
    )(x, gain, w)


def _norm_matmul_bwd(branches, x, add, name, tm=1024, ride=None):
    n, d = x.shape
    tm = min(tm, n)
    tns = [min(tn, w.shape[2]) for _, w, _, tn in branches]
    pers = [w.shape[2] // tn for (_, w, _, _), tn in zip(branches, tns)]
    ncols = [w.shape[0] * per for (_, w, _, _), per in zip(branches, pers)]
    offs = [sum(ncols[:b]) for b in range(len(branches))]
    total = sum(ncols)
    nb = len(branches)

    def body(*refs):
        dout_refs, w_refs, gain_refs = refs[:nb], refs[nb:2 * nb], refs[2 * nb:3 * nb]
        x_ref, add_ref, dx_ref, dgain_ref, acc_ref, sum_ref = refs[3 * nb:]
        i, j = pl.program_id(0), pl.program_id(1)

        @pl.when((i == 0) & (j == 0))
        def _():
            dgain_ref[...] = jnp.zeros_like(dgain_ref)

        for b in range(nb):
            first, last = offs[b], offs[b] + ncols[b] - 1

            @pl.when((j >= first) & (j <= last))
            def _(b=b, first=first):
                part = _dot(dout_refs[b][...], w_refs[b][...], tb=True)

                @pl.when(j == first)
                def _():
                    acc_ref[...] = part

                @pl.when(j > first)
                def _():
                    acc_ref[...] += part

            @pl.when(j == last)
            def _(b=b):
                xv = x_ref[...]
                xh = xv * lax.rsqrt(jnp.mean(xv * xv, axis=-1, keepdims=True) + EPS)
                dh = acc_ref[...]
                dgain_ref[b:b + 1, :] += jnp.sum(dh * xh, axis=0, keepdims=True)
                scaled = dh * gain_refs[b][...]
                if b == 0:
                    sum_ref[...] = scaled
                else:
                    sum_ref[...] += scaled

        @pl.when(j == total - 1)
        def _():
            xv = x_ref[...]
            r = lax.rsqrt(jnp.mean(xv * xv, axis=-1, keepdims=True) + EPS)
            xh = xv * r
            dxh = sum_ref[...]
            dx_ref[...] = add_ref[...] + r * (dxh - xh * jnp.mean(dxh * xh, axis=-1, keepdims=True))

    def dout_spec(b):
        return pl.BlockSpec((tm, tns[b]), lambda i, j: (i, jnp.clip(j - offs[b], 0, ncols[b] - 1)))

    def w_spec(b):
        def index(i, j):
            c = jnp.clip(j - offs[b], 0, ncols[b] - 1)
            return (c // pers[b], 0, c % pers[b])
        return pl.BlockSpec((None, d, tns[b]), index)

    row = lambda i, j: (i, 0)
    const = lambda i, j: (0, 0)
    nrow = n // tm
    (dx, dgain), got = _call(
        body, [b[0] for b in branches] + [b[1] for b in branches] + [b[2] for b in branches] + [x, add],
        name=name, grid=(nrow, total),
        in_specs=[dout_spec(b) for b in range(nb)] + [w_spec(b) for b in range(nb)]
                 + [pl.BlockSpec((1, d), const)] * nb + [pl.BlockSpec((tm, d), row), pl.BlockSpec((tm, d), row)],
        out_specs=[pl.BlockSpec((tm, d), row), pl.BlockSpec((8, d), const)],
        out_shape=[jax.ShapeDtypeStruct((n, d), F32), jax.ShapeDtypeStruct((8, d), F32)],
        scratch_shapes=[pltpu.VMEM((tm, d), F32), pltpu.VMEM((tm, d), F32)],
        compiler_params=_params("arbitrary", "arbitrary"),
        ride=ride, first=lambda: (pl.program_id(0) == 0) & (pl.program_id(1) == 0),
        last=lambda: (pl.program_id(0) == nrow - 1) & (pl.program_id(1) == total - 1))
    return (dx, dgain) if ride is None else (dx, dgain, got)


def _matmul_tn(a, b, groups, name, ta=1024, tb=1024, tk=1024):
    n, ka = a.shape
    _, kb = b.shape
    wc = kb // groups
    ta, tb, tk = min(ta, ka), min(tb, wc), min(tk, n)
    per = wc // tb
    nk = n // tk
    assert ka % ta == 0 and wc % tb == 0 and n % tk == 0

    def body(a_ref, b_ref, out_ref, acc_ref):
        k = pl.program_id(2)

        @pl.when(k == 0)
        def _():
            acc_ref[...] = jnp.zeros_like(acc_ref)

        acc_ref[...] += _dot(a_ref[...], b_ref[...], ta=True)

        @pl.when(k == nk - 1)
        def _():
            out_ref[...] = acc_ref[...].astype(BF16)

    return pl.pallas_call(
        body, name=name, grid=(ka // ta, groups * per, nk),
        in_specs=[pl.BlockSpec((tk, ta), lambda i, j, k: (k, i)),
                  pl.BlockSpec((tk, tb), lambda i, j, k: (k, j))],
        out_specs=pl.BlockSpec((None, ta, tb), lambda i, j, k: (j // per, i, j % per)),
        out_shape=jax.ShapeDtypeStruct((groups, ka, wc), BF16),
        scratch_shapes=[pltpu.VMEM((ta, tb), F32)],
        compiler_params=_params("parallel", "parallel", "arbitrary"),
    )(a, b)


def _gated_head_norm(o, gate, out_gain):
    parts = []
    for h in range(N_HEADS):
        sl = slice(h * HEAD_DIM, (h + 1) * HEAD_DIM)
        parts.append(_rms(o[:, sl], out_gain) * _silu(gate[:, sl]))
    return parts


def _out_proj(a_inputs, gated, x_in, w, gain, out_gain, name, tm=1024):
    n, d = x_in.shape
    k = w.shape[0]
    tm = min(tm, n)

    def body(*refs):
        if gated:
            o_ref, gate_ref, og_ref, x_ref, w_ref, gain_ref, xo_ref, y_ref, a_ref = refs
            parts = _gated_head_norm(o_ref[...], gate_ref[...], og_ref[...])
            for h, part in enumerate(parts):
                a_ref[:, h * HEAD_DIM:(h + 1) * HEAD_DIM] = part.astype(BF16)
            a = a_ref[...]
        else:
            a_in_ref, x_ref, w_ref, gain_ref, xo_ref, y_ref = refs
            a = a_in_ref[...]
        y = jnp.dot(a, w_ref[...], preferred_element_type=F32)
        y_ref[...] = y
        xo_ref[...] = x_ref[...] + _rms(y, gain_ref[...])

    row = lambda i: (i, 0)
    const = lambda i: (0, 0)
    if gated:
        a_specs = [pl.BlockSpec((tm, k), row), pl.BlockSpec((tm, D_MODEL), lambda i: (i, 3)),
                   pl.BlockSpec((1, HEAD_DIM), const)]
        a_args = list(a_inputs) + [out_gain]
    else:
        a_specs = [pl.BlockSpec((tm, k), row)]
        a_args = list(a_inputs)
    out_specs = [pl.BlockSpec((tm, d), row), pl.BlockSpec((tm, d), row)]
    out_shape = [jax.ShapeDtypeStruct((n, d), F32), jax.ShapeDtypeStruct((n, d), F32)]
    if gated:
        out_specs.append(pl.BlockSpec((tm, k), row))
        out_shape.append(jax.ShapeDtypeStruct((n, k), BF16))
    return pl.pallas_call(
        body, name=name, grid=(n // tm,),
        in_specs=a_specs + [pl.BlockSpec((tm, d), row), pl.BlockSpec((k, d), const), pl.BlockSpec((1, d), const)],
        out_specs=out_specs, out_shape=out_shape,
        compiler_params=_params("parallel"),
    )(*a_args, x_in, w, gain)


def _out_proj_bwd(dxo, y, w, gain, gated_inputs, name, tm=1024):
    n, d = dxo.shape
    k = w.shape[0]
    tm = min(tm, n)
    gated = gated_inputs is not None

    def body(*refs):
        if gated:
            (dxo_ref, y_ref, w_ref, gain_ref, o_ref, gate_ref, og_ref,
             dy_ref, dgain_ref, do_ref, dgate_ref, dog_ref) = refs
        else:
            dxo_ref, y_ref, w_ref, gain_ref, dy_ref, dgain_ref, da_ref = refs
        first = pl.program_id(0) == 0
        dy, dgain = _rms_bwd(y_ref[...], gain_ref[...], dxo_ref[...])
        dy_ref[...] = dy.astype(BF16)
        _acc_rows(dgain_ref, first, [dgain])
        da = _dot(dy_ref[...], w_ref[...], tb=True)
        if not gated:
            da_ref[...] = da.astype(BF16)
            return
        og = og_ref[...]
        dog = jnp.zeros_like(og)
        for h in range(N_HEADS):
            sl = slice(h * HEAD_DIM, (h + 1) * HEAD_DIM)
            fn = lambda o_h, g_h, gn: _rms(o_h, gn) * _silu(g_h)
            _, vjp = jax.vjp(fn, o_ref[:, sl], gate_ref[:, sl], og)
            do_h, dgate_h, dog_h = vjp(da[:, sl])
            do_ref[:, sl] = do_h
            dgate_ref[:, sl] = dgate_h
            dog = dog + dog_h
        _acc_rows(dog_ref, first, [dog])

    row = lambda i: (i, 0)
    const = lambda i: (0, 0)
    in_specs = [pl.BlockSpec((tm, d), row), pl.BlockSpec((tm, d), row), pl.BlockSpec((k, d), const),
                pl.BlockSpec((1, d), const)]
    args = [dxo, y, w, gain]
    out_specs = [pl.BlockSpec((tm, d), row), pl.BlockSpec((8, d), const)]
    out_shape = [jax.ShapeDtypeStruct((n, d), BF16), jax.ShapeDtypeStruct((8, d), F32)]
    if gated:
        in_specs += [pl.BlockSpec((tm, k), row), pl.BlockSpec((tm, D_MODEL), lambda i: (i, 3)),
                     pl.BlockSpec((1, HEAD_DIM), const)]
        args += list(gated_inputs)
        out_specs += [pl.BlockSpec((tm, k), row), pl.BlockSpec((tm, k), row), pl.BlockSpec((8, HEAD_DIM), const)]
        out_shape += [jax.ShapeDtypeStruct((n, k), F32), jax.ShapeDtypeStruct((n, k), F32),
                      jax.ShapeDtypeStruct((8, HEAD_DIM), F32)]
    else:
        out_specs.append(pl.BlockSpec((tm, k), row))
        out_shape.append(jax.ShapeDtypeStruct((n, k), BF16))
    return pl.pallas_call(
        body, name=name, grid=(n // tm,), in_specs=in_specs, out_specs=out_specs, out_shape=out_shape,
        compiler_params=_params("arbitrary"),
    )(*args)


def _mlp_fwd(x_in, g_pre, w_up, w_down, g_post, name, tm=1024, target=None, ride=None):
    n, d = x_in.shape
    g, _, wc = w_up.shape
    tm = min(tm, n)
    with_loss = target is not None

    def body(*refs):
        if with_loss:
            x_ref, gpre_ref, wup_ref, wdown_ref, gpost_ref, t_ref, xo_ref, y_ref, h_ref, u_ref, a_ref, loss_ref, acc_ref = refs
        else:
            x_ref, gpre_ref, wup_ref, wdown_ref, gpost_ref, xo_ref, y_ref, h_ref, u_ref, a_ref, acc_ref = refs
        i, j = pl.program_id(0), pl.program_id(1)

        @pl.when(j == 0)
        def _():
            h_ref[...] = _rms(x_ref[...], gpre_ref[...]).astype(BF16)
            acc_ref[...] = jnp.zeros_like(acc_ref)

        u = jnp.dot(h_ref[...], wup_ref[...], preferred_element_type=F32).astype(BF16)
        u_ref[...] = u
        a = jnp.square(jnp.maximum(u, 0))
        a_ref[...] = a
        acc_ref[...] += jnp.dot(a, wdown_ref[...], preferred_element_type=F32)

        @pl.when(j == g - 1)
        def _():
            y = acc_ref[...]
            y_ref[...] = y
            xo = x_ref[...] + _rms(y, gpost_ref[...])
            if with_loss:
                err = xo - t_ref[...]
                xo_ref[...] = err * (1.0 / d)
                part = 0.5 * jnp.sum(jnp.mean(err * err, axis=-1, keepdims=True), axis=0, keepdims=True)
                _acc_rows(loss_ref, i == 0, [jnp.broadcast_to(part, (1, 128))])
            else:
                xo_ref[...] = xo

    row = lambda i, j: (i, 0)
    const = lambda i, j: (0, 0)
    in_specs = [pl.BlockSpec((tm, d), row), pl.BlockSpec((1, d), const),
                pl.BlockSpec((None, d, wc), lambda i, j: (j, 0, 0)),
                pl.BlockSpec((wc, d), lambda i, j: (j, 0)), pl.BlockSpec((1, d), const)]
    out_specs = [pl.BlockSpec((tm, d), row), pl.BlockSpec((tm, d), row), pl.BlockSpec((tm, d), row),
                 pl.BlockSpec((tm, wc), lambda i, j: (i, j)), pl.BlockSpec((tm, wc), lambda i, j: (i, j))]
    out_shape = [jax.ShapeDtypeStruct((n, d), F32), jax.ShapeDtypeStruct((n, d), F32),
                 jax.ShapeDtypeStruct((n, d), BF16), jax.ShapeDtypeStruct((n, g * wc), BF16),
                 jax.ShapeDtypeStruct((n, g * wc), BF16)]
    args = [x_in, g_pre, w_up, w_down, g_post]
    if with_loss:
        in_specs.append(pl.BlockSpec((tm, d), row))
        out_specs.append(pl.BlockSpec((8, 128), const))
        out_shape.append(jax.ShapeDtypeStruct((8, 128), F32))
        args.append(target)
    nrow = n // tm
    outs, got = _call(
        body, args, name=name, grid=(nrow, g), in_specs=in_specs, out_specs=out_specs, out_shape=out_shape,
        scratch_shapes=[pltpu.VMEM((tm, d), F32)], compiler_params=_params("arbitrary", "arbitrary"),
        ride=ride, first=lambda: (pl.program_id(0) == 0) & (pl.program_id(1) == 0),
        last=lambda: (pl.program_id(0) == nrow - 1) & (pl.program_id(1) == g - 1))
    return outs if ride is None else (outs, got)


def _mlp_bwd(dxo, y, g_post, w_down, u, w_up, x_in, g_pre, name, tm=1024, ride=None):
    n, d = dxo.shape
    g, _, wc = w_up.shape
    tm = min(tm, n)

    def body(dxo_ref, y_ref, gpost_ref, wdown_ref, u_ref, wup_ref, x_ref, gpre_ref,
             dx_ref, dy_ref, du_ref, dgain_ref, acc_ref, dgpost_ref):
        i, j = pl.program_id(0), pl.program_id(1)

        @pl.when(j == 0)
        def _():
            dy, dgpost = _rms_bwd(y_ref[...], gpost_ref[...], dxo_ref[...])
            dy_ref[...] = dy.astype(BF16)
            dgpost_ref[...] = dgpost
            acc_ref[...] = jnp.zeros_like(acc_ref)

        da = _dot(dy_ref[...], wdown_ref[...], tb=True)
        du = (da * (2.0 * jnp.maximum(u_ref[...], 0).astype(F32))).astype(BF16)
        du_ref[...] = du
        acc_ref[...] += _dot(du, wup_ref[...], tb=True)

        @pl.when(j == g - 1)
        def _():
            dx, dgpre = _rms_bwd(x_ref[...], gpre_ref[...], acc_ref[...])
            dx_ref[...] = dxo_ref[...] + dx
            _acc_rows(dgain_ref, i == 0, [dgpre, dgpost_ref[...]])

    row = lambda i, j: (i, 0)
    const = lambda i, j: (0, 0)
    nrow = n // tm
    outs, got = _call(
        body, [dxo, y, g_post, w_down, u, w_up, x_in, g_pre], name=name, grid=(nrow, g),
        in_specs=[pl.BlockSpec((tm, d), row), pl.BlockSpec((tm, d), row), pl.BlockSpec((1, d), const),
                  pl.BlockSpec((wc, d), lambda i, j: (j, 0)), pl.BlockSpec((tm, wc), lambda i, j: (i, j)),
                  pl.BlockSpec((None, d, wc), lambda i, j: (j, 0, 0)), pl.BlockSpec((tm, d), row),
                  pl.BlockSpec((1, d), const)],
        out_specs=[pl.BlockSpec((tm, d), row), pl.BlockSpec((tm, d), row),
                   pl.BlockSpec((tm, wc), lambda i, j: (i, j)), pl.BlockSpec((8, d), const)],
        out_shape=[jax.ShapeDtypeStruct((n, d), F32), jax.ShapeDtypeStruct((n, d), BF16),
                   jax.ShapeDtypeStruct((n, g * wc), BF16), jax.ShapeDtypeStruct((8, d), F32)],
        scratch_shapes=[pltpu.VMEM((tm, d), F32), pltpu.VMEM((1, d), F32)],
        compiler_params=_params("arbitrary", "arbitrary"),
        ride=ride, first=lambda: (pl.program_id(0) == 0) & (pl.program_id(1) == 0),
        last=lambda: (pl.program_id(0) == nrow - 1) & (pl.program_id(1) == g - 1))
    return outs if ride is None else (outs, got)


QKV = 3 * D_MODEL


def _shifted(x, prev8, s):
    if s == 0:
        return x
    tm = x.shape[0]
    rolled = pltpu.roll(x, s, 0)
    head = pltpu.roll(prev8, s, 0)
    head = jnp.concatenate([head, jnp.zeros((tm - 8, x.shape[1]), x.dtype)], axis=0)
    rows = lax.broadcasted_iota(jnp.int32, x.shape, 0)
    return jnp.where(rows < s, head, rolled)


def _conv(x, prev8, conv_w):
    out = x * conv_w[3:4, :]
    for s in range(1, CONV_WIDTH):
        out = out + _shifted(x, prev8, s) * conv_w[3 - s:4 - s, :]
    return out


def _l2norm(x):
    return x * lax.rsqrt(jnp.sum(x * x, axis=-1, keepdims=True) + EPS)


def _gdn_act_head(cq, ck, cv):
    return _l2norm(_silu(cq)) * (HEAD_DIM ** -0.5), _l2norm(_silu(ck)), _silu(cv)


def _gdn_gates(ba, alog, dtb):
    lane = lax.broadcasted_iota(jnp.int32, ba.shape, 1)
    beta = 1.0 / (1.0 + jnp.exp(-ba))
    t = ba + dtb
    softplus = jnp.maximum(t, 0.0) + jnp.log(1.0 + jnp.exp(-jnp.abs(t)))
    g = -jnp.exp(alog) * softplus
    return jnp.where(lane < N_HEADS, beta, jnp.where(lane < 2 * N_HEADS, g, 0.0))


def _qkv_cols(part, h):
    start = part * D_MODEL + h * HEAD_DIM
    return slice(start, start + HEAD_DIM)


def _prev_rows_spec(tm, cols, colblock, order):
    per = tm // 8
    return pl.BlockSpec((8, cols), lambda i: (jnp.maximum(order(i) * per - 1, 0), colblock))


def _gdn_pre(proj, conv_w, alog, dtb, name, tm=256):
    n = proj.shape[0]

    def body(x_ref, prev_ref, ba_ref, cw_ref, alog_ref, dtb_ref, q_ref, k_ref, v_ref, bg_ref):
        first = pl.program_id(0) == 0
        for h in range(N_HEADS):
            convs = []
            for part in range(3):
                sl = _qkv_cols(part, h)
                prev8 = jnp.where(first, 0.0, prev_ref[:, sl])
                convs.append(_conv(x_ref[:, sl], prev8, cw_ref[:, sl]))
            out = slice(h * HEAD_DIM, (h + 1) * HEAD_DIM)
            q_ref[:, out], k_ref[:, out], v_ref[:, out] = _gdn_act_head(*convs)
        bg_ref[...] = _gdn_gates(ba_ref[...], alog_ref[...], dtb_ref[...])

    row = lambda i: (i, 0)
    const = lambda i: (0, 0)
    ident = lambda i: i
    return pl.pallas_call(
        body, name=name, grid=(n // tm,),
        in_specs=[pl.BlockSpec((tm, QKV), row), _prev_rows_spec(tm, QKV, 0, ident),
                  pl.BlockSpec((tm, 128), lambda i: (i, 4 * D_MODEL // 128)),
                  pl.BlockSpec((CONV_WIDTH, QKV), const), pl.BlockSpec((1, 128), const), pl.BlockSpec((1, 128), const)],
        out_specs=[pl.BlockSpec((tm, D_MODEL), row)] * 3 + [pl.BlockSpec((tm, 128), row)],
        out_shape=[jax.ShapeDtypeStruct((n, D_MODEL), F32)] * 3 + [jax.ShapeDtypeStruct((n, 128), F32)],
        compiler_params=_params("parallel"),
    )(proj, proj, proj, conv_w, alog, dtb)


def _gdn_pre_bwd(proj, conv_w, alog, dtb, dq, dk, dv, dbg, dgate, name, tm=256):
    n = proj.shape[0]
    nt = n // tm

    def body(x_ref, prev_ref, ba_ref, cw_ref, alog_ref, dtb_ref, dq_ref, dk_ref, dv_ref, dbg_ref, dgate_ref,
             dproj_ref, dcw_ref, dgates_ref, carry_ref):
        step = pl.program_id(0)
        tile = nt - 1 - step
        @pl.when(step == 0)
        def _():
            carry_ref[...] = jnp.zeros_like(carry_ref)
            dcw_ref[...] = jnp.zeros_like(dcw_ref)

        rows = lax.broadcasted_iota(jnp.int32, (tm, HEAD_DIM), 0)
        for h in range(N_HEADS):
            cols = [_qkv_cols(part, h) for part in range(3)]
            prevs = [jnp.where(tile == 0, 0.0, prev_ref[:, sl]) for sl in cols]
            convs = [_conv(x_ref[:, sl], prev8, cw_ref[:, sl]) for sl, prev8 in zip(cols, prevs)]
            _, vjp = jax.vjp(_gdn_act_head, *convs)
            out = slice(h * HEAD_DIM, (h + 1) * HEAD_DIM)
            dcs = vjp((dq_ref[:, out], dk_ref[:, out], dv_ref[:, out]))
            for sl, prev8, dc in zip(cols, prevs, dcs):
                x = x_ref[:, sl]
                cw = cw_ref[:, sl]
                dx = dc * cw[3:4, :]
                wrapped = jnp.zeros((8, HEAD_DIM), F32)
                dcw_ref[3:4, sl] += jnp.sum(dc * x, axis=0, keepdims=True)
                for s in range(1, CONV_WIDTH):
                    r = pltpu.roll(dc, tm - s, 0) * cw[3 - s:4 - s, :]
                    dx = dx + jnp.where(rows < tm - s, r, 0.0)
                    wrapped = wrapped + jnp.where(rows[tm - 8:, :] >= tm - s, r[tm - 8:, :], 0.0)
                    dcw_ref[3 - s:4 - s, sl] += jnp.sum(dc * _shifted(x, prev8, s), axis=0, keepdims=True)
                from_next = jnp.concatenate([jnp.zeros((tm - 8, HEAD_DIM), F32), carry_ref[:, sl]], axis=0)
                dproj_ref[:, sl] = (dx + from_next).astype(BF16)
                carry_ref[:, sl] = wrapped

        _, vjp = jax.vjp(_gdn_gates, ba_ref[...], alog_ref[...], dtb_ref[...])
        dba, dalog, ddtb = vjp(dbg_ref[...])
        dproj_ref[:, QKV:4 * D_MODEL] = dgate_ref[...].astype(BF16)
        dproj_ref[:, 4 * D_MODEL:] = dba.astype(BF16)
        _acc_rows(dgates_ref, step == 0, [dalog, ddtb])

    rev = lambda i: nt - 1 - i
    row = lambda i: (rev(i), 0)
    const = lambda i: (0, 0)
    return pl.pallas_call(
        body, name=name, grid=(nt,),
        in_specs=[pl.BlockSpec((tm, QKV), row), _prev_rows_spec(tm, QKV, 0, rev),
                  pl.BlockSpec((tm, 128), lambda i: (rev(i), 4 * D_MODEL // 128)),
                  pl.BlockSpec((CONV_WIDTH, QKV), const), pl.BlockSpec((1, 128), const), pl.BlockSpec((1, 128), const),
                  pl.BlockSpec((tm, D_MODEL), row), pl.BlockSpec((tm, D_MODEL), row), pl.BlockSpec((tm, D_MODEL), row),
                  pl.BlockSpec((tm, 128), row), pl.BlockSpec((tm, D_MODEL), row)],
        out_specs=[pl.BlockSpec((tm, GDN_IN_PAD), row), pl.BlockSpec((8, QKV), const), pl.BlockSpec((8, 128), const)],
        out_shape=[jax.ShapeDtypeStruct((n, GDN_IN_PAD), BF16), jax.ShapeDtypeStruct((8, QKV), F32),
                   jax.ShapeDtypeStruct((8, 128), F32)],
        scratch_shapes=[pltpu.VMEM((8, QKV), F32)],
        compiler_params=_params("arbitrary"),
    )(proj, proj, proj, conv_w, alog, dtb, dq, dk, dv, dbg, dgate)


@jax.custom_vjp
def _unit_lower_inverses(lowers):
    n = lowers[0].shape[0]
    eye = (lax.broadcasted_iota(jnp.int32, (n, n), 0) == lax.broadcasted_iota(jnp.int32, (n, n), 1)).astype(F32)
    xs = [-low for low in lowers]
    ps = [eye + x for x in xs]
    ys = [_dot_3x(x, x) for x in xs]
    for _ in range(int(math.log2(CHUNK)) - 2):
        both = [_dot_3x(y, jnp.concatenate([y, p], axis=1)) for y, p in zip(ys, ps)]
        ys = [b[:, :n] for b in both]
        ps = [p + b[:, n:] for p, b in zip(ps, both)]
    return tuple(p + _dot_3x(y, p) for y, p in zip(ys, ps))


def _unit_lower_inverses_fwd(lowers):
    ts = _unit_lower_inverses(lowers)
    return ts, ts


def _unit_lower_inverses_bwd(ts, dts):
    left = [_dot_3x(t, dt, ta=True) for t, dt in zip(ts, dts)]
    return (tuple(-_dot_3x(l, t, tb=True) for l, t in zip(left, ts)),)


_unit_lower_inverses.defvjp(_unit_lower_inverses_fwd, _unit_lower_inverses_bwd)


@jax.custom_vjp
def _known_inverses(lowers, inverses):
    return inverses


def _known_inverses_bwd(ts, dts):
    (d_lowers,) = _unit_lower_inverses_bwd(ts, dts)
    return d_lowers, tuple(jnp.zeros_like(t) for t in ts)


_known_inverses.defvjp(lambda lowers, inverses: (inverses, inverses), _known_inverses_bwd)


@functools.partial(jax.custom_vjp, nondiff_argnums=(1,))
def _halves(x, axis):
    half = x.shape[axis] // 2
    return (x[:half], x[half:]) if axis == 0 else (x[:, :half], x[:, half:])


_halves.defvjp(lambda x, axis: (_halves(x, axis), None), lambda axis, _, g: (jnp.concatenate(g, axis=axis),))

GDN_STEP_CHUNKS = 2


def _gdn_chunks(qs, ks, vs, bgs, states, inverses=None, keep_inverses=False):
    c = CHUNK
    heads = range(N_HEADS)
    items = [(j, h) for j in range(len(bgs)) for h in heads]
    row = lax.broadcasted_iota(jnp.int32, (c, c), 0)
    col = lax.broadcasted_iota(jnp.int32, (c, c), 1)
    incl, strict, eye = row >= col, row > col, row == col
    lane = lax.broadcasted_iota(jnp.int32, (c, 128), 1)
    rowc = lax.broadcasted_iota(jnp.int32, (c, 1), 0)
    gc_all = [_mm_f32(incl.astype(F32), bg, False, False) for bg in bgs]
    q, k, v = ([xs[j][h] for j, h in items] for xs in (qs, ks, vs))
    n = range(len(items))
    beta = [jnp.sum(jnp.where(lane == h, bgs[j], 0.0), axis=1, keepdims=True) for j, h in items]
    gc = [jnp.sum(jnp.where(lane == N_HEADS + h, gc_all[j], 0.0), axis=1, keepdims=True) for j, h in items]
    gc_row = [jnp.sum(jnp.where(eye, gc[i], 0.0), axis=0, keepdims=True) for i in n]
    gc_last = [jnp.sum(jnp.where(rowc == c - 1, gc[i], 0.0), axis=0, keepdims=True) for i in n]
    decay = [jnp.where(incl, jnp.exp(jnp.where(incl, gc[i] - gc_row[i], 0.0)), 0.0) for i in n]
    kb = [k[i] * beta[i] for i in n]
    kb_q = [_halves(_mm(jnp.concatenate([kb[i], q[i]], axis=0), k[i], False, True), 0) for i in n]
    lower = tuple(jnp.where(strict, kb_q[i][0] * decay[i], 0.0) for i in n)
    attn = [kb_q[i][1] * decay[i] for i in n]
    t_mat = _unit_lower_inverses(lower) if inverses is None else _known_inverses(lower, inverses)
    egc = [jnp.exp(gc[i]) for i in n]
    w_u = [_halves(_mm(t_mat[i], jnp.concatenate([kb[i] * egc[i], v[i] * beta[i]], axis=1), False, False), 1)
           for i in n]
    w, u = [x[0] for x in w_u], [x[1] for x in w_u]
    qg = [q[i] * egc[i] for i in n]
    kg = [k[i] * jnp.exp(gc_last[i] - gc[i]) for i in n]
    outs, cur = [], list(states)
    for j in range(len(bgs)):
        at = lambda h: j * N_HEADS + h
        w_qg = [_halves(_mm(jnp.concatenate([w[at(h)], qg[at(h)]], axis=0), cur[h], False, False), 0)
                for h in heads]
        v_new = [u[at(h)] - w_qg[h][0] for h in heads]
        outs.append(tuple(w_qg[h][1] + _mm(attn[at(h)], v_new[h], False, False) for h in heads))
        cur = [cur[h] * jnp.exp(gc_last[at(h)]) + _mm(kg[at(h)], v_new[h], True, False) for h in heads]
    return (tuple(outs), tuple(cur), t_mat) if keep_inverses else (tuple(outs), tuple(cur))


def _chunk_head_slices(ref, nch):
    return tuple(tuple(ref[j * CHUNK:(j + 1) * CHUNK, h * HEAD_DIM:(h + 1) * HEAD_DIM] for h in range(N_HEADS))
                 for j in range(nch))


def _store_chunk_heads(ref, values):
    for j, chunk in enumerate(values):
        for h, val in enumerate(chunk):
            ref[j * CHUNK:(j + 1) * CHUNK, h * HEAD_DIM:(h + 1) * HEAD_DIM] = val


def _gdn_scan(qn, kn, v, bg, name, ride=None):
    n = qn.shape[0]
    nch = GDN_STEP_CHUNKS
    rows = nch * CHUNK
    nc = n // rows

    def body(q_ref, k_ref, v_ref, bg_ref, o_ref, saved_ref, inv_ref, state_ref):
        @pl.when(pl.program_id(0) == 0)
        def _():
            state_ref[...] = jnp.zeros_like(state_ref)

        states = tuple(state_ref[h] for h in range(N_HEADS))
        for h in range(N_HEADS):
            saved_ref[h] = states[h]
        bgs = tuple(bg_ref[j * CHUNK:(j + 1) * CHUNK, :] for j in range(nch))
        outs, new_states, inverses = _gdn_chunks(_chunk_head_slices(q_ref, nch), _chunk_head_slices(k_ref, nch),
                                                 _chunk_head_slices(v_ref, nch), bgs, states, keep_inverses=True)
        _store_chunk_heads(o_ref, outs)
        for h in range(N_HEADS):
            state_ref[h] = new_states[h]
        for item, inverse in enumerate(inverses):
            inv_ref[item] = inverse

    row = lambda i: (i, 0)
    return _call(
        body, [qn, kn, v, bg], name=name, grid=(nc,),
        in_specs=[pl.BlockSpec((rows, D_MODEL), row)] * 3 + [pl.BlockSpec((rows, 128), row)],
        out_specs=[pl.BlockSpec((rows, D_MODEL), row),
                   pl.BlockSpec((None, N_HEADS, HEAD_DIM, HEAD_DIM), lambda i: (i, 0, 0, 0)),
                   pl.BlockSpec((None, nch * N_HEADS, CHUNK, CHUNK), lambda i: (i, 0, 0, 0))],
        out_shape=[jax.ShapeDtypeStruct((n, D_MODEL), F32),
                   jax.ShapeDtypeStruct((nc, N_HEADS, HEAD_DIM, HEAD_DIM), F32),
                   jax.ShapeDtypeStruct((nc, nch * N_HEADS, CHUNK, CHUNK), F32)],
        scratch_shapes=[pltpu.VMEM((N_HEADS, HEAD_DIM, HEAD_DIM), F32)],
        compiler_params=_params("arbitrary"),
        ride=ride, first=lambda: pl.program_id(0) == 0, last=lambda: pl.program_id(0) == nc - 1)


def _gdn_scan_bwd(qn, kn, v, bg, saved, inverses, do, name, ride=None):
    n = qn.shape[0]
    nch = GDN_STEP_CHUNKS
    rows = nch * CHUNK
    nc = n // rows

    def body(q_ref, k_ref, v_ref, bg_ref, saved_ref, inv_ref, do_ref, dq_ref, dk_ref, dv_ref, dbg_ref, dstate_ref):
        @pl.when(pl.program_id(0) == 0)
        def _():
            dstate_ref[...] = jnp.zeros_like(dstate_ref)

        states = tuple(saved_ref[h] for h in range(N_HEADS))
        bgs = tuple(bg_ref[j * CHUNK:(j + 1) * CHUNK, :] for j in range(nch))
        known = tuple(inv_ref[item] for item in range(nch * N_HEADS))
        _, vjp = jax.vjp(functools.partial(_gdn_chunks, inverses=known), _chunk_head_slices(q_ref, nch),
                         _chunk_head_slices(k_ref, nch), _chunk_head_slices(v_ref, nch), bgs, states)
        dstates = tuple(dstate_ref[h] for h in range(N_HEADS))
        dqs, dks, dvs, dbgs, dprev = vjp((_chunk_head_slices(do_ref, nch), dstates))
        _store_chunk_heads(dq_ref, dqs)
        _store_chunk_heads(dk_ref, dks)
        _store_chunk_heads(dv_ref, dvs)
        for h in range(N_HEADS):
            dstate_ref[h] = dprev[h]
        for j in range(nch):
            dbg_ref[j * CHUNK:(j + 1) * CHUNK, :] = dbgs[j]

    row = lambda i: (nc - 1 - i, 0)
    return _call(
        body, [qn, kn, v, bg, saved, inverses, do], name=name, grid=(nc,),
        in_specs=[pl.BlockSpec((rows, D_MODEL), row)] * 3 + [pl.BlockSpec((rows, 128), row),
                  pl.BlockSpec((None, N_HEADS, HEAD_DIM, HEAD_DIM), lambda i: (nc - 1 - i, 0, 0, 0)),
                  pl.BlockSpec((None, nch * N_HEADS, CHUNK, CHUNK), lambda i: (nc - 1 - i, 0, 0, 0)),
                  pl.BlockSpec((rows, D_MODEL), row)],
        out_specs=[pl.BlockSpec((rows, D_MODEL), row)] * 3 + [pl.BlockSpec((rows, 128), row)],
        out_shape=[jax.ShapeDtypeStruct((n, D_MODEL), F32)] * 3 + [jax.ShapeDtypeStruct((n, 128), F32)],
        scratch_shapes=[pltpu.VMEM((N_HEADS, HEAD_DIM, HEAD_DIM), F32)],
        compiler_params=_params("arbitrary"),
        ride=ride, first=lambda: pl.program_id(0) == 0, last=lambda: pl.program_id(0) == nc - 1)


SB_BQ = 512
SB_BK = 256
SB_SUB = 128
SB_ROWS = 128
SB_SCALE = HEAD_DIM ** -0.5
SB_DEAD = -105.0


def _sb_terms(z, before):
    e = jnp.exp(-jnp.abs(z))
    log_beta = jnp.minimum(z, 0.0) - jnp.log(1.0 + e)
    log_1m = log_beta - z
    if before is not None:
        log_1m = jnp.where(before, log_1m, 0.0)
    return e, log_beta, log_1m


def _tri_ones(n, cmp):
    r = lax.broadcasted_iota(jnp.int32, (n, 2 * n), 0)
    c = lax.broadcasted_iota(jnp.int32, (n, 2 * n), 1)
    return jnp.where((c >= n) | cmp(r, c), 1.0, 0.0).astype(BF16)


def _sums(x, tri_ones):
    both = jnp.dot(x.astype(BF16), tri_ones, preferred_element_type=F32)
    n = x.shape[1]
    return both[:, :n], both[:, n:]


def _sb_chunk_mask(diagonal, r, s):
    if not diagonal or s * SB_SUB + SB_SUB - 1 < r * SB_ROWS:
        return None
    if s * SB_SUB >= r * SB_ROWS + SB_ROWS - 1:
        return "empty"
    rows = r * SB_ROWS + lax.broadcasted_iota(jnp.int32, (SB_ROWS, SB_SUB), 0)
    cols = s * SB_SUB + lax.broadcasted_iota(jnp.int32, (SB_ROWS, SB_SUB), 1)
    return cols < rows


def _sb_attention(q, kv, name, bq=SB_BQ, ride=None):
    n = q.shape[0]
    bq = min(bq, n)
    bk = min(SB_BK, bq)
    nrc = bq // SB_ROWS

    def body(q_ref, k_ref, v_ref, o_ref, l_ref, n_ref, z_scr, a_scr):
        i = pl.program_id(1)
        qb = q_ref[...]
        after = _tri_ones(SB_SUB, lambda r, c: r > c)

        def span(start, kw, c_sum, acc, diagonal):
            start = pl.multiple_of(start, kw)
            k_w = k_ref[pl.ds(start, kw), :]
            v_w = v_ref[pl.ds(start, kw), :]
            z_scr[:, :kw] = lax.dot_general(qb, k_w, _dims(False, True), preferred_element_type=F32)
            c_rows = [c_sum[r * SB_ROWS:(r + 1) * SB_ROWS] for r in range(nrc)]
            for s in reversed(range(kw // SB_SUB)):
                cols = slice(s * SB_SUB, (s + 1) * SB_SUB)
                for r in range(nrc):
                    rows = slice(r * SB_ROWS, (r + 1) * SB_ROWS)
                    before = _sb_chunk_mask(diagonal, r, s)
                    if isinstance(before, str):
                        a_scr[rows, cols] = jnp.zeros((SB_ROWS, SB_SUB), BF16)
                        continue
                    _, log_beta, log_1m = _sb_terms(z_scr[rows, cols] * SB_SCALE, before)
                    tail, total = _sums(log_1m, after)
                    a = jnp.exp(log_beta + c_rows[r] + tail)
                    if before is not None:
                        a = jnp.where(before, a, 0.0)
                    a_scr[rows, cols] = a.astype(BF16)
                    c_rows[r] = c_rows[r] + total
            acc = acc + jnp.dot(a_scr[:, :kw], v_w, preferred_element_type=F32)
            return jnp.concatenate(c_rows, axis=0), acc

        c_sum, acc = span(i * bq, bq, jnp.zeros((bq, SB_SUB), F32), jnp.zeros((bq, HEAD_DIM), F32), True)

        def more(state):
            done, c, _ = state
            return (done < i * (bq // bk)) & (jnp.max(c) > SB_DEAD)

        def step(state):
            done, c, a = state
            c, a = span(i * bq - (done + 1) * bk, bk, c, a, False)
            return done + 1, c, a

        done, c_sum, acc = lax.while_loop(more, step, (jnp.int32(0), c_sum, acc))
        o_ref[...] = acc.astype(BF16)
        l_ref[...] = c_sum
        n_ref[...] = jnp.full((8, 128), done.astype(F32), F32)

    nq = n // bq
    return _call(
        body, [q, kv, kv], name=name, grid=(N_HEADS, nq),
        in_specs=[pl.BlockSpec((bq, HEAD_DIM), lambda h, i: (i, h)),
                  pl.BlockSpec((n, HEAD_DIM), lambda h, i: (0, h)),
                  pl.BlockSpec((n, HEAD_DIM), lambda h, i: (0, N_HEADS + h))],
        out_specs=[pl.BlockSpec((bq, HEAD_DIM), lambda h, i: (i, h)),
                   pl.BlockSpec((None, bq, 128), lambda h, i: (h, i, 0)),
                   pl.BlockSpec((None, None, 8, 128), lambda h, i: (h, i, 0, 0))],
        out_shape=[jax.ShapeDtypeStruct((n, D_MODEL), BF16), jax.ShapeDtypeStruct((N_HEADS, n, 128), F32),
                   jax.ShapeDtypeStruct((N_HEADS, nq, 8, 128), F32)],
        scratch_shapes=[pltpu.VMEM((bq, bq), F32), pltpu.VMEM((bq, bq), BF16)],
        compiler_params=_params("arbitrary", "arbitrary"),
        ride=ride, first=lambda: (pl.program_id(0) == 0) & (pl.program_id(1) == 0),
        last=lambda: (pl.program_id(0) == N_HEADS - 1) & (pl.program_id(1) == nq - 1))


def _sb_attention_bwd(q, kv, do, lsum, spans, name, bq=SB_BQ, ride=None):
    n = q.shape[0]
    bq = min(bq, n)
    bk = min(SB_BK, bq)
    nrc = bq // SB_ROWS

    def body(q_ref, k_ref, v_ref, do_ref, l_ref, n_ref, dq_ref, dk_out, dv_out, z_scr, da_scr, a_scr, dz_scr, dk_ref, dv_ref):
        i = pl.program_id(1)

        @pl.when(i == 0)
        def _():
            dk_ref[...] = jnp.zeros_like(dk_ref)
            dv_ref[...] = jnp.zeros_like(dv_ref)

        qb = q_ref[...]
        dob = do_ref[...]
        lt_rows = [l_ref[r * SB_ROWS:(r + 1) * SB_ROWS, :] for r in range(nrc)]
        upto = _tri_ones(SB_SUB, lambda r, c: r <= c)
        below = _tri_ones(SB_SUB, lambda r, c: r < c)

        def span(start, kw, l_sum, g_sum, dq, diagonal):
            start = pl.multiple_of(start, kw)
            k_w = k_ref[pl.ds(start, kw), :]
            v_w = v_ref[pl.ds(start, kw), :]
            z_scr[:, :kw] = lax.dot_general(qb, k_w, _dims(False, True), preferred_element_type=F32)
            da_scr[:, :kw] = lax.dot_general(dob, v_w, _dims(False, True), preferred_element_type=F32)
            l_rows = [l_sum[r * SB_ROWS:(r + 1) * SB_ROWS] for r in range(nrc)]
            g_rows = [g_sum[r * SB_ROWS:(r + 1) * SB_ROWS] for r in range(nrc)]
            for s in range(kw // SB_SUB):
                cols = slice(s * SB_SUB, (s + 1) * SB_SUB)
                for r in range(nrc):
                    rows = slice(r * SB_ROWS, (r + 1) * SB_ROWS)
                    before = _sb_chunk_mask(diagonal, r, s)
                    if isinstance(before, str):
                        a_scr[rows, cols] = jnp.zeros((SB_ROWS, SB_SUB), BF16)
                        dz_scr[rows, cols] = jnp.zeros((SB_ROWS, SB_SUB), BF16)
                        continue
                    zs = z_scr[rows, cols] * SB_SCALE
                    e, log_beta, log_1m = _sb_terms(zs, before)
                    l_prefix, l_total = _sums(log_1m, upto)
                    a = jnp.exp(log_beta + (lt_rows[r] - (l_rows[r] + l_prefix)))
                    if before is not None:
                        a = jnp.where(before, a, 0.0)
                    g = a * da_scr[rows, cols]
                    g_prefix, g_total = _sums(g, below)
                    inv = 1.0 / (1.0 + e)
                    beta = jnp.where(zs >= 0, inv, e * inv)
                    dz = (g - (g + g_rows[r] + g_prefix) * beta) * SB_SCALE
                    if before is not None:
                        dz = jnp.where(before, dz, 0.0)
                    a_scr[rows, cols] = a.astype(BF16)
                    dz_scr[rows, cols] = dz.astype(BF16)
                    l_rows[r] = l_rows[r] + l_total
                    g_rows[r] = g_rows[r] + g_total
            dz_w = dz_scr[:, :kw]
            dq = dq + jnp.dot(dz_w, k_w, preferred_element_type=F32)
            dk_ref[pl.ds(start, kw), :] += lax.dot_general(dz_w, qb, _dims(True, False), preferred_element_type=F32)
            dv_ref[pl.ds(start, kw), :] += lax.dot_general(a_scr[:, :kw], dob, _dims(True, False),
                                                           preferred_element_type=F32)
            return jnp.concatenate(l_rows, axis=0), jnp.concatenate(g_rows, axis=0), dq

        taken = jnp.clip(jnp.max(n_ref[...]).astype(jnp.int32), 0, i * (bq // bk))
        zero = jnp.zeros((bq, SB_SUB), F32)
        carry = lax.fori_loop(0, taken, lambda j, c: span(i * bq - (taken - j) * bk, bk, c[0], c[1], c[2], False),
                              (zero, zero, jnp.zeros((bq, HEAD_DIM), F32)))
        _, _, dq = span(i * bq, bq, carry[0], carry[1], carry[2], True)
        dq_ref[...] = dq.astype(BF16)

        @pl.when(i == nq - 1)
        def _():
            dk_out[...] = dk_ref[...].astype(BF16)
            dv_out[...] = dv_ref[...].astype(BF16)

    nq = n // bq
    return _call(
        body, [q, kv, kv, do, lsum, spans], name=name, grid=(N_HEADS, nq),
        in_specs=[pl.BlockSpec((bq, HEAD_DIM), lambda h, i: (i, h)),
                  pl.BlockSpec((n, HEAD_DIM), lambda h, i: (0, h)),
                  pl.BlockSpec((n, HEAD_DIM), lambda h, i: (0, N_HEADS + h)),
                  pl.BlockSpec((bq, HEAD_DIM), lambda h, i: (i, h)),
                  pl.BlockSpec((None, bq, 128), lambda h, i: (h, i, 0)),
                  pl.BlockSpec((None, None, 8, 128), lambda h, i: (h, i, 0, 0))],
        out_specs=[pl.BlockSpec((bq, HEAD_DIM), lambda h, i: (i, h)),
                   pl.BlockSpec((n, HEAD_DIM), lambda h, i: (0, h)),
                   pl.BlockSpec((n, HEAD_DIM), lambda h, i: (0, h))],
        out_shape=[jax.ShapeDtypeStruct((n, D_MODEL), BF16)] * 3,
        scratch_shapes=[pltpu.VMEM((bq, bq), F32), pltpu.VMEM((bq, bq), F32), pltpu.VMEM((bq, bq), BF16),
                        pltpu.VMEM((bq, bq), BF16), pltpu.VMEM((n, HEAD_DIM), F32), pltpu.VMEM((n, HEAD_DIM), F32)],
        compiler_params=_params("arbitrary", "arbitrary"),
        ride=ride, first=lambda: (pl.program_id(0) == 0) & (pl.program_id(1) == 0),
        last=lambda: (pl.program_id(0) == N_HEADS - 1) & (pl.program_id(1) == nq - 1))


def _local_step(x, target, gains, comm):
    g = gains
    w = comm.w
    big = {}
    row = lambda a, i: a[i:i + 1, :]

    proj, h0 = _norm_matmul(x, row(g["mix_pre"], 0), w["gdn_in"], F32, "gdn_in_proj", tn=1408)
    qn, kn, v, bg = _gdn_pre(proj, w["conv"], g["alog"], g["dtb"], "gdn_pre")
    (o_gdn, saved, inverses), got = _gdn_scan(qn, kn, v, bg, "gdn_scan", ride=comm.ride("scan"))
    comm.done("scan", got)
    x1, y_mix0, a_gdn = _out_proj((o_gdn, proj), True, x, w["gdn_out"], row(g["mix_post"], 0), g["out_gain"],
                                  "gdn_out_proj", tm=512)
    ride = comm.ride("mlp0")
    res = _mlp_fwd(x1, row(g["mlp_pre"], 0), w["up0"], w["down0"], row(g["mlp_post"], 0), "mlp0", ride=ride)
    (x2, y_mlp0, h_mlp0, u0, a0), got = res if ride is not None else (res, [])
    comm.done("mlp0", got)
    kv, h_kv = _norm_matmul(x2, g["kv"], w["kv"], BF16, "kv_proj", tm=2048)
    q, h_q = _norm_matmul(x2, row(g["mix_pre"], 1), w["sb_q"], BF16, "sb_q_proj")
    (o_sb, lsum, spans), got = _sb_attention(q, kv, "sb_attention", ride=comm.ride("sb"))
    comm.done("sb", got)
    x3, y_mix1 = _out_proj((o_sb,), False, x2, w["sb_o"], row(g["mix_post"], 1), None, "sb_out_proj")
    dx4, y_mlp1, h_mlp1, u1, a1, loss = _mlp_fwd(x3, row(g["mlp_pre"], 1), w["up1"], w["down1"], row(g["mlp_post"], 1),
                                                 "mlp1_loss", target=target)

    dx3, dy_mlp1, du1, dg_mlp1 = _mlp_bwd(dx4, y_mlp1, row(g["mlp_post"], 1), w["down1"], u1, w["up1"], x3,
                                          row(g["mlp_pre"], 1), "mlp1_bwd")
    big["down1"] = _matmul_tn(a1, dy_mlp1, 1, "d_down1")[0]
    big["up1"] = _matmul_tn(h_mlp1, du1, N_DEV, "d_up1", tk=2048)
    dy_mix1, dg_post1, do_sb = _out_proj_bwd(dx3, y_mix1, w["sb_o"], row(g["mix_post"], 1), None, "sb_out_proj_bwd")
    big["sb_o"] = _matmul_tn(o_sb, dy_mix1, 1, "d_sb_o")[0]
    (dq, dk, dv), got = _sb_attention_bwd(q, kv, do_sb, lsum, spans, "sb_attention_bwd", ride=comm.ride("sb_bwd", big))
    comm.done("sb_bwd", got)
    big["sb_q"] = _matmul_tn(h_q, dq, 1, "d_sb_q")[0]
    big["kv"] = jnp.concatenate([_matmul_tn(h_kv, dk, N_DEV // 2, "d_w_k", tk=2048),
                                 _matmul_tn(h_kv, dv, N_DEV // 2, "d_w_v", tk=2048)],
                                axis=0)
    dx2, dg_x2 = _norm_matmul_bwd([(dq, w["sb_q"], row(g["mix_pre"], 1), 1024), (dk, w["kv"][0:1], g["kv"], 1024),
                                   (dv, w["kv"][1:2], g["kv"], 1024)], x2, dx3, "qkv_proj_bwd", tm=512)

    ride = comm.ride("mlp0_bwd", big)
    res = _mlp_bwd(dx2, y_mlp0, row(g["mlp_post"], 0), w["down0"], u0, w["up0"], x1, row(g["mlp_pre"], 0), "mlp0_bwd",
                   ride=ride)
    (dx1, dy_mlp0, du0, dg_mlp0), got = res if ride is not None else (res, [])
    comm.done("mlp0_bwd", got)
    big["down0"] = _matmul_tn(a0, dy_mlp0, 1, "d_down0")[0]
    big["up0"] = _matmul_tn(h_mlp0, du0, N_DEV, "d_up0", tk=2048)
    dy_mix0, dg_post0, do_gdn, dgate, d_out_gain = _out_proj_bwd(
        dx1, y_mix0, w["gdn_out"], row(g["mix_post"], 0), (o_gdn, proj, g["out_gain"]), "gdn_out_proj_bwd", tm=512)
    big["gdn_out"] = _matmul_tn(a_gdn, dy_mix0, 1, "d_gdn_out")[0]
    (dqn, dkn, dvv, dbg), got = _gdn_scan_bwd(qn, kn, v, bg, saved, inverses, do_gdn, "gdn_scan_bwd",
                                              ride=comm.ride("scan_bwd", big))
    comm.done("scan_bwd", got)
    dproj, d_conv, d_gates = _gdn_pre_bwd(proj, w["conv"], g["alog"], g["dtb"], dqn, dkn, dvv, dbg, dgate, "gdn_pre_bwd")
    big["gdn_in"] = _matmul_tn(h0, dproj, 1, "d_gdn_in", tb=1408)[0]
    ride = comm.ride("in_bwd", big)
    grad_x, dg_pre0, *got = _norm_matmul_bwd([(dproj, w["gdn_in"], row(g["mix_pre"], 0), 1408)], x, dx1, "gdn_in_proj_bwd",
                                             ride=ride)
    comm.done("in_bwd", got[0] if got else [])

    small = {"pre0": dg_pre0, "x2": dg_x2, "post0": dg_post0, "post1": dg_post1, "mlp0": dg_mlp0, "mlp1": dg_mlp1,
             "gates": d_gates, "out_gain": d_out_gain, "conv": d_conv, "loss": loss}
    return grad_x, big, small


class _LocalOnly:
    def __init__(self, weights):
        self.w = weights

    def ride(self, stage, grads=None):
        return None

    def done(self, stage, outs):
        pass


GATHER_STAGES = {"scan": ("gdn_out", "up0", "down0"), "mlp0": ("kv", "sb_q"), "sb": ("sb_o", "up1", "down1")}
SCATTER_STAGES = {"sb_bwd": ("up1", "down1", "sb_o"), "mlp0_bwd": ("sb_q", "kv"), "scan_bwd": ("up0", "down0", "gdn_out"),
                  "in_bwd": ("gdn_in",)}


def _whole_weight(name, gathered):
    d = D_MODEL
    if name in ("up0", "up1"):
        return gathered
    if name == "kv":
        return gathered.reshape(2, N_DEV // 2, d, gathered.shape[2]).transpose(0, 2, 1, 3).reshape(2, d, d)
    if name == "gdn_in":
        whole = gathered.transpose(1, 0, 2).reshape(d, GDN_IN_COLS)
        return jnp.pad(whole, ((0, 0), (0, GDN_IN_PAD - GDN_IN_COLS)))[None]
    if name == "conv":
        return gathered.transpose(1, 0, 2).reshape(CONV_WIDTH, QKV)
    whole = gathered.reshape(gathered.shape[0] * gathered.shape[1], d)
    return whole[None] if name == "sb_q" else whole


def _owner_blocks(name, grad):
    if name in ("up0", "up1", "kv"):
        blocks = grad
    elif name == "gdn_in":
        blocks = grad[:, :GDN_IN_COLS].reshape(D_MODEL, N_DEV, GDN_IN_COLS // N_DEV).transpose(1, 0, 2)
    else:
        blocks = grad.reshape(N_DEV, grad.shape[0] // N_DEV, grad.shape[1])
    return blocks.astype(BF16)


class _Fsdp:
    def __init__(self, shards, weights):
        self.shards, self.w, self.landed = shards, weights, {}

    def ride(self, stage, grads=None):
        if stage in GATHER_STAGES:
            names = GATHER_STAGES[stage]
            return _Exchange([self.shards[nm] for nm in names], [True] * len(names))
        names = SCATTER_STAGES[stage]
        return _Exchange([_owner_blocks(nm, grads[nm]) for nm in names], [False] * len(names))

    def done(self, stage, outs):
        if stage in GATHER_STAGES:
            for nm, out in zip(GATHER_STAGES[stage], outs):
                self.w[nm] = _whole_weight(nm, out)
        else:
            self.landed.update(zip(SCATTER_STAGES[stage], outs))


def _my_place():
    return lax.axis_index("x"), lax.axis_index("y"), lax.axis_index("c")


class _Exchange:
    def __init__(self, arrays, gather):
        self.arrays, self.gather, self.n = list(arrays), list(gather), len(arrays)
        any_spec = pl.BlockSpec(memory_space=pl.ANY)
        self.in_specs = [any_spec] * self.n
        self.out_specs = [any_spec] * self.n
        self.out_shape = [jax.ShapeDtypeStruct(((N_DEV,) + a.shape) if g else a.shape, a.dtype)
                          for a, g in zip(self.arrays, self.gather)]
        self.scratch = [pltpu.SemaphoreType.DMA((self.n, N_DEV - 1)), pltpu.SemaphoreType.DMA((self.n, N_DEV - 1)),
                        pltpu.SemaphoreType.DMA((self.n,))]

    def _copies(self, ins, outs, sems):
        send_sems, recv_sems, local_sems = sems
        x, y, c = _my_place()
        me = 4 * x + 2 * y + c
        copies = []
        for a in range(self.n):
            src = ins[a] if self.gather[a] else ins[a].at[me]
            copies.append(pltpu.make_async_copy(src, outs[a].at[me], local_sems.at[a]))
        for k in range(1, N_DEV):
            px = 1 - x if k & 4 else x
            py = 1 - y if k & 2 else y
            pc = 1 - c if k & 1 else c
            peer = 4 * px + 2 * py + pc
            for a in range(self.n):
                src = ins[a] if self.gather[a] else ins[a].at[peer]
                copies.append(pltpu.make_async_remote_copy(
                    src_ref=src, dst_ref=outs[a].at[me], send_sem=send_sems.at[a, k - 1], recv_sem=recv_sems.at[a, k - 1],
                    device_id=(px, py, pc), device_id_type=MESH))
        return copies

    def start(self, ins, outs, sems):
        for cp in self._copies(ins, outs, sems):
            cp.start()

    def wait(self, ins, outs, sems):
        for cp in self._copies(ins, outs, sems):
            cp.wait()


def _call(body, operands, *, name, out_shape, in_specs, out_specs, grid=(), scratch_shapes=(), compiler_params=None,
          ride=None, first=None, last=None):
    n_in, n_out, n_scr = len(operands), len(out_shape), len(scratch_shapes)
    if ride is None:
        outs = pl.pallas_call(body, name=name, grid=grid, in_specs=in_specs, out_specs=out_specs, out_shape=out_shape,
                              scratch_shapes=scratch_shapes, compiler_params=compiler_params)(*operands)
        return list(outs), []
    r = ride.n

    def riding(*refs):
        ins, r_ins = refs[:n_in], refs[n_in:n_in + r]
        outs, r_outs = refs[n_in + r:n_in + r + n_out], refs[n_in + r + n_out:n_in + 2 * r + n_out]
        scr, sems = refs[n_in + 2 * r + n_out:n_in + 2 * r + n_out + n_scr], refs[n_in + 2 * r + n_out + n_scr:]

        @pl.when(first())
        def _():
            ride.start(r_ins, r_outs, sems)

        body(*ins, *outs, *scr)

        @pl.when(last())
        def _():
            ride.wait(r_ins, r_outs, sems)

    outs = pl.pallas_call(
        riding, name=name, grid=grid, in_specs=list(in_specs) + ride.in_specs, out_specs=list(out_specs) + ride.out_specs,
        out_shape=list(out_shape) + ride.out_shape, scratch_shapes=list(scratch_shapes) + ride.scratch,
        compiler_params=compiler_params)(*operands, *ride.arrays)
    return list(outs[:n_out]), list(outs[n_out:])


def _exchange(arrays, gather, name):
    ride = _Exchange(arrays, gather)

    def body(*refs):
        ins, outs, sems = refs[:ride.n], refs[ride.n:2 * ride.n], refs[2 * ride.n:]
        ride.start(ins, outs, sems)
        ride.wait(ins, outs, sems)

    return pl.pallas_call(body, name=name, in_specs=ride.in_specs, out_specs=ride.out_specs, out_shape=ride.out_shape,
                          scratch_shapes=ride.scratch)(*arrays)


def _gather_two_level(arrays, name):
    n_arr = len(arrays)

    def body(*refs):
        ins, outs = refs[:n_arr], refs[n_arr:2 * n_arr]
        send_sems, recv_sems, local_sems = refs[2 * n_arr:]
        x, y, c = _my_place()
        sibling = (x, y, 1 - c)
        chips = [(1 - x, y), (x, 1 - y), (1 - x, 1 - y)]
        index = lambda px, py, pc: 4 * px + 2 * py + pc

        def copy(a, k, block, to, src=None):
            dst = outs[a].at[index(*block)]
            return pltpu.make_async_remote_copy(src_ref=dst if src is None else src, dst_ref=dst,
                                                send_sem=send_sems.at[a, k], recv_sem=recv_sems.at[a, k],
                                                device_id=to, device_id_type=MESH)

        mine = [pltpu.make_async_copy(ins[a], outs[a].at[index(x, y, c)], local_sems.at[a]) for a in range(n_arr)]
        first = [copy(a, 0, (x, y, c), sibling, src=ins[a]) for a in range(n_arr)]
        first += [copy(a, 1 + j, (x, y, c), (*chip, c), src=ins[a]) for j, chip in enumerate(chips) for a in range(n_arr)]
        for cp in mine + first:
            cp.start()
        passed = []
        for j, chip in enumerate(chips):
            for a in range(n_arr):
                copy(a, 1 + j, (*chip, c), (x, y, c)).wait_recv()
                passed.append(copy(a, 4 + j, (*chip, c), sibling))
                passed[-1].start()
        for a in range(n_arr):
            copy(a, 0, sibling, (x, y, c)).wait_recv()
            for j, chip in enumerate(chips):
                copy(a, 4 + j, (*chip, 1 - c), (x, y, c)).wait_recv()
        for cp in first + passed:
            cp.wait_send()
        for cp in mine:
            cp.wait()

    any_spec = pl.BlockSpec(memory_space=pl.ANY)
    return pl.pallas_call(
        body, name=name, in_specs=[any_spec] * n_arr, out_specs=[any_spec] * n_arr,
        out_shape=[jax.ShapeDtypeStruct((N_DEV,) + a.shape, a.dtype) for a in arrays],
        scratch_shapes=[pltpu.SemaphoreType.DMA((n_arr, N_DEV - 1)), pltpu.SemaphoreType.DMA((n_arr, N_DEV - 1)),
                        pltpu.SemaphoreType.DMA((n_arr,))])(*arrays)


def _adamw_math(g, w, m, v):
    m = ADAM_B1 * m + (1.0 - ADAM_B1) * g
    v = ADAM_B2 * v + (1.0 - ADAM_B2) * jnp.square(g)
    m_hat = m / (1.0 - ADAM_B1 ** ADAM_STEP)
    v_hat = v / (1.0 - ADAM_B2 ** ADAM_STEP)
    delta = -ADAM_LR * (m_hat / (jnp.sqrt(v_hat) + ADAM_EPS) + ADAM_WD * w)
    return delta, m, v


def _sum_devices(ref):
    total = ref[0].astype(F32)
    for d in range(1, N_DEV):
        total = total + ref[d].astype(F32)
    return total


def _reduce_adamw(landed, w, m, v, name, tr=256):
    r, c = w.shape
    tr = min(tr, r)
    assert r % tr == 0

    def body(l_ref, w_ref, m_ref, v_ref, g_ref, d_ref, nm_ref, nv_ref):
        g = _sum_devices(l_ref)
        g_ref[...] = g
        d_ref[...], nm_ref[...], nv_ref[...] = _adamw_math(g, w_ref[...], m_ref[...], v_ref[...])

    blk = pl.BlockSpec((tr, c), lambda i: (i, 0))
    return pl.pallas_call(
        body, name=name, grid=(r // tr,),
        in_specs=[pl.BlockSpec((N_DEV, tr, c), lambda i: (0, i, 0)), blk, blk, blk],
        out_specs=[blk] * 4, out_shape=[jax.ShapeDtypeStruct((r, c), F32)] * 4,
        compiler_params=_params("parallel"),
    )(landed, w, m, v)


def _reduce_adamw_layers(landed, w, m, v, name, tr=256):
    _, r, c = w.shape
    tr = min(tr, r)
    nr = r // tr
    assert r % tr == 0

    def body(l0_ref, l1_ref, w_ref, m_ref, v_ref, g_ref, d_ref, nm_ref, nv_ref):
        g = jnp.where(pl.program_id(0) == 0, _sum_devices(l0_ref), _sum_devices(l1_ref))
        g_ref[...] = g
        d_ref[...], nm_ref[...], nv_ref[...] = _adamw_math(g, w_ref[...], m_ref[...], v_ref[...])

    blk = pl.BlockSpec((None, tr, c), lambda l, i: (l, i, 0))
    return pl.pallas_call(
        body, name=name, grid=(2, nr),
        in_specs=[pl.BlockSpec((N_DEV, tr, c), lambda l, i: (0, jnp.where(l == 0, i, nr - 1), 0)),
                  pl.BlockSpec((N_DEV, tr, c), lambda l, i: (0, jnp.where(l == 1, i, 0), 0)), blk, blk, blk],
        out_specs=[blk] * 4, out_shape=[jax.ShapeDtypeStruct(w.shape, F32)] * 4,
        compiler_params=_params("arbitrary", "arbitrary"),
    )(landed[0], landed[1], w, m, v)


def _adamw(g, w, m, v, name):
    def body(g_ref, w_ref, m_ref, v_ref, d_ref, nm_ref, nv_ref):
        d_ref[...], nm_ref[...], nv_ref[...] = _adamw_math(g_ref[...], w_ref[...], m_ref[...], v_ref[...])

    return pl.pallas_call(body, name=name, out_shape=[jax.ShapeDtypeStruct(w.shape, F32)] * 3)(g, w, m, v)


def _small_update(landed, params, name):
    layout = {
        "mix_pre": [("pre0", 0), ("x2", 0)], "mix_post": [("post0", 0), ("post1", 0)],
        "mlp_pre": [("mlp0", 0), ("mlp1", 0)], "mlp_post": [("mlp0", 1), ("mlp1", 1)],
        "kv": [("kv", 0)], "alog": [("gates", 0)], "dtb": [("gates", 1)], "out_gain": [("out_gain", 0)],
    }
    landed_names = sorted(landed)
    param_names = sorted(params)
    n_l, n_p = len(landed_names), len(param_names)

    def body(*refs):
        l_refs = dict(zip(landed_names, refs[:n_l]))
        p_refs = {p: refs[n_l + 3 * i:n_l + 3 * i + 3] for i, p in enumerate(param_names)}
        outs = refs[n_l + 3 * n_p:]
        o_refs = {p: outs[4 * i:4 * i + 4] for i, p in enumerate(param_names)}
        conv_ref, loss_ref = outs[4 * n_p:]
        sums = {nm: _sum_devices(l_refs[nm]) for nm in landed_names}
        sums["kv"] = sums["x2"][1:2, :] + sums["x2"][2:3, :]
        for p in param_names:
            w_ref, m_ref, v_ref = p_refs[p]
            g_ref, d_ref, nm_ref, nv_ref = o_refs[p]
            for r, (src, src_row) in enumerate(layout[p]):
                g = sums[src][src_row:src_row + 1, :]
                g_ref[r:r + 1, :] = g
                d, nm, nv = _adamw_math(g, w_ref[r:r + 1, :], m_ref[r:r + 1, :], v_ref[r:r + 1, :])
                d_ref[r:r + 1, :] = d
                nm_ref[r:r + 1, :] = nm
                nv_ref[r:r + 1, :] = nv
        conv_ref[...] = sums["conv"]
        loss_ref[...] = sums["loss"]

    args = [landed[nm] for nm in landed_names]
    out_shape = []
    for p in param_names:
        args += list(params[p])
        out_shape += [jax.ShapeDtypeStruct(params[p][0].shape, F32)] * 4
    out_shape += [jax.ShapeDtypeStruct(landed["conv"].shape[1:], F32), jax.ShapeDtypeStruct(landed["loss"].shape[1:], F32)]
    outs = pl.pallas_call(body, name=name, out_shape=out_shape)(*args)
    result = {p: tuple(outs[4 * i:4 * i + 4]) for i, p in enumerate(param_names)}
    result["conv"], result["loss"] = outs[4 * n_p], outs[4 * n_p + 1]
    return result


def _lanes(vec, offset):
    return jnp.pad(vec[None, :], ((0, 0), (offset, 128 - offset - vec.shape[0])))


def kernel(x, mix_pre_gain, mix_post_gain, mlp_pre_gain, mlp_post_gain, mlp_w_up, mlp_w_down, gdn_w_in, gdn_conv_w, gdn_a_log, gdn_dt_bias, gdn_out_gain, gdn_w_out, kv_gain, w_kv, sb_w_q, sb_w_o, loss_target, m_mix_pre_gain, m_mix_post_gain, m_mlp_pre_gain, m_mlp_post_gain, m_mlp_w_up, m_mlp_w_down, m_gdn_w_in, m_gdn_conv_w, m_gdn_a_log, m_gdn_dt_bias, m_gdn_out_gain, m_gdn_w_out, m_kv_gain, m_w_kv, m_sb_w_q, m_sb_w_o, v_mix_pre_gain, v_mix_post_gain, v_mlp_pre_gain, v_mlp_post_gain, v_mlp_w_up, v_mlp_w_down, v_gdn_w_in, v_gdn_conv_w, v_gdn_a_log, v_gdn_dt_bias, v_gdn_out_gain, v_gdn_w_out, v_kv_gain, v_w_kv, v_sb_w_q, v_sb_w_o):
    me = 4 * lax.axis_index("x") + 2 * lax.axis_index("y") + lax.axis_index("c")
    bf = lambda a: a.astype(BF16)

    shards = {"up0": bf(mlp_w_up[0]), "up1": bf(mlp_w_up[1]), "down0": bf(mlp_w_down[0]), "down1": bf(mlp_w_down[1]),
              "gdn_out": bf(gdn_w_out[0]), "kv": bf(w_kv), "sb_q": bf(sb_w_q[0]), "sb_o": bf(sb_w_o[0])}
    gdn_in, conv = _gather_two_level([bf(gdn_w_in[0]), gdn_conv_w[0]], "gather_first_weights")
    comm = _Fsdp(shards, {"gdn_in": _whole_weight("gdn_in", gdn_in), "conv": _whole_weight("conv", conv)})
    gains = {"mix_pre": mix_pre_gain, "mix_post": mix_post_gain, "mlp_pre": mlp_pre_gain, "mlp_post": mlp_post_gain,
             "kv": kv_gain[None, :], "alog": _lanes(gdn_a_log[0], N_HEADS), "dtb": _lanes(gdn_dt_bias[0], N_HEADS),
             "out_gain": gdn_out_gain}

    grad_x, big, small = _local_step(x[0], loss_target[0], gains, comm)

    small_names = ["pre0", "x2", "post0", "post1", "mlp0", "mlp1", "gates", "out_gain", "conv", "loss"]
    landed = _exchange([small[nm] for nm in small_names], [True] * len(small_names), "exchange_small_grads")
    small_landed = dict(zip(small_names, landed))

    results = {}
    results["mlp_w_up"] = _reduce_adamw_layers((comm.landed["up0"], comm.landed["up1"]), mlp_w_up, m_mlp_w_up, v_mlp_w_up,
                                               "adamw_mlp_w_up")
    results["mlp_w_down"] = _reduce_adamw_layers((comm.landed["down0"], comm.landed["down1"]), mlp_w_down, m_mlp_w_down,
                                                 v_mlp_w_down, "adamw_mlp_w_down")
    big_params = [("gdn_w_in", "gdn_in", gdn_w_in[0], m_gdn_w_in[0], v_gdn_w_in[0]),
                  ("gdn_w_out", "gdn_out", gdn_w_out[0], m_gdn_w_out[0], v_gdn_w_out[0]),
                  ("w_kv", "kv", w_kv, m_w_kv, v_w_kv), ("sb_w_q", "sb_q", sb_w_q[0], m_sb_w_q[0], v_sb_w_q[0]),
                  ("sb_w_o", "sb_o", sb_w_o[0], m_sb_w_o[0], v_sb_w_o[0])]
    for nm, short, w_, m_, v_ in big_params:
        results[nm] = _reduce_adamw(comm.landed[short], w_, m_, v_, "adamw_" + nm)
    lanes8 = lambda a: _lanes(a[0], N_HEADS)
    small_params = {
        "mix_pre": (mix_pre_gain, m_mix_pre_gain, v_mix_pre_gain), "mix_post": (mix_post_gain, m_mix_post_gain, v_mix_post_gain),
        "mlp_pre": (mlp_pre_gain, m_mlp_pre_gain, v_mlp_pre_gain), "mlp_post": (mlp_post_gain, m_mlp_post_gain, v_mlp_post_gain),
        "kv": (kv_gain[None, :], m_kv_gain[None, :], v_kv_gain[None, :]),
        "alog": (lanes8(gdn_a_log), lanes8(m_gdn_a_log), lanes8(v_gdn_a_log)),
        "dtb": (lanes8(gdn_dt_bias), lanes8(m_gdn_dt_bias), lanes8(v_gdn_dt_bias)),
        "out_gain": (gdn_out_gain, m_gdn_out_gain, v_gdn_out_gain),
    }
    sm = _small_update(small_landed, small_params, "small_update")
    conv_cols = QKV // N_DEV
    g_conv = lax.dynamic_slice(sm["conv"], (0, me * conv_cols), (8, conv_cols))[:CONV_WIDTH]
    conv_res = (g_conv,) + tuple(_adamw(g_conv, gdn_conv_w[0], m_gdn_conv_w[0], v_gdn_conv_w[0], "adamw_conv"))

    lead = lambda t: tuple(a[None] for a in t)
    heads8 = lambda t: tuple(a[:, N_HEADS:2 * N_HEADS] for a in t)
    per_weight = [
        sm["mix_pre"], sm["mix_post"], sm["mlp_pre"], sm["mlp_post"],
        tuple(results["mlp_w_up"]), tuple(results["mlp_w_down"]),
        lead(results["gdn_w_in"]), lead(conv_res), heads8(sm["alog"]), heads8(sm["dtb"]), sm["out_gain"],
        lead(results["gdn_w_out"]), tuple(a[0] for a in sm["kv"]), results["w_kv"], lead(results["sb_w_q"]), lead(results["sb_w_o"]),
    ]
    grads, deltas, new_ms, new_vs = zip(*per_weight)
    return (sm["loss"][0, 0], grad_x[None], *grads, *deltas, *new_ms, *new_vs)
```

```python
import functools
import math

import jax
import jax.numpy as jnp
from jax import lax
from jax.experimental import pallas as pl
from jax.experimental.pallas import tpu as pltpu

F32 = jnp.float32
BF16 = jnp.bfloat16

N_DEV = 8
D_MODEL = 1024
N_HEADS = 8
HEAD_DIM = 128
CHUNK = 64
CONV_WIDTH = 4
GDN_IN_COLS = 4 * D_MODEL + 2 * N_HEADS
GDN_IN_PAD = 4 * D_MODEL + 128
EPS = 1e-6

ADAM_LR = 0.001
ADAM_B1 = 0.9
ADAM_B2 = 0.999
ADAM_EPS = 1e-08
ADAM_WD = 0.01
ADAM_STEP = 10

VMEM_LIMIT_BYTES = 56 * 1024 * 1024
MESH = pl.DeviceIdType.MESH


def _params(*semantics):
    return pltpu.CompilerParams(dimension_semantics=semantics, vmem_limit_bytes=VMEM_LIMIT_BYTES)


def _dims(ta, tb):
    return (((0,) if ta else (1,), (1,) if tb else (0,)), ((), ()))


def _dot(a, b, ta=False, tb=False):
    return lax.dot_general(a.astype(BF16), b.astype(BF16), _dims(ta, tb), preferred_element_type=F32)


def _dot_f32(a, b, ta=False, tb=False):
    return lax.dot_general(a, b, _dims(ta, tb), precision=lax.Precision.HIGHEST, preferred_element_type=F32)


def _dot_3x(a, b, ta=False, tb=False):
    return lax.dot_general(a, b, _dims(ta, tb), precision=lax.Precision.HIGH, preferred_element_type=F32)


def _make_mm(dot):
    @functools.partial(jax.custom_vjp, nondiff_argnums=(2, 3))
    def mm(a, b, ta, tb):
        return dot(a, b, ta, tb)

    def fwd(a, b, ta, tb):
        return dot(a, b, ta, tb), (a, b)

    def bwd(ta, tb, res, g):
        a, b = res
        if not ta and not tb:
            return mm(g, b, False, True), mm(a, g, True, False)
        if not ta and tb:
            return mm(g, b, False, False), mm(g, a, True, False)
        if ta and not tb:
            return mm(b, g, False, True), mm(a, g, False, False)
        raise NotImplementedError

    mm.defvjp(fwd, bwd)
    return mm


_mm = _make_mm(_dot)
_mm_f32 = _make_mm(_dot_f32)


def _rms(x, gain):
    r = lax.rsqrt(jnp.mean(x * x, axis=-1, keepdims=True) + EPS)
    return x * r * gain


def _rms_bwd(x, gain, dy):
    r = lax.rsqrt(jnp.mean(x * x, axis=-1, keepdims=True) + EPS)
    xh = x * r
    dgain = jnp.sum(dy * xh, axis=0, keepdims=True)
    dxh = dy * gain
    dx = r * (dxh - xh * jnp.mean(dxh * xh, axis=-1, keepdims=True))
    return dx, dgain


def _silu(x):
    return x / (1.0 + jnp.exp(-x))


def _acc_rows(ref, first, rows):
    @pl.when(first)
    def _():
        ref[...] = jnp.zeros_like(ref)

    for r, val in enumerate(rows):
        ref[r:r + 1, :] += val


def _norm_matmul(x, gain, w, out_dtype, name, tm=1024, tn=1024):
    n, d = x.shape
    g, _, wc = w.shape
    tm, tn = min(tm, n), min(tn, wc)
    per = wc // tn
    assert n % tm == 0 and wc % tn == 0

    def body(x_ref, gain_ref, w_ref, out_ref, h_ref):
        @pl.when(pl.program_id(1) == 0)
        def _():
            h_ref[...] = _rms(x_ref[...], gain_ref[...]).astype(BF16)

        out_ref[...] = jnp.dot(h_ref[...], w_ref[...], preferred_element_type=F32).astype(out_dtype)

    return pl.pallas_call(
        body, name=name, grid=(n // tm, g * per),
        in_specs=[pl.BlockSpec((tm, d), lambda i, j: (i, 0)),
                  pl.BlockSpec((1, d), lambda i, j: (0, 0)),
                  pl.BlockSpec((None, d, tn), lambda i, j: (j // per, 0, j % per))],
        out_specs=[pl.BlockSpec((tm, tn), lambda i, j: (i, j)),
                   pl.BlockSpec((tm, d), lambda i, j: (i, 0))],
        out_shape=[jax.ShapeDtypeStruct((n, g * wc), out_dtype), jax.ShapeDtypeStruct((n, d), BF16)],
        compiler_params=_params("parallel", "arbitrary"),
    )(x, gain, w)


def _norm_matmul_bwd(branches, x, add, name, tm=1024, ride=None):
    n, d = x.shape
    tm = min(tm, n)
    tns = [min(tn, w.shape[2]) for _, w, _, tn in branches]
    pers = [w.shape[2] // tn for (_, w, _, _), tn in zip(branches, tns)]
    ncols = [w.shape[0] * per for (_, w, _, _), per in zip(branches, pers)]
    offs = [sum(ncols[:b]) for b in range(len(branches))]
    total = sum(ncols)
    nb = len(branches)

    def body(*refs):
        dout_refs, w_refs, gain_refs = refs[:nb], refs[nb:2 * nb], refs[2 * nb:3 * nb]
        x_ref, add_ref, dx_ref, dgain_ref, acc_ref, sum_ref = refs[3 * nb:]
        i, j = pl.program_id(0), pl.program_id(1)

        @pl.when((i == 0) & (j == 0))
        def _():
            dgain_ref[...] = jnp.zeros_like(dgain_ref)

        for b in range(nb):
            first, last = offs[b], offs[b] + ncols[b] - 1

            @pl.when((j >= first) & (j <= last))
            def _(b=b, first=first):
                part = _dot(dout_refs[b][...], w_refs[b][...], tb=True)

                @pl.when(j == first)
                def _():
                    acc_ref[...] = part

                @pl.when(j > first)
                def _():
                    acc_ref[...] += part

            @pl.when(j == last)
            def _(b=b):
                xv = x_ref[...]
                xh = xv * lax.rsqrt(jnp.mean(xv * xv, axis=-1, keepdims=True) + EPS)
                dh = acc_ref[...]
                dgain_ref[b:b + 1, :] += jnp.sum(dh * xh, axis=0, keepdims=True)
                scaled = dh * gain_refs[b][...]
                if b == 0:
                    sum_ref[...] = scaled
                else:
                    sum_ref[...] += scaled

        @pl.when(j == total - 1)
        def _():
            xv = x_ref[...]
            r = lax.rsqrt(jnp.mean(xv * xv, axis=-1, keepdims=True) + EPS)
            xh = xv * r
            dxh = sum_ref[...]
            dx_ref[...] = add_ref[...] + r * (dxh - xh * jnp.mean(dxh * xh, axis=-1, keepdims=True))

    def dout_spec(b):
        return pl.BlockSpec((tm, tns[b]), lambda i, j: (i, jnp.clip(j - offs[b], 0, ncols[b] - 1)))

    def w_spec(b):
        def index(i, j):
            c = jnp.clip(j - offs[b], 0, ncols[b] - 1)
            return (c // pers[b], 0, c % pers[b])
        return pl.BlockSpec((None, d, tns[b]), index)

    row = lambda i, j: (i, 0)
    const = lambda i, j: (0, 0)
    nrow = n // tm
    (dx, dgain), got = _call(
        body, [b[0] for b in branches] + [b[1] for b in branches] + [b[2] for b in branches] + [x, add],
        name=name, grid=(nrow, total),
        in_specs=[dout_spec(b) for b in range(nb)] + [w_spec(b) for b in range(nb)]
                 + [pl.BlockSpec((1, d), const)] * nb + [pl.BlockSpec((tm, d), row), pl.BlockSpec((tm, d), row)],
        out_specs=[pl.BlockSpec((tm, d), row), pl.BlockSpec((8, d), const)],
        out_shape=[jax.ShapeDtypeStruct((n, d), F32), jax.ShapeDtypeStruct((8, d), F32)],
        scratch_shapes=[pltpu.VMEM((tm, d), F32), pltpu.VMEM((tm, d), F32)],
        compiler_params=_params("arbitrary", "arbitrary"),
        ride=ride, first=lambda: (pl.program_id(0) == 0) & (pl.program_id(1) == 0),
        last=lambda: (pl.program_id(0) == nrow - 1) & (pl.program_id(1) == total - 1))
    return (dx, dgain) if ride is None else (dx, dgain, got)


def _matmul_tn(a, b, groups, name, ta=1024, tb=1024, tk=1024):
    n, ka = a.shape
    _, kb = b.shape
    wc = kb // groups
    ta, tb, tk = min(ta, ka), min(tb, wc), min(tk, n)
    per = wc // tb
    nk = n // tk
    assert ka % ta == 0 and wc % tb == 0 and n % tk == 0

    def body(a_ref, b_ref, out_ref, acc_ref):
        k = pl.program_id(2)

        @pl.when(k == 0)
        def _():
            acc_ref[...] = jnp.zeros_like(acc_ref)

        acc_ref[...] += _dot(a_ref[...], b_ref[...], ta=True)

        @pl.when(k == nk - 1)
        def _():
            out_ref[...] = acc_ref[...].astype(BF16)

    return pl.pallas_call(
        body, name=name, grid=(ka // ta, groups * per, nk),
        in_specs=[pl.BlockSpec((tk, ta), lambda i, j, k: (k, i)),
                  pl.BlockSpec((tk, tb), lambda i, j, k: (k, j))],
        out_specs=pl.BlockSpec((None, ta, tb), lambda i, j, k: (j // per, i, j % per)),
        out_shape=jax.ShapeDtypeStruct((groups, ka, wc), BF16),
        scratch_shapes=[pltpu.VMEM((ta, tb), F32)],
        compiler_params=_params("parallel", "parallel", "arbitrary"),
    )(a, b)


def _gated_head_norm(o, gate, out_gain):
    parts = []
    for h in range(N_HEADS):
        sl = slice(h * HEAD_DIM, (h + 1) * HEAD_DIM)
        parts.append(_rms(o[:, sl], out_gain) * _silu(gate[:, sl]))
    return parts


def _out_proj(a_inputs, gated, x_in, w, gain, out_gain, name, tm=1024):
    n, d = x_in.shape
    k = w.shape[0]
    tm = min(tm, n)

    def body(*refs):
        if gated:
            o_ref, gate_ref, og_ref, x_ref, w_ref, gain_ref, xo_ref, y_ref, a_ref = refs
            parts = _gated_head_norm(o_ref[...], gate_ref[...], og_ref[...])
            for h, part in enumerate(parts):
                a_ref[:, h * HEAD_DIM:(h + 1) * HEAD_DIM] = part.astype(BF16)
            a = a_ref[...]
        else:
            a_in_ref, x_ref, w_ref, gain_ref, xo_ref, y_ref = refs
            a = a_in_ref[...]
        y = jnp.dot(a, w_ref[...], preferred_element_type=F32)
        y_ref[...] = y
        xo_ref[...] = x_ref[...] + _rms(y, gain_ref[...])

    row = lambda i: (i, 0)
    const = lambda i: (0, 0)
    if gated:
        a_specs = [pl.BlockSpec((tm, k), row), pl.BlockSpec((tm, D_MODEL), lambda i: (i, 3)),
                   pl.BlockSpec((1, HEAD_DIM), const)]
        a_args = list(a_inputs) + [out_gain]
    else:
        a_specs = [pl.BlockSpec((tm, k), row)]
        a_args = list(a_inputs)
    out_specs = [pl.BlockSpec((tm, d), row), pl.BlockSpec((tm, d), row)]
    out_shape = [jax.ShapeDtypeStruct((n, d), F32), jax.ShapeDtypeStruct((n, d), F32)]
    if gated:
        out_specs.append(pl.BlockSpec((tm, k), row))
        out_shape.append(jax.ShapeDtypeStruct((n, k), BF16))
    return pl.pallas_call(
        body, name=name, grid=(n // tm,),
        in_specs=a_specs + [pl.BlockSpec((tm, d), row), pl.BlockSpec((k, d), const), pl.BlockSpec((1, d), const)],
        out_specs=out_specs, out_shape=out_shape,
        compiler_params=_params("parallel"),
    )(*a_args, x_in, w, gain)


def _out_proj_bwd(dxo, y, w, gain, gated_inputs, name, tm=1024):
    n, d = dxo.shape
    k = w.shape[0]
    tm = min(tm, n)
    gated = gated_inputs is not None

    def body(*refs):
        if gated:
            (dxo_ref, y_ref, w_ref, gain_ref, o_ref, gate_ref, og_ref,
             dy_ref, dgain_ref, do_ref, dgate_ref, dog_ref) = refs
        else:
            dxo_ref, y_ref, w_ref, gain_ref, dy_ref, dgain_ref, da_ref = refs
        first = pl.program_id(0) == 0
        dy, dgain = _rms_bwd(y_ref[...], gain_ref[...], dxo_ref[...])
        dy_ref[...] = dy.astype(BF16)
        _acc_rows(dgain_ref, first, [dgain])
        da = _dot(dy_ref[...], w_ref[...], tb=True)
        if not gated:
            da_ref[...] = da.astype(BF16)
            return
        og = og_ref[...]
        dog = jnp.zeros_like(og)
        for h in range(N_HEADS):
            sl = slice(h * HEAD_DIM, (h + 1) * HEAD_DIM)
            fn = lambda o_h, g_h, gn: _rms(o_h, gn) * _silu(g_h)
            _, vjp = jax.vjp(fn, o_ref[:, sl], gate_ref[:, sl], og)
            do_h, dgate_h, dog_h = vjp(da[:, sl])
            do_ref[:, sl] = do_h
            dgate_ref[:, sl] = dgate_h.astype(BF16)
            dog = dog + dog_h
        _acc_rows(dog_ref, first, [dog])

    row = lambda i: (i, 0)
    const = lambda i: (0, 0)
    in_specs = [pl.BlockSpec((tm, d), row), pl.BlockSpec((tm, d), row), pl.BlockSpec((k, d), const),
                pl.BlockSpec((1, d), const)]
    args = [dxo, y, w, gain]
    out_specs = [pl.BlockSpec((tm, d), row), pl.BlockSpec((8, d), const)]
    out_shape = [jax.ShapeDtypeStruct((n, d), BF16), jax.ShapeDtypeStruct((8, d), F32)]
    if gated:
        in_specs += [pl.BlockSpec((tm, k), row), pl.BlockSpec((tm, D_MODEL), lambda i: (i, 3)),
                     pl.BlockSpec((1, HEAD_DIM), const)]
        args += list(gated_inputs)
        out_specs += [pl.BlockSpec((tm, k), row), pl.BlockSpec((tm, k), row), pl.BlockSpec((8, HEAD_DIM), const)]
        out_shape += [jax.ShapeDtypeStruct((n, k), F32), jax.ShapeDtypeStruct((n, k), BF16),
                      jax.ShapeDtypeStruct((8, HEAD_DIM), F32)]
    else:
        out_specs.append(pl.BlockSpec((tm, k), row))
        out_shape.append(jax.ShapeDtypeStruct((n, k), BF16))
    return pl.pallas_call(
        body, name=name, grid=(n // tm,), in_specs=in_specs, out_specs=out_specs, out_shape=out_shape,
        compiler_params=_params("arbitrary"),
    )(*args)


def _mlp_fwd(x_in, g_pre, w_up, w_down, g_post, name, tm=1024, target=None, ride=None):
    n, d = x_in.shape
    g, _, wc = w_up.shape
    tm = min(tm, n)
    with_loss = target is not None

    def body(*refs):
        if with_loss:
            x_ref, gpre_ref, wup_ref, wdown_ref, gpost_ref, t_ref, xo_ref, y_ref, h_ref, u_ref, a_ref, loss_ref, acc_ref = refs
        else:
            x_ref, gpre_ref, wup_ref, wdown_ref, gpost_ref, xo_ref, y_ref, h_ref, u_ref, a_ref, acc_ref = refs
        i, j = pl.program_id(0), pl.program_id(1)

        @pl.when(j == 0)
        def _():
            h_ref[...] = _rms(x_ref[...], gpre_ref[...]).astype(BF16)
            acc_ref[...] = jnp.zeros_like(acc_ref)

        u = jnp.dot(h_ref[...], wup_ref[...], preferred_element_type=F32).astype(BF16)
        u_ref[...] = u
        a = jnp.square(jnp.maximum(u, 0))
        a_ref[...] = a
        acc_ref[...] += jnp.dot(a, wdown_ref[...], preferred_element_type=F32)

        @pl.when(j == g - 1)
        def _():
            y = acc_ref[...]
            y_ref[...] = y
            xo = x_ref[...] + _rms(y, gpost_ref[...])
            if with_loss:
                err = xo - t_ref[...]
                xo_ref[...] = err * (1.0 / d)
                part = 0.5 * jnp.sum(jnp.mean(err * err, axis=-1, keepdims=True), axis=0, keepdims=True)
                _acc_rows(loss_ref, i == 0, [jnp.broadcast_to(part, (1, 128))])
            else:
                xo_ref[...] = xo

    row = lambda i, j: (i, 0)
    const = lambda i, j: (0, 0)
    in_specs = [pl.BlockSpec((tm, d), row), pl.BlockSpec((1, d), const),
                pl.BlockSpec((None, d, wc), lambda i, j: (j, 0, 0)),
                pl.BlockSpec((wc, d), lambda i, j: (j, 0)), pl.BlockSpec((1, d), const)]
    out_specs = [pl.BlockSpec((tm, d), row), pl.BlockSpec((tm, d), row), pl.BlockSpec((tm, d), row),
                 pl.BlockSpec((tm, wc), lambda i, j: (i, j)), pl.BlockSpec((tm, wc), lambda i, j: (i, j))]
    out_shape = [jax.ShapeDtypeStruct((n, d), F32), jax.ShapeDtypeStruct((n, d), F32),
                 jax.ShapeDtypeStruct((n, d), BF16), jax.ShapeDtypeStruct((n, g * wc), BF16),
                 jax.ShapeDtypeStruct((n, g * wc), BF16)]
    args = [x_in, g_pre, w_up, w_down, g_post]
    if with_loss:
        in_specs.append(pl.BlockSpec((tm, d), row))
        out_specs.append(pl.BlockSpec((8, 128), const))
        out_shape.append(jax.ShapeDtypeStruct((8, 128), F32))
        args.append(target)
    nrow = n // tm
    outs, got = _call(
        body, args, name=name, grid=(nrow, g), in_specs=in_specs, out_specs=out_specs, out_shape=out_shape,
        scratch_shapes=[pltpu.VMEM((tm, d), F32)], compiler_params=_params("arbitrary", "arbitrary"),
        ride=ride, first=lambda: (pl.program_id(0) == 0) & (pl.program_id(1) == 0),
        last=lambda: (pl.program_id(0) == nrow - 1) & (pl.program_id(1) == g - 1))
    return outs if ride is None else (outs, got)


def _mlp_bwd(dxo, y, g_post, w_down, u, w_up, x_in, g_pre, name, tm=1024, ride=None):
    n, d = dxo.shape
    g, _, wc = w_up.shape
    tm = min(tm, n)

    def body(dxo_ref, y_ref, gpost_ref, wdown_ref, u_ref, wup_ref, x_ref, gpre_ref,
             dx_ref, dy_ref, du_ref, dgain_ref, acc_ref, dgpost_ref):
        i, j = pl.program_id(0), pl.program_id(1)

        @pl.when(j == 0)
        def _():
            dy, dgpost = _rms_bwd(y_ref[...], gpost_ref[...], dxo_ref[...])
            dy_ref[...] = dy.astype(BF16)
            dgpost_ref[...] = dgpost
            acc_ref[...] = jnp.zeros_like(acc_ref)

        da = _dot(dy_ref[...], wdown_ref[...], tb=True)
        du = (da * (2.0 * jnp.maximum(u_ref[...], 0).astype(F32))).astype(BF16)
        du_ref[...] = du
        acc_ref[...] += _dot(du, wup_ref[...], tb=True)

        @pl.when(j == g - 1)
        def _():
            dx, dgpre = _rms_bwd(x_ref[...], gpre_ref[...], acc_ref[...])
            dx_ref[...] = dxo_ref[...] + dx
            _acc_rows(dgain_ref, i == 0, [dgpre, dgpost_ref[...]])

    row = lambda i, j: (i, 0)
    const = lambda i, j: (0, 0)
    nrow = n // tm
    outs, got = _call(
        body, [dxo, y, g_post, w_down, u, w_up, x_in, g_pre], name=name, grid=(nrow, g),
        in_specs=[pl.BlockSpec((tm, d), row), pl.BlockSpec((tm, d), row), pl.BlockSpec((1, d), const),
                  pl.BlockSpec((wc, d), lambda i, j: (j, 0)), pl.BlockSpec((tm, wc), lambda i, j: (i, j)),
                  pl.BlockSpec((None, d, wc), lambda i, j: (j, 0, 0)), pl.BlockSpec((tm, d), row),
                  pl.BlockSpec((1, d), const)],
        out_specs=[pl.BlockSpec((tm, d), row), pl.BlockSpec((tm, d), row),
                   pl.BlockSpec((tm, wc), lambda i, j: (i, j)), pl.BlockSpec((8, d), const)],
        out_shape=[jax.ShapeDtypeStruct((n, d), F32), jax.ShapeDtypeStruct((n, d), BF16),
                   jax.ShapeDtypeStruct((n, g * wc), BF16), jax.ShapeDtypeStruct((8, d), F32)],
        scratch_shapes=[pltpu.VMEM((tm, d), F32), pltpu.VMEM((1, d), F32)],
        compiler_params=_params("arbitrary", "arbitrary"),
        ride=ride, first=lambda: (pl.program_id(0) == 0) & (pl.program_id(1) == 0),
        last=lambda: (pl.program_id(0) == nrow - 1) & (pl.program_id(1) == g - 1))
    return outs if ride is None else (outs, got)


QKV = 3 * D_MODEL


def _shifted(x, prev8, s):
    if s == 0:
        return x
    tm = x.shape[0]
    rolled = pltpu.roll(x, s, 0)
    head = pltpu.roll(prev8, s, 0)
    head = jnp.concatenate([head, jnp.zeros((tm - 8, x.shape[1]), x.dtype)], axis=0)
    rows = lax.broadcasted_iota(jnp.int32, x.shape, 0)
    return jnp.where(rows < s, head, rolled)


def _conv(x, prev8, conv_w):
    out = x * conv_w[3:4, :]
    for s in range(1, CONV_WIDTH):
        out = out + _shifted(x, prev8, s) * conv_w[3 - s:4 - s, :]
    return out


def _l2norm(x):
    return x * lax.rsqrt(jnp.sum(x * x, axis=-1, keepdims=True) + EPS)


def _gdn_act_head(cq, ck, cv):
    return _l2norm(_silu(cq)) * (HEAD_DIM ** -0.5), _l2norm(_silu(ck)), _silu(cv)


def _gdn_gates(ba, alog, dtb):
    lane = lax.broadcasted_iota(jnp.int32, ba.shape, 1)
    beta = 1.0 / (1.0 + jnp.exp(-ba))
    t = ba + dtb
    softplus = jnp.maximum(t, 0.0) + jnp.log(1.0 + jnp.exp(-jnp.abs(t)))
    g = -jnp.exp(alog) * softplus
    return jnp.where(lane < N_HEADS, beta, jnp.where(lane < 2 * N_HEADS, g, 0.0))


def _qkv_cols(part, h):
    start = part * D_MODEL + h * HEAD_DIM
    return slice(start, start + HEAD_DIM)


def _prev_rows_spec(tm, cols, colblock, order):
    per = tm // 8
    return pl.BlockSpec((8, cols), lambda i: (jnp.maximum(order(i) * per - 1, 0), colblock))


def _gdn_pre(proj, conv_w, alog, dtb, name, tm=256):
    n = proj.shape[0]

    def body(x_ref, prev_ref, ba_ref, cw_ref, alog_ref, dtb_ref, q_ref, k_ref, v_ref, bg_ref):
        first = pl.program_id(0) == 0
        for h in range(N_HEADS):
            convs = []
            for part in range(3):
                sl = _qkv_cols(part, h)
                prev8 = jnp.where(first, 0.0, prev_ref[:, sl])
                convs.append(_conv(x_ref[:, sl], prev8, cw_ref[:, sl]))
            out = slice(h * HEAD_DIM, (h + 1) * HEAD_DIM)
            q_ref[:, out], k_ref[:, out], v_ref[:, out] = _gdn_act_head(*convs)
        bg_ref[...] = _gdn_gates(ba_ref[...], alog_ref[...], dtb_ref[...])

    row = lambda i: (i, 0)
    const = lambda i: (0, 0)
    ident = lambda i: i
    return pl.pallas_call(
        body, name=name, grid=(n // tm,),
        in_specs=[pl.BlockSpec((tm, QKV), row), _prev_rows_spec(tm, QKV, 0, ident),
                  pl.BlockSpec((tm, 128), lambda i: (i, 4 * D_MODEL // 128)),
                  pl.BlockSpec((CONV_WIDTH, QKV), const), pl.BlockSpec((1, 128), const), pl.BlockSpec((1, 128), const)],
        out_specs=[pl.BlockSpec((tm, D_MODEL), row)] * 3 + [pl.BlockSpec((tm, 128), row)],
        out_shape=[jax.ShapeDtypeStruct((n, D_MODEL), F32)] * 3 + [jax.ShapeDtypeStruct((n, 128), F32)],
        compiler_params=_params("parallel"),
    )(proj, proj, proj, conv_w, alog, dtb)


def _gdn_pre_bwd(proj, conv_w, alog, dtb, dq, dk, dv, dbg, dgate, name, tm=256):
    n = proj.shape[0]
    nt = n // tm

    def body(x_ref, prev_ref, ba_ref, cw_ref, alog_ref, dtb_ref, dq_ref, dk_ref, dv_ref, dbg_ref, dgate_ref,
             dproj_ref, dcw_ref, dgates_ref, carry_ref):
        step = pl.program_id(0)
        tile = nt - 1 - step
        @pl.when(step == 0)
        def _():
            carry_ref[...] = jnp.zeros_like(carry_ref)
            dcw_ref[...] = jnp.zeros_like(dcw_ref)

        rows = lax.broadcasted_iota(jnp.int32, (tm, HEAD_DIM), 0)
        for h in range(N_HEADS):
            cols = [_qkv_cols(part, h) for part in range(3)]
            prevs = [jnp.where(tile == 0, 0.0, prev_ref[:, sl]) for sl in cols]
            convs = [_conv(x_ref[:, sl], prev8, cw_ref[:, sl]) for sl, prev8 in zip(cols, prevs)]
            _, vjp = jax.vjp(_gdn_act_head, *convs)
            out = slice(h * HEAD_DIM, (h + 1) * HEAD_DIM)
            dcs = vjp((dq_ref[:, out], dk_ref[:, out], dv_ref[:, out]))
            for sl, prev8, dc in zip(cols, prevs, dcs):
                x = x_ref[:, sl]
                cw = cw_ref[:, sl]
                dx = dc * cw[3:4, :]
                wrapped = jnp.zeros((8, HEAD_DIM), F32)
                dcw_ref[3:4, sl] += jnp.sum(dc * x, axis=0, keepdims=True)
                for s in range(1, CONV_WIDTH):
                    r = pltpu.roll(dc, tm - s, 0) * cw[3 - s:4 - s, :]
                    dx = dx + jnp.where(rows < tm - s, r, 0.0)
                    wrapped = wrapped + jnp.where(rows[tm - 8:, :] >= tm - s, r[tm - 8:, :], 0.0)
                    dcw_ref[3 - s:4 - s, sl] += jnp.sum(dc * _shifted(x, prev8, s), axis=0, keepdims=True)
                from_next = jnp.concatenate([jnp.zeros((tm - 8, HEAD_DIM), F32), carry_ref[:, sl]], axis=0)
                dproj_ref[:, sl] = (dx + from_next).astype(BF16)
                carry_ref[:, sl] = wrapped

        _, vjp = jax.vjp(_gdn_gates, ba_ref[...], alog_ref[...], dtb_ref[...])
        dba, dalog, ddtb = vjp(dbg_ref[...])
        dproj_ref[:, QKV:4 * D_MODEL] = dgate_ref[...]
        dproj_ref[:, 4 * D_MODEL:] = dba.astype(BF16)
        _acc_rows(dgates_ref, step == 0, [dalog, ddtb])

    rev = lambda i: nt - 1 - i
    row = lambda i: (rev(i), 0)
    const = lambda i: (0, 0)
    return pl.pallas_call(
        body, name=name, grid=(nt,),
        in_specs=[pl.BlockSpec((tm, QKV), row), _prev_rows_spec(tm, QKV, 0, rev),
                  pl.BlockSpec((tm, 128), lambda i: (rev(i), 4 * D_MODEL // 128)),
                  pl.BlockSpec((CONV_WIDTH, QKV), const), pl.BlockSpec((1, 128), const), pl.BlockSpec((1, 128), const),
                  pl.BlockSpec((tm, D_MODEL), row), pl.BlockSpec((tm, D_MODEL), row), pl.BlockSpec((tm, D_MODEL), row),
                  pl.BlockSpec((tm, 128), row), pl.BlockSpec((tm, D_MODEL), row)],
        out_specs=[pl.BlockSpec((tm, GDN_IN_PAD), row), pl.BlockSpec((8, QKV), const), pl.BlockSpec((8, 128), const)],
        out_shape=[jax.ShapeDtypeStruct((n, GDN_IN_PAD), BF16), jax.ShapeDtypeStruct((8, QKV), F32),
                   jax.ShapeDtypeStruct((8, 128), F32)],
        scratch_shapes=[pltpu.VMEM((8, QKV), F32)],
        compiler_params=_params("arbitrary"),
    )(proj, proj, proj, conv_w, alog, dtb, dq, dk, dv, dbg, dgate)


@jax.custom_vjp
def _unit_lower_inverses(lowers):
    n = lowers[0].shape[0]
    eye = (lax.broadcasted_iota(jnp.int32, (n, n), 0) == lax.broadcasted_iota(jnp.int32, (n, n), 1)).astype(F32)
    xs = [-low for low in lowers]
    ps = [eye + x for x in xs]
    ys = [_dot_3x(x, x) for x in xs]
    for _ in range(int(math.log2(CHUNK)) - 2):
        both = [_dot_3x(y, jnp.concatenate([y, p], axis=1)) for y, p in zip(ys, ps)]
        ys = [b[:, :n] for b in both]
        ps = [p + b[:, n:] for p, b in zip(ps, both)]
    return tuple(p + _dot_3x(y, p) for y, p in zip(ys, ps))


def _unit_lower_inverses_fwd(lowers):
    ts = _unit_lower_inverses(lowers)
    return ts, ts


def _unit_lower_inverses_bwd(ts, dts):
    left = [_dot_3x(t, dt, ta=True) for t, dt in zip(ts, dts)]
    return (tuple(-_dot_3x(l, t, tb=True) for l, t in zip(left, ts)),)


_unit_lower_inverses.defvjp(_unit_lower_inverses_fwd, _unit_lower_inverses_bwd)


@jax.custom_vjp
def _known_inverses(lowers, inverses):
    return inverses


def _known_inverses_bwd(ts, dts):
    (d_lowers,) = _unit_lower_inverses_bwd(ts, dts)
    return d_lowers, tuple(jnp.zeros_like(t) for t in ts)


_known_inverses.defvjp(lambda lowers, inverses: (inverses, inverses), _known_inverses_bwd)


@functools.partial(jax.custom_vjp, nondiff_argnums=(1,))
def _halves(x, axis):
    half = x.shape[axis] // 2
    return (x[:half], x[half:]) if axis == 0 else (x[:, :half], x[:, half:])


_halves.defvjp(lambda x, axis: (_halves(x, axis), None), lambda axis, _, g: (jnp.concatenate(g, axis=axis),))

GDN_STEP_CHUNKS = 2


def _gdn_chunks(qs, ks, vs, bgs, states, inverses=None, keep_inverses=False):
    c = CHUNK
    heads = range(N_HEADS)
    items = [(j, h) for j in range(len(bgs)) for h in heads]
    row = lax.broadcasted_iota(jnp.int32, (c, c), 0)
    col = lax.broadcasted_iota(jnp.int32, (c, c), 1)
    incl, strict, eye = row >= col, row > col, row == col
    lane = lax.broadcasted_iota(jnp.int32, (c, 128), 1)
    rowc = lax.broadcasted_iota(jnp.int32, (c, 1), 0)
    gc_all = [_mm_f32(incl.astype(F32), bg, False, False) for bg in bgs]
    q, k, v = ([xs[j][h] for j, h in items] for xs in (qs, ks, vs))
    n = range(len(items))
    beta = [jnp.sum(jnp.where(lane == h, bgs[j], 0.0), axis=1, keepdims=True) for j, h in items]
    gc = [jnp.sum(jnp.where(lane == N_HEADS + h, gc_all[j], 0.0), axis=1, keepdims=True) for j, h in items]
    gc_row = [jnp.sum(jnp.where(eye, gc[i], 0.0), axis=0, keepdims=True) for i in n]
    gc_last = [jnp.sum(jnp.where(rowc == c - 1, gc[i], 0.0), axis=0, keepdims=True) for i in n]
    decay = [jnp.where(incl, jnp.exp(jnp.where(incl, gc[i] - gc_row[i], 0.0)), 0.0) for i in n]
    kb = [k[i] * beta[i] for i in n]
    kb_q = [_halves(_mm(jnp.concatenate([kb[i], q[i]], axis=0), k[i], False, True), 0) for i in n]
    lower = tuple(jnp.where(strict, kb_q[i][0] * decay[i], 0.0) for i in n)
    attn = [kb_q[i][1] * decay[i] for i in n]
    t_mat = _unit_lower_inverses(lower) if inverses is None else _known_inverses(lower, inverses)
    egc = [jnp.exp(gc[i]) for i in n]
    w_u = [_halves(_mm(t_mat[i], jnp.concatenate([kb[i] * egc[i], v[i] * beta[i]], axis=1), False, False), 1)
           for i in n]
    w, u = [x[0] for x in w_u], [x[1] for x in w_u]
    qg = [q[i] * egc[i] for i in n]
    kg = [k[i] * jnp.exp(gc_last[i] - gc[i]) for i in n]
    outs, cur = [], list(states)
    for j in range(len(bgs)):
        at = lambda h: j * N_HEADS + h
        w_qg = [_halves(_mm(jnp.concatenate([w[at(h)], qg[at(h)]], axis=0), cur[h], False, False), 0)
                for h in heads]
        v_new = [u[at(h)] - w_qg[h][0] for h in heads]
        outs.append(tuple(w_qg[h][1] + _mm(attn[at(h)], v_new[h], False, False) for h in heads))
        cur = [cur[h] * jnp.exp(gc_last[at(h)]) + _mm(kg[at(h)], v_new[h], True, False) for h in heads]
    return (tuple(outs), tuple(cur), t_mat) if keep_inverses else (tuple(outs), tuple(cur))


def _chunk_head_slices(ref, nch):
    return tuple(tuple(ref[j * CHUNK:(j + 1) * CHUNK, h * HEAD_DIM:(h + 1) * HEAD_DIM] for h in range(N_HEADS))
                 for j in range(nch))


def _store_chunk_heads(ref, values):
    for j, chunk in enumerate(values):
        for h, val in enumerate(chunk):
            ref[j * CHUNK:(j + 1) * CHUNK, h * HEAD_DIM:(h + 1) * HEAD_DIM] = val


def _gdn_scan(qn, kn, v, bg, name, ride=None):
    n = qn.shape[0]
    nch = GDN_STEP_CHUNKS
    rows = nch * CHUNK
    nc = n // rows

    def body(q_ref, k_ref, v_ref, bg_ref, o_ref, saved_ref, inv_ref, state_ref):
        @pl.when(pl.program_id(0) == 0)
        def _():
            state_ref[...] = jnp.zeros_like(state_ref)

        states = tuple(state_ref[h] for h in range(N_HEADS))
        for h in range(N_HEADS):
            saved_ref[h] = states[h]
        bgs = tuple(bg_ref[j * CHUNK:(j + 1) * CHUNK, :] for j in range(nch))
        outs, new_states, inverses = _gdn_chunks(_chunk_head_slices(q_ref, nch), _chunk_head_slices(k_ref, nch),
                                                 _chunk_head_slices(v_ref, nch), bgs, states, keep_inverses=True)
        _store_chunk_heads(o_ref, outs)
        for h in range(N_HEADS):
            state_ref[h] = new_states[h]
        for item, inverse in enumerate(inverses):
            inv_ref[item] = inverse

    row = lambda i: (i, 0)
    return _call(
        body, [qn, kn, v, bg], name=name, grid=(nc,),
        in_specs=[pl.BlockSpec((rows, D_MODEL), row)] * 3 + [pl.BlockSpec((rows, 128), row)],
        out_specs=[pl.BlockSpec((rows, D_MODEL), row),
                   pl.BlockSpec((None, N_HEADS, HEAD_DIM, HEAD_DIM), lambda i: (i, 0, 0, 0)),
                   pl.BlockSpec((None, nch * N_HEADS, CHUNK, CHUNK), lambda i: (i, 0, 0, 0))],
        out_shape=[jax.ShapeDtypeStruct((n, D_MODEL), F32),
                   jax.ShapeDtypeStruct((nc, N_HEADS, HEAD_DIM, HEAD_DIM), F32),
                   jax.ShapeDtypeStruct((nc, nch * N_HEADS, CHUNK, CHUNK), F32)],
        scratch_shapes=[pltpu.VMEM((N_HEADS, HEAD_DIM, HEAD_DIM), F32)],
        compiler_params=_params("arbitrary"),
        ride=ride, first=lambda: pl.program_id(0) == 0, last=lambda: pl.program_id(0) == nc - 1)


def _gdn_scan_bwd(qn, kn, v, bg, saved, inverses, do, name, ride=None):
    n = qn.shape[0]
    nch = GDN_STEP_CHUNKS
    rows = nch * CHUNK
    nc = n // rows

    def body(q_ref, k_ref, v_ref, bg_ref, saved_ref, inv_ref, do_ref, dq_ref, dk_ref, dv_ref, dbg_ref, dstate_ref):
        @pl.when(pl.program_id(0) == 0)
        def _():
            dstate_ref[...] = jnp.zeros_like(dstate_ref)

        states = tuple(saved_ref[h] for h in range(N_HEADS))
        bgs = tuple(bg_ref[j * CHUNK:(j + 1) * CHUNK, :] for j in range(nch))
        known = tuple(inv_ref[item] for item in range(nch * N_HEADS))
        _, vjp = jax.vjp(functools.partial(_gdn_chunks, inverses=known), _chunk_head_slices(q_ref, nch),
                         _chunk_head_slices(k_ref, nch), _chunk_head_slices(v_ref, nch), bgs, states)
        dstates = tuple(dstate_ref[h] for h in range(N_HEADS))
        dqs, dks, dvs, dbgs, dprev = vjp((_chunk_head_slices(do_ref, nch), dstates))
        _store_chunk_heads(dq_ref, dqs)
        _store_chunk_heads(dk_ref, dks)
        _store_chunk_heads(dv_ref, dvs)
        for h in range(N_HEADS):
            dstate_ref[h] = dprev[h]
        for j in range(nch):
            dbg_ref[j * CHUNK:(j + 1) * CHUNK, :] = dbgs[j]

    row = lambda i: (nc - 1 - i, 0)
    return _call(
        body, [qn, kn, v, bg, saved, inverses, do], name=name, grid=(nc,),
        in_specs=[pl.BlockSpec((rows, D_MODEL), row)] * 3 + [pl.BlockSpec((rows, 128), row),
                  pl.BlockSpec((None, N_HEADS, HEAD_DIM, HEAD_DIM), lambda i: (nc - 1 - i, 0, 0, 0)),
                  pl.BlockSpec((None, nch * N_HEADS, CHUNK, CHUNK), lambda i: (nc - 1 - i, 0, 0, 0)),
                  pl.BlockSpec((rows, D_MODEL), row)],
        out_specs=[pl.BlockSpec((rows, D_MODEL), row)] * 3 + [pl.BlockSpec((rows, 128), row)],
        out_shape=[jax.ShapeDtypeStruct((n, D_MODEL), F32)] * 3 + [jax.ShapeDtypeStruct((n, 128), F32)],
        scratch_shapes=[pltpu.VMEM((N_HEADS, HEAD_DIM, HEAD_DIM), F32)],
        compiler_params=_params("arbitrary"),
        ride=ride, first=lambda: pl.program_id(0) == 0, last=lambda: pl.program_id(0) == nc - 1)


SB_BQ = 512
SB_BK = 256
SB_SUB = 128
SB_ROWS = 128
SB_SCALE = HEAD_DIM ** -0.5
SB_DEAD = -105.0


def _sb_terms(z, before):
    e = jnp.exp(-jnp.abs(z))
    log_beta = jnp.minimum(z, 0.0) - jnp.log(1.0 + e)
    log_1m = log_beta - z
    if before is not None:
        log_1m = jnp.where(before, log_1m, 0.0)
    return e, log_beta, log_1m


def _tri_ones(n, cmp):
    r = lax.broadcasted_iota(jnp.int32, (n, 2 * n), 0)
    c = lax.broadcasted_iota(jnp.int32, (n, 2 * n), 1)
    return jnp.where((c >= n) | cmp(r, c), 1.0, 0.0).astype(BF16)


def _sums(x, tri_ones):
    both = jnp.dot(x.astype(BF16), tri_ones, preferred_element_type=F32)
    n = x.shape[1]
    return both[:, :n], both[:, n:]


def _sb_chunk_mask(diagonal, r, s):
    if not diagonal or s * SB_SUB + SB_SUB - 1 < r * SB_ROWS:
        return None
    if s * SB_SUB >= r * SB_ROWS + SB_ROWS - 1:
        return "empty"
    rows = r * SB_ROWS + lax.broadcasted_iota(jnp.int32, (SB_ROWS, SB_SUB), 0)
    cols = s * SB_SUB + lax.broadcasted_iota(jnp.int32, (SB_ROWS, SB_SUB), 1)
    return cols < rows


def _sb_attention(q, kv, name, bq=SB_BQ, ride=None):
    n = q.shape[0]
    bq = min(bq, n)
    bk = min(SB_BK, bq)
    nrc = bq // SB_ROWS

    def body(q_ref, k_ref, v_ref, o_ref, l_ref, n_ref, z_scr, a_scr):
        i = pl.program_id(1)
        qb = q_ref[...]
        after = _tri_ones(SB_SUB, lambda r, c: r > c)

        def span(start, kw, c_sum, acc, diagonal):
            start = pl.multiple_of(start, kw)
            k_w = k_ref[pl.ds(start, kw), :]
            v_w = v_ref[pl.ds(start, kw), :]
            z_scr[:, :kw] = lax.dot_general(qb, k_w, _dims(False, True), preferred_element_type=F32)
            c_rows = [c_sum[r * SB_ROWS:(r + 1) * SB_ROWS] for r in range(nrc)]
            for s in reversed(range(kw // SB_SUB)):
                cols = slice(s * SB_SUB, (s + 1) * SB_SUB)
                for r in range(nrc):
                    rows = slice(r * SB_ROWS, (r + 1) * SB_ROWS)
                    before = _sb_chunk_mask(diagonal, r, s)
                    if isinstance(before, str):
                        a_scr[rows, cols] = jnp.zeros((SB_ROWS, SB_SUB), BF16)
                        continue
                    _, log_beta, log_1m = _sb_terms(z_scr[rows, cols] * SB_SCALE, before)
                    tail, total = _sums(log_1m, after)
                    a = jnp.exp(log_beta + c_rows[r] + tail)
                    if before is not None:
                        a = jnp.where(before, a, 0.0)
                    a_scr[rows, cols] = a.astype(BF16)
                    c_rows[r] = c_rows[r] + total
            acc = acc + jnp.dot(a_scr[:, :kw], v_w, preferred_element_type=F32)
            return jnp.concatenate(c_rows, axis=0), acc

        c_sum, acc = span(i * bq, bq, jnp.zeros((bq, SB_SUB), F32), jnp.zeros((bq, HEAD_DIM), F32), True)

        def more(state):
            done, c, _ = state
            return (done < i * (bq // bk)) & (jnp.max(c) > SB_DEAD)

        def step(state):
            done, c, a = state
            c, a = span(i * bq - (done + 1) * bk, bk, c, a, False)
            return done + 1, c, a

        done, c_sum, acc = lax.while_loop(more, step, (jnp.int32(0), c_sum, acc))
        o_ref[...] = acc.astype(BF16)
        l_ref[...] = c_sum
        n_ref[...] = jnp.full((8, 128), done.astype(F32), F32)

    nq = n // bq
    return _call(
        body, [q, kv, kv], name=name, grid=(N_HEADS, nq),
        in_specs=[pl.BlockSpec((bq, HEAD_DIM), lambda h, i: (i, h)),
                  pl.BlockSpec((n, HEAD_DIM), lambda h, i: (0, h)),
                  pl.BlockSpec((n, HEAD_DIM), lambda h, i: (0, N_HEADS + h))],
        out_specs=[pl.BlockSpec((bq, HEAD_DIM), lambda h, i: (i, h)),
                   pl.BlockSpec((None, bq, 128), lambda h, i: (h, i, 0)),
                   pl.BlockSpec((None, None, 8, 128), lambda h, i: (h, i, 0, 0))],
        out_shape=[jax.ShapeDtypeStruct((n, D_MODEL), BF16), jax.ShapeDtypeStruct((N_HEADS, n, 128), F32),
                   jax.ShapeDtypeStruct((N_HEADS, nq, 8, 128), F32)],
        scratch_shapes=[pltpu.VMEM((bq, bq), F32), pltpu.VMEM((bq, bq), BF16)],
        compiler_params=_params("arbitrary", "arbitrary"),
        ride=ride, first=lambda: (pl.program_id(0) == 0) & (pl.program_id(1) == 0),
        last=lambda: (pl.program_id(0) == N_HEADS - 1) & (pl.program_id(1) == nq - 1))


def _sb_attention_bwd(q, kv, do, lsum, spans, name, bq=SB_BQ, ride=None):
    n = q.shape[0]
    bq = min(bq, n)
    bk = min(SB_BK, bq)
    nrc = bq // SB_ROWS

    def body(q_ref, k_ref, v_ref, do_ref, l_ref, n_ref, dq_ref, dk_out, dv_out, z_scr, da_scr, a_scr, dz_scr, dk_ref, dv_ref):
        i = pl.program_id(1)

        @pl.when(i == 0)
        def _():
            dk_ref[...] = jnp.zeros_like(dk_ref)
            dv_ref[...] = jnp.zeros_like(dv_ref)

        qb = q_ref[...]
        dob = do_ref[...]
        lt_rows = [l_ref[r * SB_ROWS:(r + 1) * SB_ROWS, :] for r in range(nrc)]
        upto = _tri_ones(SB_SUB, lambda r, c: r <= c)
        below = _tri_ones(SB_SUB, lambda r, c: r < c)

        def span(start, kw, l_sum, g_sum, dq, diagonal):
            start = pl.multiple_of(start, kw)
            k_w = k_ref[pl.ds(start, kw), :]
            v_w = v_ref[pl.ds(start, kw), :]
            z_scr[:, :kw] = lax.dot_general(qb, k_w, _dims(False, True), preferred_element_type=F32)
            da_scr[:, :kw] = lax.dot_general(dob, v_w, _dims(False, True), preferred_element_type=F32)
            l_rows = [l_sum[r * SB_ROWS:(r + 1) * SB_ROWS] for r in range(nrc)]
            g_rows = [g_sum[r * SB_ROWS:(r + 1) * SB_ROWS] for r in range(nrc)]
            for s in range(kw // SB_SUB):
                cols = slice(s * SB_SUB, (s + 1) * SB_SUB)
                for r in range(nrc):
                    rows = slice(r * SB_ROWS, (r + 1) * SB_ROWS)
                    before = _sb_chunk_mask(diagonal, r, s)
                    if isinstance(before, str):
                        a_scr[rows, cols] = jnp.zeros((SB_ROWS, SB_SUB), BF16)
                        dz_scr[rows, cols] = jnp.zeros((SB_ROWS, SB_SUB), BF16)
                        continue
                    zs = z_scr[rows, cols] * SB_SCALE
                    e, log_beta, log_1m = _sb_terms(zs, before)
                    l_prefix, l_total = _sums(log_1m, upto)
                    a = jnp.exp(log_beta + (lt_rows[r] - (l_rows[r] + l_prefix)))
                    if before is not None:
                        a = jnp.where(before, a, 0.0)
                    g = a * da_scr[rows, cols]
                    g_prefix, g_total = _sums(g, below)
                    inv = 1.0 / (1.0 + e)
                    beta = jnp.where(zs >= 0, inv, e * inv)
                    dz = (g - (g + g_rows[r] + g_prefix) * beta) * SB_SCALE
                    if before is not None:
                        dz = jnp.where(before, dz, 0.0)
                    a_scr[rows, cols] = a.astype(BF16)
                    dz_scr[rows, cols] = dz.astype(BF16)
                    l_rows[r] = l_rows[r] + l_total
                    g_rows[r] = g_rows[r] + g_total
            dz_w = dz_scr[:, :kw]
            dq = dq + jnp.dot(dz_w, k_w, preferred_element_type=F32)
            dk_ref[pl.ds(start, kw), :] += lax.dot_general(dz_w, qb, _dims(True, False), preferred_element_type=F32)
            dv_ref[pl.ds(start, kw), :] += lax.dot_general(a_scr[:, :kw], dob, _dims(True, False),
                                                           preferred_element_type=F32)
            return jnp.concatenate(l_rows, axis=0), jnp.concatenate(g_rows, axis=0), dq

        taken = jnp.clip(jnp.max(n_ref[...]).astype(jnp.int32), 0, i * (bq // bk))
        zero = jnp.zeros((bq, SB_SUB), F32)
        carry = lax.fori_loop(0, taken, lambda j, c: span(i * bq - (taken - j) * bk, bk, c[0], c[1], c[2], False),
                              (zero, zero, jnp.zeros((bq, HEAD_DIM), F32)))
        _, _, dq = span(i * bq, bq, carry[0], carry[1], carry[2], True)
        dq_ref[...] = dq.astype(BF16)

        @pl.when(i == nq - 1)
        def _():
            dk_out[...] = dk_ref[...].astype(BF16)
            dv_out[...] = dv_ref[...].astype(BF16)

    nq = n // bq
    return _call(
        body, [q, kv, kv, do, lsum, spans], name=name, grid=(N_HEADS, nq),
        in_specs=[pl.BlockSpec((bq, HEAD_DIM), lambda h, i: (i, h)),
                  pl.BlockSpec((n, HEAD_DIM), lambda h, i: (0, h)),
                  pl.BlockSpec((n, HEAD_DIM), lambda h, i: (0, N_HEADS + h)),
                  pl.BlockSpec((bq, HEAD_DIM), lambda h, i: (i, h)),
                  pl.BlockSpec((None, bq, 128), lambda h, i: (h, i, 0)),
                  pl.BlockSpec((None, None, 8, 128), lambda h, i: (h, i, 0, 0))],
        out_specs=[pl.BlockSpec((bq, HEAD_DIM), lambda h, i: (i, h)),
                   pl.BlockSpec((n, HEAD_DIM), lambda h, i: (0, h)),
                   pl.BlockSpec((n, HEAD_DIM), lambda h, i: (0, h))],
        out_shape=[jax.ShapeDtypeStruct((n, D_MODEL), BF16)] * 3,
        scratch_shapes=[pltpu.VMEM((bq, bq), F32), pltpu.VMEM((bq, bq), F32), pltpu.VMEM((bq, bq), BF16),
                        pltpu.VMEM((bq, bq), BF16), pltpu.VMEM((n, HEAD_DIM), F32), pltpu.VMEM((n, HEAD_DIM), F32)],
        compiler_params=_params("arbitrary", "arbitrary"),
        ride=ride, first=lambda: (pl.program_id(0) == 0) & (pl.program_id(1) == 0),
        last=lambda: (pl.program_id(0) == N_HEADS - 1) & (pl.program_id(1) == nq - 1))


def _local_step(x, target, gains, comm):
    g = gains
    w = comm.w
    big = {}
    row = lambda a, i: a[i:i + 1, :]

    proj, h0 = _norm_matmul(x, row(g["mix_pre"], 0), w["gdn_in"], F32, "gdn_in_proj", tn=1408)
    qn, kn, v, bg = _gdn_pre(proj, w["conv"], g["alog"], g["dtb"], "gdn_pre")
    (o_gdn, saved, inverses), got = _gdn_scan(qn, kn, v, bg, "gdn_scan", ride=comm.ride("scan"))
    comm.done("scan", got)
    x1, y_mix0, a_gdn = _out_proj((o_gdn, proj), True, x, w["gdn_out"], row(g["mix_post"], 0), g["out_gain"],
                                  "gdn_out_proj", tm=512)
    ride = comm.ride("mlp0")
    res = _mlp_fwd(x1, row(g["mlp_pre"], 0), w["up0"], w["down0"], row(g["mlp_post"], 0), "mlp0", ride=ride)
    (x2, y_mlp0, h_mlp0, u0, a0), got = res if ride is not None else (res, [])
    comm.done("mlp0", got)
    kv, h_kv = _norm_matmul(x2, g["kv"], w["kv"], BF16, "kv_proj", tm=2048)
    q, h_q = _norm_matmul(x2, row(g["mix_pre"], 1), w["sb_q"], BF16, "sb_q_proj")
    (o_sb, lsum, spans), got = _sb_attention(q, kv, "sb_attention", ride=comm.ride("sb"))
    comm.done("sb", got)
    x3, y_mix1 = _out_proj((o_sb,), False, x2, w["sb_o"], row(g["mix_post"], 1), None, "sb_out_proj")
    dx4, y_mlp1, h_mlp1, u1, a1, loss = _mlp_fwd(x3, row(g["mlp_pre"], 1), w["up1"], w["down1"], row(g["mlp_post"], 1),
                                                 "mlp1_loss", target=target)

    dx3, dy_mlp1, du1, dg_mlp1 = _mlp_bwd(dx4, y_mlp1, row(g["mlp_post"], 1), w["down1"], u1, w["up1"], x3,
                                          row(g["mlp_pre"], 1), "mlp1_bwd")
    big["down1"] = _matmul_tn(a1, dy_mlp1, 1, "d_down1")[0]
    big["up1"] = _matmul_tn(h_mlp1, du1, N_DEV, "d_up1", tk=2048)
    dy_mix1, dg_post1, do_sb = _out_proj_bwd(dx3, y_mix1, w["sb_o"], row(g["mix_post"], 1), None, "sb_out_proj_bwd")
    big["sb_o"] = _matmul_tn(o_sb, dy_mix1, 1, "d_sb_o")[0]
    (dq, dk, dv), got = _sb_attention_bwd(q, kv, do_sb, lsum, spans, "sb_attention_bwd", ride=comm.ride("sb_bwd", big))
    comm.done("sb_bwd", got)
    big["sb_q"] = _matmul_tn(h_q, dq, 1, "d_sb_q")[0]
    big["kv"] = jnp.concatenate([_matmul_tn(h_kv, dk, N_DEV // 2, "d_w_k", tk=2048),
                                 _matmul_tn(h_kv, dv, N_DEV // 2, "d_w_v", tk=2048)],
                                axis=0)
    dx2, dg_x2 = _norm_matmul_bwd([(dq, w["sb_q"], row(g["mix_pre"], 1), 1024), (dk, w["kv"][0:1], g["kv"], 1024),
                                   (dv, w["kv"][1:2], g["kv"], 1024)], x2, dx3, "qkv_proj_bwd", tm=512)

    ride = comm.ride("mlp0_bwd", big)
    res = _mlp_bwd(dx2, y_mlp0, row(g["mlp_post"], 0), w["down0"], u0, w["up0"], x1, row(g["mlp_pre"], 0), "mlp0_bwd",
                   ride=ride)
    (dx1, dy_mlp0, du0, dg_mlp0), got = res if ride is not None else (res, [])
    comm.done("mlp0_bwd", got)
    big["down0"] = _matmul_tn(a0, dy_mlp0, 1, "d_down0")[0]
    big["up0"] = _matmul_tn(h_mlp0, du0, N_DEV, "d_up0", tk=2048)
    dy_mix0, dg_post0, do_gdn, dgate, d_out_gain = _out_proj_bwd(
        dx1, y_mix0, w["gdn_out"], row(g["mix_post"], 0), (o_gdn, proj, g["out_gain"]), "gdn_out_proj_bwd", tm=512)
    big["gdn_out"] = _matmul_tn(a_gdn, dy_mix0, 1, "d_gdn_out")[0]
    (dqn, dkn, dvv, dbg), got = _gdn_scan_bwd(qn, kn, v, bg, saved, inverses, do_gdn, "gdn_scan_bwd",
                                              ride=comm.ride("scan_bwd", big))
    comm.done("scan_bwd", got)
    dproj, d_conv, d_gates = _gdn_pre_bwd(proj, w["conv"], g["alog"], g["dtb"], dqn, dkn, dvv, dbg, dgate, "gdn_pre_bwd")
    big["gdn_in"] = _matmul_tn(h0, dproj, 1, "d_gdn_in", tb=1408)[0]
    ride = comm.ride("in_bwd", big)
    grad_x, dg_pre0, *got = _norm_matmul_bwd([(dproj, w["gdn_in"], row(g["mix_pre"], 0), 1408)], x, dx1, "gdn_in_proj_bwd",
                                             ride=ride)
    comm.done("in_bwd", got[0] if got else [])

    small = {"pre0": dg_pre0, "x2": dg_x2, "post0": dg_post0, "post1": dg_post1, "mlp0": dg_mlp0, "mlp1": dg_mlp1,
             "gates": d_gates, "out_gain": d_out_gain, "conv": d_conv, "loss": loss}
    return grad_x, big, small


GATHER_STAGES = {"scan": ("gdn_out", "up0", "down0"), "mlp0": ("kv", "sb_q"), "sb": ("sb_o", "up1", "down1")}
SCATTER_STAGES = {"sb_bwd": ("up1", "down1", "sb_o"), "mlp0_bwd": ("sb_q", "kv"), "scan_bwd": ("up0", "down0", "gdn_out"),
                  "in_bwd": ("gdn_in",)}


def _whole_weight(name, gathered):
    d = D_MODEL
    if name in ("up0", "up1"):
        return gathered
    if name == "kv":
        return gathered.reshape(2, N_DEV // 2, d, gathered.shape[2]).transpose(0, 2, 1, 3).reshape(2, d, d)
    if name == "gdn_in":
        whole = gathered.transpose(1, 0, 2).reshape(d, GDN_IN_COLS)
        return jnp.pad(whole, ((0, 0), (0, GDN_IN_PAD - GDN_IN_COLS)))[None]
    if name == "conv":
        return gathered.transpose(1, 0, 2).reshape(CONV_WIDTH, QKV)
    whole = gathered.reshape(gathered.shape[0] * gathered.shape[1], d)
    return whole[None] if name == "sb_q" else whole


def _owner_blocks(name, grad):
    if name in ("up0", "up1", "kv"):
        blocks = grad
    elif name == "gdn_in":
        blocks = grad[:, :GDN_IN_COLS].reshape(D_MODEL, N_DEV, GDN_IN_COLS // N_DEV).transpose(1, 0, 2)
    else:
        blocks = grad.reshape(N_DEV, grad.shape[0] // N_DEV, grad.shape[1])
    return blocks.astype(BF16)


class _Fsdp:
    def __init__(self, shards, weights):
        self.shards, self.w, self.landed = shards, weights, {}

    def ride(self, stage, grads=None):
        if stage in GATHER_STAGES:
            names = GATHER_STAGES[stage]
            return _Exchange([self.shards[nm] for nm in names], [True] * len(names))
        names = SCATTER_STAGES[stage]
        return _Exchange([_owner_blocks(nm, grads[nm]) for nm in names], [False] * len(names))

    def done(self, stage, outs):
        if stage in GATHER_STAGES:
            for nm, out in zip(GATHER_STAGES[stage], outs):
                self.w[nm] = _whole_weight(nm, out)
        else:
            self.landed.update(zip(SCATTER_STAGES[stage], outs))


def _my_place():
    return lax.axis_index("x"), lax.axis_index("y"), lax.axis_index("c")


class _Exchange:
    def __init__(self, arrays, gather):
        self.arrays, self.gather, self.n = list(arrays), list(gather), len(arrays)
        any_spec = pl.BlockSpec(memory_space=pl.ANY)
        self.in_specs = [any_spec] * self.n
        self.out_specs = [any_spec] * self.n
        self.out_shape = [jax.ShapeDtypeStruct(((N_DEV,) + a.shape) if g else a.shape, a.dtype)
                          for a, g in zip(self.arrays, self.gather)]
        self.scratch = [pltpu.SemaphoreType.DMA((self.n, N_DEV - 1)), pltpu.SemaphoreType.DMA((self.n, N_DEV - 1)),
                        pltpu.SemaphoreType.DMA((self.n,))]

    def _copies(self, ins, outs, sems):
        send_sems, recv_sems, local_sems = sems
        x, y, c = _my_place()
        me = 4 * x + 2 * y + c
        copies = []
        for a in range(self.n):
            src = ins[a] if self.gather[a] else ins[a].at[me]
            copies.append(pltpu.make_async_copy(src, outs[a].at[me], local_sems.at[a]))
        for k in range(1, N_DEV):
            px = 1 - x if k & 4 else x
            py = 1 - y if k & 2 else y
            pc = 1 - c if k & 1 else c
            peer = 4 * px + 2 * py + pc
            for a in range(self.n):
                src = ins[a] if self.gather[a] else ins[a].at[peer]
                copies.append(pltpu.make_async_remote_copy(
                    src_ref=src, dst_ref=outs[a].at[me], send_sem=send_sems.at[a, k - 1], recv_sem=recv_sems.at[a, k - 1],
                    device_id=(px, py, pc), device_id_type=MESH))
        return copies

    def start(self, ins, outs, sems):
        for cp in self._copies(ins, outs, sems):
            cp.start()

    def wait(self, ins, outs, sems):
        for cp in self._copies(ins, outs, sems):
            cp.wait()


def _call(body, operands, *, name, out_shape, in_specs, out_specs, grid=(), scratch_shapes=(), compiler_params=None,
          ride=None, first=None, last=None):
    n_in, n_out, n_scr = len(operands), len(out_shape), len(scratch_shapes)
    if ride is None:
        outs = pl.pallas_call(body, name=name, grid=grid, in_specs=in_specs, out_specs=out_specs, out_shape=out_shape,
                              scratch_shapes=scratch_shapes, compiler_params=compiler_params)(*operands)
        return list(outs), []
    r = ride.n

    def riding(*refs):
        ins, r_ins = refs[:n_in], refs[n_in:n_in + r]
        outs, r_outs = refs[n_in + r:n_in + r + n_out], refs[n_in + r + n_out:n_in + 2 * r + n_out]
        scr, sems = refs[n_in + 2 * r + n_out:n_in + 2 * r + n_out + n_scr], refs[n_in + 2 * r + n_out + n_scr:]

        @pl.when(first())
        def _():
            ride.start(r_ins, r_outs, sems)

        body(*ins, *outs, *scr)

        @pl.when(last())
        def _():
            ride.wait(r_ins, r_outs, sems)

    outs = pl.pallas_call(
        riding, name=name, grid=grid, in_specs=list(in_specs) + ride.in_specs, out_specs=list(out_specs) + ride.out_specs,
        out_shape=list(out_shape) + ride.out_shape, scratch_shapes=list(scratch_shapes) + ride.scratch,
        compiler_params=compiler_params)(*operands, *ride.arrays)
    return list(outs[:n_out]), list(outs[n_out:])


def _exchange(arrays, gather, name):
    ride = _Exchange(arrays, gather)

    def body(*refs):
        ins, outs, sems = refs[:ride.n], refs[ride.n:2 * ride.n], refs[2 * ride.n:]
        ride.start(ins, outs, sems)
        ride.wait(ins, outs, sems)

    return pl.pallas_call(body, name=name, in_specs=ride.in_specs, out_specs=ride.out_specs, out_shape=ride.out_shape,
                          scratch_shapes=ride.scratch)(*arrays)


def _gather_two_level(arrays, name):
    n_arr = len(arrays)

    def body(*refs):
        ins, outs = refs[:n_arr], refs[n_arr:2 * n_arr]
        send_sems, recv_sems, local_sems = refs[2 * n_arr:]
        x, y, c = _my_place()
        sibling = (x, y, 1 - c)
        chips = [(1 - x, y), (x, 1 - y), (1 - x, 1 - y)]
        index = lambda px, py, pc: 4 * px + 2 * py + pc

        def copy(a, k, block, to, src=None):
            dst = outs[a].at[index(*block)]
            return pltpu.make_async_remote_copy(src_ref=dst if src is None else src, dst_ref=dst,
                                                send_sem=send_sems.at[a, k], recv_sem=recv_sems.at[a, k],
                                                device_id=to, device_id_type=MESH)

        mine = [pltpu.make_async_copy(ins[a], outs[a].at[index(x, y, c)], local_sems.at[a]) for a in range(n_arr)]
        first = [copy(a, 0, (x, y, c), sibling, src=ins[a]) for a in range(n_arr)]
        first += [copy(a, 1 + j, (x, y, c), (*chip, c), src=ins[a]) for j, chip in enumerate(chips) for a in range(n_arr)]
        for cp in mine + first:
            cp.start()
        passed = []
        for j, chip in enumerate(chips):
            for a in range(n_arr):
                copy(a, 1 + j, (*chip, c), (x, y, c)).wait_recv()
                passed.append(copy(a, 4 + j, (*chip, c), sibling))
                passed[-1].start()
        for a in range(n_arr):
            copy(a, 0, sibling, (x, y, c)).wait_recv()
            for j, chip in enumerate(chips):
                copy(a, 4 + j, (*chip, 1 - c), (x, y, c)).wait_recv()
        for cp in first + passed:
            cp.wait_send()
        for cp in mine:
            cp.wait()

    any_spec = pl.BlockSpec(memory_space=pl.ANY)
    return pl.pallas_call(
        body, name=name, in_specs=[any_spec] * n_arr, out_specs=[any_spec] * n_arr,
        out_shape=[jax.ShapeDtypeStruct((N_DEV,) + a.shape, a.dtype) for a in arrays],
        scratch_shapes=[pltpu.SemaphoreType.DMA((n_arr, N_DEV - 1)), pltpu.SemaphoreType.DMA((n_arr, N_DEV - 1)),
                        pltpu.SemaphoreType.DMA((n_arr,))])(*arrays)


def _adamw_math(g, w, m, v):
    m = ADAM_B1 * m + (1.0 - ADAM_B1) * g
    v = ADAM_B2 * v + (1.0 - ADAM_B2) * jnp.square(g)
    m_hat = m / (1.0 - ADAM_B1 ** ADAM_STEP)
    v_hat = v / (1.0 - ADAM_B2 ** ADAM_STEP)
    delta = -ADAM_LR * (m_hat / (jnp.sqrt(v_hat) + ADAM_EPS) + ADAM_WD * w)
    return delta, m, v


def _sum_devices(ref):
    total = ref[0].astype(F32)
    for d in range(1, N_DEV):
        total = total + ref[d].astype(F32)
    return total


def _reduce_adamw(landed, w, m, v, name, tr=256):
    r, c = w.shape
    tr = min(tr, r)
    assert r % tr == 0

    def body(l_ref, w_ref, m_ref, v_ref, g_ref, d_ref, nm_ref, nv_ref):
        g = _sum_devices(l_ref)
        g_ref[...] = g
        d_ref[...], nm_ref[...], nv_ref[...] = _adamw_math(g, w_ref[...], m_ref[...], v_ref[...])

    blk = pl.BlockSpec((tr, c), lambda i: (i, 0))
    return pl.pallas_call(
        body, name=name, grid=(r // tr,),
        in_specs=[pl.BlockSpec((N_DEV, tr, c), lambda i: (0, i, 0)), blk, blk, blk],
        out_specs=[blk] * 4, out_shape=[jax.ShapeDtypeStruct((r, c), F32)] * 4,
        compiler_params=_params("parallel"),
    )(landed, w, m, v)


def _reduce_adamw_layers(landed, w, m, v, name, tr=256):
    _, r, c = w.shape
    tr = min(tr, r)
    nr = r // tr
    assert r % tr == 0

    def body(l0_ref, l1_ref, w_ref, m_ref, v_ref, g_ref, d_ref, nm_ref, nv_ref):
        g = jnp.where(pl.program_id(0) == 0, _sum_devices(l0_ref), _sum_devices(l1_ref))
        g_ref[...] = g
        d_ref[...], nm_ref[...], nv_ref[...] = _adamw_math(g, w_ref[...], m_ref[...], v_ref[...])

    blk = pl.BlockSpec((None, tr, c), lambda l, i: (l, i, 0))
    return pl.pallas_call(
        body, name=name, grid=(2, nr),
        in_specs=[pl.BlockSpec((N_DEV, tr, c), lambda l, i: (0, jnp.where(l == 0, i, nr - 1), 0)),
                  pl.BlockSpec((N_DEV, tr, c), lambda l, i: (0, jnp.where(l == 1, i, 0), 0)), blk, blk, blk],
        out_specs=[blk] * 4, out_shape=[jax.ShapeDtypeStruct(w.shape, F32)] * 4,
        compiler_params=_params("arbitrary", "arbitrary"),
    )(landed[0], landed[1], w, m, v)


def _adamw(g, w, m, v, name):
    def body(g_ref, w_ref, m_ref, v_ref, d_ref, nm_ref, nv_ref):
        d_ref[...], nm_ref[...], nv_ref[...] = _adamw_math(g_ref[...], w_ref[...], m_ref[...], v_ref[...])

    return pl.pallas_call(body, name=name, out_shape=[jax.ShapeDtypeStruct(w.shape, F32)] * 3)(g, w, m, v)


def _small_update(landed, params, name):
    layout = {
        "mix_pre": [("pre0", 0), ("x2", 0)], "mix_post": [("post0", 0), ("post1", 0)],
        "mlp_pre": [("mlp0", 0), ("mlp1", 0)], "mlp_post": [("mlp0", 1), ("mlp1", 1)],
        "kv": [("kv", 0)], "alog": [("gates", 0)], "dtb": [("gates", 1)], "out_gain": [("out_gain", 0)],
    }
    landed_names = sorted(landed)
    param_names = sorted(params)
    n_l, n_p = len(landed_names), len(param_names)

    def body(*refs):
        l_refs = dict(zip(landed_names, refs[:n_l]))
        p_refs = {p: refs[n_l + 3 * i:n_l + 3 * i + 3] for i, p in enumerate(param_names)}
        outs = refs[n_l + 3 * n_p:]
        o_refs = {p: outs[4 * i:4 * i + 4] for i, p in enumerate(param_names)}
        conv_ref, loss_ref = outs[4 * n_p:]
        sums = {nm: _sum_devices(l_refs[nm]) for nm in landed_names}
        sums["kv"] = sums["x2"][1:2, :] + sums["x2"][2:3, :]
        for p in param_names:
            w_ref, m_ref, v_ref = p_refs[p]
            g_ref, d_ref, nm_ref, nv_ref = o_refs[p]
            for r, (src, src_row) in enumerate(layout[p]):
                g = sums[src][src_row:src_row + 1, :]
                g_ref[r:r + 1, :] = g
                d, nm, nv = _adamw_math(g, w_ref[r:r + 1, :], m_ref[r:r + 1, :], v_ref[r:r + 1, :])
                d_ref[r:r + 1, :] = d
                nm_ref[r:r + 1, :] = nm
                nv_ref[r:r + 1, :] = nv
        conv_ref[...] = sums["conv"]
        loss_ref[...] = sums["loss"]

    args = [landed[nm] for nm in landed_names]
    out_shape = []
    for p in param_names:
        args += list(params[p])
        out_shape += [jax.ShapeDtypeStruct(params[p][0].shape, F32)] * 4
    out_shape += [jax.ShapeDtypeStruct(landed["conv"].shape[1:], F32), jax.ShapeDtypeStruct(landed["loss"].shape[1:], F32)]
    outs = pl.pallas_call(body, name=name, out_shape=out_shape)(*args)
    result = {p: tuple(outs[4 * i:4 * i + 4]) for i, p in enumerate(param_names)}
    result["conv"], result["loss"] = outs[4 * n_p], outs[4 * n_p + 1]
    return result


def _lanes(vec, offset):
    return jnp.pad(vec[None, :], ((0, 0), (offset, 128 - offset - vec.shape[0])))


def kernel(x, mix_pre_gain, mix_post_gain, mlp_pre_gain, mlp_post_gain, mlp_w_up, mlp_w_down, gdn_w_in, gdn_conv_w, gdn_a_log, gdn_dt_bias, gdn_out_gain, gdn_w_out, kv_gain, w_kv, sb_w_q, sb_w_o, loss_target, m_mix_pre_gain, m_mix_post_gain, m_mlp_pre_gain, m_mlp_post_gain, m_mlp_w_up, m_mlp_w_down, m_gdn_w_in, m_gdn_conv_w, m_gdn_a_log, m_gdn_dt_bias, m_gdn_out_gain, m_gdn_w_out, m_kv_gain, m_w_kv, m_sb_w_q, m_sb_w_o, v_mix_pre_gain, v_mix_post_gain, v_mlp_pre_gain, v_mlp_post_gain, v_mlp_w_up, v_mlp_w_down, v_gdn_w_in, v_gdn_conv_w, v_gdn_a_log, v_gdn_dt_bias, v_gdn_out_gain, v_gdn_w_out, v_kv_gain, v_w_kv, v_sb_w_q, v_sb_w_o):
    me = 4 * lax.axis_index("x") + 2 * lax.axis_index("y") + lax.axis_index("c")
    bf = lambda a: a.astype(BF16)

    shards = {"up0": bf(mlp_w_up[0]), "up1": bf(mlp_w_up[1]), "down0": bf(mlp_w_down[0]), "down1": bf(mlp_w_down[1]),
              "gdn_out": bf(gdn_w_out[0]), "kv": bf(w_kv), "sb_q": bf(sb_w_q[0]), "sb_o": bf(sb_w_o[0])}
    gdn_in, conv = _gather_two_level([bf(gdn_w_in[0]), gdn_conv_w[0]], "gather_first_weights")
    comm = _Fsdp(shards, {"gdn_in": _whole_weight("gdn_in", gdn_in), "conv": _whole_weight("conv", conv)})
    gains = {"mix_pre": mix_pre_gain, "mix_post": mix_post_gain, "mlp_pre": mlp_pre_gain, "mlp_post": mlp_post_gain,
             "kv": kv_gain[None, :], "alog": _lanes(gdn_a_log[0], N_HEADS), "dtb": _lanes(gdn_dt_bias[0], N_HEADS),
             "out_gain": gdn_out_gain}

    grad_x, big, small = _local_step(x[0], loss_target[0], gains, comm)

    small_names = ["pre0", "x2", "post0", "post1", "mlp0", "mlp1", "gates", "out_gain", "conv", "loss"]
    landed = _exchange([small[nm] for nm in small_names], [True] * len(small_names), "exchange_small_grads")
    small_landed = dict(zip(small_names, landed))

    results = {}
    results["mlp_w_up"] = _reduce_adamw_layers((comm.landed["up0"], comm.landed["up1"]), mlp_w_up, m_mlp_w_up, v_mlp_w_up,
                                               "adamw_mlp_w_up")
    results["mlp_w_down"] = _reduce_adamw_layers((comm.landed["down0"], comm.landed["down1"]), mlp_w_down, m_mlp_w_down,
                                                 v_mlp_w_down, "adamw_mlp_w_down")
    big_params = [("gdn_w_in", "gdn_in", gdn_w_in[0], m_gdn_w_in[0], v_gdn_w_in[0]),
                  ("gdn_w_out", "gdn_out", gdn_w_out[0], m_gdn_w_out[0], v_gdn_w_out[0]),
                  ("w_kv", "kv", w_kv, m_w_kv, v_w_kv), ("sb_w_q", "sb_q", sb_w_q[0], m_sb_w_q[0], v_sb_w_q[0]),
                  ("sb_w_o", "sb_o", sb_w_o[0], m_sb_w_o[0], v_sb_w_o[0])]
    for nm, short, w_, m_, v_ in big_params:
        results[nm] = _reduce_adamw(comm.landed[short], w_, m_, v_, "adamw_" + nm)
    lanes8 = lambda a: _lanes(a[0], N_HEADS)
    small_params = {
        "mix_pre": (mix_pre_gain, m_mix_pre_gain, v_mix_pre_gain), "mix_post": (mix_post_gain, m_mix_post_gain, v_mix_post_gain),
        "mlp_pre": (mlp_pre_gain, m_mlp_pre_gain, v_mlp_pre_gain), "mlp_post": (mlp_post_gain, m_mlp_post_gain, v_mlp_post_gain),
        "kv": (kv_gain[None, :], m_kv_gain[None, :], v_kv_gain[None, :]),
        "alog": (lanes8(gdn_a_log), lanes8(m_gdn_a_log), lanes8(v_gdn_a_log)),
        "dtb": (lanes8(gdn_dt_bias), lanes8(m_gdn_dt_bias), lanes8(v_gdn_dt_bias)),
        "out_gain": (gdn_out_gain, m_gdn_out_gain, v_gdn_out_gain),
    }
    sm = _small_update(small_landed, small_params, "small_update")
    conv_cols = QKV // N_DEV
    g_conv = lax.dynamic_slice(sm["conv"], (0, me * conv_cols), (8, conv_cols))[:CONV_WIDTH]
    conv_res = (g_conv,) + tuple(_adamw(g_conv, gdn_conv_w[0], m_gdn_conv_w[0], v_gdn_conv_w[0], "adamw_conv"))

    lead = lambda t: tuple(a[None] for a in t)
    heads8 = lambda t: tuple(a[:, N_HEADS:2 * N_HEADS] for a in t)
    per_weight = [
        sm["mix_pre"], sm["mix_post"], sm["mlp_pre"], sm["mlp_post"],
        tuple(results["mlp_w_up"]), tuple(results["mlp_w_down"]),
        lead(results["gdn_w_in"]), lead(conv_res), heads8(sm["alog"]), heads8(sm["dtb"]), sm["out_gain"],
        lead(results["gdn_w_out"]), tuple(a[0] for a in sm["kv"]), results["w_kv"], lead(results["sb_w_q"]), lead(results["sb_w_o"]),
    ]
    grads, deltas, new_ms, new_vs = zip(*per_weight)
    return (sm["loss"][0, 0], grad_x[None], *grads, *deltas, *new_ms, *new_vs)
```

```python
import functools
import math

import jax
import jax.numpy as jnp
from jax import lax
from jax.experimental import pallas as pl
from jax.experimental.pallas import tpu as pltpu

F32 = jnp.float32
BF16 = jnp.bfloat16

N_DEV = 8
D_MODEL = 1024
N_HEADS = 8
HEAD_DIM = 128
CHUNK = 64
CONV_WIDTH = 4
GDN_IN_COLS = 4 * D_MODEL + 2 * N_HEADS
GDN_IN_PAD = 4 * D_MODEL + 128
EPS = 1e-6

ADAM_LR = 0.001
ADAM_B1 = 0.9
ADAM_B2 = 0.999
ADAM_EPS = 1e-08
ADAM_WD = 0.01
ADAM_STEP = 10

VMEM_LIMIT_BYTES = 56 * 1024 * 1024
MESH = pl.DeviceIdType.MESH


def _params(*semantics):
    return pltpu.CompilerParams(dimension_semantics=semantics, vmem_limit_bytes=VMEM_LIMIT_BYTES)


def _dims(ta, tb):
    return (((0,) if ta else (1,), (1,) if tb else (0,)), ((), ()))


def _dot(a, b, ta=False, tb=False):
    return lax.dot_general(a.astype(BF16), b.astype(BF16), _dims(ta, tb), preferred_element_type=F32)


def _dot_f32(a, b, ta=False, tb=False):
    return lax.dot_general(a, b, _dims(ta, tb), precision=lax.Precision.HIGHEST, preferred_element_type=F32)


def _dot_3x(a, b, ta=False, tb=False):
    return lax.dot_general(a, b, _dims(ta, tb), precision=lax.Precision.HIGH, preferred_element_type=F32)


def _make_mm(dot):
    @functools.partial(jax.custom_vjp, nondiff_argnums=(2, 3))
    def mm(a, b, ta, tb):
        return dot(a, b, ta, tb)

    def fwd(a, b, ta, tb):
        return dot(a, b, ta, tb), (a, b)

    def bwd(ta, tb, res, g):
        a, b = res
        if not ta and not tb:
            return mm(g, b, False, True), mm(a, g, True, False)
        if not ta and tb:
            return mm(g, b, False, False), mm(g, a, True, False)
        if ta and not tb:
            return mm(b, g, False, True), mm(a, g, False, False)
        raise NotImplementedError

    mm.defvjp(fwd, bwd)
    return mm


_mm = _make_mm(_dot)
_mm_f32 = _make_mm(_dot_f32)


def _rms(x, gain):
    r = lax.rsqrt(jnp.mean(x * x, axis=-1, keepdims=True) + EPS)
    return x * r * gain


def _rms_bwd(x, gain, dy):
    r = lax.rsqrt(jnp.mean(x * x, axis=-1, keepdims=True) + EPS)
    xh = x * r
    dgain = jnp.sum(dy * xh, axis=0, keepdims=True)
    dxh = dy * gain
    dx = r * (dxh - xh * jnp.mean(dxh * xh, axis=-1, keepdims=True))
    return dx, dgain


def _silu(x):
    return x / (1.0 + jnp.exp(-x))


def _acc_rows(ref, first, rows):
    @pl.when(first)
    def _():
        ref[...] = jnp.zeros_like(ref)

    for r, val in enumerate(rows):
        ref[r:r + 1, :] += val


def _norm_matmul(x, gain, w, out_dtype, name, tm=1024, tn=1024):
    n, d = x.shape
    g, _, wc = w.shape
    tm, tn = min(tm, n), min(tn, wc)
    per = wc // tn
    assert n % tm == 0 and wc % tn == 0

    def body(x_ref, gain_ref, w_ref, out_ref, h_ref):
        @pl.when(pl.program_id(1) == 0)
        def _():
            h_ref[...] = _rms(x_ref[...], gain_ref[...]).astype(BF16)

        out_ref[...] = jnp.dot(h_ref[...], w_ref[...], preferred_element_type=F32).astype(out_dtype)

    return pl.pallas_call(
        body, name=name, grid=(n // tm, g * per),
        in_specs=[pl.BlockSpec((tm, d), lambda i, j: (i, 0)),
                  pl.BlockSpec((1, d), lambda i, j: (0, 0)),
                  pl.BlockSpec((None, d, tn), lambda i, j: (j // per, 0, j % per))],
        out_specs=[pl.BlockSpec((tm, tn), lambda i, j: (i, j)),
                   pl.BlockSpec((tm, d), lambda i, j: (i, 0))],
        out_shape=[jax.ShapeDtypeStruct((n, g * wc), out_dtype), jax.ShapeDtypeStruct((n, d), BF16)],
        compiler_params=_params("parallel", "arbitrary"),
    )(x, gain, w)


def _norm_matmul_bwd(branches, x, add, name, tm=1024, ride=None):
    n, d = x.shape
    tm = min(tm, n)
    tns = [min(tn, w.shape[2]) for _, w, _, tn in branches]
    pers = [w.shape[2] // tn for (_, w, _, _), tn in zip(branches, tns)]
    ncols = [w.shape[0] * per for (_, w, _, _), per in zip(branches, pers)]
    offs = [sum(ncols[:b]) for b in range(len(branches))]
    total = sum(ncols)
    nb = len(branches)

    def body(*refs):
        dout_refs, w_refs, gain_refs = refs[:nb], refs[nb:2 * nb], refs[2 * nb:3 * nb]
        x_ref, add_ref, dx_ref, dgain_ref, acc_ref, sum_ref = refs[3 * nb:]
        i, j = pl.program_id(0), pl.program_id(1)

        @pl.when((i == 0) & (j == 0))
        def _():
            dgain_ref[...] = jnp.zeros_like(dgain_ref)

        for b in range(nb):
            first, last = offs[b], offs[b] + ncols[b] - 1

            @pl.when((j >= first) & (j <= last))
            def _(b=b, first=first):
                part = _dot(dout_refs[b][...], w_refs[b][...], tb=True)

                @pl.when(j == first)
                def _():
                    acc_ref[...] = part

                @pl.when(j > first)
                def _():
                    acc_ref[...] += part

            @pl.when(j == last)
            def _(b=b):
                xv = x_ref[...]
                xh = xv * lax.rsqrt(jnp.mean(xv * xv, axis=-1, keepdims=True) + EPS)
                dh = acc_ref[...]
                dgain_ref[b:b + 1, :] += jnp.sum(dh * xh, axis=0, keepdims=True)
                scaled = dh * gain_refs[b][...]
                if b == 0:
                    sum_ref[...] = scaled
                else:
                    sum_ref[...] += scaled

        @pl.when(j == total - 1)
        def _():
            xv = x_ref[...]
            r = lax.rsqrt(jnp.mean(xv * xv, axis=-1, keepdims=True) + EPS)
            xh = xv * r
            dxh = sum_ref[...]
            dx_ref[...] = add_ref[...] + r * (dxh - xh * jnp.mean(dxh * xh, axis=-1, keepdims=True))

    def dout_spec(b):
        return pl.BlockSpec((tm, tns[b]), lambda i, j: (i, jnp.clip(j - offs[b], 0, ncols[b] - 1)))

    def w_spec(b):
        def index(i, j):
            c = jnp.clip(j - offs[b], 0, ncols[b] - 1)
            return (c // pers[b], 0, c % pers[b])
        return pl.BlockSpec((None, d, tns[b]), index)

    row = lambda i, j: (i, 0)
    const = lambda i, j: (0, 0)
    nrow = n // tm
    (dx, dgain), got = _call(
        body, [b[0] for b in branches] + [b[1] for b in branches] + [b[2] for b in branches] + [x, add],
        name=name, grid=(nrow, total),
        in_specs=[dout_spec(b) for b in range(nb)] + [w_spec(b) for b in range(nb)]
                 + [pl.BlockSpec((1, d), const)] * nb + [pl.BlockSpec((tm, d), row), pl.BlockSpec((tm, d), row)],
        out_specs=[pl.BlockSpec((tm, d), row), pl.BlockSpec((8, d), const)],
        out_shape=[jax.ShapeDtypeStruct((n, d), F32), jax.ShapeDtypeStruct((8, d), F32)],
        scratch_shapes=[pltpu.VMEM((tm, d), F32), pltpu.VMEM((tm, d), F32)],
        compiler_params=_params("arbitrary", "arbitrary"),
        ride=ride, first=lambda: (pl.program_id(0) == 0) & (pl.program_id(1) == 0),
        last=lambda: (pl.program_id(0) == nrow - 1) & (pl.program_id(1) == total - 1))
    return (dx, dgain) if ride is None else (dx, dgain, got)


def _matmul_tn(a, b, groups, name, ta=1024, tb=1024, tk=1024):
    n, ka = a.shape
    _, kb = b.shape
    wc = kb // groups
    ta, tb, tk = min(ta, ka), min(tb, wc), min(tk, n)
    per = wc // tb
    nk = n // tk
    assert ka % ta == 0 and wc % tb == 0 and n % tk == 0

    def body(a_ref, b_ref, out_ref, acc_ref):
        k = pl.program_id(2)

        @pl.when(k == 0)
        def _():
            acc_ref[...] = jnp.zeros_like(acc_ref)

        acc_ref[...] += _dot(a_ref[...], b_ref[...], ta=True)

        @pl.when(k == nk - 1)
        def _():
            out_ref[...] = acc_ref[...].astype(BF16)

    return pl.pallas_call(
        body, name=name, grid=(ka // ta, groups * per, nk),
        in_specs=[pl.BlockSpec((tk, ta), lambda i, j, k: (k, i)),
                  pl.BlockSpec((tk, tb), lambda i, j, k: (k, j))],
        out_specs=pl.BlockSpec((None, ta, tb), lambda i, j, k: (j // per, i, j % per)),
        out_shape=jax.ShapeDtypeStruct((groups, ka, wc), BF16),
        scratch_shapes=[pltpu.VMEM((ta, tb), F32)],
        compiler_params=_params("parallel", "parallel", "arbitrary"),
    )(a, b)


def _gated_head_norm(o, gate, out_gain):
    parts = []
    for h in range(N_HEADS):
        sl = slice(h * HEAD_DIM, (h + 1) * HEAD_DIM)
        parts.append(_rms(o[:, sl], out_gain) * _silu(gate[:, sl]))
    return parts


def _out_proj(a_inputs, gated, x_in, w, gain, out_gain, name, tm=1024):
    n, d = x_in.shape
    k = w.shape[0]
    tm = min(tm, n)

    def body(*refs):
        if gated:
            o_ref, gate_ref, og_ref, x_ref, w_ref, gain_ref, xo_ref, y_ref, a_ref = refs
            parts = _gated_head_norm(o_ref[...], gate_ref[...], og_ref[...])
            for h, part in enumerate(parts):
                a_ref[:, h * HEAD_DIM:(h + 1) * HEAD_DIM] = part.astype(BF16)
            a = a_ref[...]
        else:
            a_in_ref, x_ref, w_ref, gain_ref, xo_ref, y_ref = refs
            a = a_in_ref[...]
        y = jnp.dot(a, w_ref[...], preferred_element_type=F32)
        y_ref[...] = y
        xo_ref[...] = x_ref[...] + _rms(y, gain_ref[...])

    row = lambda i: (i, 0)
    const = lambda i: (0, 0)
    if gated:
        a_specs = [pl.BlockSpec((tm, k), row), pl.BlockSpec((tm, D_MODEL), lambda i: (i, 3)),
                   pl.BlockSpec((1, HEAD_DIM), const)]
        a_args = list(a_inputs) + [out_gain]
    else:
        a_specs = [pl.BlockSpec((tm, k), row)]
        a_args = list(a_inputs)
    out_specs = [pl.BlockSpec((tm, d), row), pl.BlockSpec((tm, d), row)]
    out_shape = [jax.ShapeDtypeStruct((n, d), F32), jax.ShapeDtypeStruct((n, d), F32)]
    if gated:
        out_specs.append(pl.BlockSpec((tm, k), row))
        out_shape.append(jax.ShapeDtypeStruct((n, k), BF16))
    return pl.pallas_call(
        body, name=name, grid=(n // tm,),
        in_specs=a_specs + [pl.BlockSpec((tm, d), row), pl.BlockSpec((k, d), const), pl.BlockSpec((1, d), const)],
        out_specs=out_specs, out_shape=out_shape,
        compiler_params=_params("parallel"),
    )(*a_args, x_in, w, gain)


def _out_proj_bwd(dxo, y, w, gain, gated_inputs, name, tm=1024):
    n, d = dxo.shape
    k = w.shape[0]
    tm = min(tm, n)
    gated = gated_inputs is not None

    def body(*refs):
        if gated:
            (dxo_ref, y_ref, w_ref, gain_ref, o_ref, gate_ref, og_ref,
             dy_ref, dgain_ref, do_ref, dgate_ref, dog_ref) = refs
        else:
            dxo_ref, y_ref, w_ref, gain_ref, dy_ref, dgain_ref, da_ref = refs
        first = pl.program_id(0) == 0
        dy, dgain = _rms_bwd(y_ref[...], gain_ref[...], dxo_ref[...])
        dy_ref[...] = dy.astype(BF16)
        _acc_rows(dgain_ref, first, [dgain])
        da = _dot(dy_ref[...], w_ref[...], tb=True)
        if not gated:
            da_ref[...] = da.astype(BF16)
            return
        og = og_ref[...]
        dog = jnp.zeros_like(og)
        for h in range(N_HEADS):
            sl = slice(h * HEAD_DIM, (h + 1) * HEAD_DIM)
            fn = lambda o_h, g_h, gn: _rms(o_h, gn) * _silu(g_h)
            _, vjp = jax.vjp(fn, o_ref[:, sl], gate_ref[:, sl], og)
            do_h, dgate_h, dog_h = vjp(da[:, sl])
            do_ref[:, sl] = do_h
            dgate_ref[:, sl] = dgate_h.astype(BF16)
            dog = dog + dog_h
        _acc_rows(dog_ref, first, [dog])

    row = lambda i: (i, 0)
    const = lambda i: (0, 0)
    in_specs = [pl.BlockSpec((tm, d), row), pl.BlockSpec((tm, d), row), pl.BlockSpec((k, d), const),
                pl.BlockSpec((1, d), const)]
    args = [dxo, y, w, gain]
    out_specs = [pl.BlockSpec((tm, d), row), pl.BlockSpec((8, d), const)]
    out_shape = [jax.ShapeDtypeStruct((n, d), BF16), jax.ShapeDtypeStruct((8, d), F32)]
    if gated:
        in_specs += [pl.BlockSpec((tm, k), row), pl.BlockSpec((tm, D_MODEL), lambda i: (i, 3)),
                     pl.BlockSpec((1, HEAD_DIM), const)]
        args += list(gated_inputs)
        out_specs += [pl.BlockSpec((tm, k), row), pl.BlockSpec((tm, k), row), pl.BlockSpec((8, HEAD_DIM), const)]
        out_shape += [jax.ShapeDtypeStruct((n, k), F32), jax.ShapeDtypeStruct((n, k), BF16),
                      jax.ShapeDtypeStruct((8, HEAD_DIM), F32)]
    else:
        out_specs.append(pl.BlockSpec((tm, k), row))
        out_shape.append(jax.ShapeDtypeStruct((n, k), BF16))
    return pl.pallas_call(
        body, name=name, grid=(n // tm,), in_specs=in_specs, out_specs=out_specs, out_shape=out_shape,
        compiler_params=_params("arbitrary"),
    )(*args)


def _mlp_fwd(x_in, g_pre, w_up, w_down, g_post, name, tm=1024, target=None, ride=None):
    n, d = x_in.shape
    g, _, wc = w_up.shape
    tm = min(tm, n)
    with_loss = target is not None

    def body(*refs):
        if with_loss:
            x_ref, gpre_ref, wup_ref, wdown_ref, gpost_ref, t_ref, xo_ref, y_ref, h_ref, u_ref, a_ref, loss_ref, acc_ref = refs
        else:
            x_ref, gpre_ref, wup_ref, wdown_ref, gpost_ref, xo_ref, y_ref, h_ref, u_ref, a_ref, acc_ref = refs
        i, j = pl.program_id(0), pl.program_id(1)

        @pl.when(j == 0)
        def _():
            h_ref[...] = _rms(x_ref[...], gpre_ref[...]).astype(BF16)
            acc_ref[...] = jnp.zeros_like(acc_ref)

        u = jnp.dot(h_ref[...], wup_ref[...], preferred_element_type=F32).astype(BF16)
        u_ref[...] = u
        a = jnp.square(jnp.maximum(u, 0))
        a_ref[...] = a
        acc_ref[...] += jnp.dot(a, wdown_ref[...], preferred_element_type=F32)

        @pl.when(j == g - 1)
        def _():
            y = acc_ref[...]
            y_ref[...] = y
            xo = x_ref[...] + _rms(y, gpost_ref[...])
            if with_loss:
                err = xo - t_ref[...]
                xo_ref[...] = err * (1.0 / d)
                part = 0.5 * jnp.sum(jnp.mean(err * err, axis=-1, keepdims=True), axis=0, keepdims=True)
                _acc_rows(loss_ref, i == 0, [jnp.broadcast_to(part, (1, 128))])
            else:
                xo_ref[...] = xo

    row = lambda i, j: (i, 0)
    const = lambda i, j: (0, 0)
    in_specs = [pl.BlockSpec((tm, d), row), pl.BlockSpec((1, d), const),
                pl.BlockSpec((None, d, wc), lambda i, j: (j, 0, 0)),
                pl.BlockSpec((wc, d), lambda i, j: (j, 0)), pl.BlockSpec((1, d), const)]
    out_specs = [pl.BlockSpec((tm, d), row), pl.BlockSpec((tm, d), row), pl.BlockSpec((tm, d), row),
                 pl.BlockSpec((tm, wc), lambda i, j: (i, j)), pl.BlockSpec((tm, wc), lambda i, j: (i, j))]
    out_shape = [jax.ShapeDtypeStruct((n, d), F32), jax.ShapeDtypeStruct((n, d), F32),
                 jax.ShapeDtypeStruct((n, d), BF16), jax.ShapeDtypeStruct((n, g * wc), BF16),
                 jax.ShapeDtypeStruct((n, g * wc), BF16)]
    args = [x_in, g_pre, w_up, w_down, g_post]
    if with_loss:
        in_specs.append(pl.BlockSpec((tm, d), row))
        out_specs.append(pl.BlockSpec((8, 128), const))
        out_shape.append(jax.ShapeDtypeStruct((8, 128), F32))
        args.append(target)
    nrow = n // tm
    outs, got = _call(
        body, args, name=name, grid=(nrow, g), in_specs=in_specs, out_specs=out_specs, out_shape=out_shape,
        scratch_shapes=[pltpu.VMEM((tm, d), F32)], compiler_params=_params("arbitrary", "arbitrary"),
        ride=ride, first=lambda: (pl.program_id(0) == 0) & (pl.program_id(1) == 0),
        last=lambda: (pl.program_id(0) == nrow - 1) & (pl.program_id(1) == g - 1))
    return outs if ride is None else (outs, got)


def _mlp_bwd(dxo, y, g_post, w_down, u, w_up, x_in, g_pre, name, tm=1024, ride=None):
    n, d = dxo.shape
    g, _, wc = w_up.shape
    tm = min(tm, n)

    def body(dxo_ref, y_ref, gpost_ref, wdown_ref, u_ref, wup_ref, x_ref, gpre_ref,
             dx_ref, dy_ref, du_ref, dgain_ref, acc_ref, dgpost_ref):
        i, j = pl.program_id(0), pl.program_id(1)

        @pl.when(j == 0)
        def _():
            dy, dgpost = _rms_bwd(y_ref[...], gpost_ref[...], dxo_ref[...])
            dy_ref[...] = dy.astype(BF16)
            dgpost_ref[...] = dgpost
            acc_ref[...] = jnp.zeros_like(acc_ref)

        da = _dot(dy_ref[...], wdown_ref[...], tb=True)
        du = (da * (2.0 * jnp.maximum(u_ref[...], 0).astype(F32))).astype(BF16)
        du_ref[...] = du
        acc_ref[...] += _dot(du, wup_ref[...], tb=True)

        @pl.when(j == g - 1)
        def _():
            dx, dgpre = _rms_bwd(x_ref[...], gpre_ref[...], acc_ref[...])
            dx_ref[...] = dxo_ref[...] + dx
            _acc_rows(dgain_ref, i == 0, [dgpre, dgpost_ref[...]])

    row = lambda i, j: (i, 0)
    const = lambda i, j: (0, 0)
    nrow = n // tm
    outs, got = _call(
        body, [dxo, y, g_post, w_down, u, w_up, x_in, g_pre], name=name, grid=(nrow, g),
        in_specs=[pl.BlockSpec((tm, d), row), pl.BlockSpec((tm, d), row), pl.BlockSpec((1, d), const),
                  pl.BlockSpec((wc, d), lambda i, j: (j, 0)), pl.BlockSpec((tm, wc), lambda i, j: (i, j)),
                  pl.BlockSpec((None, d, wc), lambda i, j: (j, 0, 0)), pl.BlockSpec((tm, d), row),
                  pl.BlockSpec((1, d), const)],
        out_specs=[pl.BlockSpec((tm, d), row), pl.BlockSpec((tm, d), row),
                   pl.BlockSpec((tm, wc), lambda i, j: (i, j)), pl.BlockSpec((8, d), const)],
        out_shape=[jax.ShapeDtypeStruct((n, d), F32), jax.ShapeDtypeStruct((n, d), BF16),
                   jax.ShapeDtypeStruct((n, g * wc), BF16), jax.ShapeDtypeStruct((8, d), F32)],
        scratch_shapes=[pltpu.VMEM((tm, d), F32), pltpu.VMEM((1, d), F32)],
        compiler_params=_params("arbitrary", "arbitrary"),
        ride=ride, first=lambda: (pl.program_id(0) == 0) & (pl.program_id(1) == 0),
        last=lambda: (pl.program_id(0) == nrow - 1) & (pl.program_id(1) == g - 1))
    return outs if ride is None else (outs, got)


QKV = 3 * D_MODEL


def _shifted(x, prev8, s):
    if s == 0:
        return x
    tm = x.shape[0]
    rolled = pltpu.roll(x, s, 0)
    head = pltpu.roll(prev8, s, 0)
    head = jnp.concatenate([head, jnp.zeros((tm - 8, x.shape[1]), x.dtype)], axis=0)
    rows = lax.broadcasted_iota(jnp.int32, x.shape, 0)
    return jnp.where(rows < s, head, rolled)


def _conv(x, prev8, conv_w):
    out = x * conv_w[3:4, :]
    for s in range(1, CONV_WIDTH):
        out = out + _shifted(x, prev8, s) * conv_w[3 - s:4 - s, :]
    return out


def _l2norm(x):
    return x * lax.rsqrt(jnp.sum(x * x, axis=-1, keepdims=True) + EPS)


def _gdn_act_head(cq, ck, cv):
    return _l2norm(_silu(cq)) * (HEAD_DIM ** -0.5), _l2norm(_silu(ck)), _silu(cv)


def _gdn_gates(ba, alog, dtb):
    lane = lax.broadcasted_iota(jnp.int32, ba.shape, 1)
    beta = 1.0 / (1.0 + jnp.exp(-ba))
    t = ba + dtb
    softplus = jnp.maximum(t, 0.0) + jnp.log(1.0 + jnp.exp(-jnp.abs(t)))
    g = -jnp.exp(alog) * softplus
    return jnp.where(lane < N_HEADS, beta, jnp.where(lane < 2 * N_HEADS, g, 0.0))


def _qkv_cols(part, h):
    start = part * D_MODEL + h * HEAD_DIM
    return slice(start, start + HEAD_DIM)


def _prev_rows_spec(tm, cols, colblock, order):
    per = tm // 8
    return pl.BlockSpec((8, cols), lambda i: (jnp.maximum(order(i) * per - 1, 0), colblock))


def _gdn_pre(proj, conv_w, alog, dtb, name, tm=256):
    n = proj.shape[0]

    def body(x_ref, prev_ref, ba_ref, cw_ref, alog_ref, dtb_ref, q_ref, k_ref, v_ref, bg_ref):
        first = pl.program_id(0) == 0
        for h in range(N_HEADS):
            convs = []
            for part in range(3):
                sl = _qkv_cols(part, h)
                prev8 = jnp.where(first, 0.0, prev_ref[:, sl])
                convs.append(_conv(x_ref[:, sl], prev8, cw_ref[:, sl]))
            out = slice(h * HEAD_DIM, (h + 1) * HEAD_DIM)
            q_ref[:, out], k_ref[:, out], v_ref[:, out] = _gdn_act_head(*convs)
        bg_ref[...] = _gdn_gates(ba_ref[...], alog_ref[...], dtb_ref[...])

    row = lambda i: (i, 0)
    const = lambda i: (0, 0)
    ident = lambda i: i
    return pl.pallas_call(
        body, name=name, grid=(n // tm,),
        in_specs=[pl.BlockSpec((tm, QKV), row), _prev_rows_spec(tm, QKV, 0, ident),
                  pl.BlockSpec((tm, 128), lambda i: (i, 4 * D_MODEL // 128)),
                  pl.BlockSpec((CONV_WIDTH, QKV), const), pl.BlockSpec((1, 128), const), pl.BlockSpec((1, 128), const)],
        out_specs=[pl.BlockSpec((tm, D_MODEL), row)] * 3 + [pl.BlockSpec((tm, 128), row)],
        out_shape=[jax.ShapeDtypeStruct((n, D_MODEL), F32)] * 3 + [jax.ShapeDtypeStruct((n, 128), F32)],
        compiler_params=_params("parallel"),
    )(proj, proj, proj, conv_w, alog, dtb)


def _gdn_pre_bwd(proj, conv_w, alog, dtb, dq, dk, dv, dbg, dgate, name, tm=256):
    n = proj.shape[0]
    nt = n // tm

    def body(x_ref, prev_ref, ba_ref, cw_ref, alog_ref, dtb_ref, dq_ref, dk_ref, dv_ref, dbg_ref, dgate_ref,
             dproj_ref, dcw_ref, dgates_ref, carry_ref):
        step = pl.program_id(0)
        tile = nt - 1 - step
        @pl.when(step == 0)
        def _():
            carry_ref[...] = jnp.zeros_like(carry_ref)
            dcw_ref[...] = jnp.zeros_like(dcw_ref)

        rows = lax.broadcasted_iota(jnp.int32, (tm, HEAD_DIM), 0)
        for h in range(N_HEADS):
            cols = [_qkv_cols(part, h) for part in range(3)]
            prevs = [jnp.where(tile == 0, 0.0, prev_ref[:, sl]) for sl in cols]
            convs = [_conv(x_ref[:, sl], prev8, cw_ref[:, sl]) for sl, prev8 in zip(cols, prevs)]
            _, vjp = jax.vjp(_gdn_act_head, *convs)
            out = slice(h * HEAD_DIM, (h + 1) * HEAD_DIM)
            dcs = vjp((dq_ref[:, out], dk_ref[:, out], dv_ref[:, out]))
            for sl, prev8, dc in zip(cols, prevs, dcs):
                x = x_ref[:, sl]
                cw = cw_ref[:, sl]
                dx = dc * cw[3:4, :]
                wrapped = jnp.zeros((8, HEAD_DIM), F32)
                dcw_ref[3:4, sl] += jnp.sum(dc * x, axis=0, keepdims=True)
                for s in range(1, CONV_WIDTH):
                    r = pltpu.roll(dc, tm - s, 0) * cw[3 - s:4 - s, :]
                    dx = dx + jnp.where(rows < tm - s, r, 0.0)
                    wrapped = wrapped + jnp.where(rows[tm - 8:, :] >= tm - s, r[tm - 8:, :], 0.0)
                    dcw_ref[3 - s:4 - s, sl] += jnp.sum(dc * _shifted(x, prev8, s), axis=0, keepdims=True)
                from_next = jnp.concatenate([jnp.zeros((tm - 8, HEAD_DIM), F32), carry_ref[:, sl]], axis=0)
                dproj_ref[:, sl] = (dx + from_next).astype(BF16)
                carry_ref[:, sl] = wrapped

        _, vjp = jax.vjp(_gdn_gates, ba_ref[...], alog_ref[...], dtb_ref[...])
        dba, dalog, ddtb = vjp(dbg_ref[...])
        dproj_ref[:, QKV:4 * D_MODEL] = dgate_ref[...]
        dproj_ref[:, 4 * D_MODEL:] = dba.astype(BF16)
        _acc_rows(dgates_ref, step == 0, [dalog, ddtb])

    rev = lambda i: nt - 1 - i
    row = lambda i: (rev(i), 0)
    const = lambda i: (0, 0)
    return pl.pallas_call(
        body, name=name, grid=(nt,),
        in_specs=[pl.BlockSpec((tm, QKV), row), _prev_rows_spec(tm, QKV, 0, rev),
                  pl.BlockSpec((tm, 128), lambda i: (rev(i), 4 * D_MODEL // 128)),
                  pl.BlockSpec((CONV_WIDTH, QKV), const), pl.BlockSpec((1, 128), const), pl.BlockSpec((1, 128), const),
                  pl.BlockSpec((tm, D_MODEL), row), pl.BlockSpec((tm, D_MODEL), row), pl.BlockSpec((tm, D_MODEL), row),
                  pl.BlockSpec((tm, 128), row), pl.BlockSpec((tm, D_MODEL), row)],
        out_specs=[pl.BlockSpec((tm, GDN_IN_PAD), row), pl.BlockSpec((8, QKV), const), pl.BlockSpec((8, 128), const)],
        out_shape=[jax.ShapeDtypeStruct((n, GDN_IN_PAD), BF16), jax.ShapeDtypeStruct((8, QKV), F32),
                   jax.ShapeDtypeStruct((8, 128), F32)],
        scratch_shapes=[pltpu.VMEM((8, QKV), F32)],
        compiler_params=_params("arbitrary"),
    )(proj, proj, proj, conv_w, alog, dtb, dq, dk, dv, dbg, dgate)


@jax.custom_vjp
def _unit_lower_inverses(lowers):
    n = lowers[0].shape[0]
    eye = (lax.broadcasted_iota(jnp.int32, (n, n), 0) == lax.broadcasted_iota(jnp.int32, (n, n), 1)).astype(F32)
    xs = [-low for low in lowers]
    ps = [eye + x for x in xs]
    ys = [_dot_3x(x, x) for x in xs]
    for _ in range(int(math.log2(CHUNK)) - 2):
        both = [_dot_3x(y, jnp.concatenate([y, p], axis=1)) for y, p in zip(ys, ps)]
        ys = [b[:, :n] for b in both]
        ps = [p + b[:, n:] for p, b in zip(ps, both)]
    return tuple(p + _dot_3x(y, p) for y, p in zip(ys, ps))


def _unit_lower_inverses_fwd(lowers):
    ts = _unit_lower_inverses(lowers)
    return ts, ts


def _unit_lower_inverses_bwd(ts, dts):
    left = [_dot_3x(t, dt, ta=True) for t, dt in zip(ts, dts)]
    return (tuple(-_dot_3x(l, t, tb=True) for l, t in zip(left, ts)),)


_unit_lower_inverses.defvjp(_unit_lower_inverses_fwd, _unit_lower_inverses_bwd)


@jax.custom_vjp
def _known_inverses(lowers, inverses):
    return inverses


def _known_inverses_bwd(ts, dts):
    (d_lowers,) = _unit_lower_inverses_bwd(ts, dts)
    return d_lowers, tuple(jnp.zeros_like(t) for t in ts)


_known_inverses.defvjp(lambda lowers, inverses: (inverses, inverses), _known_inverses_bwd)


@functools.partial(jax.custom_vjp, nondiff_argnums=(1,))
def _halves(x, axis):
    half = x.shape[axis] // 2
    return (x[:half], x[half:]) if axis == 0 else (x[:, :half], x[:, half:])


_halves.defvjp(lambda x, axis: (_halves(x, axis), None), lambda axis, _, g: (jnp.concatenate(g, axis=axis),))

GDN_STEP_CHUNKS = 2


def _gdn_chunks(qs, ks, vs, bgs, states, inverses=None, keep_inverses=False):
    c = CHUNK
    heads = range(N_HEADS)
    items = [(j, h) for j in range(len(bgs)) for h in heads]
    row = lax.broadcasted_iota(jnp.int32, (c, c), 0)
    col = lax.broadcasted_iota(jnp.int32, (c, c), 1)
    incl, strict, eye = row >= col, row > col, row == col
    lane = lax.broadcasted_iota(jnp.int32, (c, 128), 1)
    rowc = lax.broadcasted_iota(jnp.int32, (c, 1), 0)
    gc_all = [_mm_f32(incl.astype(F32), bg, False, False) for bg in bgs]
    q, k, v = ([xs[j][h] for j, h in items] for xs in (qs, ks, vs))
    n = range(len(items))
    beta = [jnp.sum(jnp.where(lane == h, bgs[j], 0.0), axis=1, keepdims=True) for j, h in items]
    gc = [jnp.sum(jnp.where(lane == N_HEADS + h, gc_all[j], 0.0), axis=1, keepdims=True) for j, h in items]
    gc_row = [jnp.sum(jnp.where(eye, gc[i], 0.0), axis=0, keepdims=True) for i in n]
    gc_last = [jnp.sum(jnp.where(rowc == c - 1, gc[i], 0.0), axis=0, keepdims=True) for i in n]
    decay = [jnp.where(incl, jnp.exp(jnp.where(incl, gc[i] - gc_row[i], 0.0)), 0.0) for i in n]
    kb = [k[i] * beta[i] for i in n]
    kb_q = [_halves(_mm(jnp.concatenate([kb[i], q[i]], axis=0), k[i], False, True), 0) for i in n]
    lower = tuple(jnp.where(strict, kb_q[i][0] * decay[i], 0.0) for i in n)
    attn = [kb_q[i][1] * decay[i] for i in n]
    t_mat = _unit_lower_inverses(lower) if inverses is None else _known_inverses(lower, inverses)
    egc = [jnp.exp(gc[i]) for i in n]
    w_u = [_halves(_mm(t_mat[i], jnp.concatenate([kb[i] * egc[i], v[i] * beta[i]], axis=1), False, False), 1)
           for i in n]
    w, u = [x[0] for x in w_u], [x[1] for x in w_u]
    qg = [q[i] * egc[i] for i in n]
    kg = [k[i] * jnp.exp(gc_last[i] - gc[i]) for i in n]
    outs, cur = [], list(states)
    for j in range(len(bgs)):
        at = lambda h: j * N_HEADS + h
        w_qg = [_halves(_mm(jnp.concatenate([w[at(h)], qg[at(h)]], axis=0), cur[h], False, False), 0)
                for h in heads]
        v_new = [u[at(h)] - w_qg[h][0] for h in heads]
        outs.append(tuple(w_qg[h][1] + _mm(attn[at(h)], v_new[h], False, False) for h in heads))
        cur = [cur[h] * jnp.exp(gc_last[at(h)]) + _mm(kg[at(h)], v_new[h], True, False) for h in heads]
    return (tuple(outs), tuple(cur), t_mat) if keep_inverses else (tuple(outs), tuple(cur))


def _chunk_head_slices(ref, nch):
    return tuple(tuple(ref[j * CHUNK:(j + 1) * CHUNK, h * HEAD_DIM:(h + 1) * HEAD_DIM] for h in range(N_HEADS))
                 for j in range(nch))


def _store_chunk_heads(ref, values):
    for j, chunk in enumerate(values):
        for h, val in enumerate(chunk):
            ref[j * CHUNK:(j + 1) * CHUNK, h * HEAD_DIM:(h + 1) * HEAD_DIM] = val


def _gdn_scan(qn, kn, v, bg, name, ride=None):
    n = qn.shape[0]
    nch = GDN_STEP_CHUNKS
    rows = nch * CHUNK
    nc = n // rows

    def body(q_ref, k_ref, v_ref, bg_ref, o_ref, saved_ref, inv_ref, state_ref):
        @pl.when(pl.program_id(0) == 0)
        def _():
            state_ref[...] = jnp.zeros_like(state_ref)

        states = tuple(state_ref[h] for h in range(N_HEADS))
        for h in range(N_HEADS):
            saved_ref[h] = states[h]
        bgs = tuple(bg_ref[j * CHUNK:(j + 1) * CHUNK, :] for j in range(nch))
        outs, new_states, inverses = _gdn_chunks(_chunk_head_slices(q_ref, nch), _chunk_head_slices(k_ref, nch),
                                                 _chunk_head_slices(v_ref, nch), bgs, states, keep_inverses=True)
        _store_chunk_heads(o_ref, outs)
        for h in range(N_HEADS):
            state_ref[h] = new_states[h]
        for item, inverse in enumerate(inverses):
            inv_ref[item] = inverse

    row = lambda i: (i, 0)
    return _call(
        body, [qn, kn, v, bg], name=name, grid=(nc,),
        in_specs=[pl.BlockSpec((rows, D_MODEL), row)] * 3 + [pl.BlockSpec((rows, 128), row)],
        out_specs=[pl.BlockSpec((rows, D_MODEL), row),
                   pl.BlockSpec((None, N_HEADS, HEAD_DIM, HEAD_DIM), lambda i: (i, 0, 0, 0)),
                   pl.BlockSpec((None, nch * N_HEADS, CHUNK, CHUNK), lambda i: (i, 0, 0, 0))],
        out_shape=[jax.ShapeDtypeStruct((n, D_MODEL), F32),
                   jax.ShapeDtypeStruct((nc, N_HEADS, HEAD_DIM, HEAD_DIM), F32),
                   jax.ShapeDtypeStruct((nc, nch * N_HEADS, CHUNK, CHUNK), F32)],
        scratch_shapes=[pltpu.VMEM((N_HEADS, HEAD_DIM, HEAD_DIM), F32)],
        compiler_params=_params("arbitrary"),
        ride=ride, first=lambda: pl.program_id(0) == 0, last=lambda: pl.program_id(0) == nc - 1)


def _gdn_scan_bwd(qn, kn, v, bg, saved, inverses, do, name, ride=None):
    n = qn.shape[0]
    nch = GDN_STEP_CHUNKS
    rows = nch * CHUNK
    nc = n // rows

    def body(q_ref, k_ref, v_ref, bg_ref, saved_ref, inv_ref, do_ref, dq_ref, dk_ref, dv_ref, dbg_ref, dstate_ref):
        @pl.when(pl.program_id(0) == 0)
        def _():
            dstate_ref[...] = jnp.zeros_like(dstate_ref)

        states = tuple(saved_ref[h] for h in range(N_HEADS))
        bgs = tuple(bg_ref[j * CHUNK:(j + 1) * CHUNK, :] for j in range(nch))
        known = tuple(inv_ref[item] for item in range(nch * N_HEADS))
        _, vjp = jax.vjp(functools.partial(_gdn_chunks, inverses=known), _chunk_head_slices(q_ref, nch),
                         _chunk_head_slices(k_ref, nch), _chunk_head_slices(v_ref, nch), bgs, states)
        dstates = tuple(dstate_ref[h] for h in range(N_HEADS))
        dqs, dks, dvs, dbgs, dprev = vjp((_chunk_head_slices(do_ref, nch), dstates))
        _store_chunk_heads(dq_ref, dqs)
        _store_chunk_heads(dk_ref, dks)
        _store_chunk_heads(dv_ref, dvs)
        for h in range(N_HEADS):
            dstate_ref[h] = dprev[h]
        for j in range(nch):
            dbg_ref[j * CHUNK:(j + 1) * CHUNK, :] = dbgs[j]

    row = lambda i: (nc - 1 - i, 0)
    return _call(
        body, [qn, kn, v, bg, saved, inverses, do], name=name, grid=(nc,),
        in_specs=[pl.BlockSpec((rows, D_MODEL), row)] * 3 + [pl.BlockSpec((rows, 128), row),
                  pl.BlockSpec((None, N_HEADS, HEAD_DIM, HEAD_DIM), lambda i: (nc - 1 - i, 0, 0, 0)),
                  pl.BlockSpec((None, nch * N_HEADS, CHUNK, CHUNK), lambda i: (nc - 1 - i, 0, 0, 0)),
                  pl.BlockSpec((rows, D_MODEL), row)],
        out_specs=[pl.BlockSpec((rows, D_MODEL), row)] * 3 + [pl.BlockSpec((rows, 128), row)],
        out_shape=[jax.ShapeDtypeStruct((n, D_MODEL), F32)] * 3 + [jax.ShapeDtypeStruct((n, 128), F32)],
        scratch_shapes=[pltpu.VMEM((N_HEADS, HEAD_DIM, HEAD_DIM), F32)],
        compiler_params=_params("arbitrary"),
        ride=ride, first=lambda: pl.program_id(0) == 0, last=lambda: pl.program_id(0) == nc - 1)


SB_BQ = 512
SB_BK = 256
SB_SUB = 128
SB_ROWS = 128
SB_SCALE = HEAD_DIM ** -0.5
SB_DEAD = -105.0


def _sb_terms(z, before):
    e = jnp.exp(-jnp.abs(z))
    log_beta = jnp.minimum(z, 0.0) - jnp.log(1.0 + e)
    log_1m = log_beta - z
    if before is not None:
        log_1m = jnp.where(before, log_1m, 0.0)
    return e, log_beta, log_1m


def _tri_ones(n, cmp):
    r = lax.broadcasted_iota(jnp.int32, (n, 2 * n), 0)
    c = lax.broadcasted_iota(jnp.int32, (n, 2 * n), 1)
    return jnp.where((c >= n) | cmp(r, c), 1.0, 0.0).astype(BF16)


def _sums(x, tri_ones):
    both = jnp.dot(x.astype(BF16), tri_ones, preferred_element_type=F32)
    n = x.shape[1]
    return both[:, :n], both[:, n:]


def _sb_chunk_mask(diagonal, r, s):
    if not diagonal or s * SB_SUB + SB_SUB - 1 < r * SB_ROWS:
        return None
    if s * SB_SUB >= r * SB_ROWS + SB_ROWS - 1:
        return "empty"
    rows = r * SB_ROWS + lax.broadcasted_iota(jnp.int32, (SB_ROWS, SB_SUB), 0)
    cols = s * SB_SUB + lax.broadcasted_iota(jnp.int32, (SB_ROWS, SB_SUB), 1)
    return cols < rows


def _sb_attention(q, kv, name, bq=SB_BQ, ride=None):
    n = q.shape[0]
    bq = min(bq, n)
    bk = min(SB_BK, bq)
    nrc = bq // SB_ROWS

    def body(q_ref, k_ref, v_ref, o_ref, l_ref, n_ref, z_scr, a_scr):
        i = pl.program_id(1)
        qb = q_ref[...]
        after = _tri_ones(SB_SUB, lambda r, c: r > c)

        def span(start, kw, c_sum, acc, diagonal):
            start = pl.multiple_of(start, kw)
            k_w = k_ref[pl.ds(start, kw), :]
            v_w = v_ref[pl.ds(start, kw), :]
            z_scr[:, :kw] = lax.dot_general(qb, k_w, _dims(False, True), preferred_element_type=F32)
            c_rows = [c_sum[r * SB_ROWS:(r + 1) * SB_ROWS] for r in range(nrc)]
            for s in reversed(range(kw // SB_SUB)):
                cols = slice(s * SB_SUB, (s + 1) * SB_SUB)
                for r in range(nrc):
                    rows = slice(r * SB_ROWS, (r + 1) * SB_ROWS)
                    before = _sb_chunk_mask(diagonal, r, s)
                    if isinstance(before, str):
                        a_scr[rows, cols] = jnp.zeros((SB_ROWS, SB_SUB), BF16)
                        continue
                    _, log_beta, log_1m = _sb_terms(z_scr[rows, cols] * SB_SCALE, before)
                    tail, total = _sums(log_1m, after)
                    a = jnp.exp(log_beta + c_rows[r] + tail)
                    if before is not None:
                        a = jnp.where(before, a, 0.0)
                    a_scr[rows, cols] = a.astype(BF16)
                    c_rows[r] = c_rows[r] + total
            acc = acc + jnp.dot(a_scr[:, :kw], v_w, preferred_element_type=F32)
            return jnp.concatenate(c_rows, axis=0), acc

        c_sum, acc = span(i * bq, bq, jnp.zeros((bq, SB_SUB), F32), jnp.zeros((bq, HEAD_DIM), F32), True)

        def more(state):
            done, c, _ = state
            return (done < i * (bq // bk)) & (jnp.max(c) > SB_DEAD)

        def step(state):
            done, c, a = state
            c, a = span(i * bq - (done + 1) * bk, bk, c, a, False)
            return done + 1, c, a

        done, c_sum, acc = lax.while_loop(more, step, (jnp.int32(0), c_sum, acc))
        o_ref[...] = acc.astype(BF16)
        l_ref[...] = c_sum
        n_ref[...] = jnp.full((8, 128), done.astype(F32), F32)

    nq = n // bq
    return _call(
        body, [q, kv, kv], name=name, grid=(N_HEADS, nq),
        in_specs=[pl.BlockSpec((bq, HEAD_DIM), lambda h, i: (i, h)),
                  pl.BlockSpec((n, HEAD_DIM), lambda h, i: (0, h)),
                  pl.BlockSpec((n, HEAD_DIM), lambda h, i: (0, N_HEADS + h))],
        out_specs=[pl.BlockSpec((bq, HEAD_DIM), lambda h, i: (i, h)),
                   pl.BlockSpec((None, bq, 128), lambda h, i: (h, i, 0)),
                   pl.BlockSpec((None, None, 8, 128), lambda h, i: (h, i, 0, 0))],
        out_shape=[jax.ShapeDtypeStruct((n, D_MODEL), BF16), jax.ShapeDtypeStruct((N_HEADS, n, 128), F32),
                   jax.ShapeDtypeStruct((N_HEADS, nq, 8, 128), F32)],
        scratch_shapes=[pltpu.VMEM((bq, bq), F32), pltpu.VMEM((bq, bq), BF16)],
        compiler_params=_params("arbitrary", "arbitrary"),
        ride=ride, first=lambda: (pl.program_id(0) == 0) & (pl.program_id(1) == 0),
        last=lambda: (pl.program_id(0) == N_HEADS - 1) & (pl.program_id(1) == nq - 1))


def _sb_attention_bwd(q, kv, do, lsum, spans, name, bq=SB_BQ, ride=None):
    n = q.shape[0]
    bq = min(bq, n)
    bk = min(SB_BK, bq)
    nrc = bq // SB_ROWS

    def body(q_ref, k_ref, v_ref, do_ref, l_ref, n_ref, dq_ref, dk_out, dv_out, z_scr, da_scr, a_scr, dz_scr, dk_ref, dv_ref):
        i = pl.program_id(1)

        @pl.when(i == 0)
        def _():
            dk_ref[...] = jnp.zeros_like(dk_ref)
            dv_ref[...] = jnp.zeros_like(dv_ref)

        qb = q_ref[...]
        dob = do_ref[...]
        lt_rows = [l_ref[r * SB_ROWS:(r + 1) * SB_ROWS, :] for r in range(nrc)]
        upto = _tri_ones(SB_SUB, lambda r, c: r <= c)
        below = _tri_ones(SB_SUB, lambda r, c: r < c)

        def span(start, kw, l_sum, g_sum, dq, diagonal):
            start = pl.multiple_of(start, kw)
            k_w = k_ref[pl.ds(start, kw), :]
            v_w = v_ref[pl.ds(start, kw), :]
            z_scr[:, :kw] = lax.dot_general(qb, k_w, _dims(False, True), preferred_element_type=F32)
            da_scr[:, :kw] = lax.dot_general(dob, v_w, _dims(False, True), preferred_element_type=F32)
            l_rows = [l_sum[r * SB_ROWS:(r + 1) * SB_ROWS] for r in range(nrc)]
            g_rows = [g_sum[r * SB_ROWS:(r + 1) * SB_ROWS] for r in range(nrc)]
            for s in range(kw // SB_SUB):
                cols = slice(s * SB_SUB, (s + 1) * SB_SUB)
                for r in range(nrc):
                    rows = slice(r * SB_ROWS, (r + 1) * SB_ROWS)
                    before = _sb_chunk_mask(diagonal, r, s)
                    if isinstance(before, str):
                        a_scr[rows, cols] = jnp.zeros((SB_ROWS, SB_SUB), BF16)
                        dz_scr[rows, cols] = jnp.zeros((SB_ROWS, SB_SUB), BF16)
                        continue
                    zs = z_scr[rows, cols] * SB_SCALE
                    e, log_beta, log_1m = _sb_terms(zs, before)
                    l_prefix, l_total = _sums(log_1m, upto)
                    a = jnp.exp(log_beta + (lt_rows[r] - (l_rows[r] + l_prefix)))
                    if before is not None:
                        a = jnp.where(before, a, 0.0)
                    g = a * da_scr[rows, cols]
                    g_prefix, g_total = _sums(g, below)
                    inv = 1.0 / (1.0 + e)
                    beta = jnp.where(zs >= 0, inv, e * inv)
                    dz = (g - (g + g_rows[r] + g_prefix) * beta) * SB_SCALE
                    if before is not None:
                        dz = jnp.where(before, dz, 0.0)
                    a_scr[rows, cols] = a.astype(BF16)
                    dz_scr[rows, cols] = dz.astype(BF16)
                    l_rows[r] = l_rows[r] + l_total
                    g_rows[r] = g_rows[r] + g_total
            dz_w = dz_scr[:, :kw]
            dq = dq + jnp.dot(dz_w, k_w, preferred_element_type=F32)
            dk_ref[pl.ds(start, kw), :] += lax.dot_general(dz_w, qb, _dims(True, False), preferred_element_type=F32)
            dv_ref[pl.ds(start, kw), :] += lax.dot_general(a_scr[:, :kw], dob, _dims(True, False),
                                                           preferred_element_type=F32)
            return jnp.concatenate(l_rows, axis=0), jnp.concatenate(g_rows, axis=0), dq

        taken = jnp.clip(jnp.max(n_ref[...]).astype(jnp.int32), 0, i * (bq // bk))
        zero = jnp.zeros((bq, SB_SUB), F32)
        carry = lax.fori_loop(0, taken, lambda j, c: span(i * bq - (taken - j) * bk, bk, c[0], c[1], c[2], False),
                              (zero, zero, jnp.zeros((bq, HEAD_DIM), F32)))
        _, _, dq = span(i * bq, bq, carry[0], carry[1], carry[2], True)
        dq_ref[...] = dq.astype(BF16)

        @pl.when(i == nq - 1)
        def _():
            dk_out[...] = dk_ref[...].astype(BF16)
            dv_out[...] = dv_ref[...].astype(BF16)

    nq = n // bq
    return _call(
        body, [q, kv, kv, do, lsum, spans], name=name, grid=(N_HEADS, nq),
        in_specs=[pl.BlockSpec((bq, HEAD_DIM), lambda h, i: (i, h)),
                  pl.BlockSpec((n, HEAD_DIM), lambda h, i: (0, h)),
                  pl.BlockSpec((n, HEAD_DIM), lambda h, i: (0, N_HEADS + h)),
                  pl.BlockSpec((bq, HEAD_DIM), lambda h, i: (i, h)),
                  pl.BlockSpec((None, bq, 128), lambda h, i: (h, i, 0)),
                  pl.BlockSpec((None, None, 8, 128), lambda h, i: (h, i, 0, 0))],
        out_specs=[pl.BlockSpec((bq, HEAD_DIM), lambda h, i: (i, h)),
                   pl.BlockSpec((n, HEAD_DIM), lambda h, i: (0, h)),
                   pl.BlockSpec((n, HEAD_DIM), lambda h, i: (0, h))],
        out_shape=[jax.ShapeDtypeStruct((n, D_MODEL), BF16)] * 3,
        scratch_shapes=[pltpu.VMEM((bq, bq), F32), pltpu.VMEM((bq, bq), F32), pltpu.VMEM((bq, bq), BF16),
                        pltpu.VMEM((bq, bq), BF16), pltpu.VMEM((n, HEAD_DIM), F32), pltpu.VMEM((n, HEAD_DIM), F32)],
        compiler_params=_params("arbitrary", "arbitrary"),
        ride=ride, first=lambda: (pl.program_id(0) == 0) & (pl.program_id(1) == 0),
        last=lambda: (pl.program_id(0) == N_HEADS - 1) & (pl.program_id(1) == nq - 1))


def _local_step(x, target, gains, comm):
    g = gains
    w = comm.w
    big = {}
    row = lambda a, i: a[i:i + 1, :]

    proj, h0 = _norm_matmul(x, row(g["mix_pre"], 0), w["gdn_in"], F32, "gdn_in_proj", tn=1408)
    qn, kn, v, bg = _gdn_pre(proj, w["conv"], g["alog"], g["dtb"], "gdn_pre")
    (o_gdn, saved, inverses), got = _gdn_scan(qn, kn, v, bg, "gdn_scan", ride=comm.ride("scan"))
    comm.done("scan", got)
    x1, y_mix0, a_gdn = _out_proj((o_gdn, proj), True, x, w["gdn_out"], row(g["mix_post"], 0), g["out_gain"],
                                  "gdn_out_proj", tm=512)
    ride = comm.ride("mlp0")
    res = _mlp_fwd(x1, row(g["mlp_pre"], 0), w["up0"], w["down0"], row(g["mlp_post"], 0), "mlp0", ride=ride)
    (x2, y_mlp0, h_mlp0, u0, a0), got = res if ride is not None else (res, [])
    comm.done("mlp0", got)
    kv, h_kv = _norm_matmul(x2, g["kv"], w["kv"], BF16, "kv_proj", tm=2048)
    q, h_q = _norm_matmul(x2, row(g["mix_pre"], 1), w["sb_q"], BF16, "sb_q_proj")
    (o_sb, lsum, spans), got = _sb_attention(q, kv, "sb_attention", ride=comm.ride("sb"))
    comm.done("sb", got)
    x3, y_mix1 = _out_proj((o_sb,), False, x2, w["sb_o"], row(g["mix_post"], 1), None, "sb_out_proj")
    dx4, y_mlp1, h_mlp1, u1, a1, loss = _mlp_fwd(x3, row(g["mlp_pre"], 1), w["up1"], w["down1"], row(g["mlp_post"], 1),
                                                 "mlp1_loss", target=target)

    dx3, dy_mlp1, du1, dg_mlp1 = _mlp_bwd(dx4, y_mlp1, row(g["mlp_post"], 1), w["down1"], u1, w["up1"], x3,
                                          row(g["mlp_pre"], 1), "mlp1_bwd")
    big["down1"] = _matmul_tn(a1, dy_mlp1, 1, "d_down1")[0]
    big["up1"] = _matmul_tn(h_mlp1, du1, N_DEV, "d_up1", tk=2048)
    dy_mix1, dg_post1, do_sb = _out_proj_bwd(dx3, y_mix1, w["sb_o"], row(g["mix_post"], 1), None, "sb_out_proj_bwd")
    big["sb_o"] = _matmul_tn(o_sb, dy_mix1, 1, "d_sb_o")[0]
    (dq, dk, dv), got = _sb_attention_bwd(q, kv, do_sb, lsum, spans, "sb_attention_bwd", ride=comm.ride("sb_bwd", big))
    comm.done("sb_bwd", got)
    big["sb_q"] = _matmul_tn(h_q, dq, 1, "d_sb_q")[0]
    big["kv"] = jnp.concatenate([_matmul_tn(h_kv, dk, N_DEV // 2, "d_w_k", tk=2048),
                                 _matmul_tn(h_kv, dv, N_DEV // 2, "d_w_v", tk=2048)],
                                axis=0)
    dx2, dg_x2 = _norm_matmul_bwd([(dq, w["sb_q"], row(g["mix_pre"], 1), 1024), (dk, w["kv"][0:1], g["kv"], 1024),
                                   (dv, w["kv"][1:2], g["kv"], 1024)], x2, dx3, "qkv_proj_bwd", tm=512)

    big.update(x2=dg_x2, post1=dg_post1, mlp1=dg_mlp1, loss=loss)
    ride = comm.ride("mlp0_bwd", big)
    res = _mlp_bwd(dx2, y_mlp0, row(g["mlp_post"], 0), w["down0"], u0, w["up0"], x1, row(g["mlp_pre"], 0), "mlp0_bwd",
                   ride=ride)
    (dx1, dy_mlp0, du0, dg_mlp0), got = res if ride is not None else (res, [])
    comm.done("mlp0_bwd", got)
    big["down0"] = _matmul_tn(a0, dy_mlp0, 1, "d_down0")[0]
    big["up0"] = _matmul_tn(h_mlp0, du0, N_DEV, "d_up0", tk=2048)
    dy_mix0, dg_post0, do_gdn, dgate, d_out_gain = _out_proj_bwd(
        dx1, y_mix0, w["gdn_out"], row(g["mix_post"], 0), (o_gdn, proj, g["out_gain"]), "gdn_out_proj_bwd", tm=512)
    big["gdn_out"] = _matmul_tn(a_gdn, dy_mix0, 1, "d_gdn_out")[0]
    big.update(post0=dg_post0, mlp0=dg_mlp0, out_gain=d_out_gain)
    (dqn, dkn, dvv, dbg), got = _gdn_scan_bwd(qn, kn, v, bg, saved, inverses, do_gdn, "gdn_scan_bwd",
                                              ride=comm.ride("scan_bwd", big))
    comm.done("scan_bwd", got)
    dproj, d_conv, d_gates = _gdn_pre_bwd(proj, w["conv"], g["alog"], g["dtb"], dqn, dkn, dvv, dbg, dgate, "gdn_pre_bwd")
    big["gdn_in"] = _matmul_tn(h0, dproj, 1, "d_gdn_in", tb=1408)[0]
    big.update(conv=d_conv, gates=d_gates)
    ride = comm.ride("in_bwd", big)
    grad_x, dg_pre0, *got = _norm_matmul_bwd([(dproj, w["gdn_in"], row(g["mix_pre"], 0), 1408)], x, dx1, "gdn_in_proj_bwd",
                                             ride=ride)
    comm.done("in_bwd", got[0] if got else [])

    small = {"pre0": dg_pre0, "x2": dg_x2, "post0": dg_post0, "post1": dg_post1, "mlp0": dg_mlp0, "mlp1": dg_mlp1,
             "gates": d_gates, "out_gain": d_out_gain, "conv": d_conv, "loss": loss}
    return grad_x, big, small


GATHER_STAGES = {"scan": ("gdn_out", "up0", "down0"), "mlp0": ("kv", "sb_q"), "sb": ("sb_o", "up1", "down1")}
SCATTER_STAGES = {"sb_bwd": ("up1", "down1", "sb_o"), "mlp0_bwd": ("sb_q", "kv"), "scan_bwd": ("up0", "down0", "gdn_out"),
                  "in_bwd": ("gdn_in",)}
SMALL_STAGES = {"mlp0_bwd": ("x2", "post1", "mlp1", "loss"), "scan_bwd": ("post0", "mlp0", "out_gain"),
                "in_bwd": ("conv", "gates")}


def _whole_weight(name, gathered):
    d = D_MODEL
    if name in ("up0", "up1"):
        return gathered
    if name == "kv":
        return gathered.reshape(2, N_DEV // 2, d, gathered.shape[2]).transpose(0, 2, 1, 3).reshape(2, d, d)
    if name == "gdn_in":
        whole = gathered.transpose(1, 0, 2).reshape(d, GDN_IN_COLS)
        return jnp.pad(whole, ((0, 0), (0, GDN_IN_PAD - GDN_IN_COLS)))[None]
    if name == "conv":
        return gathered.transpose(1, 0, 2).reshape(CONV_WIDTH, QKV)
    whole = gathered.reshape(gathered.shape[0] * gathered.shape[1], d)
    return whole[None] if name == "sb_q" else whole


def _owner_blocks(name, grad):
    if name in ("up0", "up1", "kv"):
        blocks = grad
    elif name == "gdn_in":
        blocks = grad[:, :GDN_IN_COLS].reshape(D_MODEL, N_DEV, GDN_IN_COLS // N_DEV).transpose(1, 0, 2)
    else:
        blocks = grad.reshape(N_DEV, grad.shape[0] // N_DEV, grad.shape[1])
    return blocks.astype(BF16)


class _Fsdp:
    def __init__(self, shards, weights):
        self.shards, self.w, self.landed, self.small_landed = shards, weights, {}, {}

    def ride(self, stage, grads=None):
        if stage in GATHER_STAGES:
            names = GATHER_STAGES[stage]
            return _Exchange([self.shards[nm] for nm in names], [True] * len(names))
        names, small = SCATTER_STAGES[stage], SMALL_STAGES.get(stage, ())
        return _Exchange([_owner_blocks(nm, grads[nm]) for nm in names] + [grads[nm] for nm in small],
                         [False] * len(names) + [True] * len(small))

    def done(self, stage, outs):
        if stage in GATHER_STAGES:
            for nm, out in zip(GATHER_STAGES[stage], outs):
                self.w[nm] = _whole_weight(nm, out)
        else:
            names = SCATTER_STAGES[stage]
            self.landed.update(zip(names, outs[:len(names)]))
            self.small_landed.update(zip(SMALL_STAGES.get(stage, ()), outs[len(names):]))


def _my_place():
    return lax.axis_index("x"), lax.axis_index("y"), lax.axis_index("c")


class _Exchange:
    def __init__(self, arrays, gather):
        self.arrays, self.gather, self.n = list(arrays), list(gather), len(arrays)
        any_spec = pl.BlockSpec(memory_space=pl.ANY)
        self.in_specs = [any_spec] * self.n
        self.out_specs = [any_spec] * self.n
        self.out_shape = [jax.ShapeDtypeStruct(((N_DEV,) + a.shape) if g else a.shape, a.dtype)
                          for a, g in zip(self.arrays, self.gather)]
        self.scratch = [pltpu.SemaphoreType.DMA((self.n, N_DEV - 1)), pltpu.SemaphoreType.DMA((self.n, N_DEV - 1)),
                        pltpu.SemaphoreType.DMA((self.n,))]

    def _copies(self, ins, outs, sems):
        send_sems, recv_sems, local_sems = sems
        x, y, c = _my_place()
        me = 4 * x + 2 * y + c
        copies = []
        for a in range(self.n):
            src = ins[a] if self.gather[a] else ins[a].at[me]
            copies.append(pltpu.make_async_copy(src, outs[a].at[me], local_sems.at[a]))
        for k in range(1, N_DEV):
            px = 1 - x if k & 4 else x
            py = 1 - y if k & 2 else y
            pc = 1 - c if k & 1 else c
            peer = 4 * px + 2 * py + pc
            for a in range(self.n):
                src = ins[a] if self.gather[a] else ins[a].at[peer]
                copies.append(pltpu.make_async_remote_copy(
                    src_ref=src, dst_ref=outs[a].at[me], send_sem=send_sems.at[a, k - 1], recv_sem=recv_sems.at[a, k - 1],
                    device_id=(px, py, pc), device_id_type=MESH))
        return copies

    def start(self, ins, outs, sems):
        for cp in self._copies(ins, outs, sems):
            cp.start()

    def wait(self, ins, outs, sems):
        for cp in self._copies(ins, outs, sems):
            cp.wait()


def _call(body, operands, *, name, out_shape, in_specs, out_specs, grid=(), scratch_shapes=(), compiler_params=None,
          ride=None, first=None, last=None):
    n_in, n_out, n_scr = len(operands), len(out_shape), len(scratch_shapes)
    if ride is None:
        outs = pl.pallas_call(body, name=name, grid=grid, in_specs=in_specs, out_specs=out_specs, out_shape=out_shape,
                              scratch_shapes=scratch_shapes, compiler_params=compiler_params)(*operands)
        return list(outs), []
    r = ride.n

    def riding(*refs):
        ins, r_ins = refs[:n_in], refs[n_in:n_in + r]
        outs, r_outs = refs[n_in + r:n_in + r + n_out], refs[n_in + r + n_out:n_in + 2 * r + n_out]
        scr, sems = refs[n_in + 2 * r + n_out:n_in + 2 * r + n_out + n_scr], refs[n_in + 2 * r + n_out + n_scr:]

        @pl.when(first())
        def _():
            ride.start(r_ins, r_outs, sems)

        body(*ins, *outs, *scr)

        @pl.when(last())
        def _():
            ride.wait(r_ins, r_outs, sems)

    outs = pl.pallas_call(
        riding, name=name, grid=grid, in_specs=list(in_specs) + ride.in_specs, out_specs=list(out_specs) + ride.out_specs,
        out_shape=list(out_shape) + ride.out_shape, scratch_shapes=list(scratch_shapes) + ride.scratch,
        compiler_params=compiler_params)(*operands, *ride.arrays)
    return list(outs[:n_out]), list(outs[n_out:])


def _exchange(arrays, gather, name):
    ride = _Exchange(arrays, gather)

    def body(*refs):
        ins, outs, sems = refs[:ride.n], refs[ride.n:2 * ride.n], refs[2 * ride.n:]
        ride.start(ins, outs, sems)
        ride.wait(ins, outs, sems)

    return pl.pallas_call(body, name=name, in_specs=ride.in_specs, out_specs=ride.out_specs, out_shape=ride.out_shape,
                          scratch_shapes=ride.scratch)(*arrays)


def _gather_two_level(arrays, name):
    n_arr = len(arrays)

    def body(*refs):
        ins, outs = refs[:n_arr], refs[n_arr:2 * n_arr]
        send_sems, recv_sems, local_sems = refs[2 * n_arr:]
        x, y, c = _my_place()
        sibling = (x, y, 1 - c)
        chips = [(1 - x, y), (x, 1 - y), (1 - x, 1 - y)]
        index = lambda px, py, pc: 4 * px + 2 * py + pc

        def copy(a, k, block, to, src=None):
            dst = outs[a].at[index(*block)]
            return pltpu.make_async_remote_copy(src_ref=dst if src is None else src, dst_ref=dst,
                                                send_sem=send_sems.at[a, k], recv_sem=recv_sems.at[a, k],
                                                device_id=to, device_id_type=MESH)

        mine = [pltpu.make_async_copy(ins[a], outs[a].at[index(x, y, c)], local_sems.at[a]) for a in range(n_arr)]
        first = [copy(a, 0, (x, y, c), sibling, src=ins[a]) for a in range(n_arr)]
        first += [copy(a, 1 + j, (x, y, c), (*chip, c), src=ins[a]) for j, chip in enumerate(chips) for a in range(n_arr)]
        for cp in mine + first:
            cp.start()
        passed = []
        for j, chip in enumerate(chips):
            for a in range(n_arr):
                copy(a, 1 + j, (*chip, c), (x, y, c)).wait_recv()
                passed.append(copy(a, 4 + j, (*chip, c), sibling))
                passed[-1].start()
        for a in range(n_arr):
            copy(a, 0, sibling, (x, y, c)).wait_recv()
            for j, chip in enumerate(chips):
                copy(a, 4 + j, (*chip, 1 - c), (x, y, c)).wait_recv()
        for cp in first + passed:
            cp.wait_send()
        for cp in mine:
            cp.wait()

    any_spec = pl.BlockSpec(memory_space=pl.ANY)
    return pl.pallas_call(
        body, name=name, in_specs=[any_spec] * n_arr, out_specs=[any_spec] * n_arr,
        out_shape=[jax.ShapeDtypeStruct((N_DEV,) + a.shape, a.dtype) for a in arrays],
        scratch_shapes=[pltpu.SemaphoreType.DMA((n_arr, N_DEV - 1)), pltpu.SemaphoreType.DMA((n_arr, N_DEV - 1)),
                        pltpu.SemaphoreType.DMA((n_arr,))])(*arrays)


def _adamw_math(g, w, m, v):
    m = ADAM_B1 * m + (1.0 - ADAM_B1) * g
    v = ADAM_B2 * v + (1.0 - ADAM_B2) * jnp.square(g)
    m_hat = m / (1.0 - ADAM_B1 ** ADAM_STEP)
    v_hat = v / (1.0 - ADAM_B2 ** ADAM_STEP)
    delta = -ADAM_LR * (m_hat / (jnp.sqrt(v_hat) + ADAM_EPS) + ADAM_WD * w)
    return delta, m, v


def _sum_devices(ref):
    total = ref[0].astype(F32)
    for d in range(1, N_DEV):
        total = total + ref[d].astype(F32)
    return total


def _reduce_adamw(landed, w, m, v, name, tr=256):
    r, c = w.shape
    tr = min(tr, r)
    assert r % tr == 0

    def body(l_ref, w_ref, m_ref, v_ref, g_ref, d_ref, nm_ref, nv_ref):
        g = _sum_devices(l_ref)
        g_ref[...] = g
        d_ref[...], nm_ref[...], nv_ref[...] = _adamw_math(g, w_ref[...], m_ref[...], v_ref[...])

    blk = pl.BlockSpec((tr, c), lambda i: (i, 0))
    return pl.pallas_call(
        body, name=name, grid=(r // tr,),
        in_specs=[pl.BlockSpec((N_DEV, tr, c), lambda i: (0, i, 0)), blk, blk, blk],
        out_specs=[blk] * 4, out_shape=[jax.ShapeDtypeStruct((r, c), F32)] * 4,
        compiler_params=_params("parallel"),
    )(landed, w, m, v)


def _reduce_adamw_layers(landed, w, m, v, name, tr=256):
    _, r, c = w.shape
    tr = min(tr, r)
    nr = r // tr
    assert r % tr == 0

    def body(l0_ref, l1_ref, w_ref, m_ref, v_ref, g_ref, d_ref, nm_ref, nv_ref):
        g = jnp.where(pl.program_id(0) == 0, _sum_devices(l0_ref), _sum_devices(l1_ref))
        g_ref[...] = g
        d_ref[...], nm_ref[...], nv_ref[...] = _adamw_math(g, w_ref[...], m_ref[...], v_ref[...])

    blk = pl.BlockSpec((None, tr, c), lambda l, i: (l, i, 0))
    return pl.pallas_call(
        body, name=name, grid=(2, nr),
        in_specs=[pl.BlockSpec((N_DEV, tr, c), lambda l, i: (0, jnp.where(l == 0, i, nr - 1), 0)),
                  pl.BlockSpec((N_DEV, tr, c), lambda l, i: (0, jnp.where(l == 1, i, 0), 0)), blk, blk, blk],
        out_specs=[blk] * 4, out_shape=[jax.ShapeDtypeStruct(w.shape, F32)] * 4,
        compiler_params=_params("arbitrary", "arbitrary"),
    )(landed[0], landed[1], w, m, v)


def _adamw(g, w, m, v, name):
    def body(g_ref, w_ref, m_ref, v_ref, d_ref, nm_ref, nv_ref):
        d_ref[...], nm_ref[...], nv_ref[...] = _adamw_math(g_ref[...], w_ref[...], m_ref[...], v_ref[...])

    return pl.pallas_call(body, name=name, out_shape=[jax.ShapeDtypeStruct(w.shape, F32)] * 3)(g, w, m, v)


def _small_update(landed, params, name):
    layout = {
        "mix_pre": [("pre0", 0), ("x2", 0)], "mix_post": [("post0", 0), ("post1", 0)],
        "mlp_pre": [("mlp0", 0), ("mlp1", 0)], "mlp_post": [("mlp0", 1), ("mlp1", 1)],
        "kv": [("kv", 0)], "alog": [("gates", 0)], "dtb": [("gates", 1)], "out_gain": [("out_gain", 0)],
    }
    landed_names = sorted(landed)
    param_names = sorted(params)
    n_l, n_p = len(landed_names), len(param_names)

    def body(*refs):
        l_refs = dict(zip(landed_names, refs[:n_l]))
        p_refs = {p: refs[n_l + 3 * i:n_l + 3 * i + 3] for i, p in enumerate(param_names)}
        outs = refs[n_l + 3 * n_p:]
        o_refs = {p: outs[4 * i:4 * i + 4] for i, p in enumerate(param_names)}
        conv_ref, loss_ref = outs[4 * n_p:]
        sums = {nm: _sum_devices(l_refs[nm]) for nm in landed_names}
        sums["kv"] = sums["x2"][1:2, :] + sums["x2"][2:3, :]
        for p in param_names:
            w_ref, m_ref, v_ref = p_refs[p]
            g_ref, d_ref, nm_ref, nv_ref = o_refs[p]
            for r, (src, src_row) in enumerate(layout[p]):
                g = sums[src][src_row:src_row + 1, :]
                g_ref[r:r + 1, :] = g
                d, nm, nv = _adamw_math(g, w_ref[r:r + 1, :], m_ref[r:r + 1, :], v_ref[r:r + 1, :])
                d_ref[r:r + 1, :] = d
                nm_ref[r:r + 1, :] = nm
                nv_ref[r:r + 1, :] = nv
        conv_ref[...] = sums["conv"]
        loss_ref[...] = sums["loss"]

    args = [landed[nm] for nm in landed_names]
    out_shape = []
    for p in param_names:
        args += list(params[p])
        out_shape += [jax.ShapeDtypeStruct(params[p][0].shape, F32)] * 4
    out_shape += [jax.ShapeDtypeStruct(landed["conv"].shape[1:], F32), jax.ShapeDtypeStruct(landed["loss"].shape[1:], F32)]
    outs = pl.pallas_call(body, name=name, out_shape=out_shape)(*args)
    result = {p: tuple(outs[4 * i:4 * i + 4]) for i, p in enumerate(param_names)}
    result["conv"], result["loss"] = outs[4 * n_p], outs[4 * n_p + 1]
    return result


def _lanes(vec, offset):
    return jnp.pad(vec[None, :], ((0, 0), (offset, 128 - offset - vec.shape[0])))


def kernel(x, mix_pre_gain, mix_post_gain, mlp_pre_gain, mlp_post_gain, mlp_w_up, mlp_w_down, gdn_w_in, gdn_conv_w, gdn_a_log, gdn_dt_bias, gdn_out_gain, gdn_w_out, kv_gain, w_kv, sb_w_q, sb_w_o, loss_target, m_mix_pre_gain, m_mix_post_gain, m_mlp_pre_gain, m_mlp_post_gain, m_mlp_w_up, m_mlp_w_down, m_gdn_w_in, m_gdn_conv_w, m_gdn_a_log, m_gdn_dt_bias, m_gdn_out_gain, m_gdn_w_out, m_kv_gain, m_w_kv, m_sb_w_q, m_sb_w_o, v_mix_pre_gain, v_mix_post_gain, v_mlp_pre_gain, v_mlp_post_gain, v_mlp_w_up, v_mlp_w_down, v_gdn_w_in, v_gdn_conv_w, v_gdn_a_log, v_gdn_dt_bias, v_gdn_out_gain, v_gdn_w_out, v_kv_gain, v_w_kv, v_sb_w_q, v_sb_w_o):
    me = 4 * lax.axis_index("x") + 2 * lax.axis_index("y") + lax.axis_index("c")
    bf = lambda a: a.astype(BF16)

    shards = {"up0": bf(mlp_w_up[0]), "up1": bf(mlp_w_up[1]), "down0": bf(mlp_w_down[0]), "down1": bf(mlp_w_down[1]),
              "gdn_out": bf(gdn_w_out[0]), "kv": bf(w_kv), "sb_q": bf(sb_w_q[0]), "sb_o": bf(sb_w_o[0])}
    gdn_in, conv = _gather_two_level([bf(gdn_w_in[0]), gdn_conv_w[0]], "gather_first_weights")
    comm = _Fsdp(shards, {"gdn_in": _whole_weight("gdn_in", gdn_in), "conv": _whole_weight("conv", conv)})
    gains = {"mix_pre": mix_pre_gain, "mix_post": mix_post_gain, "mlp_pre": mlp_pre_gain, "mlp_post": mlp_post_gain,
             "kv": kv_gain[None, :], "alog": _lanes(gdn_a_log[0], N_HEADS), "dtb": _lanes(gdn_dt_bias[0], N_HEADS),
             "out_gain": gdn_out_gain}

    grad_x, big, small = _local_step(x[0], loss_target[0], gains, comm)

    (pre0,) = _exchange([small["pre0"]], [True], "exchange_last_grad")
    small_landed = dict(comm.small_landed, pre0=pre0)

    results = {}
    results["mlp_w_up"] = _reduce_adamw_layers((comm.landed["up0"], comm.landed["up1"]), mlp_w_up, m_mlp_w_up, v_mlp_w_up,
                                               "adamw_mlp_w_up")
    results["mlp_w_down"] = _reduce_adamw_layers((comm.landed["down0"], comm.landed["down1"]), mlp_w_down, m_mlp_w_down,
                                                 v_mlp_w_down, "adamw_mlp_w_down")
    big_params = [("gdn_w_in", "gdn_in", gdn_w_in[0], m_gdn_w_in[0], v_gdn_w_in[0]),
                  ("gdn_w_out", "gdn_out", gdn_w_out[0], m_gdn_w_out[0], v_gdn_w_out[0]),
                  ("w_kv", "kv", w_kv, m_w_kv, v_w_kv), ("sb_w_q", "sb_q", sb_w_q[0], m_sb_w_q[0], v_sb_w_q[0]),
                  ("sb_w_o", "sb_o", sb_w_o[0], m_sb_w_o[0], v_sb_w_o[0])]
    for nm, short, w_, m_, v_ in big_params:
        results[nm] = _reduce_adamw(comm.landed[short], w_, m_, v_, "adamw_" + nm)
    lanes8 = lambda a: _lanes(a[0], N_HEADS)
    small_params = {
        "mix_pre": (mix_pre_gain, m_mix_pre_gain, v_mix_pre_gain), "mix_post": (mix_post_gain, m_mix_post_gain, v_mix_post_gain),
        "mlp_pre": (mlp_pre_gain, m_mlp_pre_gain, v_mlp_pre_gain), "mlp_post": (mlp_post_gain, m_mlp_post_gain, v_mlp_post_gain),
        "kv": (kv_gain[None, :], m_kv_gain[None, :], v_kv_gain[None, :]),
        "alog": (lanes8(gdn_a_log), lanes8(m_gdn_a_log), lanes8(v_gdn_a_log)),
        "dtb": (lanes8(gdn_dt_bias), lanes8(m_gdn_dt_bias), lanes8(v_gdn_dt_bias)),
        "out_gain": (gdn_out_gain, m_gdn_out_gain, v_gdn_out_gain),
    }
    sm = _small_update(small_landed, small_params, "small_update")
    conv_cols = QKV // N_DEV
    g_conv = lax.dynamic_slice(sm["conv"], (0, me * conv_cols), (8, conv_cols))[:CONV_WIDTH]
    conv_res = (g_conv,) + tuple(_adamw(g_conv, gdn_conv_w[0], m_gdn_conv_w[0], v_gdn_conv_w[0], "adamw_conv"))

    lead = lambda t: tuple(a[None] for a in t)
    heads8 = lambda t: tuple(a[:, N_HEADS:2 * N_HEADS] for a in t)
    per_weight = [
        sm["mix_pre"], sm["mix_post"], sm["mlp_pre"], sm["mlp_post"],
        tuple(results["mlp_w_up"]), tuple(results["mlp_w_down"]),
        lead(results["gdn_w_in"]), lead(conv_res), heads8(sm["alog"]), heads8(sm["dtb"]), sm["out_gain"],
        lead(results["gdn_w_out"]), tuple(a[0] for a in sm["kv"]), results["w_kv"], lead(results["sb_w_q"]), lead(results["sb_w_o"]),
    ]
    grads, deltas, new_ms, new_vs = zip(*per_weight)
    return (sm["loss"][0, 0], grad_x[None], *grads, *deltas, *new_ms, *new_vs)
```

```python
import functools
import math

import jax
import jax.numpy as jnp
from jax import lax
from jax.experimental import pallas as pl
from jax.experimental.pallas import tpu as pltpu

F32 = jnp.float32
BF16 = jnp.bfloat16

N_DEV = 8
D_MODEL = 1024
N_HEADS = 8
HEAD_DIM = 128
CHUNK = 64
CONV_WIDTH = 4
GDN_IN_COLS = 4 * D_MODEL + 2 * N_HEADS
GDN_IN_PAD = 4 * D_MODEL + 128
EPS = 1e-6

ADAM_LR = 0.001
ADAM_B1 = 0.9
ADAM_B2 = 0.999
ADAM_EPS = 1e-08
ADAM_WD = 0.01
ADAM_STEP = 10

VMEM_LIMIT_BYTES = 56 * 1024 * 1024
MESH = pl.DeviceIdType.MESH


def _params(*semantics):
    return pltpu.CompilerParams(dimension_semantics=semantics, vmem_limit_bytes=VMEM_LIMIT_BYTES)


def _dims(ta, tb):
    return (((0,) if ta else (1,), (1,) if tb else (0,)), ((), ()))


def _dot(a, b, ta=False, tb=False):
    return lax.dot_general(a.astype(BF16), b.astype(BF16), _dims(ta, tb), preferred_element_type=F32)


def _dot_f32(a, b, ta=False, tb=False):
    return lax.dot_general(a, b, _dims(ta, tb), precision=lax.Precision.HIGHEST, preferred_element_type=F32)


def _dot_3x(a, b, ta=False, tb=False):
    return lax.dot_general(a, b, _dims(ta, tb), precision=lax.Precision.HIGH, preferred_element_type=F32)


def _make_mm(dot):
    @functools.partial(jax.custom_vjp, nondiff_argnums=(2, 3))
    def mm(a, b, ta, tb):
        return dot(a, b, ta, tb)

    def fwd(a, b, ta, tb):
        return dot(a, b, ta, tb), (a, b)

    def bwd(ta, tb, res, g):
        a, b = res
        if not ta and not tb:
            return mm(g, b, False, True), mm(a, g, True, False)
        if not ta and tb:
            return mm(g, b, False, False), mm(g, a, True, False)
        if ta and not tb:
            return mm(b, g, False, True), mm(a, g, False, False)
        raise NotImplementedError

    mm.defvjp(fwd, bwd)
    return mm


_mm = _make_mm(_dot)
_mm_f32 = _make_mm(_dot_f32)


def _rms(x, gain):
    r = lax.rsqrt(jnp.mean(x * x, axis=-1, keepdims=True) + EPS)
    return x * r * gain


def _rms_bwd(x, gain, dy):
    r = lax.rsqrt(jnp.mean(x * x, axis=-1, keepdims=True) + EPS)
    xh = x * r
    dgain = jnp.sum(dy * xh, axis=0, keepdims=True)
    dxh = dy * gain
    dx = r * (dxh - xh * jnp.mean(dxh * xh, axis=-1, keepdims=True))
    return dx, dgain


def _silu(x):
    return x / (1.0 + jnp.exp(-x))


def _acc_rows(ref, first, rows):
    @pl.when(first)
    def _():
        ref[...] = jnp.zeros_like(ref)

    for r, val in enumerate(rows):
        ref[r:r + 1, :] += val


def _norm_matmul(x, gain, w, out_dtype, name, tm=1024, tn=1024):
    n, d = x.shape
    g, _, wc = w.shape
    tm, tn = min(tm, n), min(tn, wc)
    per = wc // tn
    assert n % tm == 0 and wc % tn == 0

    def body(x_ref, gain_ref, w_ref, out_ref, h_ref):
        @pl.when(pl.program_id(1) == 0)
        def _():
            h_ref[...] = _rms(x_ref[...], gain_ref[...]).astype(BF16)

        out_ref[...] = jnp.dot(h_ref[...], w_ref[...], preferred_element_type=F32).astype(out_dtype)

    return pl.pallas_call(
        body, name=name, grid=(n // tm, g * per),
        in_specs=[pl.BlockSpec((tm, d), lambda i, j: (i, 0)),
                  pl.BlockSpec((1, d), lambda i, j: (0, 0)),
                  pl.BlockSpec((None, d, tn), lambda i, j: (j // per, 0, j % per))],
        out_specs=[pl.BlockSpec((tm, tn), lambda i, j: (i, j)),
                   pl.BlockSpec((tm, d), lambda i, j: (i, 0))],
        out_shape=[jax.ShapeDtypeStruct((n, g * wc), out_dtype), jax.ShapeDtypeStruct((n, d), BF16)],
        compiler_params=_params("parallel", "arbitrary"),
    )(x, gain, w)


def _norm_matmul_bwd(branches, x, add, name, tm=1024, ride=None):
    n, d = x.shape
    tm = min(tm, n)
    tns = [min(tn, w.shape[2]) for _, w, _, tn in branches]
    pers = [w.shape[2] // tn for (_, w, _, _), tn in zip(branches, tns)]
    ncols = [w.shape[0] * per for (_, w, _, _), per in zip(branches, pers)]
    offs = [sum(ncols[:b]) for b in range(len(branches))]
    total = sum(ncols)
    nb = len(branches)

    def body(*refs):
        dout_refs, w_refs, gain_refs = refs[:nb], refs[nb:2 * nb], refs[2 * nb:3 * nb]
        x_ref, add_ref, dx_ref, dgain_ref, acc_ref, sum_ref = refs[3 * nb:]
        i, j = pl.program_id(0), pl.program_id(1)

        @pl.when((i == 0) & (j == 0))
        def _():
            dgain_ref[...] = jnp.zeros_like(dgain_ref)

        for b in range(nb):
            first, last = offs[b], offs[b] + ncols[b] - 1

            @pl.when((j >= first) & (j <= last))
            def _(b=b, first=first):
                part = _dot(dout_refs[b][...], w_refs[b][...], tb=True)

                @pl.when(j == first)
                def _():
                    acc_ref[...] = part

                @pl.when(j > first)
                def _():
                    acc_ref[...] += part

            @pl.when(j == last)
            def _(b=b):
                xv = x_ref[...]
                xh = xv * lax.rsqrt(jnp.mean(xv * xv, axis=-1, keepdims=True) + EPS)
                dh = acc_ref[...]
                dgain_ref[b:b + 1, :] += jnp.sum(dh * xh, axis=0, keepdims=True)
                scaled = dh * gain_refs[b][...]
                if b == 0:
                    sum_ref[...] = scaled
                else:
                    sum_ref[...] += scaled

        @pl.when(j == total - 1)
        def _():
            xv = x_ref[...]
            r = lax.rsqrt(jnp.mean(xv * xv, axis=-1, keepdims=True) + EPS)
            xh = xv * r
            dxh = sum_ref[...]
            dx_ref[...] = add_ref[...] + r * (dxh - xh * jnp.mean(dxh * xh, axis=-1, keepdims=True))

    def dout_spec(b):
        return pl.BlockSpec((tm, tns[b]), lambda i, j: (i, jnp.clip(j - offs[b], 0, ncols[b] - 1)))

    def w_spec(b):
        def index(i, j):
            c = jnp.clip(j - offs[b], 0, ncols[b] - 1)
            return (c // pers[b], 0, c % pers[b])
        return pl.BlockSpec((None, d, tns[b]), index)

    row = lambda i, j: (i, 0)
    const = lambda i, j: (0, 0)
    nrow = n // tm
    (dx, dgain), got = _call(
        body, [b[0] for b in branches] + [b[1] for b in branches] + [b[2] for b in branches] + [x, add],
        name=name, grid=(nrow, total),
        in_specs=[dout_spec(b) for b in range(nb)] + [w_spec(b) for b in range(nb)]
                 + [pl.BlockSpec((1, d), const)] * nb + [pl.BlockSpec((tm, d), row), pl.BlockSpec((tm, d), row)],
        out_specs=[pl.BlockSpec((tm, d), row), pl.BlockSpec((8, d), const)],
        out_shape=[jax.ShapeDtypeStruct((n, d), F32), jax.ShapeDtypeStruct((8, d), F32)],
        scratch_shapes=[pltpu.VMEM((tm, d), F32), pltpu.VMEM((tm, d), F32)],
        compiler_params=_params("arbitrary", "arbitrary"),
        ride=ride, first=lambda: (pl.program_id(0) == 0) & (pl.program_id(1) == 0),
        last=lambda: (pl.program_id(0) == nrow - 1) & (pl.program_id(1) == total - 1))
    return (dx, dgain) if ride is None else (dx, dgain, got)


def _matmul_tn(a, b, groups, name, ta=1024, tb=1024, tk=1024):
    n, ka = a.shape
    _, kb = b.shape
    wc = kb // groups
    ta, tb, tk = min(ta, ka), min(tb, wc), min(tk, n)
    per = wc // tb
    nk = n // tk
    assert ka % ta == 0 and wc % tb == 0 and n % tk == 0

    def body(a_ref, b_ref, out_ref, acc_ref):
        k = pl.program_id(2)

        @pl.when(k == 0)
        def _():
            acc_ref[...] = jnp.zeros_like(acc_ref)

        acc_ref[...] += _dot(a_ref[...], b_ref[...], ta=True)

        @pl.when(k == nk - 1)
        def _():
            out_ref[...] = acc_ref[...].astype(BF16)

    return pl.pallas_call(
        body, name=name, grid=(ka // ta, groups * per, nk),
        in_specs=[pl.BlockSpec((tk, ta), lambda i, j, k: (k, i)),
                  pl.BlockSpec((tk, tb), lambda i, j, k: (k, j))],
        out_specs=pl.BlockSpec((None, ta, tb), lambda i, j, k: (j // per, i, j % per)),
        out_shape=jax.ShapeDtypeStruct((groups, ka, wc), BF16),
        scratch_shapes=[pltpu.VMEM((ta, tb), F32)],
        compiler_params=_params("parallel", "parallel", "arbitrary"),
    )(a, b)


def _gated_head_norm(o, gate, out_gain):
    parts = []
    for h in range(N_HEADS):
        sl = slice(h * HEAD_DIM, (h + 1) * HEAD_DIM)
        parts.append(_rms(o[:, sl], out_gain) * _silu(gate[:, sl]))
    return parts


def _out_proj(a_inputs, gated, x_in, w, gain, out_gain, name, tm=1024):
    n, d = x_in.shape
    k = w.shape[0]
    tm = min(tm, n)

    def body(*refs):
        if gated:
            o_ref, gate_ref, og_ref, x_ref, w_ref, gain_ref, xo_ref, y_ref, a_ref = refs
            parts = _gated_head_norm(o_ref[...], gate_ref[...], og_ref[...])
            for h, part in enumerate(parts):
                a_ref[:, h * HEAD_DIM:(h + 1) * HEAD_DIM] = part.astype(BF16)
            a = a_ref[...]
        else:
            a_in_ref, x_ref, w_ref, gain_ref, xo_ref, y_ref = refs
            a = a_in_ref[...]
        y = jnp.dot(a, w_ref[...], preferred_element_type=F32)
        y_ref[...] = y
        xo_ref[...] = x_ref[...] + _rms(y, gain_ref[...])

    row = lambda i: (i, 0)
    const = lambda i: (0, 0)
    if gated:
        a_specs = [pl.BlockSpec((tm, k), row), pl.BlockSpec((tm, D_MODEL), lambda i: (i, 3)),
                   pl.BlockSpec((1, HEAD_DIM), const)]
        a_args = list(a_inputs) + [out_gain]
    else:
        a_specs = [pl.BlockSpec((tm, k), row)]
        a_args = list(a_inputs)
    out_specs = [pl.BlockSpec((tm, d), row), pl.BlockSpec((tm, d), row)]
    out_shape = [jax.ShapeDtypeStruct((n, d), F32), jax.ShapeDtypeStruct((n, d), F32)]
    if gated:
        out_specs.append(pl.BlockSpec((tm, k), row))
        out_shape.append(jax.ShapeDtypeStruct((n, k), BF16))
    return pl.pallas_call(
        body, name=name, grid=(n // tm,),
        in_specs=a_specs + [pl.BlockSpec((tm, d), row), pl.BlockSpec((k, d), const), pl.BlockSpec((1, d), const)],
        out_specs=out_specs, out_shape=out_shape,
        compiler_params=_params("parallel"),
    )(*a_args, x_in, w, gain)


def _out_proj_bwd(dxo, y, w, gain, gated_inputs, name, tm=1024):
    n, d = dxo.shape
    k = w.shape[0]
    tm = min(tm, n)
    gated = gated_inputs is not None

    def body(*refs):
        if gated:
            (dxo_ref, y_ref, w_ref, gain_ref, o_ref, gate_ref, og_ref,
             dy_ref, dgain_ref, do_ref, dgate_ref, dog_ref) = refs
        else:
            dxo_ref, y_ref, w_ref, gain_ref, dy_ref, dgain_ref, da_ref = refs
        first = pl.program_id(0) == 0
        dy, dgain = _rms_bwd(y_ref[...], gain_ref[...], dxo_ref[...])
        dy_ref[...] = dy.astype(BF16)
        _acc_rows(dgain_ref, first, [dgain])
        da = _dot(dy_ref[...], w_ref[...], tb=True)
        if not gated:
            da_ref[...] = da.astype(BF16)
            return
        og = og_ref[...]
        dog = jnp.zeros_like(og)
        for h in range(N_HEADS):
            sl = slice(h * HEAD_DIM, (h + 1) * HEAD_DIM)
            fn = lambda o_h, g_h, gn: _rms(o_h, gn) * _silu(g_h)
            _, vjp = jax.vjp(fn, o_ref[:, sl], gate_ref[:, sl], og)
            do_h, dgate_h, dog_h = vjp(da[:, sl])
            do_ref[:, sl] = do_h
            dgate_ref[:, sl] = dgate_h.astype(BF16)
            dog = dog + dog_h
        _acc_rows(dog_ref, first, [dog])

    row = lambda i: (i, 0)
    const = lambda i: (0, 0)
    in_specs = [pl.BlockSpec((tm, d), row), pl.BlockSpec((tm, d), row), pl.BlockSpec((k, d), const),
                pl.BlockSpec((1, d), const)]
    args = [dxo, y, w, gain]
    out_specs = [pl.BlockSpec((tm, d), row), pl.BlockSpec((8, d), const)]
    out_shape = [jax.ShapeDtypeStruct((n, d), BF16), jax.ShapeDtypeStruct((8, d), F32)]
    if gated:
        in_specs += [pl.BlockSpec((tm, k), row), pl.BlockSpec((tm, D_MODEL), lambda i: (i, 3)),
                     pl.BlockSpec((1, HEAD_DIM), const)]
        args += list(gated_inputs)
        out_specs += [pl.BlockSpec((tm, k), row), pl.BlockSpec((tm, k), row), pl.BlockSpec((8, HEAD_DIM), const)]
        out_shape += [jax.ShapeDtypeStruct((n, k), F32), jax.ShapeDtypeStruct((n, k), BF16),
                      jax.ShapeDtypeStruct((8, HEAD_DIM), F32)]
    else:
        out_specs.append(pl.BlockSpec((tm, k), row))
        out_shape.append(jax.ShapeDtypeStruct((n, k), BF16))
    return pl.pallas_call(
        body, name=name, grid=(n // tm,), in_specs=in_specs, out_specs=out_specs, out_shape=out_shape,
        compiler_params=_params("arbitrary"),
    )(*args)


def _mlp_fwd(x_in, g_pre, w_up, w_down, g_post, name, tm=1024, target=None, ride=None):
    n, d = x_in.shape
    g, _, wc = w_up.shape
    tm = min(tm, n)
    with_loss = target is not None

    def body(*refs):
        if with_loss:
            x_ref, gpre_ref, wup_ref, wdown_ref, gpost_ref, t_ref, xo_ref, y_ref, h_ref, u_ref, a_ref, loss_ref, acc_ref = refs
        else:
            x_ref, gpre_ref, wup_ref, wdown_ref, gpost_ref, xo_ref, y_ref, h_ref, u_ref, a_ref, acc_ref = refs
        i, j = pl.program_id(0), pl.program_id(1)

        @pl.when(j == 0)
        def _():
            h_ref[...] = _rms(x_ref[...], gpre_ref[...]).astype(BF16)
            acc_ref[...] = jnp.zeros_like(acc_ref)

        u = jnp.dot(h_ref[...], wup_ref[...], preferred_element_type=F32).astype(BF16)
        u_ref[...] = u
        a = jnp.square(jnp.maximum(u, 0))
        a_ref[...] = a
        acc_ref[...] += jnp.dot(a, wdown_ref[...], preferred_element_type=F32)

        @pl.when(j == g - 1)
        def _():
            y = acc_ref[...]
            y_ref[...] = y
            xo = x_ref[...] + _rms(y, gpost_ref[...])
            if with_loss:
                err = xo - t_ref[...]
                xo_ref[...] = err * (1.0 / d)
                part = 0.5 * jnp.sum(jnp.mean(err * err, axis=-1, keepdims=True), axis=0, keepdims=True)
                _acc_rows(loss_ref, i == 0, [jnp.broadcast_to(part, (1, 128))])
            else:
                xo_ref[...] = xo

    row = lambda i, j: (i, 0)
    const = lambda i, j: (0, 0)
    in_specs = [pl.BlockSpec((tm, d), row), pl.BlockSpec((1, d), const),
                pl.BlockSpec((None, d, wc), lambda i, j: (j, 0, 0)),
                pl.BlockSpec((wc, d), lambda i, j: (j, 0)), pl.BlockSpec((1, d), const)]
    out_specs = [pl.BlockSpec((tm, d), row), pl.BlockSpec((tm, d), row), pl.BlockSpec((tm, d), row),
                 pl.BlockSpec((tm, wc), lambda i, j: (i, j)), pl.BlockSpec((tm, wc), lambda i, j: (i, j))]
    out_shape = [jax.ShapeDtypeStruct((n, d), F32), jax.ShapeDtypeStruct((n, d), F32),
                 jax.ShapeDtypeStruct((n, d), BF16), jax.ShapeDtypeStruct((n, g * wc), BF16),
                 jax.ShapeDtypeStruct((n, g * wc), BF16)]
    args = [x_in, g_pre, w_up, w_down, g_post]
    if with_loss:
        in_specs.append(pl.BlockSpec((tm, d), row))
        out_specs.append(pl.BlockSpec((8, 128), const))
        out_shape.append(jax.ShapeDtypeStruct((8, 128), F32))
        args.append(target)
    nrow = n // tm
    outs, got = _call(
        body, args, name=name, grid=(nrow, g), in_specs=in_specs, out_specs=out_specs, out_shape=out_shape,
        scratch_shapes=[pltpu.VMEM((tm, d), F32)], compiler_params=_params("arbitrary", "arbitrary"),
        ride=ride, first=lambda: (pl.program_id(0) == 0) & (pl.program_id(1) == 0),
        last=lambda: (pl.program_id(0) == nrow - 1) & (pl.program_id(1) == g - 1))
    return outs if ride is None else (outs, got)


def _mlp_bwd(dxo, y, g_post, w_down, u, w_up, x_in, g_pre, name, tm=1024, ride=None):
    n, d = dxo.shape
    g, _, wc = w_up.shape
    tm = min(tm, n)

    def body(dxo_ref, y_ref, gpost_ref, wdown_ref, u_ref, wup_ref, x_ref, gpre_ref,
             dx_ref, dy_ref, du_ref, dgain_ref, acc_ref, dgpost_ref):
        i, j = pl.program_id(0), pl.program_id(1)

        @pl.when(j == 0)
        def _():
            dy, dgpost = _rms_bwd(y_ref[...], gpost_ref[...], dxo_ref[...])
            dy_ref[...] = dy.astype(BF16)
            dgpost_ref[...] = dgpost
            acc_ref[...] = jnp.zeros_like(acc_ref)

        da = _dot(dy_ref[...], wdown_ref[...], tb=True)
        du = (da * (2.0 * jnp.maximum(u_ref[...], 0).astype(F32))).astype(BF16)
        du_ref[...] = du
        acc_ref[...] += _dot(du, wup_ref[...], tb=True)

        @pl.when(j == g - 1)
        def _():
            dx, dgpre = _rms_bwd(x_ref[...], gpre_ref[...], acc_ref[...])
            dx_ref[...] = dxo_ref[...] + dx
            _acc_rows(dgain_ref, i == 0, [dgpre, dgpost_ref[...]])

    row = lambda i, j: (i, 0)
    const = lambda i, j: (0, 0)
    nrow = n // tm
    outs, got = _call(
        body, [dxo, y, g_post, w_down, u, w_up, x_in, g_pre], name=name, grid=(nrow, g),
        in_specs=[pl.BlockSpec((tm, d), row), pl.BlockSpec((tm, d), row), pl.BlockSpec((1, d), const),
                  pl.BlockSpec((wc, d), lambda i, j: (j, 0)), pl.BlockSpec((tm, wc), lambda i, j: (i, j)),
                  pl.BlockSpec((None, d, wc), lambda i, j: (j, 0, 0)), pl.BlockSpec((tm, d), row),
                  pl.BlockSpec((1, d), const)],
        out_specs=[pl.BlockSpec((tm, d), row), pl.BlockSpec((tm, d), row),
                   pl.BlockSpec((tm, wc), lambda i, j: (i, j)), pl.BlockSpec((8, d), const)],
        out_shape=[jax.ShapeDtypeStruct((n, d), F32), jax.ShapeDtypeStruct((n, d), BF16),
                   jax.ShapeDtypeStruct((n, g * wc), BF16), jax.ShapeDtypeStruct((8, d), F32)],
        scratch_shapes=[pltpu.VMEM((tm, d), F32), pltpu.VMEM((1, d), F32)],
        compiler_params=_params("arbitrary", "arbitrary"),
        ride=ride, first=lambda: (pl.program_id(0) == 0) & (pl.program_id(1) == 0),
        last=lambda: (pl.program_id(0) == nrow - 1) & (pl.program_id(1) == g - 1))
    return outs if ride is None else (outs, got)


QKV = 3 * D_MODEL


def _shifted(x, prev8, s):
    if s == 0:
        return x
    tm = x.shape[0]
    rolled = pltpu.roll(x, s, 0)
    head = pltpu.roll(prev8, s, 0)
    head = jnp.concatenate([head, jnp.zeros((tm - 8, x.shape[1]), x.dtype)], axis=0)
    rows = lax.broadcasted_iota(jnp.int32, x.shape, 0)
    return jnp.where(rows < s, head, rolled)


def _conv(x, prev8, conv_w):
    out = x * conv_w[3:4, :]
    for s in range(1, CONV_WIDTH):
        out = out + _shifted(x, prev8, s) * conv_w[3 - s:4 - s, :]
    return out


def _l2norm(x):
    return x * lax.rsqrt(jnp.sum(x * x, axis=-1, keepdims=True) + EPS)


def _gdn_act_head(cq, ck, cv):
    return _l2norm(_silu(cq)) * (HEAD_DIM ** -0.5), _l2norm(_silu(ck)), _silu(cv)


def _gdn_gates(ba, alog, dtb):
    lane = lax.broadcasted_iota(jnp.int32, ba.shape, 1)
    beta = 1.0 / (1.0 + jnp.exp(-ba))
    t = ba + dtb
    softplus = jnp.maximum(t, 0.0) + jnp.log(1.0 + jnp.exp(-jnp.abs(t)))
    g = -jnp.exp(alog) * softplus
    return jnp.where(lane < N_HEADS, beta, jnp.where(lane < 2 * N_HEADS, g, 0.0))


def _qkv_cols(part, h):
    start = part * D_MODEL + h * HEAD_DIM
    return slice(start, start + HEAD_DIM)


def _prev_rows_spec(tm, cols, colblock, order):
    per = tm // 8
    return pl.BlockSpec((8, cols), lambda i: (jnp.maximum(order(i) * per - 1, 0), colblock))


def _gdn_pre(proj, conv_w, alog, dtb, name, tm=256):
    n = proj.shape[0]

    def body(x_ref, prev_ref, ba_ref, cw_ref, alog_ref, dtb_ref, q_ref, k_ref, v_ref, bg_ref, c_ref):
        first = pl.program_id(0) == 0
        for h in range(N_HEADS):
            convs = []
            for part in range(3):
                sl = _qkv_cols(part, h)
                prev8 = jnp.where(first, 0.0, prev_ref[:, sl])
                convs.append(_conv(x_ref[:, sl], prev8, cw_ref[:, sl]))
                c_ref[:, sl] = convs[-1]
            out = slice(h * HEAD_DIM, (h + 1) * HEAD_DIM)
            q_ref[:, out], k_ref[:, out], v_ref[:, out] = _gdn_act_head(*convs)
        bg_ref[...] = _gdn_gates(ba_ref[...], alog_ref[...], dtb_ref[...])

    row = lambda i: (i, 0)
    const = lambda i: (0, 0)
    ident = lambda i: i
    return pl.pallas_call(
        body, name=name, grid=(n // tm,),
        in_specs=[pl.BlockSpec((tm, QKV), row), _prev_rows_spec(tm, QKV, 0, ident),
                  pl.BlockSpec((tm, 128), lambda i: (i, 4 * D_MODEL // 128)),
                  pl.BlockSpec((CONV_WIDTH, QKV), const), pl.BlockSpec((1, 128), const), pl.BlockSpec((1, 128), const)],
        out_specs=[pl.BlockSpec((tm, D_MODEL), row)] * 3 + [pl.BlockSpec((tm, 128), row), pl.BlockSpec((tm, QKV), row)],
        out_shape=[jax.ShapeDtypeStruct((n, D_MODEL), F32)] * 3 + [jax.ShapeDtypeStruct((n, 128), F32),
                                                                  jax.ShapeDtypeStruct((n, QKV), F32)],
        compiler_params=_params("parallel"),
    )(proj, proj, proj, conv_w, alog, dtb)


def _gdn_pre_bwd(proj, conv, conv_w, alog, dtb, dq, dk, dv, dbg, dgate, name, tm=256):
    n = proj.shape[0]
    nt = n // tm

    def body(x_ref, c_ref, ba_ref, cw_ref, alog_ref, dtb_ref, dq_ref, dk_ref, dv_ref, dbg_ref, dgate_ref,
             dproj_ref, dcw_ref, dgates_ref, carry_ref, head_ref):
        step = pl.program_id(0)

        @pl.when(step == 0)
        def _():
            carry_ref[...] = jnp.zeros_like(carry_ref)
            head_ref[...] = jnp.zeros_like(head_ref)
            dcw_ref[...] = jnp.zeros_like(dcw_ref)

        rows = lax.broadcasted_iota(jnp.int32, (tm, HEAD_DIM), 0)
        rows8 = lax.broadcasted_iota(jnp.int32, (8, HEAD_DIM), 0)
        for h in range(N_HEADS):
            cols = [_qkv_cols(part, h) for part in range(3)]
            _, vjp = jax.vjp(_gdn_act_head, *[c_ref[:, sl] for sl in cols])
            out = slice(h * HEAD_DIM, (h + 1) * HEAD_DIM)
            dcs = vjp((dq_ref[:, out], dk_ref[:, out], dv_ref[:, out]))
            for sl, dc in zip(cols, dcs):
                x = x_ref[:, sl]
                cw = cw_ref[:, sl]
                dx = dc * cw[3:4, :]
                wrapped = jnp.zeros((8, HEAD_DIM), F32)
                dcw_ref[3:4, sl] += jnp.sum(dc * x, axis=0, keepdims=True)
                for s in range(1, CONV_WIDTH):
                    up = pltpu.roll(dc, tm - s, 0)
                    r = up * cw[3 - s:4 - s, :]
                    dx = dx + jnp.where(rows < tm - s, r, 0.0)
                    wrapped = wrapped + jnp.where(rows[tm - 8:, :] >= tm - s, r[tm - 8:, :], 0.0)
                    inside = jnp.sum(jnp.where(rows < tm - s, x * up, 0.0), axis=0, keepdims=True)
                    next_dc = pltpu.roll(head_ref[:, sl], 8 - s, 0)
                    across = jnp.sum(jnp.where(rows8 >= 8 - s, x[tm - 8:, :] * next_dc, 0.0), axis=0, keepdims=True)
                    dcw_ref[3 - s:4 - s, sl] += inside + across
                from_next = jnp.concatenate([jnp.zeros((tm - 8, HEAD_DIM), F32), carry_ref[:, sl]], axis=0)
                dproj_ref[:, sl] = (dx + from_next).astype(BF16)
                carry_ref[:, sl] = wrapped
                head_ref[:, sl] = dc[:8, :]

        _, vjp = jax.vjp(_gdn_gates, ba_ref[...], alog_ref[...], dtb_ref[...])
        dba, dalog, ddtb = vjp(dbg_ref[...])
        dproj_ref[:, QKV:4 * D_MODEL] = dgate_ref[...]
        dproj_ref[:, 4 * D_MODEL:] = dba.astype(BF16)
        _acc_rows(dgates_ref, step == 0, [dalog, ddtb])

    rev = lambda i: nt - 1 - i
    row = lambda i: (rev(i), 0)
    const = lambda i: (0, 0)
    return pl.pallas_call(
        body, name=name, grid=(nt,),
        in_specs=[pl.BlockSpec((tm, QKV), row), pl.BlockSpec((tm, QKV), row),
                  pl.BlockSpec((tm, 128), lambda i: (rev(i), 4 * D_MODEL // 128)),
                  pl.BlockSpec((CONV_WIDTH, QKV), const), pl.BlockSpec((1, 128), const), pl.BlockSpec((1, 128), const),
                  pl.BlockSpec((tm, D_MODEL), row), pl.BlockSpec((tm, D_MODEL), row), pl.BlockSpec((tm, D_MODEL), row),
                  pl.BlockSpec((tm, 128), row), pl.BlockSpec((tm, D_MODEL), row)],
        out_specs=[pl.BlockSpec((tm, GDN_IN_PAD), row), pl.BlockSpec((8, QKV), const), pl.BlockSpec((8, 128), const)],
        out_shape=[jax.ShapeDtypeStruct((n, GDN_IN_PAD), BF16), jax.ShapeDtypeStruct((8, QKV), F32),
                   jax.ShapeDtypeStruct((8, 128), F32)],
        scratch_shapes=[pltpu.VMEM((8, QKV), F32), pltpu.VMEM((8, QKV), F32)],
        compiler_params=_params("arbitrary"),
    )(proj, conv, proj, conv_w, alog, dtb, dq, dk, dv, dbg, dgate)


@jax.custom_vjp
def _unit_lower_inverses(lowers):
    n = lowers[0].shape[0]
    eye = (lax.broadcasted_iota(jnp.int32, (n, n), 0) == lax.broadcasted_iota(jnp.int32, (n, n), 1)).astype(F32)
    xs = [-low for low in lowers]
    ps = [eye + x for x in xs]
    ys = [_dot_3x(x, x) for x in xs]
    for _ in range(int(math.log2(CHUNK)) - 2):
        both = [_dot_3x(y, jnp.concatenate([y, p], axis=1)) for y, p in zip(ys, ps)]
        ys = [b[:, :n] for b in both]
        ps = [p + b[:, n:] for p, b in zip(ps, both)]
    return tuple(p + _dot_3x(y, p) for y, p in zip(ys, ps))


def _unit_lower_inverses_fwd(lowers):
    ts = _unit_lower_inverses(lowers)
    return ts, ts


def _unit_lower_inverses_bwd(ts, dts):
    left = [_dot_3x(t, dt, ta=True) for t, dt in zip(ts, dts)]
    return (tuple(-_dot_3x(l, t, tb=True) for l, t in zip(left, ts)),)


_unit_lower_inverses.defvjp(_unit_lower_inverses_fwd, _unit_lower_inverses_bwd)


@jax.custom_vjp
def _known_inverses(lowers, inverses):
    return inverses


def _known_inverses_bwd(ts, dts):
    (d_lowers,) = _unit_lower_inverses_bwd(ts, dts)
    return d_lowers, tuple(jnp.zeros_like(t) for t in ts)


_known_inverses.defvjp(lambda lowers, inverses: (inverses, inverses), _known_inverses_bwd)


@functools.partial(jax.custom_vjp, nondiff_argnums=(1,))
def _halves(x, axis):
    half = x.shape[axis] // 2
    return (x[:half], x[half:]) if axis == 0 else (x[:, :half], x[:, half:])


_halves.defvjp(lambda x, axis: (_halves(x, axis), None), lambda axis, _, g: (jnp.concatenate(g, axis=axis),))

GDN_STEP_CHUNKS = 2


def _gdn_chunks(qs, ks, vs, bgs, states, inverses=None, keep_inverses=False):
    c = CHUNK
    heads = range(N_HEADS)
    items = [(j, h) for j in range(len(bgs)) for h in heads]
    row = lax.broadcasted_iota(jnp.int32, (c, c), 0)
    col = lax.broadcasted_iota(jnp.int32, (c, c), 1)
    incl, strict, eye = row >= col, row > col, row == col
    lane = lax.broadcasted_iota(jnp.int32, (c, 128), 1)
    rowc = lax.broadcasted_iota(jnp.int32, (c, 1), 0)
    gc_all = [_mm_f32(incl.astype(F32), bg, False, False) for bg in bgs]
    q, k, v = ([xs[j][h] for j, h in items] for xs in (qs, ks, vs))
    n = range(len(items))
    beta = [jnp.sum(jnp.where(lane == h, bgs[j], 0.0), axis=1, keepdims=True) for j, h in items]
    gc = [jnp.sum(jnp.where(lane == N_HEADS + h, gc_all[j], 0.0), axis=1, keepdims=True) for j, h in items]
    gc_row = [jnp.sum(jnp.where(eye, gc[i], 0.0), axis=0, keepdims=True) for i in n]
    gc_last = [jnp.sum(jnp.where(rowc == c - 1, gc[i], 0.0), axis=0, keepdims=True) for i in n]
    decay = [jnp.where(incl, jnp.exp(jnp.where(incl, gc[i] - gc_row[i], 0.0)), 0.0) for i in n]
    kb = [k[i] * beta[i] for i in n]
    kb_q = [_halves(_mm(jnp.concatenate([kb[i], q[i]], axis=0), k[i], False, True), 0) for i in n]
    lower = tuple(jnp.where(strict, kb_q[i][0] * decay[i], 0.0) for i in n)
    attn = [kb_q[i][1] * decay[i] for i in n]
    t_mat = _unit_lower_inverses(lower) if inverses is None else _known_inverses(lower, inverses)
    egc = [jnp.exp(gc[i]) for i in n]
    w_u = [_halves(_mm(t_mat[i], jnp.concatenate([kb[i] * egc[i], v[i] * beta[i]], axis=1), False, False), 1)
           for i in n]
    w, u = [x[0] for x in w_u], [x[1] for x in w_u]
    qg = [q[i] * egc[i] for i in n]
    kg = [k[i] * jnp.exp(gc_last[i] - gc[i]) for i in n]
    outs, cur = [], list(states)
    for j in range(len(bgs)):
        at = lambda h: j * N_HEADS + h
        w_qg = [_halves(_mm(jnp.concatenate([w[at(h)], qg[at(h)]], axis=0), cur[h], False, False), 0)
                for h in heads]
        v_new = [u[at(h)] - w_qg[h][0] for h in heads]
        outs.append(tuple(w_qg[h][1] + _mm(attn[at(h)], v_new[h], False, False) for h in heads))
        cur = [cur[h] * jnp.exp(gc_last[at(h)]) + _mm(kg[at(h)], v_new[h], True, False) for h in heads]
    return (tuple(outs), tuple(cur), t_mat) if keep_inverses else (tuple(outs), tuple(cur))


def _chunk_head_slices(ref, nch):
    return tuple(tuple(ref[j * CHUNK:(j + 1) * CHUNK, h * HEAD_DIM:(h + 1) * HEAD_DIM] for h in range(N_HEADS))
                 for j in range(nch))


def _store_chunk_heads(ref, values):
    for j, chunk in enumerate(values):
        for h, val in enumerate(chunk):
            ref[j * CHUNK:(j + 1) * CHUNK, h * HEAD_DIM:(h + 1) * HEAD_DIM] = val


def _gdn_scan(qn, kn, v, bg, name, ride=None):
    n = qn.shape[0]
    nch = GDN_STEP_CHUNKS
    rows = nch * CHUNK
    nc = n // rows

    def body(q_ref, k_ref, v_ref, bg_ref, o_ref, saved_ref, inv_ref, state_ref):
        @pl.when(pl.program_id(0) == 0)
        def _():
            state_ref[...] = jnp.zeros_like(state_ref)

        states = tuple(state_ref[h] for h in range(N_HEADS))
        for h in range(N_HEADS):
            saved_ref[h] = states[h]
        bgs = tuple(bg_ref[j * CHUNK:(j + 1) * CHUNK, :] for j in range(nch))
        outs, new_states, inverses = _gdn_chunks(_chunk_head_slices(q_ref, nch), _chunk_head_slices(k_ref, nch),
                                                 _chunk_head_slices(v_ref, nch), bgs, states, keep_inverses=True)
        _store_chunk_heads(o_ref, outs)
        for h in range(N_HEADS):
            state_ref[h] = new_states[h]
        for item, inverse in enumerate(inverses):
            inv_ref[item] = inverse

    row = lambda i: (i, 0)
    return _call(
        body, [qn, kn, v, bg], name=name, grid=(nc,),
        in_specs=[pl.BlockSpec((rows, D_MODEL), row)] * 3 + [pl.BlockSpec((rows, 128), row)],
        out_specs=[pl.BlockSpec((rows, D_MODEL), row),
                   pl.BlockSpec((None, N_HEADS, HEAD_DIM, HEAD_DIM), lambda i: (i, 0, 0, 0)),
                   pl.BlockSpec((None, nch * N_HEADS, CHUNK, CHUNK), lambda i: (i, 0, 0, 0))],
        out_shape=[jax.ShapeDtypeStruct((n, D_MODEL), F32),
                   jax.ShapeDtypeStruct((nc, N_HEADS, HEAD_DIM, HEAD_DIM), F32),
                   jax.ShapeDtypeStruct((nc, nch * N_HEADS, CHUNK, CHUNK), F32)],
        scratch_shapes=[pltpu.VMEM((N_HEADS, HEAD_DIM, HEAD_DIM), F32)],
        compiler_params=_params("arbitrary"),
        ride=ride, first=lambda: pl.program_id(0) == 0, last=lambda: pl.program_id(0) == nc - 1)


def _gdn_scan_bwd(qn, kn, v, bg, saved, inverses, do, name, ride=None):
    n = qn.shape[0]
    nch = GDN_STEP_CHUNKS
    rows = nch * CHUNK
    nc = n // rows

    def body(q_ref, k_ref, v_ref, bg_ref, saved_ref, inv_ref, do_ref, dq_ref, dk_ref, dv_ref, dbg_ref, dstate_ref):
        @pl.when(pl.program_id(0) == 0)
        def _():
            dstate_ref[...] = jnp.zeros_like(dstate_ref)

        states = tuple(saved_ref[h] for h in range(N_HEADS))
        bgs = tuple(bg_ref[j * CHUNK:(j + 1) * CHUNK, :] for j in range(nch))
        known = tuple(inv_ref[item] for item in range(nch * N_HEADS))
        _, vjp = jax.vjp(functools.partial(_gdn_chunks, inverses=known), _chunk_head_slices(q_ref, nch),
                         _chunk_head_slices(k_ref, nch), _chunk_head_slices(v_ref, nch), bgs, states)
        dstates = tuple(dstate_ref[h] for h in range(N_HEADS))
        dqs, dks, dvs, dbgs, dprev = vjp((_chunk_head_slices(do_ref, nch), dstates))
        _store_chunk_heads(dq_ref, dqs)
        _store_chunk_heads(dk_ref, dks)
        _store_chunk_heads(dv_ref, dvs)
        for h in range(N_HEADS):
            dstate_ref[h] = dprev[h]
        for j in range(nch):
            dbg_ref[j * CHUNK:(j + 1) * CHUNK, :] = dbgs[j]

    row = lambda i: (nc - 1 - i, 0)
    return _call(
        body, [qn, kn, v, bg, saved, inverses, do], name=name, grid=(nc,),
        in_specs=[pl.BlockSpec((rows, D_MODEL), row)] * 3 + [pl.BlockSpec((rows, 128), row),
                  pl.BlockSpec((None, N_HEADS, HEAD_DIM, HEAD_DIM), lambda i: (nc - 1 - i, 0, 0, 0)),
                  pl.BlockSpec((None, nch * N_HEADS, CHUNK, CHUNK), lambda i: (nc - 1 - i, 0, 0, 0)),
                  pl.BlockSpec((rows, D_MODEL), row)],
        out_specs=[pl.BlockSpec((rows, D_MODEL), row)] * 3 + [pl.BlockSpec((rows, 128), row)],
        out_shape=[jax.ShapeDtypeStruct((n, D_MODEL), F32)] * 3 + [jax.ShapeDtypeStruct((n, 128), F32)],
        scratch_shapes=[pltpu.VMEM((N_HEADS, HEAD_DIM, HEAD_DIM), F32)],
        compiler_params=_params("arbitrary"),
        ride=ride, first=lambda: pl.program_id(0) == 0, last=lambda: pl.program_id(0) == nc - 1)


SB_BQ = 512
SB_BK = 256
SB_SUB = 128
SB_ROWS = 128
SB_SCALE = HEAD_DIM ** -0.5
SB_DEAD = -105.0


def _sb_terms(z, before):
    e = jnp.exp(-jnp.abs(z))
    log_beta = jnp.minimum(z, 0.0) - jnp.log(1.0 + e)
    log_1m = log_beta - z
    if before is not None:
        log_1m = jnp.where(before, log_1m, 0.0)
    return e, log_beta, log_1m


def _tri_ones(n, cmp):
    r = lax.broadcasted_iota(jnp.int32, (n, 2 * n), 0)
    c = lax.broadcasted_iota(jnp.int32, (n, 2 * n), 1)
    return jnp.where((c >= n) | cmp(r, c), 1.0, 0.0).astype(BF16)


def _sums(x, tri_ones):
    both = jnp.dot(x.astype(BF16), tri_ones, preferred_element_type=F32)
    n = x.shape[1]
    return both[:, :n], both[:, n:]


def _sb_chunk_mask(diagonal, r, s):
    if not diagonal or s * SB_SUB + SB_SUB - 1 < r * SB_ROWS:
        return None
    if s * SB_SUB >= r * SB_ROWS + SB_ROWS - 1:
        return "empty"
    rows = r * SB_ROWS + lax.broadcasted_iota(jnp.int32, (SB_ROWS, SB_SUB), 0)
    cols = s * SB_SUB + lax.broadcasted_iota(jnp.int32, (SB_ROWS, SB_SUB), 1)
    return cols < rows


def _sb_attention(q, kv, name, bq=SB_BQ, ride=None):
    n = q.shape[0]
    bq = min(bq, n)
    bk = min(SB_BK, bq)
    nrc = bq // SB_ROWS

    def body(q_ref, k_ref, v_ref, o_ref, l_ref, n_ref, z_scr, a_scr):
        i = pl.program_id(1)
        qb = q_ref[...]
        after = _tri_ones(SB_SUB, lambda r, c: r > c)

        def span(start, kw, c_sum, acc, diagonal):
            start = pl.multiple_of(start, kw)
            k_w = k_ref[pl.ds(start, kw), :]
            v_w = v_ref[pl.ds(start, kw), :]
            z_scr[:, :kw] = lax.dot_general(qb, k_w, _dims(False, True), preferred_element_type=F32)
            c_rows = [c_sum[r * SB_ROWS:(r + 1) * SB_ROWS] for r in range(nrc)]
            for s in reversed(range(kw // SB_SUB)):
                cols = slice(s * SB_SUB, (s + 1) * SB_SUB)
                for r in range(nrc):
                    rows = slice(r * SB_ROWS, (r + 1) * SB_ROWS)
                    before = _sb_chunk_mask(diagonal, r, s)
                    if isinstance(before, str):
                        a_scr[rows, cols] = jnp.zeros((SB_ROWS, SB_SUB), BF16)
                        continue
                    _, log_beta, log_1m = _sb_terms(z_scr[rows, cols] * SB_SCALE, before)
                    tail, total = _sums(log_1m, after)
                    a = jnp.exp(log_beta + c_rows[r] + tail)
                    if before is not None:
                        a = jnp.where(before, a, 0.0)
                    a_scr[rows, cols] = a.astype(BF16)
                    c_rows[r] = c_rows[r] + total
            acc = acc + jnp.dot(a_scr[:, :kw], v_w, preferred_element_type=F32)
            return jnp.concatenate(c_rows, axis=0), acc

        c_sum, acc = span(i * bq, bq, jnp.zeros((bq, SB_SUB), F32), jnp.zeros((bq, HEAD_DIM), F32), True)

        def more(state):
            done, c, _ = state
            return (done < i * (bq // bk)) & (jnp.max(c) > SB_DEAD)

        def step(state):
            done, c, a = state
            c, a = span(i * bq - (done + 1) * bk, bk, c, a, False)
            return done + 1, c, a

        done, c_sum, acc = lax.while_loop(more, step, (jnp.int32(0), c_sum, acc))
        o_ref[...] = acc.astype(BF16)
        l_ref[...] = c_sum
        n_ref[...] = jnp.full((8, 128), done.astype(F32), F32)

    nq = n // bq
    return _call(
        body, [q, kv, kv], name=name, grid=(N_HEADS, nq),
        in_specs=[pl.BlockSpec((bq, HEAD_DIM), lambda h, i: (i, h)),
                  pl.BlockSpec((n, HEAD_DIM), lambda h, i: (0, h)),
                  pl.BlockSpec((n, HEAD_DIM), lambda h, i: (0, N_HEADS + h))],
        out_specs=[pl.BlockSpec((bq, HEAD_DIM), lambda h, i: (i, h)),
                   pl.BlockSpec((None, bq, 128), lambda h, i: (h, i, 0)),
                   pl.BlockSpec((None, None, 8, 128), lambda h, i: (h, i, 0, 0))],
        out_shape=[jax.ShapeDtypeStruct((n, D_MODEL), BF16), jax.ShapeDtypeStruct((N_HEADS, n, 128), F32),
                   jax.ShapeDtypeStruct((N_HEADS, nq, 8, 128), F32)],
        scratch_shapes=[pltpu.VMEM((bq, bq), F32), pltpu.VMEM((bq, bq), BF16)],
        compiler_params=_params("arbitrary", "arbitrary"),
        ride=ride, first=lambda: (pl.program_id(0) == 0) & (pl.program_id(1) == 0),
        last=lambda: (pl.program_id(0) == N_HEADS - 1) & (pl.program_id(1) == nq - 1))


def _sb_attention_bwd(q, kv, do, lsum, spans, name, bq=SB_BQ, ride=None):
    n = q.shape[0]
    bq = min(bq, n)
    bk = min(SB_BK, bq)
    nrc = bq // SB_ROWS

    def body(q_ref, k_ref, v_ref, do_ref, l_ref, n_ref, dq_ref, dk_out, dv_out, z_scr, da_scr, a_scr, dz_scr, dk_ref, dv_ref):
        i = pl.program_id(1)

        @pl.when(i == 0)
        def _():
            dk_ref[...] = jnp.zeros_like(dk_ref)
            dv_ref[...] = jnp.zeros_like(dv_ref)

        qb = q_ref[...]
        dob = do_ref[...]
        lt_rows = [l_ref[r * SB_ROWS:(r + 1) * SB_ROWS, :] for r in range(nrc)]
        upto = _tri_ones(SB_SUB, lambda r, c: r <= c)
        below = _tri_ones(SB_SUB, lambda r, c: r < c)

        def span(start, kw, l_sum, g_sum, dq, diagonal):
            start = pl.multiple_of(start, kw)
            k_w = k_ref[pl.ds(start, kw), :]
            v_w = v_ref[pl.ds(start, kw), :]
            z_scr[:, :kw] = lax.dot_general(qb, k_w, _dims(False, True), preferred_element_type=F32)
            da_scr[:, :kw] = lax.dot_general(dob, v_w, _dims(False, True), preferred_element_type=F32)
            l_rows = [l_sum[r * SB_ROWS:(r + 1) * SB_ROWS] for r in range(nrc)]
            g_rows = [g_sum[r * SB_ROWS:(r + 1) * SB_ROWS] for r in range(nrc)]
            for s in range(kw // SB_SUB):
                cols = slice(s * SB_SUB, (s + 1) * SB_SUB)
                for r in range(nrc):
                    rows = slice(r * SB_ROWS, (r + 1) * SB_ROWS)
                    before = _sb_chunk_mask(diagonal, r, s)
                    if isinstance(before, str):
                        a_scr[rows, cols] = jnp.zeros((SB_ROWS, SB_SUB), BF16)
                        dz_scr[rows, cols] = jnp.zeros((SB_ROWS, SB_SUB), BF16)
                        continue
                    zs = z_scr[rows, cols] * SB_SCALE
                    e, log_beta, log_1m = _sb_terms(zs, before)
                    l_prefix, l_total = _sums(log_1m, upto)
                    a = jnp.exp(log_beta + (lt_rows[r] - (l_rows[r] + l_prefix)))
                    if before is not None:
                        a = jnp.where(before, a, 0.0)
                    g = a * da_scr[rows, cols]
                    g_prefix, g_total = _sums(g, below)
                    inv = 1.0 / (1.0 + e)
                    beta = jnp.where(zs >= 0, inv, e * inv)
                    dz = (g - (g + g_rows[r] + g_prefix) * beta) * SB_SCALE
                    if before is not None:
                        dz = jnp.where(before, dz, 0.0)
                    a_scr[rows, cols] = a.astype(BF16)
                    dz_scr[rows, cols] = dz.astype(BF16)
                    l_rows[r] = l_rows[r] + l_total
                    g_rows[r] = g_rows[r] + g_total
            dz_w = dz_scr[:, :kw]
            dq = dq + jnp.dot(dz_w, k_w, preferred_element_type=F32)
            dk_ref[pl.ds(start, kw), :] += lax.dot_general(dz_w, qb, _dims(True, False), preferred_element_type=F32)
            dv_ref[pl.ds(start, kw), :] += lax.dot_general(a_scr[:, :kw], dob, _dims(True, False),
                                                           preferred_element_type=F32)
            return jnp.concatenate(l_rows, axis=0), jnp.concatenate(g_rows, axis=0), dq

        taken = jnp.clip(jnp.max(n_ref[...]).astype(jnp.int32), 0, i * (bq // bk))
        zero = jnp.zeros((bq, SB_SUB), F32)
        carry = lax.fori_loop(0, taken, lambda j, c: span(i * bq - (taken - j) * bk, bk, c[0], c[1], c[2], False),
                              (zero, zero, jnp.zeros((bq, HEAD_DIM), F32)))
        _, _, dq = span(i * bq, bq, carry[0], carry[1], carry[2], True)
        dq_ref[...] = dq.astype(BF16)

        @pl.when(i == nq - 1)
        def _():
            dk_out[...] = dk_ref[...].astype(BF16)
            dv_out[...] = dv_ref[...].astype(BF16)

    nq = n // bq
    return _call(
        body, [q, kv, kv, do, lsum, spans], name=name, grid=(N_HEADS, nq),
        in_specs=[pl.BlockSpec((bq, HEAD_DIM), lambda h, i: (i, h)),
                  pl.BlockSpec((n, HEAD_DIM), lambda h, i: (0, h)),
                  pl.BlockSpec((n, HEAD_DIM), lambda h, i: (0, N_HEADS + h)),
                  pl.BlockSpec((bq, HEAD_DIM), lambda h, i: (i, h)),
                  pl.BlockSpec((None, bq, 128), lambda h, i: (h, i, 0)),
                  pl.BlockSpec((None, None, 8, 128), lambda h, i: (h, i, 0, 0))],
        out_specs=[pl.BlockSpec((bq, HEAD_DIM), lambda h, i: (i, h)),
                   pl.BlockSpec((n, HEAD_DIM), lambda h, i: (0, h)),
                   pl.BlockSpec((n, HEAD_DIM), lambda h, i: (0, h))],
        out_shape=[jax.ShapeDtypeStruct((n, D_MODEL), BF16)] * 3,
        scratch_shapes=[pltpu.VMEM((bq, bq), F32), pltpu.VMEM((bq, bq), F32), pltpu.VMEM((bq, bq), BF16),
                        pltpu.VMEM((bq, bq), BF16), pltpu.VMEM((n, HEAD_DIM), F32), pltpu.VMEM((n, HEAD_DIM), F32)],
        compiler_params=_params("arbitrary", "arbitrary"),
        ride=ride, first=lambda: (pl.program_id(0) == 0) & (pl.program_id(1) == 0),
        last=lambda: (pl.program_id(0) == N_HEADS - 1) & (pl.program_id(1) == nq - 1))


def _local_step(x, target, gains, comm):
    g = gains
    w = comm.w
    big = {}
    row = lambda a, i: a[i:i + 1, :]

    proj, h0 = _norm_matmul(x, row(g["mix_pre"], 0), w["gdn_in"], F32, "gdn_in_proj", tn=1408)
    qn, kn, v, bg, conv = _gdn_pre(proj, w["conv"], g["alog"], g["dtb"], "gdn_pre")
    (o_gdn, saved, inverses), got = _gdn_scan(qn, kn, v, bg, "gdn_scan", ride=comm.ride("scan"))
    comm.done("scan", got)
    x1, y_mix0, a_gdn = _out_proj((o_gdn, proj), True, x, w["gdn_out"], row(g["mix_post"], 0), g["out_gain"],
                                  "gdn_out_proj", tm=512)
    ride = comm.ride("mlp0")
    res = _mlp_fwd(x1, row(g["mlp_pre"], 0), w["up0"], w["down0"], row(g["mlp_post"], 0), "mlp0", ride=ride)
    (x2, y_mlp0, h_mlp0, u0, a0), got = res if ride is not None else (res, [])
    comm.done("mlp0", got)
    kv, h_kv = _norm_matmul(x2, g["kv"], w["kv"], BF16, "kv_proj", tm=2048)
    q, h_q = _norm_matmul(x2, row(g["mix_pre"], 1), w["sb_q"], BF16, "sb_q_proj")
    (o_sb, lsum, spans), got = _sb_attention(q, kv, "sb_attention", ride=comm.ride("sb"))
    comm.done("sb", got)
    x3, y_mix1 = _out_proj((o_sb,), False, x2, w["sb_o"], row(g["mix_post"], 1), None, "sb_out_proj")
    dx4, y_mlp1, h_mlp1, u1, a1, loss = _mlp_fwd(x3, row(g["mlp_pre"], 1), w["up1"], w["down1"], row(g["mlp_post"], 1),
                                                 "mlp1_loss", target=target)

    dx3, dy_mlp1, du1, dg_mlp1 = _mlp_bwd(dx4, y_mlp1, row(g["mlp_post"], 1), w["down1"], u1, w["up1"], x3,
                                          row(g["mlp_pre"], 1), "mlp1_bwd")
    big["down1"] = _matmul_tn(a1, dy_mlp1, 1, "d_down1")[0]
    big["up1"] = _matmul_tn(h_mlp1, du1, N_DEV, "d_up1", tk=2048)
    dy_mix1, dg_post1, do_sb = _out_proj_bwd(dx3, y_mix1, w["sb_o"], row(g["mix_post"], 1), None, "sb_out_proj_bwd")
    big["sb_o"] = _matmul_tn(o_sb, dy_mix1, 1, "d_sb_o")[0]
    (dq, dk, dv), got = _sb_attention_bwd(q, kv, do_sb, lsum, spans, "sb_attention_bwd", ride=comm.ride("sb_bwd", big))
    comm.done("sb_bwd", got)
    big["sb_q"] = _matmul_tn(h_q, dq, 1, "d_sb_q")[0]
    big["kv"] = jnp.concatenate([_matmul_tn(h_kv, dk, N_DEV // 2, "d_w_k", tk=2048),
                                 _matmul_tn(h_kv, dv, N_DEV // 2, "d_w_v", tk=2048)],
                                axis=0)
    dx2, dg_x2 = _norm_matmul_bwd([(dq, w["sb_q"], row(g["mix_pre"], 1), 1024), (dk, w["kv"][0:1], g["kv"], 1024),
                                   (dv, w["kv"][1:2], g["kv"], 1024)], x2, dx3, "qkv_proj_bwd", tm=512)

    big.update(x2=dg_x2, post1=dg_post1, mlp1=dg_mlp1, loss=loss)
    ride = comm.ride("mlp0_bwd", big)
    res = _mlp_bwd(dx2, y_mlp0, row(g["mlp_post"], 0), w["down0"], u0, w["up0"], x1, row(g["mlp_pre"], 0), "mlp0_bwd",
                   ride=ride)
    (dx1, dy_mlp0, du0, dg_mlp0), got = res if ride is not None else (res, [])
    comm.done("mlp0_bwd", got)
    big["down0"] = _matmul_tn(a0, dy_mlp0, 1, "d_down0")[0]
    big["up0"] = _matmul_tn(h_mlp0, du0, N_DEV, "d_up0", tk=2048)
    dy_mix0, dg_post0, do_gdn, dgate, d_out_gain = _out_proj_bwd(
        dx1, y_mix0, w["gdn_out"], row(g["mix_post"], 0), (o_gdn, proj, g["out_gain"]), "gdn_out_proj_bwd", tm=512)
    big["gdn_out"] = _matmul_tn(a_gdn, dy_mix0, 1, "d_gdn_out")[0]
    big.update(post0=dg_post0, mlp0=dg_mlp0, out_gain=d_out_gain)
    (dqn, dkn, dvv, dbg), got = _gdn_scan_bwd(qn, kn, v, bg, saved, inverses, do_gdn, "gdn_scan_bwd",
                                              ride=comm.ride("scan_bwd", big))
    comm.done("scan_bwd", got)
    dproj, d_conv, d_gates = _gdn_pre_bwd(proj, conv, w["conv"], g["alog"], g["dtb"], dqn, dkn, dvv, dbg, dgate,
                                          "gdn_pre_bwd")
    big["gdn_in"] = _matmul_tn(h0, dproj, 1, "d_gdn_in", tb=1408)[0]
    big.update(conv=d_conv, gates=d_gates)
    ride = comm.ride("in_bwd", big)
    grad_x, dg_pre0, *got = _norm_matmul_bwd([(dproj, w["gdn_in"], row(g["mix_pre"], 0), 1408)], x, dx1, "gdn_in_proj_bwd",
                                             ride=ride)
    comm.done("in_bwd", got[0] if got else [])

    small = {"pre0": dg_pre0, "x2": dg_x2, "post0": dg_post0, "post1": dg_post1, "mlp0": dg_mlp0, "mlp1": dg_mlp1,
             "gates": d_gates, "out_gain": d_out_gain, "conv": d_conv, "loss": loss}
    return grad_x, big, small


GATHER_STAGES = {"scan": ("gdn_out", "up0", "down0"), "mlp0": ("kv", "sb_q"), "sb": ("sb_o", "up1", "down1")}
SCATTER_STAGES = {"sb_bwd": ("up1", "down1", "sb_o"), "mlp0_bwd": ("sb_q", "kv"), "scan_bwd": ("up0", "down0", "gdn_out"),
                  "in_bwd": ("gdn_in",)}
SMALL_STAGES = {"mlp0_bwd": ("x2", "post1", "mlp1", "loss"), "scan_bwd": ("post0", "mlp0", "out_gain"),
                "in_bwd": ("conv", "gates")}


def _whole_weight(name, gathered):
    d = D_MODEL
    if name in ("up0", "up1"):
        return gathered
    if name == "kv":
        return gathered.reshape(2, N_DEV // 2, d, gathered.shape[2]).transpose(0, 2, 1, 3).reshape(2, d, d)
    if name == "gdn_in":
        whole = gathered.transpose(1, 0, 2).reshape(d, GDN_IN_COLS)
        return jnp.pad(whole, ((0, 0), (0, GDN_IN_PAD - GDN_IN_COLS)))[None]
    if name == "conv":
        return gathered.transpose(1, 0, 2).reshape(CONV_WIDTH, QKV)
    whole = gathered.reshape(gathered.shape[0] * gathered.shape[1], d)
    return whole[None] if name == "sb_q" else whole


def _owner_blocks(name, grad):
    if name in ("up0", "up1", "kv"):
        blocks = grad
    elif name == "gdn_in":
        blocks = grad[:, :GDN_IN_COLS].reshape(D_MODEL, N_DEV, GDN_IN_COLS // N_DEV).transpose(1, 0, 2)
    else:
        blocks = grad.reshape(N_DEV, grad.shape[0] // N_DEV, grad.shape[1])
    return blocks.astype(BF16)


class _Fsdp:
    def __init__(self, shards, weights):
        self.shards, self.w, self.landed, self.small_landed = shards, weights, {}, {}

    def ride(self, stage, grads=None):
        if stage in GATHER_STAGES:
            names = GATHER_STAGES[stage]
            return _Exchange([self.shards[nm] for nm in names], [True] * len(names))
        names, small = SCATTER_STAGES[stage], SMALL_STAGES.get(stage, ())
        return _Exchange([_owner_blocks(nm, grads[nm]) for nm in names] + [grads[nm] for nm in small],
                         [False] * len(names) + [True] * len(small))

    def done(self, stage, outs):
        if stage in GATHER_STAGES:
            for nm, out in zip(GATHER_STAGES[stage], outs):
                self.w[nm] = _whole_weight(nm, out)
        else:
            names = SCATTER_STAGES[stage]
            self.landed.update(zip(names, outs[:len(names)]))
            self.small_landed.update(zip(SMALL_STAGES.get(stage, ()), outs[len(names):]))


def _my_place():
    return lax.axis_index("x"), lax.axis_index("y"), lax.axis_index("c")


class _Exchange:
    def __init__(self, arrays, gather):
        self.arrays, self.gather, self.n = list(arrays), list(gather), len(arrays)
        any_spec = pl.BlockSpec(memory_space=pl.ANY)
        self.in_specs = [any_spec] * self.n
        self.out_specs = [any_spec] * self.n
        self.out_shape = [jax.ShapeDtypeStruct(((N_DEV,) + a.shape) if g else a.shape, a.dtype)
                          for a, g in zip(self.arrays, self.gather)]
        self.scratch = [pltpu.SemaphoreType.DMA((self.n, N_DEV - 1)), pltpu.SemaphoreType.DMA((self.n, N_DEV - 1)),
                        pltpu.SemaphoreType.DMA((self.n,))]

    def _copies(self, ins, outs, sems):
        send_sems, recv_sems, local_sems = sems
        x, y, c = _my_place()
        me = 4 * x + 2 * y + c
        copies = []
        for a in range(self.n):
            src = ins[a] if self.gather[a] else ins[a].at[me]
            copies.append(pltpu.make_async_copy(src, outs[a].at[me], local_sems.at[a]))
        for k in range(1, N_DEV):
            px = 1 - x if k & 4 else x
            py = 1 - y if k & 2 else y
            pc = 1 - c if k & 1 else c
            peer = 4 * px + 2 * py + pc
            for a in range(self.n):
                src = ins[a] if self.gather[a] else ins[a].at[peer]
                copies.append(pltpu.make_async_remote_copy(
                    src_ref=src, dst_ref=outs[a].at[me], send_sem=send_sems.at[a, k - 1], recv_sem=recv_sems.at[a, k - 1],
                    device_id=(px, py, pc), device_id_type=MESH))
        return copies

    def start(self, ins, outs, sems):
        for cp in self._copies(ins, outs, sems):
            cp.start()

    def wait(self, ins, outs, sems):
        for cp in self._copies(ins, outs, sems):
            cp.wait()


def _call(body, operands, *, name, out_shape, in_specs, out_specs, grid=(), scratch_shapes=(), compiler_params=None,
          ride=None, first=None, last=None):
    n_in, n_out, n_scr = len(operands), len(out_shape), len(scratch_shapes)
    if ride is None:
        outs = pl.pallas_call(body, name=name, grid=grid, in_specs=in_specs, out_specs=out_specs, out_shape=out_shape,
                              scratch_shapes=scratch_shapes, compiler_params=compiler_params)(*operands)
        return list(outs), []
    r = ride.n

    def riding(*refs):
        ins, r_ins = refs[:n_in], refs[n_in:n_in + r]
        outs, r_outs = refs[n_in + r:n_in + r + n_out], refs[n_in + r + n_out:n_in + 2 * r + n_out]
        scr, sems = refs[n_in + 2 * r + n_out:n_in + 2 * r + n_out + n_scr], refs[n_in + 2 * r + n_out + n_scr:]

        @pl.when(first())
        def _():
            ride.start(r_ins, r_outs, sems)

        body(*ins, *outs, *scr)

        @pl.when(last())
        def _():
            ride.wait(r_ins, r_outs, sems)

    outs = pl.pallas_call(
        riding, name=name, grid=grid, in_specs=list(in_specs) + ride.in_specs, out_specs=list(out_specs) + ride.out_specs,
        out_shape=list(out_shape) + ride.out_shape, scratch_shapes=list(scratch_shapes) + ride.scratch,
        compiler_params=compiler_params)(*operands, *ride.arrays)
    return list(outs[:n_out]), list(outs[n_out:])


def _exchange(arrays, gather, name):
    ride = _Exchange(arrays, gather)

    def body(*refs):
        ins, outs, sems = refs[:ride.n], refs[ride.n:2 * ride.n], refs[2 * ride.n:]
        ride.start(ins, outs, sems)
        ride.wait(ins, outs, sems)

    return pl.pallas_call(body, name=name, in_specs=ride.in_specs, out_specs=ride.out_specs, out_shape=ride.out_shape,
                          scratch_shapes=ride.scratch)(*arrays)


def _gather_two_level(arrays, name):
    n_arr = len(arrays)

    def body(*refs):
        ins, outs = refs[:n_arr], refs[n_arr:2 * n_arr]
        send_sems, recv_sems, local_sems = refs[2 * n_arr:]
        x, y, c = _my_place()
        sibling = (x, y, 1 - c)
        chips = [(1 - x, y), (x, 1 - y), (1 - x, 1 - y)]
        index = lambda px, py, pc: 4 * px + 2 * py + pc

        def copy(a, k, block, to, src=None):
            dst = outs[a].at[index(*block)]
            return pltpu.make_async_remote_copy(src_ref=dst if src is None else src, dst_ref=dst,
                                                send_sem=send_sems.at[a, k], recv_sem=recv_sems.at[a, k],
                                                device_id=to, device_id_type=MESH)

        mine = [pltpu.make_async_copy(ins[a], outs[a].at[index(x, y, c)], local_sems.at[a]) for a in range(n_arr)]
        first = [copy(a, 0, (x, y, c), sibling, src=ins[a]) for a in range(n_arr)]
        first += [copy(a, 1 + j, (x, y, c), (*chip, c), src=ins[a]) for j, chip in enumerate(chips) for a in range(n_arr)]
        for cp in mine + first:
            cp.start()
        passed = []
        for j, chip in enumerate(chips):
            for a in range(n_arr):
                copy(a, 1 + j, (*chip, c), (x, y, c)).wait_recv()
                passed.append(copy(a, 4 + j, (*chip, c), sibling))
                passed[-1].start()
        for a in range(n_arr):
            copy(a, 0, sibling, (x, y, c)).wait_recv()
            for j, chip in enumerate(chips):
                copy(a, 4 + j, (*chip, 1 - c), (x, y, c)).wait_recv()
        for cp in first + passed:
            cp.wait_send()
        for cp in mine:
            cp.wait()

    any_spec = pl.BlockSpec(memory_space=pl.ANY)
    return pl.pallas_call(
        body, name=name, in_specs=[any_spec] * n_arr, out_specs=[any_spec] * n_arr,
        out_shape=[jax.ShapeDtypeStruct((N_DEV,) + a.shape, a.dtype) for a in arrays],
        scratch_shapes=[pltpu.SemaphoreType.DMA((n_arr, N_DEV - 1)), pltpu.SemaphoreType.DMA((n_arr, N_DEV - 1)),
                        pltpu.SemaphoreType.DMA((n_arr,))])(*arrays)


def _adamw_math(g, w, m, v):
    m = ADAM_B1 * m + (1.0 - ADAM_B1) * g
    v = ADAM_B2 * v + (1.0 - ADAM_B2) * jnp.square(g)
    m_hat = m / (1.0 - ADAM_B1 ** ADAM_STEP)
    v_hat = v / (1.0 - ADAM_B2 ** ADAM_STEP)
    delta = -ADAM_LR * (m_hat / (jnp.sqrt(v_hat) + ADAM_EPS) + ADAM_WD * w)
    return delta, m, v


def _sum_devices(ref):
    total = ref[0].astype(F32)
    for d in range(1, N_DEV):
        total = total + ref[d].astype(F32)
    return total


def _reduce_adamw(landed, w, m, v, name, tr=256):
    r, c = w.shape
    tr = min(tr, r)
    assert r % tr == 0

    def body(l_ref, w_ref, m_ref, v_ref, g_ref, d_ref, nm_ref, nv_ref):
        g = _sum_devices(l_ref)
        g_ref[...] = g
        d_ref[...], nm_ref[...], nv_ref[...] = _adamw_math(g, w_ref[...], m_ref[...], v_ref[...])

    blk = pl.BlockSpec((tr, c), lambda i: (i, 0))
    return pl.pallas_call(
        body, name=name, grid=(r // tr,),
        in_specs=[pl.BlockSpec((N_DEV, tr, c), lambda i: (0, i, 0)), blk, blk, blk],
        out_specs=[blk] * 4, out_shape=[jax.ShapeDtypeStruct((r, c), F32)] * 4,
        compiler_params=_params("parallel"),
    )(landed, w, m, v)


def _reduce_adamw_layers(landed, w, m, v, name, tr=256):
    _, r, c = w.shape
    tr = min(tr, r)
    nr = r // tr
    assert r % tr == 0

    def body(l0_ref, l1_ref, w_ref, m_ref, v_ref, g_ref, d_ref, nm_ref, nv_ref):
        g = jnp.where(pl.program_id(0) == 0, _sum_devices(l0_ref), _sum_devices(l1_ref))
        g_ref[...] = g
        d_ref[...], nm_ref[...], nv_ref[...] = _adamw_math(g, w_ref[...], m_ref[...], v_ref[...])

    blk = pl.BlockSpec((None, tr, c), lambda l, i: (l, i, 0))
    return pl.pallas_call(
        body, name=name, grid=(2, nr),
        in_specs=[pl.BlockSpec((N_DEV, tr, c), lambda l, i: (0, jnp.where(l == 0, i, nr - 1), 0)),
                  pl.BlockSpec((N_DEV, tr, c), lambda l, i: (0, jnp.where(l == 1, i, 0), 0)), blk, blk, blk],
        out_specs=[blk] * 4, out_shape=[jax.ShapeDtypeStruct(w.shape, F32)] * 4,
        compiler_params=_params("arbitrary", "arbitrary"),
    )(landed[0], landed[1], w, m, v)


def _adamw(g, w, m, v, name):
    def body(g_ref, w_ref, m_ref, v_ref, d_ref, nm_ref, nv_ref):
        d_ref[...], nm_ref[...], nv_ref[...] = _adamw_math(g_ref[...], w_ref[...], m_ref[...], v_ref[...])

    return pl.pallas_call(body, name=name, out_shape=[jax.ShapeDtypeStruct(w.shape, F32)] * 3)(g, w, m, v)


def _small_update(landed, params, name):
    layout = {
        "mix_pre": [("pre0", 0), ("x2", 0)], "mix_post": [("post0", 0), ("post1", 0)],
        "mlp_pre": [("mlp0", 0), ("mlp1", 0)], "mlp_post": [("mlp0", 1), ("mlp1", 1)],
        "kv": [("kv", 0)], "alog": [("gates", 0)], "dtb": [("gates", 1)], "out_gain": [("out_gain", 0)],
    }
    landed_names = sorted(landed)
    param_names = sorted(params)
    n_l, n_p = len(landed_names), len(param_names)

    def body(*refs):
        l_refs = dict(zip(landed_names, refs[:n_l]))
        p_refs = {p: refs[n_l + 3 * i:n_l + 3 * i + 3] for i, p in enumerate(param_names)}
        outs = refs[n_l + 3 * n_p:]
        o_refs = {p: outs[4 * i:4 * i + 4] for i, p in enumerate(param_names)}
        conv_ref, loss_ref = outs[4 * n_p:]
        sums = {nm: _sum_devices(l_refs[nm]) for nm in landed_names}
        sums["kv"] = sums["x2"][1:2, :] + sums["x2"][2:3, :]
        for p in param_names:
            w_ref, m_ref, v_ref = p_refs[p]
            g_ref, d_ref, nm_ref, nv_ref = o_refs[p]
            for r, (src, src_row) in enumerate(layout[p]):
                g = sums[src][src_row:src_row + 1, :]
                g_ref[r:r + 1, :] = g
                d, nm, nv = _adamw_math(g, w_ref[r:r + 1, :], m_ref[r:r + 1, :], v_ref[r:r + 1, :])
                d_ref[r:r + 1, :] = d
                nm_ref[r:r + 1, :] = nm
                nv_ref[r:r + 1, :] = nv
        conv_ref[...] = sums["conv"]
        loss_ref[...] = sums["loss"]

    args = [landed[nm] for nm in landed_names]
    out_shape = []
    for p in param_names:
        args += list(params[p])
        out_shape += [jax.ShapeDtypeStruct(params[p][0].shape, F32)] * 4
    out_shape += [jax.ShapeDtypeStruct(landed["conv"].shape[1:], F32), jax.ShapeDtypeStruct(landed["loss"].shape[1:], F32)]
    outs = pl.pallas_call(body, name=name, out_shape=out_shape)(*args)
    result = {p: tuple(outs[4 * i:4 * i + 4]) for i, p in enumerate(param_names)}
    result["conv"], result["loss"] = outs[4 * n_p], outs[4 * n_p + 1]
    return result


def _lanes(vec, offset):
    return jnp.pad(vec[None, :], ((0, 0), (offset, 128 - offset - vec.shape[0])))


def kernel(x, mix_pre_gain, mix_post_gain, mlp_pre_gain, mlp_post_gain, mlp_w_up, mlp_w_down, gdn_w_in, gdn_conv_w, gdn_a_log, gdn_dt_bias, gdn_out_gain, gdn_w_out, kv_gain, w_kv, sb_w_q, sb_w_o, loss_target, m_mix_pre_gain, m_mix_post_gain, m_mlp_pre_gain, m_mlp_post_gain, m_mlp_w_up, m_mlp_w_down, m_gdn_w_in, m_gdn_conv_w, m_gdn_a_log, m_gdn_dt_bias, m_gdn_out_gain, m_gdn_w_out, m_kv_gain, m_w_kv, m_sb_w_q, m_sb_w_o, v_mix_pre_gain, v_mix_post_gain, v_mlp_pre_gain, v_mlp_post_gain, v_mlp_w_up, v_mlp_w_down, v_gdn_w_in, v_gdn_conv_w, v_gdn_a_log, v_gdn_dt_bias, v_gdn_out_gain, v_gdn_w_out, v_kv_gain, v_w_kv, v_sb_w_q, v_sb_w_o):
    me = 4 * lax.axis_index("x") + 2 * lax.axis_index("y") + lax.axis_index("c")
    bf = lambda a: a.astype(BF16)

    shards = {"up0": bf(mlp_w_up[0]), "up1": bf(mlp_w_up[1]), "down0": bf(mlp_w_down[0]), "down1": bf(mlp_w_down[1]),
              "gdn_out": bf(gdn_w_out[0]), "kv": bf(w_kv), "sb_q": bf(sb_w_q[0]), "sb_o": bf(sb_w_o[0])}
    gdn_in, conv = _gather_two_level([bf(gdn_w_in[0]), gdn_conv_w[0]], "gather_first_weights")
    comm = _Fsdp(shards, {"gdn_in": _whole_weight("gdn_in", gdn_in), "conv": _whole_weight("conv", conv)})
    gains = {"mix_pre": mix_pre_gain, "mix_post": mix_post_gain, "mlp_pre": mlp_pre_gain, "mlp_post": mlp_post_gain,
             "kv": kv_gain[None, :], "alog": _lanes(gdn_a_log[0], N_HEADS), "dtb": _lanes(gdn_dt_bias[0], N_HEADS),
             "out_gain": gdn_out_gain}

    grad_x, big, small = _local_step(x[0], loss_target[0], gains, comm)

    (pre0,) = _exchange([small["pre0"]], [True], "exchange_last_grad")
    small_landed = dict(comm.small_landed, pre0=pre0)

    results = {}
    results["mlp_w_up"] = _reduce_adamw_layers((comm.landed["up0"], comm.landed["up1"]), mlp_w_up, m_mlp_w_up, v_mlp_w_up,
                                               "adamw_mlp_w_up")
    results["mlp_w_down"] = _reduce_adamw_layers((comm.landed["down0"], comm.landed["down1"]), mlp_w_down, m_mlp_w_down,
                                                 v_mlp_w_down, "adamw_mlp_w_down")
    big_params = [("gdn_w_in", "gdn_in", gdn_w_in[0], m_gdn_w_in[0], v_gdn_w_in[0]),
                  ("gdn_w_out", "gdn_out", gdn_w_out[0], m_gdn_w_out[0], v_gdn_w_out[0]),
                  ("w_kv", "kv", w_kv, m_w_kv, v_w_kv), ("sb_w_q", "sb_q", sb_w_q[0], m_sb_w_q[0], v_sb_w_q[0]),
                  ("sb_w_o", "sb_o", sb_w_o[0], m_sb_w_o[0], v_sb_w_o[0])]
    for nm, short, w_, m_, v_ in big_params:
        results[nm] = _reduce_adamw(comm.landed[short], w_, m_, v_, "adamw_" + nm)
    lanes8 = lambda a: _lanes(a[0], N_HEADS)
    small_params = {
        "mix_pre": (mix_pre_gain, m_mix_pre_gain, v_mix_pre_gain), "mix_post": (mix_post_gain, m_mix_post_gain, v_mix_post_gain),
        "mlp_pre": (mlp_pre_gain, m_mlp_pre_gain, v_mlp_pre_gain), "mlp_post": (mlp_post_gain, m_mlp_post_gain, v_mlp_post_gain),
        "kv": (kv_gain[None, :], m_kv_gain[None, :], v_kv_gain[None, :]),
        "alog": (lanes8(gdn_a_log), lanes8(m_gdn_a_log), lanes8(v_gdn_a_log)),
        "dtb": (lanes8(gdn_dt_bias), lanes8(m_gdn_dt_bias), lanes8(v_gdn_dt_bias)),
        "out_gain": (gdn_out_gain, m_gdn_out_gain, v_gdn_out_gain),
    }
    sm = _small_update(small_landed, small_params, "small_update")
    conv_cols = QKV // N_DEV
    g_conv = lax.dynamic_slice(sm["conv"], (0, me * conv_cols), (8, conv_cols))[:CONV_WIDTH]
    conv_res = (g_conv,) + tuple(_adamw(g_conv, gdn_conv_w[0], m_gdn_conv_w[0], v_gdn_conv_w[0], "adamw_conv"))

    lead = lambda t: tuple(a[None] for a in t)
    heads8 = lambda t: tuple(a[:, N_HEADS:2 * N_HEADS] for a in t)
    per_weight = [
        sm["mix_pre"], sm["mix_post"], sm["mlp_pre"], sm["mlp_post"],
        tuple(results["mlp_w_up"]), tuple(results["mlp_w_down"]),
        lead(results["gdn_w_in"]), lead(conv_res), heads8(sm["alog"]), heads8(sm["dtb"]), sm["out_gain"],
        lead(results["gdn_w_out"]), tuple(a[0] for a in sm["kv"]), results["w_kv"], lead(results["sb_w_q"]), lead(results["sb_w_o"]),
    ]
    grads, deltas, new_ms, new_vs = zip(*per_weight)
    return (sm["loss"][0, 0], grad_x[None], *grads, *deltas, *new_ms, *new_vs)
```

```python
import functools
import math

import jax
import jax.numpy as jnp
from jax import lax
from jax.experimental import pallas as pl
from jax.experimental.pallas import tpu as pltpu

F32 = jnp.float32
BF16 = jnp.bfloat16

N_DEV = 8
D_MODEL = 1024
N_HEADS = 8
HEAD_DIM = 128
CHUNK = 64
CONV_WIDTH = 4
GDN_IN_COLS = 4 * D_MODEL + 2 * N_HEADS
GDN_IN_PAD = 4 * D_MODEL + 128
EPS = 1e-6

ADAM_LR = 0.001
ADAM_B1 = 0.9
ADAM_B2 = 0.999
ADAM_EPS = 1e-08
ADAM_WD = 0.01
ADAM_STEP = 10

VMEM_LIMIT_BYTES = 56 * 1024 * 1024
MESH = pl.DeviceIdType.MESH


def _params(*semantics):
    return pltpu.CompilerParams(dimension_semantics=semantics, vmem_limit_bytes=VMEM_LIMIT_BYTES)


def _dims(ta, tb):
    return (((0,) if ta else (1,), (1,) if tb else (0,)), ((), ()))


def _dot(a, b, ta=False, tb=False):
    return lax.dot_general(a.astype(BF16), b.astype(BF16), _dims(ta, tb), preferred_element_type=F32)


def _dot_f32(a, b, ta=False, tb=False):
    return lax.dot_general(a, b, _dims(ta, tb), precision=lax.Precision.HIGHEST, preferred_element_type=F32)


def _dot_3x(a, b, ta=False, tb=False):
    return lax.dot_general(a, b, _dims(ta, tb), precision=lax.Precision.HIGH, preferred_element_type=F32)


def _make_mm(dot):
    @functools.partial(jax.custom_vjp, nondiff_argnums=(2, 3))
    def mm(a, b, ta, tb):
        return dot(a, b, ta, tb)

    def fwd(a, b, ta, tb):
        return dot(a, b, ta, tb), (a, b)

    def bwd(ta, tb, res, g):
        a, b = res
        if not ta and not tb:
            return mm(g, b, False, True), mm(a, g, True, False)
        if not ta and tb:
            return mm(g, b, False, False), mm(g, a, True, False)
        if ta and not tb:
            return mm(b, g, False, True), mm(a, g, False, False)
        raise NotImplementedError

    mm.defvjp(fwd, bwd)
    return mm


_mm = _make_mm(_dot)
_mm_f32 = _make_mm(_dot_f32)


def _rms(x, gain):
    r = lax.rsqrt(jnp.mean(x * x, axis=-1, keepdims=True) + EPS)
    return x * r * gain


def _rms_bwd(x, gain, dy):
    r = lax.rsqrt(jnp.mean(x * x, axis=-1, keepdims=True) + EPS)
    xh = x * r
    dgain = jnp.sum(dy * xh, axis=0, keepdims=True)
    dxh = dy * gain
    dx = r * (dxh - xh * jnp.mean(dxh * xh, axis=-1, keepdims=True))
    return dx, dgain


def _silu(x):
    return x / (1.0 + jnp.exp(-x))


def _acc_rows(ref, first, rows):
    @pl.when(first)
    def _():
        ref[...] = jnp.zeros_like(ref)

    for r, val in enumerate(rows):
        ref[r:r + 1, :] += val


def _norm_matmul(x, gain, w, out_dtype, name, tm=1024, tn=1024):
    n, d = x.shape
    g, _, wc = w.shape
    tm, tn = min(tm, n), min(tn, wc)
    per = wc // tn
    assert n % tm == 0 and wc % tn == 0

    def body(x_ref, gain_ref, w_ref, out_ref, h_ref):
        @pl.when(pl.program_id(1) == 0)
        def _():
            h_ref[...] = _rms(x_ref[...], gain_ref[...]).astype(BF16)

        out_ref[...] = jnp.dot(h_ref[...], w_ref[...], preferred_element_type=F32).astype(out_dtype)

    return pl.pallas_call(
        body, name=name, grid=(n // tm, g * per),
        in_specs=[pl.BlockSpec((tm, d), lambda i, j: (i, 0)),
                  pl.BlockSpec((1, d), lambda i, j: (0, 0)),
                  pl.BlockSpec((None, d, tn), lambda i, j: (j // per, 0, j % per))],
        out_specs=[pl.BlockSpec((tm, tn), lambda i, j: (i, j)),
                   pl.BlockSpec((tm, d), lambda i, j: (i, 0))],
        out_shape=[jax.ShapeDtypeStruct((n, g * wc), out_dtype), jax.ShapeDtypeStruct((n, d), BF16)],
        compiler_params=_params("parallel", "arbitrary"),
    )(x, gain, w)


def _norm_matmul_bwd(branches, x, add, name, tm=1024, ride=None):
    n, d = x.shape
    tm = min(tm, n)
    tns = [min(tn, w.shape[2]) for _, w, _, tn in branches]
    pers = [w.shape[2] // tn for (_, w, _, _), tn in zip(branches, tns)]
    ncols = [w.shape[0] * per for (_, w, _, _), per in zip(branches, pers)]
    offs = [sum(ncols[:b]) for b in range(len(branches))]
    total = sum(ncols)
    nb = len(branches)

    def body(*refs):
        dout_refs, w_refs, gain_refs = refs[:nb], refs[nb:2 * nb], refs[2 * nb:3 * nb]
        x_ref, add_ref, dx_ref, dgain_ref, acc_ref, sum_ref = refs[3 * nb:]
        i, j = pl.program_id(0), pl.program_id(1)

        @pl.when((i == 0) & (j == 0))
        def _():
            dgain_ref[...] = jnp.zeros_like(dgain_ref)

        for b in range(nb):
            first, last = offs[b], offs[b] + ncols[b] - 1

            @pl.when((j >= first) & (j <= last))
            def _(b=b, first=first):
                part = _dot(dout_refs[b][...], w_refs[b][...], tb=True)

                @pl.when(j == first)
                def _():
                    acc_ref[...] = part

                @pl.when(j > first)
                def _():
                    acc_ref[...] += part

            @pl.when(j == last)
            def _(b=b):
                xv = x_ref[...]
                xh = xv * lax.rsqrt(jnp.mean(xv * xv, axis=-1, keepdims=True) + EPS)
                dh = acc_ref[...]
                dgain_ref[b:b + 1, :] += jnp.sum(dh * xh, axis=0, keepdims=True)
                scaled = dh * gain_refs[b][...]
                if b == 0:
                    sum_ref[...] = scaled
                else:
                    sum_ref[...] += scaled

        @pl.when(j == total - 1)
        def _():
            xv = x_ref[...]
            r = lax.rsqrt(jnp.mean(xv * xv, axis=-1, keepdims=True) + EPS)
            xh = xv * r
            dxh = sum_ref[...]
            dx_ref[...] = add_ref[...] + r * (dxh - xh * jnp.mean(dxh * xh, axis=-1, keepdims=True))

    def dout_spec(b):
        return pl.BlockSpec((tm, tns[b]), lambda i, j: (i, jnp.clip(j - offs[b], 0, ncols[b] - 1)))

    def w_spec(b):
        def index(i, j):
            c = jnp.clip(j - offs[b], 0, ncols[b] - 1)
            return (c // pers[b], 0, c % pers[b])
        return pl.BlockSpec((None, d, tns[b]), index)

    row = lambda i, j: (i, 0)
    const = lambda i, j: (0, 0)
    nrow = n // tm
    (dx, dgain), got = _call(
        body, [b[0] for b in branches] + [b[1] for b in branches] + [b[2] for b in branches] + [x, add],
        name=name, grid=(nrow, total),
        in_specs=[dout_spec(b) for b in range(nb)] + [w_spec(b) for b in range(nb)]
                 + [pl.BlockSpec((1, d), const)] * nb + [pl.BlockSpec((tm, d), row), pl.BlockSpec((tm, d), row)],
        out_specs=[pl.BlockSpec((tm, d), row), pl.BlockSpec((8, d), const)],
        out_shape=[jax.ShapeDtypeStruct((n, d), F32), jax.ShapeDtypeStruct((8, d), F32)],
        scratch_shapes=[pltpu.VMEM((tm, d), F32), pltpu.VMEM((tm, d), F32)],
        compiler_params=_params("arbitrary", "arbitrary"),
        ride=ride, first=lambda: (pl.program_id(0) == 0) & (pl.program_id(1) == 0),
        last=lambda: (pl.program_id(0) == nrow - 1) & (pl.program_id(1) == total - 1))
    return (dx, dgain) if ride is None else (dx, dgain, got)


def _matmul_tn(a, b, groups, name, ta=1024, tb=1024, tk=1024):
    n, ka = a.shape
    _, kb = b.shape
    wc = kb // groups
    ta, tb, tk = min(ta, ka), min(tb, wc), min(tk, n)
    per = wc // tb
    nk = n // tk
    assert ka % ta == 0 and wc % tb == 0 and n % tk == 0

    def body(a_ref, b_ref, out_ref, acc_ref):
        k = pl.program_id(2)

        @pl.when(k == 0)
        def _():
            acc_ref[...] = jnp.zeros_like(acc_ref)

        acc_ref[...] += _dot(a_ref[...], b_ref[...], ta=True)

        @pl.when(k == nk - 1)
        def _():
            out_ref[...] = acc_ref[...].astype(BF16)

    return pl.pallas_call(
        body, name=name, grid=(ka // ta, groups * per, nk),
        in_specs=[pl.BlockSpec((tk, ta), lambda i, j, k: (k, i)),
                  pl.BlockSpec((tk, tb), lambda i, j, k: (k, j))],
        out_specs=pl.BlockSpec((None, ta, tb), lambda i, j, k: (j // per, i, j % per)),
        out_shape=jax.ShapeDtypeStruct((groups, ka, wc), BF16),
        scratch_shapes=[pltpu.VMEM((ta, tb), F32)],
        compiler_params=_params("parallel", "parallel", "arbitrary"),
    )(a, b)


def _gated_head_norm(o, gate, out_gain):
    parts = []
    for h in range(N_HEADS):
        sl = slice(h * HEAD_DIM, (h + 1) * HEAD_DIM)
        parts.append(_rms(o[:, sl], out_gain) * _silu(gate[:, sl]))
    return parts


def _out_proj(a_inputs, gated, x_in, w, gain, out_gain, name, tm=1024):
    n, d = x_in.shape
    k = w.shape[0]
    tm = min(tm, n)

    def body(*refs):
        if gated:
            o_ref, gate_ref, og_ref, x_ref, w_ref, gain_ref, xo_ref, y_ref, a_ref = refs
            parts = _gated_head_norm(o_ref[...], gate_ref[...], og_ref[...])
            for h, part in enumerate(parts):
                a_ref[:, h * HEAD_DIM:(h + 1) * HEAD_DIM] = part.astype(BF16)
            a = a_ref[...]
        else:
            a_in_ref, x_ref, w_ref, gain_ref, xo_ref, y_ref = refs
            a = a_in_ref[...]
        y = jnp.dot(a, w_ref[...], preferred_element_type=F32)
        y_ref[...] = y
        xo_ref[...] = x_ref[...] + _rms(y, gain_ref[...])

    row = lambda i: (i, 0)
    const = lambda i: (0, 0)
    if gated:
        a_specs = [pl.BlockSpec((tm, k), row), pl.BlockSpec((tm, D_MODEL), lambda i: (i, 3)),
                   pl.BlockSpec((1, HEAD_DIM), const)]
        a_args = list(a_inputs) + [out_gain]
    else:
        a_specs = [pl.BlockSpec((tm, k), row)]
        a_args = list(a_inputs)
    out_specs = [pl.BlockSpec((tm, d), row), pl.BlockSpec((tm, d), row)]
    out_shape = [jax.ShapeDtypeStruct((n, d), F32), jax.ShapeDtypeStruct((n, d), F32)]
    if gated:
        out_specs.append(pl.BlockSpec((tm, k), row))
        out_shape.append(jax.ShapeDtypeStruct((n, k), BF16))
    return pl.pallas_call(
        body, name=name, grid=(n // tm,),
        in_specs=a_specs + [pl.BlockSpec((tm, d), row), pl.BlockSpec((k, d), const), pl.BlockSpec((1, d), const)],
        out_specs=out_specs, out_shape=out_shape,
        compiler_params=_params("parallel"),
    )(*a_args, x_in, w, gain)


def _out_proj_bwd(dxo, y, w, gain, gated_inputs, name, tm=1024):
    n, d = dxo.shape
    k = w.shape[0]
    tm = min(tm, n)
    gated = gated_inputs is not None

    def body(*refs):
        if gated:
            (dxo_ref, y_ref, w_ref, gain_ref, o_ref, gate_ref, og_ref,
             dy_ref, dgain_ref, do_ref, dgate_ref, dog_ref) = refs
        else:
            dxo_ref, y_ref, w_ref, gain_ref, dy_ref, dgain_ref, da_ref = refs
        first = pl.program_id(0) == 0
        dy, dgain = _rms_bwd(y_ref[...], gain_ref[...], dxo_ref[...])
        dy_ref[...] = dy.astype(BF16)
        _acc_rows(dgain_ref, first, [dgain])
        da = _dot(dy_ref[...], w_ref[...], tb=True)
        if not gated:
            da_ref[...] = da.astype(BF16)
            return
        og = og_ref[...]
        dog = jnp.zeros_like(og)
        for h in range(N_HEADS):
            sl = slice(h * HEAD_DIM, (h + 1) * HEAD_DIM)
            fn = lambda o_h, g_h, gn: _rms(o_h, gn) * _silu(g_h)
            _, vjp = jax.vjp(fn, o_ref[:, sl], gate_ref[:, sl], og)
            do_h, dgate_h, dog_h = vjp(da[:, sl])
            do_ref[:, sl] = do_h
            dgate_ref[:, sl] = dgate_h.astype(BF16)
            dog = dog + dog_h
        _acc_rows(dog_ref, first, [dog])

    row = lambda i: (i, 0)
    const = lambda i: (0, 0)
    in_specs = [pl.BlockSpec((tm, d), row), pl.BlockSpec((tm, d), row), pl.BlockSpec((k, d), const),
                pl.BlockSpec((1, d), const)]
    args = [dxo, y, w, gain]
    out_specs = [pl.BlockSpec((tm, d), row), pl.BlockSpec((8, d), const)]
    out_shape = [jax.ShapeDtypeStruct((n, d), BF16), jax.ShapeDtypeStruct((8, d), F32)]
    if gated:
        in_specs += [pl.BlockSpec((tm, k), row), pl.BlockSpec((tm, D_MODEL), lambda i: (i, 3)),
                     pl.BlockSpec((1, HEAD_DIM), const)]
        args += list(gated_inputs)
        out_specs += [pl.BlockSpec((tm, k), row), pl.BlockSpec((tm, k), row), pl.BlockSpec((8, HEAD_DIM), const)]
        out_shape += [jax.ShapeDtypeStruct((n, k), F32), jax.ShapeDtypeStruct((n, k), BF16),
                      jax.ShapeDtypeStruct((8, HEAD_DIM), F32)]
    else:
        out_specs.append(pl.BlockSpec((tm, k), row))
        out_shape.append(jax.ShapeDtypeStruct((n, k), BF16))
    return pl.pallas_call(
        body, name=name, grid=(n // tm,), in_specs=in_specs, out_specs=out_specs, out_shape=out_shape,
        compiler_params=_params("arbitrary"),
    )(*args)


def _mlp_fwd(x_in, g_pre, w_up, w_down, g_post, name, tm=1024, target=None, ride=None):
    n, d = x_in.shape
    g, _, wc = w_up.shape
    tm = min(tm, n)
    with_loss = target is not None

    def body(*refs):
        if with_loss:
            x_ref, gpre_ref, wup_ref, wdown_ref, gpost_ref, t_ref, xo_ref, y_ref, h_ref, u_ref, a_ref, loss_ref, acc_ref = refs
        else:
            x_ref, gpre_ref, wup_ref, wdown_ref, gpost_ref, xo_ref, y_ref, h_ref, u_ref, a_ref, acc_ref = refs
        i, j = pl.program_id(0), pl.program_id(1)

        @pl.when(j == 0)
        def _():
            h_ref[...] = _rms(x_ref[...], gpre_ref[...]).astype(BF16)
            acc_ref[...] = jnp.zeros_like(acc_ref)

        u = jnp.dot(h_ref[...], wup_ref[...], preferred_element_type=F32).astype(BF16)
        u_ref[...] = u
        a = jnp.square(jnp.maximum(u, 0))
        a_ref[...] = a
        acc_ref[...] += jnp.dot(a, wdown_ref[...], preferred_element_type=F32)

        @pl.when(j == g - 1)
        def _():
            y = acc_ref[...]
            y_ref[...] = y
            xo = x_ref[...] + _rms(y, gpost_ref[...])
            if with_loss:
                err = xo - t_ref[...]
                xo_ref[...] = err * (1.0 / d)
                part = 0.5 * jnp.sum(jnp.mean(err * err, axis=-1, keepdims=True), axis=0, keepdims=True)
                _acc_rows(loss_ref, i == 0, [jnp.broadcast_to(part, (1, 128))])
            else:
                xo_ref[...] = xo

    row = lambda i, j: (i, 0)
    const = lambda i, j: (0, 0)
    in_specs = [pl.BlockSpec((tm, d), row), pl.BlockSpec((1, d), const),
                pl.BlockSpec((None, d, wc), lambda i, j: (j, 0, 0)),
                pl.BlockSpec((wc, d), lambda i, j: (j, 0)), pl.BlockSpec((1, d), const)]
    out_specs = [pl.BlockSpec((tm, d), row), pl.BlockSpec((tm, d), row), pl.BlockSpec((tm, d), row),
                 pl.BlockSpec((tm, wc), lambda i, j: (i, j)), pl.BlockSpec((tm, wc), lambda i, j: (i, j))]
    out_shape = [jax.ShapeDtypeStruct((n, d), F32), jax.ShapeDtypeStruct((n, d), F32),
                 jax.ShapeDtypeStruct((n, d), BF16), jax.ShapeDtypeStruct((n, g * wc), BF16),
                 jax.ShapeDtypeStruct((n, g * wc), BF16)]
    args = [x_in, g_pre, w_up, w_down, g_post]
    if with_loss:
        in_specs.append(pl.BlockSpec((tm, d), row))
        out_specs.append(pl.BlockSpec((8, 128), const))
        out_shape.append(jax.ShapeDtypeStruct((8, 128), F32))
        args.append(target)
    nrow = n // tm
    outs, got = _call(
        body, args, name=name, grid=(nrow, g), in_specs=in_specs, out_specs=out_specs, out_shape=out_shape,
        scratch_shapes=[pltpu.VMEM((tm, d), F32)], compiler_params=_params("arbitrary", "arbitrary"),
        ride=ride, first=lambda: (pl.program_id(0) == 0) & (pl.program_id(1) == 0),
        last=lambda: (pl.program_id(0) == nrow - 1) & (pl.program_id(1) == g - 1))
    return outs if ride is None else (outs, got)


def _mlp_bwd(dxo, y, g_post, w_down, u, w_up, x_in, g_pre, name, tm=1024, ride=None):
    n, d = dxo.shape
    g, _, wc = w_up.shape
    tm = min(tm, n)

    def body(dxo_ref, y_ref, gpost_ref, wdown_ref, u_ref, wup_ref, x_ref, gpre_ref,
             dx_ref, dy_ref, du_ref, dgain_ref, acc_ref, dgpost_ref):
        i, j = pl.program_id(0), pl.program_id(1)

        @pl.when(j == 0)
        def _():
            dy, dgpost = _rms_bwd(y_ref[...], gpost_ref[...], dxo_ref[...])
            dy_ref[...] = dy.astype(BF16)
            dgpost_ref[...] = dgpost
            acc_ref[...] = jnp.zeros_like(acc_ref)

        da = _dot(dy_ref[...], wdown_ref[...], tb=True)
        du = (da * (2.0 * jnp.maximum(u_ref[...], 0).astype(F32))).astype(BF16)
        du_ref[...] = du
        acc_ref[...] += _dot(du, wup_ref[...], tb=True)

        @pl.when(j == g - 1)
        def _():
            dx, dgpre = _rms_bwd(x_ref[...], gpre_ref[...], acc_ref[...])
            dx_ref[...] = dxo_ref[...] + dx
            _acc_rows(dgain_ref, i == 0, [dgpre, dgpost_ref[...]])

    row = lambda i, j: (i, 0)
    const = lambda i, j: (0, 0)
    nrow = n // tm
    outs, got = _call(
        body, [dxo, y, g_post, w_down, u, w_up, x_in, g_pre], name=name, grid=(nrow, g),
        in_specs=[pl.BlockSpec((tm, d), row), pl.BlockSpec((tm, d), row), pl.BlockSpec((1, d), const),
                  pl.BlockSpec((wc, d), lambda i, j: (j, 0)), pl.BlockSpec((tm, wc), lambda i, j: (i, j)),
                  pl.BlockSpec((None, d, wc), lambda i, j: (j, 0, 0)), pl.BlockSpec((tm, d), row),
                  pl.BlockSpec((1, d), const)],
        out_specs=[pl.BlockSpec((tm, d), row), pl.BlockSpec((tm, d), row),
                   pl.BlockSpec((tm, wc), lambda i, j: (i, j)), pl.BlockSpec((8, d), const)],
        out_shape=[jax.ShapeDtypeStruct((n, d), F32), jax.ShapeDtypeStruct((n, d), BF16),
                   jax.ShapeDtypeStruct((n, g * wc), BF16), jax.ShapeDtypeStruct((8, d), F32)],
        scratch_shapes=[pltpu.VMEM((tm, d), F32), pltpu.VMEM((1, d), F32)],
        compiler_params=_params("arbitrary", "arbitrary"),
        ride=ride, first=lambda: (pl.program_id(0) == 0) & (pl.program_id(1) == 0),
        last=lambda: (pl.program_id(0) == nrow - 1) & (pl.program_id(1) == g - 1))
    return outs if ride is None else (outs, got)


QKV = 3 * D_MODEL


def _shifted(x, prev8, s):
    if s == 0:
        return x
    tm = x.shape[0]
    rolled = pltpu.roll(x, s, 0)
    head = pltpu.roll(prev8, s, 0)
    head = jnp.concatenate([head, jnp.zeros((tm - 8, x.shape[1]), x.dtype)], axis=0)
    rows = lax.broadcasted_iota(jnp.int32, x.shape, 0)
    return jnp.where(rows < s, head, rolled)


def _conv(x, prev8, conv_w):
    out = x * conv_w[3:4, :]
    for s in range(1, CONV_WIDTH):
        out = out + _shifted(x, prev8, s) * conv_w[3 - s:4 - s, :]
    return out


def _l2norm(x):
    return x * lax.rsqrt(jnp.sum(x * x, axis=-1, keepdims=True) + EPS)


def _gdn_act_head(cq, ck, cv):
    return _l2norm(_silu(cq)) * (HEAD_DIM ** -0.5), _l2norm(_silu(ck)), _silu(cv)


def _gdn_act_head_bwd(cq, ck, cv, dq, dk, dv):
    def silu_and_slope(c):
        s = 1.0 / (1.0 + jnp.exp(-c))
        return c * s, s * (1.0 + c * (1.0 - s))

    def through_norm(c, dy, scale):
        a, slope = silu_and_slope(c)
        n = lax.rsqrt(jnp.sum(a * a, axis=-1, keepdims=True) + EPS)
        t = dy * scale if scale != 1.0 else dy
        da = n * (t - a * (n * n * jnp.sum(t * a, axis=-1, keepdims=True)))
        return da * slope

    return through_norm(cq, dq, HEAD_DIM ** -0.5), through_norm(ck, dk, 1.0), dv * silu_and_slope(cv)[1]


def _gdn_gates(ba, alog, dtb):
    lane = lax.broadcasted_iota(jnp.int32, ba.shape, 1)
    beta = 1.0 / (1.0 + jnp.exp(-ba))
    t = ba + dtb
    softplus = jnp.maximum(t, 0.0) + jnp.log(1.0 + jnp.exp(-jnp.abs(t)))
    g = -jnp.exp(alog) * softplus
    return jnp.where(lane < N_HEADS, beta, jnp.where(lane < 2 * N_HEADS, g, 0.0))


def _qkv_cols(part, h):
    start = part * D_MODEL + h * HEAD_DIM
    return slice(start, start + HEAD_DIM)


def _prev_rows_spec(tm, cols):
    per = tm // 8
    return pl.BlockSpec((8, cols), lambda i: (jnp.maximum(i * per - 1, 0), 0))


def _gdn_pre(proj, conv_w, alog, dtb, name, tm=256):
    n = proj.shape[0]

    def body(x_ref, prev_ref, ba_ref, cw_ref, alog_ref, dtb_ref, q_ref, k_ref, v_ref, bg_ref, c_ref):
        first = pl.program_id(0) == 0
        for h in range(N_HEADS):
            convs = []
            for part in range(3):
                sl = _qkv_cols(part, h)
                prev8 = jnp.where(first, 0.0, prev_ref[:, sl])
                convs.append(_conv(x_ref[:, sl], prev8, cw_ref[:, sl]))
                c_ref[:, sl] = convs[-1]
            out = slice(h * HEAD_DIM, (h + 1) * HEAD_DIM)
            q_ref[:, out], k_ref[:, out], v_ref[:, out] = _gdn_act_head(*convs)
        bg_ref[...] = _gdn_gates(ba_ref[...], alog_ref[...], dtb_ref[...])

    row = lambda i: (i, 0)
    const = lambda i: (0, 0)
    return pl.pallas_call(
        body, name=name, grid=(n // tm,),
        in_specs=[pl.BlockSpec((tm, QKV), row), _prev_rows_spec(tm, QKV),
                  pl.BlockSpec((tm, 128), lambda i: (i, 4 * D_MODEL // 128)),
                  pl.BlockSpec((CONV_WIDTH, QKV), const), pl.BlockSpec((1, 128), const), pl.BlockSpec((1, 128), const)],
        out_specs=[pl.BlockSpec((tm, D_MODEL), row)] * 3 + [pl.BlockSpec((tm, 128), row), pl.BlockSpec((tm, QKV), row)],
        out_shape=[jax.ShapeDtypeStruct((n, D_MODEL), F32)] * 3 + [jax.ShapeDtypeStruct((n, 128), F32),
                                                                  jax.ShapeDtypeStruct((n, QKV), F32)],
        compiler_params=_params("parallel"),
    )(proj, proj, proj, conv_w, alog, dtb)


def _gdn_pre_bwd(proj, conv, conv_w, alog, dtb, dq, dk, dv, dbg, dgate, name, tm=256):
    n = proj.shape[0]
    nt = n // tm

    def body(x_ref, c_ref, ba_ref, cw_ref, alog_ref, dtb_ref, dq_ref, dk_ref, dv_ref, dbg_ref, dgate_ref,
             dproj_ref, dcw_ref, dgates_ref, carry_ref, head_ref):
        step = pl.program_id(0)

        @pl.when(step == 0)
        def _():
            carry_ref[...] = jnp.zeros_like(carry_ref)
            head_ref[...] = jnp.zeros_like(head_ref)
            dcw_ref[...] = jnp.zeros_like(dcw_ref)

        rows = lax.broadcasted_iota(jnp.int32, (tm, HEAD_DIM), 0)
        rows8 = lax.broadcasted_iota(jnp.int32, (8, HEAD_DIM), 0)
        for h in range(N_HEADS):
            cols = [_qkv_cols(part, h) for part in range(3)]
            out = slice(h * HEAD_DIM, (h + 1) * HEAD_DIM)
            dcs = _gdn_act_head_bwd(*[c_ref[:, sl] for sl in cols], dq_ref[:, out], dk_ref[:, out], dv_ref[:, out])
            for sl, dc in zip(cols, dcs):
                x = x_ref[:, sl]
                cw = cw_ref[:, sl]
                dx = dc * cw[3:4, :]
                wrapped = jnp.zeros((8, HEAD_DIM), F32)
                dcw_ref[3:4, sl] += jnp.sum(dc * x, axis=0, keepdims=True)
                for s in range(1, CONV_WIDTH):
                    up = pltpu.roll(dc, tm - s, 0)
                    r = up * cw[3 - s:4 - s, :]
                    dx = dx + jnp.where(rows < tm - s, r, 0.0)
                    wrapped = wrapped + jnp.where(rows[tm - 8:, :] >= tm - s, r[tm - 8:, :], 0.0)
                    inside = jnp.sum(jnp.where(rows < tm - s, x * up, 0.0), axis=0, keepdims=True)
                    next_dc = pltpu.roll(head_ref[:, sl], 8 - s, 0)
                    across = jnp.sum(jnp.where(rows8 >= 8 - s, x[tm - 8:, :] * next_dc, 0.0), axis=0, keepdims=True)
                    dcw_ref[3 - s:4 - s, sl] += inside + across
                from_next = jnp.concatenate([jnp.zeros((tm - 8, HEAD_DIM), F32), carry_ref[:, sl]], axis=0)
                dproj_ref[:, sl] = (dx + from_next).astype(BF16)
                carry_ref[:, sl] = wrapped
                head_ref[:, sl] = dc[:8, :]

        _, vjp = jax.vjp(_gdn_gates, ba_ref[...], alog_ref[...], dtb_ref[...])
        dba, dalog, ddtb = vjp(dbg_ref[...])
        dproj_ref[:, QKV:4 * D_MODEL] = dgate_ref[...]
        dproj_ref[:, 4 * D_MODEL:] = dba.astype(BF16)
        _acc_rows(dgates_ref, step == 0, [dalog, ddtb])

    rev = lambda i: nt - 1 - i
    row = lambda i: (rev(i), 0)
    const = lambda i: (0, 0)
    return pl.pallas_call(
        body, name=name, grid=(nt,),
        in_specs=[pl.BlockSpec((tm, QKV), row), pl.BlockSpec((tm, QKV), row),
                  pl.BlockSpec((tm, 128), lambda i: (rev(i), 4 * D_MODEL // 128)),
                  pl.BlockSpec((CONV_WIDTH, QKV), const), pl.BlockSpec((1, 128), const), pl.BlockSpec((1, 128), const),
                  pl.BlockSpec((tm, D_MODEL), row), pl.BlockSpec((tm, D_MODEL), row), pl.BlockSpec((tm, D_MODEL), row),
                  pl.BlockSpec((tm, 128), row), pl.BlockSpec((tm, D_MODEL), row)],
        out_specs=[pl.BlockSpec((tm, GDN_IN_PAD), row), pl.BlockSpec((8, QKV), const), pl.BlockSpec((8, 128), const)],
        out_shape=[jax.ShapeDtypeStruct((n, GDN_IN_PAD), BF16), jax.ShapeDtypeStruct((8, QKV), F32),
                   jax.ShapeDtypeStruct((8, 128), F32)],
        scratch_shapes=[pltpu.VMEM((8, QKV), F32), pltpu.VMEM((8, QKV), F32)],
        compiler_params=_params("arbitrary"),
    )(proj, conv, proj, conv_w, alog, dtb, dq, dk, dv, dbg, dgate)


@jax.custom_vjp
def _unit_lower_inverses(lowers):
    n = lowers[0].shape[0]
    eye = (lax.broadcasted_iota(jnp.int32, (n, n), 0) == lax.broadcasted_iota(jnp.int32, (n, n), 1)).astype(F32)
    xs = [-low for low in lowers]
    ps = [eye + x for x in xs]
    ys = [_dot_3x(x, x) for x in xs]
    for _ in range(int(math.log2(CHUNK)) - 2):
        both = [_dot_3x(y, jnp.concatenate([y, p], axis=1)) for y, p in zip(ys, ps)]
        ys = [b[:, :n] for b in both]
        ps = [p + b[:, n:] for p, b in zip(ps, both)]
    return tuple(p + _dot_3x(y, p) for y, p in zip(ys, ps))


def _unit_lower_inverses_fwd(lowers):
    ts = _unit_lower_inverses(lowers)
    return ts, ts


def _unit_lower_inverses_bwd(ts, dts):
    left = [_dot_3x(t, dt, ta=True) for t, dt in zip(ts, dts)]
    return (tuple(-_dot_3x(l, t, tb=True) for l, t in zip(left, ts)),)


_unit_lower_inverses.defvjp(_unit_lower_inverses_fwd, _unit_lower_inverses_bwd)


@jax.custom_vjp
def _known_inverses(lowers, inverses):
    return inverses


def _known_inverses_bwd(ts, dts):
    (d_lowers,) = _unit_lower_inverses_bwd(ts, dts)
    return d_lowers, tuple(jnp.zeros_like(t) for t in ts)


_known_inverses.defvjp(lambda lowers, inverses: (inverses, inverses), _known_inverses_bwd)


@functools.partial(jax.custom_vjp, nondiff_argnums=(1,))
def _halves(x, axis):
    half = x.shape[axis] // 2
    return (x[:half], x[half:]) if axis == 0 else (x[:, :half], x[:, half:])


_halves.defvjp(lambda x, axis: (_halves(x, axis), None), lambda axis, _, g: (jnp.concatenate(g, axis=axis),))

GDN_STEP_CHUNKS = 2


def _gdn_chunks(qs, ks, vs, bgs, states, inverses=None, keep_inverses=False):
    c = CHUNK
    heads = range(N_HEADS)
    items = [(j, h) for j in range(len(bgs)) for h in heads]
    row = lax.broadcasted_iota(jnp.int32, (c, c), 0)
    col = lax.broadcasted_iota(jnp.int32, (c, c), 1)
    incl, strict, eye = row >= col, row > col, row == col
    lane = lax.broadcasted_iota(jnp.int32, (c, 128), 1)
    rowc = lax.broadcasted_iota(jnp.int32, (c, 1), 0)
    gc_all = [_mm_f32(incl.astype(F32), bg, False, False) for bg in bgs]
    q, k, v = ([xs[j][h] for j, h in items] for xs in (qs, ks, vs))
    n = range(len(items))
    beta = [jnp.sum(jnp.where(lane == h, bgs[j], 0.0), axis=1, keepdims=True) for j, h in items]
    gc = [jnp.sum(jnp.where(lane == N_HEADS + h, gc_all[j], 0.0), axis=1, keepdims=True) for j, h in items]
    gc_row = [jnp.sum(jnp.where(eye, gc[i], 0.0), axis=0, keepdims=True) for i in n]
    gc_last = [jnp.sum(jnp.where(rowc == c - 1, gc[i], 0.0), axis=0, keepdims=True) for i in n]
    decay = [jnp.where(incl, jnp.exp(jnp.where(incl, gc[i] - gc_row[i], 0.0)), 0.0) for i in n]
    kb = [k[i] * beta[i] for i in n]
    kb_q = [_halves(_mm(jnp.concatenate([kb[i], q[i]], axis=0), k[i], False, True), 0) for i in n]
    lower = tuple(jnp.where(strict, kb_q[i][0] * decay[i], 0.0) for i in n)
    attn = [kb_q[i][1] * decay[i] for i in n]
    t_mat = _unit_lower_inverses(lower) if inverses is None else _known_inverses(lower, inverses)
    egc = [jnp.exp(gc[i]) for i in n]
    w_u = [_halves(_mm(t_mat[i], jnp.concatenate([kb[i] * egc[i], v[i] * beta[i]], axis=1), False, False), 1)
           for i in n]
    w, u = [x[0] for x in w_u], [x[1] for x in w_u]
    qg = [q[i] * egc[i] for i in n]
    kg = [k[i] * jnp.exp(gc_last[i] - gc[i]) for i in n]
    outs, cur = [], list(states)
    for j in range(len(bgs)):
        at = lambda h: j * N_HEADS + h
        w_qg = [_halves(_mm(jnp.concatenate([w[at(h)], qg[at(h)]], axis=0), cur[h], False, False), 0)
                for h in heads]
        v_new = [u[at(h)] - w_qg[h][0] for h in heads]
        outs.append(tuple(w_qg[h][1] + _mm(attn[at(h)], v_new[h], False, False) for h in heads))
        cur = [cur[h] * jnp.exp(gc_last[at(h)]) + _mm(kg[at(h)], v_new[h], True, False) for h in heads]
    return (tuple(outs), tuple(cur), t_mat) if keep_inverses else (tuple(outs), tuple(cur))


def _chunk_head_slices(ref, nch):
    return tuple(tuple(ref[j * CHUNK:(j + 1) * CHUNK, h * HEAD_DIM:(h + 1) * HEAD_DIM] for h in range(N_HEADS))
                 for j in range(nch))


def _store_chunk_heads(ref, values):
    for j, chunk in enumerate(values):
        for h, val in enumerate(chunk):
            ref[j * CHUNK:(j + 1) * CHUNK, h * HEAD_DIM:(h + 1) * HEAD_DIM] = val


def _gdn_scan(qn, kn, v, bg, name, ride=None):
    n = qn.shape[0]
    nch = GDN_STEP_CHUNKS
    rows = nch * CHUNK
    nc = n // rows

    def body(q_ref, k_ref, v_ref, bg_ref, o_ref, saved_ref, inv_ref, state_ref):
        @pl.when(pl.program_id(0) == 0)
        def _():
            state_ref[...] = jnp.zeros_like(state_ref)

        states = tuple(state_ref[h] for h in range(N_HEADS))
        for h in range(N_HEADS):
            saved_ref[h] = states[h]
        bgs = tuple(bg_ref[j * CHUNK:(j + 1) * CHUNK, :] for j in range(nch))
        outs, new_states, inverses = _gdn_chunks(_chunk_head_slices(q_ref, nch), _chunk_head_slices(k_ref, nch),
                                                 _chunk_head_slices(v_ref, nch), bgs, states, keep_inverses=True)
        _store_chunk_heads(o_ref, outs)
        for h in range(N_HEADS):
            state_ref[h] = new_states[h]
        for item, inverse in enumerate(inverses):
            inv_ref[item] = inverse

    row = lambda i: (i, 0)
    return _call(
        body, [qn, kn, v, bg], name=name, grid=(nc,),
        in_specs=[pl.BlockSpec((rows, D_MODEL), row)] * 3 + [pl.BlockSpec((rows, 128), row)],
        out_specs=[pl.BlockSpec((rows, D_MODEL), row),
                   pl.BlockSpec((None, N_HEADS, HEAD_DIM, HEAD_DIM), lambda i: (i, 0, 0, 0)),
                   pl.BlockSpec((None, nch * N_HEADS, CHUNK, CHUNK), lambda i: (i, 0, 0, 0))],
        out_shape=[jax.ShapeDtypeStruct((n, D_MODEL), F32),
                   jax.ShapeDtypeStruct((nc, N_HEADS, HEAD_DIM, HEAD_DIM), F32),
                   jax.ShapeDtypeStruct((nc, nch * N_HEADS, CHUNK, CHUNK), F32)],
        scratch_shapes=[pltpu.VMEM((N_HEADS, HEAD_DIM, HEAD_DIM), F32)],
        compiler_params=_params("arbitrary"),
        ride=ride, first=lambda: pl.program_id(0) == 0, last=lambda: pl.program_id(0) == nc - 1)


def _gdn_scan_bwd(qn, kn, v, bg, saved, inverses, do, name, ride=None):
    n = qn.shape[0]
    nch = GDN_STEP_CHUNKS
    rows = nch * CHUNK
    nc = n // rows

    def body(q_ref, k_ref, v_ref, bg_ref, saved_ref, inv_ref, do_ref, dq_ref, dk_ref, dv_ref, dbg_ref, dstate_ref):
        @pl.when(pl.program_id(0) == 0)
        def _():
            dstate_ref[...] = jnp.zeros_like(dstate_ref)

        states = tuple(saved_ref[h] for h in range(N_HEADS))
        bgs = tuple(bg_ref[j * CHUNK:(j + 1) * CHUNK, :] for j in range(nch))
        known = tuple(inv_ref[item] for item in range(nch * N_HEADS))
        _, vjp = jax.vjp(functools.partial(_gdn_chunks, inverses=known), _chunk_head_slices(q_ref, nch),
                         _chunk_head_slices(k_ref, nch), _chunk_head_slices(v_ref, nch), bgs, states)
        dstates = tuple(dstate_ref[h] for h in range(N_HEADS))
        dqs, dks, dvs, dbgs, dprev = vjp((_chunk_head_slices(do_ref, nch), dstates))
        _store_chunk_heads(dq_ref, dqs)
        _store_chunk_heads(dk_ref, dks)
        _store_chunk_heads(dv_ref, dvs)
        for h in range(N_HEADS):
            dstate_ref[h] = dprev[h]
        for j in range(nch):
            dbg_ref[j * CHUNK:(j + 1) * CHUNK, :] = dbgs[j]

    row = lambda i: (nc - 1 - i, 0)
    return _call(
        body, [qn, kn, v, bg, saved, inverses, do], name=name, grid=(nc,),
        in_specs=[pl.BlockSpec((rows, D_MODEL), row)] * 3 + [pl.BlockSpec((rows, 128), row),
                  pl.BlockSpec((None, N_HEADS, HEAD_DIM, HEAD_DIM), lambda i: (nc - 1 - i, 0, 0, 0)),
                  pl.BlockSpec((None, nch * N_HEADS, CHUNK, CHUNK), lambda i: (nc - 1 - i, 0, 0, 0)),
                  pl.BlockSpec((rows, D_MODEL), row)],
        out_specs=[pl.BlockSpec((rows, D_MODEL), row)] * 3 + [pl.BlockSpec((rows, 128), row)],
        out_shape=[jax.ShapeDtypeStruct((n, D_MODEL), F32)] * 3 + [jax.ShapeDtypeStruct((n, 128), F32)],
        scratch_shapes=[pltpu.VMEM((N_HEADS, HEAD_DIM, HEAD_DIM), F32)],
        compiler_params=_params("arbitrary"),
        ride=ride, first=lambda: pl.program_id(0) == 0, last=lambda: pl.program_id(0) == nc - 1)


SB_BQ = 512
SB_BK = 256
SB_SUB = 128
SB_ROWS = 128
SB_SCALE = HEAD_DIM ** -0.5
SB_DEAD = -105.0


def _sb_terms(z, before):
    e = jnp.exp(-jnp.abs(z))
    log_beta = jnp.minimum(z, 0.0) - jnp.log(1.0 + e)
    log_1m = log_beta - z
    if before is not None:
        log_1m = jnp.where(before, log_1m, 0.0)
    return e, log_beta, log_1m


def _tri_ones(n, cmp):
    r = lax.broadcasted_iota(jnp.int32, (n, 2 * n), 0)
    c = lax.broadcasted_iota(jnp.int32, (n, 2 * n), 1)
    return jnp.where((c >= n) | cmp(r, c), 1.0, 0.0).astype(BF16)


def _sums(x, tri_ones):
    both = jnp.dot(x.astype(BF16), tri_ones, preferred_element_type=F32)
    n = x.shape[1]
    return both[:, :n], both[:, n:]


def _sb_chunk_mask(diagonal, r, s):
    if not diagonal or s * SB_SUB + SB_SUB - 1 < r * SB_ROWS:
        return None
    if s * SB_SUB >= r * SB_ROWS + SB_ROWS - 1:
        return "empty"
    rows = r * SB_ROWS + lax.broadcasted_iota(jnp.int32, (SB_ROWS, SB_SUB), 0)
    cols = s * SB_SUB + lax.broadcasted_iota(jnp.int32, (SB_ROWS, SB_SUB), 1)
    return cols < rows


def _sb_attention(q, kv, name, bq=SB_BQ, ride=None):
    n = q.shape[0]
    bq = min(bq, n)
    bk = min(SB_BK, bq)
    nrc = bq // SB_ROWS

    def body(q_ref, k_ref, v_ref, o_ref, l_ref, n_ref, z_scr, a_scr):
        i = pl.program_id(1)
        qb = q_ref[...]
        after = _tri_ones(SB_SUB, lambda r, c: r > c)

        def span(start, kw, c_sum, acc, diagonal):
            start = pl.multiple_of(start, kw)
            k_w = k_ref[pl.ds(start, kw), :]
            v_w = v_ref[pl.ds(start, kw), :]
            z_scr[:, :kw] = lax.dot_general(qb, k_w, _dims(False, True), preferred_element_type=F32)
            c_rows = [c_sum[r * SB_ROWS:(r + 1) * SB_ROWS] for r in range(nrc)]
            for s in reversed(range(kw // SB_SUB)):
                cols = slice(s * SB_SUB, (s + 1) * SB_SUB)
                for r in range(nrc):
                    rows = slice(r * SB_ROWS, (r + 1) * SB_ROWS)
                    before = _sb_chunk_mask(diagonal, r, s)
                    if isinstance(before, str):
                        a_scr[rows, cols] = jnp.zeros((SB_ROWS, SB_SUB), BF16)
                        continue
                    _, log_beta, log_1m = _sb_terms(z_scr[rows, cols] * SB_SCALE, before)
                    tail, total = _sums(log_1m, after)
                    a = jnp.exp(log_beta + c_rows[r] + tail)
                    if before is not None:
                        a = jnp.where(before, a, 0.0)
                    a_scr[rows, cols] = a.astype(BF16)
                    c_rows[r] = c_rows[r] + total
            acc = acc + jnp.dot(a_scr[:, :kw], v_w, preferred_element_type=F32)
            return jnp.concatenate(c_rows, axis=0), acc

        c_sum, acc = span(i * bq, bq, jnp.zeros((bq, SB_SUB), F32), jnp.zeros((bq, HEAD_DIM), F32), True)

        def more(state):
            done, c, _ = state
            return (done < i * (bq // bk)) & (jnp.max(c) > SB_DEAD)

        def step(state):
            done, c, a = state
            c, a = span(i * bq - (done + 1) * bk, bk, c, a, False)
            return done + 1, c, a

        done, c_sum, acc = lax.while_loop(more, step, (jnp.int32(0), c_sum, acc))
        o_ref[...] = acc.astype(BF16)
        l_ref[...] = c_sum
        n_ref[...] = jnp.full((8, 128), done.astype(F32), F32)

    nq = n // bq
    return _call(
        body, [q, kv, kv], name=name, grid=(N_HEADS, nq),
        in_specs=[pl.BlockSpec((bq, HEAD_DIM), lambda h, i: (i, h)),
                  pl.BlockSpec((n, HEAD_DIM), lambda h, i: (0, h)),
                  pl.BlockSpec((n, HEAD_DIM), lambda h, i: (0, N_HEADS + h))],
        out_specs=[pl.BlockSpec((bq, HEAD_DIM), lambda h, i: (i, h)),
                   pl.BlockSpec((None, bq, 128), lambda h, i: (h, i, 0)),
                   pl.BlockSpec((None, None, 8, 128), lambda h, i: (h, i, 0, 0))],
        out_shape=[jax.ShapeDtypeStruct((n, D_MODEL), BF16), jax.ShapeDtypeStruct((N_HEADS, n, 128), F32),
                   jax.ShapeDtypeStruct((N_HEADS, nq, 8, 128), F32)],
        scratch_shapes=[pltpu.VMEM((bq, bq), F32), pltpu.VMEM((bq, bq), BF16)],
        compiler_params=_params("arbitrary", "arbitrary"),
        ride=ride, first=lambda: (pl.program_id(0) == 0) & (pl.program_id(1) == 0),
        last=lambda: (pl.program_id(0) == N_HEADS - 1) & (pl.program_id(1) == nq - 1))


def _sb_attention_bwd(q, kv, do, lsum, spans, name, bq=SB_BQ, ride=None):
    n = q.shape[0]
    bq = min(bq, n)
    bk = min(SB_BK, bq)
    nrc = bq // SB_ROWS

    def body(q_ref, k_ref, v_ref, do_ref, l_ref, n_ref, dq_ref, dk_out, dv_out, z_scr, da_scr, a_scr, dz_scr, dk_ref, dv_ref):
        i = pl.program_id(1)

        @pl.when(i == 0)
        def _():
            dk_ref[...] = jnp.zeros_like(dk_ref)
            dv_ref[...] = jnp.zeros_like(dv_ref)

        qb = q_ref[...]
        dob = do_ref[...]
        lt_rows = [l_ref[r * SB_ROWS:(r + 1) * SB_ROWS, :] for r in range(nrc)]
        upto = _tri_ones(SB_SUB, lambda r, c: r <= c)
        below = _tri_ones(SB_SUB, lambda r, c: r < c)

        def span(start, kw, l_sum, g_sum, dq, diagonal):
            start = pl.multiple_of(start, kw)
            k_w = k_ref[pl.ds(start, kw), :]
            v_w = v_ref[pl.ds(start, kw), :]
            z_scr[:, :kw] = lax.dot_general(qb, k_w, _dims(False, True), preferred_element_type=F32)
            da_scr[:, :kw] = lax.dot_general(dob, v_w, _dims(False, True), preferred_element_type=F32)
            l_rows = [l_sum[r * SB_ROWS:(r + 1) * SB_ROWS] for r in range(nrc)]
            g_rows = [g_sum[r * SB_ROWS:(r + 1) * SB_ROWS] for r in range(nrc)]
            for s in range(kw // SB_SUB):
                cols = slice(s * SB_SUB, (s + 1) * SB_SUB)
                for r in range(nrc):
                    rows = slice(r * SB_ROWS, (r + 1) * SB_ROWS)
                    before = _sb_chunk_mask(diagonal, r, s)
                    if isinstance(before, str):
                        a_scr[rows, cols] = jnp.zeros((SB_ROWS, SB_SUB), BF16)
                        dz_scr[rows, cols] = jnp.zeros((SB_ROWS, SB_SUB), BF16)
                        continue
                    zs = z_scr[rows, cols] * SB_SCALE
                    e, log_beta, log_1m = _sb_terms(zs, before)
                    l_prefix, l_total = _sums(log_1m, upto)
                    a = jnp.exp(log_beta + (lt_rows[r] - (l_rows[r] + l_prefix)))
                    if before is not None:
                        a = jnp.where(before, a, 0.0)
                    g = a * da_scr[rows, cols]
                    g_prefix, g_total = _sums(g, below)
                    inv = 1.0 / (1.0 + e)
                    beta = jnp.where(zs >= 0, inv, e * inv)
                    dz = (g - (g + g_rows[r] + g_prefix) * beta) * SB_SCALE
                    if before is not None:
                        dz = jnp.where(before, dz, 0.0)
                    a_scr[rows, cols] = a.astype(BF16)
                    dz_scr[rows, cols] = dz.astype(BF16)
                    l_rows[r] = l_rows[r] + l_total
                    g_rows[r] = g_rows[r] + g_total
            dz_w = dz_scr[:, :kw]
            dq = dq + jnp.dot(dz_w, k_w, preferred_element_type=F32)
            dk_ref[pl.ds(start, kw), :] += lax.dot_general(dz_w, qb, _dims(True, False), preferred_element_type=F32)
            dv_ref[pl.ds(start, kw), :] += lax.dot_general(a_scr[:, :kw], dob, _dims(True, False),
                                                           preferred_element_type=F32)
            return jnp.concatenate(l_rows, axis=0), jnp.concatenate(g_rows, axis=0), dq

        taken = jnp.clip(jnp.max(n_ref[...]).astype(jnp.int32), 0, i * (bq // bk))
        zero = jnp.zeros((bq, SB_SUB), F32)
        carry = lax.fori_loop(0, taken, lambda j, c: span(i * bq - (taken - j) * bk, bk, c[0], c[1], c[2], False),
                              (zero, zero, jnp.zeros((bq, HEAD_DIM), F32)))
        _, _, dq = span(i * bq, bq, carry[0], carry[1], carry[2], True)
        dq_ref[...] = dq.astype(BF16)

        @pl.when(i == nq - 1)
        def _():
            dk_out[...] = dk_ref[...].astype(BF16)
            dv_out[...] = dv_ref[...].astype(BF16)

    nq = n // bq
    return _call(
        body, [q, kv, kv, do, lsum, spans], name=name, grid=(N_HEADS, nq),
        in_specs=[pl.BlockSpec((bq, HEAD_DIM), lambda h, i: (i, h)),
                  pl.BlockSpec((n, HEAD_DIM), lambda h, i: (0, h)),
                  pl.BlockSpec((n, HEAD_DIM), lambda h, i: (0, N_HEADS + h)),
                  pl.BlockSpec((bq, HEAD_DIM), lambda h, i: (i, h)),
                  pl.BlockSpec((None, bq, 128), lambda h, i: (h, i, 0)),
                  pl.BlockSpec((None, None, 8, 128), lambda h, i: (h, i, 0, 0))],
        out_specs=[pl.BlockSpec((bq, HEAD_DIM), lambda h, i: (i, h)),
                   pl.BlockSpec((n, HEAD_DIM), lambda h, i: (0, h)),
                   pl.BlockSpec((n, HEAD_DIM), lambda h, i: (0, h))],
        out_shape=[jax.ShapeDtypeStruct((n, D_MODEL), BF16)] * 3,
        scratch_shapes=[pltpu.VMEM((bq, bq), F32), pltpu.VMEM((bq, bq), F32), pltpu.VMEM((bq, bq), BF16),
                        pltpu.VMEM((bq, bq), BF16), pltpu.VMEM((n, HEAD_DIM), F32), pltpu.VMEM((n, HEAD_DIM), F32)],
        compiler_params=_params("arbitrary", "arbitrary"),
        ride=ride, first=lambda: (pl.program_id(0) == 0) & (pl.program_id(1) == 0),
        last=lambda: (pl.program_id(0) == N_HEADS - 1) & (pl.program_id(1) == nq - 1))


def _local_step(x, target, gains, comm):
    g = gains
    w = comm.w
    big = {}
    row = lambda a, i: a[i:i + 1, :]

    proj, h0 = _norm_matmul(x, row(g["mix_pre"], 0), w["gdn_in"], F32, "gdn_in_proj", tn=1408)
    qn, kn, v, bg, conv = _gdn_pre(proj, w["conv"], g["alog"], g["dtb"], "gdn_pre")
    (o_gdn, saved, inverses), got = _gdn_scan(qn, kn, v, bg, "gdn_scan", ride=comm.ride("scan"))
    comm.done("scan", got)
    x1, y_mix0, a_gdn = _out_proj((o_gdn, proj), True, x, w["gdn_out"], row(g["mix_post"], 0), g["out_gain"],
                                  "gdn_out_proj", tm=512)
    ride = comm.ride("mlp0")
    res = _mlp_fwd(x1, row(g["mlp_pre"], 0), w["up0"], w["down0"], row(g["mlp_post"], 0), "mlp0", ride=ride)
    (x2, y_mlp0, h_mlp0, u0, a0), got = res if ride is not None else (res, [])
    comm.done("mlp0", got)
    kv, h_kv = _norm_matmul(x2, g["kv"], w["kv"], BF16, "kv_proj", tm=2048)
    q, h_q = _norm_matmul(x2, row(g["mix_pre"], 1), w["sb_q"], BF16, "sb_q_proj")
    (o_sb, lsum, spans), got = _sb_attention(q, kv, "sb_attention", ride=comm.ride("sb"))
    comm.done("sb", got)
    x3, y_mix1 = _out_proj((o_sb,), False, x2, w["sb_o"], row(g["mix_post"], 1), None, "sb_out_proj")
    dx4, y_mlp1, h_mlp1, u1, a1, loss = _mlp_fwd(x3, row(g["mlp_pre"], 1), w["up1"], w["down1"], row(g["mlp_post"], 1),
                                                 "mlp1_loss", target=target)

    dx3, dy_mlp1, du1, dg_mlp1 = _mlp_bwd(dx4, y_mlp1, row(g["mlp_post"], 1), w["down1"], u1, w["up1"], x3,
                                          row(g["mlp_pre"], 1), "mlp1_bwd")
    big["down1"] = _matmul_tn(a1, dy_mlp1, 1, "d_down1")[0]
    big["up1"] = _matmul_tn(h_mlp1, du1, N_DEV, "d_up1", tk=2048)
    dy_mix1, dg_post1, do_sb = _out_proj_bwd(dx3, y_mix1, w["sb_o"], row(g["mix_post"], 1), None, "sb_out_proj_bwd")
    big["sb_o"] = _matmul_tn(o_sb, dy_mix1, 1, "d_sb_o")[0]
    (dq, dk, dv), got = _sb_attention_bwd(q, kv, do_sb, lsum, spans, "sb_attention_bwd", ride=comm.ride("sb_bwd", big))
    comm.done("sb_bwd", got)
    big["sb_q"] = _matmul_tn(h_q, dq, 1, "d_sb_q")[0]
    big["kv"] = jnp.concatenate([_matmul_tn(h_kv, dk, N_DEV // 2, "d_w_k", tk=2048),
                                 _matmul_tn(h_kv, dv, N_DEV // 2, "d_w_v", tk=2048)],
                                axis=0)
    dx2, dg_x2 = _norm_matmul_bwd([(dq, w["sb_q"], row(g["mix_pre"], 1), 1024), (dk, w["kv"][0:1], g["kv"], 1024),
                                   (dv, w["kv"][1:2], g["kv"], 1024)], x2, dx3, "qkv_proj_bwd", tm=512)

    big.update(x2=dg_x2, post1=dg_post1, mlp1=dg_mlp1, loss=loss)
    ride = comm.ride("mlp0_bwd", big)
    res = _mlp_bwd(dx2, y_mlp0, row(g["mlp_post"], 0), w["down0"], u0, w["up0"], x1, row(g["mlp_pre"], 0), "mlp0_bwd",
                   ride=ride)
    (dx1, dy_mlp0, du0, dg_mlp0), got = res if ride is not None else (res, [])
    comm.done("mlp0_bwd", got)
    big["down0"] = _matmul_tn(a0, dy_mlp0, 1, "d_down0")[0]
    big["up0"] = _matmul_tn(h_mlp0, du0, N_DEV, "d_up0", tk=2048)
    dy_mix0, dg_post0, do_gdn, dgate, d_out_gain = _out_proj_bwd(
        dx1, y_mix0, w["gdn_out"], row(g["mix_post"], 0), (o_gdn, proj, g["out_gain"]), "gdn_out_proj_bwd", tm=512)
    big["gdn_out"] = _matmul_tn(a_gdn, dy_mix0, 1, "d_gdn_out")[0]
    big.update(post0=dg_post0, mlp0=dg_mlp0, out_gain=d_out_gain)
    (dqn, dkn, dvv, dbg), got = _gdn_scan_bwd(qn, kn, v, bg, saved, inverses, do_gdn, "gdn_scan_bwd",
                                              ride=comm.ride("scan_bwd", big))
    comm.done("scan_bwd", got)
    dproj, d_conv, d_gates = _gdn_pre_bwd(proj, conv, w["conv"], g["alog"], g["dtb"], dqn, dkn, dvv, dbg, dgate,
                                          "gdn_pre_bwd")
    big["gdn_in"] = _matmul_tn(h0, dproj, 1, "d_gdn_in", tb=1408)[0]
    big.update(conv=d_conv, gates=d_gates)
    ride = comm.ride("in_bwd", big)
    grad_x, dg_pre0, *got = _norm_matmul_bwd([(dproj, w["gdn_in"], row(g["mix_pre"], 0), 1408)], x, dx1, "gdn_in_proj_bwd",
                                             ride=ride)
    comm.done("in_bwd", got[0] if got else [])

    small = {"pre0": dg_pre0, "x2": dg_x2, "post0": dg_post0, "post1": dg_post1, "mlp0": dg_mlp0, "mlp1": dg_mlp1,
             "gates": d_gates, "out_gain": d_out_gain, "conv": d_conv, "loss": loss}
    return grad_x, big, small


GATHER_STAGES = {"scan": ("gdn_out", "up0", "down0"), "mlp0": ("kv", "sb_q"), "sb": ("sb_o", "up1", "down1")}
SCATTER_STAGES = {"sb_bwd": ("up1", "down1", "sb_o"), "mlp0_bwd": ("sb_q", "kv"), "scan_bwd": ("up0", "down0", "gdn_out"),
                  "in_bwd": ("gdn_in",)}
SMALL_STAGES = {"mlp0_bwd": ("x2", "post1", "mlp1", "loss"), "scan_bwd": ("post0", "mlp0", "out_gain"),
                "in_bwd": ("conv", "gates")}


def _whole_weight(name, gathered):
    d = D_MODEL
    if name in ("up0", "up1"):
        return gathered
    if name == "kv":
        return gathered.reshape(2, N_DEV // 2, d, gathered.shape[2]).transpose(0, 2, 1, 3).reshape(2, d, d)
    if name == "gdn_in":
        whole = gathered.transpose(1, 0, 2).reshape(d, GDN_IN_COLS)
        return jnp.pad(whole, ((0, 0), (0, GDN_IN_PAD - GDN_IN_COLS)))[None]
    if name == "conv":
        return gathered.transpose(1, 0, 2).reshape(CONV_WIDTH, QKV)
    whole = gathered.reshape(gathered.shape[0] * gathered.shape[1], d)
    return whole[None] if name == "sb_q" else whole


def _owner_blocks(name, grad):
    if name in ("up0", "up1", "kv"):
        blocks = grad
    elif name == "gdn_in":
        blocks = grad[:, :GDN_IN_COLS].reshape(D_MODEL, N_DEV, GDN_IN_COLS // N_DEV).transpose(1, 0, 2)
    else:
        blocks = grad.reshape(N_DEV, grad.shape[0] // N_DEV, grad.shape[1])
    return blocks.astype(BF16)


class _Fsdp:
    def __init__(self, shards, weights):
        self.shards, self.w, self.landed, self.small_landed = shards, weights, {}, {}

    def ride(self, stage, grads=None):
        if stage in GATHER_STAGES:
            names = GATHER_STAGES[stage]
            return _Exchange([self.shards[nm] for nm in names], [True] * len(names))
        names, small = SCATTER_STAGES[stage], SMALL_STAGES.get(stage, ())
        return _Exchange([_owner_blocks(nm, grads[nm]) for nm in names] + [grads[nm] for nm in small],
                         [False] * len(names) + [True] * len(small))

    def done(self, stage, outs):
        if stage in GATHER_STAGES:
            for nm, out in zip(GATHER_STAGES[stage], outs):
                self.w[nm] = _whole_weight(nm, out)
        else:
            names = SCATTER_STAGES[stage]
            self.landed.update(zip(names, outs[:len(names)]))
            self.small_landed.update(zip(SMALL_STAGES.get(stage, ()), outs[len(names):]))


def _my_place():
    return lax.axis_index("x"), lax.axis_index("y"), lax.axis_index("c")


class _Exchange:
    def __init__(self, arrays, gather):
        self.arrays, self.gather, self.n = list(arrays), list(gather), len(arrays)
        any_spec = pl.BlockSpec(memory_space=pl.ANY)
        self.in_specs = [any_spec] * self.n
        self.out_specs = [any_spec] * self.n
        self.out_shape = [jax.ShapeDtypeStruct(((N_DEV,) + a.shape) if g else a.shape, a.dtype)
                          for a, g in zip(self.arrays, self.gather)]
        self.scratch = [pltpu.SemaphoreType.DMA((self.n, N_DEV - 1)), pltpu.SemaphoreType.DMA((self.n, N_DEV - 1)),
                        pltpu.SemaphoreType.DMA((self.n,))]

    def _copies(self, ins, outs, sems):
        send_sems, recv_sems, local_sems = sems
        x, y, c = _my_place()
        me = 4 * x + 2 * y + c
        copies = []
        for a in range(self.n):
            src = ins[a] if self.gather[a] else ins[a].at[me]
            copies.append(pltpu.make_async_copy(src, outs[a].at[me], local_sems.at[a]))
        for k in range(1, N_DEV):
            px = 1 - x if k & 4 else x
            py = 1 - y if k & 2 else y
            pc = 1 - c if k & 1 else c
            peer = 4 * px + 2 * py + pc
            for a in range(self.n):
                src = ins[a] if self.gather[a] else ins[a].at[peer]
                copies.append(pltpu.make_async_remote_copy(
                    src_ref=src, dst_ref=outs[a].at[me], send_sem=send_sems.at[a, k - 1], recv_sem=recv_sems.at[a, k - 1],
                    device_id=(px, py, pc), device_id_type=MESH))
        return copies

    def start(self, ins, outs, sems):
        for cp in self._copies(ins, outs, sems):
            cp.start()

    def wait(self, ins, outs, sems):
        for cp in self._copies(ins, outs, sems):
            cp.wait()


def _call(body, operands, *, name, out_shape, in_specs, out_specs, grid=(), scratch_shapes=(), compiler_params=None,
          ride=None, first=None, last=None):
    n_in, n_out, n_scr = len(operands), len(out_shape), len(scratch_shapes)
    if ride is None:
        outs = pl.pallas_call(body, name=name, grid=grid, in_specs=in_specs, out_specs=out_specs, out_shape=out_shape,
                              scratch_shapes=scratch_shapes, compiler_params=compiler_params)(*operands)
        return list(outs), []
    r = ride.n

    def riding(*refs):
        ins, r_ins = refs[:n_in], refs[n_in:n_in + r]
        outs, r_outs = refs[n_in + r:n_in + r + n_out], refs[n_in + r + n_out:n_in + 2 * r + n_out]
        scr, sems = refs[n_in + 2 * r + n_out:n_in + 2 * r + n_out + n_scr], refs[n_in + 2 * r + n_out + n_scr:]

        @pl.when(first())
        def _():
            ride.start(r_ins, r_outs, sems)

        body(*ins, *outs, *scr)

        @pl.when(last())
        def _():
            ride.wait(r_ins, r_outs, sems)

    outs = pl.pallas_call(
        riding, name=name, grid=grid, in_specs=list(in_specs) + ride.in_specs, out_specs=list(out_specs) + ride.out_specs,
        out_shape=list(out_shape) + ride.out_shape, scratch_shapes=list(scratch_shapes) + ride.scratch,
        compiler_params=compiler_params)(*operands, *ride.arrays)
    return list(outs[:n_out]), list(outs[n_out:])


def _exchange(arrays, gather, name):
    ride = _Exchange(arrays, gather)

    def body(*refs):
        ins, outs, sems = refs[:ride.n], refs[ride.n:2 * ride.n], refs[2 * ride.n:]
        ride.start(ins, outs, sems)
        ride.wait(ins, outs, sems)

    return pl.pallas_call(body, name=name, in_specs=ride.in_specs, out_specs=ride.out_specs, out_shape=ride.out_shape,
                          scratch_shapes=ride.scratch)(*arrays)


def _gather_two_level(arrays, name):
    n_arr = len(arrays)

    def body(*refs):
        ins, outs = refs[:n_arr], refs[n_arr:2 * n_arr]
        send_sems, recv_sems, local_sems = refs[2 * n_arr:]
        x, y, c = _my_place()
        sibling = (x, y, 1 - c)
        chips = [(1 - x, y), (x, 1 - y), (1 - x, 1 - y)]
        index = lambda px, py, pc: 4 * px + 2 * py + pc

        def copy(a, k, block, to, src=None):
            dst = outs[a].at[index(*block)]
            return pltpu.make_async_remote_copy(src_ref=dst if src is None else src, dst_ref=dst,
                                                send_sem=send_sems.at[a, k], recv_sem=recv_sems.at[a, k],
                                                device_id=to, device_id_type=MESH)

        mine = [pltpu.make_async_copy(ins[a], outs[a].at[index(x, y, c)], local_sems.at[a]) for a in range(n_arr)]
        first = [copy(a, 0, (x, y, c), sibling, src=ins[a]) for a in range(n_arr)]
        first += [copy(a, 1 + j, (x, y, c), (*chip, c), src=ins[a]) for j, chip in enumerate(chips) for a in range(n_arr)]
        for cp in mine + first:
            cp.start()
        passed = []
        for j, chip in enumerate(chips):
            for a in range(n_arr):
                copy(a, 1 + j, (*chip, c), (x, y, c)).wait_recv()
                passed.append(copy(a, 4 + j, (*chip, c), sibling))
                passed[-1].start()
        for a in range(n_arr):
            copy(a, 0, sibling, (x, y, c)).wait_recv()
            for j, chip in enumerate(chips):
                copy(a, 4 + j, (*chip, 1 - c), (x, y, c)).wait_recv()
        for cp in first + passed:
            cp.wait_send()
        for cp in mine:
            cp.wait()

    any_spec = pl.BlockSpec(memory_space=pl.ANY)
    return pl.pallas_call(
        body, name=name, in_specs=[any_spec] * n_arr, out_specs=[any_spec] * n_arr,
        out_shape=[jax.ShapeDtypeStruct((N_DEV,) + a.shape, a.dtype) for a in arrays],
        scratch_shapes=[pltpu.SemaphoreType.DMA((n_arr, N_DEV - 1)), pltpu.SemaphoreType.DMA((n_arr, N_DEV - 1)),
                        pltpu.SemaphoreType.DMA((n_arr,))])(*arrays)


def _adamw_math(g, w, m, v):
    m = ADAM_B1 * m + (1.0 - ADAM_B1) * g
    v = ADAM_B2 * v + (1.0 - ADAM_B2) * jnp.square(g)
    m_hat = m / (1.0 - ADAM_B1 ** ADAM_STEP)
    v_hat = v / (1.0 - ADAM_B2 ** ADAM_STEP)
    delta = -ADAM_LR * (m_hat / (jnp.sqrt(v_hat) + ADAM_EPS) + ADAM_WD * w)
    return delta, m, v


def _sum_devices(ref):
    total = ref[0].astype(F32)
    for d in range(1, N_DEV):
        total = total + ref[d].astype(F32)
    return total


def _reduce_adamw(landed, w, m, v, name, tr=256):
    r, c = w.shape
    tr = min(tr, r)
    assert r % tr == 0

    def body(l_ref, w_ref, m_ref, v_ref, g_ref, d_ref, nm_ref, nv_ref):
        g = _sum_devices(l_ref)
        g_ref[...] = g
        d_ref[...], nm_ref[...], nv_ref[...] = _adamw_math(g, w_ref[...], m_ref[...], v_ref[...])

    blk = pl.BlockSpec((tr, c), lambda i: (i, 0))
    return pl.pallas_call(
        body, name=name, grid=(r // tr,),
        in_specs=[pl.BlockSpec((N_DEV, tr, c), lambda i: (0, i, 0)), blk, blk, blk],
        out_specs=[blk] * 4, out_shape=[jax.ShapeDtypeStruct((r, c), F32)] * 4,
        compiler_params=_params("parallel"),
    )(landed, w, m, v)


def _reduce_adamw_layers(landed, w, m, v, name, tr=256):
    _, r, c = w.shape
    tr = min(tr, r)
    nr = r // tr
    assert r % tr == 0

    def body(l0_ref, l1_ref, w_ref, m_ref, v_ref, g_ref, d_ref, nm_ref, nv_ref):
        g = jnp.where(pl.program_id(0) == 0, _sum_devices(l0_ref), _sum_devices(l1_ref))
        g_ref[...] = g
        d_ref[...], nm_ref[...], nv_ref[...] = _adamw_math(g, w_ref[...], m_ref[...], v_ref[...])

    blk = pl.BlockSpec((None, tr, c), lambda l, i: (l, i, 0))
    return pl.pallas_call(
        body, name=name, grid=(2, nr),
        in_specs=[pl.BlockSpec((N_DEV, tr, c), lambda l, i: (0, jnp.where(l == 0, i, nr - 1), 0)),
                  pl.BlockSpec((N_DEV, tr, c), lambda l, i: (0, jnp.where(l == 1, i, 0), 0)), blk, blk, blk],
        out_specs=[blk] * 4, out_shape=[jax.ShapeDtypeStruct(w.shape, F32)] * 4,
        compiler_params=_params("arbitrary", "arbitrary"),
    )(landed[0], landed[1], w, m, v)


def _adamw(g, w, m, v, name):
    def body(g_ref, w_ref, m_ref, v_ref, d_ref, nm_ref, nv_ref):
        d_ref[...], nm_ref[...], nv_ref[...] = _adamw_math(g_ref[...], w_ref[...], m_ref[...], v_ref[...])

    return pl.pallas_call(body, name=name, out_shape=[jax.ShapeDtypeStruct(w.shape, F32)] * 3)(g, w, m, v)


def _small_update(landed, params, name):
    layout = {
        "mix_pre": [("pre0", 0), ("x2", 0)], "mix_post": [("post0", 0), ("post1", 0)],
        "mlp_pre": [("mlp0", 0), ("mlp1", 0)], "mlp_post": [("mlp0", 1), ("mlp1", 1)],
        "kv": [("kv", 0)], "alog": [("gates", 0)], "dtb": [("gates", 1)], "out_gain": [("out_gain", 0)],
    }
    landed_names = sorted(landed)
    param_names = sorted(params)
    n_l, n_p = len(landed_names), len(param_names)

    def body(*refs):
        l_refs = dict(zip(landed_names, refs[:n_l]))
        p_refs = {p: refs[n_l + 3 * i:n_l + 3 * i + 3] for i, p in enumerate(param_names)}
        outs = refs[n_l + 3 * n_p:]
        o_refs = {p: outs[4 * i:4 * i + 4] for i, p in enumerate(param_names)}
        conv_ref, loss_ref = outs[4 * n_p:]
        sums = {nm: _sum_devices(l_refs[nm]) for nm in landed_names}
        sums["kv"] = sums["x2"][1:2, :] + sums["x2"][2:3, :]
        for p in param_names:
            w_ref, m_ref, v_ref = p_refs[p]
            g_ref, d_ref, nm_ref, nv_ref = o_refs[p]
            for r, (src, src_row) in enumerate(layout[p]):
                g = sums[src][src_row:src_row + 1, :]
                g_ref[r:r + 1, :] = g
                d, nm, nv = _adamw_math(g, w_ref[r:r + 1, :], m_ref[r:r + 1, :], v_ref[r:r + 1, :])
                d_ref[r:r + 1, :] = d
                nm_ref[r:r + 1, :] = nm
                nv_ref[r:r + 1, :] = nv
        conv_ref[...] = sums["conv"]
        loss_ref[...] = sums["loss"]

    args = [landed[nm] for nm in landed_names]
    out_shape = []
    for p in param_names:
        args += list(params[p])
        out_shape += [jax.ShapeDtypeStruct(params[p][0].shape, F32)] * 4
    out_shape += [jax.ShapeDtypeStruct(landed["conv"].shape[1:], F32), jax.ShapeDtypeStruct(landed["loss"].shape[1:], F32)]
    outs = pl.pallas_call(body, name=name, out_shape=out_shape)(*args)
    result = {p: tuple(outs[4 * i:4 * i + 4]) for i, p in enumerate(param_names)}
    result["conv"], result["loss"] = outs[4 * n_p], outs[4 * n_p + 1]
    return result


def _lanes(vec, offset):
    return jnp.pad(vec[None, :], ((0, 0), (offset, 128 - offset - vec.shape[0])))


def kernel(x, mix_pre_gain, mix_post_gain, mlp_pre_gain, mlp_post_gain, mlp_w_up, mlp_w_down, gdn_w_in, gdn_conv_w, gdn_a_log, gdn_dt_bias, gdn_out_gain, gdn_w_out, kv_gain, w_kv, sb_w_q, sb_w_o, loss_target, m_mix_pre_gain, m_mix_post_gain, m_mlp_pre_gain, m_mlp_post_gain, m_mlp_w_up, m_mlp_w_down, m_gdn_w_in, m_gdn_conv_w, m_gdn_a_log, m_gdn_dt_bias, m_gdn_out_gain, m_gdn_w_out, m_kv_gain, m_w_kv, m_sb_w_q, m_sb_w_o, v_mix_pre_gain, v_mix_post_gain, v_mlp_pre_gain, v_mlp_post_gain, v_mlp_w_up, v_mlp_w_down, v_gdn_w_in, v_gdn_conv_w, v_gdn_a_log, v_gdn_dt_bias, v_gdn_out_gain, v_gdn_w_out, v_kv_gain, v_w_kv, v_sb_w_q, v_sb_w_o):
    me = 4 * lax.axis_index("x") + 2 * lax.axis_index("y") + lax.axis_index("c")
    bf = lambda a: a.astype(BF16)

    shards = {"up0": bf(mlp_w_up[0]), "up1": bf(mlp_w_up[1]), "down0": bf(mlp_w_down[0]), "down1": bf(mlp_w_down[1]),
              "gdn_out": bf(gdn_w_out[0]), "kv": bf(w_kv), "sb_q": bf(sb_w_q[0]), "sb_o": bf(sb_w_o[0])}
    gdn_in, conv = _gather_two_level([bf(gdn_w_in[0]), gdn_conv_w[0]], "gather_first_weights")
    comm = _Fsdp(shards, {"gdn_in": _whole_weight("gdn_in", gdn_in), "conv": _whole_weight("conv", conv)})
    gains = {"mix_pre": mix_pre_gain, "mix_post": mix_post_gain, "mlp_pre": mlp_pre_gain, "mlp_post": mlp_post_gain,
             "kv": kv_gain[None, :], "alog": _lanes(gdn_a_log[0], N_HEADS), "dtb": _lanes(gdn_dt_bias[0], N_HEADS),
             "out_gain": gdn_out_gain}

    grad_x, big, small = _local_step(x[0], loss_target[0], gains, comm)

    (pre0,) = _exchange([small["pre0"]], [True], "exchange_last_grad")
    small_landed = dict(comm.small_landed, pre0=pre0)

    results = {}
    results["mlp_w_up"] = _reduce_adamw_layers((comm.landed["up0"], comm.landed["up1"]), mlp_w_up, m_mlp_w_up, v_mlp_w_up,
                                               "adamw_mlp_w_up")
    results["mlp_w_down"] = _reduce_adamw_layers((comm.landed["down0"], comm.landed["down1"]), mlp_w_down, m_mlp_w_down,
                                                 v_mlp_w_down, "adamw_mlp_w_down")
    big_params = [("gdn_w_in", "gdn_in", gdn_w_in[0], m_gdn_w_in[0], v_gdn_w_in[0]),
                  ("gdn_w_out", "gdn_out", gdn_w_out[0], m_gdn_w_out[0], v_gdn_w_out[0]),
                  ("w_kv", "kv", w_kv, m_w_kv, v_w_kv), ("sb_w_q", "sb_q", sb_w_q[0], m_sb_w_q[0], v_sb_w_q[0]),
                  ("sb_w_o", "sb_o", sb_w_o[0], m_sb_w_o[0], v_sb_w_o[0])]
    for nm, short, w_, m_, v_ in big_params:
        results[nm] = _reduce_adamw(comm.landed[short], w_, m_, v_, "adamw_" + nm)
    lanes8 = lambda a: _lanes(a[0], N_HEADS)
    small_params = {
        "mix_pre": (mix_pre_gain, m_mix_pre_gain, v_mix_pre_gain), "mix_post": (mix_post_gain, m_mix_post_gain, v_mix_post_gain),
        "mlp_pre": (mlp_pre_gain, m_mlp_pre_gain, v_mlp_pre_gain), "mlp_post": (mlp_post_gain, m_mlp_post_gain, v_mlp_post_gain),
        "kv": (kv_gain[None, :], m_kv_gain[None, :], v_kv_gain[None, :]),
        "alog": (lanes8(gdn_a_log), lanes8(m_gdn_a_log), lanes8(v_gdn_a_log)),
        "dtb": (lanes8(gdn_dt_bias), lanes8(m_gdn_dt_bias), lanes8(v_gdn_dt_bias)),
        "out_gain": (gdn_out_gain, m_gdn_out_gain, v_gdn_out_gain),
    }
    sm = _small_update(small_landed, small_params, "small_update")
    conv_cols = QKV // N_DEV
    g_conv = lax.dynamic_slice(sm["conv"], (0, me * conv_cols), (8, conv_cols))[:CONV_WIDTH]
    conv_res = (g_conv,) + tuple(_adamw(g_conv, gdn_conv_w[0], m_gdn_conv_w[0], v_gdn_conv_w[0], "adamw_conv"))

    lead = lambda t: tuple(a[None] for a in t)
    heads8 = lambda t: tuple(a[:, N_HEADS:2 * N_HEADS] for a in t)
    per_weight = [
        sm["mix_pre"], sm["mix_post"], sm["mlp_pre"], sm["mlp_post"],
        tuple(results["mlp_w_up"]), tuple(results["mlp_w_down"]),
        lead(results["gdn_w_in"]), lead(conv_res), heads8(sm["alog"]), heads8(sm["dtb"]), sm["out_gain"],
        lead(results["gdn_w_out"]), tuple(a[0] for a in sm["kv"]), results["w_kv"], lead(results["sb_w_q"]), lead(results["sb_w_o"]),
    ]
    grads, deltas, new_ms, new_vs = zip(*per_weight)
    return (sm["loss"][0, 0], grad_x[None], *grads, *deltas, *new_ms, *new_vs)
```

```python
import functools
import math

import jax
import jax.numpy as jnp
from jax import lax
from jax.experimental import pallas as pl
from jax.experimental.pallas import tpu as pltpu

F32 = jnp.float32
BF16 = jnp.bfloat16

N_DEV = 8
D_MODEL = 1024
N_HEADS = 8
HEAD_DIM = 128
CHUNK = 64
CONV_WIDTH = 4
GDN_IN_COLS = 4 * D_MODEL + 2 * N_HEADS
GDN_IN_PAD = 4 * D_MODEL + 128
EPS = 1e-6

ADAM_LR = 0.001
ADAM_B1 = 0.9
ADAM_B2 = 0.999
ADAM_EPS = 1e-08
ADAM_WD = 0.01
ADAM_STEP = 10

VMEM_LIMIT_BYTES = 56 * 1024 * 1024
MESH = pl.DeviceIdType.MESH


def _params(*semantics):
    return pltpu.CompilerParams(dimension_semantics=semantics, vmem_limit_bytes=VMEM_LIMIT_BYTES)


def _dims(ta, tb):
    return (((0,) if ta else (1,), (1,) if tb else (0,)), ((), ()))


def _dot(a, b, ta=False, tb=False):
    return lax.dot_general(a.astype(BF16), b.astype(BF16), _dims(ta, tb), preferred_element_type=F32)


def _dot_f32(a, b, ta=False, tb=False):
    return lax.dot_general(a, b, _dims(ta, tb), precision=lax.Precision.HIGHEST, preferred_element_type=F32)


def _dot_3x(a, b, ta=False, tb=False):
    return lax.dot_general(a, b, _dims(ta, tb), precision=lax.Precision.HIGH, preferred_element_type=F32)


def _make_mm(dot):
    @functools.partial(jax.custom_vjp, nondiff_argnums=(2, 3))
    def mm(a, b, ta, tb):
        return dot(a, b, ta, tb)

    def fwd(a, b, ta, tb):
        return dot(a, b, ta, tb), (a, b)

    def bwd(ta, tb, res, g):
        a, b = res
        if not ta and not tb:
            return mm(g, b, False, True), mm(a, g, True, False)
        if not ta and tb:
            return mm(g, b, False, False), mm(g, a, True, False)
        if ta and not tb:
            return mm(b, g, False, True), mm(a, g, False, False)
        raise NotImplementedError

    mm.defvjp(fwd, bwd)
    return mm


_mm = _make_mm(_dot)
_mm_f32 = _make_mm(_dot_f32)


def _rms(x, gain):
    r = lax.rsqrt(jnp.mean(x * x, axis=-1, keepdims=True) + EPS)
    return x * r * gain


def _rms_bwd(x, gain, dy):
    r = lax.rsqrt(jnp.mean(x * x, axis=-1, keepdims=True) + EPS)
    xh = x * r
    dgain = jnp.sum(dy * xh, axis=0, keepdims=True)
    dxh = dy * gain
    dx = r * (dxh - xh * jnp.mean(dxh * xh, axis=-1, keepdims=True))
    return dx, dgain


def _silu(x):
    return x / (1.0 + jnp.exp(-x))


def _acc_rows(ref, first, rows):
    @pl.when(first)
    def _():
        ref[...] = jnp.zeros_like(ref)

    for r, val in enumerate(rows):
        ref[r:r + 1, :] += val


def _norm_matmul(x, gain, w, out_dtype, name, tm=1024, tn=1024):
    n, d = x.shape
    g, _, wc = w.shape
    tm, tn = min(tm, n), min(tn, wc)
    per = wc // tn
    assert n % tm == 0 and wc % tn == 0

    def body(x_ref, gain_ref, w_ref, out_ref, h_ref):
        @pl.when(pl.program_id(1) == 0)
        def _():
            h_ref[...] = _rms(x_ref[...], gain_ref[...]).astype(BF16)

        out_ref[...] = jnp.dot(h_ref[...], w_ref[...], preferred_element_type=F32).astype(out_dtype)

    return pl.pallas_call(
        body, name=name, grid=(n // tm, g * per),
        in_specs=[pl.BlockSpec((tm, d), lambda i, j: (i, 0)),
                  pl.BlockSpec((1, d), lambda i, j: (0, 0)),
                  pl.BlockSpec((None, d, tn), lambda i, j: (j // per, 0, j % per))],
        out_specs=[pl.BlockSpec((tm, tn), lambda i, j: (i, j)),
                   pl.BlockSpec((tm, d), lambda i, j: (i, 0))],
        out_shape=[jax.ShapeDtypeStruct((n, g * wc), out_dtype), jax.ShapeDtypeStruct((n, d), BF16)],
        compiler_params=_params("parallel", "arbitrary"),
    )(x, gain, w)


def _norm_matmul_bwd(branches, x, add, name, tm=1024, ride=None):
    n, d = x.shape
    tm = min(tm, n)
    tns = [min(tn, w.shape[2]) for _, w, _, tn in branches]
    pers = [w.shape[2] // tn for (_, w, _, _), tn in zip(branches, tns)]
    ncols = [w.shape[0] * per for (_, w, _, _), per in zip(branches, pers)]
    offs = [sum(ncols[:b]) for b in range(len(branches))]
    total = sum(ncols)
    nb = len(branches)

    def body(*refs):
        dout_refs, w_refs, gain_refs = refs[:nb], refs[nb:2 * nb], refs[2 * nb:3 * nb]
        x_ref, add_ref, dx_ref, dgain_ref, acc_ref, sum_ref = refs[3 * nb:]
        i, j = pl.program_id(0), pl.program_id(1)

        @pl.when((i == 0) & (j == 0))
        def _():
            dgain_ref[...] = jnp.zeros_like(dgain_ref)

        for b in range(nb):
            first, last = offs[b], offs[b] + ncols[b] - 1

            @pl.when((j >= first) & (j <= last))
            def _(b=b, first=first):
                part = _dot(dout_refs[b][...], w_refs[b][...], tb=True)

                @pl.when(j == first)
                def _():
                    acc_ref[...] = part

                @pl.when(j > first)
                def _():
                    acc_ref[...] += part

            @pl.when(j == last)
            def _(b=b):
                xv = x_ref[...]
                xh = xv * lax.rsqrt(jnp.mean(xv * xv, axis=-1, keepdims=True) + EPS)
                dh = acc_ref[...]
                dgain_ref[b:b + 1, :] += jnp.sum(dh * xh, axis=0, keepdims=True)
                scaled = dh * gain_refs[b][...]
                if b == 0:
                    sum_ref[...] = scaled
                else:
                    sum_ref[...] += scaled

        @pl.when(j == total - 1)
        def _():
            xv = x_ref[...]
            r = lax.rsqrt(jnp.mean(xv * xv, axis=-1, keepdims=True) + EPS)
            xh = xv * r
            dxh = sum_ref[...]
            dx_ref[...] = add_ref[...] + r * (dxh - xh * jnp.mean(dxh * xh, axis=-1, keepdims=True))

    def dout_spec(b):
        return pl.BlockSpec((tm, tns[b]), lambda i, j: (i, jnp.clip(j - offs[b], 0, ncols[b] - 1)))

    def w_spec(b):
        def index(i, j):
            c = jnp.clip(j - offs[b], 0, ncols[b] - 1)
            return (c // pers[b], 0, c % pers[b])
        return pl.BlockSpec((None, d, tns[b]), index)

    row = lambda i, j: (i, 0)
    const = lambda i, j: (0, 0)
    nrow = n // tm
    (dx, dgain), got = _call(
        body, [b[0] for b in branches] + [b[1] for b in branches] + [b[2] for b in branches] + [x, add],
        name=name, grid=(nrow, total),
        in_specs=[dout_spec(b) for b in range(nb)] + [w_spec(b) for b in range(nb)]
                 + [pl.BlockSpec((1, d), const)] * nb + [pl.BlockSpec((tm, d), row), pl.BlockSpec((tm, d), row)],
        out_specs=[pl.BlockSpec((tm, d), row), pl.BlockSpec((8, d), const)],
        out_shape=[jax.ShapeDtypeStruct((n, d), F32), jax.ShapeDtypeStruct((8, d), F32)],
        scratch_shapes=[pltpu.VMEM((tm, d), F32), pltpu.VMEM((tm, d), F32)],
        compiler_params=_params("arbitrary", "arbitrary"),
        ride=ride, first=lambda: (pl.program_id(0) == 0) & (pl.program_id(1) == 0),
        last=lambda: (pl.program_id(0) == nrow - 1) & (pl.program_id(1) == total - 1))
    return (dx, dgain) if ride is None else (dx, dgain, got)


def _matmul_tn(a, b, groups, name, ta=1024, tb=1024, tk=1024):
    bs = list(b) if isinstance(b, (list, tuple)) else [b]
    n, ka = a.shape
    _, kb = bs[0].shape
    wc = kb // groups
    ta, tb, tk = min(ta, ka), min(tb, wc), min(tk, n)
    per = wc // tb
    ncol = groups * per
    nk = n // tk
    assert ka % ta == 0 and wc % tb == 0 and n % tk == 0

    def body(a_ref, *refs):
        b_refs, (out_ref, acc_ref) = refs[:len(bs)], refs[len(bs):]
        j, k = pl.program_id(1), pl.program_id(2)

        @pl.when(k == 0)
        def _():
            acc_ref[...] = jnp.zeros_like(acc_ref)

        for m, b_ref in enumerate(b_refs):
            @pl.when((j >= m * ncol) & (j < (m + 1) * ncol))
            def _(b_ref=b_ref):
                acc_ref[...] += _dot(a_ref[...], b_ref[...], ta=True)

        @pl.when(k == nk - 1)
        def _():
            out_ref[...] = acc_ref[...].astype(BF16)

    def b_spec(m):
        return pl.BlockSpec((tk, tb), lambda i, j, k: (k, jnp.clip(j - m * ncol, 0, ncol - 1)))

    return pl.pallas_call(
        body, name=name, grid=(ka // ta, len(bs) * ncol, nk),
        in_specs=[pl.BlockSpec((tk, ta), lambda i, j, k: (k, i))] + [b_spec(m) for m in range(len(bs))],
        out_specs=pl.BlockSpec((None, ta, tb), lambda i, j, k: (j // per, i, j % per)),
        out_shape=jax.ShapeDtypeStruct((len(bs) * groups, ka, wc), BF16),
        scratch_shapes=[pltpu.VMEM((ta, tb), F32)],
        compiler_params=_params("parallel", "parallel", "arbitrary"),
    )(a, *bs)


def _gated_head_norm(o, gate, out_gain):
    parts = []
    for h in range(N_HEADS):
        sl = slice(h * HEAD_DIM, (h + 1) * HEAD_DIM)
        parts.append(_rms(o[:, sl], out_gain) * _silu(gate[:, sl]))
    return parts


def _out_proj(a_inputs, gated, x_in, w, gain, out_gain, name, tm=1024):
    n, d = x_in.shape
    k = w.shape[0]
    tm = min(tm, n)

    def body(*refs):
        if gated:
            o_ref, gate_ref, og_ref, x_ref, w_ref, gain_ref, xo_ref, y_ref, a_ref = refs
            parts = _gated_head_norm(o_ref[...], gate_ref[...], og_ref[...])
            for h, part in enumerate(parts):
                a_ref[:, h * HEAD_DIM:(h + 1) * HEAD_DIM] = part.astype(BF16)
            a = a_ref[...]
        else:
            a_in_ref, x_ref, w_ref, gain_ref, xo_ref, y_ref = refs
            a = a_in_ref[...]
        y = jnp.dot(a, w_ref[...], preferred_element_type=F32)
        y_ref[...] = y
        xo_ref[...] = x_ref[...] + _rms(y, gain_ref[...])

    row = lambda i: (i, 0)
    const = lambda i: (0, 0)
    if gated:
        a_specs = [pl.BlockSpec((tm, k), row), pl.BlockSpec((tm, D_MODEL), lambda i: (i, 3)),
                   pl.BlockSpec((1, HEAD_DIM), const)]
        a_args = list(a_inputs) + [out_gain]
    else:
        a_specs = [pl.BlockSpec((tm, k), row)]
        a_args = list(a_inputs)
    out_specs = [pl.BlockSpec((tm, d), row), pl.BlockSpec((tm, d), row)]
    out_shape = [jax.ShapeDtypeStruct((n, d), F32), jax.ShapeDtypeStruct((n, d), F32)]
    if gated:
        out_specs.append(pl.BlockSpec((tm, k), row))
        out_shape.append(jax.ShapeDtypeStruct((n, k), BF16))
    return pl.pallas_call(
        body, name=name, grid=(n // tm,),
        in_specs=a_specs + [pl.BlockSpec((tm, d), row), pl.BlockSpec((k, d), const), pl.BlockSpec((1, d), const)],
        out_specs=out_specs, out_shape=out_shape,
        compiler_params=_params("parallel"),
    )(*a_args, x_in, w, gain)


def _out_proj_bwd(dxo, y, w, gain, gated_inputs, name, tm=1024):
    n, d = dxo.shape
    k = w.shape[0]
    tm = min(tm, n)
    gated = gated_inputs is not None

    def body(*refs):
        if gated:
            (dxo_ref, y_ref, w_ref, gain_ref, o_ref, gate_ref, og_ref,
             dy_ref, dgain_ref, do_ref, dgate_ref, dog_ref) = refs
        else:
            dxo_ref, y_ref, w_ref, gain_ref, dy_ref, dgain_ref, da_ref = refs
        first = pl.program_id(0) == 0
        dy, dgain = _rms_bwd(y_ref[...], gain_ref[...], dxo_ref[...])
        dy_ref[...] = dy.astype(BF16)
        _acc_rows(dgain_ref, first, [dgain])
        da = _dot(dy_ref[...], w_ref[...], tb=True)
        if not gated:
            da_ref[...] = da.astype(BF16)
            return
        og = og_ref[...]
        dog = jnp.zeros_like(og)
        for h in range(N_HEADS):
            sl = slice(h * HEAD_DIM, (h + 1) * HEAD_DIM)
            o_h, g_h, da_h = o_ref[:, sl], gate_ref[:, sl], da[:, sl]
            s = 1.0 / (1.0 + jnp.exp(-g_h))
            do_h, dog_h = _rms_bwd(o_h, og, da_h * (g_h * s))
            do_ref[:, sl] = do_h
            dgate_ref[:, sl] = (da_h * _rms(o_h, og) * (s * (1.0 + g_h * (1.0 - s)))).astype(BF16)
            dog = dog + dog_h
        _acc_rows(dog_ref, first, [dog])

    row = lambda i: (i, 0)
    const = lambda i: (0, 0)
    in_specs = [pl.BlockSpec((tm, d), row), pl.BlockSpec((tm, d), row), pl.BlockSpec((k, d), const),
                pl.BlockSpec((1, d), const)]
    args = [dxo, y, w, gain]
    out_specs = [pl.BlockSpec((tm, d), row), pl.BlockSpec((8, d), const)]
    out_shape = [jax.ShapeDtypeStruct((n, d), BF16), jax.ShapeDtypeStruct((8, d), F32)]
    if gated:
        in_specs += [pl.BlockSpec((tm, k), row), pl.BlockSpec((tm, D_MODEL), lambda i: (i, 3)),
                     pl.BlockSpec((1, HEAD_DIM), const)]
        args += list(gated_inputs)
        out_specs += [pl.BlockSpec((tm, k), row), pl.BlockSpec((tm, k), row), pl.BlockSpec((8, HEAD_DIM), const)]
        out_shape += [jax.ShapeDtypeStruct((n, k), F32), jax.ShapeDtypeStruct((n, k), BF16),
                      jax.ShapeDtypeStruct((8, HEAD_DIM), F32)]
    else:
        out_specs.append(pl.BlockSpec((tm, k), row))
        out_shape.append(jax.ShapeDtypeStruct((n, k), BF16))
    return pl.pallas_call(
        body, name=name, grid=(n // tm,), in_specs=in_specs, out_specs=out_specs, out_shape=out_shape,
        compiler_params=_params("arbitrary"),
    )(*args)


def _mlp_fwd(x_in, g_pre, w_up, w_down, g_post, name, tm=1024, target=None, ride=None):
    n, d = x_in.shape
    g, _, wc = w_up.shape
    tm = min(tm, n)
    with_loss = target is not None

    def body(*refs):
        if with_loss:
            x_ref, gpre_ref, wup_ref, wdown_ref, gpost_ref, t_ref, xo_ref, y_ref, h_ref, u_ref, a_ref, loss_ref, acc_ref = refs
        else:
            x_ref, gpre_ref, wup_ref, wdown_ref, gpost_ref, xo_ref, y_ref, h_ref, u_ref, a_ref, acc_ref = refs
        i, j = pl.program_id(0), pl.program_id(1)

        @pl.when(j == 0)
        def _():
            h_ref[...] = _rms(x_ref[...], gpre_ref[...]).astype(BF16)
            acc_ref[...] = jnp.zeros_like(acc_ref)

        u = jnp.dot(h_ref[...], wup_ref[...], preferred_element_type=F32).astype(BF16)
        u_ref[...] = u
        a = jnp.square(jnp.maximum(u, 0))
        a_ref[...] = a
        acc_ref[...] += jnp.dot(a, wdown_ref[...], preferred_element_type=F32)

        @pl.when(j == g - 1)
        def _():
            y = acc_ref[...]
            y_ref[...] = y
            xo = x_ref[...] + _rms(y, gpost_ref[...])
            if with_loss:
                err = xo - t_ref[...]
                xo_ref[...] = err * (1.0 / d)
                part = 0.5 * jnp.sum(jnp.mean(err * err, axis=-1, keepdims=True), axis=0, keepdims=True)
                _acc_rows(loss_ref, i == 0, [jnp.broadcast_to(part, (1, 128))])
            else:
                xo_ref[...] = xo

    row = lambda i, j: (i, 0)
    const = lambda i, j: (0, 0)
    in_specs = [pl.BlockSpec((tm, d), row), pl.BlockSpec((1, d), const),
                pl.BlockSpec((None, d, wc), lambda i, j: (j, 0, 0)),
                pl.BlockSpec((wc, d), lambda i, j: (j, 0)), pl.BlockSpec((1, d), const)]
    out_specs = [pl.BlockSpec((tm, d), row), pl.BlockSpec((tm, d), row), pl.BlockSpec((tm, d), row),
                 pl.BlockSpec((tm, wc), lambda i, j: (i, j)), pl.BlockSpec((tm, wc), lambda i, j: (i, j))]
    out_shape = [jax.ShapeDtypeStruct((n, d), F32), jax.ShapeDtypeStruct((n, d), F32),
                 jax.ShapeDtypeStruct((n, d), BF16), jax.ShapeDtypeStruct((n, g * wc), BF16),
                 jax.ShapeDtypeStruct((n, g * wc), BF16)]
    args = [x_in, g_pre, w_up, w_down, g_post]
    if with_loss:
        in_specs.append(pl.BlockSpec((tm, d), row))
        out_specs.append(pl.BlockSpec((8, 128), const))
        out_shape.append(jax.ShapeDtypeStruct((8, 128), F32))
        args.append(target)
    nrow = n // tm
    outs, got = _call(
        body, args, name=name, grid=(nrow, g), in_specs=in_specs, out_specs=out_specs, out_shape=out_shape,
        scratch_shapes=[pltpu.VMEM((tm, d), F32)], compiler_params=_params("arbitrary", "arbitrary"),
        ride=ride, first=lambda: (pl.program_id(0) == 0) & (pl.program_id(1) == 0),
        last=lambda: (pl.program_id(0) == nrow - 1) & (pl.program_id(1) == g - 1))
    return outs if ride is None else (outs, got)


def _mlp_bwd(dxo, y, g_post, w_down, u, w_up, x_in, g_pre, name, tm=1024, ride=None):
    n, d = dxo.shape
    g, _, wc = w_up.shape
    tm = min(tm, n)

    def body(dxo_ref, y_ref, gpost_ref, wdown_ref, u_ref, wup_ref, x_ref, gpre_ref,
             dx_ref, dy_ref, du_ref, dgain_ref, acc_ref, dgpost_ref):
        i, j = pl.program_id(0), pl.program_id(1)

        @pl.when(j == 0)
        def _():
            dy, dgpost = _rms_bwd(y_ref[...], gpost_ref[...], dxo_ref[...])
            dy_ref[...] = dy.astype(BF16)
            dgpost_ref[...] = dgpost
            acc_ref[...] = jnp.zeros_like(acc_ref)

        da = _dot(dy_ref[...], wdown_ref[...], tb=True)
        du = (da * (2.0 * jnp.maximum(u_ref[...], 0).astype(F32))).astype(BF16)
        du_ref[...] = du
        acc_ref[...] += _dot(du, wup_ref[...], tb=True)

        @pl.when(j == g - 1)
        def _():
            dx, dgpre = _rms_bwd(x_ref[...], gpre_ref[...], acc_ref[...])
            dx_ref[...] = dxo_ref[...] + dx
            _acc_rows(dgain_ref, i == 0, [dgpre, dgpost_ref[...]])

    row = lambda i, j: (i, 0)
    const = lambda i, j: (0, 0)
    nrow = n // tm
    outs, got = _call(
        body, [dxo, y, g_post, w_down, u, w_up, x_in, g_pre], name=name, grid=(nrow, g),
        in_specs=[pl.BlockSpec((tm, d), row), pl.BlockSpec((tm, d), row), pl.BlockSpec((1, d), const),
                  pl.BlockSpec((wc, d), lambda i, j: (j, 0)), pl.BlockSpec((tm, wc), lambda i, j: (i, j)),
                  pl.BlockSpec((None, d, wc), lambda i, j: (j, 0, 0)), pl.BlockSpec((tm, d), row),
                  pl.BlockSpec((1, d), const)],
        out_specs=[pl.BlockSpec((tm, d), row), pl.BlockSpec((tm, d), row),
                   pl.BlockSpec((tm, wc), lambda i, j: (i, j)), pl.BlockSpec((8, d), const)],
        out_shape=[jax.ShapeDtypeStruct((n, d), F32), jax.ShapeDtypeStruct((n, d), BF16),
                   jax.ShapeDtypeStruct((n, g * wc), BF16), jax.ShapeDtypeStruct((8, d), F32)],
        scratch_shapes=[pltpu.VMEM((tm, d), F32), pltpu.VMEM((1, d), F32)],
        compiler_params=_params("arbitrary", "arbitrary"),
        ride=ride, first=lambda: (pl.program_id(0) == 0) & (pl.program_id(1) == 0),
        last=lambda: (pl.program_id(0) == nrow - 1) & (pl.program_id(1) == g - 1))
    return outs if ride is None else (outs, got)


QKV = 3 * D_MODEL


def _shifted(x, prev8, s):
    if s == 0:
        return x
    tm = x.shape[0]
    rolled = pltpu.roll(x, s, 0)
    head = pltpu.roll(prev8, s, 0)
    head = jnp.concatenate([head, jnp.zeros((tm - 8, x.shape[1]), x.dtype)], axis=0)
    rows = lax.broadcasted_iota(jnp.int32, x.shape, 0)
    return jnp.where(rows < s, head, rolled)


def _conv(x, prev8, conv_w):
    out = x * conv_w[3:4, :]
    for s in range(1, CONV_WIDTH):
        out = out + _shifted(x, prev8, s) * conv_w[3 - s:4 - s, :]
    return out


def _l2norm(x):
    return x * lax.rsqrt(jnp.sum(x * x, axis=-1, keepdims=True) + EPS)


def _gdn_act_head(cq, ck, cv):
    return _l2norm(_silu(cq)) * (HEAD_DIM ** -0.5), _l2norm(_silu(ck)), _silu(cv)


def _gdn_act_head_bwd(cq, ck, cv, dq, dk, dv):
    def silu_and_slope(c):
        s = 1.0 / (1.0 + jnp.exp(-c))
        return c * s, s * (1.0 + c * (1.0 - s))

    def through_norm(c, dy, scale):
        a, slope = silu_and_slope(c)
        n = lax.rsqrt(jnp.sum(a * a, axis=-1, keepdims=True) + EPS)
        t = dy * scale if scale != 1.0 else dy
        da = n * (t - a * (n * n * jnp.sum(t * a, axis=-1, keepdims=True)))
        return da * slope

    return through_norm(cq, dq, HEAD_DIM ** -0.5), through_norm(ck, dk, 1.0), dv * silu_and_slope(cv)[1]


def _gdn_gates(ba, alog, dtb):
    lane = lax.broadcasted_iota(jnp.int32, ba.shape, 1)
    beta = 1.0 / (1.0 + jnp.exp(-ba))
    t = ba + dtb
    softplus = jnp.maximum(t, 0.0) + jnp.log(1.0 + jnp.exp(-jnp.abs(t)))
    g = -jnp.exp(alog) * softplus
    return jnp.where(lane < N_HEADS, beta, jnp.where(lane < 2 * N_HEADS, g, 0.0))


def _qkv_cols(part, h):
    start = part * D_MODEL + h * HEAD_DIM
    return slice(start, start + HEAD_DIM)


def _prev_rows_spec(tm, cols):
    per = tm // 8
    return pl.BlockSpec((8, cols), lambda i: (jnp.maximum(i * per - 1, 0), 0))


def _gdn_pre(proj, conv_w, alog, dtb, name, tm=256):
    n = proj.shape[0]

    def body(x_ref, prev_ref, ba_ref, cw_ref, alog_ref, dtb_ref, q_ref, k_ref, v_ref, bg_ref, c_ref):
        first = pl.program_id(0) == 0
        for h in range(N_HEADS):
            convs = []
            for part in range(3):
                sl = _qkv_cols(part, h)
                prev8 = jnp.where(first, 0.0, prev_ref[:, sl])
                convs.append(_conv(x_ref[:, sl], prev8, cw_ref[:, sl]))
                c_ref[:, sl] = convs[-1]
            out = slice(h * HEAD_DIM, (h + 1) * HEAD_DIM)
            q_ref[:, out], k_ref[:, out], v_ref[:, out] = _gdn_act_head(*convs)
        bg_ref[...] = _gdn_gates(ba_ref[...], alog_ref[...], dtb_ref[...])

    row = lambda i: (i, 0)
    const = lambda i: (0, 0)
    return pl.pallas_call(
        body, name=name, grid=(n // tm,),
        in_specs=[pl.BlockSpec((tm, QKV), row), _prev_rows_spec(tm, QKV),
                  pl.BlockSpec((tm, 128), lambda i: (i, 4 * D_MODEL // 128)),
                  pl.BlockSpec((CONV_WIDTH, QKV), const), pl.BlockSpec((1, 128), const), pl.BlockSpec((1, 128), const)],
        out_specs=[pl.BlockSpec((tm, D_MODEL), row)] * 3 + [pl.BlockSpec((tm, 128), row), pl.BlockSpec((tm, QKV), row)],
        out_shape=[jax.ShapeDtypeStruct((n, D_MODEL), F32)] * 3 + [jax.ShapeDtypeStruct((n, 128), F32),
                                                                  jax.ShapeDtypeStruct((n, QKV), F32)],
        compiler_params=_params("parallel"),
    )(proj, proj, proj, conv_w, alog, dtb)


def _gdn_pre_bwd(proj, conv, conv_w, alog, dtb, dq, dk, dv, dbg, dgate, name, tm=256):
    n = proj.shape[0]
    nt = n // tm

    def body(x_ref, c_ref, ba_ref, cw_ref, alog_ref, dtb_ref, dq_ref, dk_ref, dv_ref, dbg_ref, dgate_ref,
             dproj_ref, dcw_ref, dgates_ref, carry_ref, head_ref):
        step = pl.program_id(0)

        @pl.when(step == 0)
        def _():
            carry_ref[...] = jnp.zeros_like(carry_ref)
            head_ref[...] = jnp.zeros_like(head_ref)
            dcw_ref[...] = jnp.zeros_like(dcw_ref)

        rows = lax.broadcasted_iota(jnp.int32, (tm, HEAD_DIM), 0)
        rows8 = lax.broadcasted_iota(jnp.int32, (8, HEAD_DIM), 0)
        for h in range(N_HEADS):
            cols = [_qkv_cols(part, h) for part in range(3)]
            out = slice(h * HEAD_DIM, (h + 1) * HEAD_DIM)
            dcs = _gdn_act_head_bwd(*[c_ref[:, sl] for sl in cols], dq_ref[:, out], dk_ref[:, out], dv_ref[:, out])
            for sl, dc in zip(cols, dcs):
                x = x_ref[:, sl]
                cw = cw_ref[:, sl]
                dx = dc * cw[3:4, :]
                wrapped = jnp.zeros((8, HEAD_DIM), F32)
                dcw_ref[3:4, sl] += jnp.sum(dc * x, axis=0, keepdims=True)
                for s in range(1, CONV_WIDTH):
                    up = pltpu.roll(dc, tm - s, 0)
                    r = up * cw[3 - s:4 - s, :]
                    dx = dx + jnp.where(rows < tm - s, r, 0.0)
                    wrapped = wrapped + jnp.where(rows[tm - 8:, :] >= tm - s, r[tm - 8:, :], 0.0)
                    inside = jnp.sum(jnp.where(rows < tm - s, x * up, 0.0), axis=0, keepdims=True)
                    next_dc = pltpu.roll(head_ref[:, sl], 8 - s, 0)
                    across = jnp.sum(jnp.where(rows8 >= 8 - s, x[tm - 8:, :] * next_dc, 0.0), axis=0, keepdims=True)
                    dcw_ref[3 - s:4 - s, sl] += inside + across
                from_next = jnp.concatenate([jnp.zeros((tm - 8, HEAD_DIM), F32), carry_ref[:, sl]], axis=0)
                dproj_ref[:, sl] = (dx + from_next).astype(BF16)
                carry_ref[:, sl] = wrapped
                head_ref[:, sl] = dc[:8, :]

        _, vjp = jax.vjp(_gdn_gates, ba_ref[...], alog_ref[...], dtb_ref[...])
        dba, dalog, ddtb = vjp(dbg_ref[...])
        dproj_ref[:, QKV:4 * D_MODEL] = dgate_ref[...]
        dproj_ref[:, 4 * D_MODEL:] = dba.astype(BF16)
        _acc_rows(dgates_ref, step == 0, [dalog, ddtb])

    rev = lambda i: nt - 1 - i
    row = lambda i: (rev(i), 0)
    const = lambda i: (0, 0)
    return pl.pallas_call(
        body, name=name, grid=(nt,),
        in_specs=[pl.BlockSpec((tm, QKV), row), pl.BlockSpec((tm, QKV), row),
                  pl.BlockSpec((tm, 128), lambda i: (rev(i), 4 * D_MODEL // 128)),
                  pl.BlockSpec((CONV_WIDTH, QKV), const), pl.BlockSpec((1, 128), const), pl.BlockSpec((1, 128), const),
                  pl.BlockSpec((tm, D_MODEL), row), pl.BlockSpec((tm, D_MODEL), row), pl.BlockSpec((tm, D_MODEL), row),
                  pl.BlockSpec((tm, 128), row), pl.BlockSpec((tm, D_MODEL), row)],
        out_specs=[pl.BlockSpec((tm, GDN_IN_PAD), row), pl.BlockSpec((8, QKV), const), pl.BlockSpec((8, 128), const)],
        out_shape=[jax.ShapeDtypeStruct((n, GDN_IN_PAD), BF16), jax.ShapeDtypeStruct((8, QKV), F32),
                   jax.ShapeDtypeStruct((8, 128), F32)],
        scratch_shapes=[pltpu.VMEM((8, QKV), F32), pltpu.VMEM((8, QKV), F32)],
        compiler_params=_params("arbitrary"),
    )(proj, conv, proj, conv_w, alog, dtb, dq, dk, dv, dbg, dgate)


@jax.custom_vjp
def _unit_lower_inverses(lowers):
    n = lowers[0].shape[0]
    eye = (lax.broadcasted_iota(jnp.int32, (n, n), 0) == lax.broadcasted_iota(jnp.int32, (n, n), 1)).astype(F32)
    xs = [-low for low in lowers]
    ps = [eye + x for x in xs]
    ys = [_dot_3x(x, x) for x in xs]
    for _ in range(int(math.log2(CHUNK)) - 2):
        both = [_dot_3x(y, jnp.concatenate([y, p], axis=1)) for y, p in zip(ys, ps)]
        ys = [b[:, :n] for b in both]
        ps = [p + b[:, n:] for p, b in zip(ps, both)]
    return tuple(p + _dot_3x(y, p) for y, p in zip(ys, ps))


def _unit_lower_inverses_fwd(lowers):
    ts = _unit_lower_inverses(lowers)
    return ts, ts


def _unit_lower_inverses_bwd(ts, dts):
    left = [_dot_3x(t, dt, ta=True) for t, dt in zip(ts, dts)]
    return (tuple(-_dot_3x(l, t, tb=True) for l, t in zip(left, ts)),)


_unit_lower_inverses.defvjp(_unit_lower_inverses_fwd, _unit_lower_inverses_bwd)


@jax.custom_vjp
def _known_inverses(lowers, inverses):
    return inverses


def _known_inverses_bwd(ts, dts):
    (d_lowers,) = _unit_lower_inverses_bwd(ts, dts)
    return d_lowers, tuple(jnp.zeros_like(t) for t in ts)


_known_inverses.defvjp(lambda lowers, inverses: (inverses, inverses), _known_inverses_bwd)


@functools.partial(jax.custom_vjp, nondiff_argnums=(1,))
def _halves(x, axis):
    half = x.shape[axis] // 2
    return (x[:half], x[half:]) if axis == 0 else (x[:, :half], x[:, half:])


_halves.defvjp(lambda x, axis: (_halves(x, axis), None), lambda axis, _, g: (jnp.concatenate(g, axis=axis),))

GDN_STEP_CHUNKS = 2


def _gdn_chunks(qs, ks, vs, bgs, states, inverses=None, keep_inverses=False):
    c = CHUNK
    heads = range(N_HEADS)
    items = [(j, h) for j in range(len(bgs)) for h in heads]
    row = lax.broadcasted_iota(jnp.int32, (c, c), 0)
    col = lax.broadcasted_iota(jnp.int32, (c, c), 1)
    incl, strict, eye = row >= col, row > col, row == col
    lane = lax.broadcasted_iota(jnp.int32, (c, 128), 1)
    rowc = lax.broadcasted_iota(jnp.int32, (c, 1), 0)
    gc_all = [_mm_f32(incl.astype(F32), bg, False, False) for bg in bgs]
    q, k, v = ([xs[j][h] for j, h in items] for xs in (qs, ks, vs))
    n = range(len(items))
    beta = [jnp.sum(jnp.where(lane == h, bgs[j], 0.0), axis=1, keepdims=True) for j, h in items]
    gc = [jnp.sum(jnp.where(lane == N_HEADS + h, gc_all[j], 0.0), axis=1, keepdims=True) for j, h in items]
    gc_row = [jnp.sum(jnp.where(eye, gc[i], 0.0), axis=0, keepdims=True) for i in n]
    gc_last = [jnp.sum(jnp.where(rowc == c - 1, gc[i], 0.0), axis=0, keepdims=True) for i in n]
    decay = [jnp.where(incl, jnp.exp(jnp.where(incl, gc[i] - gc_row[i], 0.0)), 0.0) for i in n]
    kb = [k[i] * beta[i] for i in n]
    kb_q = [_halves(_mm(jnp.concatenate([kb[i], q[i]], axis=0), k[i], False, True), 0) for i in n]
    lower = tuple(jnp.where(strict, kb_q[i][0] * decay[i], 0.0) for i in n)
    attn = [kb_q[i][1] * decay[i] for i in n]
    t_mat = _unit_lower_inverses(lower) if inverses is None else _known_inverses(lower, inverses)
    egc = [jnp.exp(gc[i]) for i in n]
    w_u = [_halves(_mm(t_mat[i], jnp.concatenate([kb[i] * egc[i], v[i] * beta[i]], axis=1), False, False), 1)
           for i in n]
    w, u = [x[0] for x in w_u], [x[1] for x in w_u]
    qg = [q[i] * egc[i] for i in n]
    kg = [k[i] * jnp.exp(gc_last[i] - gc[i]) for i in n]
    outs, cur = [], list(states)
    for j in range(len(bgs)):
        at = lambda h: j * N_HEADS + h
        w_qg = [_halves(_mm(jnp.concatenate([w[at(h)], qg[at(h)]], axis=0), cur[h], False, False), 0)
                for h in heads]
        v_new = [u[at(h)] - w_qg[h][0] for h in heads]
        outs.append(tuple(w_qg[h][1] + _mm(attn[at(h)], v_new[h], False, False) for h in heads))
        cur = [cur[h] * jnp.exp(gc_last[at(h)]) + _mm(kg[at(h)], v_new[h], True, False) for h in heads]
    return (tuple(outs), tuple(cur), t_mat) if keep_inverses else (tuple(outs), tuple(cur))


def _chunk_head_slices(ref, nch):
    return tuple(tuple(ref[j * CHUNK:(j + 1) * CHUNK, h * HEAD_DIM:(h + 1) * HEAD_DIM] for h in range(N_HEADS))
                 for j in range(nch))


def _store_chunk_heads(ref, values):
    for j, chunk in enumerate(values):
        for h, val in enumerate(chunk):
            ref[j * CHUNK:(j + 1) * CHUNK, h * HEAD_DIM:(h + 1) * HEAD_DIM] = val


def _gdn_scan(qn, kn, v, bg, name, ride=None):
    n = qn.shape[0]
    nch = GDN_STEP_CHUNKS
    rows = nch * CHUNK
    nc = n // rows

    def body(q_ref, k_ref, v_ref, bg_ref, o_ref, saved_ref, inv_ref, state_ref):
        @pl.when(pl.program_id(0) == 0)
        def _():
            state_ref[...] = jnp.zeros_like(state_ref)

        states = tuple(state_ref[h] for h in range(N_HEADS))
        for h in range(N_HEADS):
            saved_ref[h] = states[h]
        bgs = tuple(bg_ref[j * CHUNK:(j + 1) * CHUNK, :] for j in range(nch))
        outs, new_states, inverses = _gdn_chunks(_chunk_head_slices(q_ref, nch), _chunk_head_slices(k_ref, nch),
                                                 _chunk_head_slices(v_ref, nch), bgs, states, keep_inverses=True)
        _store_chunk_heads(o_ref, outs)
        for h in range(N_HEADS):
            state_ref[h] = new_states[h]
        for item, inverse in enumerate(inverses):
            inv_ref[item] = inverse

    row = lambda i: (i, 0)
    return _call(
        body, [qn, kn, v, bg], name=name, grid=(nc,),
        in_specs=[pl.BlockSpec((rows, D_MODEL), row)] * 3 + [pl.BlockSpec((rows, 128), row)],
        out_specs=[pl.BlockSpec((rows, D_MODEL), row),
                   pl.BlockSpec((None, N_HEADS, HEAD_DIM, HEAD_DIM), lambda i: (i, 0, 0, 0)),
                   pl.BlockSpec((None, nch * N_HEADS, CHUNK, CHUNK), lambda i: (i, 0, 0, 0))],
        out_shape=[jax.ShapeDtypeStruct((n, D_MODEL), F32),
                   jax.ShapeDtypeStruct((nc, N_HEADS, HEAD_DIM, HEAD_DIM), F32),
                   jax.ShapeDtypeStruct((nc, nch * N_HEADS, CHUNK, CHUNK), F32)],
        scratch_shapes=[pltpu.VMEM((N_HEADS, HEAD_DIM, HEAD_DIM), F32)],
        compiler_params=_params("arbitrary"),
        ride=ride, first=lambda: pl.program_id(0) == 0, last=lambda: pl.program_id(0) == nc - 1)


def _gdn_scan_bwd(qn, kn, v, bg, saved, inverses, do, name, ride=None):
    n = qn.shape[0]
    nch = GDN_STEP_CHUNKS
    rows = nch * CHUNK
    nc = n // rows

    def body(q_ref, k_ref, v_ref, bg_ref, saved_ref, inv_ref, do_ref, dq_ref, dk_ref, dv_ref, dbg_ref, dstate_ref):
        @pl.when(pl.program_id(0) == 0)
        def _():
            dstate_ref[...] = jnp.zeros_like(dstate_ref)

        states = tuple(saved_ref[h] for h in range(N_HEADS))
        bgs = tuple(bg_ref[j * CHUNK:(j + 1) * CHUNK, :] for j in range(nch))
        known = tuple(inv_ref[item] for item in range(nch * N_HEADS))
        _, vjp = jax.vjp(functools.partial(_gdn_chunks, inverses=known), _chunk_head_slices(q_ref, nch),
                         _chunk_head_slices(k_ref, nch), _chunk_head_slices(v_ref, nch), bgs, states)
        dstates = tuple(dstate_ref[h] for h in range(N_HEADS))
        dqs, dks, dvs, dbgs, dprev = vjp((_chunk_head_slices(do_ref, nch), dstates))
        _store_chunk_heads(dq_ref, dqs)
        _store_chunk_heads(dk_ref, dks)
        _store_chunk_heads(dv_ref, dvs)
        for h in range(N_HEADS):
            dstate_ref[h] = dprev[h]
        for j in range(nch):
            dbg_ref[j * CHUNK:(j + 1) * CHUNK, :] = dbgs[j]

    row = lambda i: (nc - 1 - i, 0)
    return _call(
        body, [qn, kn, v, bg, saved, inverses, do], name=name, grid=(nc,),
        in_specs=[pl.BlockSpec((rows, D_MODEL), row)] * 3 + [pl.BlockSpec((rows, 128), row),
                  pl.BlockSpec((None, N_HEADS, HEAD_DIM, HEAD_DIM), lambda i: (nc - 1 - i, 0, 0, 0)),
                  pl.BlockSpec((None, nch * N_HEADS, CHUNK, CHUNK), lambda i: (nc - 1 - i, 0, 0, 0)),
                  pl.BlockSpec((rows, D_MODEL), row)],
        out_specs=[pl.BlockSpec((rows, D_MODEL), row)] * 3 + [pl.BlockSpec((rows, 128), row)],
        out_shape=[jax.ShapeDtypeStruct((n, D_MODEL), F32)] * 3 + [jax.ShapeDtypeStruct((n, 128), F32)],
        scratch_shapes=[pltpu.VMEM((N_HEADS, HEAD_DIM, HEAD_DIM), F32)],
        compiler_params=_params("arbitrary"),
        ride=ride, first=lambda: pl.program_id(0) == 0, last=lambda: pl.program_id(0) == nc - 1)


SB_BQ = 512
SB_BK = 256
SB_SUB = 128
SB_ROWS = 128
SB_SCALE = HEAD_DIM ** -0.5
SB_DEAD = -105.0


def _sb_terms(z, before):
    e = jnp.exp(-jnp.abs(z))
    log_beta = jnp.minimum(z, 0.0) - jnp.log(1.0 + e)
    log_1m = log_beta - z
    if before is not None:
        log_1m = jnp.where(before, log_1m, 0.0)
    return e, log_beta, log_1m


def _tri_ones(n, cmp):
    r = lax.broadcasted_iota(jnp.int32, (n, 2 * n), 0)
    c = lax.broadcasted_iota(jnp.int32, (n, 2 * n), 1)
    return jnp.where((c >= n) | cmp(r, c), 1.0, 0.0).astype(BF16)


def _sums(x, tri_ones):
    both = jnp.dot(x.astype(BF16), tri_ones, preferred_element_type=F32)
    n = x.shape[1]
    return both[:, :n], both[:, n:]


def _sb_chunk_mask(diagonal, r, s):
    if not diagonal or s * SB_SUB + SB_SUB - 1 < r * SB_ROWS:
        return None
    if s * SB_SUB >= r * SB_ROWS + SB_ROWS - 1:
        return "empty"
    rows = r * SB_ROWS + lax.broadcasted_iota(jnp.int32, (SB_ROWS, SB_SUB), 0)
    cols = s * SB_SUB + lax.broadcasted_iota(jnp.int32, (SB_ROWS, SB_SUB), 1)
    return cols < rows


def _sb_attention(q, kv, name, bq=SB_BQ, ride=None):
    n = q.shape[0]
    bq = min(bq, n)
    bk = min(SB_BK, bq)
    nrc = bq // SB_ROWS

    def body(q_ref, k_ref, v_ref, o_ref, l_ref, n_ref, z_scr, a_scr):
        i = pl.program_id(1)
        qb = q_ref[...]
        after = _tri_ones(SB_SUB, lambda r, c: r > c)

        def span(start, kw, c_sum, acc, diagonal):
            start = pl.multiple_of(start, kw)
            k_w = k_ref[pl.ds(start, kw), :]
            v_w = v_ref[pl.ds(start, kw), :]
            z_scr[:, :kw] = lax.dot_general(qb, k_w, _dims(False, True), preferred_element_type=F32)
            c_rows = [c_sum[r * SB_ROWS:(r + 1) * SB_ROWS] for r in range(nrc)]
            for s in reversed(range(kw // SB_SUB)):
                cols = slice(s * SB_SUB, (s + 1) * SB_SUB)
                for r in range(nrc):
                    rows = slice(r * SB_ROWS, (r + 1) * SB_ROWS)
                    before = _sb_chunk_mask(diagonal, r, s)
                    if isinstance(before, str):
                        a_scr[rows, cols] = jnp.zeros((SB_ROWS, SB_SUB), BF16)
                        continue
                    _, log_beta, log_1m = _sb_terms(z_scr[rows, cols] * SB_SCALE, before)
                    tail, total = _sums(log_1m, after)
                    a = jnp.exp(log_beta + c_rows[r] + tail)
                    if before is not None:
                        a = jnp.where(before, a, 0.0)
                    a_scr[rows, cols] = a.astype(BF16)
                    c_rows[r] = c_rows[r] + total
            acc = acc + jnp.dot(a_scr[:, :kw], v_w, preferred_element_type=F32)
            return jnp.concatenate(c_rows, axis=0), acc

        c_sum, acc = span(i * bq, bq, jnp.zeros((bq, SB_SUB), F32), jnp.zeros((bq, HEAD_DIM), F32), True)

        def more(state):
            done, c, _ = state
            return (done < i * (bq // bk)) & (jnp.max(c) > SB_DEAD)

        def step(state):
            done, c, a = state
            c, a = span(i * bq - (done + 1) * bk, bk, c, a, False)
            return done + 1, c, a

        done, c_sum, acc = lax.while_loop(more, step, (jnp.int32(0), c_sum, acc))
        o_ref[...] = acc.astype(BF16)
        l_ref[...] = c_sum
        n_ref[...] = jnp.full((8, 128), done.astype(F32), F32)

    nq = n // bq
    return _call(
        body, [q, kv, kv], name=name, grid=(N_HEADS, nq),
        in_specs=[pl.BlockSpec((bq, HEAD_DIM), lambda h, i: (i, h)),
                  pl.BlockSpec((n, HEAD_DIM), lambda h, i: (0, h)),
                  pl.BlockSpec((n, HEAD_DIM), lambda h, i: (0, N_HEADS + h))],
        out_specs=[pl.BlockSpec((bq, HEAD_DIM), lambda h, i: (i, h)),
                   pl.BlockSpec((None, bq, 128), lambda h, i: (h, i, 0)),
                   pl.BlockSpec((None, None, 8, 128), lambda h, i: (h, i, 0, 0))],
        out_shape=[jax.ShapeDtypeStruct((n, D_MODEL), BF16), jax.ShapeDtypeStruct((N_HEADS, n, 128), F32),
                   jax.ShapeDtypeStruct((N_HEADS, nq, 8, 128), F32)],
        scratch_shapes=[pltpu.VMEM((bq, bq), F32), pltpu.VMEM((bq, bq), BF16)],
        compiler_params=_params("arbitrary", "arbitrary"),
        ride=ride, first=lambda: (pl.program_id(0) == 0) & (pl.program_id(1) == 0),
        last=lambda: (pl.program_id(0) == N_HEADS - 1) & (pl.program_id(1) == nq - 1))


def _sb_attention_bwd(q, kv, do, lsum, spans, name, bq=SB_BQ, ride=None):
    n = q.shape[0]
    bq = min(bq, n)
    bk = min(SB_BK, bq)
    nrc = bq // SB_ROWS

    def body(q_ref, k_ref, v_ref, do_ref, l_ref, n_ref, dq_ref, dk_out, dv_out, z_scr, da_scr, a_scr, dz_scr, dk_ref, dv_ref):
        i = pl.program_id(1)

        @pl.when(i == 0)
        def _():
            dk_ref[...] = jnp.zeros_like(dk_ref)
            dv_ref[...] = jnp.zeros_like(dv_ref)

        qb = q_ref[...]
        dob = do_ref[...]
        lt_rows = [l_ref[r * SB_ROWS:(r + 1) * SB_ROWS, :] for r in range(nrc)]
        upto = _tri_ones(SB_SUB, lambda r, c: r <= c)
        below = _tri_ones(SB_SUB, lambda r, c: r < c)

        def span(start, kw, l_sum, g_sum, dq, diagonal):
            start = pl.multiple_of(start, kw)
            k_w = k_ref[pl.ds(start, kw), :]
            v_w = v_ref[pl.ds(start, kw), :]
            z_scr[:, :kw] = lax.dot_general(qb, k_w, _dims(False, True), preferred_element_type=F32)
            da_scr[:, :kw] = lax.dot_general(dob, v_w, _dims(False, True), preferred_element_type=F32)
            l_rows = [l_sum[r * SB_ROWS:(r + 1) * SB_ROWS] for r in range(nrc)]
            g_rows = [g_sum[r * SB_ROWS:(r + 1) * SB_ROWS] for r in range(nrc)]
            for s in range(kw // SB_SUB):
                cols = slice(s * SB_SUB, (s + 1) * SB_SUB)
                for r in range(nrc):
                    rows = slice(r * SB_ROWS, (r + 1) * SB_ROWS)
                    before = _sb_chunk_mask(diagonal, r, s)
                    if isinstance(before, str):
                        a_scr[rows, cols] = jnp.zeros((SB_ROWS, SB_SUB), BF16)
                        dz_scr[rows, cols] = jnp.zeros((SB_ROWS, SB_SUB), BF16)
                        continue
                    zs = z_scr[rows, cols] * SB_SCALE
                    e, log_beta, log_1m = _sb_terms(zs, before)
                    l_prefix, l_total = _sums(log_1m, upto)
                    a = jnp.exp(log_beta + (lt_rows[r] - (l_rows[r] + l_prefix)))
                    if before is not None:
                        a = jnp.where(before, a, 0.0)
                    g = a * da_scr[rows, cols]
                    g_prefix, g_total = _sums(g, below)
                    inv = 1.0 / (1.0 + e)
                    beta = jnp.where(zs >= 0, inv, e * inv)
                    dz = (g - (g + g_rows[r] + g_prefix) * beta) * SB_SCALE
                    if before is not None:
                        dz = jnp.where(before, dz, 0.0)
                    a_scr[rows, cols] = a.astype(BF16)
                    dz_scr[rows, cols] = dz.astype(BF16)
                    l_rows[r] = l_rows[r] + l_total
                    g_rows[r] = g_rows[r] + g_total
            dz_w = dz_scr[:, :kw]
            dq = dq + jnp.dot(dz_w, k_w, preferred_element_type=F32)
            dk_ref[pl.ds(start, kw), :] += lax.dot_general(dz_w, qb, _dims(True, False), preferred_element_type=F32)
            dv_ref[pl.ds(start, kw), :] += lax.dot_general(a_scr[:, :kw], dob, _dims(True, False),
                                                           preferred_element_type=F32)
            return jnp.concatenate(l_rows, axis=0), jnp.concatenate(g_rows, axis=0), dq

        taken = jnp.clip(jnp.max(n_ref[...]).astype(jnp.int32), 0, i * (bq // bk))
        zero = jnp.zeros((bq, SB_SUB), F32)
        carry = lax.fori_loop(0, taken, lambda j, c: span(i * bq - (taken - j) * bk, bk, c[0], c[1], c[2], False),
                              (zero, zero, jnp.zeros((bq, HEAD_DIM), F32)))
        _, _, dq = span(i * bq, bq, carry[0], carry[1], carry[2], True)
        dq_ref[...] = dq.astype(BF16)

        @pl.when(i == nq - 1)
        def _():
            dk_out[...] = dk_ref[...].astype(BF16)
            dv_out[...] = dv_ref[...].astype(BF16)

    nq = n // bq
    return _call(
        body, [q, kv, kv, do, lsum, spans], name=name, grid=(N_HEADS, nq),
        in_specs=[pl.BlockSpec((bq, HEAD_DIM), lambda h, i: (i, h)),
                  pl.BlockSpec((n, HEAD_DIM), lambda h, i: (0, h)),
                  pl.BlockSpec((n, HEAD_DIM), lambda h, i: (0, N_HEADS + h)),
                  pl.BlockSpec((bq, HEAD_DIM), lambda h, i: (i, h)),
                  pl.BlockSpec((None, bq, 128), lambda h, i: (h, i, 0)),
                  pl.BlockSpec((None, None, 8, 128), lambda h, i: (h, i, 0, 0))],
        out_specs=[pl.BlockSpec((bq, HEAD_DIM), lambda h, i: (i, h)),
                   pl.BlockSpec((n, HEAD_DIM), lambda h, i: (0, h)),
                   pl.BlockSpec((n, HEAD_DIM), lambda h, i: (0, h))],
        out_shape=[jax.ShapeDtypeStruct((n, D_MODEL), BF16)] * 3,
        scratch_shapes=[pltpu.VMEM((bq, bq), F32), pltpu.VMEM((bq, bq), F32), pltpu.VMEM((bq, bq), BF16),
                        pltpu.VMEM((bq, bq), BF16), pltpu.VMEM((n, HEAD_DIM), F32), pltpu.VMEM((n, HEAD_DIM), F32)],
        compiler_params=_params("arbitrary", "arbitrary"),
        ride=ride, first=lambda: (pl.program_id(0) == 0) & (pl.program_id(1) == 0),
        last=lambda: (pl.program_id(0) == N_HEADS - 1) & (pl.program_id(1) == nq - 1))


def _local_step(x, target, gains, comm):
    g = gains
    w = comm.w
    big = {}
    row = lambda a, i: a[i:i + 1, :]

    proj, h0 = _norm_matmul(x, row(g["mix_pre"], 0), w["gdn_in"], F32, "gdn_in_proj", tn=1408)
    qn, kn, v, bg, conv = _gdn_pre(proj, w["conv"], g["alog"], g["dtb"], "gdn_pre")
    (o_gdn, saved, inverses), got = _gdn_scan(qn, kn, v, bg, "gdn_scan", ride=comm.ride("scan"))
    comm.done("scan", got)
    x1, y_mix0, a_gdn = _out_proj((o_gdn, proj), True, x, w["gdn_out"], row(g["mix_post"], 0), g["out_gain"],
                                  "gdn_out_proj", tm=512)
    ride = comm.ride("mlp0")
    res = _mlp_fwd(x1, row(g["mlp_pre"], 0), w["up0"], w["down0"], row(g["mlp_post"], 0), "mlp0", ride=ride)
    (x2, y_mlp0, h_mlp0, u0, a0), got = res if ride is not None else (res, [])
    comm.done("mlp0", got)
    kv, h_kv = _norm_matmul(x2, g["kv"], w["kv"], BF16, "kv_proj", tm=2048)
    q, h_q = _norm_matmul(x2, row(g["mix_pre"], 1), w["sb_q"], BF16, "sb_q_proj")
    (o_sb, lsum, spans), got = _sb_attention(q, kv, "sb_attention", ride=comm.ride("sb"))
    comm.done("sb", got)
    x3, y_mix1 = _out_proj((o_sb,), False, x2, w["sb_o"], row(g["mix_post"], 1), None, "sb_out_proj")
    dx4, y_mlp1, h_mlp1, u1, a1, loss = _mlp_fwd(x3, row(g["mlp_pre"], 1), w["up1"], w["down1"], row(g["mlp_post"], 1),
                                                 "mlp1_loss", target=target)

    dx3, dy_mlp1, du1, dg_mlp1 = _mlp_bwd(dx4, y_mlp1, row(g["mlp_post"], 1), w["down1"], u1, w["up1"], x3,
                                          row(g["mlp_pre"], 1), "mlp1_bwd")
    big["down1"] = _matmul_tn(a1, dy_mlp1, 1, "d_down1")[0]
    big["up1"] = _matmul_tn(h_mlp1, du1, N_DEV, "d_up1", tk=2048)
    dy_mix1, dg_post1, do_sb = _out_proj_bwd(dx3, y_mix1, w["sb_o"], row(g["mix_post"], 1), None, "sb_out_proj_bwd")
    big["sb_o"] = _matmul_tn(o_sb, dy_mix1, 1, "d_sb_o")[0]
    (dq, dk, dv), got = _sb_attention_bwd(q, kv, do_sb, lsum, spans, "sb_attention_bwd", ride=comm.ride("sb_bwd", big))
    comm.done("sb_bwd", got)
    big["sb_q"] = _matmul_tn(h_q, dq, 1, "d_sb_q")[0]
    big["kv"] = _matmul_tn(h_kv, [dk, dv], N_DEV // 2, "d_w_kv", tk=2048)
    dx2, dg_x2 = _norm_matmul_bwd([(dq, w["sb_q"], row(g["mix_pre"], 1), 1024), (dk, w["kv"][0:1], g["kv"], 1024),
                                   (dv, w["kv"][1:2], g["kv"], 1024)], x2, dx3, "qkv_proj_bwd", tm=512)

    big.update(x2=dg_x2, post1=dg_post1, mlp1=dg_mlp1, loss=loss)
    ride = comm.ride("mlp0_bwd", big)
    res = _mlp_bwd(dx2, y_mlp0, row(g["mlp_post"], 0), w["down0"], u0, w["up0"], x1, row(g["mlp_pre"], 0), "mlp0_bwd",
                   ride=ride)
    (dx1, dy_mlp0, du0, dg_mlp0), got = res if ride is not None else (res, [])
    comm.done("mlp0_bwd", got)
    big["down0"] = _matmul_tn(a0, dy_mlp0, 1, "d_down0")[0]
    big["up0"] = _matmul_tn(h_mlp0, du0, N_DEV, "d_up0", tk=2048)
    dy_mix0, dg_post0, do_gdn, dgate, d_out_gain = _out_proj_bwd(
        dx1, y_mix0, w["gdn_out"], row(g["mix_post"], 0), (o_gdn, proj, g["out_gain"]), "gdn_out_proj_bwd", tm=512)
    big["gdn_out"] = _matmul_tn(a_gdn, dy_mix0, 1, "d_gdn_out")[0]
    big.update(post0=dg_post0, mlp0=dg_mlp0, out_gain=d_out_gain)
    (dqn, dkn, dvv, dbg), got = _gdn_scan_bwd(qn, kn, v, bg, saved, inverses, do_gdn, "gdn_scan_bwd",
                                              ride=comm.ride("scan_bwd", big))
    comm.done("scan_bwd", got)
    dproj, d_conv, d_gates = _gdn_pre_bwd(proj, conv, w["conv"], g["alog"], g["dtb"], dqn, dkn, dvv, dbg, dgate,
                                          "gdn_pre_bwd")
    big["gdn_in"] = _matmul_tn(h0, dproj, 1, "d_gdn_in", tb=1408)[0]
    big.update(conv=d_conv, gates=d_gates)
    ride = comm.ride("in_bwd", big)
    grad_x, dg_pre0, *got = _norm_matmul_bwd([(dproj, w["gdn_in"], row(g["mix_pre"], 0), 1408)], x, dx1, "gdn_in_proj_bwd",
                                             ride=ride)
    comm.done("in_bwd", got[0] if got else [])

    small = {"pre0": dg_pre0, "x2": dg_x2, "post0": dg_post0, "post1": dg_post1, "mlp0": dg_mlp0, "mlp1": dg_mlp1,
             "gates": d_gates, "out_gain": d_out_gain, "conv": d_conv, "loss": loss}
    return grad_x, big, small


GATHER_STAGES = {"scan": ("gdn_out", "up0", "down0"), "mlp0": ("kv", "sb_q"), "sb": ("sb_o", "up1", "down1")}
SCATTER_STAGES = {"sb_bwd": ("up1", "down1", "sb_o"), "mlp0_bwd": ("sb_q", "kv"), "scan_bwd": ("up0", "down0", "gdn_out"),
                  "in_bwd": ("gdn_in",)}
SMALL_STAGES = {"mlp0_bwd": ("x2", "post1", "mlp1", "loss"), "scan_bwd": ("post0", "mlp0", "out_gain"),
                "in_bwd": ("conv", "gates")}


def _whole_weight(name, gathered):
    d = D_MODEL
    if name in ("up0", "up1"):
        return gathered
    if name == "kv":
        return gathered.reshape(2, N_DEV // 2, d, gathered.shape[2]).transpose(0, 2, 1, 3).reshape(2, d, d)
    if name == "gdn_in":
        whole = gathered.transpose(1, 0, 2).reshape(d, GDN_IN_COLS)
        return jnp.pad(whole, ((0, 0), (0, GDN_IN_PAD - GDN_IN_COLS)))[None]
    if name == "conv":
        return gathered.transpose(1, 0, 2).reshape(CONV_WIDTH, QKV)
    whole = gathered.reshape(gathered.shape[0] * gathered.shape[1], d)
    return whole[None] if name == "sb_q" else whole


def _owner_blocks(name, grad):
    if name in ("up0", "up1", "kv"):
        blocks = grad
    elif name == "gdn_in":
        blocks = grad[:, :GDN_IN_COLS].reshape(D_MODEL, N_DEV, GDN_IN_COLS // N_DEV).transpose(1, 0, 2)
    else:
        blocks = grad.reshape(N_DEV, grad.shape[0] // N_DEV, grad.shape[1])
    return blocks.astype(BF16)


class _Fsdp:
    def __init__(self, shards, weights):
        self.shards, self.w, self.landed, self.small_landed = shards, weights, {}, {}

    def ride(self, stage, grads=None):
        if stage in GATHER_STAGES:
            names = GATHER_STAGES[stage]
            return _Exchange([self.shards[nm] for nm in names], [True] * len(names))
        names, small = SCATTER_STAGES[stage], SMALL_STAGES.get(stage, ())
        return _Exchange([_owner_blocks(nm, grads[nm]) for nm in names] + [grads[nm] for nm in small],
                         [False] * len(names) + [True] * len(small))

    def done(self, stage, outs):
        if stage in GATHER_STAGES:
            for nm, out in zip(GATHER_STAGES[stage], outs):
                self.w[nm] = _whole_weight(nm, out)
        else:
            names = SCATTER_STAGES[stage]
            self.landed.update(zip(names, outs[:len(names)]))
            self.small_landed.update(zip(SMALL_STAGES.get(stage, ()), outs[len(names):]))


def _my_place():
    return lax.axis_index("x"), lax.axis_index("y"), lax.axis_index("c")


class _Exchange:
    def __init__(self, arrays, gather):
        self.arrays, self.gather, self.n = list(arrays), list(gather), len(arrays)
        any_spec = pl.BlockSpec(memory_space=pl.ANY)
        self.in_specs = [any_spec] * self.n
        self.out_specs = [any_spec] * self.n
        self.out_shape = [jax.ShapeDtypeStruct(((N_DEV,) + a.shape) if g else a.shape, a.dtype)
                          for a, g in zip(self.arrays, self.gather)]
        self.scratch = [pltpu.SemaphoreType.DMA((self.n, N_DEV - 1)), pltpu.SemaphoreType.DMA((self.n, N_DEV - 1)),
                        pltpu.SemaphoreType.DMA((self.n,))]

    def _copies(self, ins, outs, sems):
        send_sems, recv_sems, local_sems = sems
        x, y, c = _my_place()
        me = 4 * x + 2 * y + c
        copies = []
        for a in range(self.n):
            src = ins[a] if self.gather[a] else ins[a].at[me]
            copies.append(pltpu.make_async_copy(src, outs[a].at[me], local_sems.at[a]))
        for k in range(1, N_DEV):
            px = 1 - x if k & 4 else x
            py = 1 - y if k & 2 else y
            pc = 1 - c if k & 1 else c
            peer = 4 * px + 2 * py + pc
            for a in range(self.n):
                src = ins[a] if self.gather[a] else ins[a].at[peer]
                copies.append(pltpu.make_async_remote_copy(
                    src_ref=src, dst_ref=outs[a].at[me], send_sem=send_sems.at[a, k - 1], recv_sem=recv_sems.at[a, k - 1],
                    device_id=(px, py, pc), device_id_type=MESH))
        return copies

    def start(self, ins, outs, sems):
        for cp in self._copies(ins, outs, sems):
            cp.start()

    def wait(self, ins, outs, sems):
        for cp in self._copies(ins, outs, sems):
            cp.wait()


def _call(body, operands, *, name, out_shape, in_specs, out_specs, grid=(), scratch_shapes=(), compiler_params=None,
          ride=None, first=None, last=None):
    n_in, n_out, n_scr = len(operands), len(out_shape), len(scratch_shapes)
    if ride is None:
        outs = pl.pallas_call(body, name=name, grid=grid, in_specs=in_specs, out_specs=out_specs, out_shape=out_shape,
                              scratch_shapes=scratch_shapes, compiler_params=compiler_params)(*operands)
        return list(outs), []
    r = ride.n

    def riding(*refs):
        ins, r_ins = refs[:n_in], refs[n_in:n_in + r]
        outs, r_outs = refs[n_in + r:n_in + r + n_out], refs[n_in + r + n_out:n_in + 2 * r + n_out]
        scr, sems = refs[n_in + 2 * r + n_out:n_in + 2 * r + n_out + n_scr], refs[n_in + 2 * r + n_out + n_scr:]

        @pl.when(first())
        def _():
            ride.start(r_ins, r_outs, sems)

        body(*ins, *outs, *scr)

        @pl.when(last())
        def _():
            ride.wait(r_ins, r_outs, sems)

    outs = pl.pallas_call(
        riding, name=name, grid=grid, in_specs=list(in_specs) + ride.in_specs, out_specs=list(out_specs) + ride.out_specs,
        out_shape=list(out_shape) + ride.out_shape, scratch_shapes=list(scratch_shapes) + ride.scratch,
        compiler_params=compiler_params)(*operands, *ride.arrays)
    return list(outs[:n_out]), list(outs[n_out:])


def _exchange(arrays, gather, name):
    ride = _Exchange(arrays, gather)

    def body(*refs):
        ins, outs, sems = refs[:ride.n], refs[ride.n:2 * ride.n], refs[2 * ride.n:]
        ride.start(ins, outs, sems)
        ride.wait(ins, outs, sems)

    return pl.pallas_call(body, name=name, in_specs=ride.in_specs, out_specs=ride.out_specs, out_shape=ride.out_shape,
                          scratch_shapes=ride.scratch)(*arrays)


def _gather_two_level(arrays, name):
    n_arr = len(arrays)

    def body(*refs):
        ins, outs = refs[:n_arr], refs[n_arr:2 * n_arr]
        send_sems, recv_sems, local_sems = refs[2 * n_arr:]
        x, y, c = _my_place()
        sibling = (x, y, 1 - c)
        chips = [(1 - x, y), (x, 1 - y), (1 - x, 1 - y)]
        index = lambda px, py, pc: 4 * px + 2 * py + pc

        def copy(a, k, block, to, src=None):
            dst = outs[a].at[index(*block)]
            return pltpu.make_async_remote_copy(src_ref=dst if src is None else src, dst_ref=dst,
                                                send_sem=send_sems.at[a, k], recv_sem=recv_sems.at[a, k],
                                                device_id=to, device_id_type=MESH)

        mine = [pltpu.make_async_copy(ins[a], outs[a].at[index(x, y, c)], local_sems.at[a]) for a in range(n_arr)]
        first = [copy(a, 0, (x, y, c), sibling, src=ins[a]) for a in range(n_arr)]
        first += [copy(a, 1 + j, (x, y, c), (*chip, c), src=ins[a]) for j, chip in enumerate(chips) for a in range(n_arr)]
        for cp in mine + first:
            cp.start()
        passed = []
        for j, chip in enumerate(chips):
            for a in range(n_arr):
                copy(a, 1 + j, (*chip, c), (x, y, c)).wait_recv()
                passed.append(copy(a, 4 + j, (*chip, c), sibling))
                passed[-1].start()
        for a in range(n_arr):
            copy(a, 0, sibling, (x, y, c)).wait_recv()
            for j, chip in enumerate(chips):
                copy(a, 4 + j, (*chip, 1 - c), (x, y, c)).wait_recv()
        for cp in first + passed:
            cp.wait_send()
        for cp in mine:
            cp.wait()

    any_spec = pl.BlockSpec(memory_space=pl.ANY)
    return pl.pallas_call(
        body, name=name, in_specs=[any_spec] * n_arr, out_specs=[any_spec] * n_arr,
        out_shape=[jax.ShapeDtypeStruct((N_DEV,) + a.shape, a.dtype) for a in arrays],
        scratch_shapes=[pltpu.SemaphoreType.DMA((n_arr, N_DEV - 1)), pltpu.SemaphoreType.DMA((n_arr, N_DEV - 1)),
                        pltpu.SemaphoreType.DMA((n_arr,))])(*arrays)


def _adamw_math(g, w, m, v):
    m = ADAM_B1 * m + (1.0 - ADAM_B1) * g
    v = ADAM_B2 * v + (1.0 - ADAM_B2) * jnp.square(g)
    m_hat = m / (1.0 - ADAM_B1 ** ADAM_STEP)
    v_hat = v / (1.0 - ADAM_B2 ** ADAM_STEP)
    delta = -ADAM_LR * (m_hat / (jnp.sqrt(v_hat) + ADAM_EPS) + ADAM_WD * w)
    return delta, m, v


def _sum_devices(ref):
    total = ref[0].astype(F32)
    for d in range(1, N_DEV):
        total = total + ref[d].astype(F32)
    return total


def _reduce_adamw(landed, w, m, v, name, tr=256):
    r, c = w.shape
    tr = min(tr, r)
    assert r % tr == 0

    def body(l_ref, w_ref, m_ref, v_ref, g_ref, d_ref, nm_ref, nv_ref):
        g = _sum_devices(l_ref)
        g_ref[...] = g
        d_ref[...], nm_ref[...], nv_ref[...] = _adamw_math(g, w_ref[...], m_ref[...], v_ref[...])

    blk = pl.BlockSpec((tr, c), lambda i: (i, 0))
    return pl.pallas_call(
        body, name=name, grid=(r // tr,),
        in_specs=[pl.BlockSpec((N_DEV, tr, c), lambda i: (0, i, 0)), blk, blk, blk],
        out_specs=[blk] * 4, out_shape=[jax.ShapeDtypeStruct((r, c), F32)] * 4,
        compiler_params=_params("parallel"),
    )(landed, w, m, v)


def _reduce_adamw_layers(landed, w, m, v, name, tr=256):
    _, r, c = w.shape
    tr = min(tr, r)
    nr = r // tr
    assert r % tr == 0

    def body(l0_ref, l1_ref, w_ref, m_ref, v_ref, g_ref, d_ref, nm_ref, nv_ref):
        g = jnp.where(pl.program_id(0) == 0, _sum_devices(l0_ref), _sum_devices(l1_ref))
        g_ref[...] = g
        d_ref[...], nm_ref[...], nv_ref[...] = _adamw_math(g, w_ref[...], m_ref[...], v_ref[...])

    blk = pl.BlockSpec((None, tr, c), lambda l, i: (l, i, 0))
    return pl.pallas_call(
        body, name=name, grid=(2, nr),
        in_specs=[pl.BlockSpec((N_DEV, tr, c), lambda l, i: (0, jnp.where(l == 0, i, nr - 1), 0)),
                  pl.BlockSpec((N_DEV, tr, c), lambda l, i: (0, jnp.where(l == 1, i, 0), 0)), blk, blk, blk],
        out_specs=[blk] * 4, out_shape=[jax.ShapeDtypeStruct(w.shape, F32)] * 4,
        compiler_params=_params("arbitrary", "arbitrary"),
    )(landed[0], landed[1], w, m, v)


def _adamw(g, w, m, v, name):
    def body(g_ref, w_ref, m_ref, v_ref, d_ref, nm_ref, nv_ref):
        d_ref[...], nm_ref[...], nv_ref[...] = _adamw_math(g_ref[...], w_ref[...], m_ref[...], v_ref[...])

    return pl.pallas_call(body, name=name, out_shape=[jax.ShapeDtypeStruct(w.shape, F32)] * 3)(g, w, m, v)


def _small_update(landed, params, name):
    layout = {
        "mix_pre": [("pre0", 0), ("x2", 0)], "mix_post": [("post0", 0), ("post1", 0)],
        "mlp_pre": [("mlp0", 0), ("mlp1", 0)], "mlp_post": [("mlp0", 1), ("mlp1", 1)],
        "kv": [("kv", 0)], "alog": [("gates", 0)], "dtb": [("gates", 1)], "out_gain": [("out_gain", 0)],
    }
    landed_names = sorted(landed)
    param_names = sorted(params)
    n_l, n_p = len(landed_names), len(param_names)

    def body(*refs):
        l_refs = dict(zip(landed_names, refs[:n_l]))
        p_refs = {p: refs[n_l + 3 * i:n_l + 3 * i + 3] for i, p in enumerate(param_names)}
        outs = refs[n_l + 3 * n_p:]
        o_refs = {p: outs[4 * i:4 * i + 4] for i, p in enumerate(param_names)}
        conv_ref, loss_ref = outs[4 * n_p:]
        sums = {nm: _sum_devices(l_refs[nm]) for nm in landed_names}
        sums["kv"] = sums["x2"][1:2, :] + sums["x2"][2:3, :]
        for p in param_names:
            w_ref, m_ref, v_ref = p_refs[p]
            g_ref, d_ref, nm_ref, nv_ref = o_refs[p]
            for r, (src, src_row) in enumerate(layout[p]):
                g = sums[src][src_row:src_row + 1, :]
                g_ref[r:r + 1, :] = g
                d, nm, nv = _adamw_math(g, w_ref[r:r + 1, :], m_ref[r:r + 1, :], v_ref[r:r + 1, :])
                d_ref[r:r + 1, :] = d
                nm_ref[r:r + 1, :] = nm
                nv_ref[r:r + 1, :] = nv
        conv_ref[...] = sums["conv"]
        loss_ref[...] = sums["loss"]

    args = [landed[nm] for nm in landed_names]
    out_shape = []
    for p in param_names:
        args += list(params[p])
        out_shape += [jax.ShapeDtypeStruct(params[p][0].shape, F32)] * 4
    out_shape += [jax.ShapeDtypeStruct(landed["conv"].shape[1:], F32), jax.ShapeDtypeStruct(landed["loss"].shape[1:], F32)]
    outs = pl.pallas_call(body, name=name, out_shape=out_shape)(*args)
    result = {p: tuple(outs[4 * i:4 * i + 4]) for i, p in enumerate(param_names)}
    result["conv"], result["loss"] = outs[4 * n_p], outs[4 * n_p + 1]
    return result


def _lanes(vec, offset):
    return jnp.pad(vec[None, :], ((0, 0), (offset, 128 - offset - vec.shape[0])))


def kernel(x, mix_pre_gain, mix_post_gain, mlp_pre_gain, mlp_post_gain, mlp_w_up, mlp_w_down, gdn_w_in, gdn_conv_w, gdn_a_log, gdn_dt_bias, gdn_out_gain, gdn_w_out, kv_gain, w_kv, sb_w_q, sb_w_o, loss_target, m_mix_pre_gain, m_mix_post_gain, m_mlp_pre_gain, m_mlp_post_gain, m_mlp_w_up, m_mlp_w_down, m_gdn_w_in, m_gdn_conv_w, m_gdn_a_log, m_gdn_dt_bias, m_gdn_out_gain, m_gdn_w_out, m_kv_gain, m_w_kv, m_sb_w_q, m_sb_w_o, v_mix_pre_gain, v_mix_post_gain, v_mlp_pre_gain, v_mlp_post_gain, v_mlp_w_up, v_mlp_w_down, v_gdn_w_in, v_gdn_conv_w, v_gdn_a_log, v_gdn_dt_bias, v_gdn_out_gain, v_gdn_w_out, v_kv_gain, v_w_kv, v_sb_w_q, v_sb_w_o):
    me = 4 * lax.axis_index("x") + 2 * lax.axis_index("y") + lax.axis_index("c")
    bf = lambda a: a.astype(BF16)

    shards = {"up0": bf(mlp_w_up[0]), "up1": bf(mlp_w_up[1]), "down0": bf(mlp_w_down[0]), "down1": bf(mlp_w_down[1]),
              "gdn_out": bf(gdn_w_out[0]), "kv": bf(w_kv), "sb_q": bf(sb_w_q[0]), "sb_o": bf(sb_w_o[0])}
    gdn_in, conv = _gather_two_level([bf(gdn_w_in[0]), gdn_conv_w[0]], "gather_first_weights")
    comm = _Fsdp(shards, {"gdn_in": _whole_weight("gdn_in", gdn_in), "conv": _whole_weight("conv", conv)})
    gains = {"mix_pre": mix_pre_gain, "mix_post": mix_post_gain, "mlp_pre": mlp_pre_gain, "mlp_post": mlp_post_gain,
             "kv": kv_gain[None, :], "alog": _lanes(gdn_a_log[0], N_HEADS), "dtb": _lanes(gdn_dt_bias[0], N_HEADS),
             "out_gain": gdn_out_gain}

    grad_x, big, small = _local_step(x[0], loss_target[0], gains, comm)

    (pre0,) = _exchange([small["pre0"]], [True], "exchange_last_grad")
    small_landed = dict(comm.small_landed, pre0=pre0)

    results = {}
    results["mlp_w_up"] = _reduce_adamw_layers((comm.landed["up0"], comm.landed["up1"]), mlp_w_up, m_mlp_w_up, v_mlp_w_up,
                                               "adamw_mlp_w_up")
    results["mlp_w_down"] = _reduce_adamw_layers((comm.landed["down0"], comm.landed["down1"]), mlp_w_down, m_mlp_w_down,
                                                 v_mlp_w_down, "adamw_mlp_w_down")
    big_params = [("gdn_w_in", "gdn_in", gdn_w_in[0], m_gdn_w_in[0], v_gdn_w_in[0]),
                  ("gdn_w_out", "gdn_out", gdn_w_out[0], m_gdn_w_out[0], v_gdn_w_out[0]),
                  ("w_kv", "kv", w_kv, m_w_kv, v_w_kv), ("sb_w_q", "sb_q", sb_w_q[0], m_sb_w_q[0], v_sb_w_q[0]),
                  ("sb_w_o", "sb_o", sb_w_o[0], m_sb_w_o[0], v_sb_w_o[0])]
    for nm, short, w_, m_, v_ in big_params:
        results[nm] = _reduce_adamw(comm.landed[short], w_, m_, v_, "adamw_" + nm)
    lanes8 = lambda a: _lanes(a[0], N_HEADS)
    small_params = {
        "mix_pre": (mix_pre_gain, m_mix_pre_gain, v_mix_pre_gain), "mix_post": (mix_post_gain, m_mix_post_gain, v_mix_post_gain),
        "mlp_pre": (mlp_pre_gain, m_mlp_pre_gain, v_mlp_pre_gain), "mlp_post": (mlp_post_gain, m_mlp_post_gain, v_mlp_post_gain),
        "kv": (kv_gain[None, :], m_kv_gain[None, :], v_kv_gain[None, :]),
        "alog": (lanes8(gdn_a_log), lanes8(m_gdn_a_log), lanes8(v_gdn_a_log)),
        "dtb": (lanes8(gdn_dt_bias), lanes8(m_gdn_dt_bias), lanes8(v_gdn_dt_bias)),
        "out_gain": (gdn_out_gain, m_gdn_out_gain, v_gdn_out_gain),
    }
    sm = _small_update(small_landed, small_params, "small_update")
    conv_cols = QKV // N_DEV
    g_conv = lax.dynamic_slice(sm["conv"], (0, me * conv_cols), (8, conv_cols))[:CONV_WIDTH]
    conv_res = (g_conv,) + tuple(_adamw(g_conv, gdn_conv_w[0], m_gdn_conv_w[0], v_gdn_conv_w[0], "adamw_conv"))

    lead = lambda t: tuple(a[None] for a in t)
    heads8 = lambda t: tuple(a[:, N_HEADS:2 * N_HEADS] for a in t)
    per_weight = [
        sm["mix_pre"], sm["mix_post"], sm["mlp_pre"], sm["mlp_post"],
        tuple(results["mlp_w_up"]), tuple(results["mlp_w_down"]),
        lead(results["gdn_w_in"]), lead(conv_res), heads8(sm["alog"]), heads8(sm["dtb"]), sm["out_gain"],
        lead(results["gdn_w_out"]), tuple(a[0] for a in sm["kv"]), results["w_kv"], lead(results["sb_w_q"]), lead(results["sb_w_o"]),
    ]
    grads, deltas, new_ms, new_vs = zip(*per_weight)
    return (sm["loss"][0, 0], grad_x[None], *grads, *deltas, *new_ms, *new_vs)
```

```python
import functools
import math

import jax
import jax.numpy as jnp
from jax import lax
from jax.experimental import pallas as pl
from jax.experimental.pallas import tpu as pltpu

F32 = jnp.float32
BF16 = jnp.bfloat16

N_DEV = 8
D_MODEL = 1024
N_HEADS = 8
HEAD_DIM = 128
CHUNK = 64
CONV_WIDTH = 4
GDN_IN_COLS = 4 * D_MODEL + 2 * N_HEADS
GDN_IN_PAD = 4 * D_MODEL + 128
EPS = 1e-6

ADAM_LR = 0.001
ADAM_B1 = 0.9
ADAM_B2 = 0.999
ADAM_EPS = 1e-08
ADAM_WD = 0.01
ADAM_STEP = 10

VMEM_LIMIT_BYTES = 56 * 1024 * 1024
MESH = pl.DeviceIdType.MESH


def _params(*semantics):
    return pltpu.CompilerParams(dimension_semantics=semantics, vmem_limit_bytes=VMEM_LIMIT_BYTES)


def _dims(ta, tb):
    return (((0,) if ta else (1,), (1,) if tb else (0,)), ((), ()))


def _dot(a, b, ta=False, tb=False):
    return lax.dot_general(a.astype(BF16), b.astype(BF16), _dims(ta, tb), preferred_element_type=F32)


def _dot_f32(a, b, ta=False, tb=False):
    return lax.dot_general(a, b, _dims(ta, tb), precision=lax.Precision.HIGHEST, preferred_element_type=F32)


def _dot_3x(a, b, ta=False, tb=False):
    return lax.dot_general(a, b, _dims(ta, tb), precision=lax.Precision.HIGH, preferred_element_type=F32)


def _make_mm(dot):
    @functools.partial(jax.custom_vjp, nondiff_argnums=(2, 3))
    def mm(a, b, ta, tb):
        return dot(a, b, ta, tb)

    def fwd(a, b, ta, tb):
        return dot(a, b, ta, tb), (a, b)

    def bwd(ta, tb, res, g):
        a, b = res
        if not ta and not tb:
            return mm(g, b, False, True), mm(a, g, True, False)
        if not ta and tb:
            return mm(g, b, False, False), mm(g, a, True, False)
        if ta and not tb:
            return mm(b, g, False, True), mm(a, g, False, False)
        raise NotImplementedError

    mm.defvjp(fwd, bwd)
    return mm


_mm = _make_mm(_dot)
_mm_f32 = _make_mm(_dot_f32)


def _rms(x, gain):
    r = lax.rsqrt(jnp.mean(x * x, axis=-1, keepdims=True) + EPS)
    return x * r * gain


def _rms_bwd(x, gain, dy):
    r = lax.rsqrt(jnp.mean(x * x, axis=-1, keepdims=True) + EPS)
    xh = x * r
    dgain = jnp.sum(dy * xh, axis=0, keepdims=True)
    dxh = dy * gain
    dx = r * (dxh - xh * jnp.mean(dxh * xh, axis=-1, keepdims=True))
    return dx, dgain


def _silu(x):
    return x / (1.0 + jnp.exp(-x))


def _acc_rows(ref, first, rows):
    @pl.when(first)
    def _():
        ref[...] = jnp.zeros_like(ref)

    for r, val in enumerate(rows):
        ref[r:r + 1, :] += val


def _norm_matmul(x, gain, w, out_dtype, name, tm=1024, tn=1024):
    n, d = x.shape
    g, _, wc = w.shape
    tm, tn = min(tm, n), min(tn, wc)
    per = wc // tn
    assert n % tm == 0 and wc % tn == 0

    def body(x_ref, gain_ref, w_ref, out_ref, h_ref):
        @pl.when(pl.program_id(1) == 0)
        def _():
            h_ref[...] = _rms(x_ref[...], gain_ref[...]).astype(BF16)

        out_ref[...] = jnp.dot(h_ref[...], w_ref[...], preferred_element_type=F32).astype(out_dtype)

    return pl.pallas_call(
        body, name=name, grid=(n // tm, g * per),
        in_specs=[pl.BlockSpec((tm, d), lambda i, j: (i, 0)),
                  pl.BlockSpec((1, d), lambda i, j: (0, 0)),
                  pl.BlockSpec((None, d, tn), lambda i, j: (j // per, 0, j % per))],
        out_specs=[pl.BlockSpec((tm, tn), lambda i, j: (i, j)),
                   pl.BlockSpec((tm, d), lambda i, j: (i, 0))],
        out_shape=[jax.ShapeDtypeStruct((n, g * wc), out_dtype), jax.ShapeDtypeStruct((n, d), BF16)],
        compiler_params=_params("parallel", "arbitrary"),
    )(x, gain, w)


def _norm_matmul_bwd(branches, x, add, name, tm=1024, ride=None):
    n, d = x.shape
    tm = min(tm, n)
    tns = [min(tn, w.shape[2]) for _, w, _, tn in branches]
    pers = [w.shape[2] // tn for (_, w, _, _), tn in zip(branches, tns)]
    ncols = [w.shape[0] * per for (_, w, _, _), per in zip(branches, pers)]
    offs = [sum(ncols[:b]) for b in range(len(branches))]
    total = sum(ncols)
    nb = len(branches)

    def body(*refs):
        dout_refs, w_refs, gain_refs = refs[:nb], refs[nb:2 * nb], refs[2 * nb:3 * nb]
        x_ref, add_ref, dx_ref, dgain_ref, acc_ref, sum_ref = refs[3 * nb:]
        i, j = pl.program_id(0), pl.program_id(1)

        @pl.when((i == 0) & (j == 0))
        def _():
            dgain_ref[...] = jnp.zeros_like(dgain_ref)

        for b in range(nb):
            first, last = offs[b], offs[b] + ncols[b] - 1

            @pl.when((j >= first) & (j <= last))
            def _(b=b, first=first):
                part = _dot(dout_refs[b][...], w_refs[b][...], tb=True)

                @pl.when(j == first)
                def _():
                    acc_ref[...] = part

                @pl.when(j > first)
                def _():
                    acc_ref[...] += part

            @pl.when(j == last)
            def _(b=b):
                xv = x_ref[...]
                xh = xv * lax.rsqrt(jnp.mean(xv * xv, axis=-1, keepdims=True) + EPS)
                dh = acc_ref[...]
                dgain_ref[b:b + 1, :] += jnp.sum(dh * xh, axis=0, keepdims=True)
                scaled = dh * gain_refs[b][...]
                if b == 0:
                    sum_ref[...] = scaled
                else:
                    sum_ref[...] += scaled

        @pl.when(j == total - 1)
        def _():
            xv = x_ref[...]
            r = lax.rsqrt(jnp.mean(xv * xv, axis=-1, keepdims=True) + EPS)
            xh = xv * r
            dxh = sum_ref[...]
            dx_ref[...] = add_ref[...] + r * (dxh - xh * jnp.mean(dxh * xh, axis=-1, keepdims=True))

    def dout_spec(b):
        return pl.BlockSpec((tm, tns[b]), lambda i, j: (i, jnp.clip(j - offs[b], 0, ncols[b] - 1)))

    def w_spec(b):
        def index(i, j):
            c = jnp.clip(j - offs[b], 0, ncols[b] - 1)
            return (c // pers[b], 0, c % pers[b])
        return pl.BlockSpec((None, d, tns[b]), index)

    row = lambda i, j: (i, 0)
    const = lambda i, j: (0, 0)
    nrow = n // tm
    (dx, dgain), got = _call(
        body, [b[0] for b in branches] + [b[1] for b in branches] + [b[2] for b in branches] + [x, add],
        name=name, grid=(nrow, total),
        in_specs=[dout_spec(b) for b in range(nb)] + [w_spec(b) for b in range(nb)]
                 + [pl.BlockSpec((1, d), const)] * nb + [pl.BlockSpec((tm, d), row), pl.BlockSpec((tm, d), row)],
        out_specs=[pl.BlockSpec((tm, d), row), pl.BlockSpec((8, d), const)],
        out_shape=[jax.ShapeDtypeStruct((n, d), F32), jax.ShapeDtypeStruct((8, d), F32)],
        scratch_shapes=[pltpu.VMEM((tm, d), F32), pltpu.VMEM((tm, d), F32)],
        compiler_params=_params("arbitrary", "arbitrary"),
        ride=ride, first=lambda: (pl.program_id(0) == 0) & (pl.program_id(1) == 0),
        last=lambda: (pl.program_id(0) == nrow - 1) & (pl.program_id(1) == total - 1))
    return (dx, dgain) if ride is None else (dx, dgain, got)


def _matmul_tn(a, b, groups, name, ta=1024, tb=1024, tk=1024):
    n, ka = a.shape
    _, kb = b.shape
    wc = kb // groups
    ta, tb, tk = min(ta, ka), min(tb, wc), min(tk, n)
    per = wc // tb
    nk = n // tk
    assert ka % ta == 0 and wc % tb == 0 and n % tk == 0

    def body(a_ref, b_ref, out_ref, acc_ref):
        k = pl.program_id(2)

        @pl.when(k == 0)
        def _():
            acc_ref[...] = jnp.zeros_like(acc_ref)

        acc_ref[...] += _dot(a_ref[...], b_ref[...], ta=True)

        @pl.when(k == nk - 1)
        def _():
            out_ref[...] = acc_ref[...].astype(BF16)

    return pl.pallas_call(
        body, name=name, grid=(ka // ta, groups * per, nk),
        in_specs=[pl.BlockSpec((tk, ta), lambda i, j, k: (k, i)),
                  pl.BlockSpec((tk, tb), lambda i, j, k: (k, j))],
        out_specs=pl.BlockSpec((None, ta, tb), lambda i, j, k: (j // per, i, j % per)),
        out_shape=jax.ShapeDtypeStruct((groups, ka, wc), BF16),
        scratch_shapes=[pltpu.VMEM((ta, tb), F32)],
        compiler_params=_params("parallel", "parallel", "arbitrary"),
    )(a, b)


def _gated_head_norm(o, gate, out_gain):
    parts = []
    for h in range(N_HEADS):
        sl = slice(h * HEAD_DIM, (h + 1) * HEAD_DIM)
        parts.append(_rms(o[:, sl], out_gain) * _silu(gate[:, sl]))
    return parts


def _out_proj(a_inputs, gated, x_in, w, gain, out_gain, name, tm=1024):
    n, d = x_in.shape
    k = w.shape[0]
    tm = min(tm, n)

    def body(*refs):
        if gated:
            o_ref, gate_ref, og_ref, x_ref, w_ref, gain_ref, xo_ref, y_ref, a_ref = refs
            parts = _gated_head_norm(o_ref[...], gate_ref[...], og_ref[...])
            for h, part in enumerate(parts):
                a_ref[:, h * HEAD_DIM:(h + 1) * HEAD_DIM] = part.astype(BF16)
            a = a_ref[...]
        else:
            a_in_ref, x_ref, w_ref, gain_ref, xo_ref, y_ref = refs
            a = a_in_ref[...]
        y = jnp.dot(a, w_ref[...], preferred_element_type=F32)
        y_ref[...] = y
        xo_ref[...] = x_ref[...] + _rms(y, gain_ref[...])

    row = lambda i: (i, 0)
    const = lambda i: (0, 0)
    if gated:
        a_specs = [pl.BlockSpec((tm, k), row), pl.BlockSpec((tm, D_MODEL), lambda i: (i, 3)),
                   pl.BlockSpec((1, HEAD_DIM), const)]
        a_args = list(a_inputs) + [out_gain]
    else:
        a_specs = [pl.BlockSpec((tm, k), row)]
        a_args = list(a_inputs)
    out_specs = [pl.BlockSpec((tm, d), row), pl.BlockSpec((tm, d), row)]
    out_shape = [jax.ShapeDtypeStruct((n, d), F32), jax.ShapeDtypeStruct((n, d), F32)]
    if gated:
        out_specs.append(pl.BlockSpec((tm, k), row))
        out_shape.append(jax.ShapeDtypeStruct((n, k), BF16))
    return pl.pallas_call(
        body, name=name, grid=(n // tm,),
        in_specs=a_specs + [pl.BlockSpec((tm, d), row), pl.BlockSpec((k, d), const), pl.BlockSpec((1, d), const)],
        out_specs=out_specs, out_shape=out_shape,
        compiler_params=_params("parallel"),
    )(*a_args, x_in, w, gain)


def _out_proj_bwd(dxo, y, w, gain, gated_inputs, name, tm=1024):
    n, d = dxo.shape
    k = w.shape[0]
    tm = min(tm, n)
    gated = gated_inputs is not None

    def body(*refs):
        if gated:
            (dxo_ref, y_ref, w_ref, gain_ref, o_ref, gate_ref, og_ref,
             dy_ref, dgain_ref, do_ref, dgate_ref, dog_ref) = refs
        else:
            dxo_ref, y_ref, w_ref, gain_ref, dy_ref, dgain_ref, da_ref = refs
        first = pl.program_id(0) == 0
        dy, dgain = _rms_bwd(y_ref[...], gain_ref[...], dxo_ref[...])
        dy_ref[...] = dy.astype(BF16)
        _acc_rows(dgain_ref, first, [dgain])
        da = _dot(dy_ref[...], w_ref[...], tb=True)
        if not gated:
            da_ref[...] = da.astype(BF16)
            return
        og = og_ref[...]
        dog = jnp.zeros_like(og)
        for h in range(N_HEADS):
            sl = slice(h * HEAD_DIM, (h + 1) * HEAD_DIM)
            fn = lambda o_h, g_h, gn: _rms(o_h, gn) * _silu(g_h)
            _, vjp = jax.vjp(fn, o_ref[:, sl], gate_ref[:, sl], og)
            do_h, dgate_h, dog_h = vjp(da[:, sl])
            do_ref[:, sl] = do_h
            dgate_ref[:, sl] = dgate_h.astype(BF16)
            dog = dog + dog_h
        _acc_rows(dog_ref, first, [dog])

    row = lambda i: (i, 0)
    const = lambda i: (0, 0)
    in_specs = [pl.BlockSpec((tm, d), row), pl.BlockSpec((tm, d), row), pl.BlockSpec((k, d), const),
                pl.BlockSpec((1, d), const)]
    args = [dxo, y, w, gain]
    out_specs = [pl.BlockSpec((tm, d), row), pl.BlockSpec((8, d), const)]
    out_shape = [jax.ShapeDtypeStruct((n, d), BF16), jax.ShapeDtypeStruct((8, d), F32)]
    if gated:
        in_specs += [pl.BlockSpec((tm, k), row), pl.BlockSpec((tm, D_MODEL), lambda i: (i, 3)),
                     pl.BlockSpec((1, HEAD_DIM), const)]
        args += list(gated_inputs)
        out_specs += [pl.BlockSpec((tm, k), row), pl.BlockSpec((tm, k), row), pl.BlockSpec((8, HEAD_DIM), const)]
        out_shape += [jax.ShapeDtypeStruct((n, k), F32), jax.ShapeDtypeStruct((n, k), BF16),
                      jax.ShapeDtypeStruct((8, HEAD_DIM), F32)]
    else:
        out_specs.append(pl.BlockSpec((tm, k), row))
        out_shape.append(jax.ShapeDtypeStruct((n, k), BF16))
    return pl.pallas_call(
        body, name=name, grid=(n // tm,), in_specs=in_specs, out_specs=out_specs, out_shape=out_shape,
        compiler_params=_params("arbitrary"),
    )(*args)


def _mlp_fwd(x_in, g_pre, w_up, w_down, g_post, name, tm=1024, target=None, ride=None):
    n, d = x_in.shape
    g, _, wc = w_up.shape
    tm = min(tm, n)
    with_loss = target is not None

    def body(*refs):
        if with_loss:
            x_ref, gpre_ref, wup_ref, wdown_ref, gpost_ref, t_ref, xo_ref, y_ref, h_ref, u_ref, a_ref, loss_ref, acc_ref = refs
        else:
            x_ref, gpre_ref, wup_ref, wdown_ref, gpost_ref, xo_ref, y_ref, h_ref, u_ref, a_ref, acc_ref = refs
        i, j = pl.program_id(0), pl.program_id(1)

        @pl.when(j == 0)
        def _():
            h_ref[...] = _rms(x_ref[...], gpre_ref[...]).astype(BF16)
            acc_ref[...] = jnp.zeros_like(acc_ref)

        u = jnp.dot(h_ref[...], wup_ref[...], preferred_element_type=F32).astype(BF16)
        u_ref[...] = u
        a = jnp.square(jnp.maximum(u, 0))
        a_ref[...] = a
        acc_ref[...] += jnp.dot(a, wdown_ref[...], preferred_element_type=F32)

        @pl.when(j == g - 1)
        def _():
            y = acc_ref[...]
            y_ref[...] = y
            xo = x_ref[...] + _rms(y, gpost_ref[...])
            if with_loss:
                err = xo - t_ref[...]
                xo_ref[...] = err * (1.0 / d)
                part = 0.5 * jnp.sum(jnp.mean(err * err, axis=-1, keepdims=True), axis=0, keepdims=True)
                _acc_rows(loss_ref, i == 0, [jnp.broadcast_to(part, (1, 128))])
            else:
                xo_ref[...] = xo

    row = lambda i, j: (i, 0)
    const = lambda i, j: (0, 0)
    in_specs = [pl.BlockSpec((tm, d), row), pl.BlockSpec((1, d), const),
                pl.BlockSpec((None, d, wc), lambda i, j: (j, 0, 0)),
                pl.BlockSpec((wc, d), lambda i, j: (j, 0)), pl.BlockSpec((1, d), const)]
    out_specs = [pl.BlockSpec((tm, d), row), pl.BlockSpec((tm, d), row), pl.BlockSpec((tm, d), row),
                 pl.BlockSpec((tm, wc), lambda i, j: (i, j)), pl.BlockSpec((tm, wc), lambda i, j: (i, j))]
    out_shape = [jax.ShapeDtypeStruct((n, d), F32), jax.ShapeDtypeStruct((n, d), F32),
                 jax.ShapeDtypeStruct((n, d), BF16), jax.ShapeDtypeStruct((n, g * wc), BF16),
                 jax.ShapeDtypeStruct((n, g * wc), BF16)]
    args = [x_in, g_pre, w_up, w_down, g_post]
    if with_loss:
        in_specs.append(pl.BlockSpec((tm, d), row))
        out_specs.append(pl.BlockSpec((8, 128), const))
        out_shape.append(jax.ShapeDtypeStruct((8, 128), F32))
        args.append(target)
    nrow = n // tm
    outs, got = _call(
        body, args, name=name, grid=(nrow, g), in_specs=in_specs, out_specs=out_specs, out_shape=out_shape,
        scratch_shapes=[pltpu.VMEM((tm, d), F32)], compiler_params=_params("arbitrary", "arbitrary"),
        ride=ride, first=lambda: (pl.program_id(0) == 0) & (pl.program_id(1) == 0),
        last=lambda: (pl.program_id(0) == nrow - 1) & (pl.program_id(1) == g - 1))
    return outs if ride is None else (outs, got)


def _mlp_bwd(dxo, y, g_post, w_down, u, w_up, x_in, g_pre, name, tm=1024, ride=None):
    n, d = dxo.shape
    g, _, wc = w_up.shape
    tm = min(tm, n)

    def body(dxo_ref, y_ref, gpost_ref, wdown_ref, u_ref, wup_ref, x_ref, gpre_ref,
             dx_ref, dy_ref, du_ref, dgain_ref, acc_ref, dgpost_ref):
        i, j = pl.program_id(0), pl.program_id(1)

        @pl.when(j == 0)
        def _():
            dy, dgpost = _rms_bwd(y_ref[...], gpost_ref[...], dxo_ref[...])
            dy_ref[...] = dy.astype(BF16)
            dgpost_ref[...] = dgpost
            acc_ref[...] = jnp.zeros_like(acc_ref)

        da = _dot(dy_ref[...], wdown_ref[...], tb=True)
        du = (da * (2.0 * jnp.maximum(u_ref[...], 0).astype(F32))).astype(BF16)
        du_ref[...] = du
        acc_ref[...] += _dot(du, wup_ref[...], tb=True)

        @pl.when(j == g - 1)
        def _():
            dx, dgpre = _rms_bwd(x_ref[...], gpre_ref[...], acc_ref[...])
            dx_ref[...] = dxo_ref[...] + dx
            _acc_rows(dgain_ref, i == 0, [dgpre, dgpost_ref[...]])

    row = lambda i, j: (i, 0)
    const = lambda i, j: (0, 0)
    nrow = n // tm
    outs, got = _call(
        body, [dxo, y, g_post, w_down, u, w_up, x_in, g_pre], name=name, grid=(nrow, g),
        in_specs=[pl.BlockSpec((tm, d), row), pl.BlockSpec((tm, d), row), pl.BlockSpec((1, d), const),
                  pl.BlockSpec((wc, d), lambda i, j: (j, 0)), pl.BlockSpec((tm, wc), lambda i, j: (i, j)),
                  pl.BlockSpec((None, d, wc), lambda i, j: (j, 0, 0)), pl.BlockSpec((tm, d), row),
                  pl.BlockSpec((1, d), const)],
        out_specs=[pl.BlockSpec((tm, d), row), pl.BlockSpec((tm, d), row),
                   pl.BlockSpec((tm, wc), lambda i, j: (i, j)), pl.BlockSpec((8, d), const)],
        out_shape=[jax.ShapeDtypeStruct((n, d), F32), jax.ShapeDtypeStruct((n, d), BF16),
                   jax.ShapeDtypeStruct((n, g * wc), BF16), jax.ShapeDtypeStruct((8, d), F32)],
        scratch_shapes=[pltpu.VMEM((tm, d), F32), pltpu.VMEM((1, d), F32)],
        compiler_params=_params("arbitrary", "arbitrary"),
        ride=ride, first=lambda: (pl.program_id(0) == 0) & (pl.program_id(1) == 0),
        last=lambda: (pl.program_id(0) == nrow - 1) & (pl.program_id(1) == g - 1))
    return outs if ride is None else (outs, got)


QKV = 3 * D_MODEL


def _shifted(x, prev8, s):
    if s == 0:
        return x
    tm = x.shape[0]
    rolled = pltpu.roll(x, s, 0)
    head = pltpu.roll(prev8, s, 0)
    head = jnp.concatenate([head, jnp.zeros((tm - 8, x.shape[1]), x.dtype)], axis=0)
    rows = lax.broadcasted_iota(jnp.int32, x.shape, 0)
    return jnp.where(rows < s, head, rolled)


def _conv(x, prev8, conv_w):
    out = x * conv_w[3:4, :]
    for s in range(1, CONV_WIDTH):
        out = out + _shifted(x, prev8, s) * conv_w[3 - s:4 - s, :]
    return out


def _l2norm(x):
    return x * lax.rsqrt(jnp.sum(x * x, axis=-1, keepdims=True) + EPS)


def _gdn_act_head(cq, ck, cv):
    return _l2norm(_silu(cq)) * (HEAD_DIM ** -0.5), _l2norm(_silu(ck)), _silu(cv)


def _gdn_act_head_bwd(cq, ck, cv, dq, dk, dv):
    def silu_and_slope(c):
        s = 1.0 / (1.0 + jnp.exp(-c))
        return c * s, s * (1.0 + c * (1.0 - s))

    def through_norm(c, dy, scale):
        a, slope = silu_and_slope(c)
        n = lax.rsqrt(jnp.sum(a * a, axis=-1, keepdims=True) + EPS)
        t = dy * scale if scale != 1.0 else dy
        da = n * (t - a * (n * n * jnp.sum(t * a, axis=-1, keepdims=True)))
        return da * slope

    return through_norm(cq, dq, HEAD_DIM ** -0.5), through_norm(ck, dk, 1.0), dv * silu_and_slope(cv)[1]


def _gdn_gates(ba, alog, dtb):
    lane = lax.broadcasted_iota(jnp.int32, ba.shape, 1)
    beta = 1.0 / (1.0 + jnp.exp(-ba))
    t = ba + dtb
    softplus = jnp.maximum(t, 0.0) + jnp.log(1.0 + jnp.exp(-jnp.abs(t)))
    g = -jnp.exp(alog) * softplus
    return jnp.where(lane < N_HEADS, beta, jnp.where(lane < 2 * N_HEADS, g, 0.0))


def _qkv_cols(part, h):
    start = part * D_MODEL + h * HEAD_DIM
    return slice(start, start + HEAD_DIM)


def _prev_rows_spec(tm, cols):
    per = tm // 8
    return pl.BlockSpec((8, cols), lambda i: (jnp.maximum(i * per - 1, 0), 0))


def _gdn_pre(proj, conv_w, alog, dtb, name, tm=256):
    n = proj.shape[0]

    def body(x_ref, prev_ref, ba_ref, cw_ref, alog_ref, dtb_ref, q_ref, k_ref, v_ref, bg_ref, c_ref):
        first = pl.program_id(0) == 0
        for h in range(N_HEADS):
            convs = []
            for part in range(3):
                sl = _qkv_cols(part, h)
                prev8 = jnp.where(first, 0.0, prev_ref[:, sl])
                convs.append(_conv(x_ref[:, sl], prev8, cw_ref[:, sl]))
                c_ref[:, sl] = convs[-1]
            out = slice(h * HEAD_DIM, (h + 1) * HEAD_DIM)
            q_ref[:, out], k_ref[:, out], v_ref[:, out] = _gdn_act_head(*convs)
        bg_ref[...] = _gdn_gates(ba_ref[...], alog_ref[...], dtb_ref[...])

    row = lambda i: (i, 0)
    const = lambda i: (0, 0)
    return pl.pallas_call(
        body, name=name, grid=(n // tm,),
        in_specs=[pl.BlockSpec((tm, QKV), row), _prev_rows_spec(tm, QKV),
                  pl.BlockSpec((tm, 128), lambda i: (i, 4 * D_MODEL // 128)),
                  pl.BlockSpec((CONV_WIDTH, QKV), const), pl.BlockSpec((1, 128), const), pl.BlockSpec((1, 128), const)],
        out_specs=[pl.BlockSpec((tm, D_MODEL), row)] * 3 + [pl.BlockSpec((tm, 128), row), pl.BlockSpec((tm, QKV), row)],
        out_shape=[jax.ShapeDtypeStruct((n, D_MODEL), F32)] * 3 + [jax.ShapeDtypeStruct((n, 128), F32),
                                                                  jax.ShapeDtypeStruct((n, QKV), F32)],
        compiler_params=_params("parallel"),
    )(proj, proj, proj, conv_w, alog, dtb)


def _gdn_pre_bwd(proj, conv, conv_w, alog, dtb, dq, dk, dv, dbg, dgate, name, tm=256):
    n = proj.shape[0]
    nt = n // tm

    def body(x_ref, c_ref, ba_ref, cw_ref, alog_ref, dtb_ref, dq_ref, dk_ref, dv_ref, dbg_ref, dgate_ref,
             dproj_ref, dcw_ref, dgates_ref, carry_ref, head_ref):
        step = pl.program_id(0)

        @pl.when(step == 0)
        def _():
            carry_ref[...] = jnp.zeros_like(carry_ref)
            head_ref[...] = jnp.zeros_like(head_ref)
            dcw_ref[...] = jnp.zeros_like(dcw_ref)

        rows = lax.broadcasted_iota(jnp.int32, (tm, HEAD_DIM), 0)
        rows8 = lax.broadcasted_iota(jnp.int32, (8, HEAD_DIM), 0)
        for h in range(N_HEADS):
            cols = [_qkv_cols(part, h) for part in range(3)]
            out = slice(h * HEAD_DIM, (h + 1) * HEAD_DIM)
            dcs = _gdn_act_head_bwd(*[c_ref[:, sl] for sl in cols], dq_ref[:, out], dk_ref[:, out], dv_ref[:, out])
            for sl, dc in zip(cols, dcs):
                x = x_ref[:, sl]
                cw = cw_ref[:, sl]
                dx = dc * cw[3:4, :]
                wrapped = jnp.zeros((8, HEAD_DIM), F32)
                dcw_ref[3:4, sl] += jnp.sum(dc * x, axis=0, keepdims=True)
                for s in range(1, CONV_WIDTH):
                    up = pltpu.roll(dc, tm - s, 0)
                    r = up * cw[3 - s:4 - s, :]
                    dx = dx + jnp.where(rows < tm - s, r, 0.0)
                    wrapped = wrapped + jnp.where(rows[tm - 8:, :] >= tm - s, r[tm - 8:, :], 0.0)
                    inside = jnp.sum(jnp.where(rows < tm - s, x * up, 0.0), axis=0, keepdims=True)
                    next_dc = pltpu.roll(head_ref[:, sl], 8 - s, 0)
                    across = jnp.sum(jnp.where(rows8 >= 8 - s, x[tm - 8:, :] * next_dc, 0.0), axis=0, keepdims=True)
                    dcw_ref[3 - s:4 - s, sl] += inside + across
                from_next = jnp.concatenate([jnp.zeros((tm - 8, HEAD_DIM), F32), carry_ref[:, sl]], axis=0)
                dproj_ref[:, sl] = (dx + from_next).astype(BF16)
                carry_ref[:, sl] = wrapped
                head_ref[:, sl] = dc[:8, :]

        _, vjp = jax.vjp(_gdn_gates, ba_ref[...], alog_ref[...], dtb_ref[...])
        dba, dalog, ddtb = vjp(dbg_ref[...])
        dproj_ref[:, QKV:4 * D_MODEL] = dgate_ref[...]
        dproj_ref[:, 4 * D_MODEL:] = dba.astype(BF16)
        _acc_rows(dgates_ref, step == 0, [dalog, ddtb])

    rev = lambda i: nt - 1 - i
    row = lambda i: (rev(i), 0)
    const = lambda i: (0, 0)
    return pl.pallas_call(
        body, name=name, grid=(nt,),
        in_specs=[pl.BlockSpec((tm, QKV), row), pl.BlockSpec((tm, QKV), row),
                  pl.BlockSpec((tm, 128), lambda i: (rev(i), 4 * D_MODEL // 128)),
                  pl.BlockSpec((CONV_WIDTH, QKV), const), pl.BlockSpec((1, 128), const), pl.BlockSpec((1, 128), const),
                  pl.BlockSpec((tm, D_MODEL), row), pl.BlockSpec((tm, D_MODEL), row), pl.BlockSpec((tm, D_MODEL), row),
                  pl.BlockSpec((tm, 128), row), pl.BlockSpec((tm, D_MODEL), row)],
        out_specs=[pl.BlockSpec((tm, GDN_IN_PAD), row), pl.BlockSpec((8, QKV), const), pl.BlockSpec((8, 128), const)],
        out_shape=[jax.ShapeDtypeStruct((n, GDN_IN_PAD), BF16), jax.ShapeDtypeStruct((8, QKV), F32),
                   jax.ShapeDtypeStruct((8, 128), F32)],
        scratch_shapes=[pltpu.VMEM((8, QKV), F32), pltpu.VMEM((8, QKV), F32)],
        compiler_params=_params("arbitrary"),
    )(proj, conv, proj, conv_w, alog, dtb, dq, dk, dv, dbg, dgate)


@jax.custom_vjp
def _unit_lower_inverses(lowers):
    n = lowers[0].shape[0]
    eye = (lax.broadcasted_iota(jnp.int32, (n, n), 0) == lax.broadcasted_iota(jnp.int32, (n, n), 1)).astype(F32)
    xs = [-low for low in lowers]
    ps = [eye + x for x in xs]
    ys = [_dot_3x(x, x) for x in xs]
    for _ in range(int(math.log2(CHUNK)) - 2):
        both = [_dot_3x(y, jnp.concatenate([y, p], axis=1)) for y, p in zip(ys, ps)]
        ys = [b[:, :n] for b in both]
        ps = [p + b[:, n:] for p, b in zip(ps, both)]
    return tuple(p + _dot_3x(y, p) for y, p in zip(ys, ps))


def _unit_lower_inverses_fwd(lowers):
    ts = _unit_lower_inverses(lowers)
    return ts, ts


def _unit_lower_inverses_bwd(ts, dts):
    left = [_dot_3x(t, dt, ta=True) for t, dt in zip(ts, dts)]
    return (tuple(-_dot_3x(l, t, tb=True) for l, t in zip(left, ts)),)


_unit_lower_inverses.defvjp(_unit_lower_inverses_fwd, _unit_lower_inverses_bwd)


@jax.custom_vjp
def _known_inverses(lowers, inverses):
    return inverses


def _known_inverses_bwd(ts, dts):
    (d_lowers,) = _unit_lower_inverses_bwd(ts, dts)
    return d_lowers, tuple(jnp.zeros_like(t) for t in ts)


_known_inverses.defvjp(lambda lowers, inverses: (inverses, inverses), _known_inverses_bwd)


@functools.partial(jax.custom_vjp, nondiff_argnums=(1,))
def _halves(x, axis):
    half = x.shape[axis] // 2
    return (x[:half], x[half:]) if axis == 0 else (x[:, :half], x[:, half:])


_halves.defvjp(lambda x, axis: (_halves(x, axis), None), lambda axis, _, g: (jnp.concatenate(g, axis=axis),))

GDN_STEP_CHUNKS = 4


def _gdn_chunks(qs, ks, vs, bgs, states, inverses=None, keep_inverses=False):
    c = CHUNK
    heads = range(N_HEADS)
    items = [(j, h) for j in range(len(bgs)) for h in heads]
    row = lax.broadcasted_iota(jnp.int32, (c, c), 0)
    col = lax.broadcasted_iota(jnp.int32, (c, c), 1)
    incl, strict, eye = row >= col, row > col, row == col
    lane = lax.broadcasted_iota(jnp.int32, (c, 128), 1)
    rowc = lax.broadcasted_iota(jnp.int32, (c, 1), 0)
    gc_all = [_mm_f32(incl.astype(F32), bg, False, False) for bg in bgs]
    q, k, v = ([xs[j][h] for j, h in items] for xs in (qs, ks, vs))
    n = range(len(items))
    beta = [jnp.sum(jnp.where(lane == h, bgs[j], 0.0), axis=1, keepdims=True) for j, h in items]
    gc = [jnp.sum(jnp.where(lane == N_HEADS + h, gc_all[j], 0.0), axis=1, keepdims=True) for j, h in items]
    gc_row = [jnp.sum(jnp.where(eye, gc[i], 0.0), axis=0, keepdims=True) for i in n]
    gc_last = [jnp.sum(jnp.where(rowc == c - 1, gc[i], 0.0), axis=0, keepdims=True) for i in n]
    decay = [jnp.where(incl, jnp.exp(jnp.where(incl, gc[i] - gc_row[i], 0.0)), 0.0) for i in n]
    kb = [k[i] * beta[i] for i in n]
    kb_q = [_halves(_mm(jnp.concatenate([kb[i], q[i]], axis=0), k[i], False, True), 0) for i in n]
    lower = tuple(jnp.where(strict, kb_q[i][0] * decay[i], 0.0) for i in n)
    attn = [kb_q[i][1] * decay[i] for i in n]
    t_mat = _unit_lower_inverses(lower) if inverses is None else _known_inverses(lower, inverses)
    egc = [jnp.exp(gc[i]) for i in n]
    w_u = [_halves(_mm(t_mat[i], jnp.concatenate([kb[i] * egc[i], v[i] * beta[i]], axis=1), False, False), 1)
           for i in n]
    w, u = [x[0] for x in w_u], [x[1] for x in w_u]
    qg = [q[i] * egc[i] for i in n]
    kg = [k[i] * jnp.exp(gc_last[i] - gc[i]) for i in n]
    outs, cur = [], list(states)
    for j in range(len(bgs)):
        at = lambda h: j * N_HEADS + h
        w_qg = [_halves(_mm(jnp.concatenate([w[at(h)], qg[at(h)]], axis=0), cur[h], False, False), 0)
                for h in heads]
        v_new = [u[at(h)] - w_qg[h][0] for h in heads]
        outs.append(tuple(w_qg[h][1] + _mm(attn[at(h)], v_new[h], False, False) for h in heads))
        cur = [cur[h] * jnp.exp(gc_last[at(h)]) + _mm(kg[at(h)], v_new[h], True, False) for h in heads]
    return (tuple(outs), tuple(cur), t_mat) if keep_inverses else (tuple(outs), tuple(cur))


def _chunk_head_slices(ref, nch):
    return tuple(tuple(ref[j * CHUNK:(j + 1) * CHUNK, h * HEAD_DIM:(h + 1) * HEAD_DIM] for h in range(N_HEADS))
                 for j in range(nch))


def _store_chunk_heads(ref, values):
    for j, chunk in enumerate(values):
        for h, val in enumerate(chunk):
            ref[j * CHUNK:(j + 1) * CHUNK, h * HEAD_DIM:(h + 1) * HEAD_DIM] = val


def _gdn_scan(qn, kn, v, bg, name, ride=None):
    n = qn.shape[0]
    nch = GDN_STEP_CHUNKS
    rows = nch * CHUNK
    nc = n // rows

    def body(q_ref, k_ref, v_ref, bg_ref, o_ref, saved_ref, inv_ref, state_ref):
        @pl.when(pl.program_id(0) == 0)
        def _():
            state_ref[...] = jnp.zeros_like(state_ref)

        states = tuple(state_ref[h] for h in range(N_HEADS))
        for h in range(N_HEADS):
            saved_ref[h] = states[h]
        bgs = tuple(bg_ref[j * CHUNK:(j + 1) * CHUNK, :] for j in range(nch))
        outs, new_states, inverses = _gdn_chunks(_chunk_head_slices(q_ref, nch), _chunk_head_slices(k_ref, nch),
                                                 _chunk_head_slices(v_ref, nch), bgs, states, keep_inverses=True)
        _store_chunk_heads(o_ref, outs)
        for h in range(N_HEADS):
            state_ref[h] = new_states[h]
        for item, inverse in enumerate(inverses):
            inv_ref[item] = inverse

    row = lambda i: (i, 0)
    return _call(
        body, [qn, kn, v, bg], name=name, grid=(nc,),
        in_specs=[pl.BlockSpec((rows, D_MODEL), row)] * 3 + [pl.BlockSpec((rows, 128), row)],
        out_specs=[pl.BlockSpec((rows, D_MODEL), row),
                   pl.BlockSpec((None, N_HEADS, HEAD_DIM, HEAD_DIM), lambda i: (i, 0, 0, 0)),
                   pl.BlockSpec((None, nch * N_HEADS, CHUNK, CHUNK), lambda i: (i, 0, 0, 0))],
        out_shape=[jax.ShapeDtypeStruct((n, D_MODEL), F32),
                   jax.ShapeDtypeStruct((nc, N_HEADS, HEAD_DIM, HEAD_DIM), F32),
                   jax.ShapeDtypeStruct((nc, nch * N_HEADS, CHUNK, CHUNK), F32)],
        scratch_shapes=[pltpu.VMEM((N_HEADS, HEAD_DIM, HEAD_DIM), F32)],
        compiler_params=_params("arbitrary"),
        ride=ride, first=lambda: pl.program_id(0) == 0, last=lambda: pl.program_id(0) == nc - 1)


def _gdn_scan_bwd(qn, kn, v, bg, saved, inverses, do, name, ride=None):
    n = qn.shape[0]
    nch = GDN_STEP_CHUNKS
    rows = nch * CHUNK
    nc = n // rows

    def body(q_ref, k_ref, v_ref, bg_ref, saved_ref, inv_ref, do_ref, dq_ref, dk_ref, dv_ref, dbg_ref, dstate_ref):
        @pl.when(pl.program_id(0) == 0)
        def _():
            dstate_ref[...] = jnp.zeros_like(dstate_ref)

        states = tuple(saved_ref[h] for h in range(N_HEADS))
        bgs = tuple(bg_ref[j * CHUNK:(j + 1) * CHUNK, :] for j in range(nch))
        known = tuple(inv_ref[item] for item in range(nch * N_HEADS))
        _, vjp = jax.vjp(functools.partial(_gdn_chunks, inverses=known), _chunk_head_slices(q_ref, nch),
                         _chunk_head_slices(k_ref, nch), _chunk_head_slices(v_ref, nch), bgs, states)
        dstates = tuple(dstate_ref[h] for h in range(N_HEADS))
        dqs, dks, dvs, dbgs, dprev = vjp((_chunk_head_slices(do_ref, nch), dstates))
        _store_chunk_heads(dq_ref, dqs)
        _store_chunk_heads(dk_ref, dks)
        _store_chunk_heads(dv_ref, dvs)
        for h in range(N_HEADS):
            dstate_ref[h] = dprev[h]
        for j in range(nch):
            dbg_ref[j * CHUNK:(j + 1) * CHUNK, :] = dbgs[j]

    row = lambda i: (nc - 1 - i, 0)
    return _call(
        body, [qn, kn, v, bg, saved, inverses, do], name=name, grid=(nc,),
        in_specs=[pl.BlockSpec((rows, D_MODEL), row)] * 3 + [pl.BlockSpec((rows, 128), row),
                  pl.BlockSpec((None, N_HEADS, HEAD_DIM, HEAD_DIM), lambda i: (nc - 1 - i, 0, 0, 0)),
                  pl.BlockSpec((None, nch * N_HEADS, CHUNK, CHUNK), lambda i: (nc - 1 - i, 0, 0, 0)),
                  pl.BlockSpec((rows, D_MODEL), row)],
        out_specs=[pl.BlockSpec((rows, D_MODEL), row)] * 3 + [pl.BlockSpec((rows, 128), row)],
        out_shape=[jax.ShapeDtypeStruct((n, D_MODEL), F32)] * 3 + [jax.ShapeDtypeStruct((n, 128), F32)],
        scratch_shapes=[pltpu.VMEM((N_HEADS, HEAD_DIM, HEAD_DIM), F32)],
        compiler_params=_params("arbitrary"),
        ride=ride, first=lambda: pl.program_id(0) == 0, last=lambda: pl.program_id(0) == nc - 1)


SB_BQ = 512
SB_BK = 256
SB_SUB = 128
SB_ROWS = 128
SB_SCALE = HEAD_DIM ** -0.5
SB_DEAD = -105.0


def _sb_terms(z, before):
    e = jnp.exp(-jnp.abs(z))
    log_beta = jnp.minimum(z, 0.0) - jnp.log(1.0 + e)
    log_1m = log_beta - z
    if before is not None:
        log_1m = jnp.where(before, log_1m, 0.0)
    return e, log_beta, log_1m


def _tri_ones(n, cmp):
    r = lax.broadcasted_iota(jnp.int32, (n, 2 * n), 0)
    c = lax.broadcasted_iota(jnp.int32, (n, 2 * n), 1)
    return jnp.where((c >= n) | cmp(r, c), 1.0, 0.0).astype(BF16)


def _sums(x, tri_ones):
    both = jnp.dot(x.astype(BF16), tri_ones, preferred_element_type=F32)
    n = x.shape[1]
    return both[:, :n], both[:, n:]


def _sb_chunk_mask(diagonal, r, s):
    if not diagonal or s * SB_SUB + SB_SUB - 1 < r * SB_ROWS:
        return None
    if s * SB_SUB >= r * SB_ROWS + SB_ROWS - 1:
        return "empty"
    rows = r * SB_ROWS + lax.broadcasted_iota(jnp.int32, (SB_ROWS, SB_SUB), 0)
    cols = s * SB_SUB + lax.broadcasted_iota(jnp.int32, (SB_ROWS, SB_SUB), 1)
    return cols < rows


def _sb_attention(q, kv, name, bq=SB_BQ, ride=None):
    n = q.shape[0]
    bq = min(bq, n)
    bk = min(SB_BK, bq)
    nrc = bq // SB_ROWS

    def body(q_ref, k_ref, v_ref, o_ref, l_ref, n_ref, z_scr, a_scr):
        i = pl.program_id(1)
        qb = q_ref[...]
        after = _tri_ones(SB_SUB, lambda r, c: r > c)

        def span(start, kw, c_sum, acc, diagonal):
            start = pl.multiple_of(start, kw)
            k_w = k_ref[pl.ds(start, kw), :]
            v_w = v_ref[pl.ds(start, kw), :]
            z_scr[:, :kw] = lax.dot_general(qb, k_w, _dims(False, True), preferred_element_type=F32)
            c_rows = [c_sum[r * SB_ROWS:(r + 1) * SB_ROWS] for r in range(nrc)]
            for s in reversed(range(kw // SB_SUB)):
                cols = slice(s * SB_SUB, (s + 1) * SB_SUB)
                for r in range(nrc):
                    rows = slice(r * SB_ROWS, (r + 1) * SB_ROWS)
                    before = _sb_chunk_mask(diagonal, r, s)
                    if isinstance(before, str):
                        a_scr[rows, cols] = jnp.zeros((SB_ROWS, SB_SUB), BF16)
                        continue
                    _, log_beta, log_1m = _sb_terms(z_scr[rows, cols] * SB_SCALE, before)
                    tail, total = _sums(log_1m, after)
                    a = jnp.exp(log_beta + c_rows[r] + tail)
                    if before is not None:
                        a = jnp.where(before, a, 0.0)
                    a_scr[rows, cols] = a.astype(BF16)
                    c_rows[r] = c_rows[r] + total
            acc = acc + jnp.dot(a_scr[:, :kw], v_w, preferred_element_type=F32)
            return jnp.concatenate(c_rows, axis=0), acc

        c_sum, acc = span(i * bq, bq, jnp.zeros((bq, SB_SUB), F32), jnp.zeros((bq, HEAD_DIM), F32), True)

        def more(state):
            done, c, _ = state
            return (done < i * (bq // bk)) & (jnp.max(c) > SB_DEAD)

        def step(state):
            done, c, a = state
            c, a = span(i * bq - (done + 1) * bk, bk, c, a, False)
            return done + 1, c, a

        done, c_sum, acc = lax.while_loop(more, step, (jnp.int32(0), c_sum, acc))
        o_ref[...] = acc.astype(BF16)
        l_ref[...] = c_sum
        n_ref[...] = jnp.full((8, 128), done.astype(F32), F32)

    nq = n // bq
    return _call(
        body, [q, kv, kv], name=name, grid=(N_HEADS, nq),
        in_specs=[pl.BlockSpec((bq, HEAD_DIM), lambda h, i: (i, h)),
                  pl.BlockSpec((n, HEAD_DIM), lambda h, i: (0, h)),
                  pl.BlockSpec((n, HEAD_DIM), lambda h, i: (0, N_HEADS + h))],
        out_specs=[pl.BlockSpec((bq, HEAD_DIM), lambda h, i: (i, h)),
                   pl.BlockSpec((None, bq, 128), lambda h, i: (h, i, 0)),
                   pl.BlockSpec((None, None, 8, 128), lambda h, i: (h, i, 0, 0))],
        out_shape=[jax.ShapeDtypeStruct((n, D_MODEL), BF16), jax.ShapeDtypeStruct((N_HEADS, n, 128), F32),
                   jax.ShapeDtypeStruct((N_HEADS, nq, 8, 128), F32)],
        scratch_shapes=[pltpu.VMEM((bq, bq), F32), pltpu.VMEM((bq, bq), BF16)],
        compiler_params=_params("arbitrary", "arbitrary"),
        ride=ride, first=lambda: (pl.program_id(0) == 0) & (pl.program_id(1) == 0),
        last=lambda: (pl.program_id(0) == N_HEADS - 1) & (pl.program_id(1) == nq - 1))


def _sb_attention_bwd(q, kv, do, lsum, spans, name, bq=SB_BQ, ride=None):
    n = q.shape[0]
    bq = min(bq, n)
    bk = min(SB_BK, bq)
    nrc = bq // SB_ROWS

    def body(q_ref, k_ref, v_ref, do_ref, l_ref, n_ref, dq_ref, dk_out, dv_out, z_scr, da_scr, a_scr, dz_scr, dk_ref, dv_ref):
        i = pl.program_id(1)

        @pl.when(i == 0)
        def _():
            dk_ref[...] = jnp.zeros_like(dk_ref)
            dv_ref[...] = jnp.zeros_like(dv_ref)

        qb = q_ref[...]
        dob = do_ref[...]
        lt_rows = [l_ref[r * SB_ROWS:(r + 1) * SB_ROWS, :] for r in range(nrc)]
        upto = _tri_ones(SB_SUB, lambda r, c: r <= c)
        below = _tri_ones(SB_SUB, lambda r, c: r < c)

        def span(start, kw, l_sum, g_sum, dq, diagonal):
            start = pl.multiple_of(start, kw)
            k_w = k_ref[pl.ds(start, kw), :]
            v_w = v_ref[pl.ds(start, kw), :]
            z_scr[:, :kw] = lax.dot_general(qb, k_w, _dims(False, True), preferred_element_type=F32)
            da_scr[:, :kw] = lax.dot_general(dob, v_w, _dims(False, True), preferred_element_type=F32)
            l_rows = [l_sum[r * SB_ROWS:(r + 1) * SB_ROWS] for r in range(nrc)]
            g_rows = [g_sum[r * SB_ROWS:(r + 1) * SB_ROWS] for r in range(nrc)]
            for s in range(kw // SB_SUB):
                cols = slice(s * SB_SUB, (s + 1) * SB_SUB)
                for r in range(nrc):
                    rows = slice(r * SB_ROWS, (r + 1) * SB_ROWS)
                    before = _sb_chunk_mask(diagonal, r, s)
                    if isinstance(before, str):
                        a_scr[rows, cols] = jnp.zeros((SB_ROWS, SB_SUB), BF16)
                        dz_scr[rows, cols] = jnp.zeros((SB_ROWS, SB_SUB), BF16)
                        continue
                    zs = z_scr[rows, cols] * SB_SCALE
                    e, log_beta, log_1m = _sb_terms(zs, before)
                    l_prefix, l_total = _sums(log_1m, upto)
                    a = jnp.exp(log_beta + (lt_rows[r] - (l_rows[r] + l_prefix)))
                    if before is not None:
                        a = jnp.where(before, a, 0.0)
                    g = a * da_scr[rows, cols]
                    g_prefix, g_total = _sums(g, below)
                    inv = 1.0 / (1.0 + e)
                    beta = jnp.where(zs >= 0, inv, e * inv)
                    dz = (g - (g + g_rows[r] + g_prefix) * beta) * SB_SCALE
                    if before is not None:
                        dz = jnp.where(before, dz, 0.0)
                    a_scr[rows, cols] = a.astype(BF16)
                    dz_scr[rows, cols] = dz.astype(BF16)
                    l_rows[r] = l_rows[r] + l_total
                    g_rows[r] = g_rows[r] + g_total
            dz_w = dz_scr[:, :kw]
            dq = dq + jnp.dot(dz_w, k_w, preferred_element_type=F32)
            dk_ref[pl.ds(start, kw), :] += lax.dot_general(dz_w, qb, _dims(True, False), preferred_element_type=F32)
            dv_ref[pl.ds(start, kw), :] += lax.dot_general(a_scr[:, :kw], dob, _dims(True, False),
                                                           preferred_element_type=F32)
            return jnp.concatenate(l_rows, axis=0), jnp.concatenate(g_rows, axis=0), dq

        taken = jnp.clip(jnp.max(n_ref[...]).astype(jnp.int32), 0, i * (bq // bk))
        zero = jnp.zeros((bq, SB_SUB), F32)
        carry = lax.fori_loop(0, taken, lambda j, c: span(i * bq - (taken - j) * bk, bk, c[0], c[1], c[2], False),
                              (zero, zero, jnp.zeros((bq, HEAD_DIM), F32)))
        _, _, dq = span(i * bq, bq, carry[0], carry[1], carry[2], True)
        dq_ref[...] = dq.astype(BF16)

        @pl.when(i == nq - 1)
        def _():
            dk_out[...] = dk_ref[...].astype(BF16)
            dv_out[...] = dv_ref[...].astype(BF16)

    nq = n // bq
    return _call(
        body, [q, kv, kv, do, lsum, spans], name=name, grid=(N_HEADS, nq),
        in_specs=[pl.BlockSpec((bq, HEAD_DIM), lambda h, i: (i, h)),
                  pl.BlockSpec((n, HEAD_DIM), lambda h, i: (0, h)),
                  pl.BlockSpec((n, HEAD_DIM), lambda h, i: (0, N_HEADS + h)),
                  pl.BlockSpec((bq, HEAD_DIM), lambda h, i: (i, h)),
                  pl.BlockSpec((None, bq, 128), lambda h, i: (h, i, 0)),
                  pl.BlockSpec((None, None, 8, 128), lambda h, i: (h, i, 0, 0))],
        out_specs=[pl.BlockSpec((bq, HEAD_DIM), lambda h, i: (i, h)),
                   pl.BlockSpec((n, HEAD_DIM), lambda h, i: (0, h)),
                   pl.BlockSpec((n, HEAD_DIM), lambda h, i: (0, h))],
        out_shape=[jax.ShapeDtypeStruct((n, D_MODEL), BF16)] * 3,
        scratch_shapes=[pltpu.VMEM((bq, bq), F32), pltpu.VMEM((bq, bq), F32), pltpu.VMEM((bq, bq), BF16),
                        pltpu.VMEM((bq, bq), BF16), pltpu.VMEM((n, HEAD_DIM), F32), pltpu.VMEM((n, HEAD_DIM), F32)],
        compiler_params=_params("arbitrary", "arbitrary"),
        ride=ride, first=lambda: (pl.program_id(0) == 0) & (pl.program_id(1) == 0),
        last=lambda: (pl.program_id(0) == N_HEADS - 1) & (pl.program_id(1) == nq - 1))


def _local_step(x, target, gains, comm):
    g = gains
    w = comm.w
    big = {}
    row = lambda a, i: a[i:i + 1, :]

    proj, h0 = _norm_matmul(x, row(g["mix_pre"], 0), w["gdn_in"], F32, "gdn_in_proj", tn=1408)
    qn, kn, v, bg, conv = _gdn_pre(proj, w["conv"], g["alog"], g["dtb"], "gdn_pre")
    (o_gdn, saved, inverses), got = _gdn_scan(qn, kn, v, bg, "gdn_scan", ride=comm.ride("scan"))
    comm.done("scan", got)
    x1, y_mix0, a_gdn = _out_proj((o_gdn, proj), True, x, w["gdn_out"], row(g["mix_post"], 0), g["out_gain"],
                                  "gdn_out_proj", tm=512)
    ride = comm.ride("mlp0")
    res = _mlp_fwd(x1, row(g["mlp_pre"], 0), w["up0"], w["down0"], row(g["mlp_post"], 0), "mlp0", ride=ride)
    (x2, y_mlp0, h_mlp0, u0, a0), got = res if ride is not None else (res, [])
    comm.done("mlp0", got)
    kv, h_kv = _norm_matmul(x2, g["kv"], w["kv"], BF16, "kv_proj", tm=2048)
    q, h_q = _norm_matmul(x2, row(g["mix_pre"], 1), w["sb_q"], BF16, "sb_q_proj")
    (o_sb, lsum, spans), got = _sb_attention(q, kv, "sb_attention", ride=comm.ride("sb"))
    comm.done("sb", got)
    x3, y_mix1 = _out_proj((o_sb,), False, x2, w["sb_o"], row(g["mix_post"], 1), None, "sb_out_proj")
    dx4, y_mlp1, h_mlp1, u1, a1, loss = _mlp_fwd(x3, row(g["mlp_pre"], 1), w["up1"], w["down1"], row(g["mlp_post"], 1),
                                                 "mlp1_loss", target=target)

    dx3, dy_mlp1, du1, dg_mlp1 = _mlp_bwd(dx4, y_mlp1, row(g["mlp_post"], 1), w["down1"], u1, w["up1"], x3,
                                          row(g["mlp_pre"], 1), "mlp1_bwd")
    big["down1"] = _matmul_tn(a1, dy_mlp1, 1, "d_down1")[0]
    big["up1"] = _matmul_tn(h_mlp1, du1, N_DEV, "d_up1", tk=2048)
    dy_mix1, dg_post1, do_sb = _out_proj_bwd(dx3, y_mix1, w["sb_o"], row(g["mix_post"], 1), None, "sb_out_proj_bwd")
    big["sb_o"] = _matmul_tn(o_sb, dy_mix1, 1, "d_sb_o")[0]
    (dq, dk, dv), got = _sb_attention_bwd(q, kv, do_sb, lsum, spans, "sb_attention_bwd", ride=comm.ride("sb_bwd", big))
    comm.done("sb_bwd", got)
    big["sb_q"] = _matmul_tn(h_q, dq, 1, "d_sb_q")[0]
    big["kv"] = jnp.concatenate([_matmul_tn(h_kv, dk, N_DEV // 2, "d_w_k", tk=2048),
                                 _matmul_tn(h_kv, dv, N_DEV // 2, "d_w_v", tk=2048)],
                                axis=0)
    dx2, dg_x2 = _norm_matmul_bwd([(dq, w["sb_q"], row(g["mix_pre"], 1), 1024), (dk, w["kv"][0:1], g["kv"], 1024),
                                   (dv, w["kv"][1:2], g["kv"], 1024)], x2, dx3, "qkv_proj_bwd", tm=512)

    big.update(x2=dg_x2, post1=dg_post1, mlp1=dg_mlp1, loss=loss)
    ride = comm.ride("mlp0_bwd", big)
    res = _mlp_bwd(dx2, y_mlp0, row(g["mlp_post"], 0), w["down0"], u0, w["up0"], x1, row(g["mlp_pre"], 0), "mlp0_bwd",
                   ride=ride)
    (dx1, dy_mlp0, du0, dg_mlp0), got = res if ride is not None else (res, [])
    comm.done("mlp0_bwd", got)
    big["down0"] = _matmul_tn(a0, dy_mlp0, 1, "d_down0")[0]
    big["up0"] = _matmul_tn(h_mlp0, du0, N_DEV, "d_up0", tk=2048)
    dy_mix0, dg_post0, do_gdn, dgate, d_out_gain = _out_proj_bwd(
        dx1, y_mix0, w["gdn_out"], row(g["mix_post"], 0), (o_gdn, proj, g["out_gain"]), "gdn_out_proj_bwd", tm=512)
    big["gdn_out"] = _matmul_tn(a_gdn, dy_mix0, 1, "d_gdn_out")[0]
    big.update(post0=dg_post0, mlp0=dg_mlp0, out_gain=d_out_gain)
    (dqn, dkn, dvv, dbg), got = _gdn_scan_bwd(qn, kn, v, bg, saved, inverses, do_gdn, "gdn_scan_bwd",
                                              ride=comm.ride("scan_bwd", big))
    comm.done("scan_bwd", got)
    dproj, d_conv, d_gates = _gdn_pre_bwd(proj, conv, w["conv"], g["alog"], g["dtb"], dqn, dkn, dvv, dbg, dgate,
                                          "gdn_pre_bwd")
    big["gdn_in"] = _matmul_tn(h0, dproj, 1, "d_gdn_in", tb=1408)[0]
    big.update(conv=d_conv, gates=d_gates)
    ride = comm.ride("in_bwd", big)
    grad_x, dg_pre0, *got = _norm_matmul_bwd([(dproj, w["gdn_in"], row(g["mix_pre"], 0), 1408)], x, dx1, "gdn_in_proj_bwd",
                                             ride=ride)
    comm.done("in_bwd", got[0] if got else [])

    small = {"pre0": dg_pre0, "x2": dg_x2, "post0": dg_post0, "post1": dg_post1, "mlp0": dg_mlp0, "mlp1": dg_mlp1,
             "gates": d_gates, "out_gain": d_out_gain, "conv": d_conv, "loss": loss}
    return grad_x, big, small


GATHER_STAGES = {"scan": ("gdn_out", "up0", "down0"), "mlp0": ("kv", "sb_q"), "sb": ("sb_o", "up1", "down1")}
SCATTER_STAGES = {"sb_bwd": ("up1", "down1", "sb_o"), "mlp0_bwd": ("sb_q", "kv"), "scan_bwd": ("up0", "down0", "gdn_out"),
                  "in_bwd": ("gdn_in",)}
SMALL_STAGES = {"mlp0_bwd": ("x2", "post1", "mlp1", "loss"), "scan_bwd": ("post0", "mlp0", "out_gain"),
                "in_bwd": ("conv", "gates")}


def _whole_weight(name, gathered):
    d = D_MODEL
    if name in ("up0", "up1"):
        return gathered
    if name == "kv":
        return gathered.reshape(2, N_DEV // 2, d, gathered.shape[2]).transpose(0, 2, 1, 3).reshape(2, d, d)
    if name == "gdn_in":
        whole = gathered.transpose(1, 0, 2).reshape(d, GDN_IN_COLS)
        return jnp.pad(whole, ((0, 0), (0, GDN_IN_PAD - GDN_IN_COLS)))[None]
    if name == "conv":
        return gathered.transpose(1, 0, 2).reshape(CONV_WIDTH, QKV)
    whole = gathered.reshape(gathered.shape[0] * gathered.shape[1], d)
    return whole[None] if name == "sb_q" else whole


def _owner_blocks(name, grad):
    if name in ("up0", "up1", "kv"):
        blocks = grad
    elif name == "gdn_in":
        blocks = grad[:, :GDN_IN_COLS].reshape(D_MODEL, N_DEV, GDN_IN_COLS // N_DEV).transpose(1, 0, 2)
    else:
        blocks = grad.reshape(N_DEV, grad.shape[0] // N_DEV, grad.shape[1])
    return blocks.astype(BF16)


class _Fsdp:
    def __init__(self, shards, weights):
        self.shards, self.w, self.landed, self.small_landed = shards, weights, {}, {}

    def ride(self, stage, grads=None):
        if stage in GATHER_STAGES:
            names = GATHER_STAGES[stage]
            return _Exchange([self.shards[nm] for nm in names], [True] * len(names))
        names, small = SCATTER_STAGES[stage], SMALL_STAGES.get(stage, ())
        return _Exchange([_owner_blocks(nm, grads[nm]) for nm in names] + [grads[nm] for nm in small],
                         [False] * len(names) + [True] * len(small))

    def done(self, stage, outs):
        if stage in GATHER_STAGES:
            for nm, out in zip(GATHER_STAGES[stage], outs):
                self.w[nm] = _whole_weight(nm, out)
        else:
            names = SCATTER_STAGES[stage]
            self.landed.update(zip(names, outs[:len(names)]))
            self.small_landed.update(zip(SMALL_STAGES.get(stage, ()), outs[len(names):]))


def _my_place():
    return lax.axis_index("x"), lax.axis_index("y"), lax.axis_index("c")


class _Exchange:
    def __init__(self, arrays, gather):
        self.arrays, self.gather, self.n = list(arrays), list(gather), len(arrays)
        any_spec = pl.BlockSpec(memory_space=pl.ANY)
        self.in_specs = [any_spec] * self.n
        self.out_specs = [any_spec] * self.n
        self.out_shape = [jax.ShapeDtypeStruct(((N_DEV,) + a.shape) if g else a.shape, a.dtype)
                          for a, g in zip(self.arrays, self.gather)]
        self.scratch = [pltpu.SemaphoreType.DMA((self.n, N_DEV - 1)), pltpu.SemaphoreType.DMA((self.n, N_DEV - 1)),
                        pltpu.SemaphoreType.DMA((self.n,))]

    def _copies(self, ins, outs, sems):
        send_sems, recv_sems, local_sems = sems
        x, y, c = _my_place()
        me = 4 * x + 2 * y + c
        copies = []
        for a in range(self.n):
            src = ins[a] if self.gather[a] else ins[a].at[me]
            copies.append(pltpu.make_async_copy(src, outs[a].at[me], local_sems.at[a]))
        for k in range(1, N_DEV):
            px = 1 - x if k & 4 else x
            py = 1 - y if k & 2 else y
            pc = 1 - c if k & 1 else c
            peer = 4 * px + 2 * py + pc
            for a in range(self.n):
                src = ins[a] if self.gather[a] else ins[a].at[peer]
                copies.append(pltpu.make_async_remote_copy(
                    src_ref=src, dst_ref=outs[a].at[me], send_sem=send_sems.at[a, k - 1], recv_sem=recv_sems.at[a, k - 1],
                    device_id=(px, py, pc), device_id_type=MESH))
        return copies

    def start(self, ins, outs, sems):
        for cp in self._copies(ins, outs, sems):
            cp.start()

    def wait(self, ins, outs, sems):
        for cp in self._copies(ins, outs, sems):
            cp.wait()


def _call(body, operands, *, name, out_shape, in_specs, out_specs, grid=(), scratch_shapes=(), compiler_params=None,
          ride=None, first=None, last=None):
    n_in, n_out, n_scr = len(operands), len(out_shape), len(scratch_shapes)
    if ride is None:
        outs = pl.pallas_call(body, name=name, grid=grid, in_specs=in_specs, out_specs=out_specs, out_shape=out_shape,
                              scratch_shapes=scratch_shapes, compiler_params=compiler_params)(*operands)
        return list(outs), []
    r = ride.n

    def riding(*refs):
        ins, r_ins = refs[:n_in], refs[n_in:n_in + r]
        outs, r_outs = refs[n_in + r:n_in + r + n_out], refs[n_in + r + n_out:n_in + 2 * r + n_out]
        scr, sems = refs[n_in + 2 * r + n_out:n_in + 2 * r + n_out + n_scr], refs[n_in + 2 * r + n_out + n_scr:]

        @pl.when(first())
        def _():
            ride.start(r_ins, r_outs, sems)

        body(*ins, *outs, *scr)

        @pl.when(last())
        def _():
            ride.wait(r_ins, r_outs, sems)

    outs = pl.pallas_call(
        riding, name=name, grid=grid, in_specs=list(in_specs) + ride.in_specs, out_specs=list(out_specs) + ride.out_specs,
        out_shape=list(out_shape) + ride.out_shape, scratch_shapes=list(scratch_shapes) + ride.scratch,
        compiler_params=compiler_params)(*operands, *ride.arrays)
    return list(outs[:n_out]), list(outs[n_out:])


def _exchange(arrays, gather, name):
    ride = _Exchange(arrays, gather)

    def body(*refs):
        ins, outs, sems = refs[:ride.n], refs[ride.n:2 * ride.n], refs[2 * ride.n:]
        ride.start(ins, outs, sems)
        ride.wait(ins, outs, sems)

    return pl.pallas_call(body, name=name, in_specs=ride.in_specs, out_specs=ride.out_specs, out_shape=ride.out_shape,
                          scratch_shapes=ride.scratch)(*arrays)


def _gather_two_level(arrays, name):
    n_arr = len(arrays)

    def body(*refs):
        ins, outs = refs[:n_arr], refs[n_arr:2 * n_arr]
        send_sems, recv_sems, local_sems = refs[2 * n_arr:]
        x, y, c = _my_place()
        sibling = (x, y, 1 - c)
        chips = [(1 - x, y), (x, 1 - y), (1 - x, 1 - y)]
        index = lambda px, py, pc: 4 * px + 2 * py + pc

        def copy(a, k, block, to, src=None):
            dst = outs[a].at[index(*block)]
            return pltpu.make_async_remote_copy(src_ref=dst if src is None else src, dst_ref=dst,
                                                send_sem=send_sems.at[a, k], recv_sem=recv_sems.at[a, k],
                                                device_id=to, device_id_type=MESH)

        mine = [pltpu.make_async_copy(ins[a], outs[a].at[index(x, y, c)], local_sems.at[a]) for a in range(n_arr)]
        first = [copy(a, 0, (x, y, c), sibling, src=ins[a]) for a in range(n_arr)]
        first += [copy(a, 1 + j, (x, y, c), (*chip, c), src=ins[a]) for j, chip in enumerate(chips) for a in range(n_arr)]
        for cp in mine + first:
            cp.start()
        passed = []
        for j, chip in enumerate(chips):
            for a in range(n_arr):
                copy(a, 1 + j, (*chip, c), (x, y, c)).wait_recv()
                passed.append(copy(a, 4 + j, (*chip, c), sibling))
                passed[-1].start()
        for a in range(n_arr):
            copy(a, 0, sibling, (x, y, c)).wait_recv()
            for j, chip in enumerate(chips):
                copy(a, 4 + j, (*chip, 1 - c), (x, y, c)).wait_recv()
        for cp in first + passed:
            cp.wait_send()
        for cp in mine:
            cp.wait()

    any_spec = pl.BlockSpec(memory_space=pl.ANY)
    return pl.pallas_call(
        body, name=name, in_specs=[any_spec] * n_arr, out_specs=[any_spec] * n_arr,
        out_shape=[jax.ShapeDtypeStruct((N_DEV,) + a.shape, a.dtype) for a in arrays],
        scratch_shapes=[pltpu.SemaphoreType.DMA((n_arr, N_DEV - 1)), pltpu.SemaphoreType.DMA((n_arr, N_DEV - 1)),
                        pltpu.SemaphoreType.DMA((n_arr,))])(*arrays)


def _adamw_math(g, w, m, v):
    m = ADAM_B1 * m + (1.0 - ADAM_B1) * g
    v = ADAM_B2 * v + (1.0 - ADAM_B2) * jnp.square(g)
    m_hat = m / (1.0 - ADAM_B1 ** ADAM_STEP)
    v_hat = v / (1.0 - ADAM_B2 ** ADAM_STEP)
    delta = -ADAM_LR * (m_hat / (jnp.sqrt(v_hat) + ADAM_EPS) + ADAM_WD * w)
    return delta, m, v


def _sum_devices(ref):
    total = ref[0].astype(F32)
    for d in range(1, N_DEV):
        total = total + ref[d].astype(F32)
    return total


def _reduce_adamw(landed, w, m, v, name, tr=256):
    r, c = w.shape
    tr = min(tr, r)
    assert r % tr == 0

    def body(l_ref, w_ref, m_ref, v_ref, g_ref, d_ref, nm_ref, nv_ref):
        g = _sum_devices(l_ref)
        g_ref[...] = g
        d_ref[...], nm_ref[...], nv_ref[...] = _adamw_math(g, w_ref[...], m_ref[...], v_ref[...])

    blk = pl.BlockSpec((tr, c), lambda i: (i, 0))
    return pl.pallas_call(
        body, name=name, grid=(r // tr,),
        in_specs=[pl.BlockSpec((N_DEV, tr, c), lambda i: (0, i, 0)), blk, blk, blk],
        out_specs=[blk] * 4, out_shape=[jax.ShapeDtypeStruct((r, c), F32)] * 4,
        compiler_params=_params("parallel"),
    )(landed, w, m, v)


def _reduce_adamw_layers(landed, w, m, v, name, tr=256):
    _, r, c = w.shape
    tr = min(tr, r)
    nr = r // tr
    assert r % tr == 0

    def body(l0_ref, l1_ref, w_ref, m_ref, v_ref, g_ref, d_ref, nm_ref, nv_ref):
        g = jnp.where(pl.program_id(0) == 0, _sum_devices(l0_ref), _sum_devices(l1_ref))
        g_ref[...] = g
        d_ref[...], nm_ref[...], nv_ref[...] = _adamw_math(g, w_ref[...], m_ref[...], v_ref[...])

    blk = pl.BlockSpec((None, tr, c), lambda l, i: (l, i, 0))
    return pl.pallas_call(
        body, name=name, grid=(2, nr),
        in_specs=[pl.BlockSpec((N_DEV, tr, c), lambda l, i: (0, jnp.where(l == 0, i, nr - 1), 0)),
                  pl.BlockSpec((N_DEV, tr, c), lambda l, i: (0, jnp.where(l == 1, i, 0), 0)), blk, blk, blk],
        out_specs=[blk] * 4, out_shape=[jax.ShapeDtypeStruct(w.shape, F32)] * 4,
        compiler_params=_params("arbitrary", "arbitrary"),
    )(landed[0], landed[1], w, m, v)


def _adamw(g, w, m, v, name):
    def body(g_ref, w_ref, m_ref, v_ref, d_ref, nm_ref, nv_ref):
        d_ref[...], nm_ref[...], nv_ref[...] = _adamw_math(g_ref[...], w_ref[...], m_ref[...], v_ref[...])

    return pl.pallas_call(body, name=name, out_shape=[jax.ShapeDtypeStruct(w.shape, F32)] * 3)(g, w, m, v)


def _small_update(landed, params, name):
    layout = {
        "mix_pre": [("pre0", 0), ("x2", 0)], "mix_post": [("post0", 0), ("post1", 0)],
        "mlp_pre": [("mlp0", 0), ("mlp1", 0)], "mlp_post": [("mlp0", 1), ("mlp1", 1)],
        "kv": [("kv", 0)], "alog": [("gates", 0)], "dtb": [("gates", 1)], "out_gain": [("out_gain", 0)],
    }
    landed_names = sorted(landed)
    param_names = sorted(params)
    n_l, n_p = len(landed_names), len(param_names)

    def body(*refs):
        l_refs = dict(zip(landed_names, refs[:n_l]))
        p_refs = {p: refs[n_l + 3 * i:n_l + 3 * i + 3] for i, p in enumerate(param_names)}
        outs = refs[n_l + 3 * n_p:]
        o_refs = {p: outs[4 * i:4 * i + 4] for i, p in enumerate(param_names)}
        conv_ref, loss_ref = outs[4 * n_p:]
        sums = {nm: _sum_devices(l_refs[nm]) for nm in landed_names}
        sums["kv"] = sums["x2"][1:2, :] + sums["x2"][2:3, :]
        for p in param_names:
            w_ref, m_ref, v_ref = p_refs[p]
            g_ref, d_ref, nm_ref, nv_ref = o_refs[p]
            for r, (src, src_row) in enumerate(layout[p]):
                g = sums[src][src_row:src_row + 1, :]
                g_ref[r:r + 1, :] = g
                d, nm, nv = _adamw_math(g, w_ref[r:r + 1, :], m_ref[r:r + 1, :], v_ref[r:r + 1, :])
                d_ref[r:r + 1, :] = d
                nm_ref[r:r + 1, :] = nm
                nv_ref[r:r + 1, :] = nv
        conv_ref[...] = sums["conv"]
        loss_ref[...] = sums["loss"]

    args = [landed[nm] for nm in landed_names]
    out_shape = []
    for p in param_names:
        args += list(params[p])
        out_shape += [jax.ShapeDtypeStruct(params[p][0].shape, F32)] * 4
    out_shape += [jax.ShapeDtypeStruct(landed["conv"].shape[1:], F32), jax.ShapeDtypeStruct(landed["loss"].shape[1:], F32)]
    outs = pl.pallas_call(body, name=name, out_shape=out_shape)(*args)
    result = {p: tuple(outs[4 * i:4 * i + 4]) for i, p in enumerate(param_names)}
    result["conv"], result["loss"] = outs[4 * n_p], outs[4 * n_p + 1]
    return result


def _lanes(vec, offset):
    return jnp.pad(vec[None, :], ((0, 0), (offset, 128 - offset - vec.shape[0])))


def kernel(x, mix_pre_gain, mix_post_gain, mlp_pre_gain, mlp_post_gain, mlp_w_up, mlp_w_down, gdn_w_in, gdn_conv_w, gdn_a_log, gdn_dt_bias, gdn_out_gain, gdn_w_out, kv_gain, w_kv, sb_w_q, sb_w_o, loss_target, m_mix_pre_gain, m_mix_post_gain, m_mlp_pre_gain, m_mlp_post_gain, m_mlp_w_up, m_mlp_w_down, m_gdn_w_in, m_gdn_conv_w, m_gdn_a_log, m_gdn_dt_bias, m_gdn_out_gain, m_gdn_w_out, m_kv_gain, m_w_kv, m_sb_w_q, m_sb_w_o, v_mix_pre_gain, v_mix_post_gain, v_mlp_pre_gain, v_mlp_post_gain, v_mlp_w_up, v_mlp_w_down, v_gdn_w_in, v_gdn_conv_w, v_gdn_a_log, v_gdn_dt_bias, v_gdn_out_gain, v_gdn_w_out, v_kv_gain, v_w_kv, v_sb_w_q, v_sb_w_o):
    me = 4 * lax.axis_index("x") + 2 * lax.axis_index("y") + lax.axis_index("c")
    bf = lambda a: a.astype(BF16)

    shards = {"up0": bf(mlp_w_up[0]), "up1": bf(mlp_w_up[1]), "down0": bf(mlp_w_down[0]), "down1": bf(mlp_w_down[1]),
              "gdn_out": bf(gdn_w_out[0]), "kv": bf(w_kv), "sb_q": bf(sb_w_q[0]), "sb_o": bf(sb_w_o[0])}
    gdn_in, conv = _gather_two_level([bf(gdn_w_in[0]), gdn_conv_w[0]], "gather_first_weights")
    comm = _Fsdp(shards, {"gdn_in": _whole_weight("gdn_in", gdn_in), "conv": _whole_weight("conv", conv)})
    gains = {"mix_pre": mix_pre_gain, "mix_post": mix_post_gain, "mlp_pre": mlp_pre_gain, "mlp_post": mlp_post_gain,
             "kv": kv_gain[None, :], "alog": _lanes(gdn_a_log[0], N_HEADS), "dtb": _lanes(gdn_dt_bias[0], N_HEADS),
             "out_gain": gdn_out_gain}

    grad_x, big, small = _local_step(x[0], loss_target[0], gains, comm)

    (pre0,) = _exchange([small["pre0"]], [True], "exchange_last_grad")
    small_landed = dict(comm.small_landed, pre0=pre0)

    results = {}
    results["mlp_w_up"] = _reduce_adamw_layers((comm.landed["up0"], comm.landed["up1"]), mlp_w_up, m_mlp_w_up, v_mlp_w_up,
                                               "adamw_mlp_w_up")
    results["mlp_w_down"] = _reduce_adamw_layers((comm.landed["down0"], comm.landed["down1"]), mlp_w_down, m_mlp_w_down,
                                                 v_mlp_w_down, "adamw_mlp_w_down")
    big_params = [("gdn_w_in", "gdn_in", gdn_w_in[0], m_gdn_w_in[0], v_gdn_w_in[0]),
                  ("gdn_w_out", "gdn_out", gdn_w_out[0], m_gdn_w_out[0], v_gdn_w_out[0]),
                  ("w_kv", "kv", w_kv, m_w_kv, v_w_kv), ("sb_w_q", "sb_q", sb_w_q[0], m_sb_w_q[0], v_sb_w_q[0]),
                  ("sb_w_o", "sb_o", sb_w_o[0], m_sb_w_o[0], v_sb_w_o[0])]
    for nm, short, w_, m_, v_ in big_params:
        results[nm] = _reduce_adamw(comm.landed[short], w_, m_, v_, "adamw_" + nm)
    lanes8 = lambda a: _lanes(a[0], N_HEADS)
    small_params = {
        "mix_pre": (mix_pre_gain, m_mix_pre_gain, v_mix_pre_gain), "mix_post": (mix_post_gain, m_mix_post_gain, v_mix_post_gain),
        "mlp_pre": (mlp_pre_gain, m_mlp_pre_gain, v_mlp_pre_gain), "mlp_post": (mlp_post_gain, m_mlp_post_gain, v_mlp_post_gain),
        "kv": (kv_gain[None, :], m_kv_gain[None, :], v_kv_gain[None, :]),
        "alog": (lanes8(gdn_a_log), lanes8(m_gdn_a_log), lanes8(v_gdn_a_log)),
        "dtb": (lanes8(gdn_dt_bias), lanes8(m_gdn_dt_bias), lanes8(v_gdn_dt_bias)),
        "out_gain": (gdn_out_gain, m_gdn_out_gain, v_gdn_out_gain),
    }
    sm = _small_update(small_landed, small_params, "small_update")
    conv_cols = QKV // N_DEV
    g_conv = lax.dynamic_slice(sm["conv"], (0, me * conv_cols), (8, conv_cols))[:CONV_WIDTH]
    conv_res = (g_conv,) + tuple(_adamw(g_conv, gdn_conv_w[0], m_gdn_conv_w[0], v_gdn_conv_w[0], "adamw_conv"))

    lead = lambda t: tuple(a[None] for a in t)
    heads8 = lambda t: tuple(a[:, N_HEADS:2 * N_HEADS] for a in t)
    per_weight = [
        sm["mix_pre"], sm["mix_post"], sm["mlp_pre"], sm["mlp_post"],
        tuple(results["mlp_w_up"]), tuple(results["mlp_w_down"]),
        lead(results["gdn_w_in"]), lead(conv_res), heads8(sm["alog"]), heads8(sm["dtb"]), sm["out_gain"],
        lead(results["gdn_w_out"]), tuple(a[0] for a in sm["kv"]), results["w_kv"], lead(results["sb_w_q"]), lead(results["sb_w_o"]),
    ]
    grads, deltas, new_ms, new_vs = zip(*per_weight)
    return (sm["loss"][0, 0], grad_x[None], *grads, *deltas, *new_ms, *new_vs)
```

```python
import functools
import math

import jax
import jax.numpy as jnp
from jax import lax
from jax.experimental import pallas as pl
from jax.experimental.pallas import tpu as pltpu

F32 = jnp.float32
BF16 = jnp.bfloat16

N_DEV = 8
D_MODEL = 1024
N_HEADS = 8
HEAD_DIM = 128
CHUNK = 64
CONV_WIDTH = 4
GDN_IN_COLS = 4 * D_MODEL + 2 * N_HEADS
GDN_IN_PAD = 4 * D_MODEL + 128
EPS = 1e-6

ADAM_LR = 0.001
ADAM_B1 = 0.9
ADAM_B2 = 0.999
ADAM_EPS = 1e-08
ADAM_WD = 0.01
ADAM_STEP = 10

VMEM_LIMIT_BYTES = 56 * 1024 * 1024
MESH = pl.DeviceIdType.MESH


def _params(*semantics):
    return pltpu.CompilerParams(dimension_semantics=semantics, vmem_limit_bytes=VMEM_LIMIT_BYTES)


def _dims(ta, tb):
    return (((0,) if ta else (1,), (1,) if tb else (0,)), ((), ()))


def _dot(a, b, ta=False, tb=False):
    return lax.dot_general(a.astype(BF16), b.astype(BF16), _dims(ta, tb), preferred_element_type=F32)


def _dot_f32(a, b, ta=False, tb=False):
    return lax.dot_general(a, b, _dims(ta, tb), precision=lax.Precision.HIGHEST, preferred_element_type=F32)


def _dot_3x(a, b, ta=False, tb=False):
    return lax.dot_general(a, b, _dims(ta, tb), precision=lax.Precision.HIGH, preferred_element_type=F32)


def _make_mm(dot):
    @functools.partial(jax.custom_vjp, nondiff_argnums=(2, 3))
    def mm(a, b, ta, tb):
        return dot(a, b, ta, tb)

    def fwd(a, b, ta, tb):
        return dot(a, b, ta, tb), (a, b)

    def bwd(ta, tb, res, g):
        a, b = res
        if not ta and not tb:
            return mm(g, b, False, True), mm(a, g, True, False)
        if not ta and tb:
            return mm(g, b, False, False), mm(g, a, True, False)
        if ta and not tb:
            return mm(b, g, False, True), mm(a, g, False, False)
        raise NotImplementedError

    mm.defvjp(fwd, bwd)
    return mm


_mm = _make_mm(_dot)
_mm_f32 = _make_mm(_dot_f32)


def _rms(x, gain):
    r = lax.rsqrt(jnp.mean(x * x, axis=-1, keepdims=True) + EPS)
    return x * r * gain


def _rms_bwd(x, gain, dy):
    r = lax.rsqrt(jnp.mean(x * x, axis=-1, keepdims=True) + EPS)
    xh = x * r
    dgain = jnp.sum(dy * xh, axis=0, keepdims=True)
    dxh = dy * gain
    dx = r * (dxh - xh * jnp.mean(dxh * xh, axis=-1, keepdims=True))
    return dx, dgain


def _silu(x):
    return x / (1.0 + jnp.exp(-x))


def _acc_rows(ref, first, rows):
    @pl.when(first)
    def _():
        ref[...] = jnp.zeros_like(ref)

    for r, val in enumerate(rows):
        ref[r:r + 1, :] += val


def _norm_matmul(x, gain, w, out_dtype, name, tm=1024, tn=1024):
    n, d = x.shape
    g, _, wc = w.shape
    tm, tn = min(tm, n), min(tn, wc)
    per = wc // tn
    assert n % tm == 0 and wc % tn == 0

    def body(x_ref, gain_ref, w_ref, out_ref, h_ref):
        @pl.when(pl.program_id(1) == 0)
        def _():
            h_ref[...] = _rms(x_ref[...], gain_ref[...]).astype(BF16)

        out_ref[...] = jnp.dot(h_ref[...], w_ref[...], preferred_element_type=F32).astype(out_dtype)

    return pl.pallas_call(
        body, name=name, grid=(n // tm, g * per),
        in_specs=[pl.BlockSpec((tm, d), lambda i, j: (i, 0)),
                  pl.BlockSpec((1, d), lambda i, j: (0, 0)),
                  pl.BlockSpec((None, d, tn), lambda i, j: (j // per, 0, j % per))],
        out_specs=[pl.BlockSpec((tm, tn), lambda i, j: (i, j)),
                   pl.BlockSpec((tm, d), lambda i, j: (i, 0))],
        out_shape=[jax.ShapeDtypeStruct((n, g * wc), out_dtype), jax.ShapeDtypeStruct((n, d), BF16)],
        compiler_params=_params("parallel", "arbitrary"),
    )(x, gain, w)


def _norm_matmul_bwd(branches, x, add, name, tm=1024, ride=None):
    n, d = x.shape
    tm = min(tm, n)
    tns = [min(tn, w.shape[2]) for _, w, _, tn in branches]
    pers = [w.shape[2] // tn for (_, w, _, _), tn in zip(branches, tns)]
    ncols = [w.shape[0] * per for (_, w, _, _), per in zip(branches, pers)]
    offs = [sum(ncols[:b]) for b in range(len(branches))]
    total = sum(ncols)
    nb = len(branches)

    def body(*refs):
        dout_refs, w_refs, gain_refs = refs[:nb], refs[nb:2 * nb], refs[2 * nb:3 * nb]
        x_ref, add_ref, dx_ref, dgain_ref, acc_ref, sum_ref = refs[3 * nb:]
        i, j = pl.program_id(0), pl.program_id(1)

        @pl.when((i == 0) & (j == 0))
        def _():
            dgain_ref[...] = jnp.zeros_like(dgain_ref)

        for b in range(nb):
            first, last = offs[b], offs[b] + ncols[b] - 1

            @pl.when((j >= first) & (j <= last))
            def _(b=b, first=first):
                part = _dot(dout_refs[b][...], w_refs[b][...], tb=True)

                @pl.when(j == first)
                def _():
                    acc_ref[...] = part

                @pl.when(j > first)
                def _():
                    acc_ref[...] += part

            @pl.when(j == last)
            def _(b=b):
                xv = x_ref[...]
                xh = xv * lax.rsqrt(jnp.mean(xv * xv, axis=-1, keepdims=True) + EPS)
                dh = acc_ref[...]
                dgain_ref[b:b + 1, :] += jnp.sum(dh * xh, axis=0, keepdims=True)
                scaled = dh * gain_refs[b][...]
                if b == 0:
                    sum_ref[...] = scaled
                else:
                    sum_ref[...] += scaled

        @pl.when(j == total - 1)
        def _():
            xv = x_ref[...]
            r = lax.rsqrt(jnp.mean(xv * xv, axis=-1, keepdims=True) + EPS)
            xh = xv * r
            dxh = sum_ref[...]
            dx_ref[...] = add_ref[...] + r * (dxh - xh * jnp.mean(dxh * xh, axis=-1, keepdims=True))

    def dout_spec(b):
        return pl.BlockSpec((tm, tns[b]), lambda i, j: (i, jnp.clip(j - offs[b], 0, ncols[b] - 1)))

    def w_spec(b):
        def index(i, j):
            c = jnp.clip(j - offs[b], 0, ncols[b] - 1)
            return (c // pers[b], 0, c % pers[b])
        return pl.BlockSpec((None, d, tns[b]), index)

    row = lambda i, j: (i, 0)
    const = lambda i, j: (0, 0)
    nrow = n // tm
    (dx, dgain), got = _call(
        body, [b[0] for b in branches] + [b[1] for b in branches] + [b[2] for b in branches] + [x, add],
        name=name, grid=(nrow, total),
        in_specs=[dout_spec(b) for b in range(nb)] + [w_spec(b) for b in range(nb)]
                 + [pl.BlockSpec((1, d), const)] * nb + [pl.BlockSpec((tm, d), row), pl.BlockSpec((tm, d), row)],
        out_specs=[pl.BlockSpec((tm, d), row), pl.BlockSpec((8, d), const)],
        out_shape=[jax.ShapeDtypeStruct((n, d), F32), jax.ShapeDtypeStruct((8, d), F32)],
        scratch_shapes=[pltpu.VMEM((tm, d), F32), pltpu.VMEM((tm, d), F32)],
        compiler_params=_params("arbitrary", "arbitrary"),
        ride=ride, first=lambda: (pl.program_id(0) == 0) & (pl.program_id(1) == 0),
        last=lambda: (pl.program_id(0) == nrow - 1) & (pl.program_id(1) == total - 1))
    return (dx, dgain) if ride is None else (dx, dgain, got)


def _matmul_tn(a, b, groups, name, ta=1024, tb=1024, tk=1024):
    n, ka = a.shape
    _, kb = b.shape
    wc = kb // groups
    ta, tb, tk = min(ta, ka), min(tb, wc), min(tk, n)
    per = wc // tb
    nk = n // tk
    assert ka % ta == 0 and wc % tb == 0 and n % tk == 0

    def body(a_ref, b_ref, out_ref, acc_ref):
        k = pl.program_id(2)

        @pl.when(k == 0)
        def _():
            acc_ref[...] = jnp.zeros_like(acc_ref)

        acc_ref[...] += _dot(a_ref[...], b_ref[...], ta=True)

        @pl.when(k == nk - 1)
        def _():
            out_ref[...] = acc_ref[...].astype(BF16)

    return pl.pallas_call(
        body, name=name, grid=(ka // ta, groups * per, nk),
        in_specs=[pl.BlockSpec((tk, ta), lambda i, j, k: (k, i)),
                  pl.BlockSpec((tk, tb), lambda i, j, k: (k, j))],
        out_specs=pl.BlockSpec((None, ta, tb), lambda i, j, k: (j // per, i, j % per)),
        out_shape=jax.ShapeDtypeStruct((groups, ka, wc), BF16),
        scratch_shapes=[pltpu.VMEM((ta, tb), F32)],
        compiler_params=_params("parallel", "parallel", "arbitrary"),
    )(a, b)


def _gated_head_norm(o, gate, out_gain):
    parts = []
    for h in range(N_HEADS):
        sl = slice(h * HEAD_DIM, (h + 1) * HEAD_DIM)
        parts.append(_rms(o[:, sl], out_gain) * _silu(gate[:, sl]))
    return parts


def _out_proj(a_inputs, gated, x_in, w, gain, out_gain, name, tm=1024):
    n, d = x_in.shape
    k = w.shape[0]
    tm = min(tm, n)

    def body(*refs):
        if gated:
            o_ref, gate_ref, og_ref, x_ref, w_ref, gain_ref, xo_ref, y_ref, a_ref = refs
            parts = _gated_head_norm(o_ref[...], gate_ref[...], og_ref[...])
            for h, part in enumerate(parts):
                a_ref[:, h * HEAD_DIM:(h + 1) * HEAD_DIM] = part.astype(BF16)
            a = a_ref[...]
        else:
            a_in_ref, x_ref, w_ref, gain_ref, xo_ref, y_ref = refs
            a = a_in_ref[...]
        y = jnp.dot(a, w_ref[...], preferred_element_type=F32)
        y_ref[...] = y
        xo_ref[...] = x_ref[...] + _rms(y, gain_ref[...])

    row = lambda i: (i, 0)
    const = lambda i: (0, 0)
    if gated:
        a_specs = [pl.BlockSpec((tm, k), row), pl.BlockSpec((tm, D_MODEL), lambda i: (i, 3)),
                   pl.BlockSpec((1, HEAD_DIM), const)]
        a_args = list(a_inputs) + [out_gain]
    else:
        a_specs = [pl.BlockSpec((tm, k), row)]
        a_args = list(a_inputs)
    out_specs = [pl.BlockSpec((tm, d), row), pl.BlockSpec((tm, d), row)]
    out_shape = [jax.ShapeDtypeStruct((n, d), F32), jax.ShapeDtypeStruct((n, d), F32)]
    if gated:
        out_specs.append(pl.BlockSpec((tm, k), row))
        out_shape.append(jax.ShapeDtypeStruct((n, k), BF16))
    return pl.pallas_call(
        body, name=name, grid=(n // tm,),
        in_specs=a_specs + [pl.BlockSpec((tm, d), row), pl.BlockSpec((k, d), const), pl.BlockSpec((1, d), const)],
        out_specs=out_specs, out_shape=out_shape,
        compiler_params=_params("parallel"),
    )(*a_args, x_in, w, gain)


def _out_proj_bwd(dxo, y, w, gain, gated_inputs, name, tm=1024):
    n, d = dxo.shape
    k = w.shape[0]
    tm = min(tm, n)
    gated = gated_inputs is not None

    def body(*refs):
        if gated:
            (dxo_ref, y_ref, w_ref, gain_ref, o_ref, gate_ref, og_ref,
             dy_ref, dgain_ref, do_ref, dgate_ref, dog_ref) = refs
        else:
            dxo_ref, y_ref, w_ref, gain_ref, dy_ref, dgain_ref, da_ref = refs
        first = pl.program_id(0) == 0
        dy, dgain = _rms_bwd(y_ref[...], gain_ref[...], dxo_ref[...])
        dy_ref[...] = dy.astype(BF16)
        _acc_rows(dgain_ref, first, [dgain])
        da = _dot(dy_ref[...], w_ref[...], tb=True)
        if not gated:
            da_ref[...] = da.astype(BF16)
            return
        og = og_ref[...]
        dog = jnp.zeros_like(og)
        for h in range(N_HEADS):
            sl = slice(h * HEAD_DIM, (h + 1) * HEAD_DIM)
            fn = lambda o_h, g_h, gn: _rms(o_h, gn) * _silu(g_h)
            _, vjp = jax.vjp(fn, o_ref[:, sl], gate_ref[:, sl], og)
            do_h, dgate_h, dog_h = vjp(da[:, sl])
            do_ref[:, sl] = do_h
            dgate_ref[:, sl] = dgate_h.astype(BF16)
            dog = dog + dog_h
        _acc_rows(dog_ref, first, [dog])

    row = lambda i: (i, 0)
    const = lambda i: (0, 0)
    in_specs = [pl.BlockSpec((tm, d), row), pl.BlockSpec((tm, d), row), pl.BlockSpec((k, d), const),
                pl.BlockSpec((1, d), const)]
    args = [dxo, y, w, gain]
    out_specs = [pl.BlockSpec((tm, d), row), pl.BlockSpec((8, d), const)]
    out_shape = [jax.ShapeDtypeStruct((n, d), BF16), jax.ShapeDtypeStruct((8, d), F32)]
    if gated:
        in_specs += [pl.BlockSpec((tm, k), row), pl.BlockSpec((tm, D_MODEL), lambda i: (i, 3)),
                     pl.BlockSpec((1, HEAD_DIM), const)]
        args += list(gated_inputs)
        out_specs += [pl.BlockSpec((tm, k), row), pl.BlockSpec((tm, k), row), pl.BlockSpec((8, HEAD_DIM), const)]
        out_shape += [jax.ShapeDtypeStruct((n, k), F32), jax.ShapeDtypeStruct((n, k), BF16),
                      jax.ShapeDtypeStruct((8, HEAD_DIM), F32)]
    else:
        out_specs.append(pl.BlockSpec((tm, k), row))
        out_shape.append(jax.ShapeDtypeStruct((n, k), BF16))
    return pl.pallas_call(
        body, name=name, grid=(n // tm,), in_specs=in_specs, out_specs=out_specs, out_shape=out_shape,
        compiler_params=_params("arbitrary"),
    )(*args)


def _mlp_fwd(x_in, g_pre, w_up, w_down, g_post, name, tm=1024, target=None, ride=None):
    n, d = x_in.shape
    g, _, wc = w_up.shape
    tm = min(tm, n)
    with_loss = target is not None

    def body(*refs):
        if with_loss:
            x_ref, gpre_ref, wup_ref, wdown_ref, gpost_ref, t_ref, xo_ref, y_ref, h_ref, u_ref, a_ref, loss_ref, acc_ref = refs
        else:
            x_ref, gpre_ref, wup_ref, wdown_ref, gpost_ref, xo_ref, y_ref, h_ref, u_ref, a_ref, acc_ref = refs
        i, j = pl.program_id(0), pl.program_id(1)

        @pl.when(j == 0)
        def _():
            h_ref[...] = _rms(x_ref[...], gpre_ref[...]).astype(BF16)
            acc_ref[...] = jnp.zeros_like(acc_ref)

        u = jnp.dot(h_ref[...], wup_ref[...], preferred_element_type=F32).astype(BF16)
        u_ref[...] = u
        a = jnp.square(jnp.maximum(u, 0))
        a_ref[...] = a
        acc_ref[...] += jnp.dot(a, wdown_ref[...], preferred_element_type=F32)

        @pl.when(j == g - 1)
        def _():
            y = acc_ref[...]
            y_ref[...] = y
            xo = x_ref[...] + _rms(y, gpost_ref[...])
            if with_loss:
                err = xo - t_ref[...]
                xo_ref[...] = err * (1.0 / d)
                part = 0.5 * jnp.sum(jnp.mean(err * err, axis=-1, keepdims=True), axis=0, keepdims=True)
                _acc_rows(loss_ref, i == 0, [jnp.broadcast_to(part, (1, 128))])
            else:
                xo_ref[...] = xo

    row = lambda i, j: (i, 0)
    const = lambda i, j: (0, 0)
    in_specs = [pl.BlockSpec((tm, d), row), pl.BlockSpec((1, d), const),
                pl.BlockSpec((None, d, wc), lambda i, j: (j, 0, 0)),
                pl.BlockSpec((wc, d), lambda i, j: (j, 0)), pl.BlockSpec((1, d), const)]
    out_specs = [pl.BlockSpec((tm, d), row), pl.BlockSpec((tm, d), row), pl.BlockSpec((tm, d), row),
                 pl.BlockSpec((tm, wc), lambda i, j: (i, j)), pl.BlockSpec((tm, wc), lambda i, j: (i, j))]
    out_shape = [jax.ShapeDtypeStruct((n, d), F32), jax.ShapeDtypeStruct((n, d), F32),
                 jax.ShapeDtypeStruct((n, d), BF16), jax.ShapeDtypeStruct((n, g * wc), BF16),
                 jax.ShapeDtypeStruct((n, g * wc), BF16)]
    args = [x_in, g_pre, w_up, w_down, g_post]
    if with_loss:
        in_specs.append(pl.BlockSpec((tm, d), row))
        out_specs.append(pl.BlockSpec((8, 128), const))
        out_shape.append(jax.ShapeDtypeStruct((8, 128), F32))
        args.append(target)
    nrow = n // tm
    outs, got = _call(
        body, args, name=name, grid=(nrow, g), in_specs=in_specs, out_specs=out_specs, out_shape=out_shape,
        scratch_shapes=[pltpu.VMEM((tm, d), F32)], compiler_params=_params("arbitrary", "arbitrary"),
        ride=ride, first=lambda: (pl.program_id(0) == 0) & (pl.program_id(1) == 0),
        last=lambda: (pl.program_id(0) == nrow - 1) & (pl.program_id(1) == g - 1))
    return outs if ride is None else (outs, got)


def _mlp_bwd(dxo, y, g_post, w_down, u, w_up, x_in, g_pre, name, tm=1024, ride=None):
    n, d = dxo.shape
    g, _, wc = w_up.shape
    tm = min(tm, n)

    def body(dxo_ref, y_ref, gpost_ref, wdown_ref, u_ref, wup_ref, x_ref, gpre_ref,
             dx_ref, dy_ref, du_ref, dgain_ref, acc_ref, dgpost_ref):
        i, j = pl.program_id(0), pl.program_id(1)

        @pl.when(j == 0)
        def _():
            dy, dgpost = _rms_bwd(y_ref[...], gpost_ref[...], dxo_ref[...])
            dy_ref[...] = dy.astype(BF16)
            dgpost_ref[...] = dgpost
            acc_ref[...] = jnp.zeros_like(acc_ref)

        da = _dot(dy_ref[...], wdown_ref[...], tb=True)
        du = (da * (2.0 * jnp.maximum(u_ref[...], 0).astype(F32))).astype(BF16)
        du_ref[...] = du
        acc_ref[...] += _dot(du, wup_ref[...], tb=True)

        @pl.when(j == g - 1)
        def _():
            dx, dgpre = _rms_bwd(x_ref[...], gpre_ref[...], acc_ref[...])
            dx_ref[...] = dxo_ref[...] + dx
            _acc_rows(dgain_ref, i == 0, [dgpre, dgpost_ref[...]])

    row = lambda i, j: (i, 0)
    const = lambda i, j: (0, 0)
    nrow = n // tm
    outs, got = _call(
        body, [dxo, y, g_post, w_down, u, w_up, x_in, g_pre], name=name, grid=(nrow, g),
        in_specs=[pl.BlockSpec((tm, d), row), pl.BlockSpec((tm, d), row), pl.BlockSpec((1, d), const),
                  pl.BlockSpec((wc, d), lambda i, j: (j, 0)), pl.BlockSpec((tm, wc), lambda i, j: (i, j)),
                  pl.BlockSpec((None, d, wc), lambda i, j: (j, 0, 0)), pl.BlockSpec((tm, d), row),
                  pl.BlockSpec((1, d), const)],
        out_specs=[pl.BlockSpec((tm, d), row), pl.BlockSpec((tm, d), row),
                   pl.BlockSpec((tm, wc), lambda i, j: (i, j)), pl.BlockSpec((8, d), const)],
        out_shape=[jax.ShapeDtypeStruct((n, d), F32), jax.ShapeDtypeStruct((n, d), BF16),
                   jax.ShapeDtypeStruct((n, g * wc), BF16), jax.ShapeDtypeStruct((8, d), F32)],
        scratch_shapes=[pltpu.VMEM((tm, d), F32), pltpu.VMEM((1, d), F32)],
        compiler_params=_params("arbitrary", "arbitrary"),
        ride=ride, first=lambda: (pl.program_id(0) == 0) & (pl.program_id(1) == 0),
        last=lambda: (pl.program_id(0) == nrow - 1) & (pl.program_id(1) == g - 1))
    return outs if ride is None else (outs, got)


QKV = 3 * D_MODEL


def _shifted(x, prev8, s):
    if s == 0:
        return x
    tm = x.shape[0]
    rolled = pltpu.roll(x, s, 0)
    head = pltpu.roll(prev8, s, 0)
    head = jnp.concatenate([head, jnp.zeros((tm - 8, x.shape[1]), x.dtype)], axis=0)
    rows = lax.broadcasted_iota(jnp.int32, x.shape, 0)
    return jnp.where(rows < s, head, rolled)


def _conv(x, prev8, conv_w):
    out = x * conv_w[3:4, :]
    for s in range(1, CONV_WIDTH):
        out = out + _shifted(x, prev8, s) * conv_w[3 - s:4 - s, :]
    return out


def _l2norm(x):
    return x * lax.rsqrt(jnp.sum(x * x, axis=-1, keepdims=True) + EPS)


def _gdn_act_head(cq, ck, cv):
    return _l2norm(_silu(cq)) * (HEAD_DIM ** -0.5), _l2norm(_silu(ck)), _silu(cv)


def _gdn_act_head_bwd(cq, ck, cv, dq, dk, dv):
    def silu_and_slope(c):
        s = 1.0 / (1.0 + jnp.exp(-c))
        return c * s, s * (1.0 + c * (1.0 - s))

    def through_norm(c, dy, scale):
        a, slope = silu_and_slope(c)
        n = lax.rsqrt(jnp.sum(a * a, axis=-1, keepdims=True) + EPS)
        t = dy * scale if scale != 1.0 else dy
        da = n * (t - a * (n * n * jnp.sum(t * a, axis=-1, keepdims=True)))
        return da * slope

    return through_norm(cq, dq, HEAD_DIM ** -0.5), through_norm(ck, dk, 1.0), dv * silu_and_slope(cv)[1]


def _gdn_gates(ba, alog, dtb):
    lane = lax.broadcasted_iota(jnp.int32, ba.shape, 1)
    beta = 1.0 / (1.0 + jnp.exp(-ba))
    t = ba + dtb
    softplus = jnp.maximum(t, 0.0) + jnp.log(1.0 + jnp.exp(-jnp.abs(t)))
    g = -jnp.exp(alog) * softplus
    return jnp.where(lane < N_HEADS, beta, jnp.where(lane < 2 * N_HEADS, g, 0.0))


def _qkv_cols(part, h):
    start = part * D_MODEL + h * HEAD_DIM
    return slice(start, start + HEAD_DIM)


def _prev_rows_spec(tm, cols):
    per = tm // 8
    return pl.BlockSpec((8, cols), lambda i: (jnp.maximum(i * per - 1, 0), 0))


def _gdn_pre(proj, conv_w, alog, dtb, name, tm=256, ride=None):
    n = proj.shape[0]

    def body(x_ref, prev_ref, ba_ref, cw_ref, alog_ref, dtb_ref, q_ref, k_ref, v_ref, bg_ref, c_ref):
        first = pl.program_id(0) == 0
        for h in range(N_HEADS):
            convs = []
            for part in range(3):
                sl = _qkv_cols(part, h)
                prev8 = jnp.where(first, 0.0, prev_ref[:, sl])
                convs.append(_conv(x_ref[:, sl], prev8, cw_ref[:, sl]))
                c_ref[:, sl] = convs[-1]
            out = slice(h * HEAD_DIM, (h + 1) * HEAD_DIM)
            q_ref[:, out], k_ref[:, out], v_ref[:, out] = _gdn_act_head(*convs)
        bg_ref[...] = _gdn_gates(ba_ref[...], alog_ref[...], dtb_ref[...])

    row = lambda i: (i, 0)
    const = lambda i: (0, 0)
    nt = n // tm
    return _call(
        body, [proj, proj, proj, conv_w, alog, dtb], name=name, grid=(nt,),
        in_specs=[pl.BlockSpec((tm, QKV), row), _prev_rows_spec(tm, QKV),
                  pl.BlockSpec((tm, 128), lambda i: (i, 4 * D_MODEL // 128)),
                  pl.BlockSpec((CONV_WIDTH, QKV), const), pl.BlockSpec((1, 128), const), pl.BlockSpec((1, 128), const)],
        out_specs=[pl.BlockSpec((tm, D_MODEL), row)] * 3 + [pl.BlockSpec((tm, 128), row), pl.BlockSpec((tm, QKV), row)],
        out_shape=[jax.ShapeDtypeStruct((n, D_MODEL), F32)] * 3 + [jax.ShapeDtypeStruct((n, 128), F32),
                                                                  jax.ShapeDtypeStruct((n, QKV), F32)],
        compiler_params=_params("arbitrary"),
        ride=ride, first=lambda: pl.program_id(0) == 0, last=lambda: pl.program_id(0) == nt - 1)


def _gdn_pre_bwd(proj, conv, conv_w, alog, dtb, dq, dk, dv, dbg, dgate, name, tm=256):
    n = proj.shape[0]
    nt = n // tm

    def body(x_ref, c_ref, ba_ref, cw_ref, alog_ref, dtb_ref, dq_ref, dk_ref, dv_ref, dbg_ref, dgate_ref,
             dproj_ref, dcw_ref, dgates_ref, carry_ref, head_ref):
        step = pl.program_id(0)

        @pl.when(step == 0)
        def _():
            carry_ref[...] = jnp.zeros_like(carry_ref)
            head_ref[...] = jnp.zeros_like(head_ref)
            dcw_ref[...] = jnp.zeros_like(dcw_ref)

        rows = lax.broadcasted_iota(jnp.int32, (tm, HEAD_DIM), 0)
        rows8 = lax.broadcasted_iota(jnp.int32, (8, HEAD_DIM), 0)
        for h in range(N_HEADS):
            cols = [_qkv_cols(part, h) for part in range(3)]
            out = slice(h * HEAD_DIM, (h + 1) * HEAD_DIM)
            dcs = _gdn_act_head_bwd(*[c_ref[:, sl] for sl in cols], dq_ref[:, out], dk_ref[:, out], dv_ref[:, out])
            for sl, dc in zip(cols, dcs):
                x = x_ref[:, sl]
                cw = cw_ref[:, sl]
                dx = dc * cw[3:4, :]
                wrapped = jnp.zeros((8, HEAD_DIM), F32)
                dcw_ref[3:4, sl] += jnp.sum(dc * x, axis=0, keepdims=True)
                for s in range(1, CONV_WIDTH):
                    up = pltpu.roll(dc, tm - s, 0)
                    r = up * cw[3 - s:4 - s, :]
                    dx = dx + jnp.where(rows < tm - s, r, 0.0)
                    wrapped = wrapped + jnp.where(rows[tm - 8:, :] >= tm - s, r[tm - 8:, :], 0.0)
                    inside = jnp.sum(jnp.where(rows < tm - s, x * up, 0.0), axis=0, keepdims=True)
                    next_dc = pltpu.roll(head_ref[:, sl], 8 - s, 0)
                    across = jnp.sum(jnp.where(rows8 >= 8 - s, x[tm - 8:, :] * next_dc, 0.0), axis=0, keepdims=True)
                    dcw_ref[3 - s:4 - s, sl] += inside + across
                from_next = jnp.concatenate([jnp.zeros((tm - 8, HEAD_DIM), F32), carry_ref[:, sl]], axis=0)
                dproj_ref[:, sl] = (dx + from_next).astype(BF16)
                carry_ref[:, sl] = wrapped
                head_ref[:, sl] = dc[:8, :]

        _, vjp = jax.vjp(_gdn_gates, ba_ref[...], alog_ref[...], dtb_ref[...])
        dba, dalog, ddtb = vjp(dbg_ref[...])
        dproj_ref[:, QKV:4 * D_MODEL] = dgate_ref[...]
        dproj_ref[:, 4 * D_MODEL:] = dba.astype(BF16)
        _acc_rows(dgates_ref, step == 0, [dalog, ddtb])

    rev = lambda i: nt - 1 - i
    row = lambda i: (rev(i), 0)
    const = lambda i: (0, 0)
    return pl.pallas_call(
        body, name=name, grid=(nt,),
        in_specs=[pl.BlockSpec((tm, QKV), row), pl.BlockSpec((tm, QKV), row),
                  pl.BlockSpec((tm, 128), lambda i: (rev(i), 4 * D_MODEL // 128)),
                  pl.BlockSpec((CONV_WIDTH, QKV), const), pl.BlockSpec((1, 128), const), pl.BlockSpec((1, 128), const),
                  pl.BlockSpec((tm, D_MODEL), row), pl.BlockSpec((tm, D_MODEL), row), pl.BlockSpec((tm, D_MODEL), row),
                  pl.BlockSpec((tm, 128), row), pl.BlockSpec((tm, D_MODEL), row)],
        out_specs=[pl.BlockSpec((tm, GDN_IN_PAD), row), pl.BlockSpec((8, QKV), const), pl.BlockSpec((8, 128), const)],
        out_shape=[jax.ShapeDtypeStruct((n, GDN_IN_PAD), BF16), jax.ShapeDtypeStruct((8, QKV), F32),
                   jax.ShapeDtypeStruct((8, 128), F32)],
        scratch_shapes=[pltpu.VMEM((8, QKV), F32), pltpu.VMEM((8, QKV), F32)],
        compiler_params=_params("arbitrary"),
    )(proj, conv, proj, conv_w, alog, dtb, dq, dk, dv, dbg, dgate)


@jax.custom_vjp
def _unit_lower_inverses(lowers):
    n = lowers[0].shape[0]
    eye = (lax.broadcasted_iota(jnp.int32, (n, n), 0) == lax.broadcasted_iota(jnp.int32, (n, n), 1)).astype(F32)
    xs = [-low for low in lowers]
    ps = [eye + x for x in xs]
    ys = [_dot_3x(x, x) for x in xs]
    for _ in range(int(math.log2(CHUNK)) - 2):
        both = [_dot_3x(y, jnp.concatenate([y, p], axis=1)) for y, p in zip(ys, ps)]
        ys = [b[:, :n] for b in both]
        ps = [p + b[:, n:] for p, b in zip(ps, both)]
    return tuple(p + _dot_3x(y, p) for y, p in zip(ys, ps))


def _unit_lower_inverses_fwd(lowers):
    ts = _unit_lower_inverses(lowers)
    return ts, ts


def _unit_lower_inverses_bwd(ts, dts):
    left = [_dot_3x(t, dt, ta=True) for t, dt in zip(ts, dts)]
    return (tuple(-_dot_3x(l, t, tb=True) for l, t in zip(left, ts)),)


_unit_lower_inverses.defvjp(_unit_lower_inverses_fwd, _unit_lower_inverses_bwd)


@jax.custom_vjp
def _known_inverses(lowers, inverses):
    return inverses


def _known_inverses_bwd(ts, dts):
    (d_lowers,) = _unit_lower_inverses_bwd(ts, dts)
    return d_lowers, tuple(jnp.zeros_like(t) for t in ts)


_known_inverses.defvjp(lambda lowers, inverses: (inverses, inverses), _known_inverses_bwd)


@functools.partial(jax.custom_vjp, nondiff_argnums=(1,))
def _halves(x, axis):
    half = x.shape[axis] // 2
    return (x[:half], x[half:]) if axis == 0 else (x[:, :half], x[:, half:])


_halves.defvjp(lambda x, axis: (_halves(x, axis), None), lambda axis, _, g: (jnp.concatenate(g, axis=axis),))

GDN_STEP_CHUNKS = 4


def _gdn_chunks(qs, ks, vs, bgs, states, inverses=None, keep_inverses=False):
    c = CHUNK
    heads = range(N_HEADS)
    items = [(j, h) for j in range(len(bgs)) for h in heads]
    row = lax.broadcasted_iota(jnp.int32, (c, c), 0)
    col = lax.broadcasted_iota(jnp.int32, (c, c), 1)
    incl, strict, eye = row >= col, row > col, row == col
    lane = lax.broadcasted_iota(jnp.int32, (c, 128), 1)
    rowc = lax.broadcasted_iota(jnp.int32, (c, 1), 0)
    gc_all = [_mm_f32(incl.astype(F32), bg, False, False) for bg in bgs]
    q, k, v = ([xs[j][h] for j, h in items] for xs in (qs, ks, vs))
    n = range(len(items))
    beta = [jnp.sum(jnp.where(lane == h, bgs[j], 0.0), axis=1, keepdims=True) for j, h in items]
    gc = [jnp.sum(jnp.where(lane == N_HEADS + h, gc_all[j], 0.0), axis=1, keepdims=True) for j, h in items]
    gc_row = [jnp.sum(jnp.where(eye, gc[i], 0.0), axis=0, keepdims=True) for i in n]
    gc_last = [jnp.sum(jnp.where(rowc == c - 1, gc[i], 0.0), axis=0, keepdims=True) for i in n]
    decay = [jnp.where(incl, jnp.exp(jnp.where(incl, gc[i] - gc_row[i], 0.0)), 0.0) for i in n]
    kb = [k[i] * beta[i] for i in n]
    kb_q = [_halves(_mm(jnp.concatenate([kb[i], q[i]], axis=0), k[i], False, True), 0) for i in n]
    lower = tuple(jnp.where(strict, kb_q[i][0] * decay[i], 0.0) for i in n)
    attn = [kb_q[i][1] * decay[i] for i in n]
    t_mat = _unit_lower_inverses(lower) if inverses is None else _known_inverses(lower, inverses)
    egc = [jnp.exp(gc[i]) for i in n]
    w_u = [_halves(_mm(t_mat[i], jnp.concatenate([kb[i] * egc[i], v[i] * beta[i]], axis=1), False, False), 1)
           for i in n]
    w, u = [x[0] for x in w_u], [x[1] for x in w_u]
    qg = [q[i] * egc[i] for i in n]
    kg = [k[i] * jnp.exp(gc_last[i] - gc[i]) for i in n]
    outs, cur = [], list(states)
    for j in range(len(bgs)):
        at = lambda h: j * N_HEADS + h
        w_qg = [_halves(_mm(jnp.concatenate([w[at(h)], qg[at(h)]], axis=0), cur[h], False, False), 0)
                for h in heads]
        v_new = [u[at(h)] - w_qg[h][0] for h in heads]
        outs.append(tuple(w_qg[h][1] + _mm(attn[at(h)], v_new[h], False, False) for h in heads))
        cur = [cur[h] * jnp.exp(gc_last[at(h)]) + _mm(kg[at(h)], v_new[h], True, False) for h in heads]
    return (tuple(outs), tuple(cur), t_mat) if keep_inverses else (tuple(outs), tuple(cur))


def _chunk_head_slices(ref, nch):
    return tuple(tuple(ref[j * CHUNK:(j + 1) * CHUNK, h * HEAD_DIM:(h + 1) * HEAD_DIM] for h in range(N_HEADS))
                 for j in range(nch))


def _store_chunk_heads(ref, values):
    for j, chunk in enumerate(values):
        for h, val in enumerate(chunk):
            ref[j * CHUNK:(j + 1) * CHUNK, h * HEAD_DIM:(h + 1) * HEAD_DIM] = val


def _gdn_scan(qn, kn, v, bg, name, ride=None):
    n = qn.shape[0]
    nch = GDN_STEP_CHUNKS
    rows = nch * CHUNK
    nc = n // rows

    def body(q_ref, k_ref, v_ref, bg_ref, o_ref, saved_ref, inv_ref, state_ref):
        @pl.when(pl.program_id(0) == 0)
        def _():
            state_ref[...] = jnp.zeros_like(state_ref)

        states = tuple(state_ref[h] for h in range(N_HEADS))
        for h in range(N_HEADS):
            saved_ref[h] = states[h]
        bgs = tuple(bg_ref[j * CHUNK:(j + 1) * CHUNK, :] for j in range(nch))
        outs, new_states, inverses = _gdn_chunks(_chunk_head_slices(q_ref, nch), _chunk_head_slices(k_ref, nch),
                                                 _chunk_head_slices(v_ref, nch), bgs, states, keep_inverses=True)
        _store_chunk_heads(o_ref, outs)
        for h in range(N_HEADS):
            state_ref[h] = new_states[h]
        for item, inverse in enumerate(inverses):
            inv_ref[item] = inverse

    row = lambda i: (i, 0)
    return _call(
        body, [qn, kn, v, bg], name=name, grid=(nc,),
        in_specs=[pl.BlockSpec((rows, D_MODEL), row)] * 3 + [pl.BlockSpec((rows, 128), row)],
        out_specs=[pl.BlockSpec((rows, D_MODEL), row),
                   pl.BlockSpec((None, N_HEADS, HEAD_DIM, HEAD_DIM), lambda i: (i, 0, 0, 0)),
                   pl.BlockSpec((None, nch * N_HEADS, CHUNK, CHUNK), lambda i: (i, 0, 0, 0))],
        out_shape=[jax.ShapeDtypeStruct((n, D_MODEL), F32),
                   jax.ShapeDtypeStruct((nc, N_HEADS, HEAD_DIM, HEAD_DIM), F32),
                   jax.ShapeDtypeStruct((nc, nch * N_HEADS, CHUNK, CHUNK), F32)],
        scratch_shapes=[pltpu.VMEM((N_HEADS, HEAD_DIM, HEAD_DIM), F32)],
        compiler_params=_params("arbitrary"),
        ride=ride, first=lambda: pl.program_id(0) == 0, last=lambda: pl.program_id(0) == nc - 1)


def _gdn_scan_bwd(qn, kn, v, bg, saved, inverses, do, name, ride=None):
    n = qn.shape[0]
    nch = GDN_STEP_CHUNKS
    rows = nch * CHUNK
    nc = n // rows

    def body(q_ref, k_ref, v_ref, bg_ref, saved_ref, inv_ref, do_ref, dq_ref, dk_ref, dv_ref, dbg_ref, dstate_ref):
        @pl.when(pl.program_id(0) == 0)
        def _():
            dstate_ref[...] = jnp.zeros_like(dstate_ref)

        states = tuple(saved_ref[h] for h in range(N_HEADS))
        bgs = tuple(bg_ref[j * CHUNK:(j + 1) * CHUNK, :] for j in range(nch))
        known = tuple(inv_ref[item] for item in range(nch * N_HEADS))
        _, vjp = jax.vjp(functools.partial(_gdn_chunks, inverses=known), _chunk_head_slices(q_ref, nch),
                         _chunk_head_slices(k_ref, nch), _chunk_head_slices(v_ref, nch), bgs, states)
        dstates = tuple(dstate_ref[h] for h in range(N_HEADS))
        dqs, dks, dvs, dbgs, dprev = vjp((_chunk_head_slices(do_ref, nch), dstates))
        _store_chunk_heads(dq_ref, dqs)
        _store_chunk_heads(dk_ref, dks)
        _store_chunk_heads(dv_ref, dvs)
        for h in range(N_HEADS):
            dstate_ref[h] = dprev[h]
        for j in range(nch):
            dbg_ref[j * CHUNK:(j + 1) * CHUNK, :] = dbgs[j]

    row = lambda i: (nc - 1 - i, 0)
    return _call(
        body, [qn, kn, v, bg, saved, inverses, do], name=name, grid=(nc,),
        in_specs=[pl.BlockSpec((rows, D_MODEL), row)] * 3 + [pl.BlockSpec((rows, 128), row),
                  pl.BlockSpec((None, N_HEADS, HEAD_DIM, HEAD_DIM), lambda i: (nc - 1 - i, 0, 0, 0)),
                  pl.BlockSpec((None, nch * N_HEADS, CHUNK, CHUNK), lambda i: (nc - 1 - i, 0, 0, 0)),
                  pl.BlockSpec((rows, D_MODEL), row)],
        out_specs=[pl.BlockSpec((rows, D_MODEL), row)] * 3 + [pl.BlockSpec((rows, 128), row)],
        out_shape=[jax.ShapeDtypeStruct((n, D_MODEL), F32)] * 3 + [jax.ShapeDtypeStruct((n, 128), F32)],
        scratch_shapes=[pltpu.VMEM((N_HEADS, HEAD_DIM, HEAD_DIM), F32)],
        compiler_params=_params("arbitrary"),
        ride=ride, first=lambda: pl.program_id(0) == 0, last=lambda: pl.program_id(0) == nc - 1)


SB_BQ = 512
SB_BK = 256
SB_SUB = 128
SB_ROWS = 128
SB_SCALE = HEAD_DIM ** -0.5
SB_DEAD = -105.0


def _sb_terms(z, before):
    e = jnp.exp(-jnp.abs(z))
    log_beta = jnp.minimum(z, 0.0) - jnp.log(1.0 + e)
    log_1m = log_beta - z
    if before is not None:
        log_1m = jnp.where(before, log_1m, 0.0)
    return e, log_beta, log_1m


def _tri_ones(n, cmp):
    r = lax.broadcasted_iota(jnp.int32, (n, 2 * n), 0)
    c = lax.broadcasted_iota(jnp.int32, (n, 2 * n), 1)
    return jnp.where((c >= n) | cmp(r, c), 1.0, 0.0).astype(BF16)


def _sums(x, tri_ones):
    both = jnp.dot(x.astype(BF16), tri_ones, preferred_element_type=F32)
    n = x.shape[1]
    return both[:, :n], both[:, n:]


def _sb_chunk_mask(diagonal, r, s):
    if not diagonal or s * SB_SUB + SB_SUB - 1 < r * SB_ROWS:
        return None
    if s * SB_SUB >= r * SB_ROWS + SB_ROWS - 1:
        return "empty"
    rows = r * SB_ROWS + lax.broadcasted_iota(jnp.int32, (SB_ROWS, SB_SUB), 0)
    cols = s * SB_SUB + lax.broadcasted_iota(jnp.int32, (SB_ROWS, SB_SUB), 1)
    return cols < rows


def _sb_attention(q, kv, name, bq=SB_BQ, ride=None):
    n = q.shape[0]
    bq = min(bq, n)
    bk = min(SB_BK, bq)
    nrc = bq // SB_ROWS

    def body(q_ref, k_ref, v_ref, o_ref, l_ref, n_ref, z_scr, a_scr):
        i = pl.program_id(1)
        qb = q_ref[...]
        after = _tri_ones(SB_SUB, lambda r, c: r > c)

        def span(start, kw, c_sum, acc, diagonal):
            start = pl.multiple_of(start, kw)
            k_w = k_ref[pl.ds(start, kw), :]
            v_w = v_ref[pl.ds(start, kw), :]
            z_scr[:, :kw] = lax.dot_general(qb, k_w, _dims(False, True), preferred_element_type=F32)
            c_rows = [c_sum[r * SB_ROWS:(r + 1) * SB_ROWS] for r in range(nrc)]
            for s in reversed(range(kw // SB_SUB)):
                cols = slice(s * SB_SUB, (s + 1) * SB_SUB)
                for r in range(nrc):
                    rows = slice(r * SB_ROWS, (r + 1) * SB_ROWS)
                    before = _sb_chunk_mask(diagonal, r, s)
                    if isinstance(before, str):
                        a_scr[rows, cols] = jnp.zeros((SB_ROWS, SB_SUB), BF16)
                        continue
                    _, log_beta, log_1m = _sb_terms(z_scr[rows, cols] * SB_SCALE, before)
                    tail, total = _sums(log_1m, after)
                    a = jnp.exp(log_beta + c_rows[r] + tail)
                    if before is not None:
                        a = jnp.where(before, a, 0.0)
                    a_scr[rows, cols] = a.astype(BF16)
                    c_rows[r] = c_rows[r] + total
            acc = acc + jnp.dot(a_scr[:, :kw], v_w, preferred_element_type=F32)
            return jnp.concatenate(c_rows, axis=0), acc

        c_sum, acc = span(i * bq, bq, jnp.zeros((bq, SB_SUB), F32), jnp.zeros((bq, HEAD_DIM), F32), True)

        def more(state):
            done, c, _ = state
            return (done < i * (bq // bk)) & (jnp.max(c) > SB_DEAD)

        def step(state):
            done, c, a = state
            c, a = span(i * bq - (done + 1) * bk, bk, c, a, False)
            return done + 1, c, a

        done, c_sum, acc = lax.while_loop(more, step, (jnp.int32(0), c_sum, acc))
        o_ref[...] = acc.astype(BF16)
        l_ref[...] = c_sum
        n_ref[...] = jnp.full((8, 128), done.astype(F32), F32)

    nq = n // bq
    return _call(
        body, [q, kv, kv], name=name, grid=(N_HEADS, nq),
        in_specs=[pl.BlockSpec((bq, HEAD_DIM), lambda h, i: (i, h)),
                  pl.BlockSpec((n, HEAD_DIM), lambda h, i: (0, h)),
                  pl.BlockSpec((n, HEAD_DIM), lambda h, i: (0, N_HEADS + h))],
        out_specs=[pl.BlockSpec((bq, HEAD_DIM), lambda h, i: (i, h)),
                   pl.BlockSpec((None, bq, 128), lambda h, i: (h, i, 0)),
                   pl.BlockSpec((None, None, 8, 128), lambda h, i: (h, i, 0, 0))],
        out_shape=[jax.ShapeDtypeStruct((n, D_MODEL), BF16), jax.ShapeDtypeStruct((N_HEADS, n, 128), F32),
                   jax.ShapeDtypeStruct((N_HEADS, nq, 8, 128), F32)],
        scratch_shapes=[pltpu.VMEM((bq, bq), F32), pltpu.VMEM((bq, bq), BF16)],
        compiler_params=_params("arbitrary", "arbitrary"),
        ride=ride, first=lambda: (pl.program_id(0) == 0) & (pl.program_id(1) == 0),
        last=lambda: (pl.program_id(0) == N_HEADS - 1) & (pl.program_id(1) == nq - 1))


def _sb_attention_bwd(q, kv, do, lsum, spans, name, bq=SB_BQ, ride=None):
    n = q.shape[0]
    bq = min(bq, n)
    bk = min(SB_BK, bq)
    nrc = bq // SB_ROWS

    def body(q_ref, k_ref, v_ref, do_ref, l_ref, n_ref, dq_ref, dk_out, dv_out, z_scr, da_scr, a_scr, dz_scr, dk_ref, dv_ref):
        i = pl.program_id(1)

        @pl.when(i == 0)
        def _():
            dk_ref[...] = jnp.zeros_like(dk_ref)
            dv_ref[...] = jnp.zeros_like(dv_ref)

        qb = q_ref[...]
        dob = do_ref[...]
        lt_rows = [l_ref[r * SB_ROWS:(r + 1) * SB_ROWS, :] for r in range(nrc)]
        upto = _tri_ones(SB_SUB, lambda r, c: r <= c)
        below = _tri_ones(SB_SUB, lambda r, c: r < c)

        def span(start, kw, l_sum, g_sum, dq, diagonal):
            start = pl.multiple_of(start, kw)
            k_w = k_ref[pl.ds(start, kw), :]
            v_w = v_ref[pl.ds(start, kw), :]
            z_scr[:, :kw] = lax.dot_general(qb, k_w, _dims(False, True), preferred_element_type=F32)
            da_scr[:, :kw] = lax.dot_general(dob, v_w, _dims(False, True), preferred_element_type=F32)
            l_rows = [l_sum[r * SB_ROWS:(r + 1) * SB_ROWS] for r in range(nrc)]
            g_rows = [g_sum[r * SB_ROWS:(r + 1) * SB_ROWS] for r in range(nrc)]
            for s in range(kw // SB_SUB):
                cols = slice(s * SB_SUB, (s + 1) * SB_SUB)
                for r in range(nrc):
                    rows = slice(r * SB_ROWS, (r + 1) * SB_ROWS)
                    before = _sb_chunk_mask(diagonal, r, s)
                    if isinstance(before, str):
                        a_scr[rows, cols] = jnp.zeros((SB_ROWS, SB_SUB), BF16)
                        dz_scr[rows, cols] = jnp.zeros((SB_ROWS, SB_SUB), BF16)
                        continue
                    zs = z_scr[rows, cols] * SB_SCALE
                    e, log_beta, log_1m = _sb_terms(zs, before)
                    l_prefix, l_total = _sums(log_1m, upto)
                    a = jnp.exp(log_beta + (lt_rows[r] - (l_rows[r] + l_prefix)))
                    if before is not None:
                        a = jnp.where(before, a, 0.0)
                    g = a * da_scr[rows, cols]
                    g_prefix, g_total = _sums(g, below)
                    inv = 1.0 / (1.0 + e)
                    beta = jnp.where(zs >= 0, inv, e * inv)
                    dz = (g - (g + g_rows[r] + g_prefix) * beta) * SB_SCALE
                    if before is not None:
                        dz = jnp.where(before, dz, 0.0)
                    a_scr[rows, cols] = a.astype(BF16)
                    dz_scr[rows, cols] = dz.astype(BF16)
                    l_rows[r] = l_rows[r] + l_total
                    g_rows[r] = g_rows[r] + g_total
            dz_w = dz_scr[:, :kw]
            dq = dq + jnp.dot(dz_w, k_w, preferred_element_type=F32)
            dk_ref[pl.ds(start, kw), :] += lax.dot_general(dz_w, qb, _dims(True, False), preferred_element_type=F32)
            dv_ref[pl.ds(start, kw), :] += lax.dot_general(a_scr[:, :kw], dob, _dims(True, False),
                                                           preferred_element_type=F32)
            return jnp.concatenate(l_rows, axis=0), jnp.concatenate(g_rows, axis=0), dq

        taken = jnp.clip(jnp.max(n_ref[...]).astype(jnp.int32), 0, i * (bq // bk))
        zero = jnp.zeros((bq, SB_SUB), F32)
        carry = lax.fori_loop(0, taken, lambda j, c: span(i * bq - (taken - j) * bk, bk, c[0], c[1], c[2], False),
                              (zero, zero, jnp.zeros((bq, HEAD_DIM), F32)))
        _, _, dq = span(i * bq, bq, carry[0], carry[1], carry[2], True)
        dq_ref[...] = dq.astype(BF16)

        @pl.when(i == nq - 1)
        def _():
            dk_out[...] = dk_ref[...].astype(BF16)
            dv_out[...] = dv_ref[...].astype(BF16)

    nq = n // bq
    return _call(
        body, [q, kv, kv, do, lsum, spans], name=name, grid=(N_HEADS, nq),
        in_specs=[pl.BlockSpec((bq, HEAD_DIM), lambda h, i: (i, h)),
                  pl.BlockSpec((n, HEAD_DIM), lambda h, i: (0, h)),
                  pl.BlockSpec((n, HEAD_DIM), lambda h, i: (0, N_HEADS + h)),
                  pl.BlockSpec((bq, HEAD_DIM), lambda h, i: (i, h)),
                  pl.BlockSpec((None, bq, 128), lambda h, i: (h, i, 0)),
                  pl.BlockSpec((None, None, 8, 128), lambda h, i: (h, i, 0, 0))],
        out_specs=[pl.BlockSpec((bq, HEAD_DIM), lambda h, i: (i, h)),
                   pl.BlockSpec((n, HEAD_DIM), lambda h, i: (0, h)),
                   pl.BlockSpec((n, HEAD_DIM), lambda h, i: (0, h))],
        out_shape=[jax.ShapeDtypeStruct((n, D_MODEL), BF16)] * 3,
        scratch_shapes=[pltpu.VMEM((bq, bq), F32), pltpu.VMEM((bq, bq), F32), pltpu.VMEM((bq, bq), BF16),
                        pltpu.VMEM((bq, bq), BF16), pltpu.VMEM((n, HEAD_DIM), F32), pltpu.VMEM((n, HEAD_DIM), F32)],
        compiler_params=_params("arbitrary", "arbitrary"),
        ride=ride, first=lambda: (pl.program_id(0) == 0) & (pl.program_id(1) == 0),
        last=lambda: (pl.program_id(0) == N_HEADS - 1) & (pl.program_id(1) == nq - 1))


def _local_step(x, target, gains, comm):
    g = gains
    w = comm.w
    big = {}
    row = lambda a, i: a[i:i + 1, :]

    proj, h0 = _norm_matmul(x, row(g["mix_pre"], 0), w["gdn_in"], F32, "gdn_in_proj", tn=1408)
    (qn, kn, v, bg, conv), got = _gdn_pre(proj, w["conv"], g["alog"], g["dtb"], "gdn_pre", ride=comm.ride("pre"))
    comm.done("pre", got)
    (o_gdn, saved, inverses), got = _gdn_scan(qn, kn, v, bg, "gdn_scan", ride=comm.ride("scan"))
    comm.done("scan", got)
    x1, y_mix0, a_gdn = _out_proj((o_gdn, proj), True, x, w["gdn_out"], row(g["mix_post"], 0), g["out_gain"],
                                  "gdn_out_proj", tm=512)
    ride = comm.ride("mlp0")
    res = _mlp_fwd(x1, row(g["mlp_pre"], 0), w["up0"], w["down0"], row(g["mlp_post"], 0), "mlp0", ride=ride)
    (x2, y_mlp0, h_mlp0, u0, a0), got = res if ride is not None else (res, [])
    comm.done("mlp0", got)
    kv, h_kv = _norm_matmul(x2, g["kv"], w["kv"], BF16, "kv_proj", tm=2048)
    q, h_q = _norm_matmul(x2, row(g["mix_pre"], 1), w["sb_q"], BF16, "sb_q_proj")
    (o_sb, lsum, spans), got = _sb_attention(q, kv, "sb_attention", ride=comm.ride("sb"))
    comm.done("sb", got)
    x3, y_mix1 = _out_proj((o_sb,), False, x2, w["sb_o"], row(g["mix_post"], 1), None, "sb_out_proj")
    dx4, y_mlp1, h_mlp1, u1, a1, loss = _mlp_fwd(x3, row(g["mlp_pre"], 1), w["up1"], w["down1"], row(g["mlp_post"], 1),
                                                 "mlp1_loss", target=target)

    dx3, dy_mlp1, du1, dg_mlp1 = _mlp_bwd(dx4, y_mlp1, row(g["mlp_post"], 1), w["down1"], u1, w["up1"], x3,
                                          row(g["mlp_pre"], 1), "mlp1_bwd")
    big["down1"] = _matmul_tn(a1, dy_mlp1, 1, "d_down1")[0]
    big["up1"] = _matmul_tn(h_mlp1, du1, N_DEV, "d_up1", tk=2048)
    dy_mix1, dg_post1, do_sb = _out_proj_bwd(dx3, y_mix1, w["sb_o"], row(g["mix_post"], 1), None, "sb_out_proj_bwd")
    big["sb_o"] = _matmul_tn(o_sb, dy_mix1, 1, "d_sb_o")[0]
    (dq, dk, dv), got = _sb_attention_bwd(q, kv, do_sb, lsum, spans, "sb_attention_bwd", ride=comm.ride("sb_bwd", big))
    comm.done("sb_bwd", got)
    big["sb_q"] = _matmul_tn(h_q, dq, 1, "d_sb_q")[0]
    big["kv"] = jnp.concatenate([_matmul_tn(h_kv, dk, N_DEV // 2, "d_w_k", tk=2048),
                                 _matmul_tn(h_kv, dv, N_DEV // 2, "d_w_v", tk=2048)],
                                axis=0)
    dx2, dg_x2 = _norm_matmul_bwd([(dq, w["sb_q"], row(g["mix_pre"], 1), 1024), (dk, w["kv"][0:1], g["kv"], 1024),
                                   (dv, w["kv"][1:2], g["kv"], 1024)], x2, dx3, "qkv_proj_bwd", tm=512)

    big.update(x2=dg_x2, post1=dg_post1, mlp1=dg_mlp1, loss=loss)
    ride = comm.ride("mlp0_bwd", big)
    res = _mlp_bwd(dx2, y_mlp0, row(g["mlp_post"], 0), w["down0"], u0, w["up0"], x1, row(g["mlp_pre"], 0), "mlp0_bwd",
                   ride=ride)
    (dx1, dy_mlp0, du0, dg_mlp0), got = res if ride is not None else (res, [])
    comm.done("mlp0_bwd", got)
    big["down0"] = _matmul_tn(a0, dy_mlp0, 1, "d_down0")[0]
    big["up0"] = _matmul_tn(h_mlp0, du0, N_DEV, "d_up0", tk=2048)
    dy_mix0, dg_post0, do_gdn, dgate, d_out_gain = _out_proj_bwd(
        dx1, y_mix0, w["gdn_out"], row(g["mix_post"], 0), (o_gdn, proj, g["out_gain"]), "gdn_out_proj_bwd", tm=512)
    big["gdn_out"] = _matmul_tn(a_gdn, dy_mix0, 1, "d_gdn_out")[0]
    big.update(post0=dg_post0, mlp0=dg_mlp0, out_gain=d_out_gain)
    (dqn, dkn, dvv, dbg), got = _gdn_scan_bwd(qn, kn, v, bg, saved, inverses, do_gdn, "gdn_scan_bwd",
                                              ride=comm.ride("scan_bwd", big))
    comm.done("scan_bwd", got)
    dproj, d_conv, d_gates = _gdn_pre_bwd(proj, conv, w["conv"], g["alog"], g["dtb"], dqn, dkn, dvv, dbg, dgate,
                                          "gdn_pre_bwd")
    big["gdn_in"] = _matmul_tn(h0, dproj, 1, "d_gdn_in", tb=1408)[0]
    big.update(conv=d_conv, gates=d_gates)
    ride = comm.ride("in_bwd", big)
    grad_x, dg_pre0, *got = _norm_matmul_bwd([(dproj, w["gdn_in"], row(g["mix_pre"], 0), 1408)], x, dx1, "gdn_in_proj_bwd",
                                             ride=ride)
    comm.done("in_bwd", got[0] if got else [])

    small = {"pre0": dg_pre0, "x2": dg_x2, "post0": dg_post0, "post1": dg_post1, "mlp0": dg_mlp0, "mlp1": dg_mlp1,
             "gates": d_gates, "out_gain": d_out_gain, "conv": d_conv, "loss": loss}
    return grad_x, big, small


GATHER_STAGES = {"pre": ("up0",), "scan": ("gdn_out", "down0"), "mlp0": ("kv", "sb_q"), "sb": ("sb_o", "up1", "down1")}
SCATTER_STAGES = {"sb_bwd": ("up1", "down1", "sb_o"), "mlp0_bwd": ("sb_q", "kv"), "scan_bwd": ("up0", "down0", "gdn_out"),
                  "in_bwd": ("gdn_in",)}
SMALL_STAGES = {"mlp0_bwd": ("x2", "post1", "mlp1", "loss"), "scan_bwd": ("post0", "mlp0", "out_gain"),
                "in_bwd": ("conv", "gates")}


def _whole_weight(name, gathered):
    d = D_MODEL
    if name in ("up0", "up1"):
        return gathered
    if name == "kv":
        return gathered.reshape(2, N_DEV // 2, d, gathered.shape[2]).transpose(0, 2, 1, 3).reshape(2, d, d)
    if name == "gdn_in":
        whole = gathered.transpose(1, 0, 2).reshape(d, GDN_IN_COLS)
        return jnp.pad(whole, ((0, 0), (0, GDN_IN_PAD - GDN_IN_COLS)))[None]
    if name == "conv":
        return gathered.transpose(1, 0, 2).reshape(CONV_WIDTH, QKV)
    whole = gathered.reshape(gathered.shape[0] * gathered.shape[1], d)
    return whole[None] if name == "sb_q" else whole


def _owner_blocks(name, grad):
    if name in ("up0", "up1", "kv"):
        blocks = grad
    elif name == "gdn_in":
        blocks = grad[:, :GDN_IN_COLS].reshape(D_MODEL, N_DEV, GDN_IN_COLS // N_DEV).transpose(1, 0, 2)
    else:
        blocks = grad.reshape(N_DEV, grad.shape[0] // N_DEV, grad.shape[1])
    return blocks.astype(BF16)


class _Fsdp:
    def __init__(self, shards, weights):
        self.shards, self.w, self.landed, self.small_landed = shards, weights, {}, {}

    def ride(self, stage, grads=None):
        if stage in GATHER_STAGES:
            names = GATHER_STAGES[stage]
            return _Exchange([self.shards[nm] for nm in names], [True] * len(names))
        names, small = SCATTER_STAGES[stage], SMALL_STAGES.get(stage, ())
        return _Exchange([_owner_blocks(nm, grads[nm]) for nm in names] + [grads[nm] for nm in small],
                         [False] * len(names) + [True] * len(small))

    def done(self, stage, outs):
        if stage in GATHER_STAGES:
            for nm, out in zip(GATHER_STAGES[stage], outs):
                self.w[nm] = _whole_weight(nm, out)
        else:
            names = SCATTER_STAGES[stage]
            self.landed.update(zip(names, outs[:len(names)]))
            self.small_landed.update(zip(SMALL_STAGES.get(stage, ()), outs[len(names):]))


def _my_place():
    return lax.axis_index("x"), lax.axis_index("y"), lax.axis_index("c")


class _Exchange:
    def __init__(self, arrays, gather):
        self.arrays, self.gather, self.n = list(arrays), list(gather), len(arrays)
        any_spec = pl.BlockSpec(memory_space=pl.ANY)
        self.in_specs = [any_spec] * self.n
        self.out_specs = [any_spec] * self.n
        self.out_shape = [jax.ShapeDtypeStruct(((N_DEV,) + a.shape) if g else a.shape, a.dtype)
                          for a, g in zip(self.arrays, self.gather)]
        self.scratch = [pltpu.SemaphoreType.DMA((self.n, N_DEV - 1)), pltpu.SemaphoreType.DMA((self.n, N_DEV - 1)),
                        pltpu.SemaphoreType.DMA((self.n,))]

    def _copies(self, ins, outs, sems):
        send_sems, recv_sems, local_sems = sems
        x, y, c = _my_place()
        me = 4 * x + 2 * y + c
        copies = []
        for a in range(self.n):
            src = ins[a] if self.gather[a] else ins[a].at[me]
            copies.append(pltpu.make_async_copy(src, outs[a].at[me], local_sems.at[a]))
        for k in range(1, N_DEV):
            px = 1 - x if k & 4 else x
            py = 1 - y if k & 2 else y
            pc = 1 - c if k & 1 else c
            peer = 4 * px + 2 * py + pc
            for a in range(self.n):
                src = ins[a] if self.gather[a] else ins[a].at[peer]
                copies.append(pltpu.make_async_remote_copy(
                    src_ref=src, dst_ref=outs[a].at[me], send_sem=send_sems.at[a, k - 1], recv_sem=recv_sems.at[a, k - 1],
                    device_id=(px, py, pc), device_id_type=MESH))
        return copies

    def start(self, ins, outs, sems):
        for cp in self._copies(ins, outs, sems):
            cp.start()

    def wait(self, ins, outs, sems):
        for cp in self._copies(ins, outs, sems):
            cp.wait()


def _call(body, operands, *, name, out_shape, in_specs, out_specs, grid=(), scratch_shapes=(), compiler_params=None,
          ride=None, first=None, last=None):
    n_in, n_out, n_scr = len(operands), len(out_shape), len(scratch_shapes)
    if ride is None:
        outs = pl.pallas_call(body, name=name, grid=grid, in_specs=in_specs, out_specs=out_specs, out_shape=out_shape,
                              scratch_shapes=scratch_shapes, compiler_params=compiler_params)(*operands)
        return list(outs), []
    r = ride.n

    def riding(*refs):
        ins, r_ins = refs[:n_in], refs[n_in:n_in + r]
        outs, r_outs = refs[n_in + r:n_in + r + n_out], refs[n_in + r + n_out:n_in + 2 * r + n_out]
        scr, sems = refs[n_in + 2 * r + n_out:n_in + 2 * r + n_out + n_scr], refs[n_in + 2 * r + n_out + n_scr:]

        @pl.when(first())
        def _():
            ride.start(r_ins, r_outs, sems)

        body(*ins, *outs, *scr)

        @pl.when(last())
        def _():
            ride.wait(r_ins, r_outs, sems)

    outs = pl.pallas_call(
        riding, name=name, grid=grid, in_specs=list(in_specs) + ride.in_specs, out_specs=list(out_specs) + ride.out_specs,
        out_shape=list(out_shape) + ride.out_shape, scratch_shapes=list(scratch_shapes) + ride.scratch,
        compiler_params=compiler_params)(*operands, *ride.arrays)
    return list(outs[:n_out]), list(outs[n_out:])


def _exchange(arrays, gather, name):
    ride = _Exchange(arrays, gather)

    def body(*refs):
        ins, outs, sems = refs[:ride.n], refs[ride.n:2 * ride.n], refs[2 * ride.n:]
        ride.start(ins, outs, sems)
        ride.wait(ins, outs, sems)

    return pl.pallas_call(body, name=name, in_specs=ride.in_specs, out_specs=ride.out_specs, out_shape=ride.out_shape,
                          scratch_shapes=ride.scratch)(*arrays)


def _gather_two_level(arrays, name):
    n_arr = len(arrays)

    def body(*refs):
        ins, outs = refs[:n_arr], refs[n_arr:2 * n_arr]
        send_sems, recv_sems, local_sems = refs[2 * n_arr:]
        x, y, c = _my_place()
        sibling = (x, y, 1 - c)
        chips = [(1 - x, y), (x, 1 - y), (1 - x, 1 - y)]
        index = lambda px, py, pc: 4 * px + 2 * py + pc

        def copy(a, k, block, to, src=None):
            dst = outs[a].at[index(*block)]
            return pltpu.make_async_remote_copy(src_ref=dst if src is None else src, dst_ref=dst,
                                                send_sem=send_sems.at[a, k], recv_sem=recv_sems.at[a, k],
                                                device_id=to, device_id_type=MESH)

        mine = [pltpu.make_async_copy(ins[a], outs[a].at[index(x, y, c)], local_sems.at[a]) for a in range(n_arr)]
        first = [copy(a, 0, (x, y, c), sibling, src=ins[a]) for a in range(n_arr)]
        first += [copy(a, 1 + j, (x, y, c), (*chip, c), src=ins[a]) for j, chip in enumerate(chips) for a in range(n_arr)]
        for cp in mine + first:
            cp.start()
        passed = []
        for j, chip in enumerate(chips):
            for a in range(n_arr):
                copy(a, 1 + j, (*chip, c), (x, y, c)).wait_recv()
                passed.append(copy(a, 4 + j, (*chip, c), sibling))
                passed[-1].start()
        for a in range(n_arr):
            copy(a, 0, sibling, (x, y, c)).wait_recv()
            for j, chip in enumerate(chips):
                copy(a, 4 + j, (*chip, 1 - c), (x, y, c)).wait_recv()
        for cp in first + passed:
            cp.wait_send()
        for cp in mine:
            cp.wait()

    any_spec = pl.BlockSpec(memory_space=pl.ANY)
    return pl.pallas_call(
        body, name=name, in_specs=[any_spec] * n_arr, out_specs=[any_spec] * n_arr,
        out_shape=[jax.ShapeDtypeStruct((N_DEV,) + a.shape, a.dtype) for a in arrays],
        scratch_shapes=[pltpu.SemaphoreType.DMA((n_arr, N_DEV - 1)), pltpu.SemaphoreType.DMA((n_arr, N_DEV - 1)),
                        pltpu.SemaphoreType.DMA((n_arr,))])(*arrays)


def _adamw_math(g, w, m, v):
    m = ADAM_B1 * m + (1.0 - ADAM_B1) * g
    v = ADAM_B2 * v + (1.0 - ADAM_B2) * jnp.square(g)
    m_hat = m / (1.0 - ADAM_B1 ** ADAM_STEP)
    v_hat = v / (1.0 - ADAM_B2 ** ADAM_STEP)
    delta = -ADAM_LR * (m_hat / (jnp.sqrt(v_hat) + ADAM_EPS) + ADAM_WD * w)
    return delta, m, v


def _sum_devices(ref):
    total = ref[0].astype(F32)
    for d in range(1, N_DEV):
        total = total + ref[d].astype(F32)
    return total


def _reduce_adamw(landed, w, m, v, name, tr=256):
    r, c = w.shape
    tr = min(tr, r)
    assert r % tr == 0

    def body(l_ref, w_ref, m_ref, v_ref, g_ref, d_ref, nm_ref, nv_ref):
        g = _sum_devices(l_ref)
        g_ref[...] = g
        d_ref[...], nm_ref[...], nv_ref[...] = _adamw_math(g, w_ref[...], m_ref[...], v_ref[...])

    blk = pl.BlockSpec((tr, c), lambda i: (i, 0))
    return pl.pallas_call(
        body, name=name, grid=(r // tr,),
        in_specs=[pl.BlockSpec((N_DEV, tr, c), lambda i: (0, i, 0)), blk, blk, blk],
        out_specs=[blk] * 4, out_shape=[jax.ShapeDtypeStruct((r, c), F32)] * 4,
        compiler_params=_params("parallel"),
    )(landed, w, m, v)


def _reduce_adamw_layers(landed, w, m, v, name, tr=256):
    _, r, c = w.shape
    tr = min(tr, r)
    nr = r // tr
    assert r % tr == 0

    def body(l0_ref, l1_ref, w_ref, m_ref, v_ref, g_ref, d_ref, nm_ref, nv_ref):
        g = jnp.where(pl.program_id(0) == 0, _sum_devices(l0_ref), _sum_devices(l1_ref))
        g_ref[...] = g
        d_ref[...], nm_ref[...], nv_ref[...] = _adamw_math(g, w_ref[...], m_ref[...], v_ref[...])

    blk = pl.BlockSpec((None, tr, c), lambda l, i: (l, i, 0))
    return pl.pallas_call(
        body, name=name, grid=(2, nr),
        in_specs=[pl.BlockSpec((N_DEV, tr, c), lambda l, i: (0, jnp.where(l == 0, i, nr - 1), 0)),
                  pl.BlockSpec((N_DEV, tr, c), lambda l, i: (0, jnp.where(l == 1, i, 0), 0)), blk, blk, blk],
        out_specs=[blk] * 4, out_shape=[jax.ShapeDtypeStruct(w.shape, F32)] * 4,
        compiler_params=_params("arbitrary", "arbitrary"),
    )(landed[0], landed[1], w, m, v)


def _adamw(g, w, m, v, name):
    def body(g_ref, w_ref, m_ref, v_ref, d_ref, nm_ref, nv_ref):
        d_ref[...], nm_ref[...], nv_ref[...] = _adamw_math(g_ref[...], w_ref[...], m_ref[...], v_ref[...])

    return pl.pallas_call(body, name=name, out_shape=[jax.ShapeDtypeStruct(w.shape, F32)] * 3)(g, w, m, v)


def _small_update(landed, params, name):
    layout = {
        "mix_pre": [("pre0", 0), ("x2", 0)], "mix_post": [("post0", 0), ("post1", 0)],
        "mlp_pre": [("mlp0", 0), ("mlp1", 0)], "mlp_post": [("mlp0", 1), ("mlp1", 1)],
        "kv": [("kv", 0)], "alog": [("gates", 0)], "dtb": [("gates", 1)], "out_gain": [("out_gain", 0)],
    }
    landed_names = sorted(landed)
    param_names = sorted(params)
    n_l, n_p = len(landed_names), len(param_names)

    def body(*refs):
        l_refs = dict(zip(landed_names, refs[:n_l]))
        p_refs = {p: refs[n_l + 3 * i:n_l + 3 * i + 3] for i, p in enumerate(param_names)}
        outs = refs[n_l + 3 * n_p:]
        o_refs = {p: outs[4 * i:4 * i + 4] for i, p in enumerate(param_names)}
        conv_ref, loss_ref = outs[4 * n_p:]
        sums = {nm: _sum_devices(l_refs[nm]) for nm in landed_names}
        sums["kv"] = sums["x2"][1:2, :] + sums["x2"][2:3, :]
        for p in param_names:
            w_ref, m_ref, v_ref = p_refs[p]
            g_ref, d_ref, nm_ref, nv_ref = o_refs[p]
            for r, (src, src_row) in enumerate(layout[p]):
                g = sums[src][src_row:src_row + 1, :]
                g_ref[r:r + 1, :] = g
                d, nm, nv = _adamw_math(g, w_ref[r:r + 1, :], m_ref[r:r + 1, :], v_ref[r:r + 1, :])
                d_ref[r:r + 1, :] = d
                nm_ref[r:r + 1, :] = nm
                nv_ref[r:r + 1, :] = nv
        conv_ref[...] = sums["conv"]
        loss_ref[...] = sums["loss"]

    args = [landed[nm] for nm in landed_names]
    out_shape = []
    for p in param_names:
        args += list(params[p])
        out_shape += [jax.ShapeDtypeStruct(params[p][0].shape, F32)] * 4
    out_shape += [jax.ShapeDtypeStruct(landed["conv"].shape[1:], F32), jax.ShapeDtypeStruct(landed["loss"].shape[1:], F32)]
    outs = pl.pallas_call(body, name=name, out_shape=out_shape)(*args)
    result = {p: tuple(outs[4 * i:4 * i + 4]) for i, p in enumerate(param_names)}
    result["conv"], result["loss"] = outs[4 * n_p], outs[4 * n_p + 1]
    return result


def _lanes(vec, offset):
    return jnp.pad(vec[None, :], ((0, 0), (offset, 128 - offset - vec.shape[0])))


def kernel(x, mix_pre_gain, mix_post_gain, mlp_pre_gain, mlp_post_gain, mlp_w_up, mlp_w_down, gdn_w_in, gdn_conv_w, gdn_a_log, gdn_dt_bias, gdn_out_gain, gdn_w_out, kv_gain, w_kv, sb_w_q, sb_w_o, loss_target, m_mix_pre_gain, m_mix_post_gain, m_mlp_pre_gain, m_mlp_post_gain, m_mlp_w_up, m_mlp_w_down, m_gdn_w_in, m_gdn_conv_w, m_gdn_a_log, m_gdn_dt_bias, m_gdn_out_gain, m_gdn_w_out, m_kv_gain, m_w_kv, m_sb_w_q, m_sb_w_o, v_mix_pre_gain, v_mix_post_gain, v_mlp_pre_gain, v_mlp_post_gain, v_mlp_w_up, v_mlp_w_down, v_gdn_w_in, v_gdn_conv_w, v_gdn_a_log, v_gdn_dt_bias, v_gdn_out_gain, v_gdn_w_out, v_kv_gain, v_w_kv, v_sb_w_q, v_sb_w_o):
    me = 4 * lax.axis_index("x") + 2 * lax.axis_index("y") + lax.axis_index("c")
    bf = lambda a: a.astype(BF16)

    shards = {"up0": bf(mlp_w_up[0]), "up1": bf(mlp_w_up[1]), "down0": bf(mlp_w_down[0]), "down1": bf(mlp_w_down[1]),
              "gdn_out": bf(gdn_w_out[0]), "kv": bf(w_kv), "sb_q": bf(sb_w_q[0]), "sb_o": bf(sb_w_o[0])}
    gdn_in, conv = _gather_two_level([bf(gdn_w_in[0]), gdn_conv_w[0]], "gather_first_weights")
    comm = _Fsdp(shards, {"gdn_in": _whole_weight("gdn_in", gdn_in), "conv": _whole_weight("conv", conv)})
    gains = {"mix_pre": mix_pre_gain, "mix_post": mix_post_gain, "mlp_pre": mlp_pre_gain, "mlp_post": mlp_post_gain,
             "kv": kv_gain[None, :], "alog": _lanes(gdn_a_log[0], N_HEADS), "dtb": _lanes(gdn_dt_bias[0], N_HEADS),
             "out_gain": gdn_out_gain}

    grad_x, big, small = _local_step(x[0], loss_target[0], gains, comm)

    (pre0,) = _exchange([small["pre0"]], [True], "exchange_last_grad")
    small_landed = dict(comm.small_landed, pre0=pre0)

    results = {}
    results["mlp_w_up"] = _reduce_adamw_layers((comm.landed["up0"], comm.landed["up1"]), mlp_w_up, m_mlp_w_up, v_mlp_w_up,
                                               "adamw_mlp_w_up")
    results["mlp_w_down"] = _reduce_adamw_layers((comm.landed["down0"], comm.landed["down1"]), mlp_w_down, m_mlp_w_down,
                                                 v_mlp_w_down, "adamw_mlp_w_down")
    big_params = [("gdn_w_in", "gdn_in", gdn_w_in[0], m_gdn_w_in[0], v_gdn_w_in[0]),
                  ("gdn_w_out", "gdn_out", gdn_w_out[0], m_gdn_w_out[0], v_gdn_w_out[0]),
                  ("w_kv", "kv", w_kv, m_w_kv, v_w_kv), ("sb_w_q", "sb_q", sb_w_q[0], m_sb_w_q[0], v_sb_w_q[0]),
                  ("sb_w_o", "sb_o", sb_w_o[0], m_sb_w_o[0], v_sb_w_o[0])]
    for nm, short, w_, m_, v_ in big_params:
        results[nm] = _reduce_adamw(comm.landed[short], w_, m_, v_, "adamw_" + nm)
    lanes8 = lambda a: _lanes(a[0], N_HEADS)
    small_params = {
        "mix_pre": (mix_pre_gain, m_mix_pre_gain, v_mix_pre_gain), "mix_post": (mix_post_gain, m_mix_post_gain, v_mix_post_gain),
        "mlp_pre": (mlp_pre_gain, m_mlp_pre_gain, v_mlp_pre_gain), "mlp_post": (mlp_post_gain, m_mlp_post_gain, v_mlp_post_gain),
        "kv": (kv_gain[None, :], m_kv_gain[None, :], v_kv_gain[None, :]),
        "alog": (lanes8(gdn_a_log), lanes8(m_gdn_a_log), lanes8(v_gdn_a_log)),
        "dtb": (lanes8(gdn_dt_bias), lanes8(m_gdn_dt_bias), lanes8(v_gdn_dt_bias)),
        "out_gain": (gdn_out_gain, m_gdn_out_gain, v_gdn_out_gain),
    }
    sm = _small_update(small_landed, small_params, "small_update")
    conv_cols = QKV // N_DEV
    g_conv = lax.dynamic_slice(sm["conv"], (0, me * conv_cols), (8, conv_cols))[:CONV_WIDTH]
    conv_res = (g_conv,) + tuple(_adamw(g_conv, gdn_conv_w[0], m_gdn_conv_w[0], v_gdn_conv_w[0], "adamw_conv"))

    lead = lambda t: tuple(a[None] for a in t)
    heads8 = lambda t: tuple(a[:, N_HEADS:2 * N_HEADS] for a in t)
    per_weight = [
        sm["mix_pre"], sm["mix_post"], sm["mlp_pre"], sm["mlp_post"],
        tuple(results["mlp_w_up"]), tuple(results["mlp_w_down"]),
        lead(results["gdn_w_in"]), lead(conv_res), heads8(sm["alog"]), heads8(sm["dtb"]), sm["out_gain"],
        lead(results["gdn_w_out"]), tuple(a[0] for a in sm["kv"]), results["w_kv"], lead(results["sb_w_q"]), lead(results["sb_w_o"]),
    ]
    grads, deltas, new_ms, new_vs = zip(*per_weight)
    return (sm["loss"][0, 0], grad_x[None], *grads, *deltas, *new_ms, *new_vs)
```
